```python
import math
import jax, jax.numpy as jnp
from jax import lax
import numpy as np

D_MODEL = 1024
BATCH = 8
SEQ = 2048
DEPTH = 1

CHUNK = 64
D_MIX = D_MODEL
D_LRU = D_MIX // 2
D_CONV = D_MIX - D_LRU
LRU_HEADS = 8
LRU_HEAD_DIM = D_LRU // LRU_HEADS
CONV_GROUPS = 8
CONV_GROUP_DIM = D_CONV // CONV_GROUPS
LRU_CONV_WIDTH = 4
SHORT_CONV_WIDTH = 3
D_FF = 4 * D_MODEL
C_GATE = 8.0
MIN_RAD = 0.9
MAX_RAD = 0.999
EPS = 1e-6
N_ADA = 6
D_IN = 2 * D_LRU + 3 * D_CONV

kernel_name = "hybrid_rglru_shortconv_adaln_block"


def rmsnorm(x, g):
    xf = x.astype(jnp.float32)
    y = xf * lax.rsqrt(jnp.mean(xf * xf, axis=-1, keepdims=True) + EPS)
    return (y * g.astype(jnp.float32)).astype(x.dtype)


def headwise_rmsnorm(y, g, n_heads):
    b, s, w = y.shape
    yh = y.reshape(b, s, n_heads, w // n_heads).astype(jnp.float32)
    yh = yh * lax.rsqrt(jnp.mean(yh * yh, axis=-1, keepdims=True) + EPS)
    return (yh.reshape(b, s, w) * g.astype(jnp.float32)).astype(y.dtype)


def causal_depthwise_conv(x, w):
    k, ch = w.shape
    rhs = w[:, None, :].astype(x.dtype)
    return lax.conv_general_dilated(
        x, rhs, window_strides=(1,), padding=[(k - 1, 0)],
        dimension_numbers=("NWC", "WIO", "NWC"), feature_group_count=ch)


def chunked_linear_scan(a, b):
    bn, s, w = a.shape
    nc = s // CHUNK
    a = a.reshape(bn, nc, CHUNK, w)
    b = b.reshape(bn, nc, CHUNK, w)

    def combine(left, right):
        al, bl = left
        ar, br = right
        return al * ar, ar * bl + br

    a_cum, h_loc = lax.associative_scan(combine, (a, b), axis=2)

    def step(h, inp):
        a_last, h_last = inp
        return a_last * h + h_last, h

    _, h_in = lax.scan(step, jnp.zeros((bn, w), jnp.float32),
                       (jnp.swapaxes(a_cum[:, :, -1], 0, 1), jnp.swapaxes(h_loc[:, :, -1], 0, 1)))
    h_in = jnp.swapaxes(h_in, 0, 1)
    h = h_loc + a_cum * h_in[:, :, None, :]
    return h.reshape(bn, s, w)


def rg_lru(xl, gate_a_w, gate_a_b, gate_x_w, gate_x_b, a_param):
    bn, s, w = xl.shape
    xh = xl.reshape(bn, s, LRU_HEADS, LRU_HEAD_DIM)
    r = jax.nn.sigmoid(jnp.einsum("bshi,hij->bshj", xh, gate_a_w).reshape(bn, s, w) + gate_a_b)
    i = jax.nn.sigmoid(jnp.einsum("bshi,hij->bshj", xh, gate_x_w).reshape(bn, s, w) + gate_x_b)
    log_a = -C_GATE * r.astype(jnp.float32) * jax.nn.softplus(a_param.astype(jnp.float32))
    a = jnp.exp(log_a)
    mult = jnp.sqrt(-jnp.expm1(2.0 * log_a))
    is_first = (jnp.arange(s) == 0)[None, :, None]
    mult = jnp.where(is_first, jnp.ones_like(mult), mult)
    bx = mult * (i * xl).astype(jnp.float32)
    return chunked_linear_scan(a, bx).astype(xl.dtype)


def _fwd_setup_inputs(seed: int = 0) -> dict:
    key = jax.random.key(seed)
    ks = jax.random.split(key, 24)
    f32 = jnp.float32
    nrm = lambda k, shape, scale: (jax.random.normal(k, shape, f32) * scale)
    x = jax.random.normal(ks[0], (BATCH, SEQ, D_MODEL), f32)
    c = jax.random.normal(ks[1], (BATCH, D_MODEL), f32)
    ada_w = nrm(ks[2], (DEPTH, D_MODEL, N_ADA * D_MODEL), 0.5 * D_MODEL ** -0.5)
    ada_b = nrm(ks[3], (DEPTH, N_ADA * D_MODEL), 0.01)
    norm1_g = 1.0 + nrm(ks[4], (DEPTH, D_MODEL), 0.02)
    w_in = nrm(ks[5], (DEPTH, D_MODEL, D_IN), D_MODEL ** -0.5)
    lru_conv_w = nrm(ks[6], (DEPTH, LRU_CONV_WIDTH, D_LRU), LRU_CONV_WIDTH ** -0.5)
    lru_conv_b = nrm(ks[7], (DEPTH, D_LRU), 0.01)
    gate_a_w = nrm(ks[8], (DEPTH, LRU_HEADS, LRU_HEAD_DIM, LRU_HEAD_DIM), LRU_HEAD_DIM ** -0.5)
    gate_a_b = nrm(ks[9], (DEPTH, D_LRU), 0.01)
    gate_x_w = nrm(ks[10], (DEPTH, LRU_HEADS, LRU_HEAD_DIM, LRU_HEAD_DIM), LRU_HEAD_DIM ** -0.5)
    gate_x_b = nrm(ks[11], (DEPTH, D_LRU), 0.01)
    u = jax.random.uniform(ks[12], (DEPTH, D_LRU), f32, MIN_RAD ** 2, MAX_RAD ** 2)
    a_param = jnp.log(jnp.expm1(-0.5 * jnp.log(u)))
    short_conv_w = nrm(ks[13], (DEPTH, SHORT_CONV_WIDTH, D_CONV), SHORT_CONV_WIDTH ** -0.5)
    lru_out_g = 1.0 + nrm(ks[14], (DEPTH, D_LRU), 0.02)
    conv_out_g = 1.0 + nrm(ks[15], (DEPTH, D_CONV), 0.02)
    w_out = nrm(ks[16], (DEPTH, D_MIX, D_MODEL), D_MIX ** -0.5)
    norm2_g = 1.0 + nrm(ks[17], (DEPTH, D_MODEL), 0.02)
    w_mlp1 = nrm(ks[18], (DEPTH, D_MODEL, D_FF), D_MODEL ** -0.5)
    w_mlp2 = nrm(ks[19], (DEPTH, D_FF, D_MODEL), D_FF ** -0.5)
    final_g = 1.0 + nrm(ks[20], (D_MODEL,), 0.02)
    return {"x": x, "c": c, "ada_w": ada_w, "ada_b": ada_b, "norm1_g": norm1_g,
            "w_in": w_in, "lru_conv_w": lru_conv_w, "lru_conv_b": lru_conv_b,
            "gate_a_w": gate_a_w, "gate_a_b": gate_a_b, "gate_x_w": gate_x_w,
            "gate_x_b": gate_x_b, "a_param": a_param, "short_conv_w": short_conv_w,
            "lru_out_g": lru_out_g, "conv_out_g": conv_out_g, "w_out": w_out,
            "norm2_g": norm2_g, "w_mlp1": w_mlp1, "w_mlp2": w_mlp2, "final_g": final_g}


def _fwd_reference(x, c, ada_w, ada_b, norm1_g, w_in, lru_conv_w, lru_conv_b, gate_a_w, gate_a_b,
              gate_x_w, gate_x_b, a_param, short_conv_w, lru_out_g, conv_out_g, w_out,
              norm2_g, w_mlp1, w_mlp2, final_g):
    sc = jax.nn.silu(c)
    for l in range(DEPTH):
        mod = sc @ ada_w[l] + ada_b[l]
        shift1, scale1, gate1, shift2, scale2, gate2 = jnp.split(mod[:, None, :], N_ADA, axis=-1)

        h = rmsnorm(x, norm1_g[l]) * (1.0 + scale1) + shift1
        proj = h @ w_in[l]
        u_lx, u_ly, u_b, u_c, u_v = jnp.split(
            proj, np.cumsum([D_LRU, D_LRU, D_CONV, D_CONV])[:].tolist(), axis=-1)

        xl = causal_depthwise_conv(u_lx, lru_conv_w[l]) + lru_conv_b[l]
        hl = rg_lru(xl, gate_a_w[l], gate_a_b[l], gate_x_w[l], gate_x_b[l], a_param[l])
        y_lru = headwise_rmsnorm(jax.nn.gelu(u_ly) * hl, lru_out_g[l], LRU_HEADS)

        y_conv = u_b * causal_depthwise_conv(u_c * u_v, short_conv_w[l])
        y_conv = headwise_rmsnorm(y_conv, conv_out_g[l], CONV_GROUPS)

        mixed = jnp.concatenate([y_lru, y_conv], axis=-1) @ w_out[l]
        x = x + gate1 * mixed

        h2 = rmsnorm(x, norm2_g[l]) * (1.0 + scale2) + shift2
        x = x + gate2 * (jnp.square(jax.nn.relu(h2 @ w_mlp1[l])) @ w_mlp2[l])
    return rmsnorm(x, final_g)


import jax as _jax
import jax.numpy as _jnp

TWIN_FORMAT = 'train_step'
FWD_PARAMS = ['x', 'c', 'ada_w', 'ada_b', 'norm1_g', 'w_in', 'lru_conv_w', 'lru_conv_b', 'gate_a_w', 'gate_a_b', 'gate_x_w', 'gate_x_b', 'a_param', 'short_conv_w', 'lru_out_g', 'conv_out_g', 'w_out', 'norm2_g', 'w_mlp1', 'w_mlp2', 'final_g']
TWIN_WEIGHTS = ['ada_w', 'ada_b', 'norm1_g', 'w_in', 'lru_conv_w', 'lru_conv_b', 'gate_a_w', 'gate_a_b', 'gate_x_w', 'gate_x_b', 'a_param', 'short_conv_w', 'lru_out_g', 'conv_out_g', 'w_out', 'norm2_g', 'w_mlp1', 'w_mlp2', 'final_g']
TWIN_DIFF_INPUT = 'x'
TWIN_INPUTS = ['x', 'c', 'ada_w', 'ada_b', 'norm1_g', 'w_in', 'lru_conv_w', 'lru_conv_b', 'gate_a_w', 'gate_a_b', 'gate_x_w', 'gate_x_b', 'a_param', 'short_conv_w', 'lru_out_g', 'conv_out_g', 'w_out', 'norm2_g', 'w_mlp1', 'w_mlp2', 'final_g', 'loss_target', 'm_ada_w', 'm_ada_b', 'm_norm1_g', 'm_w_in', 'm_lru_conv_w', 'm_lru_conv_b', 'm_gate_a_w', 'm_gate_a_b', 'm_gate_x_w', 'm_gate_x_b', 'm_a_param', 'm_short_conv_w', 'm_lru_out_g', 'm_conv_out_g', 'm_w_out', 'm_norm2_g', 'm_w_mlp1', 'm_w_mlp2', 'm_final_g', 'v_ada_w', 'v_ada_b', 'v_norm1_g', 'v_w_in', 'v_lru_conv_w', 'v_lru_conv_b', 'v_gate_a_w', 'v_gate_a_b', 'v_gate_x_w', 'v_gate_x_b', 'v_a_param', 'v_short_conv_w', 'v_lru_out_g', 'v_conv_out_g', 'v_w_out', 'v_norm2_g', 'v_w_mlp1', 'v_w_mlp2', 'v_final_g']
TWIN_OUTPUTS = ['loss', 'grad_x', 'grad_ada_w', 'grad_ada_b', 'grad_norm1_g', 'grad_w_in', 'grad_lru_conv_w', 'grad_lru_conv_b', 'grad_gate_a_w', 'grad_gate_a_b', 'grad_gate_x_w', 'grad_gate_x_b', 'grad_a_param', 'grad_short_conv_w', 'grad_lru_out_g', 'grad_conv_out_g', 'grad_w_out', 'grad_norm2_g', 'grad_w_mlp1', 'grad_w_mlp2', 'grad_final_g', 'delta_ada_w', 'delta_ada_b', 'delta_norm1_g', 'delta_w_in', 'delta_lru_conv_w', 'delta_lru_conv_b', 'delta_gate_a_w', 'delta_gate_a_b', 'delta_gate_x_w', 'delta_gate_x_b', 'delta_a_param', 'delta_short_conv_w', 'delta_lru_out_g', 'delta_conv_out_g', 'delta_w_out', 'delta_norm2_g', 'delta_w_mlp1', 'delta_w_mlp2', 'delta_final_g', 'new_m_ada_w', 'new_m_ada_b', 'new_m_norm1_g', 'new_m_w_in', 'new_m_lru_conv_w', 'new_m_lru_conv_b', 'new_m_gate_a_w', 'new_m_gate_a_b', 'new_m_gate_x_w', 'new_m_gate_x_b', 'new_m_a_param', 'new_m_short_conv_w', 'new_m_lru_out_g', 'new_m_conv_out_g', 'new_m_w_out', 'new_m_norm2_g', 'new_m_w_mlp1', 'new_m_w_mlp2', 'new_m_final_g', 'new_v_ada_w', 'new_v_ada_b', 'new_v_norm1_g', 'new_v_w_in', 'new_v_lru_conv_w', 'new_v_lru_conv_b', 'new_v_gate_a_w', 'new_v_gate_a_b', 'new_v_gate_x_w', 'new_v_gate_x_b', 'new_v_a_param', 'new_v_short_conv_w', 'new_v_lru_out_g', 'new_v_conv_out_g', 'new_v_w_out', 'new_v_norm2_g', 'new_v_w_mlp1', 'new_v_w_mlp2', 'new_v_final_g']
TWIN_LEAF_KINDS = {'loss': 'loss', 'grad_x': 'grad_x', 'grad_ada_w': 'grad_w', 'grad_ada_b': 'grad_w', 'grad_norm1_g': 'grad_w', 'grad_w_in': 'grad_w', 'grad_lru_conv_w': 'grad_w', 'grad_lru_conv_b': 'grad_w', 'grad_gate_a_w': 'grad_w', 'grad_gate_a_b': 'grad_w', 'grad_gate_x_w': 'grad_w', 'grad_gate_x_b': 'grad_w', 'grad_a_param': 'grad_w', 'grad_short_conv_w': 'grad_w', 'grad_lru_out_g': 'grad_w', 'grad_conv_out_g': 'grad_w', 'grad_w_out': 'grad_w', 'grad_norm2_g': 'grad_w', 'grad_w_mlp1': 'grad_w', 'grad_w_mlp2': 'grad_w', 'grad_final_g': 'grad_w', 'delta_ada_w': 'delta_w', 'delta_ada_b': 'delta_w', 'delta_norm1_g': 'delta_w', 'delta_w_in': 'delta_w', 'delta_lru_conv_w': 'delta_w', 'delta_lru_conv_b': 'delta_w', 'delta_gate_a_w': 'delta_w', 'delta_gate_a_b': 'delta_w', 'delta_gate_x_w': 'delta_w', 'delta_gate_x_b': 'delta_w', 'delta_a_param': 'delta_w', 'delta_short_conv_w': 'delta_w', 'delta_lru_out_g': 'delta_w', 'delta_conv_out_g': 'delta_w', 'delta_w_out': 'delta_w', 'delta_norm2_g': 'delta_w', 'delta_w_mlp1': 'delta_w', 'delta_w_mlp2': 'delta_w', 'delta_final_g': 'delta_w', 'new_m_ada_w': 'new_m', 'new_m_ada_b': 'new_m', 'new_m_norm1_g': 'new_m', 'new_m_w_in': 'new_m', 'new_m_lru_conv_w': 'new_m', 'new_m_lru_conv_b': 'new_m', 'new_m_gate_a_w': 'new_m', 'new_m_gate_a_b': 'new_m', 'new_m_gate_x_w': 'new_m', 'new_m_gate_x_b': 'new_m', 'new_m_a_param': 'new_m', 'new_m_short_conv_w': 'new_m', 'new_m_lru_out_g': 'new_m', 'new_m_conv_out_g': 'new_m', 'new_m_w_out': 'new_m', 'new_m_norm2_g': 'new_m', 'new_m_w_mlp1': 'new_m', 'new_m_w_mlp2': 'new_m', 'new_m_final_g': 'new_m', 'new_v_ada_w': 'new_v', 'new_v_ada_b': 'new_v', 'new_v_norm1_g': 'new_v', 'new_v_w_in': 'new_v', 'new_v_lru_conv_w': 'new_v', 'new_v_lru_conv_b': 'new_v', 'new_v_gate_a_w': 'new_v', 'new_v_gate_a_b': 'new_v', 'new_v_gate_x_w': 'new_v', 'new_v_gate_x_b': 'new_v', 'new_v_a_param': 'new_v', 'new_v_short_conv_w': 'new_v', 'new_v_lru_out_g': 'new_v', 'new_v_conv_out_g': 'new_v', 'new_v_w_out': 'new_v', 'new_v_norm2_g': 'new_v', 'new_v_w_mlp1': 'new_v', 'new_v_w_mlp2': 'new_v', 'new_v_final_g': 'new_v'}


def _forward(args):
    return _fwd_reference(*[args[k] for k in FWD_PARAMS])


def _output_shape():
    out = _jax.eval_shape(lambda: _forward(_fwd_setup_inputs(0)))
    return out.shape, out.dtype

N_MICROBATCH = 1
ADAM_LR = 0.001
ADAM_B1 = 0.9
ADAM_B2 = 0.999
ADAM_EPS = 1e-08
ADAM_WD = 0.01
ADAM_STEP = 10
PER_EXAMPLE_BATCH_AXIS = {'x': 0, 'c': 0, 'loss_target': 0}
SHARED_INPUTS = []
_WEIGHT_DTYPES = {'ada_w': _jnp.float32, 'ada_b': _jnp.float32, 'norm1_g': _jnp.float32, 'w_in': _jnp.float32, 'lru_conv_w': _jnp.float32, 'lru_conv_b': _jnp.float32, 'gate_a_w': _jnp.float32, 'gate_a_b': _jnp.float32, 'gate_x_w': _jnp.float32, 'gate_x_b': _jnp.float32, 'a_param': _jnp.float32, 'short_conv_w': _jnp.float32, 'lru_out_g': _jnp.float32, 'conv_out_g': _jnp.float32, 'w_out': _jnp.float32, 'norm2_g': _jnp.float32, 'w_mlp1': _jnp.float32, 'w_mlp2': _jnp.float32, 'final_g': _jnp.float32}
MOMENT_SCALE = {'ada_w': 6.678095e-02, 'ada_b': 1.080639e-01, 'norm1_g': 6.192669e-02, 'w_in': 4.168050e-02, 'lru_conv_w': 4.499979e-02, 'lru_conv_b': 1.171895e-01, 'gate_a_w': 8.357273e-03, 'gate_a_b': 7.709774e-03, 'gate_x_w': 1.439511e-02, 'gate_x_b': 1.363397e-02, 'a_param': 1.654298e-02, 'short_conv_w': 4.019865e-02, 'lru_out_g': 3.872860e-02, 'conv_out_g': 3.876420e-02, 'w_out': 3.754012e-02, 'norm2_g': 5.579212e-02, 'w_mlp1': 2.860656e-02, 'w_mlp2': 5.021965e-02, 'final_g': 1.617704e+01}


def _to_microbatches(a, axis):
    t = _jnp.moveaxis(a, axis, 0)
    t = t.reshape((N_MICROBATCH, t.shape[0] // N_MICROBATCH) + t.shape[1:])
    return _jnp.moveaxis(t, 1, axis + 1)


def setup_inputs(seed: int = 0) -> dict:
    inp = _fwd_setup_inputs(seed)
    key = _jax.random.fold_in(_jax.random.key(seed), 7919)
    shape, _ = _output_shape()
    out = dict(inp)
    out["loss_target"] = _jax.random.normal(_jax.random.fold_in(key, 0), shape, _jnp.float32)
    for i, name in enumerate(TWIN_WEIGHTS):
        w = inp[name].astype(_jnp.float32)
        if MOMENT_SCALE is None:
            s = _jnp.sqrt(_jnp.mean(_jnp.square(w)) + 1e-30)
        else:
            s = MOMENT_SCALE[name]
        km, kv = _jax.random.split(_jax.random.fold_in(key, i + 1))
        out[name] = w
        out["m_" + name] = s * _jax.random.normal(km, w.shape, _jnp.float32)
        out["v_" + name] = (s * s) * _jax.random.uniform(kv, w.shape, _jnp.float32, 0.5, 1.5)
    if N_MICROBATCH > 1:
        for name, axis in PER_EXAMPLE_BATCH_AXIS.items():
            out[name] = _to_microbatches(out[name], axis)
    return {'x': out['x'], 'c': out['c'], 'ada_w': out['ada_w'], 'ada_b': out['ada_b'], 'norm1_g': out['norm1_g'], 'w_in': out['w_in'], 'lru_conv_w': out['lru_conv_w'], 'lru_conv_b': out['lru_conv_b'], 'gate_a_w': out['gate_a_w'], 'gate_a_b': out['gate_a_b'], 'gate_x_w': out['gate_x_w'], 'gate_x_b': out['gate_x_b'], 'a_param': out['a_param'], 'short_conv_w': out['short_conv_w'], 'lru_out_g': out['lru_out_g'], 'conv_out_g': out['conv_out_g'], 'w_out': out['w_out'], 'norm2_g': out['norm2_g'], 'w_mlp1': out['w_mlp1'], 'w_mlp2': out['w_mlp2'], 'final_g': out['final_g'], 'loss_target': out['loss_target'], 'm_ada_w': out['m_ada_w'], 'm_ada_b': out['m_ada_b'], 'm_norm1_g': out['m_norm1_g'], 'm_w_in': out['m_w_in'], 'm_lru_conv_w': out['m_lru_conv_w'], 'm_lru_conv_b': out['m_lru_conv_b'], 'm_gate_a_w': out['m_gate_a_w'], 'm_gate_a_b': out['m_gate_a_b'], 'm_gate_x_w': out['m_gate_x_w'], 'm_gate_x_b': out['m_gate_x_b'], 'm_a_param': out['m_a_param'], 'm_short_conv_w': out['m_short_conv_w'], 'm_lru_out_g': out['m_lru_out_g'], 'm_conv_out_g': out['m_conv_out_g'], 'm_w_out': out['m_w_out'], 'm_norm2_g': out['m_norm2_g'], 'm_w_mlp1': out['m_w_mlp1'], 'm_w_mlp2': out['m_w_mlp2'], 'm_final_g': out['m_final_g'], 'v_ada_w': out['v_ada_w'], 'v_ada_b': out['v_ada_b'], 'v_norm1_g': out['v_norm1_g'], 'v_w_in': out['v_w_in'], 'v_lru_conv_w': out['v_lru_conv_w'], 'v_lru_conv_b': out['v_lru_conv_b'], 'v_gate_a_w': out['v_gate_a_w'], 'v_gate_a_b': out['v_gate_a_b'], 'v_gate_x_w': out['v_gate_x_w'], 'v_gate_x_b': out['v_gate_x_b'], 'v_a_param': out['v_a_param'], 'v_short_conv_w': out['v_short_conv_w'], 'v_lru_out_g': out['v_lru_out_g'], 'v_conv_out_g': out['v_conv_out_g'], 'v_w_out': out['v_w_out'], 'v_norm2_g': out['v_norm2_g'], 'v_w_mlp1': out['v_w_mlp1'], 'v_w_mlp2': out['v_w_mlp2'], 'v_final_g': out['v_final_g']}


def _loss(weights, diff, rest, loss_target):
    with _jax.named_scope("forward"):
        args = {**rest, TWIN_DIFF_INPUT: diff, **{k: w.astype(_WEIGHT_DTYPES[k]) for k, w in weights.items()}}
        y = _forward(args)
    with _jax.named_scope("loss_head"):
        err = _jnp.square(y.astype(_jnp.float32) - loss_target)
        return 0.5 * _jnp.sum(_jnp.mean(err, axis=-1)) if err.ndim else 0.5 * err


def _adamw(w, g, m, v):
    m = ADAM_B1 * m + (1.0 - ADAM_B1) * g
    v = ADAM_B2 * v + (1.0 - ADAM_B2) * _jnp.square(g)
    m_hat = m / (1.0 - ADAM_B1 ** ADAM_STEP)
    v_hat = v / (1.0 - ADAM_B2 ** ADAM_STEP)
    delta = -ADAM_LR * (m_hat / (_jnp.sqrt(v_hat) + ADAM_EPS) + ADAM_WD * w)
    return delta, m, v


def reference(x, c, ada_w, ada_b, norm1_g, w_in, lru_conv_w, lru_conv_b, gate_a_w, gate_a_b, gate_x_w, gate_x_b, a_param, short_conv_w, lru_out_g, conv_out_g, w_out, norm2_g, w_mlp1, w_mlp2, final_g, loss_target, m_ada_w, m_ada_b, m_norm1_g, m_w_in, m_lru_conv_w, m_lru_conv_b, m_gate_a_w, m_gate_a_b, m_gate_x_w, m_gate_x_b, m_a_param, m_short_conv_w, m_lru_out_g, m_conv_out_g, m_w_out, m_norm2_g, m_w_mlp1, m_w_mlp2, m_final_g, v_ada_w, v_ada_b, v_norm1_g, v_w_in, v_lru_conv_w, v_lru_conv_b, v_gate_a_w, v_gate_a_b, v_gate_x_w, v_gate_x_b, v_a_param, v_short_conv_w, v_lru_out_g, v_conv_out_g, v_w_out, v_norm2_g, v_w_mlp1, v_w_mlp2, v_final_g):
    given = dict(x=x, c=c, ada_w=ada_w, ada_b=ada_b, norm1_g=norm1_g, w_in=w_in, lru_conv_w=lru_conv_w, lru_conv_b=lru_conv_b, gate_a_w=gate_a_w, gate_a_b=gate_a_b, gate_x_w=gate_x_w, gate_x_b=gate_x_b, a_param=a_param, short_conv_w=short_conv_w, lru_out_g=lru_out_g, conv_out_g=conv_out_g, w_out=w_out, norm2_g=norm2_g, w_mlp1=w_mlp1, w_mlp2=w_mlp2, final_g=final_g, loss_target=loss_target, m_ada_w=m_ada_w, m_ada_b=m_ada_b, m_norm1_g=m_norm1_g, m_w_in=m_w_in, m_lru_conv_w=m_lru_conv_w, m_lru_conv_b=m_lru_conv_b, m_gate_a_w=m_gate_a_w, m_gate_a_b=m_gate_a_b, m_gate_x_w=m_gate_x_w, m_gate_x_b=m_gate_x_b, m_a_param=m_a_param, m_short_conv_w=m_short_conv_w, m_lru_out_g=m_lru_out_g, m_conv_out_g=m_conv_out_g, m_w_out=m_w_out, m_norm2_g=m_norm2_g, m_w_mlp1=m_w_mlp1, m_w_mlp2=m_w_mlp2, m_final_g=m_final_g, v_ada_w=v_ada_w, v_ada_b=v_ada_b, v_norm1_g=v_norm1_g, v_w_in=v_w_in, v_lru_conv_w=v_lru_conv_w, v_lru_conv_b=v_lru_conv_b, v_gate_a_w=v_gate_a_w, v_gate_a_b=v_gate_a_b, v_gate_x_w=v_gate_x_w, v_gate_x_b=v_gate_x_b, v_a_param=v_a_param, v_short_conv_w=v_short_conv_w, v_lru_out_g=v_lru_out_g, v_conv_out_g=v_conv_out_g, v_w_out=v_w_out, v_norm2_g=v_norm2_g, v_w_mlp1=v_w_mlp1, v_w_mlp2=v_w_mlp2, v_final_g=v_final_g)
    weights = {n: given[n] for n in TWIN_WEIGHTS}
    shared = {n: given[n] for n in SHARED_INPUTS}
    per_example = {n: given[n] for n in ['x', 'c']}
    grad_fn = _jax.value_and_grad(_loss, argnums=(0, 1))

    def one_microbatch(ex, loss_target):
        ex = dict(ex)
        diff = ex.pop(TWIN_DIFF_INPUT)
        return grad_fn(weights, diff, {**shared, **ex}, loss_target)

    if N_MICROBATCH == 1:
        loss, (grad_w, grad_x) = one_microbatch(per_example, given["loss_target"])
    else:
        def body(carry, xs):
            loss_sum, grad_sum = carry
            l_k, (gw_k, gx_k) = one_microbatch(xs[0], xs[1])
            with _jax.named_scope("update"):
                return (loss_sum + l_k, _jax.tree.map(_jnp.add, grad_sum, gw_k)), gx_k

        init = (_jnp.zeros((), _jnp.float32), _jax.tree.map(_jnp.zeros_like, weights))
        (loss, grad_w), grad_x = _jax.lax.scan(body, init, (per_example, given["loss_target"]))
    with _jax.named_scope("update"):
        delta_w, new_m, new_v = {}, {}, {}
        for n in TWIN_WEIGHTS:
            delta_w[n], new_m[n], new_v[n] = _adamw(weights[n], grad_w[n], given["m_" + n], given["v_" + n])
    return (loss, grad_x, *[grad_w[n] for n in TWIN_WEIGHTS], *[delta_w[n] for n in TWIN_WEIGHTS],
            *[new_m[n] for n in TWIN_WEIGHTS], *[new_v[n] for n in TWIN_WEIGHTS])
```

```python
import functools

import jax
import jax.numpy as jnp
from jax import lax
from jax.experimental import pallas as pl
from jax.experimental.pallas import tpu as pltpu

F32 = jnp.float32
BF16 = jnp.bfloat16
MESH = pl.DeviceIdType.MESH

N_DEV = 8
D = 1024
W = 512
D_IN = 5 * W
D_FF = 4096
FF_BLK = D_FF // N_DEV
EPS = 1e-6
C_GATE = 8.0
CONV_L = 4
CONV_S = 3
HALO = 8

ROWS_W1T, ROWS_W2, ROWS_WOUT, ROWS_WIN = FF_BLK, FF_BLK, D // N_DEV, D_IN // N_DEV
OFF_W1T = 0
OFF_W2 = OFF_W1T + ROWS_W1T
OFF_WOUT = OFF_W2 + ROWS_W2
OFF_WIN = OFF_WOUT + ROWS_WOUT
PACK_ROWS = OFF_WIN + ROWS_WIN

ADAM_LR = 0.001
ADAM_B1 = 0.9
ADAM_B2 = 0.999
ADAM_EPS = 1e-08
ADAM_WD = 0.01
ADAM_STEP = 10

VMEM_LIMIT = 56 * 1024 * 1024

TB_MIX = 256
TB_MIXB = 128
TB_MLP = 256
TB_MLPB = 512

ANY = pl.BlockSpec(memory_space=pl.ANY)
WHOLE = pl.BlockSpec(memory_space=pltpu.VMEM)


def _dot(a, b):
    return jnp.dot(a, b, preferred_element_type=F32)


def _dot_nt(a, b):
    return lax.dot_general(a, b, (((1,), (1,)), ((), ())), preferred_element_type=F32)


def _dot_tn(a, b):
    return lax.dot_general(a, b, (((0,), (0,)), ((), ())), preferred_element_type=F32)


def _sigmoid(v):
    return 1.0 / (1.0 + jnp.exp(-v))


def _softplus(v):
    t = jnp.exp(-jnp.abs(v))
    small = t * (1.0 - t * (0.5 - t * (1.0 / 3.0)))
    return jnp.maximum(v, 0.0) + jnp.where(t < 1e-2, small, jnp.log(1.0 + t))


def _expm1(v):
    series = v * (1.0 + v * (0.5 + v * (1.0 / 6.0 + v * (1.0 / 24.0))))
    return jnp.where(jnp.abs(v) < 0.05, series, jnp.exp(v) - 1.0)


_GELU_K = 0.7978845608028654
_GELU_C = 0.044715


def _gelu(u):
    th = jnp.tanh(_GELU_K * (u + _GELU_C * u * u * u))
    return 0.5 * u * (1.0 + th), th


def _gelu_grad(u, th):
    return 0.5 * (1.0 + th) + 0.5 * u * (1.0 - th * th) * _GELU_K * (1.0 + 3.0 * _GELU_C * u * u)


def _group_mean(v, avg):
    hi = v.astype(BF16)
    lo = (v - hi.astype(F32)).astype(BF16)
    return _dot(hi, avg) + _dot(lo, avg)


def _colsum(v):
    return jnp.sum(v, axis=0, keepdims=True)


def _rowmean(v):
    return jnp.mean(v, axis=-1, keepdims=True)


def _load_packed(wpack_hbm, off, rows, dst, sem):
    copies = [
        pltpu.make_async_copy(wpack_hbm.at[d, pl.ds(off, rows), :], dst.at[pl.ds(d * rows, rows), :], sem)
        for d in range(N_DEV)
    ]
    for cp in copies:
        cp.start()
    return copies


def _scan_groups(n_groups, a_ref, b_ref, out_ref, carry_ref, reverse):
    row = lax.broadcasted_iota(jnp.int32, (HALO, W), 0)

    def step(k, carry):
        g = (n_groups - 1 - k) if reverse else k
        rows = pl.ds(pl.multiple_of(g * HALO, HALO), HALO)
        a = a_ref[rows, :]
        b = b_ref[rows, :]
        for s in (1, 2, 4):
            if reverse:
                keep = row < HALO - s
                sh = HALO - s
            else:
                keep = row >= s
                sh = s
            a_sh = pltpu.roll(a, sh, axis=0)
            b_sh = pltpu.roll(b, sh, axis=0)
            b = jnp.where(keep, a * b_sh + b, b)
            a = jnp.where(keep, a * a_sh, a)
        h = b + a * carry
        out_ref[rows, :] = h
        edge = h[0:1, :] if reverse else h[HALO - 1:HALO, :]
        return jnp.broadcast_to(edge, (HALO, W))

    carry_ref[...] = lax.fori_loop(0, n_groups, step, carry_ref[...])


def _mixer_fwd(x, modraw, adab, g1, wl, bl, bda, bdx, ba, bxb, ap, ws, gl, gc, avg, wpack):
    t_len = x.shape[0]
    tb = TB_MIX
    nb = t_len // tb

    def body(x_ref, modraw_ref, adab_ref, g1_ref, wl_ref, bl_ref, bda_ref, bdx_ref, ba_ref, bxb_ref, ap_ref,
             ws_ref, gl_ref, gc_ref, avg_ref, wpack_hbm, proj_ref, hl_ref, mixed_ref,
             win_v, wout_v, sem, ulx_ext, cv_ext, hcar, a_s, b_s):
        i = pl.program_id(0)

        @pl.when(i == 0)
        def _():
            cps = _load_packed(wpack_hbm, OFF_WIN, ROWS_WIN, win_v, sem.at[0])
            cps += _load_packed(wpack_hbm, OFF_WOUT, ROWS_WOUT, wout_v, sem.at[1])
            ulx_ext[0:HALO, :] = jnp.zeros((HALO, W), F32)
            cv_ext[0:HALO, :] = jnp.zeros((HALO, W), F32)
            hcar[...] = jnp.zeros((HALO, W), F32)
            for cp in cps:
                cp.wait()

        mod = modraw_ref[...] + adab_ref[...]
        shift1, scale1, gate1 = mod[0:1], mod[1:2], mod[2:3]
        x = x_ref[...]
        r1 = lax.rsqrt(_rowmean(x * x) + EPS)
        h = (x * r1 * g1_ref[...]) * (1.0 + scale1) + shift1
        proj = _dot_nt(h.astype(BF16), win_v[...])
        proj_ref[...] = proj
        u_lx, u_ly, u_b, u_c, u_v = (proj[:, k * W:(k + 1) * W] for k in range(5))

        ulx_ext[HALO:HALO + tb, :] = u_lx
        xl = bl_ref[...] + wl_ref[CONV_L - 1:CONV_L, :] * u_lx
        for k in range(CONV_L - 1):
            xl = xl + wl_ref[k:k + 1, :] * ulx_ext[pl.ds(HALO - (CONV_L - 1) + k, tb), :]
        ulx_ext[0:HALO, :] = ulx_ext[tb:tb + HALO, :]
        xlb = xl.astype(BF16)
        r = _sigmoid(_dot(xlb, bda_ref[...]) + ba_ref[...])
        ig = _sigmoid(_dot(xlb, bdx_ref[...]) + bxb_ref[...])
        log_a = (-C_GATE) * r * _softplus(ap_ref[...])
        a = jnp.exp(log_a)
        mult = jnp.sqrt(-_expm1(2.0 * log_a))
        grow = i * tb + lax.broadcasted_iota(jnp.int32, (tb, W), 0)
        mult = jnp.where(grow == 0, 1.0, mult)
        a_s[...] = a
        b_s[...] = mult * (ig * xl)
        _scan_groups(tb // HALO, a_s, b_s, hl_ref, hcar, reverse=False)
        hl = hl_ref[...]
        ge, _ = _gelu(u_ly)
        p = ge * hl
        y_lru = p * lax.rsqrt(_group_mean(p * p, avg_ref[...]) + EPS) * gl_ref[...]

        cv = u_c * u_v
        cv_ext[HALO:HALO + tb, :] = cv
        cc = ws_ref[CONV_S - 1:CONV_S, :] * cv
        for k in range(CONV_S - 1):
            cc = cc + ws_ref[k:k + 1, :] * cv_ext[pl.ds(HALO - (CONV_S - 1) + k, tb), :]
        cv_ext[0:HALO, :] = cv_ext[tb:tb + HALO, :]
        q = u_b * cc
        y_conv = q * lax.rsqrt(_group_mean(q * q, avg_ref[...]) + EPS) * gc_ref[...]

        mixed_ref[...] = (_dot(y_lru.astype(BF16), wout_v[0:W, :]) + _dot(y_conv.astype(BF16), wout_v[W:2 * W, :]))

    tok = lambda cols: pl.BlockSpec((tb, cols), lambda i: (i, 0))
    full = lambda a: pl.BlockSpec(a.shape, lambda i: (0,) * a.ndim)
    small = (modraw, adab, g1, wl, bl, bda, bdx, ba, bxb, ap, ws, gl, gc, avg)
    return pl.pallas_call(
        body,
        name="mixer_fwd",
        grid=(nb,),
        in_specs=[tok(D)] + [full(a) for a in small] + [ANY],
        out_specs=[tok(D_IN), tok(W), tok(D)],
        out_shape=[jax.ShapeDtypeStruct((t_len, D_IN), F32), jax.ShapeDtypeStruct((t_len, W), F32),
                   jax.ShapeDtypeStruct((t_len, D), F32)],
        scratch_shapes=[pltpu.VMEM((D_IN, D), BF16), pltpu.VMEM((D, D), BF16), pltpu.SemaphoreType.DMA((2,)),
                        pltpu.VMEM((tb + HALO, W), F32), pltpu.VMEM((tb + HALO, W), F32), pltpu.VMEM((HALO, W), F32),
                        pltpu.VMEM((tb, W), F32), pltpu.VMEM((tb, W), F32)],
        compiler_params=pltpu.CompilerParams(dimension_semantics=("arbitrary",), vmem_limit_bytes=VMEM_LIMIT),
    )(x, *small, wpack)


def _mlp_fwd(x, mixed, tgt, modraw, adab, g2, gf, wpack):
    t_len = x.shape[0]
    tb = TB_MLP
    nb = t_len // tb

    def body(x_ref, mixed_ref, tgt_ref, modraw_ref, adab_ref, g2_ref, gf_ref, wpack_hbm,
             h2_ref, f_ref, dx2_ref, dz_ref, vec_ref, loss_ref, w1t_v, w2_v, sem):
        i = pl.program_id(0)

        @pl.when(i == 0)
        def _():
            cps = _load_packed(wpack_hbm, OFF_W1T, ROWS_W1T, w1t_v, sem.at[0])
            cps += _load_packed(wpack_hbm, OFF_W2, ROWS_W2, w2_v, sem.at[1])
            vec_ref[...] = jnp.zeros(vec_ref.shape, F32)
            loss_ref[...] = jnp.zeros(loss_ref.shape, F32)
            for cp in cps:
                cp.wait()

        mod = modraw_ref[...] + adab_ref[...]
        gate1, shift2, scale2, gate2 = mod[2:3], mod[3:4], mod[4:5], mod[5:6]
        x1 = x_ref[...] + gate1 * mixed_ref[...]
        r2 = lax.rsqrt(_rowmean(x1 * x1) + EPS)
        h2 = (x1 * r2 * g2_ref[...]) * (1.0 + scale2) + shift2
        h2b = h2.astype(BF16)
        h2_ref[...] = h2b
        z = jnp.zeros((tb, D), F32)
        for j in range(N_DEV):
            cols = slice(j * FF_BLK, (j + 1) * FF_BLK)
            fj = _dot_nt(h2b, w1t_v[cols, :])
            f_ref[:, cols] = fj
            rf = jnp.maximum(fj, 0.0)
            z = z + _dot((rf * rf).astype(BF16), w2_v[cols, :])
        x2 = x1 + gate2 * z
        r3 = lax.rsqrt(_rowmean(x2 * x2) + EPS)
        xn3 = x2 * r3
        diff = xn3 * gf_ref[...] - tgt_ref[...]
        sq = _colsum(diff * diff)
        loss_ref[...] += jnp.broadcast_to(jnp.sum(sq, axis=1, keepdims=True) * (0.5 / D), loss_ref.shape)
        dy = diff * (1.0 / D)
        dyn = dy * gf_ref[...]
        dx2 = r3 * (dyn - xn3 * _rowmean(dyn * xn3))
        dx2_ref[...] = dx2
        dz_ref[...] = (gate2 * dx2).astype(BF16)
        vec_ref[0:1, :] += _colsum(dx2 * z)
        vec_ref[1:2, :] += _colsum(dy * xn3)

    tok = lambda cols: pl.BlockSpec((tb, cols), lambda i: (i, 0))
    full = lambda a: pl.BlockSpec(a.shape, lambda i: (0,) * a.ndim)
    small = (modraw, adab, g2, gf)
    return pl.pallas_call(
        body,
        name="mlp_fwd",
        grid=(nb,),
        in_specs=[tok(D), tok(D), tok(D)] + [full(a) for a in small] + [ANY],
        out_specs=[tok(D), tok(D_FF), tok(D), tok(D), pl.BlockSpec((8, D), lambda i: (0, 0)),
                   pl.BlockSpec((8, 128), lambda i: (0, 0))],
        out_shape=[jax.ShapeDtypeStruct((t_len, D), BF16), jax.ShapeDtypeStruct((t_len, D_FF), F32),
                   jax.ShapeDtypeStruct((t_len, D), F32), jax.ShapeDtypeStruct((t_len, D), BF16),
                   jax.ShapeDtypeStruct((8, D), F32), jax.ShapeDtypeStruct((8, 128), F32)],
        scratch_shapes=[pltpu.VMEM((D_FF, D), BF16), pltpu.VMEM((D_FF, D), BF16), pltpu.SemaphoreType.DMA((2,))],
        compiler_params=pltpu.CompilerParams(dimension_semantics=("arbitrary",), vmem_limit_bytes=VMEM_LIMIT),
    )(x, mixed, tgt, *small, wpack)


def _mlp_bwd(h2, f, dz, wpack):
    t_len = h2.shape[0]
    tb = TB_MLPB
    nb = t_len // tb

    def body(h2_ref, f_ref, dz_ref, w1t_ref, w2_ref, dh2_ref, dw1t_ref, dw2_ref):
        j = pl.program_id(0)
        t = pl.program_id(1)
        rows = pl.ds(pl.multiple_of(t * tb, tb), tb)
        w1t = w1t_ref[0]
        w2 = w2_ref[0]
        dz = dz_ref[...]
        rf = jnp.maximum(f_ref[...], 0.0)
        a2 = (rf * rf).astype(BF16)
        df = (_dot_nt(dz, w2) * (2.0 * rf)).astype(BF16)
        g2 = _dot_tn(a2, dz)
        g1 = _dot_tn(df, h2_ref[...])
        dh = _dot(df, w1t)

        @pl.when(t == 0)
        def _():
            dw2_ref[0] = g2
            dw1t_ref[0] = g1

        @pl.when(t != 0)
        def _():
            dw2_ref[0] += g2
            dw1t_ref[0] += g1

        @pl.when(j == 0)
        def _():
            dh2_ref[rows, :] = dh

        @pl.when(j != 0)
        def _():
            dh2_ref[rows, :] += dh

    return pl.pallas_call(
        body,
        name="mlp_bwd",
        grid=(N_DEV, nb),
        in_specs=[pl.BlockSpec((tb, D), lambda j, t: (t, 0)),
                  pl.BlockSpec((tb, FF_BLK), lambda j, t: (t, j)),
                  pl.BlockSpec((tb, D), lambda j, t: (t, 0)),
                  pl.BlockSpec((1, ROWS_W1T, D), lambda j, t: (j, OFF_W1T // ROWS_W1T, 0)),
                  pl.BlockSpec((1, ROWS_W2, D), lambda j, t: (j, OFF_W2 // ROWS_W2, 0))],
        out_specs=[pl.BlockSpec((t_len, D), lambda j, t: (0, 0)),
                   pl.BlockSpec((1, FF_BLK, D), lambda j, t: (j, 0, 0)),
                   pl.BlockSpec((1, FF_BLK, D), lambda j, t: (j, 0, 0))],
        out_shape=[jax.ShapeDtypeStruct((t_len, D), F32), jax.ShapeDtypeStruct((N_DEV, FF_BLK, D), F32),
                   jax.ShapeDtypeStruct((N_DEV, FF_BLK, D), F32)],
        compiler_params=pltpu.CompilerParams(dimension_semantics=("arbitrary", "arbitrary"),
                                             vmem_limit_bytes=VMEM_LIMIT),
    )(h2, f, dz, wpack, wpack)


V_SHIFT1, V_SCALE1, V_GATE1, V_SHIFT2, V_SCALE2, V_G1, V_G2 = 0, 1, 2, 3, 4, 6, 7
V_BL_BA, V_BX_SP, V_GL_GC, V_WL01, V_WL23, V_WS01, V_WS2 = 8, 9, 10, 11, 12, 13, 14
V_ROWS = 16


def _mixer_bwd(x, mixed, dh2, dx2, proj, hl, modraw, adab, g1, g2, wl, bl, bda, bdx, ba, bxb, ap, ws, gl, gc, avg, wpack):
    t_len = x.shape[0]
    tb = TB_MIXB
    nb = t_len // tb
    hb = tb // HALO

    def body(x_ref, mixed_ref, dh2_ref, dx2_ref, proj_ref, projh_ref, hl_ref, hlh_ref,
             modraw_ref, adab_ref, g1_ref, g2_ref, wl_ref, bl_ref, bda_ref, bdx_ref, ba_ref, bxb_ref, ap_ref,
             ws_ref, gl_ref, gc_ref, avg_ref, wpack_hbm,
             gx_ref, vec_ref, dga_ref, dgx_ref, dwout_hbm, dwint_hbm,
             win_v, wout_v, dwout_v, dwint_v, sem, ulx_ext, cv_ext, hl_ext, a_ext, dxl_ext, dcc_ext, dcar, an_s, g_s, dh_s):
        i = pl.program_id(0)
        blk = nb - 1 - i

        @pl.when(i == 0)
        def _():
            cps = _load_packed(wpack_hbm, OFF_WIN, ROWS_WIN, win_v, sem.at[0])
            cps += _load_packed(wpack_hbm, OFF_WOUT, ROWS_WOUT, wout_v, sem.at[1])
            vec_ref[...] = jnp.zeros(vec_ref.shape, F32)
            dga_ref[...] = jnp.zeros(dga_ref.shape, F32)
            dgx_ref[...] = jnp.zeros(dgx_ref.shape, F32)
            dwout_v[...] = jnp.zeros(dwout_v.shape, F32)
            dwint_v[...] = jnp.zeros(dwint_v.shape, F32)
            zero = jnp.zeros((HALO, W), F32)
            a_ext[tb:tb + HALO, :] = zero
            dxl_ext[tb:tb + HALO, :] = zero
            dcc_ext[tb:tb + HALO, :] = zero
            dcar[...] = zero
            for cp in cps:
                cp.wait()

        mod = modraw_ref[...] + adab_ref[...]
        shift1, scale1, gate1, scale2 = mod[0:1], mod[1:2], mod[2:3], mod[4:5]
        x = x_ref[...]
        mixed = mixed_ref[...]

        x1 = x + gate1 * mixed
        r2 = lax.rsqrt(_rowmean(x1 * x1) + EPS)
        xn2 = x1 * r2
        dh2 = dh2_ref[...]
        vec_ref[V_SHIFT2:V_SHIFT2 + 1, :] += _colsum(dh2)
        vec_ref[V_SCALE2:V_SCALE2 + 1, :] += _colsum(dh2 * xn2 * g2_ref[...])
        vec_ref[V_G2:V_G2 + 1, :] += _colsum(dh2 * (1.0 + scale2) * xn2)
        dxn2 = dh2 * g2_ref[...] * (1.0 + scale2)
        dx1 = dx2_ref[...] + r2 * (dxn2 - xn2 * _rowmean(dxn2 * xn2))
        vec_ref[V_GATE1:V_GATE1 + 1, :] += _colsum(dx1 * mixed)
        dmixed = (gate1 * dx1).astype(BF16)

        proj = proj_ref[...]
        u_lx, u_ly, u_b, u_c, u_v = (proj[:, k * W:(k + 1) * W] for k in range(5))
        has_prev = (blk > 0).astype(F32)
        projh = projh_ref[...]
        ulx_ext[0:HALO, :] = projh[:, 0:W] * has_prev
        ulx_ext[HALO:HALO + tb, :] = u_lx
        xl = bl_ref[...] + wl_ref[CONV_L - 1:CONV_L, :] * u_lx
        for k in range(CONV_L - 1):
            xl = xl + wl_ref[k:k + 1, :] * ulx_ext[pl.ds(HALO - (CONV_L - 1) + k, tb), :]
        xlb = xl.astype(BF16)
        r = _sigmoid(_dot(xlb, bda_ref[...]) + ba_ref[...])
        ig = _sigmoid(_dot(xlb, bdx_ref[...]) + bxb_ref[...])
        sp = _softplus(ap_ref[...])
        log_a = (-C_GATE) * r * sp
        a = jnp.exp(log_a)
        mult_raw = jnp.sqrt(-_expm1(2.0 * log_a))
        first = (blk * tb + lax.broadcasted_iota(jnp.int32, (tb, W), 0)) == 0
        mult = jnp.where(first, 1.0, mult_raw)
        hl = hl_ref[...]
        ge, th = _gelu(u_ly)
        p = ge * hl
        rp = lax.rsqrt(_group_mean(p * p, avg_ref[...]) + EPS)
        pn = p * rp
        cv = u_c * u_v
        cv_ext[0:HALO, :] = projh[:, 3 * W:4 * W] * projh[:, 4 * W:5 * W] * has_prev
        cv_ext[HALO:HALO + tb, :] = cv
        cc = ws_ref[CONV_S - 1:CONV_S, :] * cv
        for k in range(CONV_S - 1):
            cc = cc + ws_ref[k:k + 1, :] * cv_ext[pl.ds(HALO - (CONV_S - 1) + k, tb), :]
        q = u_b * cc
        rq = lax.rsqrt(_group_mean(q * q, avg_ref[...]) + EPS)
        qn = q * rq

        y_lru_b = (pn * gl_ref[...]).astype(BF16)
        y_conv_b = (qn * gc_ref[...]).astype(BF16)
        dwout_v[0:W, :] += _dot_tn(y_lru_b, dmixed)
        dwout_v[W:2 * W, :] += _dot_tn(y_conv_b, dmixed)
        dyl = _dot_nt(dmixed, wout_v[0:W, :])
        dyc = _dot_nt(dmixed, wout_v[W:2 * W, :])

        dqn = dyc * gc_ref[...]
        dq = rq * (dqn - qn * _group_mean(dqn * qn, avg_ref[...]))
        du_b = dq * cc
        dcc = dq * u_b
        dcc_ext[0:tb, :] = dcc
        dcv = ws_ref[CONV_S - 1:CONV_S, :] * dcc
        for k in range(CONV_S - 1):
            dcv = dcv + ws_ref[k:k + 1, :] * dcc_ext[pl.ds(CONV_S - 1 - k, tb), :]
        dcc_ext[tb:tb + HALO, :] = dcc_ext[0:HALO, :]
        du_c = dcv * u_v
        du_v = dcv * u_c
        dws = [_colsum(dcc * cv_ext[pl.ds(HALO - (CONV_S - 1) + k, tb), :]) for k in range(CONV_S)]

        dpn = dyl * gl_ref[...]
        dp = rp * (dpn - pn * _group_mean(dpn * pn, avg_ref[...]))
        du_ly = dp * hl * _gelu_grad(u_ly, th)
        g_s[...] = dp * ge
        a_ext[0:tb, :] = a
        an_s[...] = a_ext[pl.ds(1, tb), :]
        _scan_groups(hb, an_s, g_s, dh_s, dcar, reverse=True)
        a_ext[tb:tb + HALO, :] = a_ext[0:HALO, :]
        dh = dh_s[...]
        hl_ext[0:HALO, :] = hlh_ref[...] * has_prev
        hl_ext[HALO:HALO + tb, :] = hl
        da = dh * hl_ext[pl.ds(HALO - 1, tb), :]
        dmult = dh * (ig * xl)
        dig = dh * (mult * xl)
        dxl = dh * (mult * ig)
        dlog = da * a - jnp.where(first, 0.0, dmult * (a * a) / mult_raw)
        dr = dlog * ((-C_GATE) * sp)
        dsp = _colsum(dlog * ((-C_GATE) * r))
        dga = dr * r * (1.0 - r)
        dgx = dig * ig * (1.0 - ig)
        dgab = dga.astype(BF16)
        dgxb = dgx.astype(BF16)
        dga_ref[...] += _dot_tn(xlb, dgab)
        dgx_ref[...] += _dot_tn(xlb, dgxb)
        dxl = dxl + _dot_nt(dgab, bda_ref[...]) + _dot_nt(dgxb, bdx_ref[...])
        dxl_ext[0:tb, :] = dxl
        du_lx = wl_ref[CONV_L - 1:CONV_L, :] * dxl
        for k in range(CONV_L - 1):
            du_lx = du_lx + wl_ref[k:k + 1, :] * dxl_ext[pl.ds(CONV_L - 1 - k, tb), :]
        dxl_ext[tb:tb + HALO, :] = dxl_ext[0:HALO, :]
        dwl = [_colsum(dxl * ulx_ext[pl.ds(HALO - (CONV_L - 1) + k, tb), :]) for k in range(CONV_L)]

        cat = lambda u, v: jnp.concatenate([u, v], axis=1)
        vec_ref[V_BL_BA:V_BL_BA + 1, :] += cat(_colsum(dxl), _colsum(dga))
        vec_ref[V_BX_SP:V_BX_SP + 1, :] += cat(_colsum(dgx), dsp)
        vec_ref[V_GL_GC:V_GL_GC + 1, :] += cat(_colsum(dyl * pn), _colsum(dyc * qn))
        vec_ref[V_WL01:V_WL01 + 1, :] += cat(dwl[0], dwl[1])
        vec_ref[V_WL23:V_WL23 + 1, :] += cat(dwl[2], dwl[3])
        vec_ref[V_WS01:V_WS01 + 1, :] += cat(dws[0], dws[1])
        vec_ref[V_WS2:V_WS2 + 1, 0:W] += dws[2]

        r1 = lax.rsqrt(_rowmean(x * x) + EPS)
        xn1 = x * r1
        hb16 = ((xn1 * g1_ref[...]) * (1.0 + scale1) + shift1).astype(BF16)
        dh_in = jnp.zeros((tb, D), F32)
        for k, du in enumerate((du_lx, du_ly, du_b, du_c, du_v)):
            dub = du.astype(BF16)
            dwint_v[k * W:(k + 1) * W, :] += _dot_tn(dub, hb16)
            dh_in = dh_in + _dot(dub, win_v[k * W:(k + 1) * W, :])
        vec_ref[V_SHIFT1:V_SHIFT1 + 1, :] += _colsum(dh_in)
        vec_ref[V_SCALE1:V_SCALE1 + 1, :] += _colsum(dh_in * xn1 * g1_ref[...])
        vec_ref[V_G1:V_G1 + 1, :] += _colsum(dh_in * (1.0 + scale1) * xn1)
        dxn1 = dh_in * g1_ref[...] * (1.0 + scale1)
        gx_ref[...] = dx1 + r1 * (dxn1 - xn1 * _rowmean(dxn1 * xn1))

        @pl.when(i == nb - 1)
        def _():
            pltpu.sync_copy(dwout_v, dwout_hbm)
            pltpu.sync_copy(dwint_v, dwint_hbm)

    rev = lambda cols: pl.BlockSpec((tb, cols), lambda i: (nb - 1 - i, 0))
    halo = lambda cols: pl.BlockSpec((HALO, cols), lambda i: (jnp.maximum((nb - 1 - i) * hb - 1, 0), 0))
    full = lambda a: pl.BlockSpec(a.shape, lambda i: (0,) * a.ndim)
    small = (modraw, adab, g1, g2, wl, bl, bda, bdx, ba, bxb, ap, ws, gl, gc, avg)
    ext = pltpu.VMEM((tb + HALO, W), F32)
    return pl.pallas_call(
        body,
        name="mixer_bwd",
        grid=(nb,),
        in_specs=[rev(D), rev(D), rev(D), rev(D), rev(D_IN), halo(D_IN), rev(W), halo(W)]
        + [full(a) for a in small] + [ANY],
        out_specs=[rev(D), pl.BlockSpec((V_ROWS, D), lambda i: (0, 0)), pl.BlockSpec((W, W), lambda i: (0, 0)),
                   pl.BlockSpec((W, W), lambda i: (0, 0)), ANY, ANY],
        out_shape=[jax.ShapeDtypeStruct((t_len, D), F32), jax.ShapeDtypeStruct((V_ROWS, D), F32),
                   jax.ShapeDtypeStruct((W, W), F32), jax.ShapeDtypeStruct((W, W), F32),
                   jax.ShapeDtypeStruct((D, D), F32), jax.ShapeDtypeStruct((D_IN, D), F32)],
        scratch_shapes=[pltpu.VMEM((D_IN, D), BF16), pltpu.VMEM((D, D), BF16), pltpu.VMEM((D, D), F32),
                        pltpu.VMEM((D_IN, D), F32), pltpu.SemaphoreType.DMA((2,)),
                        ext, ext, ext, ext, ext, ext, pltpu.VMEM((HALO, W), F32),
                        pltpu.VMEM((tb, W), F32), pltpu.VMEM((tb, W), F32), pltpu.VMEM((tb, W), F32)],
        compiler_params=pltpu.CompilerParams(dimension_semantics=("arbitrary",), vmem_limit_bytes=VMEM_LIMIT),
    )(x, mixed, dh2, dx2, proj, proj, hl, hl, *small, wpack)


def _block_diag(w):
    n, m, _ = w.shape
    eye = jnp.eye(n, dtype=w.dtype)
    return (w[:, :, None, :] * eye[:, None, :, None]).reshape(n * m, n * m)


def _diag_blocks(mat, n=8):
    m = mat.shape[0] // n
    return jnp.stack([mat[h * m:(h + 1) * m, h * m:(h + 1) * m] for h in range(n)])


def _pad_rows(a, rows):
    return jnp.pad(a, ((0, rows - a.shape[0]),) + ((0, 0),) * (a.ndim - 1))


def _position():
    return lax.axis_index("x"), lax.axis_index("y"), lax.axis_index("c")


def _linear(pos):
    return 4 * pos[0] + 2 * pos[1] + pos[2]


def _flip(pos, k):
    return tuple(1 - p if k & bit else p for p, bit in zip(pos, (4, 2, 1)))


def _exchange_all(make_copy, make_arrival):
    copies = [make_copy(k) for k in range(1, N_DEV)]
    for cp in copies:
        cp.start()
    for k in range(1, N_DEV):
        make_arrival(k).wait_recv()
    for cp in copies:
        cp.wait_send()


def _mod_exchange(msg, ada_w):
    cols = ada_w.shape[1]

    def body(msg_ref, adaw_ref, gath_ref, mod_ref, sendbuf, send_a, recv_a, send_b, recv_b):
        me = _position()
        me_lin = _linear(me)
        m = msg_ref[...]
        row = lax.broadcasted_iota(jnp.int32, m.shape, 0)
        gath_ref[me_lin] = jnp.where(row == 0, m * _sigmoid(m), m)

        def gather_copy(k, src_lin):
            return pltpu.make_async_remote_copy(
                src_ref=gath_ref.at[src_lin], dst_ref=gath_ref.at[src_lin], send_sem=send_a.at[k - 1],
                recv_sem=recv_a.at[k - 1], device_id=_flip(me, k), device_id_type=MESH)

        _exchange_all(lambda k: gather_copy(k, me_lin), lambda k: gather_copy(k, _linear(_flip(me, k))))

        sc_all = gath_ref[:, 0, :]
        scb = jnp.concatenate([sc_all, jnp.zeros_like(sc_all)], axis=0).astype(BF16)
        prod = _dot(scb, adaw_ref[...].astype(BF16))
        for b in range(N_DEV):
            sendbuf[b] = jnp.broadcast_to(prod[b:b + 1, :], (HALO, cols))
        mod_ref[me_lin] = sendbuf[me_lin]

        def row_copy(k, dst_lin):
            peer = _flip(me, k)
            return pltpu.make_async_remote_copy(
                src_ref=sendbuf.at[_linear(peer)], dst_ref=mod_ref.at[dst_lin], send_sem=send_b.at[k - 1],
                recv_sem=recv_b.at[k - 1], device_id=peer, device_id_type=MESH)

        _exchange_all(lambda k: row_copy(k, me_lin), lambda k: row_copy(k, _linear(_flip(me, k))))

    return pl.pallas_call(
        body,
        name="mod_exchange",
        in_specs=[WHOLE, WHOLE],
        out_specs=[WHOLE, WHOLE],
        out_shape=[jax.ShapeDtypeStruct((N_DEV, HALO, D), F32), jax.ShapeDtypeStruct((N_DEV, HALO, cols), F32)],
        scratch_shapes=[pltpu.VMEM((N_DEV, HALO, cols), F32)] + [pltpu.SemaphoreType.DMA((N_DEV - 1,))] * 4,
        compiler_params=pltpu.CompilerParams(vmem_limit_bytes=VMEM_LIMIT),
    )(msg, ada_w)


def _weight_gather(block):
    rows, cols = block.shape

    def body(x_ref, out_ref, send_sems, recv_sems, local_sem):
        x, y, c = _position()
        me, sibling = (x, y, c), (x, y, 1 - c)
        chips = [(1 - x, y), (x, 1 - y), (1 - x, 1 - y)]

        def copy(k, block_of, to, src=None):
            dst = out_ref.at[_linear(block_of)]
            return pltpu.make_async_remote_copy(
                src_ref=dst if src is None else src, dst_ref=dst, send_sem=send_sems.at[k], recv_sem=recv_sems.at[k],
                device_id=to, device_id_type=MESH)

        mine = pltpu.make_async_copy(x_ref, out_ref.at[_linear(me)], local_sem)
        mine.start()
        first = [copy(0, me, sibling, src=x_ref)]
        first += [copy(1 + j, me, (*chip, c), src=x_ref) for j, chip in enumerate(chips)]
        for cp in first:
            cp.start()
        passed = [copy(4 + j, (*chip, c), sibling) for j, chip in enumerate(chips)]
        for j, chip in enumerate(chips):
            copy(1 + j, (*chip, c), me).wait_recv()
            passed[j].start()
        copy(0, sibling, me).wait_recv()
        for j, chip in enumerate(chips):
            copy(4 + j, (*chip, 1 - c), me).wait_recv()
        for cp in first + passed:
            cp.wait_send()
        mine.wait()

    return pl.pallas_call(
        body,
        name="weight_gather",
        in_specs=[ANY],
        out_specs=ANY,
        out_shape=jax.ShapeDtypeStruct((N_DEV, rows, cols), block.dtype),
        scratch_shapes=[pltpu.SemaphoreType.DMA((7,)), pltpu.SemaphoreType.DMA((7,)), pltpu.SemaphoreType.DMA],
    )(block)


def _grads_to_sibling(arrs):
    n = len(arrs)

    def body(*refs):
        srcs, dsts = refs[:n], refs[n:2 * n]
        send_sems, recv_sems = refs[2 * n:]
        x, y, c = _position()
        copies = []
        for a in range(n):
            for k in range(4):
                copies.append(pltpu.make_async_remote_copy(
                    src_ref=srcs[a].at[k, 1 - c], dst_ref=dsts[a].at[k], send_sem=send_sems.at[4 * a + k],
                    recv_sem=recv_sems.at[4 * a + k], device_id=(x, y, 1 - c), device_id_type=MESH))
        for cp in copies:
            cp.start()
        for cp in copies:
            cp.wait_recv()
        for cp in copies:
            cp.wait_send()

    return pl.pallas_call(
        body,
        name="grads_to_sibling",
        in_specs=[ANY] * n,
        out_specs=[ANY] * n,
        out_shape=[jax.ShapeDtypeStruct((4,) + a.shape[2:], a.dtype) for a in arrs],
        scratch_shapes=[pltpu.SemaphoreType.DMA((4 * n,)), pltpu.SemaphoreType.DMA((4 * n,))],
    )(*arrs)


def _grads_to_chips(arrs):
    n = len(arrs)

    def body(*refs):
        srcs, dsts = refs[:n], refs[n:2 * n]
        send_sems, recv_sems = refs[2 * n:]
        me = _position()
        copies = []
        for a in range(n):
            for j, k in enumerate((4, 2, 6)):
                peer = _flip(me, k)
                copies.append(pltpu.make_async_remote_copy(
                    src_ref=srcs[a].at[2 * peer[0] + peer[1]], dst_ref=dsts[a].at[j], send_sem=send_sems.at[3 * a + j],
                    recv_sem=recv_sems.at[3 * a + j], device_id=peer, device_id_type=MESH))
        for cp in copies:
            cp.start()
        for cp in copies:
            cp.wait_recv()
        for cp in copies:
            cp.wait_send()

    return pl.pallas_call(
        body,
        name="grads_to_chips",
        in_specs=[ANY] * n,
        out_specs=[ANY] * n,
        out_shape=[jax.ShapeDtypeStruct((3,) + a.shape[1:], a.dtype) for a in arrs],
        scratch_shapes=[pltpu.SemaphoreType.DMA((3 * n,)), pltpu.SemaphoreType.DMA((3 * n,))],
    )(*arrs)


def _row_block(rows):
    return 256 if rows % 256 == 0 else rows // 2


def _pair_sum(pos, mine, recv):
    _, _, rows, cols = mine.shape
    rb = _row_block(rows)

    def body(pos_ref, mine_ref, recv_ref, out_ref):
        out_ref[0] = (mine_ref[0, 0] + recv_ref[0]).astype(BF16)

    return pl.pallas_call(
        body,
        name="grad_pair_sum",
        grid_spec=pltpu.PrefetchScalarGridSpec(
            num_scalar_prefetch=1, grid=(4, rows // rb),
            in_specs=[pl.BlockSpec((1, 1, rb, cols), lambda k, r, pos: (k, pos[0], r, 0)),
                      pl.BlockSpec((1, rb, cols), lambda k, r, pos: (k, r, 0))],
            out_specs=pl.BlockSpec((1, rb, cols), lambda k, r, pos: (k, r, 0))),
        out_shape=jax.ShapeDtypeStruct((4, rows, cols), BF16),
        compiler_params=pltpu.CompilerParams(dimension_semantics=("arbitrary", "arbitrary")),
    )(pos, mine, recv)


def _final_sum(pos, mine, recv, chips):
    _, _, rows, cols = mine.shape
    rb = _row_block(rows)

    def body(pos_ref, mine_ref, recv_ref, chips_ref, out_ref):
        g = mine_ref[0, 0] + recv_ref[0]
        for j in range(3):
            g = g + chips_ref[j].astype(F32)
        out_ref[...] = g

    return pl.pallas_call(
        body,
        name="grad_final_sum",
        grid_spec=pltpu.PrefetchScalarGridSpec(
            num_scalar_prefetch=1, grid=(rows // rb,),
            in_specs=[pl.BlockSpec((1, 1, rb, cols), lambda r, pos: (pos[1], pos[0], r, 0)),
                      pl.BlockSpec((1, rb, cols), lambda r, pos: (pos[1], r, 0)),
                      pl.BlockSpec((3, rb, cols), lambda r, pos: (0, r, 0))],
            out_specs=pl.BlockSpec((rb, cols), lambda r, pos: (r, 0))),
        out_shape=jax.ShapeDtypeStruct((rows, cols), F32),
        compiler_params=pltpu.CompilerParams(dimension_semantics=("arbitrary",)),
    )(pos, mine, recv, chips)


GB_ROW = 24
MSG_ROWS = 32


def _grad_exchange(gmod8, sc_t, msg_vec, msg_gate):
    cols = gmod8.shape[1]

    def body(gmod_ref, sct_ref, vec_ref, gate_ref, gadaw_ref, sumv_ref, sumg_ref, gb_ref,
             sendbuf, grecv, bufv, bufg, send_a, recv_a, send_v, recv_v, send_g, recv_g):
        me = _position()
        me_lin = _linear(me)
        gm = gmod_ref[...]
        for b in range(N_DEV):
            sendbuf[b] = jnp.broadcast_to(gm[b:b + 1, :], (HALO, cols))
        grecv[me_lin] = sendbuf[me_lin]

        def row_copy(k, dst_lin):
            peer = _flip(me, k)
            return pltpu.make_async_remote_copy(
                src_ref=sendbuf.at[_linear(peer)], dst_ref=grecv.at[dst_lin], send_sem=send_a.at[k - 1],
                recv_sem=recv_a.at[k - 1], device_id=peer, device_id_type=MESH)

        _exchange_all(lambda k: row_copy(k, me_lin), lambda k: row_copy(k, _linear(_flip(me, k))))

        g_all = grecv[:, 0, :]
        g_pad = jnp.concatenate([g_all, jnp.zeros((sct_ref.shape[1] - N_DEV, cols), F32)], axis=0).astype(BF16)
        gadaw_ref[...] = _dot(sct_ref[...], g_pad)
        bufv[me_lin] = vec_ref[...]
        bufv[me_lin, GB_ROW:GB_ROW + 1, 0:cols] = _colsum(g_all)
        bufg[me_lin] = gate_ref[...]

        def gather_copy(buf, sends, recvs, k, src_lin):
            return pltpu.make_async_remote_copy(
                src_ref=buf.at[src_lin], dst_ref=buf.at[src_lin], send_sem=sends.at[k - 1], recv_sem=recvs.at[k - 1],
                device_id=_flip(me, k), device_id_type=MESH)

        for buf, sends, recvs in ((bufv, send_v, recv_v), (bufg, send_g, recv_g)):
            _exchange_all(lambda k: gather_copy(buf, sends, recvs, k, me_lin),
                          lambda k: gather_copy(buf, sends, recvs, k, _linear(_flip(me, k))))

        sv = bufv[0]
        sg = bufg[0]
        for d in range(1, N_DEV):
            sv = sv + bufv[d]
            sg = sg + bufg[d]
        sumv_ref[...] = sv
        sumg_ref[...] = sg
        gb_ref[...] = bufv[:, GB_ROW, :]

    return pl.pallas_call(
        body,
        name="grad_exchange",
        in_specs=[WHOLE] * 4,
        out_specs=[WHOLE] * 4,
        out_shape=[jax.ShapeDtypeStruct((D, cols), F32), jax.ShapeDtypeStruct(msg_vec.shape, F32),
                   jax.ShapeDtypeStruct(msg_gate.shape, F32), jax.ShapeDtypeStruct((N_DEV, D), F32)],
        scratch_shapes=[pltpu.VMEM((N_DEV, HALO, cols), F32), pltpu.VMEM((N_DEV, HALO, cols), F32),
                        pltpu.VMEM((N_DEV,) + msg_vec.shape, F32), pltpu.VMEM((N_DEV,) + msg_gate.shape, F32)]
        + [pltpu.SemaphoreType.DMA((N_DEV - 1,))] * 6,
        compiler_params=pltpu.CompilerParams(vmem_limit_bytes=VMEM_LIMIT),
    )(gmod8, sc_t, msg_vec, msg_gate)


def _adamw_math(w, g, m, v):
    m = ADAM_B1 * m + (1.0 - ADAM_B1) * g
    v = ADAM_B2 * v + (1.0 - ADAM_B2) * (g * g)
    m_hat = m / (1.0 - ADAM_B1 ** ADAM_STEP)
    v_hat = v / (1.0 - ADAM_B2 ** ADAM_STEP)
    delta = -ADAM_LR * (m_hat / (jnp.sqrt(v_hat) + ADAM_EPS) + ADAM_WD * w)
    return delta, m, v


def _adamw(name, w, g, m, v):
    rows, cols = w.shape
    rb = 256 if rows % 256 == 0 else rows

    def body(w_ref, g_ref, m_ref, v_ref, d_ref, mo_ref, vo_ref):
        d_ref[...], mo_ref[...], vo_ref[...] = _adamw_math(w_ref[...], g_ref[...], m_ref[...], v_ref[...])

    spec = pl.BlockSpec((rb, cols), lambda r: (r, 0))
    return pl.pallas_call(
        body,
        name="adamw_" + name,
        grid=(rows // rb,),
        in_specs=[spec] * 4,
        out_specs=[spec] * 3,
        out_shape=[jax.ShapeDtypeStruct((rows, cols), F32)] * 3,
        compiler_params=pltpu.CompilerParams(dimension_semantics=("arbitrary",)),
    )(w, g, m, v)


def _adamw_small(ws, gs, ms, vs, sigmoid_scaled):
    n = len(ws)

    def body(*refs):
        w_refs, g_refs, m_refs, v_refs = (refs[i * n:(i + 1) * n] for i in range(4))
        outs = refs[4 * n:]
        for i in range(n):
            w = w_refs[i][...]
            g = g_refs[i][...]
            if sigmoid_scaled[i]:
                g = g * _sigmoid(w)
            delta, m, v = _adamw_math(w, g, m_refs[i][...], v_refs[i][...])
            outs[4 * i][...] = g
            outs[4 * i + 1][...] = delta
            outs[4 * i + 2][...] = m
            outs[4 * i + 3][...] = v

    shapes = [jax.ShapeDtypeStruct(w.shape, F32) for w in ws for _ in range(4)]
    outs = pl.pallas_call(
        body,
        name="adamw_small",
        in_specs=[WHOLE] * (4 * n),
        out_specs=[WHOLE] * (4 * n),
        out_shape=shapes,
    )(*ws, *gs, *ms, *vs)
    return [outs[4 * i:4 * i + 4] for i in range(n)]


def _local_step(x, tgt, modraw, adab, g1, g2, gf, wl, bl, gate_a_w, ba, gate_x_w, bxb, ap, ws, gl, gc, wpack):
    bda = _block_diag(gate_a_w).astype(BF16)
    bdx = _block_diag(gate_x_w).astype(BF16)
    avg = _block_diag(jnp.full((8, W // 8, W // 8), 8.0 / W, F32)).astype(BF16)
    wl8 = _pad_rows(wl, HALO)
    ws8 = _pad_rows(ws, HALO)
    proj, hl, mixed = _mixer_fwd(x, modraw, adab, g1, wl8, bl, bda, bdx, ba, bxb, ap, ws8, gl, gc, avg, wpack)
    h2, f, dx2, dz, vec2, loss = _mlp_fwd(x, mixed, tgt, modraw, adab, g2, gf, wpack)
    dh2, dw1t, dw2 = _mlp_bwd(h2, f, dz, wpack)
    gx, vec, dga, dgx, dwout, dwint = _mixer_bwd(x, mixed, dh2, dx2, proj, hl, modraw, adab, g1, g2, wl8, bl, bda, bdx,
                                                 ba, bxb, ap, ws8, gl, gc, avg, wpack)
    return loss, gx, vec, vec2, dga, dgx, dwout, dwint, dw1t, dw2


_WEIGHT_NAMES = ("ada_w", "ada_b", "norm1_g", "w_in", "lru_conv_w", "lru_conv_b", "gate_a_w", "gate_a_b", "gate_x_w",
                 "gate_x_b", "a_param", "short_conv_w", "lru_out_g", "conv_out_g", "w_out", "norm2_g", "w_mlp1",
                 "w_mlp2", "final_g")
_BIG = ("ada_w", "w_in", "w_out", "w_mlp1", "w_mlp2")


def kernel(x, c, ada_w, ada_b, norm1_g, w_in, lru_conv_w, lru_conv_b, gate_a_w, gate_a_b, gate_x_w, gate_x_b, a_param, short_conv_w, lru_out_g, conv_out_g, w_out, norm2_g, w_mlp1, w_mlp2, final_g, loss_target, m_ada_w, m_ada_b, m_norm1_g, m_w_in, m_lru_conv_w, m_lru_conv_b, m_gate_a_w, m_gate_a_b, m_gate_x_w, m_gate_x_b, m_a_param, m_short_conv_w, m_lru_out_g, m_conv_out_g, m_w_out, m_norm2_g, m_w_mlp1, m_w_mlp2, m_final_g, v_ada_w, v_ada_b, v_norm1_g, v_w_in, v_lru_conv_w, v_lru_conv_b, v_gate_a_w, v_gate_a_b, v_gate_x_w, v_gate_x_b, v_a_param, v_short_conv_w, v_lru_out_g, v_conv_out_g, v_w_out, v_norm2_g, v_w_mlp1, v_w_mlp2, v_final_g):
    given = dict(locals())
    weights = {n: given[n] for n in _WEIGHT_NAMES}
    xi, yi, ci = _position()
    me_lin = _linear((xi, yi, ci))
    hd = W // N_DEV

    msg = (jnp.pad(c, ((0, HALO - 1), (0, 0)))
           + jnp.pad(lru_conv_w[0], ((1, HALO - 1 - CONV_L), (0, D - hd)))
           + jnp.pad(short_conv_w[0], ((1 + CONV_L, 0), (0, D - hd))))
    gath, mod_all = _mod_exchange(msg, ada_w[0])
    sc_all = gath[:, 0, :]
    wl = jnp.transpose(gath[:, 1:1 + CONV_L, :hd], (1, 0, 2)).reshape(CONV_L, W)
    ws = jnp.transpose(gath[:, 1 + CONV_L:HALO, :hd], (1, 0, 2)).reshape(CONV_S, W)
    modraw = _pad_rows(mod_all[:, 0, :].reshape(6, D), HALO)
    adab = _pad_rows(ada_b.reshape(6, D), HALO)

    block = jnp.concatenate([w_mlp1[0].T, w_mlp2[0], w_out[0], w_in[0].T], axis=0).astype(BF16)
    wpack = _weight_gather(block)

    loss8, gx, vec, vec2, dga, dgx, dwout, dwint, dw1t, dw2 = _local_step(
        x[0], loss_target[0], modraw, adab, norm1_g, norm2_g, final_g.reshape(1, D), wl, lru_conv_b, gate_a_w[0],
        gate_a_b, gate_x_w[0], gate_x_b, a_param, ws, lru_out_g, conv_out_g, wpack)
    loss = lax.psum(loss8[0, 0], ("x", "y", "c"))

    parts = [g.reshape(4, 2, g.shape[0] // N_DEV, D) for g in
             (dw1t.reshape(D_FF, D), dw2.reshape(D_FF, D), dwout, dwint)]
    from_sibling = _grads_to_sibling(parts)
    pos = jnp.stack([ci, 2 * xi + yi]).astype(jnp.int32)
    chip_sums = [_pair_sum(pos, p, r) for p, r in zip(parts, from_sibling)]
    from_chips = _grads_to_chips(chip_sums)
    g_w1t, g_w2, g_wout, g_wint = (_final_sum(pos, p, r, q) for p, r, q in zip(parts, from_sibling, from_chips))

    gmod8 = (jnp.pad(vec[0:5], ((0, 1), (0, 0))) + jnp.pad(vec2[0:1], ((5, 0), (0, 0)))).reshape(N_DEV, 6 * D // N_DEV)
    sc_t = jnp.pad(sc_all.T, ((0, 0), (0, 128 - N_DEV))).astype(BF16)
    msg_vec = jnp.concatenate([vec, vec2, jnp.zeros((MSG_ROWS - V_ROWS - 8, D), F32)], axis=0)
    msg_gate = jnp.stack([_diag_blocks(dga), _diag_blocks(dgx)]).reshape(W, 128)
    g_adaw, sum_vec, sum_gate, gb = _grad_exchange(gmod8, sc_t, msg_vec, msg_gate)
    sum_gate = sum_gate.reshape(2, W, W // 8)
    lo, hi = slice(0, W), slice(W, 2 * W)
    wl_full = sum_vec[V_WL01:V_WL23 + 1].reshape(CONV_L, W)
    ws_full = sum_vec[V_WS01:V_WS2 + 1].reshape(CONV_S + 1, W)[:CONV_S]
    row = lambda r, cols: sum_vec[r:r + 1, cols]
    small_grads = {
        "ada_b": gb[:, :6 * D // N_DEV].reshape(1, 6 * D),
        "norm1_g": row(V_G1, slice(0, D)),
        "lru_conv_w": lax.dynamic_slice(wl_full, (0, me_lin * hd), (CONV_L, hd)),
        "lru_conv_b": row(V_BL_BA, lo),
        "gate_a_w": sum_gate[0],
        "gate_a_b": row(V_BL_BA, hi),
        "gate_x_w": sum_gate[1],
        "gate_x_b": row(V_BX_SP, lo),
        "a_param": row(V_BX_SP, hi),
        "short_conv_w": lax.dynamic_slice(ws_full, (0, me_lin * hd), (CONV_S, hd)),
        "lru_out_g": row(V_GL_GC, lo),
        "conv_out_g": row(V_GL_GC, hi),
        "norm2_g": row(V_G2, slice(0, D)),
        "final_g": sum_vec[V_ROWS + 1:V_ROWS + 2, :],
    }
    names = list(small_grads)
    as2d = lambda a, n: a.reshape(small_grads[n].shape)
    small = _adamw_small([as2d(weights[n], n) for n in names], [small_grads[n] for n in names],
                         [as2d(given["m_" + n], n) for n in names], [as2d(given["v_" + n], n) for n in names],
                         [n == "a_param" for n in names])
    result = {n: tuple(o.reshape(weights[n].shape) for o in outs) for n, outs in zip(names, small)}

    big_grads = {"ada_w": g_adaw, "w_in": g_wint.T, "w_out": g_wout, "w_mlp1": g_w1t.T, "w_mlp2": g_w2}
    for n in _BIG:
        g = big_grads[n]
        delta, new_m, new_v = _adamw(n, weights[n][0], g, given["m_" + n][0], given["v_" + n][0])
        result[n] = tuple(o[None] for o in (g, delta, new_m, new_v))

    return (loss, gx[None], *[result[n][0] for n in _WEIGHT_NAMES], *[result[n][1] for n in _WEIGHT_NAMES],
            *[result[n][2] for n in _WEIGHT_NAMES], *[result[n][3] for n in _WEIGHT_NAMES])
```

```python
import functools

import jax
import jax.numpy as jnp
from jax import lax
from jax.experimental import pallas as pl
from jax.experimental.pallas import tpu as pltpu

F32 = jnp.float32
BF16 = jnp.bfloat16
MESH = pl.DeviceIdType.MESH

N_DEV = 8
D = 1024
W = 512
D_IN = 5 * W
D_FF = 4096
FF_BLK = D_FF // N_DEV
EPS = 1e-6
C_GATE = 8.0
CONV_L = 4
CONV_S = 3
HALO = 8

ROWS_W1T, ROWS_W2, ROWS_WOUT, ROWS_WIN = FF_BLK, FF_BLK, D // N_DEV, D_IN // N_DEV
OFF_WOUT = 0
OFF_WIN = OFF_WOUT + ROWS_WOUT
MIX_ROWS = OFF_WIN + ROWS_WIN
OFF_W1T = 0
OFF_W2 = OFF_W1T + ROWS_W1T
MLP_ROWS = OFF_W2 + ROWS_W2
CHIP_FLIPS = (4, 2, 6)

ADAM_LR = 0.001
ADAM_B1 = 0.9
ADAM_B2 = 0.999
ADAM_EPS = 1e-08
ADAM_WD = 0.01
ADAM_STEP = 10

VMEM_LIMIT = 56 * 1024 * 1024

TB_MIX = 256
TB_MIXB = 128
TB_MLP = 256
TB_MLPB = 512

ANY = pl.BlockSpec(memory_space=pl.ANY)
WHOLE = pl.BlockSpec(memory_space=pltpu.VMEM)


def _dot(a, b):
    return jnp.dot(a, b, preferred_element_type=F32)


def _dot_nt(a, b):
    return lax.dot_general(a, b, (((1,), (1,)), ((), ())), preferred_element_type=F32)


def _dot_tn(a, b):
    return lax.dot_general(a, b, (((0,), (0,)), ((), ())), preferred_element_type=F32)


def _sigmoid(v):
    return 1.0 / (1.0 + jnp.exp(-v))


def _softplus(v):
    t = jnp.exp(-jnp.abs(v))
    small = t * (1.0 - t * (0.5 - t * (1.0 / 3.0)))
    return jnp.maximum(v, 0.0) + jnp.where(t < 1e-2, small, jnp.log(1.0 + t))


def _expm1(v):
    series = v * (1.0 + v * (0.5 + v * (1.0 / 6.0 + v * (1.0 / 24.0))))
    return jnp.where(jnp.abs(v) < 0.05, series, jnp.exp(v) - 1.0)


_GELU_K = 0.7978845608028654
_GELU_C = 0.044715


def _gelu(u):
    th = jnp.tanh(_GELU_K * (u + _GELU_C * u * u * u))
    return 0.5 * u * (1.0 + th), th


def _gelu_grad(u, th):
    return 0.5 * (1.0 + th) + 0.5 * u * (1.0 - th * th) * _GELU_K * (1.0 + 3.0 * _GELU_C * u * u)


def _group_mean(v, avg):
    hi = v.astype(BF16)
    lo = (v - hi.astype(F32)).astype(BF16)
    return _dot(hi, avg) + _dot(lo, avg)


def _colsum(v):
    return jnp.sum(v, axis=0, keepdims=True)


def _rowmean(v):
    return jnp.mean(v, axis=-1, keepdims=True)


def _load_packed(wpack_hbm, off, rows, dst, sem):
    copies = [
        pltpu.make_async_copy(wpack_hbm.at[d, pl.ds(off, rows), :], dst.at[pl.ds(d * rows, rows), :], sem)
        for d in range(N_DEV)
    ]
    for cp in copies:
        cp.start()
    return copies


def _scan_groups(n_groups, a_ref, b_ref, out_ref, carry_ref, reverse):
    row = lax.broadcasted_iota(jnp.int32, (HALO, W), 0)

    def step(k, carry):
        g = (n_groups - 1 - k) if reverse else k
        rows = pl.ds(pl.multiple_of(g * HALO, HALO), HALO)
        a = a_ref[rows, :]
        b = b_ref[rows, :]
        for s in (1, 2, 4):
            if reverse:
                keep = row < HALO - s
                sh = HALO - s
            else:
                keep = row >= s
                sh = s
            a_sh = pltpu.roll(a, sh, axis=0)
            b_sh = pltpu.roll(b, sh, axis=0)
            b = jnp.where(keep, a * b_sh + b, b)
            a = jnp.where(keep, a * a_sh, a)
        h = b + a * carry
        out_ref[rows, :] = h
        edge = h[0:1, :] if reverse else h[HALO - 1:HALO, :]
        return jnp.broadcast_to(edge, (HALO, W))

    carry_ref[...] = lax.fori_loop(0, n_groups, step, carry_ref[...])


def _chip_gather_copies(block_hbm, out_hbm, send_sems, recv_sems):
    me = _position()

    def copy(j, k, slot_of):
        return pltpu.make_async_remote_copy(
            src_ref=block_hbm, dst_ref=out_hbm.at[_linear(slot_of)], send_sem=send_sems.at[j], recv_sem=recv_sems.at[j],
            device_id=_flip(me, k), device_id_type=MESH)

    sends = [copy(j, k, me) for j, k in enumerate(CHIP_FLIPS)]
    arrivals = [copy(j, k, _flip(me, k)) for j, k in enumerate(CHIP_FLIPS)]
    return sends, arrivals


def _mixer_fwd(x, modraw, adab, g1, wl, bl, bda, bdx, ba, bxb, ap, ws, gl, gc, avg, wpack, mlp_block):
    t_len = x.shape[0]
    tb = TB_MIX
    nb = t_len // tb

    def body(x_ref, modraw_ref, adab_ref, g1_ref, wl_ref, bl_ref, bda_ref, bdx_ref, ba_ref, bxb_ref, ap_ref,
             ws_ref, gl_ref, gc_ref, avg_ref, wpack_hbm, block_hbm, proj_ref, hl_ref, mixed_ref, wmlp_hbm,
             win_v, wout_v, sem, ulx_ext, cv_ext, hcar, a_s, b_s, send_sems, recv_sems, local_sem):
        i = pl.program_id(0)
        own = pltpu.make_async_copy(block_hbm, wmlp_hbm.at[_linear(_position())], local_sem)
        sends, arrivals = _chip_gather_copies(block_hbm, wmlp_hbm, send_sems, recv_sems)

        @pl.when(i == 0)
        def _():
            own.start()
            for cp in sends:
                cp.start()
            cps = _load_packed(wpack_hbm, OFF_WIN, ROWS_WIN, win_v, sem.at[0])
            cps += _load_packed(wpack_hbm, OFF_WOUT, ROWS_WOUT, wout_v, sem.at[1])
            ulx_ext[0:HALO, :] = jnp.zeros((HALO, W), F32)
            cv_ext[0:HALO, :] = jnp.zeros((HALO, W), F32)
            hcar[...] = jnp.zeros((HALO, W), F32)
            for cp in cps:
                cp.wait()

        mod = modraw_ref[...] + adab_ref[...]
        shift1, scale1, gate1 = mod[0:1], mod[1:2], mod[2:3]
        x = x_ref[...]
        r1 = lax.rsqrt(_rowmean(x * x) + EPS)
        h = (x * r1 * g1_ref[...]) * (1.0 + scale1) + shift1
        proj = _dot_nt(h.astype(BF16), win_v[...])
        proj_ref[...] = proj
        u_lx, u_ly, u_b, u_c, u_v = (proj[:, k * W:(k + 1) * W] for k in range(5))

        ulx_ext[HALO:HALO + tb, :] = u_lx
        xl = bl_ref[...] + wl_ref[CONV_L - 1:CONV_L, :] * u_lx
        for k in range(CONV_L - 1):
            xl = xl + wl_ref[k:k + 1, :] * ulx_ext[pl.ds(HALO - (CONV_L - 1) + k, tb), :]
        ulx_ext[0:HALO, :] = ulx_ext[tb:tb + HALO, :]
        xlb = xl.astype(BF16)
        r = _sigmoid(_dot(xlb, bda_ref[...]) + ba_ref[...])
        ig = _sigmoid(_dot(xlb, bdx_ref[...]) + bxb_ref[...])
        log_a = (-C_GATE) * r * _softplus(ap_ref[...])
        a = jnp.exp(log_a)
        mult = jnp.sqrt(-_expm1(2.0 * log_a))
        grow = i * tb + lax.broadcasted_iota(jnp.int32, (tb, W), 0)
        mult = jnp.where(grow == 0, 1.0, mult)
        a_s[...] = a
        b_s[...] = mult * (ig * xl)
        _scan_groups(tb // HALO, a_s, b_s, hl_ref, hcar, reverse=False)
        hl = hl_ref[...]
        ge, _ = _gelu(u_ly)
        p = ge * hl
        y_lru = p * lax.rsqrt(_group_mean(p * p, avg_ref[...]) + EPS) * gl_ref[...]

        cv = u_c * u_v
        cv_ext[HALO:HALO + tb, :] = cv
        cc = ws_ref[CONV_S - 1:CONV_S, :] * cv
        for k in range(CONV_S - 1):
            cc = cc + ws_ref[k:k + 1, :] * cv_ext[pl.ds(HALO - (CONV_S - 1) + k, tb), :]
        cv_ext[0:HALO, :] = cv_ext[tb:tb + HALO, :]
        q = u_b * cc
        y_conv = q * lax.rsqrt(_group_mean(q * q, avg_ref[...]) + EPS) * gc_ref[...]

        mixed_ref[...] = (_dot(y_lru.astype(BF16), wout_v[0:W, :]) + _dot(y_conv.astype(BF16), wout_v[W:2 * W, :]))

        @pl.when(i == nb - 1)
        def _():
            for cp in arrivals:
                cp.wait_recv()
            for cp in sends:
                cp.wait_send()
            own.wait()

    tok = lambda cols: pl.BlockSpec((tb, cols), lambda i: (i, 0))
    full = lambda a: pl.BlockSpec(a.shape, lambda i: (0,) * a.ndim)
    small = (modraw, adab, g1, wl, bl, bda, bdx, ba, bxb, ap, ws, gl, gc, avg)
    n_chips = len(CHIP_FLIPS)
    return pl.pallas_call(
        body,
        name="mixer_fwd",
        grid=(nb,),
        in_specs=[tok(D)] + [full(a) for a in small] + [ANY, ANY],
        out_specs=[tok(D_IN), tok(W), tok(D), ANY],
        out_shape=[jax.ShapeDtypeStruct((t_len, D_IN), F32), jax.ShapeDtypeStruct((t_len, W), F32),
                   jax.ShapeDtypeStruct((t_len, D), F32), jax.ShapeDtypeStruct((N_DEV,) + mlp_block.shape, BF16)],
        scratch_shapes=[pltpu.VMEM((D_IN, D), BF16), pltpu.VMEM((D, D), BF16), pltpu.SemaphoreType.DMA((2,)),
                        pltpu.VMEM((tb + HALO, W), F32), pltpu.VMEM((tb + HALO, W), F32), pltpu.VMEM((HALO, W), F32),
                        pltpu.VMEM((tb, W), F32), pltpu.VMEM((tb, W), F32),
                        pltpu.SemaphoreType.DMA((n_chips,)), pltpu.SemaphoreType.DMA((n_chips,)), pltpu.SemaphoreType.DMA],
        compiler_params=pltpu.CompilerParams(dimension_semantics=("arbitrary",), vmem_limit_bytes=VMEM_LIMIT),
    )(x, *small, wpack, mlp_block)


def _sibling_forward(wmlp):
    def body(in_hbm, out_hbm, send_sems, recv_sems):
        x, y, c = _position()
        copies, arrivals = [], []
        for j, k in enumerate((0,) + CHIP_FLIPS):
            mine = out_hbm.at[_linear(_flip((x, y, c), k))]
            theirs = out_hbm.at[_linear(_flip((x, y, 1 - c), k))]
            copies.append(pltpu.make_async_remote_copy(
                src_ref=mine, dst_ref=mine, send_sem=send_sems.at[j], recv_sem=recv_sems.at[j],
                device_id=(x, y, 1 - c), device_id_type=MESH))
            arrivals.append(pltpu.make_async_remote_copy(
                src_ref=theirs, dst_ref=theirs, send_sem=send_sems.at[j], recv_sem=recv_sems.at[j],
                device_id=(x, y, 1 - c), device_id_type=MESH))
        for cp in copies:
            cp.start()
        for cp in arrivals:
            cp.wait_recv()
        for cp in copies:
            cp.wait_send()

    return pl.pallas_call(
        body,
        name="sibling_forward",
        in_specs=[ANY],
        out_specs=ANY,
        out_shape=jax.ShapeDtypeStruct(wmlp.shape, wmlp.dtype),
        input_output_aliases={0: 0},
        scratch_shapes=[pltpu.SemaphoreType.DMA((4,)), pltpu.SemaphoreType.DMA((4,))],
    )(wmlp)


def _mlp_fwd(x, mixed, tgt, modraw, adab, g2, gf, wpack):
    t_len = x.shape[0]
    tb = TB_MLP
    nb = t_len // tb

    def body(x_ref, mixed_ref, tgt_ref, modraw_ref, adab_ref, g2_ref, gf_ref, wpack_hbm,
             h2_ref, f_ref, dx2_ref, dz_ref, vec_ref, loss_ref, w1t_v, w2_v, sem):
        i = pl.program_id(0)

        @pl.when(i == 0)
        def _():
            cps = _load_packed(wpack_hbm, OFF_W1T, ROWS_W1T, w1t_v, sem.at[0])
            cps += _load_packed(wpack_hbm, OFF_W2, ROWS_W2, w2_v, sem.at[1])
            vec_ref[...] = jnp.zeros(vec_ref.shape, F32)
            loss_ref[...] = jnp.zeros(loss_ref.shape, F32)
            for cp in cps:
                cp.wait()

        mod = modraw_ref[...] + adab_ref[...]
        gate1, shift2, scale2, gate2 = mod[2:3], mod[3:4], mod[4:5], mod[5:6]
        x1 = x_ref[...] + gate1 * mixed_ref[...]
        r2 = lax.rsqrt(_rowmean(x1 * x1) + EPS)
        h2 = (x1 * r2 * g2_ref[...]) * (1.0 + scale2) + shift2
        h2b = h2.astype(BF16)
        h2_ref[...] = h2b
        z = jnp.zeros((tb, D), F32)
        for j in range(N_DEV):
            cols = slice(j * FF_BLK, (j + 1) * FF_BLK)
            fj = _dot_nt(h2b, w1t_v[cols, :])
            f_ref[:, cols] = fj
            rf = jnp.maximum(fj, 0.0)
            z = z + _dot((rf * rf).astype(BF16), w2_v[cols, :])
        x2 = x1 + gate2 * z
        r3 = lax.rsqrt(_rowmean(x2 * x2) + EPS)
        xn3 = x2 * r3
        diff = xn3 * gf_ref[...] - tgt_ref[...]
        sq = _colsum(diff * diff)
        loss_ref[...] += jnp.broadcast_to(jnp.sum(sq, axis=1, keepdims=True) * (0.5 / D), loss_ref.shape)
        dy = diff * (1.0 / D)
        dyn = dy * gf_ref[...]
        dx2 = r3 * (dyn - xn3 * _rowmean(dyn * xn3))
        dx2_ref[...] = dx2
        dz_ref[...] = (gate2 * dx2).astype(BF16)
        vec_ref[0:1, :] += _colsum(dx2 * z)
        vec_ref[1:2, :] += _colsum(dy * xn3)

    tok = lambda cols: pl.BlockSpec((tb, cols), lambda i: (i, 0))
    full = lambda a: pl.BlockSpec(a.shape, lambda i: (0,) * a.ndim)
    small = (modraw, adab, g2, gf)
    return pl.pallas_call(
        body,
        name="mlp_fwd",
        grid=(nb,),
        in_specs=[tok(D), tok(D), tok(D)] + [full(a) for a in small] + [ANY],
        out_specs=[tok(D), tok(D_FF), tok(D), tok(D), pl.BlockSpec((8, D), lambda i: (0, 0)),
                   pl.BlockSpec((8, 128), lambda i: (0, 0))],
        out_shape=[jax.ShapeDtypeStruct((t_len, D), BF16), jax.ShapeDtypeStruct((t_len, D_FF), F32),
                   jax.ShapeDtypeStruct((t_len, D), F32), jax.ShapeDtypeStruct((t_len, D), BF16),
                   jax.ShapeDtypeStruct((8, D), F32), jax.ShapeDtypeStruct((8, 128), F32)],
        scratch_shapes=[pltpu.VMEM((D_FF, D), BF16), pltpu.VMEM((D_FF, D), BF16), pltpu.SemaphoreType.DMA((2,))],
        compiler_params=pltpu.CompilerParams(dimension_semantics=("arbitrary",), vmem_limit_bytes=VMEM_LIMIT),
    )(x, mixed, tgt, *small, wpack)


def _mlp_bwd(h2, f, dz, wpack):
    t_len = h2.shape[0]
    tb = TB_MLPB
    nb = t_len // tb

    def body(h2_ref, f_ref, dz_ref, w1t_ref, w2_ref, dh2_ref, dw1t_ref, dw2_ref):
        j = pl.program_id(0)
        t = pl.program_id(1)
        rows = pl.ds(pl.multiple_of(t * tb, tb), tb)
        w1t = w1t_ref[0]
        w2 = w2_ref[0]
        dz = dz_ref[...]
        rf = jnp.maximum(f_ref[...], 0.0)
        a2 = (rf * rf).astype(BF16)
        df = (_dot_nt(dz, w2) * (2.0 * rf)).astype(BF16)
        g2 = _dot_tn(a2, dz)
        g1 = _dot_tn(df, h2_ref[...])
        dh = _dot(df, w1t)

        @pl.when(t == 0)
        def _():
            dw2_ref[0] = g2
            dw1t_ref[0] = g1

        @pl.when(t != 0)
        def _():
            dw2_ref[0] += g2
            dw1t_ref[0] += g1

        @pl.when(j == 0)
        def _():
            dh2_ref[rows, :] = dh

        @pl.when(j != 0)
        def _():
            dh2_ref[rows, :] += dh

    return pl.pallas_call(
        body,
        name="mlp_bwd",
        grid=(N_DEV, nb),
        in_specs=[pl.BlockSpec((tb, D), lambda j, t: (t, 0)),
                  pl.BlockSpec((tb, FF_BLK), lambda j, t: (t, j)),
                  pl.BlockSpec((tb, D), lambda j, t: (t, 0)),
                  pl.BlockSpec((1, ROWS_W1T, D), lambda j, t: (j, OFF_W1T // ROWS_W1T, 0)),
                  pl.BlockSpec((1, ROWS_W2, D), lambda j, t: (j, OFF_W2 // ROWS_W2, 0))],
        out_specs=[pl.BlockSpec((t_len, D), lambda j, t: (0, 0)),
                   pl.BlockSpec((1, FF_BLK, D), lambda j, t: (j, 0, 0)),
                   pl.BlockSpec((1, FF_BLK, D), lambda j, t: (j, 0, 0))],
        out_shape=[jax.ShapeDtypeStruct((t_len, D), F32), jax.ShapeDtypeStruct((N_DEV, FF_BLK, D), F32),
                   jax.ShapeDtypeStruct((N_DEV, FF_BLK, D), F32)],
        compiler_params=pltpu.CompilerParams(dimension_semantics=("arbitrary", "arbitrary"),
                                             vmem_limit_bytes=VMEM_LIMIT),
    )(h2, f, dz, wpack, wpack)


V_SHIFT1, V_SCALE1, V_GATE1, V_SHIFT2, V_SCALE2, V_G1, V_G2 = 0, 1, 2, 3, 4, 6, 7
V_BL_BA, V_BX_SP, V_GL_GC, V_WL01, V_WL23, V_WS01, V_WS2 = 8, 9, 10, 11, 12, 13, 14
V_ROWS = 16


def _chip_scatter_copies(srcs, dsts, send_sems, recv_sems):
    me = _position()
    copies = []
    for a, (src, dst) in enumerate(zip(srcs, dsts)):
        for j, k in enumerate(CHIP_FLIPS):
            peer = _flip(me, k)
            copies.append(pltpu.make_async_remote_copy(
                src_ref=src.at[2 * peer[0] + peer[1]], dst_ref=dst.at[j], send_sem=send_sems.at[len(CHIP_FLIPS) * a + j],
                recv_sem=recv_sems.at[len(CHIP_FLIPS) * a + j], device_id=peer, device_id_type=MESH))
    return copies


def _mixer_bwd(x, mixed, dh2, dx2, proj, hl, modraw, adab, g1, g2, wl, bl, bda, bdx, ba, bxb, ap, ws, gl, gc, avg, wpack,
               chip_sums):
    t_len = x.shape[0]
    tb = TB_MIXB
    nb = t_len // tb
    hb = tb // HALO
    n_sums = len(chip_sums)

    def body(x_ref, mixed_ref, dh2_ref, dx2_ref, proj_ref, projh_ref, hl_ref, hlh_ref,
             modraw_ref, adab_ref, g1_ref, g2_ref, wl_ref, bl_ref, bda_ref, bdx_ref, ba_ref, bxb_ref, ap_ref,
             ws_ref, gl_ref, gc_ref, avg_ref, wpack_hbm, *rest):
        sums_hbm, rest = rest[:n_sums], rest[n_sums:]
        gx_ref, vec_ref, dga_ref, dgx_ref, dwout_hbm, dwint_hbm = rest[:6]
        landed_hbm, rest = rest[6:6 + n_sums], rest[6 + n_sums:]
        (win_v, wout_v, dwout_v, dwint_v, sem, ulx_ext, cv_ext, hl_ext, a_ext, dxl_ext, dcc_ext, dcar, an_s, g_s, dh_s,
         send_sems, recv_sems) = rest
        i = pl.program_id(0)
        blk = nb - 1 - i
        chip_copies = _chip_scatter_copies(sums_hbm, landed_hbm, send_sems, recv_sems)

        @pl.when(i == 0)
        def _():
            for cp in chip_copies:
                cp.start()
            cps = _load_packed(wpack_hbm, OFF_WIN, ROWS_WIN, win_v, sem.at[0])
            cps += _load_packed(wpack_hbm, OFF_WOUT, ROWS_WOUT, wout_v, sem.at[1])
            vec_ref[...] = jnp.zeros(vec_ref.shape, F32)
            dga_ref[...] = jnp.zeros(dga_ref.shape, F32)
            dgx_ref[...] = jnp.zeros(dgx_ref.shape, F32)
            dwout_v[...] = jnp.zeros(dwout_v.shape, F32)
            dwint_v[...] = jnp.zeros(dwint_v.shape, F32)
            zero = jnp.zeros((HALO, W), F32)
            a_ext[tb:tb + HALO, :] = zero
            dxl_ext[tb:tb + HALO, :] = zero
            dcc_ext[tb:tb + HALO, :] = zero
            dcar[...] = zero
            for cp in cps:
                cp.wait()

        mod = modraw_ref[...] + adab_ref[...]
        shift1, scale1, gate1, scale2 = mod[0:1], mod[1:2], mod[2:3], mod[4:5]
        x = x_ref[...]
        mixed = mixed_ref[...]

        x1 = x + gate1 * mixed
        r2 = lax.rsqrt(_rowmean(x1 * x1) + EPS)
        xn2 = x1 * r2
        dh2 = dh2_ref[...]
        vec_ref[V_SHIFT2:V_SHIFT2 + 1, :] += _colsum(dh2)
        vec_ref[V_SCALE2:V_SCALE2 + 1, :] += _colsum(dh2 * xn2 * g2_ref[...])
        vec_ref[V_G2:V_G2 + 1, :] += _colsum(dh2 * (1.0 + scale2) * xn2)
        dxn2 = dh2 * g2_ref[...] * (1.0 + scale2)
        dx1 = dx2_ref[...] + r2 * (dxn2 - xn2 * _rowmean(dxn2 * xn2))
        vec_ref[V_GATE1:V_GATE1 + 1, :] += _colsum(dx1 * mixed)
        dmixed = (gate1 * dx1).astype(BF16)

        proj = proj_ref[...]
        u_lx, u_ly, u_b, u_c, u_v = (proj[:, k * W:(k + 1) * W] for k in range(5))
        has_prev = (blk > 0).astype(F32)
        projh = projh_ref[...]
        ulx_ext[0:HALO, :] = projh[:, 0:W] * has_prev
        ulx_ext[HALO:HALO + tb, :] = u_lx
        xl = bl_ref[...] + wl_ref[CONV_L - 1:CONV_L, :] * u_lx
        for k in range(CONV_L - 1):
            xl = xl + wl_ref[k:k + 1, :] * ulx_ext[pl.ds(HALO - (CONV_L - 1) + k, tb), :]
        xlb = xl.astype(BF16)
        r = _sigmoid(_dot(xlb, bda_ref[...]) + ba_ref[...])
        ig = _sigmoid(_dot(xlb, bdx_ref[...]) + bxb_ref[...])
        sp = _softplus(ap_ref[...])
        log_a = (-C_GATE) * r * sp
        a = jnp.exp(log_a)
        mult_raw = jnp.sqrt(-_expm1(2.0 * log_a))
        first = (blk * tb + lax.broadcasted_iota(jnp.int32, (tb, W), 0)) == 0
        mult = jnp.where(first, 1.0, mult_raw)
        hl = hl_ref[...]
        ge, th = _gelu(u_ly)
        p = ge * hl
        rp = lax.rsqrt(_group_mean(p * p, avg_ref[...]) + EPS)
        pn = p * rp
        cv = u_c * u_v
        cv_ext[0:HALO, :] = projh[:, 3 * W:4 * W] * projh[:, 4 * W:5 * W] * has_prev
        cv_ext[HALO:HALO + tb, :] = cv
        cc = ws_ref[CONV_S - 1:CONV_S, :] * cv
        for k in range(CONV_S - 1):
            cc = cc + ws_ref[k:k + 1, :] * cv_ext[pl.ds(HALO - (CONV_S - 1) + k, tb), :]
        q = u_b * cc
        rq = lax.rsqrt(_group_mean(q * q, avg_ref[...]) + EPS)
        qn = q * rq

        y_lru_b = (pn * gl_ref[...]).astype(BF16)
        y_conv_b = (qn * gc_ref[...]).astype(BF16)
        dwout_v[0:W, :] += _dot_tn(y_lru_b, dmixed)
        dwout_v[W:2 * W, :] += _dot_tn(y_conv_b, dmixed)
        dyl = _dot_nt(dmixed, wout_v[0:W, :])
        dyc = _dot_nt(dmixed, wout_v[W:2 * W, :])

        dqn = dyc * gc_ref[...]
        dq = rq * (dqn - qn * _group_mean(dqn * qn, avg_ref[...]))
        du_b = dq * cc
        dcc = dq * u_b
        dcc_ext[0:tb, :] = dcc
        dcv = ws_ref[CONV_S - 1:CONV_S, :] * dcc
        for k in range(CONV_S - 1):
            dcv = dcv + ws_ref[k:k + 1, :] * dcc_ext[pl.ds(CONV_S - 1 - k, tb), :]
        dcc_ext[tb:tb + HALO, :] = dcc_ext[0:HALO, :]
        du_c = dcv * u_v
        du_v = dcv * u_c
        dws = [_colsum(dcc * cv_ext[pl.ds(HALO - (CONV_S - 1) + k, tb), :]) for k in range(CONV_S)]

        dpn = dyl * gl_ref[...]
        dp = rp * (dpn - pn * _group_mean(dpn * pn, avg_ref[...]))
        du_ly = dp * hl * _gelu_grad(u_ly, th)
        g_s[...] = dp * ge
        a_ext[0:tb, :] = a
        an_s[...] = a_ext[pl.ds(1, tb), :]
        _scan_groups(hb, an_s, g_s, dh_s, dcar, reverse=True)
        a_ext[tb:tb + HALO, :] = a_ext[0:HALO, :]
        dh = dh_s[...]
        hl_ext[0:HALO, :] = hlh_ref[...] * has_prev
        hl_ext[HALO:HALO + tb, :] = hl
        da = dh * hl_ext[pl.ds(HALO - 1, tb), :]
        dmult = dh * (ig * xl)
        dig = dh * (mult * xl)
        dxl = dh * (mult * ig)
        dlog = da * a - jnp.where(first, 0.0, dmult * (a * a) / mult_raw)
        dr = dlog * ((-C_GATE) * sp)
        dsp = _colsum(dlog * ((-C_GATE) * r))
        dga = dr * r * (1.0 - r)
        dgx = dig * ig * (1.0 - ig)
        dgab = dga.astype(BF16)
        dgxb = dgx.astype(BF16)
        dga_ref[...] += _dot_tn(xlb, dgab)
        dgx_ref[...] += _dot_tn(xlb, dgxb)
        dxl = dxl + _dot_nt(dgab, bda_ref[...]) + _dot_nt(dgxb, bdx_ref[...])
        dxl_ext[0:tb, :] = dxl
        du_lx = wl_ref[CONV_L - 1:CONV_L, :] * dxl
        for k in range(CONV_L - 1):
            du_lx = du_lx + wl_ref[k:k + 1, :] * dxl_ext[pl.ds(CONV_L - 1 - k, tb), :]
        dxl_ext[tb:tb + HALO, :] = dxl_ext[0:HALO, :]
        dwl = [_colsum(dxl * ulx_ext[pl.ds(HALO - (CONV_L - 1) + k, tb), :]) for k in range(CONV_L)]

        cat = lambda u, v: jnp.concatenate([u, v], axis=1)
        vec_ref[V_BL_BA:V_BL_BA + 1, :] += cat(_colsum(dxl), _colsum(dga))
        vec_ref[V_BX_SP:V_BX_SP + 1, :] += cat(_colsum(dgx), dsp)
        vec_ref[V_GL_GC:V_GL_GC + 1, :] += cat(_colsum(dyl * pn), _colsum(dyc * qn))
        vec_ref[V_WL01:V_WL01 + 1, :] += cat(dwl[0], dwl[1])
        vec_ref[V_WL23:V_WL23 + 1, :] += cat(dwl[2], dwl[3])
        vec_ref[V_WS01:V_WS01 + 1, :] += cat(dws[0], dws[1])
        vec_ref[V_WS2:V_WS2 + 1, 0:W] += dws[2]

        r1 = lax.rsqrt(_rowmean(x * x) + EPS)
        xn1 = x * r1
        hb16 = ((xn1 * g1_ref[...]) * (1.0 + scale1) + shift1).astype(BF16)
        dh_in = jnp.zeros((tb, D), F32)
        for k, du in enumerate((du_lx, du_ly, du_b, du_c, du_v)):
            dub = du.astype(BF16)
            dwint_v[k * W:(k + 1) * W, :] += _dot_tn(dub, hb16)
            dh_in = dh_in + _dot(dub, win_v[k * W:(k + 1) * W, :])
        vec_ref[V_SHIFT1:V_SHIFT1 + 1, :] += _colsum(dh_in)
        vec_ref[V_SCALE1:V_SCALE1 + 1, :] += _colsum(dh_in * xn1 * g1_ref[...])
        vec_ref[V_G1:V_G1 + 1, :] += _colsum(dh_in * (1.0 + scale1) * xn1)
        dxn1 = dh_in * g1_ref[...] * (1.0 + scale1)
        gx_ref[...] = dx1 + r1 * (dxn1 - xn1 * _rowmean(dxn1 * xn1))

        @pl.when(i == nb - 1)
        def _():
            pltpu.sync_copy(dwout_v, dwout_hbm)
            pltpu.sync_copy(dwint_v, dwint_hbm)
            for cp in chip_copies:
                cp.wait_recv()
            for cp in chip_copies:
                cp.wait_send()

    rev = lambda cols: pl.BlockSpec((tb, cols), lambda i: (nb - 1 - i, 0))
    halo = lambda cols: pl.BlockSpec((HALO, cols), lambda i: (jnp.maximum((nb - 1 - i) * hb - 1, 0), 0))
    full = lambda a: pl.BlockSpec(a.shape, lambda i: (0,) * a.ndim)
    small = (modraw, adab, g1, g2, wl, bl, bda, bdx, ba, bxb, ap, ws, gl, gc, avg)
    ext = pltpu.VMEM((tb + HALO, W), F32)
    n_sems = len(CHIP_FLIPS) * n_sums
    return pl.pallas_call(
        body,
        name="mixer_bwd",
        grid=(nb,),
        in_specs=[rev(D), rev(D), rev(D), rev(D), rev(D_IN), halo(D_IN), rev(W), halo(W)]
        + [full(a) for a in small] + [ANY] * (1 + n_sums),
        out_specs=[rev(D), pl.BlockSpec((V_ROWS, D), lambda i: (0, 0)), pl.BlockSpec((W, W), lambda i: (0, 0)),
                   pl.BlockSpec((W, W), lambda i: (0, 0)), ANY, ANY] + [ANY] * n_sums,
        out_shape=[jax.ShapeDtypeStruct((t_len, D), F32), jax.ShapeDtypeStruct((V_ROWS, D), F32),
                   jax.ShapeDtypeStruct((W, W), F32), jax.ShapeDtypeStruct((W, W), F32),
                   jax.ShapeDtypeStruct((D, D), F32), jax.ShapeDtypeStruct((D_IN, D), F32)]
        + [jax.ShapeDtypeStruct((len(CHIP_FLIPS),) + s.shape[1:], s.dtype) for s in chip_sums],
        scratch_shapes=[pltpu.VMEM((D_IN, D), BF16), pltpu.VMEM((D, D), BF16), pltpu.VMEM((D, D), F32),
                        pltpu.VMEM((D_IN, D), F32), pltpu.SemaphoreType.DMA((2,)),
                        ext, ext, ext, ext, ext, ext, pltpu.VMEM((HALO, W), F32),
                        pltpu.VMEM((tb, W), F32), pltpu.VMEM((tb, W), F32), pltpu.VMEM((tb, W), F32),
                        pltpu.SemaphoreType.DMA((n_sems,)), pltpu.SemaphoreType.DMA((n_sems,))],
        compiler_params=pltpu.CompilerParams(dimension_semantics=("arbitrary",), vmem_limit_bytes=VMEM_LIMIT),
    )(x, mixed, dh2, dx2, proj, proj, hl, hl, *small, wpack, *chip_sums)


def _block_diag(w):
    n, m, _ = w.shape
    eye = jnp.eye(n, dtype=w.dtype)
    return (w[:, :, None, :] * eye[:, None, :, None]).reshape(n * m, n * m)


def _diag_blocks(mat, n=8):
    m = mat.shape[0] // n
    return jnp.stack([mat[h * m:(h + 1) * m, h * m:(h + 1) * m] for h in range(n)])


def _pad_rows(a, rows):
    return jnp.pad(a, ((0, rows - a.shape[0]),) + ((0, 0),) * (a.ndim - 1))


def _position():
    return lax.axis_index("x"), lax.axis_index("y"), lax.axis_index("c")


def _linear(pos):
    return 4 * pos[0] + 2 * pos[1] + pos[2]


def _flip(pos, k):
    return tuple(1 - p if k & bit else p for p, bit in zip(pos, (4, 2, 1)))


def _exchange_all(make_copy, make_arrival):
    copies = [make_copy(k) for k in range(1, N_DEV)]
    for cp in copies:
        cp.start()
    for k in range(1, N_DEV):
        make_arrival(k).wait_recv()
    for cp in copies:
        cp.wait_send()


def _mod_exchange(msg, ada_w):
    cols = ada_w.shape[1]

    def body(msg_ref, adaw_ref, gath_ref, mod_ref, sendbuf, send_a, recv_a, send_b, recv_b):
        me = _position()
        me_lin = _linear(me)
        m = msg_ref[...]
        row = lax.broadcasted_iota(jnp.int32, m.shape, 0)
        gath_ref[me_lin] = jnp.where(row == 0, m * _sigmoid(m), m)

        def gather_copy(k, src_lin):
            return pltpu.make_async_remote_copy(
                src_ref=gath_ref.at[src_lin], dst_ref=gath_ref.at[src_lin], send_sem=send_a.at[k - 1],
                recv_sem=recv_a.at[k - 1], device_id=_flip(me, k), device_id_type=MESH)

        _exchange_all(lambda k: gather_copy(k, me_lin), lambda k: gather_copy(k, _linear(_flip(me, k))))

        sc_all = gath_ref[:, 0, :]
        scb = jnp.concatenate([sc_all, jnp.zeros_like(sc_all)], axis=0).astype(BF16)
        prod = _dot(scb, adaw_ref[...].astype(BF16))
        for b in range(N_DEV):
            sendbuf[b] = jnp.broadcast_to(prod[b:b + 1, :], (HALO, cols))
        mod_ref[me_lin] = sendbuf[me_lin]

        def row_copy(k, dst_lin):
            peer = _flip(me, k)
            return pltpu.make_async_remote_copy(
                src_ref=sendbuf.at[_linear(peer)], dst_ref=mod_ref.at[dst_lin], send_sem=send_b.at[k - 1],
                recv_sem=recv_b.at[k - 1], device_id=peer, device_id_type=MESH)

        _exchange_all(lambda k: row_copy(k, me_lin), lambda k: row_copy(k, _linear(_flip(me, k))))

    return pl.pallas_call(
        body,
        name="mod_exchange",
        in_specs=[WHOLE, WHOLE],
        out_specs=[WHOLE, WHOLE],
        out_shape=[jax.ShapeDtypeStruct((N_DEV, HALO, D), F32), jax.ShapeDtypeStruct((N_DEV, HALO, cols), F32)],
        scratch_shapes=[pltpu.VMEM((N_DEV, HALO, cols), F32)] + [pltpu.SemaphoreType.DMA((N_DEV - 1,))] * 4,
        compiler_params=pltpu.CompilerParams(vmem_limit_bytes=VMEM_LIMIT),
    )(msg, ada_w)


def _weight_gather(block):
    rows, cols = block.shape

    def body(x_ref, out_ref, send_sems, recv_sems, local_sem):
        x, y, c = _position()
        me, sibling = (x, y, c), (x, y, 1 - c)
        chips = [(1 - x, y), (x, 1 - y), (1 - x, 1 - y)]

        def copy(k, block_of, to, src=None):
            dst = out_ref.at[_linear(block_of)]
            return pltpu.make_async_remote_copy(
                src_ref=dst if src is None else src, dst_ref=dst, send_sem=send_sems.at[k], recv_sem=recv_sems.at[k],
                device_id=to, device_id_type=MESH)

        mine = pltpu.make_async_copy(x_ref, out_ref.at[_linear(me)], local_sem)
        mine.start()
        first = [copy(0, me, sibling, src=x_ref)]
        first += [copy(1 + j, me, (*chip, c), src=x_ref) for j, chip in enumerate(chips)]
        for cp in first:
            cp.start()
        passed = [copy(4 + j, (*chip, c), sibling) for j, chip in enumerate(chips)]
        for j, chip in enumerate(chips):
            copy(1 + j, (*chip, c), me).wait_recv()
            passed[j].start()
        copy(0, sibling, me).wait_recv()
        for j, chip in enumerate(chips):
            copy(4 + j, (*chip, 1 - c), me).wait_recv()
        for cp in first + passed:
            cp.wait_send()
        mine.wait()

    return pl.pallas_call(
        body,
        name="weight_gather",
        in_specs=[ANY],
        out_specs=ANY,
        out_shape=jax.ShapeDtypeStruct((N_DEV, rows, cols), block.dtype),
        scratch_shapes=[pltpu.SemaphoreType.DMA((7,)), pltpu.SemaphoreType.DMA((7,)), pltpu.SemaphoreType.DMA],
    )(block)


def _grads_to_sibling(which, arrs):
    n = len(arrs)

    def body(*refs):
        srcs, dsts = refs[:n], refs[n:2 * n]
        send_sems, recv_sems = refs[2 * n:]
        x, y, c = _position()
        copies = []
        for a in range(n):
            for k in range(4):
                copies.append(pltpu.make_async_remote_copy(
                    src_ref=srcs[a].at[k, 1 - c], dst_ref=dsts[a].at[k], send_sem=send_sems.at[4 * a + k],
                    recv_sem=recv_sems.at[4 * a + k], device_id=(x, y, 1 - c), device_id_type=MESH))
        for cp in copies:
            cp.start()
        for cp in copies:
            cp.wait_recv()
        for cp in copies:
            cp.wait_send()

    return pl.pallas_call(
        body,
        name=which + "_grads_to_sibling",
        in_specs=[ANY] * n,
        out_specs=[ANY] * n,
        out_shape=[jax.ShapeDtypeStruct((4,) + a.shape[2:], a.dtype) for a in arrs],
        scratch_shapes=[pltpu.SemaphoreType.DMA((4 * n,)), pltpu.SemaphoreType.DMA((4 * n,))],
    )(*arrs)


def _grads_to_chips(arrs):
    n = len(arrs)

    def body(*refs):
        srcs, dsts = refs[:n], refs[n:2 * n]
        send_sems, recv_sems = refs[2 * n:]
        copies = _chip_scatter_copies(srcs, dsts, send_sems, recv_sems)
        for cp in copies:
            cp.start()
        for cp in copies:
            cp.wait_recv()
        for cp in copies:
            cp.wait_send()

    return pl.pallas_call(
        body,
        name="grads_to_chips",
        in_specs=[ANY] * n,
        out_specs=[ANY] * n,
        out_shape=[jax.ShapeDtypeStruct((3,) + a.shape[1:], a.dtype) for a in arrs],
        scratch_shapes=[pltpu.SemaphoreType.DMA((3 * n,)), pltpu.SemaphoreType.DMA((3 * n,))],
    )(*arrs)


def _row_block(rows):
    return 256 if rows % 256 == 0 else rows // 2


def _pair_sum(pos, mine, recv):
    _, _, rows, cols = mine.shape
    rb = _row_block(rows)

    def body(pos_ref, mine_ref, recv_ref, out_ref):
        out_ref[0] = (mine_ref[0, 0] + recv_ref[0]).astype(BF16)

    return pl.pallas_call(
        body,
        name="grad_pair_sum",
        grid_spec=pltpu.PrefetchScalarGridSpec(
            num_scalar_prefetch=1, grid=(4, rows // rb),
            in_specs=[pl.BlockSpec((1, 1, rb, cols), lambda k, r, pos: (k, pos[0], r, 0)),
                      pl.BlockSpec((1, rb, cols), lambda k, r, pos: (k, r, 0))],
            out_specs=pl.BlockSpec((1, rb, cols), lambda k, r, pos: (k, r, 0))),
        out_shape=jax.ShapeDtypeStruct((4, rows, cols), BF16),
        compiler_params=pltpu.CompilerParams(dimension_semantics=("arbitrary", "arbitrary")),
    )(pos, mine, recv)


def _final_sum(pos, mine, recv, chips):
    _, _, rows, cols = mine.shape
    rb = _row_block(rows)

    def body(pos_ref, mine_ref, recv_ref, chips_ref, out_ref):
        g = mine_ref[0, 0] + recv_ref[0]
        for j in range(3):
            g = g + chips_ref[j].astype(F32)
        out_ref[...] = g

    return pl.pallas_call(
        body,
        name="grad_final_sum",
        grid_spec=pltpu.PrefetchScalarGridSpec(
            num_scalar_prefetch=1, grid=(rows // rb,),
            in_specs=[pl.BlockSpec((1, 1, rb, cols), lambda r, pos: (pos[1], pos[0], r, 0)),
                      pl.BlockSpec((1, rb, cols), lambda r, pos: (pos[1], r, 0)),
                      pl.BlockSpec((3, rb, cols), lambda r, pos: (0, r, 0))],
            out_specs=pl.BlockSpec((rb, cols), lambda r, pos: (r, 0))),
        out_shape=jax.ShapeDtypeStruct((rows, cols), F32),
        compiler_params=pltpu.CompilerParams(dimension_semantics=("arbitrary",)),
    )(pos, mine, recv, chips)


LOSS_ROW = V_ROWS + 8
GB_ROW = LOSS_ROW + 7


def _grad_exchange(gmod8, sc_t, msg_vec, msg_gate):
    cols = gmod8.shape[1]

    def body(gmod_ref, sct_ref, vec_ref, gate_ref, gadaw_ref, sumv_ref, sumg_ref, gb_ref,
             sendbuf, grecv, bufv, bufg, send_a, recv_a, send_v, recv_v, send_g, recv_g):
        me = _position()
        me_lin = _linear(me)
        gm = gmod_ref[...]
        for b in range(N_DEV):
            sendbuf[b] = jnp.broadcast_to(gm[b:b + 1, :], (HALO, cols))
        grecv[me_lin] = sendbuf[me_lin]

        def row_copy(k, dst_lin):
            peer = _flip(me, k)
            return pltpu.make_async_remote_copy(
                src_ref=sendbuf.at[_linear(peer)], dst_ref=grecv.at[dst_lin], send_sem=send_a.at[k - 1],
                recv_sem=recv_a.at[k - 1], device_id=peer, device_id_type=MESH)

        _exchange_all(lambda k: row_copy(k, me_lin), lambda k: row_copy(k, _linear(_flip(me, k))))

        g_all = grecv[:, 0, :]
        g_pad = jnp.concatenate([g_all, jnp.zeros((sct_ref.shape[1] - N_DEV, cols), F32)], axis=0).astype(BF16)
        gadaw_ref[...] = _dot(sct_ref[...], g_pad)
        bufv[me_lin] = vec_ref[...]
        bufv[me_lin, GB_ROW:GB_ROW + 1, 0:cols] = _colsum(g_all)
        bufg[me_lin] = gate_ref[...]

        def gather_copy(buf, sends, recvs, k, src_lin):
            return pltpu.make_async_remote_copy(
                src_ref=buf.at[src_lin], dst_ref=buf.at[src_lin], send_sem=sends.at[k - 1], recv_sem=recvs.at[k - 1],
                device_id=_flip(me, k), device_id_type=MESH)

        for buf, sends, recvs in ((bufv, send_v, recv_v), (bufg, send_g, recv_g)):
            _exchange_all(lambda k: gather_copy(buf, sends, recvs, k, me_lin),
                          lambda k: gather_copy(buf, sends, recvs, k, _linear(_flip(me, k))))

        sv = bufv[0]
        sg = bufg[0]
        for d in range(1, N_DEV):
            sv = sv + bufv[d]
            sg = sg + bufg[d]
        sumv_ref[...] = sv
        sumg_ref[...] = sg
        gb_ref[...] = bufv[:, GB_ROW, :]

    return pl.pallas_call(
        body,
        name="grad_exchange",
        in_specs=[WHOLE] * 4,
        out_specs=[WHOLE] * 4,
        out_shape=[jax.ShapeDtypeStruct((D, cols), F32), jax.ShapeDtypeStruct(msg_vec.shape, F32),
                   jax.ShapeDtypeStruct(msg_gate.shape, F32), jax.ShapeDtypeStruct((N_DEV, D), F32)],
        scratch_shapes=[pltpu.VMEM((N_DEV, HALO, cols), F32), pltpu.VMEM((N_DEV, HALO, cols), F32),
                        pltpu.VMEM((N_DEV,) + msg_vec.shape, F32), pltpu.VMEM((N_DEV,) + msg_gate.shape, F32)]
        + [pltpu.SemaphoreType.DMA((N_DEV - 1,))] * 6,
        compiler_params=pltpu.CompilerParams(vmem_limit_bytes=VMEM_LIMIT),
    )(gmod8, sc_t, msg_vec, msg_gate)


def _adamw_math(w, g, m, v):
    m = ADAM_B1 * m + (1.0 - ADAM_B1) * g
    v = ADAM_B2 * v + (1.0 - ADAM_B2) * (g * g)
    m_hat = m / (1.0 - ADAM_B1 ** ADAM_STEP)
    v_hat = v / (1.0 - ADAM_B2 ** ADAM_STEP)
    delta = -ADAM_LR * (m_hat / (jnp.sqrt(v_hat) + ADAM_EPS) + ADAM_WD * w)
    return delta, m, v


def _adamw(name, w, g, m, v):
    rows, cols = w.shape
    rb = 256 if rows % 256 == 0 else rows

    def body(w_ref, g_ref, m_ref, v_ref, d_ref, mo_ref, vo_ref):
        d_ref[...], mo_ref[...], vo_ref[...] = _adamw_math(w_ref[...], g_ref[...], m_ref[...], v_ref[...])

    spec = pl.BlockSpec((rb, cols), lambda r: (r, 0))
    return pl.pallas_call(
        body,
        name="adamw_" + name,
        grid=(rows // rb,),
        in_specs=[spec] * 4,
        out_specs=[spec] * 3,
        out_shape=[jax.ShapeDtypeStruct((rows, cols), F32)] * 3,
        compiler_params=pltpu.CompilerParams(dimension_semantics=("arbitrary",)),
    )(w, g, m, v)


def _adamw_small(ws, gs, ms, vs, sigmoid_scaled):
    n = len(ws)

    def body(*refs):
        w_refs, g_refs, m_refs, v_refs = (refs[i * n:(i + 1) * n] for i in range(4))
        outs = refs[4 * n:]
        for i in range(n):
            w = w_refs[i][...]
            g = g_refs[i][...]
            if sigmoid_scaled[i]:
                g = g * _sigmoid(w)
            delta, m, v = _adamw_math(w, g, m_refs[i][...], v_refs[i][...])
            outs[4 * i][...] = g
            outs[4 * i + 1][...] = delta
            outs[4 * i + 2][...] = m
            outs[4 * i + 3][...] = v

    shapes = [jax.ShapeDtypeStruct(w.shape, F32) for w in ws for _ in range(4)]
    outs = pl.pallas_call(
        body,
        name="adamw_small",
        in_specs=[WHOLE] * (4 * n),
        out_specs=[WHOLE] * (4 * n),
        out_shape=shapes,
    )(*ws, *gs, *ms, *vs)
    return [outs[4 * i:4 * i + 4] for i in range(n)]


_WEIGHT_NAMES = ("ada_w", "ada_b", "norm1_g", "w_in", "lru_conv_w", "lru_conv_b", "gate_a_w", "gate_a_b", "gate_x_w",
                 "gate_x_b", "a_param", "short_conv_w", "lru_out_g", "conv_out_g", "w_out", "norm2_g", "w_mlp1",
                 "w_mlp2", "final_g")
_BIG = ("ada_w", "w_in", "w_out", "w_mlp1", "w_mlp2")


def kernel(x, c, ada_w, ada_b, norm1_g, w_in, lru_conv_w, lru_conv_b, gate_a_w, gate_a_b, gate_x_w, gate_x_b, a_param, short_conv_w, lru_out_g, conv_out_g, w_out, norm2_g, w_mlp1, w_mlp2, final_g, loss_target, m_ada_w, m_ada_b, m_norm1_g, m_w_in, m_lru_conv_w, m_lru_conv_b, m_gate_a_w, m_gate_a_b, m_gate_x_w, m_gate_x_b, m_a_param, m_short_conv_w, m_lru_out_g, m_conv_out_g, m_w_out, m_norm2_g, m_w_mlp1, m_w_mlp2, m_final_g, v_ada_w, v_ada_b, v_norm1_g, v_w_in, v_lru_conv_w, v_lru_conv_b, v_gate_a_w, v_gate_a_b, v_gate_x_w, v_gate_x_b, v_a_param, v_short_conv_w, v_lru_out_g, v_conv_out_g, v_w_out, v_norm2_g, v_w_mlp1, v_w_mlp2, v_final_g):
    given = dict(locals())
    weights = {n: given[n] for n in _WEIGHT_NAMES}
    xi, yi, ci = _position()
    me_lin = _linear((xi, yi, ci))
    hd = W // N_DEV

    msg = (jnp.pad(c, ((0, HALO - 1), (0, 0)))
           + jnp.pad(lru_conv_w[0], ((1, HALO - 1 - CONV_L), (0, D - hd)))
           + jnp.pad(short_conv_w[0], ((1 + CONV_L, 0), (0, D - hd))))
    gath, mod_all = _mod_exchange(msg, ada_w[0])
    sc_all = gath[:, 0, :]
    wl = jnp.transpose(gath[:, 1:1 + CONV_L, :hd], (1, 0, 2)).reshape(CONV_L, W)
    ws = jnp.transpose(gath[:, 1 + CONV_L:HALO, :hd], (1, 0, 2)).reshape(CONV_S, W)
    modraw = _pad_rows(mod_all[:, 0, :].reshape(6, D), HALO)
    adab = _pad_rows(ada_b.reshape(6, D), HALO)

    mixer_block = jnp.concatenate([w_out[0], w_in[0].T], axis=0).astype(BF16)
    mlp_block = jnp.concatenate([w_mlp1[0].T, w_mlp2[0]], axis=0).astype(BF16)
    wmix = _weight_gather(mixer_block)

    x2d, tgt = x[0], loss_target[0]
    gf = final_g.reshape(1, D)
    bda = _block_diag(gate_a_w[0]).astype(BF16)
    bdx = _block_diag(gate_x_w[0]).astype(BF16)
    avg = _block_diag(jnp.full((8, W // 8, W // 8), 8.0 / W, F32)).astype(BF16)
    wl8 = _pad_rows(wl, HALO)
    ws8 = _pad_rows(ws, HALO)
    mixer_small = (wl8, lru_conv_b, bda, bdx, gate_a_b, gate_x_b, a_param, ws8, lru_out_g, conv_out_g, avg)
    proj, hl, mixed, wmlp = _mixer_fwd(x2d, modraw, adab, norm1_g, *mixer_small, wmix, mlp_block)
    wmlp = _sibling_forward(wmlp)
    h2, f, dx2, dz, vec2, loss8 = _mlp_fwd(x2d, mixed, tgt, modraw, adab, norm2_g, gf, wmlp)
    dh2, dw1t, dw2 = _mlp_bwd(h2, f, dz, wmlp)

    pos = jnp.stack([ci, 2 * xi + yi]).astype(jnp.int32)
    by_dest = lambda g: g.reshape(4, 2, g.shape[0] // N_DEV, D)
    mlp_parts = [by_dest(dw1t.reshape(D_FF, D)), by_dest(dw2.reshape(D_FF, D))]
    mlp_sib = _grads_to_sibling("mlp", mlp_parts)
    mlp_sums = [_pair_sum(pos, p, r) for p, r in zip(mlp_parts, mlp_sib)]
    gx, vec, dga, dgx, dwout, dwint, *mlp_chips = _mixer_bwd(
        x2d, mixed, dh2, dx2, proj, hl, modraw, adab, norm1_g, norm2_g, *mixer_small, wmix, mlp_sums)
    mix_parts = [by_dest(dwout), by_dest(dwint)]
    mix_sib = _grads_to_sibling("mixer", mix_parts)
    mix_sums = [_pair_sum(pos, p, r) for p, r in zip(mix_parts, mix_sib)]
    mix_chips = _grads_to_chips(mix_sums)
    g_w1t, g_w2, g_wout, g_wint = (_final_sum(pos, p, r, q) for p, r, q in zip(
        mlp_parts + mix_parts, list(mlp_sib) + list(mix_sib), list(mlp_chips) + list(mix_chips)))

    gmod8 = (jnp.pad(vec[0:5], ((0, 1), (0, 0))) + jnp.pad(vec2[0:1], ((5, 0), (0, 0)))).reshape(N_DEV, 6 * D // N_DEV)
    sc_t = jnp.pad(sc_all.T, ((0, 0), (0, 128 - N_DEV))).astype(BF16)
    loss_rows = jnp.pad(loss8[0:1], ((0, HALO - 1), (0, D - loss8.shape[1])))
    msg_vec = jnp.concatenate([vec, vec2, loss_rows], axis=0)
    msg_gate = jnp.stack([_diag_blocks(dga), _diag_blocks(dgx)]).reshape(W, 128)
    g_adaw, sum_vec, sum_gate, gb = _grad_exchange(gmod8, sc_t, msg_vec, msg_gate)
    loss = sum_vec[LOSS_ROW, 0]
    sum_gate = sum_gate.reshape(2, W, W // 8)
    lo, hi = slice(0, W), slice(W, 2 * W)
    wl_full = sum_vec[V_WL01:V_WL23 + 1].reshape(CONV_L, W)
    ws_full = sum_vec[V_WS01:V_WS2 + 1].reshape(CONV_S + 1, W)[:CONV_S]
    row = lambda r, cols: sum_vec[r:r + 1, cols]
    small_grads = {
        "ada_b": gb[:, :6 * D // N_DEV].reshape(1, 6 * D),
        "norm1_g": row(V_G1, slice(0, D)),
        "lru_conv_w": lax.dynamic_slice(wl_full, (0, me_lin * hd), (CONV_L, hd)),
        "lru_conv_b": row(V_BL_BA, lo),
        "gate_a_w": sum_gate[0],
        "gate_a_b": row(V_BL_BA, hi),
        "gate_x_w": sum_gate[1],
        "gate_x_b": row(V_BX_SP, lo),
        "a_param": row(V_BX_SP, hi),
        "short_conv_w": lax.dynamic_slice(ws_full, (0, me_lin * hd), (CONV_S, hd)),
        "lru_out_g": row(V_GL_GC, lo),
        "conv_out_g": row(V_GL_GC, hi),
        "norm2_g": row(V_G2, slice(0, D)),
        "final_g": sum_vec[V_ROWS + 1:V_ROWS + 2, :],
    }
    names = list(small_grads)
    as2d = lambda a, n: a.reshape(small_grads[n].shape)
    small = _adamw_small([as2d(weights[n], n) for n in names], [small_grads[n] for n in names],
                         [as2d(given["m_" + n], n) for n in names], [as2d(given["v_" + n], n) for n in names],
                         [n == "a_param" for n in names])
    result = {n: tuple(o.reshape(weights[n].shape) for o in outs) for n, outs in zip(names, small)}

    big_grads = {"ada_w": g_adaw, "w_in": g_wint.T, "w_out": g_wout, "w_mlp1": g_w1t.T, "w_mlp2": g_w2}
    for n in _BIG:
        g = big_grads[n]
        delta, new_m, new_v = _adamw(n, weights[n][0], g, given["m_" + n][0], given["v_" + n][0])
        result[n] = tuple(o[None] for o in (g, delta, new_m, new_v))

    return (loss, gx[None], *[result[n][0] for n in _WEIGHT_NAMES], *[result[n][1] for n in _WEIGHT_NAMES],
            *[result[n][2] for n in _WEIGHT_NAMES], *[result[n][3] for n in _WEIGHT_NAMES])
```

```python
import functools

import jax
import jax.numpy as jnp
from jax import lax
from jax.experimental import pallas as pl
from jax.experimental.pallas import tpu as pltpu

F32 = jnp.float32
BF16 = jnp.bfloat16
MESH = pl.DeviceIdType.MESH

N_DEV = 8
D = 1024
W = 512
D_IN = 5 * W
D_FF = 4096
FF_BLK = D_FF // N_DEV
EPS = 1e-6
C_GATE = 8.0
CONV_L = 4
CONV_S = 3
HALO = 8

ROWS_W1T, ROWS_W2, ROWS_WOUT, ROWS_WIN = FF_BLK, FF_BLK, D // N_DEV, D_IN // N_DEV
OFF_WOUT = 0
OFF_WIN = OFF_WOUT + ROWS_WOUT
MIX_ROWS = OFF_WIN + ROWS_WIN
OFF_W1T = 0
OFF_W2 = OFF_W1T + ROWS_W1T
MLP_ROWS = OFF_W2 + ROWS_W2
CHIP_FLIPS = (4, 2, 6)

ADAM_LR = 0.001
ADAM_B1 = 0.9
ADAM_B2 = 0.999
ADAM_EPS = 1e-08
ADAM_WD = 0.01
ADAM_STEP = 10

VMEM_LIMIT = 56 * 1024 * 1024

TB_MIX = 256
TB_MIXB = 256
TB_MLP = 256
TB_MLPB = 512

ANY = pl.BlockSpec(memory_space=pl.ANY)
WHOLE = pl.BlockSpec(memory_space=pltpu.VMEM)


def _dot(a, b):
    return jnp.dot(a, b, preferred_element_type=F32)


def _dot_nt(a, b):
    return lax.dot_general(a, b, (((1,), (1,)), ((), ())), preferred_element_type=F32)


def _dot_tn(a, b):
    return lax.dot_general(a, b, (((0,), (0,)), ((), ())), preferred_element_type=F32)


def _sigmoid(v):
    return 1.0 / (1.0 + jnp.exp(-v))


def _softplus(v):
    t = jnp.exp(-jnp.abs(v))
    small = t * (1.0 - t * (0.5 - t * (1.0 / 3.0)))
    return jnp.maximum(v, 0.0) + jnp.where(t < 1e-2, small, jnp.log(1.0 + t))


def _one_minus_sq(a, log_a):
    return -jnp.tanh(log_a) * (a * a + 1.0)


_GELU_K = 0.7978845608028654
_GELU_C = 0.044715


def _gelu(u):
    th = jnp.tanh(_GELU_K * (u + _GELU_C * u * u * u))
    return 0.5 * u * (1.0 + th), th


def _gelu_grad(u, th):
    return 0.5 * (1.0 + th) + 0.5 * u * (1.0 - th * th) * _GELU_K * (1.0 + 3.0 * _GELU_C * u * u)


def _group_mean(v, avg):
    hi = v.astype(BF16)
    lo = (v - hi.astype(F32)).astype(BF16)
    return _dot(hi, avg) + _dot(lo, avg)


def _colsum(v):
    return jnp.sum(v, axis=0, keepdims=True)


def _rowmean(v):
    return jnp.mean(v, axis=-1, keepdims=True)


def _load_packed(wpack_hbm, off, rows, dst, sem):
    copies = [
        pltpu.make_async_copy(wpack_hbm.at[d, pl.ds(off, rows), :], dst.at[pl.ds(d * rows, rows), :], sem)
        for d in range(N_DEV)
    ]
    for cp in copies:
        cp.start()
    return copies


def _scan_groups(n_groups, a_ref, b_ref, out_ref, carry_ref, reverse):
    row = lax.broadcasted_iota(jnp.int32, (HALO, W), 0)

    def step(k, carry):
        g = (n_groups - 1 - k) if reverse else k
        rows = pl.ds(pl.multiple_of(g * HALO, HALO), HALO)
        a = a_ref[rows, :]
        b = b_ref[rows, :]
        for s in (1, 2, 4):
            if reverse:
                keep = row < HALO - s
                sh = HALO - s
            else:
                keep = row >= s
                sh = s
            a_sh = pltpu.roll(a, sh, axis=0)
            b_sh = pltpu.roll(b, sh, axis=0)
            b = jnp.where(keep, a * b_sh + b, b)
            a = jnp.where(keep, a * a_sh, a)
        h = b + a * carry
        out_ref[rows, :] = h
        edge = h[0:1, :] if reverse else h[HALO - 1:HALO, :]
        return jnp.broadcast_to(edge, (HALO, W))

    carry_ref[...] = lax.fori_loop(0, n_groups, step, carry_ref[...])


def _chip_gather_copies(block_hbm, out_hbm, send_sems, recv_sems):
    me = _position()

    def copy(j, k, slot_of):
        return pltpu.make_async_remote_copy(
            src_ref=block_hbm, dst_ref=out_hbm.at[_linear(slot_of)], send_sem=send_sems.at[j], recv_sem=recv_sems.at[j],
            device_id=_flip(me, k), device_id_type=MESH)

    sends = [copy(j, k, me) for j, k in enumerate(CHIP_FLIPS)]
    arrivals = [copy(j, k, _flip(me, k)) for j, k in enumerate(CHIP_FLIPS)]
    return sends, arrivals


def _mixer_fwd(x, modraw, adab, g1, wl, bl, bda, bdx, ba, bxb, ap, ws, gl, gc, avg, wpack, mlp_block):
    t_len = x.shape[0]
    tb = TB_MIX
    nb = t_len // tb

    def body(x_ref, modraw_ref, adab_ref, g1_ref, wl_ref, bl_ref, bda_ref, bdx_ref, ba_ref, bxb_ref, ap_ref,
             ws_ref, gl_ref, gc_ref, avg_ref, wpack_hbm, block_hbm, proj_ref, hl_ref, mixed_ref, wmlp_hbm,
             win_v, wout_v, sem, ulx_ext, cv_ext, hcar, a_s, b_s, send_sems, recv_sems, local_sem):
        i = pl.program_id(0)
        own = pltpu.make_async_copy(block_hbm, wmlp_hbm.at[_linear(_position())], local_sem)
        sends, arrivals = _chip_gather_copies(block_hbm, wmlp_hbm, send_sems, recv_sems)

        @pl.when(i == 0)
        def _():
            own.start()
            for cp in sends:
                cp.start()
            cps = _load_packed(wpack_hbm, OFF_WIN, ROWS_WIN, win_v, sem.at[0])
            cps += _load_packed(wpack_hbm, OFF_WOUT, ROWS_WOUT, wout_v, sem.at[1])
            ulx_ext[0:HALO, :] = jnp.zeros((HALO, W), F32)
            cv_ext[0:HALO, :] = jnp.zeros((HALO, W), F32)
            hcar[...] = jnp.zeros((HALO, W), F32)
            for cp in cps:
                cp.wait()

        mod = modraw_ref[...] + adab_ref[...]
        shift1, scale1, gate1 = mod[0:1], mod[1:2], mod[2:3]
        x = x_ref[...]
        r1 = lax.rsqrt(_rowmean(x * x) + EPS)
        h = (x * r1 * g1_ref[...]) * (1.0 + scale1) + shift1
        proj = _dot_nt(h.astype(BF16), win_v[...])
        proj_ref[...] = proj
        u_lx, u_ly, u_b, u_c, u_v = (proj[:, k * W:(k + 1) * W] for k in range(5))

        ulx_ext[HALO:HALO + tb, :] = u_lx
        xl = bl_ref[...] + wl_ref[CONV_L - 1:CONV_L, :] * u_lx
        for k in range(CONV_L - 1):
            xl = xl + wl_ref[k:k + 1, :] * ulx_ext[pl.ds(HALO - (CONV_L - 1) + k, tb), :]
        ulx_ext[0:HALO, :] = ulx_ext[tb:tb + HALO, :]
        xlb = xl.astype(BF16)
        r = _sigmoid(_dot(xlb, bda_ref[...]) + ba_ref[...])
        ig = _sigmoid(_dot(xlb, bdx_ref[...]) + bxb_ref[...])
        log_a = (-C_GATE) * r * _softplus(ap_ref[...])
        a = jnp.exp(log_a)
        mult = jnp.sqrt(_one_minus_sq(a, log_a))
        grow = i * tb + lax.broadcasted_iota(jnp.int32, (tb, W), 0)
        mult = jnp.where(grow == 0, 1.0, mult)
        a_s[...] = a
        b_s[...] = mult * (ig * xl)
        _scan_groups(tb // HALO, a_s, b_s, hl_ref, hcar, reverse=False)
        hl = hl_ref[...]
        ge, _ = _gelu(u_ly)
        p = ge * hl
        y_lru = p * lax.rsqrt(_group_mean(p * p, avg_ref[...]) + EPS) * gl_ref[...]

        cv = u_c * u_v
        cv_ext[HALO:HALO + tb, :] = cv
        cc = ws_ref[CONV_S - 1:CONV_S, :] * cv
        for k in range(CONV_S - 1):
            cc = cc + ws_ref[k:k + 1, :] * cv_ext[pl.ds(HALO - (CONV_S - 1) + k, tb), :]
        cv_ext[0:HALO, :] = cv_ext[tb:tb + HALO, :]
        q = u_b * cc
        y_conv = q * lax.rsqrt(_group_mean(q * q, avg_ref[...]) + EPS) * gc_ref[...]

        mixed_ref[...] = (_dot(y_lru.astype(BF16), wout_v[0:W, :]) + _dot(y_conv.astype(BF16), wout_v[W:2 * W, :]))

        @pl.when(i == nb - 1)
        def _():
            for cp in arrivals:
                cp.wait_recv()
            for cp in sends:
                cp.wait_send()
            own.wait()

    tok = lambda cols: pl.BlockSpec((tb, cols), lambda i: (i, 0))
    full = lambda a: pl.BlockSpec(a.shape, lambda i: (0,) * a.ndim)
    small = (modraw, adab, g1, wl, bl, bda, bdx, ba, bxb, ap, ws, gl, gc, avg)
    n_chips = len(CHIP_FLIPS)
    return pl.pallas_call(
        body,
        name="mixer_fwd",
        grid=(nb,),
        in_specs=[tok(D)] + [full(a) for a in small] + [ANY, ANY],
        out_specs=[tok(D_IN), tok(W), tok(D), ANY],
        out_shape=[jax.ShapeDtypeStruct((t_len, D_IN), F32), jax.ShapeDtypeStruct((t_len, W), F32),
                   jax.ShapeDtypeStruct((t_len, D), F32), jax.ShapeDtypeStruct((N_DEV,) + mlp_block.shape, BF16)],
        scratch_shapes=[pltpu.VMEM((D_IN, D), BF16), pltpu.VMEM((D, D), BF16), pltpu.SemaphoreType.DMA((2,)),
                        pltpu.VMEM((tb + HALO, W), F32), pltpu.VMEM((tb + HALO, W), F32), pltpu.VMEM((HALO, W), F32),
                        pltpu.VMEM((tb, W), F32), pltpu.VMEM((tb, W), F32),
                        pltpu.SemaphoreType.DMA((n_chips,)), pltpu.SemaphoreType.DMA((n_chips,)), pltpu.SemaphoreType.DMA],
        compiler_params=pltpu.CompilerParams(dimension_semantics=("arbitrary",), vmem_limit_bytes=VMEM_LIMIT),
    )(x, *small, wpack, mlp_block)


def _sibling_forward(wmlp):
    def body(in_hbm, out_hbm, send_sems, recv_sems):
        x, y, c = _position()
        copies, arrivals = [], []
        for j, k in enumerate((0,) + CHIP_FLIPS):
            mine = out_hbm.at[_linear(_flip((x, y, c), k))]
            theirs = out_hbm.at[_linear(_flip((x, y, 1 - c), k))]
            copies.append(pltpu.make_async_remote_copy(
                src_ref=mine, dst_ref=mine, send_sem=send_sems.at[j], recv_sem=recv_sems.at[j],
                device_id=(x, y, 1 - c), device_id_type=MESH))
            arrivals.append(pltpu.make_async_remote_copy(
                src_ref=theirs, dst_ref=theirs, send_sem=send_sems.at[j], recv_sem=recv_sems.at[j],
                device_id=(x, y, 1 - c), device_id_type=MESH))
        for cp in copies:
            cp.start()
        for cp in arrivals:
            cp.wait_recv()
        for cp in copies:
            cp.wait_send()

    return pl.pallas_call(
        body,
        name="sibling_forward",
        in_specs=[ANY],
        out_specs=ANY,
        out_shape=jax.ShapeDtypeStruct(wmlp.shape, wmlp.dtype),
        input_output_aliases={0: 0},
        scratch_shapes=[pltpu.SemaphoreType.DMA((4,)), pltpu.SemaphoreType.DMA((4,))],
    )(wmlp)


def _mlp_fwd(x, mixed, tgt, modraw, adab, g2, gf, wpack):
    t_len = x.shape[0]
    tb = TB_MLP
    nb = t_len // tb

    def body(x_ref, mixed_ref, tgt_ref, modraw_ref, adab_ref, g2_ref, gf_ref, wpack_hbm,
             h2t_ref, f_ref, dx2_ref, dz_ref, dzt_ref, vec_ref, loss_ref, w1t_v, w2_v, sem):
        i = pl.program_id(0)

        @pl.when(i == 0)
        def _():
            cps = _load_packed(wpack_hbm, OFF_W1T, ROWS_W1T, w1t_v, sem.at[0])
            cps += _load_packed(wpack_hbm, OFF_W2, ROWS_W2, w2_v, sem.at[1])
            vec_ref[...] = jnp.zeros(vec_ref.shape, F32)
            loss_ref[...] = jnp.zeros(loss_ref.shape, F32)
            for cp in cps:
                cp.wait()

        mod = modraw_ref[...] + adab_ref[...]
        gate1, shift2, scale2, gate2 = mod[2:3], mod[3:4], mod[4:5], mod[5:6]
        x1 = x_ref[...] + gate1 * mixed_ref[...]
        r2 = lax.rsqrt(_rowmean(x1 * x1) + EPS)
        h2 = (x1 * r2 * g2_ref[...]) * (1.0 + scale2) + shift2
        h2b = h2.astype(BF16)
        h2t_ref[...] = h2.T.astype(BF16)
        z = jnp.zeros((tb, D), F32)
        for j in range(N_DEV):
            cols = slice(j * FF_BLK, (j + 1) * FF_BLK)
            fj = _dot_nt(h2b, w1t_v[cols, :])
            f_ref[:, cols] = fj
            rf = jnp.maximum(fj, 0.0)
            z = z + _dot((rf * rf).astype(BF16), w2_v[cols, :])
        x2 = x1 + gate2 * z
        r3 = lax.rsqrt(_rowmean(x2 * x2) + EPS)
        xn3 = x2 * r3
        diff = xn3 * gf_ref[...] - tgt_ref[...]
        sq = _colsum(diff * diff)
        loss_ref[...] += jnp.broadcast_to(jnp.sum(sq, axis=1, keepdims=True) * (0.5 / D), loss_ref.shape)
        dy = diff * (1.0 / D)
        dyn = dy * gf_ref[...]
        dx2 = r3 * (dyn - xn3 * _rowmean(dyn * xn3))
        dx2_ref[...] = dx2
        dz = gate2 * dx2
        dz_ref[...] = dz.astype(BF16)
        dzt_ref[...] = dz.T.astype(BF16)
        vec_ref[0:1, :] += _colsum(dx2 * z)
        vec_ref[1:2, :] += _colsum(dy * xn3)

    tok = lambda cols: pl.BlockSpec((tb, cols), lambda i: (i, 0))
    tok_t = pl.BlockSpec((D, tb), lambda i: (0, i))
    full = lambda a: pl.BlockSpec(a.shape, lambda i: (0,) * a.ndim)
    small = (modraw, adab, g2, gf)
    return pl.pallas_call(
        body,
        name="mlp_fwd",
        grid=(nb,),
        in_specs=[tok(D), tok(D), tok(D)] + [full(a) for a in small] + [ANY],
        out_specs=[tok_t, tok(D_FF), tok(D), tok(D), tok_t, pl.BlockSpec((8, D), lambda i: (0, 0)),
                   pl.BlockSpec((8, 128), lambda i: (0, 0))],
        out_shape=[jax.ShapeDtypeStruct((D, t_len), BF16), jax.ShapeDtypeStruct((t_len, D_FF), F32),
                   jax.ShapeDtypeStruct((t_len, D), F32), jax.ShapeDtypeStruct((t_len, D), BF16),
                   jax.ShapeDtypeStruct((D, t_len), BF16),
                   jax.ShapeDtypeStruct((8, D), F32), jax.ShapeDtypeStruct((8, 128), F32)],
        scratch_shapes=[pltpu.VMEM((D_FF, D), BF16), pltpu.VMEM((D_FF, D), BF16), pltpu.SemaphoreType.DMA((2,))],
        compiler_params=pltpu.CompilerParams(dimension_semantics=("arbitrary",), vmem_limit_bytes=VMEM_LIMIT),
    )(x, mixed, tgt, *small, wpack)


def _mlp_bwd(h2t, f, dz, dzt, wpack):
    t_len = dz.shape[0]
    tb = TB_MLPB
    nb = t_len // tb

    def body(h2t_ref, f_ref, dz_ref, dzt_ref, w1t_ref, w2_ref, dh2_ref, dw1_ref, dw2t_ref):
        j = pl.program_id(0)
        t = pl.program_id(1)
        rows = pl.ds(pl.multiple_of(t * tb, tb), tb)
        w1t = w1t_ref[0]
        w2 = w2_ref[0]
        rf = jnp.maximum(f_ref[...], 0.0)
        a2 = (rf * rf).astype(BF16)
        df = (_dot_nt(dz_ref[...], w2) * (2.0 * rf)).astype(BF16)
        g2 = _dot(dzt_ref[...], a2)
        g1 = _dot(h2t_ref[...], df)
        dh = _dot(df, w1t)

        @pl.when(t == 0)
        def _():
            dw2t_ref[0] = g2
            dw1_ref[0] = g1

        @pl.when(t != 0)
        def _():
            dw2t_ref[0] += g2
            dw1_ref[0] += g1

        @pl.when(j == 0)
        def _():
            dh2_ref[rows, :] = dh

        @pl.when(j != 0)
        def _():
            dh2_ref[rows, :] += dh

    return pl.pallas_call(
        body,
        name="mlp_bwd",
        grid=(N_DEV, nb),
        in_specs=[pl.BlockSpec((D, tb), lambda j, t: (0, t)),
                  pl.BlockSpec((tb, FF_BLK), lambda j, t: (t, j)),
                  pl.BlockSpec((tb, D), lambda j, t: (t, 0)),
                  pl.BlockSpec((D, tb), lambda j, t: (0, t)),
                  pl.BlockSpec((1, ROWS_W1T, D), lambda j, t: (j, OFF_W1T // ROWS_W1T, 0)),
                  pl.BlockSpec((1, ROWS_W2, D), lambda j, t: (j, OFF_W2 // ROWS_W2, 0))],
        out_specs=[pl.BlockSpec((t_len, D), lambda j, t: (0, 0)),
                   pl.BlockSpec((1, D, FF_BLK), lambda j, t: (j, 0, 0)),
                   pl.BlockSpec((1, D, FF_BLK), lambda j, t: (j, 0, 0))],
        out_shape=[jax.ShapeDtypeStruct((t_len, D), F32), jax.ShapeDtypeStruct((N_DEV, D, FF_BLK), F32),
                   jax.ShapeDtypeStruct((N_DEV, D, FF_BLK), F32)],
        compiler_params=pltpu.CompilerParams(dimension_semantics=("arbitrary", "arbitrary"),
                                             vmem_limit_bytes=VMEM_LIMIT),
    )(h2t, f, dz, dzt, wpack, wpack)


V_SHIFT1, V_SCALE1, V_GATE1, V_SHIFT2, V_SCALE2, V_G1, V_G2 = 0, 1, 2, 3, 4, 6, 7
V_BL_BA, V_BX_SP, V_GL_GC, V_WL01, V_WL23, V_WS01, V_WS2 = 8, 9, 10, 11, 12, 13, 14
V_ROWS = 16


def _chip_scatter_copies(srcs, dsts, send_sems, recv_sems):
    me = _position()
    copies = []
    for a, (src, dst) in enumerate(zip(srcs, dsts)):
        for j, k in enumerate(CHIP_FLIPS):
            peer = _flip(me, k)
            copies.append(pltpu.make_async_remote_copy(
                src_ref=src.at[2 * peer[0] + peer[1]], dst_ref=dst.at[j], send_sem=send_sems.at[len(CHIP_FLIPS) * a + j],
                recv_sem=recv_sems.at[len(CHIP_FLIPS) * a + j], device_id=peer, device_id_type=MESH))
    return copies


def _mixer_bwd(x, mixed, dh2, dx2, proj, hl, modraw, adab, g1, g2, wl, bl, bda, bdx, ba, bxb, ap, ws, gl, gc, avg, wpack,
               chip_sums):
    t_len = x.shape[0]
    tb = TB_MIXB
    nb = t_len // tb
    hb = tb // HALO
    n_sums = len(chip_sums)

    def body(x_ref, mixed_ref, dh2_ref, dx2_ref, proj_ref, projh_ref, hl_ref, hlh_ref,
             modraw_ref, adab_ref, g1_ref, g2_ref, wl_ref, bl_ref, bda_ref, bdx_ref, ba_ref, bxb_ref, ap_ref,
             ws_ref, gl_ref, gc_ref, avg_ref, wpack_hbm, *rest):
        sums_hbm, rest = rest[:n_sums], rest[n_sums:]
        gx_ref, vec_ref, hb_ref, dprojt_ref, dmixed_ref, ycatt_ref, xlt_ref, dgate_ref = rest[:8]
        landed_hbm, rest = rest[8:8 + n_sums], rest[8 + n_sums:]
        (win_v, wout_v, sem, ulx_ext, cv_ext, hl_ext, a_ext, dxl_ext, dcc_ext, dcar, an_s, g_s, dh_s,
         send_sems, recv_sems) = rest
        i = pl.program_id(0)
        blk = nb - 1 - i
        chip_copies = _chip_scatter_copies(sums_hbm, landed_hbm, send_sems, recv_sems)

        @pl.when(i == 0)
        def _():
            for cp in chip_copies:
                cp.start()
            cps = _load_packed(wpack_hbm, OFF_WIN, ROWS_WIN, win_v, sem.at[0])
            cps += _load_packed(wpack_hbm, OFF_WOUT, ROWS_WOUT, wout_v, sem.at[1])
            vec_ref[...] = jnp.zeros(vec_ref.shape, F32)
            zero = jnp.zeros((HALO, W), F32)
            a_ext[tb:tb + HALO, :] = zero
            dxl_ext[tb:tb + HALO, :] = zero
            dcc_ext[tb:tb + HALO, :] = zero
            dcar[...] = zero
            for cp in cps:
                cp.wait()

        mod = modraw_ref[...] + adab_ref[...]
        shift1, scale1, gate1, scale2 = mod[0:1], mod[1:2], mod[2:3], mod[4:5]
        x = x_ref[...]
        mixed = mixed_ref[...]

        x1 = x + gate1 * mixed
        r2 = lax.rsqrt(_rowmean(x1 * x1) + EPS)
        xn2 = x1 * r2
        dh2 = dh2_ref[...]
        vec_ref[V_SHIFT2:V_SHIFT2 + 1, :] += _colsum(dh2)
        vec_ref[V_SCALE2:V_SCALE2 + 1, :] += _colsum(dh2 * xn2 * g2_ref[...])
        vec_ref[V_G2:V_G2 + 1, :] += _colsum(dh2 * (1.0 + scale2) * xn2)
        dxn2 = dh2 * g2_ref[...] * (1.0 + scale2)
        dx1 = dx2_ref[...] + r2 * (dxn2 - xn2 * _rowmean(dxn2 * xn2))
        vec_ref[V_GATE1:V_GATE1 + 1, :] += _colsum(dx1 * mixed)
        dmixed = (gate1 * dx1).astype(BF16)

        proj = proj_ref[...]
        u_lx, u_ly, u_b, u_c, u_v = (proj[:, k * W:(k + 1) * W] for k in range(5))
        has_prev = (blk > 0).astype(F32)
        projh = projh_ref[...]
        ulx_ext[0:HALO, :] = projh[:, 0:W] * has_prev
        ulx_ext[HALO:HALO + tb, :] = u_lx
        xl = bl_ref[...] + wl_ref[CONV_L - 1:CONV_L, :] * u_lx
        for k in range(CONV_L - 1):
            xl = xl + wl_ref[k:k + 1, :] * ulx_ext[pl.ds(HALO - (CONV_L - 1) + k, tb), :]
        xlb = xl.astype(BF16)
        r = _sigmoid(_dot(xlb, bda_ref[...]) + ba_ref[...])
        ig = _sigmoid(_dot(xlb, bdx_ref[...]) + bxb_ref[...])
        sp = _softplus(ap_ref[...])
        log_a = (-C_GATE) * r * sp
        a = jnp.exp(log_a)
        mult_raw = jnp.sqrt(_one_minus_sq(a, log_a))
        first = (blk * tb + lax.broadcasted_iota(jnp.int32, (tb, W), 0)) == 0
        mult = jnp.where(first, 1.0, mult_raw)
        hl = hl_ref[...]
        ge, th = _gelu(u_ly)
        p = ge * hl
        rp = lax.rsqrt(_group_mean(p * p, avg_ref[...]) + EPS)
        pn = p * rp
        cv = u_c * u_v
        cv_ext[0:HALO, :] = projh[:, 3 * W:4 * W] * projh[:, 4 * W:5 * W] * has_prev
        cv_ext[HALO:HALO + tb, :] = cv
        cc = ws_ref[CONV_S - 1:CONV_S, :] * cv
        for k in range(CONV_S - 1):
            cc = cc + ws_ref[k:k + 1, :] * cv_ext[pl.ds(HALO - (CONV_S - 1) + k, tb), :]
        q = u_b * cc
        rq = lax.rsqrt(_group_mean(q * q, avg_ref[...]) + EPS)
        qn = q * rq

        dmixed_ref[...] = dmixed
        ycatt_ref[0:W, :] = (pn * gl_ref[...]).T.astype(BF16)
        ycatt_ref[W:2 * W, :] = (qn * gc_ref[...]).T.astype(BF16)
        dyl = _dot_nt(dmixed, wout_v[0:W, :])
        dyc = _dot_nt(dmixed, wout_v[W:2 * W, :])

        dqn = dyc * gc_ref[...]
        dq = rq * (dqn - qn * _group_mean(dqn * qn, avg_ref[...]))
        du_b = dq * cc
        dcc = dq * u_b
        dcc_ext[0:tb, :] = dcc
        dcv = ws_ref[CONV_S - 1:CONV_S, :] * dcc
        for k in range(CONV_S - 1):
            dcv = dcv + ws_ref[k:k + 1, :] * dcc_ext[pl.ds(CONV_S - 1 - k, tb), :]
        dcc_ext[tb:tb + HALO, :] = dcc_ext[0:HALO, :]
        du_c = dcv * u_v
        du_v = dcv * u_c
        dws = [_colsum(dcc * cv_ext[pl.ds(HALO - (CONV_S - 1) + k, tb), :]) for k in range(CONV_S)]

        dpn = dyl * gl_ref[...]
        dp = rp * (dpn - pn * _group_mean(dpn * pn, avg_ref[...]))
        du_ly = dp * hl * _gelu_grad(u_ly, th)
        g_s[...] = dp * ge
        a_ext[0:tb, :] = a
        an_s[...] = a_ext[pl.ds(1, tb), :]
        _scan_groups(hb, an_s, g_s, dh_s, dcar, reverse=True)
        a_ext[tb:tb + HALO, :] = a_ext[0:HALO, :]
        dh = dh_s[...]
        hl_ext[0:HALO, :] = hlh_ref[...] * has_prev
        hl_ext[HALO:HALO + tb, :] = hl
        da = dh * hl_ext[pl.ds(HALO - 1, tb), :]
        dmult = dh * (ig * xl)
        dig = dh * (mult * xl)
        dxl = dh * (mult * ig)
        dlog = da * a - jnp.where(first, 0.0, dmult * (a * a) / mult_raw)
        dr = dlog * ((-C_GATE) * sp)
        dsp = _colsum(dlog * ((-C_GATE) * r))
        dga = dr * r * (1.0 - r)
        dgx = dig * ig * (1.0 - ig)
        dgab = dga.astype(BF16)
        dgxb = dgx.astype(BF16)
        xlt_ref[...] = xl.T.astype(BF16)
        dgate_ref[:, 0:W] = dgab
        dgate_ref[:, W:2 * W] = dgxb
        dxl = dxl + _dot_nt(dgab, bda_ref[...]) + _dot_nt(dgxb, bdx_ref[...])
        dxl_ext[0:tb, :] = dxl
        du_lx = wl_ref[CONV_L - 1:CONV_L, :] * dxl
        for k in range(CONV_L - 1):
            du_lx = du_lx + wl_ref[k:k + 1, :] * dxl_ext[pl.ds(CONV_L - 1 - k, tb), :]
        dxl_ext[tb:tb + HALO, :] = dxl_ext[0:HALO, :]
        dwl = [_colsum(dxl * ulx_ext[pl.ds(HALO - (CONV_L - 1) + k, tb), :]) for k in range(CONV_L)]

        cat = lambda u, v: jnp.concatenate([u, v], axis=1)
        vec_ref[V_BL_BA:V_BL_BA + 1, :] += cat(_colsum(dxl), _colsum(dga))
        vec_ref[V_BX_SP:V_BX_SP + 1, :] += cat(_colsum(dgx), dsp)
        vec_ref[V_GL_GC:V_GL_GC + 1, :] += cat(_colsum(dyl * pn), _colsum(dyc * qn))
        vec_ref[V_WL01:V_WL01 + 1, :] += cat(dwl[0], dwl[1])
        vec_ref[V_WL23:V_WL23 + 1, :] += cat(dwl[2], dwl[3])
        vec_ref[V_WS01:V_WS01 + 1, :] += cat(dws[0], dws[1])
        vec_ref[V_WS2:V_WS2 + 1, 0:W] += dws[2]

        r1 = lax.rsqrt(_rowmean(x * x) + EPS)
        xn1 = x * r1
        hb_ref[...] = ((xn1 * g1_ref[...]) * (1.0 + scale1) + shift1).astype(BF16)
        dh_in = jnp.zeros((tb, D), F32)
        for k, du in enumerate((du_lx, du_ly, du_b, du_c, du_v)):
            dprojt_ref[k * W:(k + 1) * W, :] = du.T.astype(BF16)
            dh_in = dh_in + _dot(du.astype(BF16), win_v[k * W:(k + 1) * W, :])
        vec_ref[V_SHIFT1:V_SHIFT1 + 1, :] += _colsum(dh_in)
        vec_ref[V_SCALE1:V_SCALE1 + 1, :] += _colsum(dh_in * xn1 * g1_ref[...])
        vec_ref[V_G1:V_G1 + 1, :] += _colsum(dh_in * (1.0 + scale1) * xn1)
        dxn1 = dh_in * g1_ref[...] * (1.0 + scale1)
        gx_ref[...] = dx1 + r1 * (dxn1 - xn1 * _rowmean(dxn1 * xn1))

        @pl.when(i == nb - 1)
        def _():
            for cp in chip_copies:
                cp.wait_recv()
            for cp in chip_copies:
                cp.wait_send()

    rev = lambda cols: pl.BlockSpec((tb, cols), lambda i: (nb - 1 - i, 0))
    rev_t = lambda rows: pl.BlockSpec((rows, tb), lambda i: (0, nb - 1 - i))
    halo = lambda cols: pl.BlockSpec((HALO, cols), lambda i: (jnp.maximum((nb - 1 - i) * hb - 1, 0), 0))
    full = lambda a: pl.BlockSpec(a.shape, lambda i: (0,) * a.ndim)
    small = (modraw, adab, g1, g2, wl, bl, bda, bdx, ba, bxb, ap, ws, gl, gc, avg)
    ext = pltpu.VMEM((tb + HALO, W), F32)
    n_sems = len(CHIP_FLIPS) * n_sums
    return pl.pallas_call(
        body,
        name="mixer_bwd",
        grid=(nb,),
        in_specs=[rev(D), rev(D), rev(D), rev(D), rev(D_IN), halo(D_IN), rev(W), halo(W)]
        + [full(a) for a in small] + [ANY] * (1 + n_sums),
        out_specs=[rev(D), pl.BlockSpec((V_ROWS, D), lambda i: (0, 0)), rev(D), rev_t(D_IN), rev(D), rev_t(D),
                   rev_t(W), rev(2 * W)] + [ANY] * n_sums,
        out_shape=[jax.ShapeDtypeStruct((t_len, D), F32), jax.ShapeDtypeStruct((V_ROWS, D), F32),
                   jax.ShapeDtypeStruct((t_len, D), BF16), jax.ShapeDtypeStruct((D_IN, t_len), BF16),
                   jax.ShapeDtypeStruct((t_len, D), BF16), jax.ShapeDtypeStruct((D, t_len), BF16),
                   jax.ShapeDtypeStruct((W, t_len), BF16), jax.ShapeDtypeStruct((t_len, 2 * W), BF16)]
        + [jax.ShapeDtypeStruct((len(CHIP_FLIPS),) + s.shape[1:], s.dtype) for s in chip_sums],
        scratch_shapes=[pltpu.VMEM((D_IN, D), BF16), pltpu.VMEM((D, D), BF16), pltpu.SemaphoreType.DMA((2,)),
                        ext, ext, ext, ext, ext, ext, pltpu.VMEM((HALO, W), F32),
                        pltpu.VMEM((tb, W), F32), pltpu.VMEM((tb, W), F32), pltpu.VMEM((tb, W), F32),
                        pltpu.SemaphoreType.DMA((n_sems,)), pltpu.SemaphoreType.DMA((n_sems,))],
        compiler_params=pltpu.CompilerParams(dimension_semantics=("arbitrary",), vmem_limit_bytes=VMEM_LIMIT),
    )(x, mixed, dh2, dx2, proj, proj, hl, hl, *small, wpack, *chip_sums)


def _matmul(name, a, b, tm=512):
    m, k = a.shape
    n = b.shape[1]

    def body(a_ref, b_ref, o_ref):
        o_ref[...] = _dot(a_ref[...], b_ref[...])

    return pl.pallas_call(
        body,
        name=name,
        grid=(m // tm,),
        in_specs=[pl.BlockSpec((tm, k), lambda i: (i, 0)), pl.BlockSpec((k, n), lambda i: (0, 0))],
        out_specs=pl.BlockSpec((tm, n), lambda i: (i, 0)),
        out_shape=jax.ShapeDtypeStruct((m, n), F32),
        compiler_params=pltpu.CompilerParams(dimension_semantics=("arbitrary",), vmem_limit_bytes=VMEM_LIMIT),
    )(a, b)


def _block_diag(w):
    n, m, _ = w.shape
    eye = jnp.eye(n, dtype=w.dtype)
    return (w[:, :, None, :] * eye[:, None, :, None]).reshape(n * m, n * m)


def _diag_blocks(mat, n=8):
    m = mat.shape[0] // n
    return jnp.stack([mat[h * m:(h + 1) * m, h * m:(h + 1) * m] for h in range(n)])


def _pad_rows(a, rows):
    return jnp.pad(a, ((0, rows - a.shape[0]),) + ((0, 0),) * (a.ndim - 1))


def _position():
    return lax.axis_index("x"), lax.axis_index("y"), lax.axis_index("c")


def _linear(pos):
    return 4 * pos[0] + 2 * pos[1] + pos[2]


def _flip(pos, k):
    return tuple(1 - p if k & bit else p for p, bit in zip(pos, (4, 2, 1)))


def _exchange_all(make_copy, make_arrival):
    copies = [make_copy(k) for k in range(1, N_DEV)]
    for cp in copies:
        cp.start()
    for k in range(1, N_DEV):
        make_arrival(k).wait_recv()
    for cp in copies:
        cp.wait_send()


def _mod_exchange(msg, ada_w):
    cols = ada_w.shape[1]

    def body(msg_ref, adaw_ref, gath_ref, mod_ref, sendbuf, send_a, recv_a, send_b, recv_b):
        me = _position()
        me_lin = _linear(me)
        m = msg_ref[...]
        row = lax.broadcasted_iota(jnp.int32, m.shape, 0)
        gath_ref[me_lin] = jnp.where(row == 0, m * _sigmoid(m), m)

        def gather_copy(k, src_lin):
            return pltpu.make_async_remote_copy(
                src_ref=gath_ref.at[src_lin], dst_ref=gath_ref.at[src_lin], send_sem=send_a.at[k - 1],
                recv_sem=recv_a.at[k - 1], device_id=_flip(me, k), device_id_type=MESH)

        _exchange_all(lambda k: gather_copy(k, me_lin), lambda k: gather_copy(k, _linear(_flip(me, k))))

        sc_all = gath_ref[:, 0, :]
        scb = jnp.concatenate([sc_all, jnp.zeros_like(sc_all)], axis=0).astype(BF16)
        prod = _dot(scb, adaw_ref[...].astype(BF16))
        for b in range(N_DEV):
            sendbuf[b] = jnp.broadcast_to(prod[b:b + 1, :], (HALO, cols))
        mod_ref[me_lin] = sendbuf[me_lin]

        def row_copy(k, dst_lin):
            peer = _flip(me, k)
            return pltpu.make_async_remote_copy(
                src_ref=sendbuf.at[_linear(peer)], dst_ref=mod_ref.at[dst_lin], send_sem=send_b.at[k - 1],
                recv_sem=recv_b.at[k - 1], device_id=peer, device_id_type=MESH)

        _exchange_all(lambda k: row_copy(k, me_lin), lambda k: row_copy(k, _linear(_flip(me, k))))

    return pl.pallas_call(
        body,
        name="mod_exchange",
        in_specs=[WHOLE, WHOLE],
        out_specs=[WHOLE, WHOLE],
        out_shape=[jax.ShapeDtypeStruct((N_DEV, HALO, D), F32), jax.ShapeDtypeStruct((N_DEV, HALO, cols), F32)],
        scratch_shapes=[pltpu.VMEM((N_DEV, HALO, cols), F32)] + [pltpu.SemaphoreType.DMA((N_DEV - 1,))] * 4,
        compiler_params=pltpu.CompilerParams(vmem_limit_bytes=VMEM_LIMIT),
    )(msg, ada_w)


def _weight_gather(block):
    rows, cols = block.shape

    def body(x_ref, out_ref, send_sems, recv_sems, local_sem):
        x, y, c = _position()
        me, sibling = (x, y, c), (x, y, 1 - c)
        chips = [(1 - x, y), (x, 1 - y), (1 - x, 1 - y)]

        def copy(k, block_of, to, src=None):
            dst = out_ref.at[_linear(block_of)]
            return pltpu.make_async_remote_copy(
                src_ref=dst if src is None else src, dst_ref=dst, send_sem=send_sems.at[k], recv_sem=recv_sems.at[k],
                device_id=to, device_id_type=MESH)

        mine = pltpu.make_async_copy(x_ref, out_ref.at[_linear(me)], local_sem)
        mine.start()
        first = [copy(0, me, sibling, src=x_ref)]
        first += [copy(1 + j, me, (*chip, c), src=x_ref) for j, chip in enumerate(chips)]
        for cp in first:
            cp.start()
        passed = [copy(4 + j, (*chip, c), sibling) for j, chip in enumerate(chips)]
        for j, chip in enumerate(chips):
            copy(1 + j, (*chip, c), me).wait_recv()
            passed[j].start()
        copy(0, sibling, me).wait_recv()
        for j, chip in enumerate(chips):
            copy(4 + j, (*chip, 1 - c), me).wait_recv()
        for cp in first + passed:
            cp.wait_send()
        mine.wait()

    return pl.pallas_call(
        body,
        name="weight_gather",
        in_specs=[ANY],
        out_specs=ANY,
        out_shape=jax.ShapeDtypeStruct((N_DEV, rows, cols), block.dtype),
        scratch_shapes=[pltpu.SemaphoreType.DMA((7,)), pltpu.SemaphoreType.DMA((7,)), pltpu.SemaphoreType.DMA],
    )(block)


def _grads_to_sibling(which, arrs):
    n = len(arrs)

    def body(*refs):
        srcs, dsts = refs[:n], refs[n:2 * n]
        send_sems, recv_sems = refs[2 * n:]
        x, y, c = _position()
        copies = []
        for a in range(n):
            for k in range(4):
                copies.append(pltpu.make_async_remote_copy(
                    src_ref=srcs[a].at[k, 1 - c], dst_ref=dsts[a].at[k], send_sem=send_sems.at[4 * a + k],
                    recv_sem=recv_sems.at[4 * a + k], device_id=(x, y, 1 - c), device_id_type=MESH))
        for cp in copies:
            cp.start()
        for cp in copies:
            cp.wait_recv()
        for cp in copies:
            cp.wait_send()

    return pl.pallas_call(
        body,
        name=which + "_grads_to_sibling",
        in_specs=[ANY] * n,
        out_specs=[ANY] * n,
        out_shape=[jax.ShapeDtypeStruct((4,) + a.shape[2:], a.dtype) for a in arrs],
        scratch_shapes=[pltpu.SemaphoreType.DMA((4 * n,)), pltpu.SemaphoreType.DMA((4 * n,))],
    )(*arrs)


def _grads_to_chips(arrs):
    n = len(arrs)

    def body(*refs):
        srcs, dsts = refs[:n], refs[n:2 * n]
        send_sems, recv_sems = refs[2 * n:]
        copies = _chip_scatter_copies(srcs, dsts, send_sems, recv_sems)
        for cp in copies:
            cp.start()
        for cp in copies:
            cp.wait_recv()
        for cp in copies:
            cp.wait_send()

    return pl.pallas_call(
        body,
        name="grads_to_chips",
        in_specs=[ANY] * n,
        out_specs=[ANY] * n,
        out_shape=[jax.ShapeDtypeStruct((3,) + a.shape[1:], a.dtype) for a in arrs],
        scratch_shapes=[pltpu.SemaphoreType.DMA((3 * n,)), pltpu.SemaphoreType.DMA((3 * n,))],
    )(*arrs)


def _row_block(rows):
    return 256 if rows % 256 == 0 else rows // 2


def _pair_sum(pos, mine, recv):
    _, _, rows, cols = mine.shape
    rb = _row_block(rows)

    def body(pos_ref, mine_ref, recv_ref, out_ref):
        out_ref[0] = (mine_ref[0, 0] + recv_ref[0]).astype(BF16)

    return pl.pallas_call(
        body,
        name="grad_pair_sum",
        grid_spec=pltpu.PrefetchScalarGridSpec(
            num_scalar_prefetch=1, grid=(4, rows // rb),
            in_specs=[pl.BlockSpec((1, 1, rb, cols), lambda k, r, pos: (k, pos[0], r, 0)),
                      pl.BlockSpec((1, rb, cols), lambda k, r, pos: (k, r, 0))],
            out_specs=pl.BlockSpec((1, rb, cols), lambda k, r, pos: (k, r, 0))),
        out_shape=jax.ShapeDtypeStruct((4, rows, cols), BF16),
        compiler_params=pltpu.CompilerParams(dimension_semantics=("arbitrary", "arbitrary")),
    )(pos, mine, recv)


def _final_sum(pos, mine, recv, chips):
    _, _, rows, cols = mine.shape
    rb = _row_block(rows)

    def body(pos_ref, mine_ref, recv_ref, chips_ref, out_ref):
        g = mine_ref[0, 0] + recv_ref[0]
        for j in range(3):
            g = g + chips_ref[j].astype(F32)
        out_ref[...] = g

    return pl.pallas_call(
        body,
        name="grad_final_sum",
        grid_spec=pltpu.PrefetchScalarGridSpec(
            num_scalar_prefetch=1, grid=(rows // rb,),
            in_specs=[pl.BlockSpec((1, 1, rb, cols), lambda r, pos: (pos[1], pos[0], r, 0)),
                      pl.BlockSpec((1, rb, cols), lambda r, pos: (pos[1], r, 0)),
                      pl.BlockSpec((3, rb, cols), lambda r, pos: (0, r, 0))],
            out_specs=pl.BlockSpec((rb, cols), lambda r, pos: (r, 0))),
        out_shape=jax.ShapeDtypeStruct((rows, cols), F32),
        compiler_params=pltpu.CompilerParams(dimension_semantics=("arbitrary",)),
    )(pos, mine, recv, chips)


LOSS_ROW = V_ROWS + 8
GB_ROW = LOSS_ROW + 7


def _grad_exchange(gmod8, sc_t, msg_vec, msg_gate):
    cols = gmod8.shape[1]

    def body(gmod_ref, sct_ref, vec_ref, gate_ref, gadaw_ref, sumv_ref, sumg_ref, gb_ref,
             sendbuf, grecv, bufv, bufg, send_a, recv_a, send_v, recv_v, send_g, recv_g):
        me = _position()
        me_lin = _linear(me)
        gm = gmod_ref[...]
        for b in range(N_DEV):
            sendbuf[b] = jnp.broadcast_to(gm[b:b + 1, :], (HALO, cols))
        grecv[me_lin] = sendbuf[me_lin]

        def row_copy(k, dst_lin):
            peer = _flip(me, k)
            return pltpu.make_async_remote_copy(
                src_ref=sendbuf.at[_linear(peer)], dst_ref=grecv.at[dst_lin], send_sem=send_a.at[k - 1],
                recv_sem=recv_a.at[k - 1], device_id=peer, device_id_type=MESH)

        _exchange_all(lambda k: row_copy(k, me_lin), lambda k: row_copy(k, _linear(_flip(me, k))))

        g_all = grecv[:, 0, :]
        g_pad = jnp.concatenate([g_all, jnp.zeros((sct_ref.shape[1] - N_DEV, cols), F32)], axis=0).astype(BF16)
        gadaw_ref[...] = _dot(sct_ref[...], g_pad)
        bufv[me_lin] = vec_ref[...]
        bufv[me_lin, GB_ROW:GB_ROW + 1, 0:cols] = _colsum(g_all)
        bufg[me_lin] = gate_ref[...]

        def gather_copy(buf, sends, recvs, k, src_lin):
            return pltpu.make_async_remote_copy(
                src_ref=buf.at[src_lin], dst_ref=buf.at[src_lin], send_sem=sends.at[k - 1], recv_sem=recvs.at[k - 1],
                device_id=_flip(me, k), device_id_type=MESH)

        for buf, sends, recvs in ((bufv, send_v, recv_v), (bufg, send_g, recv_g)):
            _exchange_all(lambda k: gather_copy(buf, sends, recvs, k, me_lin),
                          lambda k: gather_copy(buf, sends, recvs, k, _linear(_flip(me, k))))

        sv = bufv[0]
        sg = bufg[0]
        for d in range(1, N_DEV):
            sv = sv + bufv[d]
            sg = sg + bufg[d]
        sumv_ref[...] = sv
        sumg_ref[...] = sg
        gb_ref[...] = bufv[:, GB_ROW, :]

    return pl.pallas_call(
        body,
        name="grad_exchange",
        in_specs=[WHOLE] * 4,
        out_specs=[WHOLE] * 4,
        out_shape=[jax.ShapeDtypeStruct((D, cols), F32), jax.ShapeDtypeStruct(msg_vec.shape, F32),
                   jax.ShapeDtypeStruct(msg_gate.shape, F32), jax.ShapeDtypeStruct((N_DEV, D), F32)],
        scratch_shapes=[pltpu.VMEM((N_DEV, HALO, cols), F32), pltpu.VMEM((N_DEV, HALO, cols), F32),
                        pltpu.VMEM((N_DEV,) + msg_vec.shape, F32), pltpu.VMEM((N_DEV,) + msg_gate.shape, F32)]
        + [pltpu.SemaphoreType.DMA((N_DEV - 1,))] * 6,
        compiler_params=pltpu.CompilerParams(vmem_limit_bytes=VMEM_LIMIT),
    )(gmod8, sc_t, msg_vec, msg_gate)


def _adamw_math(w, g, m, v):
    m = ADAM_B1 * m + (1.0 - ADAM_B1) * g
    v = ADAM_B2 * v + (1.0 - ADAM_B2) * (g * g)
    m_hat = m / (1.0 - ADAM_B1 ** ADAM_STEP)
    v_hat = v / (1.0 - ADAM_B2 ** ADAM_STEP)
    delta = -ADAM_LR * (m_hat / (jnp.sqrt(v_hat) + ADAM_EPS) + ADAM_WD * w)
    return delta, m, v


def _adamw(name, w, g, m, v):
    rows, cols = w.shape
    rb = 256 if rows % 256 == 0 else rows

    def body(w_ref, g_ref, m_ref, v_ref, d_ref, mo_ref, vo_ref):
        d_ref[...], mo_ref[...], vo_ref[...] = _adamw_math(w_ref[...], g_ref[...], m_ref[...], v_ref[...])

    spec = pl.BlockSpec((rb, cols), lambda r: (r, 0))
    return pl.pallas_call(
        body,
        name="adamw_" + name,
        grid=(rows // rb,),
        in_specs=[spec] * 4,
        out_specs=[spec] * 3,
        out_shape=[jax.ShapeDtypeStruct((rows, cols), F32)] * 3,
        compiler_params=pltpu.CompilerParams(dimension_semantics=("arbitrary",)),
    )(w, g, m, v)


def _adamw_small(ws, gs, ms, vs, sigmoid_scaled):
    n = len(ws)

    def body(*refs):
        w_refs, g_refs, m_refs, v_refs = (refs[i * n:(i + 1) * n] for i in range(4))
        outs = refs[4 * n:]
        for i in range(n):
            w = w_refs[i][...]
            g = g_refs[i][...]
            if sigmoid_scaled[i]:
                g = g * _sigmoid(w)
            delta, m, v = _adamw_math(w, g, m_refs[i][...], v_refs[i][...])
            outs[4 * i][...] = g
            outs[4 * i + 1][...] = delta
            outs[4 * i + 2][...] = m
            outs[4 * i + 3][...] = v

    shapes = [jax.ShapeDtypeStruct(w.shape, F32) for w in ws for _ in range(4)]
    outs = pl.pallas_call(
        body,
        name="adamw_small",
        in_specs=[WHOLE] * (4 * n),
        out_specs=[WHOLE] * (4 * n),
        out_shape=shapes,
    )(*ws, *gs, *ms, *vs)
    return [outs[4 * i:4 * i + 4] for i in range(n)]


_WEIGHT_NAMES = ("ada_w", "ada_b", "norm1_g", "w_in", "lru_conv_w", "lru_conv_b", "gate_a_w", "gate_a_b", "gate_x_w",
                 "gate_x_b", "a_param", "short_conv_w", "lru_out_g", "conv_out_g", "w_out", "norm2_g", "w_mlp1",
                 "w_mlp2", "final_g")
_BIG = ("ada_w", "w_in", "w_out", "w_mlp1", "w_mlp2")


def kernel(x, c, ada_w, ada_b, norm1_g, w_in, lru_conv_w, lru_conv_b, gate_a_w, gate_a_b, gate_x_w, gate_x_b, a_param, short_conv_w, lru_out_g, conv_out_g, w_out, norm2_g, w_mlp1, w_mlp2, final_g, loss_target, m_ada_w, m_ada_b, m_norm1_g, m_w_in, m_lru_conv_w, m_lru_conv_b, m_gate_a_w, m_gate_a_b, m_gate_x_w, m_gate_x_b, m_a_param, m_short_conv_w, m_lru_out_g, m_conv_out_g, m_w_out, m_norm2_g, m_w_mlp1, m_w_mlp2, m_final_g, v_ada_w, v_ada_b, v_norm1_g, v_w_in, v_lru_conv_w, v_lru_conv_b, v_gate_a_w, v_gate_a_b, v_gate_x_w, v_gate_x_b, v_a_param, v_short_conv_w, v_lru_out_g, v_conv_out_g, v_w_out, v_norm2_g, v_w_mlp1, v_w_mlp2, v_final_g):
    given = dict(locals())
    weights = {n: given[n] for n in _WEIGHT_NAMES}
    xi, yi, ci = _position()
    me_lin = _linear((xi, yi, ci))
    hd = W // N_DEV

    msg = (jnp.pad(c, ((0, HALO - 1), (0, 0)))
           + jnp.pad(lru_conv_w[0], ((1, HALO - 1 - CONV_L), (0, D - hd)))
           + jnp.pad(short_conv_w[0], ((1 + CONV_L, 0), (0, D - hd))))
    gath, mod_all = _mod_exchange(msg, ada_w[0])
    sc_all = gath[:, 0, :]
    wl = jnp.transpose(gath[:, 1:1 + CONV_L, :hd], (1, 0, 2)).reshape(CONV_L, W)
    ws = jnp.transpose(gath[:, 1 + CONV_L:HALO, :hd], (1, 0, 2)).reshape(CONV_S, W)
    modraw = _pad_rows(mod_all[:, 0, :].reshape(6, D), HALO)
    adab = _pad_rows(ada_b.reshape(6, D), HALO)

    mixer_block = jnp.concatenate([w_out[0], w_in[0].T], axis=0).astype(BF16)
    mlp_block = jnp.concatenate([w_mlp1[0].T, w_mlp2[0]], axis=0).astype(BF16)
    wmix = _weight_gather(mixer_block)

    x2d, tgt = x[0], loss_target[0]
    gf = final_g.reshape(1, D)
    bda = _block_diag(gate_a_w[0]).astype(BF16)
    bdx = _block_diag(gate_x_w[0]).astype(BF16)
    avg = _block_diag(jnp.full((8, W // 8, W // 8), 8.0 / W, F32)).astype(BF16)
    wl8 = _pad_rows(wl, HALO)
    ws8 = _pad_rows(ws, HALO)
    mixer_small = (wl8, lru_conv_b, bda, bdx, gate_a_b, gate_x_b, a_param, ws8, lru_out_g, conv_out_g, avg)
    proj, hl, mixed, wmlp = _mixer_fwd(x2d, modraw, adab, norm1_g, *mixer_small, wmix, mlp_block)
    wmlp = _sibling_forward(wmlp)
    h2t, f, dx2, dz, dzt, vec2, loss8 = _mlp_fwd(x2d, mixed, tgt, modraw, adab, norm2_g, gf, wmlp)
    dh2, dw1, dw2t = _mlp_bwd(h2t, f, dz, dzt, wmlp)

    pos = jnp.stack([ci, 2 * xi + yi]).astype(jnp.int32)
    by_dest = lambda g: g.reshape((4, 2, -1) + g.shape[-1:])
    mlp_parts = [by_dest(dw1), by_dest(dw2t)]
    mlp_sib = _grads_to_sibling("mlp", mlp_parts)
    mlp_sums = [_pair_sum(pos, p, r) for p, r in zip(mlp_parts, mlp_sib)]
    gx, vec, hb, dproj_t, dmixed, ycat_t, xl_t, dgate, *mlp_chips = _mixer_bwd(
        x2d, mixed, dh2, dx2, proj, hl, modraw, adab, norm1_g, norm2_g, *mixer_small, wmix, mlp_sums)
    dwint = _matmul("wgrad_in", dproj_t, hb)
    dwout = _matmul("wgrad_out", ycat_t, dmixed)
    dgates = _matmul("wgrad_gate", xl_t, dgate)
    mix_parts = [by_dest(dwout), by_dest(dwint)]
    mix_sib = _grads_to_sibling("mixer", mix_parts)
    mix_sums = [_pair_sum(pos, p, r) for p, r in zip(mix_parts, mix_sib)]
    mix_chips = _grads_to_chips(mix_sums)
    g_w1, g_w2t, g_wout, g_wint = (_final_sum(pos, p, r, q) for p, r, q in zip(
        mlp_parts + mix_parts, list(mlp_sib) + list(mix_sib), list(mlp_chips) + list(mix_chips)))

    gmod8 = (jnp.pad(vec[0:5], ((0, 1), (0, 0))) + jnp.pad(vec2[0:1], ((5, 0), (0, 0)))).reshape(N_DEV, 6 * D // N_DEV)
    sc_t = jnp.pad(sc_all.T, ((0, 0), (0, 128 - N_DEV))).astype(BF16)
    loss_rows = jnp.pad(loss8[0:1], ((0, HALO - 1), (0, D - loss8.shape[1])))
    msg_vec = jnp.concatenate([vec, vec2, loss_rows], axis=0)
    msg_gate = jnp.stack([_diag_blocks(dgates[:, :W]), _diag_blocks(dgates[:, W:])]).reshape(W, 128)
    g_adaw, sum_vec, sum_gate, gb = _grad_exchange(gmod8, sc_t, msg_vec, msg_gate)
    loss = sum_vec[LOSS_ROW, 0]
    sum_gate = sum_gate.reshape(2, W, W // 8)
    lo, hi = slice(0, W), slice(W, 2 * W)
    wl_full = sum_vec[V_WL01:V_WL23 + 1].reshape(CONV_L, W)
    ws_full = sum_vec[V_WS01:V_WS2 + 1].reshape(CONV_S + 1, W)[:CONV_S]
    row = lambda r, cols: sum_vec[r:r + 1, cols]
    small_grads = {
        "ada_b": gb[:, :6 * D // N_DEV].reshape(1, 6 * D),
        "norm1_g": row(V_G1, slice(0, D)),
        "lru_conv_w": lax.dynamic_slice(wl_full, (0, me_lin * hd), (CONV_L, hd)),
        "lru_conv_b": row(V_BL_BA, lo),
        "gate_a_w": sum_gate[0],
        "gate_a_b": row(V_BL_BA, hi),
        "gate_x_w": sum_gate[1],
        "gate_x_b": row(V_BX_SP, lo),
        "a_param": row(V_BX_SP, hi),
        "short_conv_w": lax.dynamic_slice(ws_full, (0, me_lin * hd), (CONV_S, hd)),
        "lru_out_g": row(V_GL_GC, lo),
        "conv_out_g": row(V_GL_GC, hi),
        "norm2_g": row(V_G2, slice(0, D)),
        "final_g": sum_vec[V_ROWS + 1:V_ROWS + 2, :],
    }
    names = list(small_grads)
    as2d = lambda a, n: a.reshape(small_grads[n].shape)
    small = _adamw_small([as2d(weights[n], n) for n in names], [small_grads[n] for n in names],
                         [as2d(given["m_" + n], n) for n in names], [as2d(given["v_" + n], n) for n in names],
                         [n == "a_param" for n in names])
    result = {n: tuple(o.reshape(weights[n].shape) for o in outs) for n, outs in zip(names, small)}

    big_grads = {"ada_w": g_adaw, "w_in": g_wint.T, "w_out": g_wout, "w_mlp1": g_w1, "w_mlp2": g_w2t.T}
    for n in _BIG:
        g = big_grads[n]
        delta, new_m, new_v = _adamw(n, weights[n][0], g, given["m_" + n][0], given["v_" + n][0])
        result[n] = tuple(o[None] for o in (g, delta, new_m, new_v))

    return (loss, gx[None], *[result[n][0] for n in _WEIGHT_NAMES], *[result[n][1] for n in _WEIGHT_NAMES],
            *[result[n][2] for n in _WEIGHT_NAMES], *[result[n][3] for n in _WEIGHT_NAMES])
```

```python
import functools

import jax
import jax.numpy as jnp
from jax import lax
from jax.experimental import pallas as pl
from jax.experimental.pallas import tpu as pltpu

F32 = jnp.float32
BF16 = jnp.bfloat16
MESH = pl.DeviceIdType.MESH

N_DEV = 8
D = 1024
W = 512
D_IN = 5 * W
D_FF = 4096
FF_BLK = D_FF // N_DEV
EPS = 1e-6
C_GATE = 8.0
CONV_L = 4
CONV_S = 3
HALO = 8

ROWS_W1T, ROWS_W2, ROWS_WOUT, ROWS_WIN = FF_BLK, FF_BLK, D // N_DEV, D_IN // N_DEV
OFF_WOUT = 0
OFF_WIN = OFF_WOUT + ROWS_WOUT
MIX_ROWS = OFF_WIN + ROWS_WIN
OFF_W1T = 0
OFF_W2 = OFF_W1T + ROWS_W1T
MLP_ROWS = OFF_W2 + ROWS_W2
CHIP_FLIPS = (4, 2, 6)

ADAM_LR = 0.001
ADAM_B1 = 0.9
ADAM_B2 = 0.999
ADAM_EPS = 1e-08
ADAM_WD = 0.01
ADAM_STEP = 10

VMEM_LIMIT = 56 * 1024 * 1024

TB_MIX = 256
TB_MIXB = 256
TB_MLP = 256
TB_MLPB = 512

ANY = pl.BlockSpec(memory_space=pl.ANY)
WHOLE = pl.BlockSpec(memory_space=pltpu.VMEM)


def _dot(a, b):
    return jnp.dot(a, b, preferred_element_type=F32)


def _dot_nt(a, b):
    return lax.dot_general(a, b, (((1,), (1,)), ((), ())), preferred_element_type=F32)


def _dot_tn(a, b):
    return lax.dot_general(a, b, (((0,), (0,)), ((), ())), preferred_element_type=F32)


def _sigmoid(v):
    return 1.0 / (1.0 + jnp.exp(-v))


def _softplus(v):
    t = jnp.exp(-jnp.abs(v))
    small = t * (1.0 - t * (0.5 - t * (1.0 / 3.0)))
    return jnp.maximum(v, 0.0) + jnp.where(t < 1e-2, small, jnp.log(1.0 + t))


def _one_minus_sq(a, log_a):
    return -jnp.tanh(log_a) * (a * a + 1.0)


_GELU_K = 0.7978845608028654
_GELU_C = 0.044715


def _gelu(u):
    th = jnp.tanh(_GELU_K * (u + _GELU_C * u * u * u))
    return 0.5 * u * (1.0 + th), th


def _gelu_grad(u, th):
    return 0.5 * (1.0 + th) + 0.5 * u * (1.0 - th * th) * _GELU_K * (1.0 + 3.0 * _GELU_C * u * u)


def _group_mean(v, avg):
    hi = v.astype(BF16)
    lo = (v - hi.astype(F32)).astype(BF16)
    return _dot(hi, avg) + _dot(lo, avg)


def _colsum(v):
    return jnp.sum(v, axis=0, keepdims=True)


def _rowmean(v):
    return jnp.mean(v, axis=-1, keepdims=True)


def _load_packed(wpack_hbm, off, rows, dst, sem):
    copies = [
        pltpu.make_async_copy(wpack_hbm.at[d, pl.ds(off, rows), :], dst.at[pl.ds(d * rows, rows), :], sem)
        for d in range(N_DEV)
    ]
    for cp in copies:
        cp.start()
    return copies


def _scan_groups(n_groups, a_ref, b_ref, out_ref, carry_ref, reverse):
    row = lax.broadcasted_iota(jnp.int32, (HALO, W), 0)

    def step(k, carry):
        g = (n_groups - 1 - k) if reverse else k
        rows = pl.ds(pl.multiple_of(g * HALO, HALO), HALO)
        a = a_ref[rows, :]
        b = b_ref[rows, :]
        for s in (1, 2, 4):
            if reverse:
                keep = row < HALO - s
                sh = HALO - s
            else:
                keep = row >= s
                sh = s
            a_sh = pltpu.roll(a, sh, axis=0)
            b_sh = pltpu.roll(b, sh, axis=0)
            b = jnp.where(keep, a * b_sh + b, b)
            a = jnp.where(keep, a * a_sh, a)
        h = b + a * carry
        out_ref[rows, :] = h
        edge = h[0:1, :] if reverse else h[HALO - 1:HALO, :]
        return jnp.broadcast_to(edge, (HALO, W))

    carry_ref[...] = lax.fori_loop(0, n_groups, step, carry_ref[...])


def _route_peers(me):
    x, y, c = me
    first = ((x + 1 - c) % 2, (y + c) % 2, c)
    second = ((x + c) % 2, (y + 1 - c) % 2, c)
    return first, second, (1 - x, 1 - y, c)


def _chip_gather_copies(block_hbm, out_hbm, send_sems, recv_sems):
    me = _position()
    first, second, diag = _route_peers(me)

    def copy(j, src, slot_of, to):
        return pltpu.make_async_remote_copy(
            src_ref=src, dst_ref=out_hbm.at[_linear(slot_of)], send_sem=send_sems.at[j], recv_sem=recv_sems.at[j],
            device_id=to, device_id_type=MESH)

    own_sends = [copy(0, block_hbm, me, first), copy(1, block_hbm, me, second)]
    forward = copy(2, out_hbm.at[_linear(first)], first, second)
    arrivals = [copy(0, block_hbm, first, first), copy(1, block_hbm, second, second), copy(2, block_hbm, diag, second)]
    return own_sends, forward, arrivals


def _mixer_fwd(x, modraw, adab, g1, wl, bl, bda, bdx, ba, bxb, ap, ws, gl, gc, avg, wpack, mlp_block):
    t_len = x.shape[0]
    tb = TB_MIX
    nb = t_len // tb

    def body(x_ref, modraw_ref, adab_ref, g1_ref, wl_ref, bl_ref, bda_ref, bdx_ref, ba_ref, bxb_ref, ap_ref,
             ws_ref, gl_ref, gc_ref, avg_ref, wpack_hbm, block_hbm, proj_ref, hl_ref, mixed_ref, wmlp_hbm,
             win_v, wout_v, sem, ulx_ext, cv_ext, hcar, a_s, b_s, send_sems, recv_sems, local_sem):
        i = pl.program_id(0)
        own = pltpu.make_async_copy(block_hbm, wmlp_hbm.at[_linear(_position())], local_sem)
        sends, forward, arrivals = _chip_gather_copies(block_hbm, wmlp_hbm, send_sems, recv_sems)

        @pl.when(i == 0)
        def _():
            own.start()
            for cp in sends:
                cp.start()

        @pl.when(i == nb - 1)
        def _():
            arrivals[0].wait_recv()
            forward.start()

        @pl.when(i == 0)
        def _():
            cps = _load_packed(wpack_hbm, OFF_WIN, ROWS_WIN, win_v, sem.at[0])
            cps += _load_packed(wpack_hbm, OFF_WOUT, ROWS_WOUT, wout_v, sem.at[1])
            ulx_ext[0:HALO, :] = jnp.zeros((HALO, W), F32)
            cv_ext[0:HALO, :] = jnp.zeros((HALO, W), F32)
            hcar[...] = jnp.zeros((HALO, W), F32)
            for cp in cps:
                cp.wait()

        mod = modraw_ref[...] + adab_ref[...]
        shift1, scale1, gate1 = mod[0:1], mod[1:2], mod[2:3]
        x = x_ref[...]
        r1 = lax.rsqrt(_rowmean(x * x) + EPS)
        h = (x * r1 * g1_ref[...]) * (1.0 + scale1) + shift1
        proj = _dot_nt(h.astype(BF16), win_v[...])
        proj_ref[...] = proj
        u_lx, u_ly, u_b, u_c, u_v = (proj[:, k * W:(k + 1) * W] for k in range(5))

        ulx_ext[HALO:HALO + tb, :] = u_lx
        xl = bl_ref[...] + wl_ref[CONV_L - 1:CONV_L, :] * u_lx
        for k in range(CONV_L - 1):
            xl = xl + wl_ref[k:k + 1, :] * ulx_ext[pl.ds(HALO - (CONV_L - 1) + k, tb), :]
        ulx_ext[0:HALO, :] = ulx_ext[tb:tb + HALO, :]
        xlb = xl.astype(BF16)
        r = _sigmoid(_dot(xlb, bda_ref[...]) + ba_ref[...])
        ig = _sigmoid(_dot(xlb, bdx_ref[...]) + bxb_ref[...])
        log_a = (-C_GATE) * r * _softplus(ap_ref[...])
        a = jnp.exp(log_a)
        mult = jnp.sqrt(_one_minus_sq(a, log_a))
        grow = i * tb + lax.broadcasted_iota(jnp.int32, (tb, W), 0)
        mult = jnp.where(grow == 0, 1.0, mult)
        a_s[...] = a
        b_s[...] = mult * (ig * xl)
        _scan_groups(tb // HALO, a_s, b_s, hl_ref, hcar, reverse=False)
        hl = hl_ref[...]
        ge, _ = _gelu(u_ly)
        p = ge * hl
        y_lru = p * lax.rsqrt(_group_mean(p * p, avg_ref[...]) + EPS) * gl_ref[...]

        cv = u_c * u_v
        cv_ext[HALO:HALO + tb, :] = cv
        cc = ws_ref[CONV_S - 1:CONV_S, :] * cv
        for k in range(CONV_S - 1):
            cc = cc + ws_ref[k:k + 1, :] * cv_ext[pl.ds(HALO - (CONV_S - 1) + k, tb), :]
        cv_ext[0:HALO, :] = cv_ext[tb:tb + HALO, :]
        q = u_b * cc
        y_conv = q * lax.rsqrt(_group_mean(q * q, avg_ref[...]) + EPS) * gc_ref[...]

        mixed_ref[...] = (_dot(y_lru.astype(BF16), wout_v[0:W, :]) + _dot(y_conv.astype(BF16), wout_v[W:2 * W, :]))

        @pl.when(i == nb - 1)
        def _():
            for cp in arrivals[1:]:
                cp.wait_recv()
            for cp in sends + [forward]:
                cp.wait_send()
            own.wait()

    tok = lambda cols: pl.BlockSpec((tb, cols), lambda i: (i, 0))
    full = lambda a: pl.BlockSpec(a.shape, lambda i: (0,) * a.ndim)
    small = (modraw, adab, g1, wl, bl, bda, bdx, ba, bxb, ap, ws, gl, gc, avg)
    n_chips = len(CHIP_FLIPS)
    return pl.pallas_call(
        body,
        name="mixer_fwd",
        grid=(nb,),
        in_specs=[tok(D)] + [full(a) for a in small] + [ANY, ANY],
        out_specs=[tok(D_IN), tok(W), tok(D), ANY],
        out_shape=[jax.ShapeDtypeStruct((t_len, D_IN), F32), jax.ShapeDtypeStruct((t_len, W), F32),
                   jax.ShapeDtypeStruct((t_len, D), F32), jax.ShapeDtypeStruct((N_DEV,) + mlp_block.shape, BF16)],
        scratch_shapes=[pltpu.VMEM((D_IN, D), BF16), pltpu.VMEM((D, D), BF16), pltpu.SemaphoreType.DMA((2,)),
                        pltpu.VMEM((tb + HALO, W), F32), pltpu.VMEM((tb + HALO, W), F32), pltpu.VMEM((HALO, W), F32),
                        pltpu.VMEM((tb, W), F32), pltpu.VMEM((tb, W), F32),
                        pltpu.SemaphoreType.DMA((n_chips,)), pltpu.SemaphoreType.DMA((n_chips,)), pltpu.SemaphoreType.DMA],
        compiler_params=pltpu.CompilerParams(dimension_semantics=("arbitrary",), vmem_limit_bytes=VMEM_LIMIT),
    )(x, *small, wpack, mlp_block)


def _sibling_forward(wmlp):
    def body(in_hbm, out_hbm, send_sems, recv_sems):
        x, y, c = _position()
        copies, arrivals = [], []
        for j, k in enumerate((0,) + CHIP_FLIPS):
            mine = out_hbm.at[_linear(_flip((x, y, c), k))]
            theirs = out_hbm.at[_linear(_flip((x, y, 1 - c), k))]
            copies.append(pltpu.make_async_remote_copy(
                src_ref=mine, dst_ref=mine, send_sem=send_sems.at[j], recv_sem=recv_sems.at[j],
                device_id=(x, y, 1 - c), device_id_type=MESH))
            arrivals.append(pltpu.make_async_remote_copy(
                src_ref=theirs, dst_ref=theirs, send_sem=send_sems.at[j], recv_sem=recv_sems.at[j],
                device_id=(x, y, 1 - c), device_id_type=MESH))
        for cp in copies:
            cp.start()
        for cp in arrivals:
            cp.wait_recv()
        for cp in copies:
            cp.wait_send()

    return pl.pallas_call(
        body,
        name="sibling_forward",
        in_specs=[ANY],
        out_specs=ANY,
        out_shape=jax.ShapeDtypeStruct(wmlp.shape, wmlp.dtype),
        input_output_aliases={0: 0},
        scratch_shapes=[pltpu.SemaphoreType.DMA((4,)), pltpu.SemaphoreType.DMA((4,))],
    )(wmlp)


def _mlp_fwd(x, mixed, tgt, modraw, adab, g2, gf, wpack):
    t_len = x.shape[0]
    tb = TB_MLP
    nb = t_len // tb

    def body(x_ref, mixed_ref, tgt_ref, modraw_ref, adab_ref, g2_ref, gf_ref, wpack_hbm,
             h2t_ref, f_ref, dx2_ref, dz_ref, dzt_ref, vec_ref, loss_ref, w1t_v, w2_v, sem):
        i = pl.program_id(0)

        @pl.when(i == 0)
        def _():
            cps = _load_packed(wpack_hbm, OFF_W1T, ROWS_W1T, w1t_v, sem.at[0])
            cps += _load_packed(wpack_hbm, OFF_W2, ROWS_W2, w2_v, sem.at[1])
            vec_ref[...] = jnp.zeros(vec_ref.shape, F32)
            loss_ref[...] = jnp.zeros(loss_ref.shape, F32)
            for cp in cps:
                cp.wait()

        mod = modraw_ref[...] + adab_ref[...]
        gate1, shift2, scale2, gate2 = mod[2:3], mod[3:4], mod[4:5], mod[5:6]
        x1 = x_ref[...] + gate1 * mixed_ref[...]
        r2 = lax.rsqrt(_rowmean(x1 * x1) + EPS)
        h2 = (x1 * r2 * g2_ref[...]) * (1.0 + scale2) + shift2
        h2b = h2.astype(BF16)
        h2t_ref[...] = h2.T.astype(BF16)
        z = jnp.zeros((tb, D), F32)
        for j in range(N_DEV):
            cols = slice(j * FF_BLK, (j + 1) * FF_BLK)
            fj = _dot_nt(h2b, w1t_v[cols, :])
            f_ref[:, cols] = fj
            rf = jnp.maximum(fj, 0.0)
            z = z + _dot((rf * rf).astype(BF16), w2_v[cols, :])
        x2 = x1 + gate2 * z
        r3 = lax.rsqrt(_rowmean(x2 * x2) + EPS)
        xn3 = x2 * r3
        diff = xn3 * gf_ref[...] - tgt_ref[...]
        sq = _colsum(diff * diff)
        loss_ref[...] += jnp.broadcast_to(jnp.sum(sq, axis=1, keepdims=True) * (0.5 / D), loss_ref.shape)
        dy = diff * (1.0 / D)
        dyn = dy * gf_ref[...]
        dx2 = r3 * (dyn - xn3 * _rowmean(dyn * xn3))
        dx2_ref[...] = dx2
        dz = gate2 * dx2
        dz_ref[...] = dz.astype(BF16)
        dzt_ref[...] = dz.T.astype(BF16)
        vec_ref[0:1, :] += _colsum(dx2 * z)
        vec_ref[1:2, :] += _colsum(dy * xn3)

    tok = lambda cols: pl.BlockSpec((tb, cols), lambda i: (i, 0))
    tok_t = pl.BlockSpec((D, tb), lambda i: (0, i))
    full = lambda a: pl.BlockSpec(a.shape, lambda i: (0,) * a.ndim)
    small = (modraw, adab, g2, gf)
    return pl.pallas_call(
        body,
        name="mlp_fwd",
        grid=(nb,),
        in_specs=[tok(D), tok(D), tok(D)] + [full(a) for a in small] + [ANY],
        out_specs=[tok_t, tok(D_FF), tok(D), tok(D), tok_t, pl.BlockSpec((8, D), lambda i: (0, 0)),
                   pl.BlockSpec((8, 128), lambda i: (0, 0))],
        out_shape=[jax.ShapeDtypeStruct((D, t_len), BF16), jax.ShapeDtypeStruct((t_len, D_FF), F32),
                   jax.ShapeDtypeStruct((t_len, D), F32), jax.ShapeDtypeStruct((t_len, D), BF16),
                   jax.ShapeDtypeStruct((D, t_len), BF16),
                   jax.ShapeDtypeStruct((8, D), F32), jax.ShapeDtypeStruct((8, 128), F32)],
        scratch_shapes=[pltpu.VMEM((D_FF, D), BF16), pltpu.VMEM((D_FF, D), BF16), pltpu.SemaphoreType.DMA((2,))],
        compiler_params=pltpu.CompilerParams(dimension_semantics=("arbitrary",), vmem_limit_bytes=VMEM_LIMIT),
    )(x, mixed, tgt, *small, wpack)


def _mlp_bwd(h2t, f, dz, dzt, wpack):
    t_len = dz.shape[0]
    tb = TB_MLPB
    nb = t_len // tb

    def body(h2t_ref, f_ref, dz_ref, dzt_ref, w1t_ref, w2_ref, dh2_ref, dw1_ref, dw2t_ref):
        j = pl.program_id(0)
        t = pl.program_id(1)
        rows = pl.ds(pl.multiple_of(t * tb, tb), tb)
        w1t = w1t_ref[0]
        w2 = w2_ref[0]
        rf = jnp.maximum(f_ref[...], 0.0)
        a2 = (rf * rf).astype(BF16)
        df = (_dot_nt(dz_ref[...], w2) * (2.0 * rf)).astype(BF16)
        g2 = _dot(dzt_ref[...], a2)
        g1 = _dot(h2t_ref[...], df)
        dh = _dot(df, w1t)

        @pl.when(t == 0)
        def _():
            dw2t_ref[0] = g2
            dw1_ref[0] = g1

        @pl.when(t != 0)
        def _():
            dw2t_ref[0] += g2
            dw1_ref[0] += g1

        @pl.when(j == 0)
        def _():
            dh2_ref[rows, :] = dh

        @pl.when(j != 0)
        def _():
            dh2_ref[rows, :] += dh

    return pl.pallas_call(
        body,
        name="mlp_bwd",
        grid=(N_DEV, nb),
        in_specs=[pl.BlockSpec((D, tb), lambda j, t: (0, t)),
                  pl.BlockSpec((tb, FF_BLK), lambda j, t: (t, j)),
                  pl.BlockSpec((tb, D), lambda j, t: (t, 0)),
                  pl.BlockSpec((D, tb), lambda j, t: (0, t)),
                  pl.BlockSpec((1, ROWS_W1T, D), lambda j, t: (j, OFF_W1T // ROWS_W1T, 0)),
                  pl.BlockSpec((1, ROWS_W2, D), lambda j, t: (j, OFF_W2 // ROWS_W2, 0))],
        out_specs=[pl.BlockSpec((t_len, D), lambda j, t: (0, 0)),
                   pl.BlockSpec((1, D, FF_BLK), lambda j, t: (j, 0, 0)),
                   pl.BlockSpec((1, D, FF_BLK), lambda j, t: (j, 0, 0))],
        out_shape=[jax.ShapeDtypeStruct((t_len, D), F32), jax.ShapeDtypeStruct((N_DEV, D, FF_BLK), F32),
                   jax.ShapeDtypeStruct((N_DEV, D, FF_BLK), F32)],
        compiler_params=pltpu.CompilerParams(dimension_semantics=("arbitrary", "arbitrary"),
                                             vmem_limit_bytes=VMEM_LIMIT),
    )(h2t, f, dz, dzt, wpack, wpack)


V_SHIFT1, V_SCALE1, V_GATE1, V_SHIFT2, V_SCALE2, V_G1, V_G2 = 0, 1, 2, 3, 4, 6, 7
V_BL_BA, V_BX_SP, V_GL_GC, V_WL01, V_WL23, V_WS01, V_WS2 = 8, 9, 10, 11, 12, 13, 14
V_ROWS = 16


def _chip_scatter_copies(srcs, dsts, send_sems, recv_sems):
    me = _position()
    copies = []
    for a, (src, dst) in enumerate(zip(srcs, dsts)):
        for j, k in enumerate(CHIP_FLIPS):
            peer = _flip(me, k)
            copies.append(pltpu.make_async_remote_copy(
                src_ref=src.at[2 * peer[0] + peer[1]], dst_ref=dst.at[j], send_sem=send_sems.at[len(CHIP_FLIPS) * a + j],
                recv_sem=recv_sems.at[len(CHIP_FLIPS) * a + j], device_id=peer, device_id_type=MESH))
    return copies


def _mixer_bwd(x, mixed, dh2, dx2, proj, hl, modraw, adab, g1, g2, wl, bl, bda, bdx, ba, bxb, ap, ws, gl, gc, avg, wpack,
               chip_sums):
    t_len = x.shape[0]
    tb = TB_MIXB
    nb = t_len // tb
    hb = tb // HALO
    n_sums = len(chip_sums)

    def body(x_ref, mixed_ref, dh2_ref, dx2_ref, proj_ref, projh_ref, hl_ref, hlh_ref,
             modraw_ref, adab_ref, g1_ref, g2_ref, wl_ref, bl_ref, bda_ref, bdx_ref, ba_ref, bxb_ref, ap_ref,
             ws_ref, gl_ref, gc_ref, avg_ref, wpack_hbm, *rest):
        sums_hbm, rest = rest[:n_sums], rest[n_sums:]
        gx_ref, vec_ref, hb_ref, dprojt_ref, dmixed_ref, ycatt_ref, xlt_ref, dgate_ref = rest[:8]
        landed_hbm, rest = rest[8:8 + n_sums], rest[8 + n_sums:]
        (win_v, wout_v, sem, ulx_ext, cv_ext, hl_ext, a_ext, dxl_ext, dcc_ext, dcar, an_s, g_s, dh_s,
         send_sems, recv_sems) = rest
        i = pl.program_id(0)
        blk = nb - 1 - i
        chip_copies = _chip_scatter_copies(sums_hbm, landed_hbm, send_sems, recv_sems)

        @pl.when(i == 0)
        def _():
            for cp in chip_copies:
                cp.start()
            cps = _load_packed(wpack_hbm, OFF_WIN, ROWS_WIN, win_v, sem.at[0])
            cps += _load_packed(wpack_hbm, OFF_WOUT, ROWS_WOUT, wout_v, sem.at[1])
            vec_ref[...] = jnp.zeros(vec_ref.shape, F32)
            zero = jnp.zeros((HALO, W), F32)
            a_ext[tb:tb + HALO, :] = zero
            dxl_ext[tb:tb + HALO, :] = zero
            dcc_ext[tb:tb + HALO, :] = zero
            dcar[...] = zero
            for cp in cps:
                cp.wait()

        mod = modraw_ref[...] + adab_ref[...]
        shift1, scale1, gate1, scale2 = mod[0:1], mod[1:2], mod[2:3], mod[4:5]
        x = x_ref[...]
        mixed = mixed_ref[...]

        x1 = x + gate1 * mixed
        r2 = lax.rsqrt(_rowmean(x1 * x1) + EPS)
        xn2 = x1 * r2
        dh2 = dh2_ref[...]
        vec_ref[V_SHIFT2:V_SHIFT2 + 1, :] += _colsum(dh2)
        vec_ref[V_SCALE2:V_SCALE2 + 1, :] += _colsum(dh2 * xn2 * g2_ref[...])
        vec_ref[V_G2:V_G2 + 1, :] += _colsum(dh2 * (1.0 + scale2) * xn2)
        dxn2 = dh2 * g2_ref[...] * (1.0 + scale2)
        dx1 = dx2_ref[...] + r2 * (dxn2 - xn2 * _rowmean(dxn2 * xn2))
        vec_ref[V_GATE1:V_GATE1 + 1, :] += _colsum(dx1 * mixed)
        dmixed = (gate1 * dx1).astype(BF16)

        proj = proj_ref[...]
        u_lx, u_ly, u_b, u_c, u_v = (proj[:, k * W:(k + 1) * W] for k in range(5))
        has_prev = (blk > 0).astype(F32)
        projh = projh_ref[...]
        ulx_ext[0:HALO, :] = projh[:, 0:W] * has_prev
        ulx_ext[HALO:HALO + tb, :] = u_lx
        xl = bl_ref[...] + wl_ref[CONV_L - 1:CONV_L, :] * u_lx
        for k in range(CONV_L - 1):
            xl = xl + wl_ref[k:k + 1, :] * ulx_ext[pl.ds(HALO - (CONV_L - 1) + k, tb), :]
        xlb = xl.astype(BF16)
        r = _sigmoid(_dot(xlb, bda_ref[...]) + ba_ref[...])
        ig = _sigmoid(_dot(xlb, bdx_ref[...]) + bxb_ref[...])
        sp = _softplus(ap_ref[...])
        log_a = (-C_GATE) * r * sp
        a = jnp.exp(log_a)
        mult_raw = jnp.sqrt(_one_minus_sq(a, log_a))
        first = (blk * tb + lax.broadcasted_iota(jnp.int32, (tb, W), 0)) == 0
        mult = jnp.where(first, 1.0, mult_raw)
        hl = hl_ref[...]
        ge, th = _gelu(u_ly)
        p = ge * hl
        rp = lax.rsqrt(_group_mean(p * p, avg_ref[...]) + EPS)
        pn = p * rp
        cv = u_c * u_v
        cv_ext[0:HALO, :] = projh[:, 3 * W:4 * W] * projh[:, 4 * W:5 * W] * has_prev
        cv_ext[HALO:HALO + tb, :] = cv
        cc = ws_ref[CONV_S - 1:CONV_S, :] * cv
        for k in range(CONV_S - 1):
            cc = cc + ws_ref[k:k + 1, :] * cv_ext[pl.ds(HALO - (CONV_S - 1) + k, tb), :]
        q = u_b * cc
        rq = lax.rsqrt(_group_mean(q * q, avg_ref[...]) + EPS)
        qn = q * rq

        dmixed_ref[...] = dmixed
        ycatt_ref[0:W, :] = (pn * gl_ref[...]).T.astype(BF16)
        ycatt_ref[W:2 * W, :] = (qn * gc_ref[...]).T.astype(BF16)
        dyl = _dot_nt(dmixed, wout_v[0:W, :])
        dyc = _dot_nt(dmixed, wout_v[W:2 * W, :])

        dqn = dyc * gc_ref[...]
        dq = rq * (dqn - qn * _group_mean(dqn * qn, avg_ref[...]))
        du_b = dq * cc
        dcc = dq * u_b
        dcc_ext[0:tb, :] = dcc
        dcv = ws_ref[CONV_S - 1:CONV_S, :] * dcc
        for k in range(CONV_S - 1):
            dcv = dcv + ws_ref[k:k + 1, :] * dcc_ext[pl.ds(CONV_S - 1 - k, tb), :]
        dcc_ext[tb:tb + HALO, :] = dcc_ext[0:HALO, :]
        du_c = dcv * u_v
        du_v = dcv * u_c
        dws = [_colsum(dcc * cv_ext[pl.ds(HALO - (CONV_S - 1) + k, tb), :]) for k in range(CONV_S)]

        dpn = dyl * gl_ref[...]
        dp = rp * (dpn - pn * _group_mean(dpn * pn, avg_ref[...]))
        du_ly = dp * hl * _gelu_grad(u_ly, th)
        g_s[...] = dp * ge
        a_ext[0:tb, :] = a
        an_s[...] = a_ext[pl.ds(1, tb), :]
        _scan_groups(hb, an_s, g_s, dh_s, dcar, reverse=True)
        a_ext[tb:tb + HALO, :] = a_ext[0:HALO, :]
        dh = dh_s[...]
        hl_ext[0:HALO, :] = hlh_ref[...] * has_prev
        hl_ext[HALO:HALO + tb, :] = hl
        da = dh * hl_ext[pl.ds(HALO - 1, tb), :]
        dmult = dh * (ig * xl)
        dig = dh * (mult * xl)
        dxl = dh * (mult * ig)
        dlog = da * a - jnp.where(first, 0.0, dmult * (a * a) / mult_raw)
        dr = dlog * ((-C_GATE) * sp)
        dsp = _colsum(dlog * ((-C_GATE) * r))
        dga = dr * r * (1.0 - r)
        dgx = dig * ig * (1.0 - ig)
        dgab = dga.astype(BF16)
        dgxb = dgx.astype(BF16)
        xlt_ref[...] = xl.T.astype(BF16)
        dgate_ref[:, 0:W] = dgab
        dgate_ref[:, W:2 * W] = dgxb
        dxl = dxl + _dot_nt(dgab, bda_ref[...]) + _dot_nt(dgxb, bdx_ref[...])
        dxl_ext[0:tb, :] = dxl
        du_lx = wl_ref[CONV_L - 1:CONV_L, :] * dxl
        for k in range(CONV_L - 1):
            du_lx = du_lx + wl_ref[k:k + 1, :] * dxl_ext[pl.ds(CONV_L - 1 - k, tb), :]
        dxl_ext[tb:tb + HALO, :] = dxl_ext[0:HALO, :]
        dwl = [_colsum(dxl * ulx_ext[pl.ds(HALO - (CONV_L - 1) + k, tb), :]) for k in range(CONV_L)]

        cat = lambda u, v: jnp.concatenate([u, v], axis=1)
        vec_ref[V_BL_BA:V_BL_BA + 1, :] += cat(_colsum(dxl), _colsum(dga))
        vec_ref[V_BX_SP:V_BX_SP + 1, :] += cat(_colsum(dgx), dsp)
        vec_ref[V_GL_GC:V_GL_GC + 1, :] += cat(_colsum(dyl * pn), _colsum(dyc * qn))
        vec_ref[V_WL01:V_WL01 + 1, :] += cat(dwl[0], dwl[1])
        vec_ref[V_WL23:V_WL23 + 1, :] += cat(dwl[2], dwl[3])
        vec_ref[V_WS01:V_WS01 + 1, :] += cat(dws[0], dws[1])
        vec_ref[V_WS2:V_WS2 + 1, 0:W] += dws[2]

        r1 = lax.rsqrt(_rowmean(x * x) + EPS)
        xn1 = x * r1
        hb_ref[...] = ((xn1 * g1_ref[...]) * (1.0 + scale1) + shift1).astype(BF16)
        dh_in = jnp.zeros((tb, D), F32)
        for k, du in enumerate((du_lx, du_ly, du_b, du_c, du_v)):
            dprojt_ref[k * W:(k + 1) * W, :] = du.T.astype(BF16)
            dh_in = dh_in + _dot(du.astype(BF16), win_v[k * W:(k + 1) * W, :])
        vec_ref[V_SHIFT1:V_SHIFT1 + 1, :] += _colsum(dh_in)
        vec_ref[V_SCALE1:V_SCALE1 + 1, :] += _colsum(dh_in * xn1 * g1_ref[...])
        vec_ref[V_G1:V_G1 + 1, :] += _colsum(dh_in * (1.0 + scale1) * xn1)
        dxn1 = dh_in * g1_ref[...] * (1.0 + scale1)
        gx_ref[...] = dx1 + r1 * (dxn1 - xn1 * _rowmean(dxn1 * xn1))

        @pl.when(i == nb - 1)
        def _():
            for cp in chip_copies:
                cp.wait_recv()
            for cp in chip_copies:
                cp.wait_send()

    rev = lambda cols: pl.BlockSpec((tb, cols), lambda i: (nb - 1 - i, 0))
    rev_t = lambda rows: pl.BlockSpec((rows, tb), lambda i: (0, nb - 1 - i))
    halo = lambda cols: pl.BlockSpec((HALO, cols), lambda i: (jnp.maximum((nb - 1 - i) * hb - 1, 0), 0))
    full = lambda a: pl.BlockSpec(a.shape, lambda i: (0,) * a.ndim)
    small = (modraw, adab, g1, g2, wl, bl, bda, bdx, ba, bxb, ap, ws, gl, gc, avg)
    ext = pltpu.VMEM((tb + HALO, W), F32)
    n_sems = len(CHIP_FLIPS) * n_sums
    return pl.pallas_call(
        body,
        name="mixer_bwd",
        grid=(nb,),
        in_specs=[rev(D), rev(D), rev(D), rev(D), rev(D_IN), halo(D_IN), rev(W), halo(W)]
        + [full(a) for a in small] + [ANY] * (1 + n_sums),
        out_specs=[rev(D), pl.BlockSpec((V_ROWS, D), lambda i: (0, 0)), rev(D), rev_t(D_IN), rev(D), rev_t(D),
                   rev_t(W), rev(2 * W)] + [ANY] * n_sums,
        out_shape=[jax.ShapeDtypeStruct((t_len, D), F32), jax.ShapeDtypeStruct((V_ROWS, D), F32),
                   jax.ShapeDtypeStruct((t_len, D), BF16), jax.ShapeDtypeStruct((D_IN, t_len), BF16),
                   jax.ShapeDtypeStruct((t_len, D), BF16), jax.ShapeDtypeStruct((D, t_len), BF16),
                   jax.ShapeDtypeStruct((W, t_len), BF16), jax.ShapeDtypeStruct((t_len, 2 * W), BF16)]
        + [jax.ShapeDtypeStruct((len(CHIP_FLIPS),) + s.shape[1:], s.dtype) for s in chip_sums],
        scratch_shapes=[pltpu.VMEM((D_IN, D), BF16), pltpu.VMEM((D, D), BF16), pltpu.SemaphoreType.DMA((2,)),
                        ext, ext, ext, ext, ext, ext, pltpu.VMEM((HALO, W), F32),
                        pltpu.VMEM((tb, W), F32), pltpu.VMEM((tb, W), F32), pltpu.VMEM((tb, W), F32),
                        pltpu.SemaphoreType.DMA((n_sems,)), pltpu.SemaphoreType.DMA((n_sems,))],
        compiler_params=pltpu.CompilerParams(dimension_semantics=("arbitrary",), vmem_limit_bytes=VMEM_LIMIT),
    )(x, mixed, dh2, dx2, proj, proj, hl, hl, *small, wpack, *chip_sums)


def _matmul(name, a, b, tm=512):
    m, k = a.shape
    n = b.shape[1]

    def body(a_ref, b_ref, o_ref):
        o_ref[...] = _dot(a_ref[...], b_ref[...])

    return pl.pallas_call(
        body,
        name=name,
        grid=(m // tm,),
        in_specs=[pl.BlockSpec((tm, k), lambda i: (i, 0)), pl.BlockSpec((k, n), lambda i: (0, 0))],
        out_specs=pl.BlockSpec((tm, n), lambda i: (i, 0)),
        out_shape=jax.ShapeDtypeStruct((m, n), F32),
        compiler_params=pltpu.CompilerParams(dimension_semantics=("arbitrary",), vmem_limit_bytes=VMEM_LIMIT),
    )(a, b)


def _block_diag(w):
    n, m, _ = w.shape
    eye = jnp.eye(n, dtype=w.dtype)
    return (w[:, :, None, :] * eye[:, None, :, None]).reshape(n * m, n * m)


def _diag_blocks(mat, n=8):
    m = mat.shape[0] // n
    return jnp.stack([mat[h * m:(h + 1) * m, h * m:(h + 1) * m] for h in range(n)])


def _pad_rows(a, rows):
    return jnp.pad(a, ((0, rows - a.shape[0]),) + ((0, 0),) * (a.ndim - 1))


def _position():
    return lax.axis_index("x"), lax.axis_index("y"), lax.axis_index("c")


def _linear(pos):
    return 4 * pos[0] + 2 * pos[1] + pos[2]


def _flip(pos, k):
    return tuple(1 - p if k & bit else p for p, bit in zip(pos, (4, 2, 1)))


def _exchange_all(make_copy, make_arrival):
    copies = [make_copy(k) for k in range(1, N_DEV)]
    for cp in copies:
        cp.start()
    for k in range(1, N_DEV):
        make_arrival(k).wait_recv()
    for cp in copies:
        cp.wait_send()


def _mod_exchange(msg, ada_w):
    cols = ada_w.shape[1]

    def body(msg_ref, adaw_ref, gath_ref, mod_ref, sendbuf, send_a, recv_a, send_b, recv_b):
        me = _position()
        me_lin = _linear(me)
        m = msg_ref[...]
        row = lax.broadcasted_iota(jnp.int32, m.shape, 0)
        gath_ref[me_lin] = jnp.where(row == 0, m * _sigmoid(m), m)

        def gather_copy(k, src_lin):
            return pltpu.make_async_remote_copy(
                src_ref=gath_ref.at[src_lin], dst_ref=gath_ref.at[src_lin], send_sem=send_a.at[k - 1],
                recv_sem=recv_a.at[k - 1], device_id=_flip(me, k), device_id_type=MESH)

        _exchange_all(lambda k: gather_copy(k, me_lin), lambda k: gather_copy(k, _linear(_flip(me, k))))

        sc_all = gath_ref[:, 0, :]
        scb = jnp.concatenate([sc_all, jnp.zeros_like(sc_all)], axis=0).astype(BF16)
        prod = _dot(scb, adaw_ref[...].astype(BF16))
        for b in range(N_DEV):
            sendbuf[b] = jnp.broadcast_to(prod[b:b + 1, :], (HALO, cols))
        mod_ref[me_lin] = sendbuf[me_lin]

        def row_copy(k, dst_lin):
            peer = _flip(me, k)
            return pltpu.make_async_remote_copy(
                src_ref=sendbuf.at[_linear(peer)], dst_ref=mod_ref.at[dst_lin], send_sem=send_b.at[k - 1],
                recv_sem=recv_b.at[k - 1], device_id=peer, device_id_type=MESH)

        _exchange_all(lambda k: row_copy(k, me_lin), lambda k: row_copy(k, _linear(_flip(me, k))))

    return pl.pallas_call(
        body,
        name="mod_exchange",
        in_specs=[WHOLE, WHOLE],
        out_specs=[WHOLE, WHOLE],
        out_shape=[jax.ShapeDtypeStruct((N_DEV, HALO, D), F32), jax.ShapeDtypeStruct((N_DEV, HALO, cols), F32)],
        scratch_shapes=[pltpu.VMEM((N_DEV, HALO, cols), F32)] + [pltpu.SemaphoreType.DMA((N_DEV - 1,))] * 4,
        compiler_params=pltpu.CompilerParams(vmem_limit_bytes=VMEM_LIMIT),
    )(msg, ada_w)


def _weight_gather(block):
    rows, cols = block.shape

    def body(x_ref, out_ref, send_sems, recv_sems, sib_send_sems, sib_recv_sems, local_sem):
        x, y, c = _position()
        me, sibling = (x, y, c), (x, y, 1 - c)
        sends, forward, arrivals = _chip_gather_copies(x_ref, out_ref, send_sems, recv_sems)

        def to_sibling(j, block_of, src=None):
            dst = out_ref.at[_linear(block_of)]
            return pltpu.make_async_remote_copy(
                src_ref=dst if src is None else src, dst_ref=dst, send_sem=sib_send_sems.at[j],
                recv_sem=sib_recv_sems.at[j], device_id=sibling, device_id_type=MESH)

        mine = pltpu.make_async_copy(x_ref, out_ref.at[_linear(me)], local_sem)
        mine.start()
        passes = [to_sibling(0, me, src=x_ref)] + [to_sibling(1 + j, p) for j, p in enumerate(_route_peers(me))]
        passes[0].start()
        for cp in sends:
            cp.start()
        arrivals[0].wait_recv()
        forward.start()
        passes[1].start()
        arrivals[1].wait_recv()
        passes[2].start()
        arrivals[2].wait_recv()
        passes[3].start()
        for j, p in enumerate((sibling,) + _route_peers(sibling)):
            to_sibling(j, p).wait_recv()
        for cp in sends + [forward] + passes:
            cp.wait_send()
        mine.wait()

    return pl.pallas_call(
        body,
        name="weight_gather",
        in_specs=[ANY],
        out_specs=ANY,
        out_shape=jax.ShapeDtypeStruct((N_DEV, rows, cols), block.dtype),
        scratch_shapes=[pltpu.SemaphoreType.DMA((3,)), pltpu.SemaphoreType.DMA((3,)), pltpu.SemaphoreType.DMA((4,)),
                        pltpu.SemaphoreType.DMA((4,)), pltpu.SemaphoreType.DMA],
    )(block)


def _grads_to_sibling(which, arrs):
    n = len(arrs)

    def body(*refs):
        srcs, dsts = refs[:n], refs[n:2 * n]
        send_sems, recv_sems = refs[2 * n:]
        x, y, c = _position()
        copies = []
        for a in range(n):
            for k in range(4):
                copies.append(pltpu.make_async_remote_copy(
                    src_ref=srcs[a].at[k, 1 - c], dst_ref=dsts[a].at[k], send_sem=send_sems.at[4 * a + k],
                    recv_sem=recv_sems.at[4 * a + k], device_id=(x, y, 1 - c), device_id_type=MESH))
        for cp in copies:
            cp.start()
        for cp in copies:
            cp.wait_recv()
        for cp in copies:
            cp.wait_send()

    return pl.pallas_call(
        body,
        name=which + "_grads_to_sibling",
        in_specs=[ANY] * n,
        out_specs=[ANY] * n,
        out_shape=[jax.ShapeDtypeStruct((4,) + a.shape[2:], a.dtype) for a in arrs],
        scratch_shapes=[pltpu.SemaphoreType.DMA((4 * n,)), pltpu.SemaphoreType.DMA((4 * n,))],
    )(*arrs)


def _grads_to_chips(arrs):
    n = len(arrs)

    def body(*refs):
        srcs, dsts = refs[:n], refs[n:2 * n]
        send_sems, recv_sems = refs[2 * n:]
        copies = _chip_scatter_copies(srcs, dsts, send_sems, recv_sems)
        for cp in copies:
            cp.start()
        for cp in copies:
            cp.wait_recv()
        for cp in copies:
            cp.wait_send()

    return pl.pallas_call(
        body,
        name="grads_to_chips",
        in_specs=[ANY] * n,
        out_specs=[ANY] * n,
        out_shape=[jax.ShapeDtypeStruct((3,) + a.shape[1:], a.dtype) for a in arrs],
        scratch_shapes=[pltpu.SemaphoreType.DMA((3 * n,)), pltpu.SemaphoreType.DMA((3 * n,))],
    )(*arrs)


def _row_block(rows):
    return 256 if rows % 256 == 0 else rows // 2


def _pair_sum(pos, mine, recv):
    _, _, rows, cols = mine.shape
    rb = _row_block(rows)

    def body(pos_ref, mine_ref, recv_ref, out_ref):
        out_ref[0] = (mine_ref[0, 0] + recv_ref[0]).astype(BF16)

    other = lambda k, pos: jnp.bitwise_xor(pos[1], k + 1)
    return pl.pallas_call(
        body,
        name="grad_pair_sum",
        grid_spec=pltpu.PrefetchScalarGridSpec(
            num_scalar_prefetch=1, grid=(3, rows // rb),
            in_specs=[pl.BlockSpec((1, 1, rb, cols), lambda k, r, pos: (other(k, pos), pos[0], r, 0)),
                      pl.BlockSpec((1, rb, cols), lambda k, r, pos: (other(k, pos), r, 0))],
            out_specs=pl.BlockSpec((1, rb, cols), lambda k, r, pos: (other(k, pos), r, 0))),
        out_shape=jax.ShapeDtypeStruct((4, rows, cols), BF16),
        compiler_params=pltpu.CompilerParams(dimension_semantics=("arbitrary", "arbitrary")),
    )(pos, mine, recv)


def _final_sum(pos, mine, recv, chips):
    _, _, rows, cols = mine.shape
    rb = _row_block(rows)

    def body(pos_ref, mine_ref, recv_ref, chips_ref, out_ref):
        g = mine_ref[0, 0] + recv_ref[0]
        for j in range(3):
            g = g + chips_ref[j].astype(F32)
        out_ref[...] = g

    return pl.pallas_call(
        body,
        name="grad_final_sum",
        grid_spec=pltpu.PrefetchScalarGridSpec(
            num_scalar_prefetch=1, grid=(rows // rb,),
            in_specs=[pl.BlockSpec((1, 1, rb, cols), lambda r, pos: (pos[1], pos[0], r, 0)),
                      pl.BlockSpec((1, rb, cols), lambda r, pos: (pos[1], r, 0)),
                      pl.BlockSpec((3, rb, cols), lambda r, pos: (0, r, 0))],
            out_specs=pl.BlockSpec((rb, cols), lambda r, pos: (r, 0))),
        out_shape=jax.ShapeDtypeStruct((rows, cols), F32),
        compiler_params=pltpu.CompilerParams(dimension_semantics=("arbitrary",)),
    )(pos, mine, recv, chips)


LOSS_ROW = V_ROWS + 8
GB_ROW = LOSS_ROW + 7


def _grad_exchange(gmod8, sc_t, msg_vec, msg_gate):
    cols = gmod8.shape[1]

    def body(gmod_ref, sct_ref, vec_ref, gate_ref, gadaw_ref, sumv_ref, sumg_ref, gb_ref,
             sendbuf, grecv, bufv, bufg, send_a, recv_a, send_v, recv_v, send_g, recv_g):
        me = _position()
        me_lin = _linear(me)
        gm = gmod_ref[...]
        for b in range(N_DEV):
            sendbuf[b] = jnp.broadcast_to(gm[b:b + 1, :], (HALO, cols))
        grecv[me_lin] = sendbuf[me_lin]

        def row_copy(k, dst_lin):
            peer = _flip(me, k)
            return pltpu.make_async_remote_copy(
                src_ref=sendbuf.at[_linear(peer)], dst_ref=grecv.at[dst_lin], send_sem=send_a.at[k - 1],
                recv_sem=recv_a.at[k - 1], device_id=peer, device_id_type=MESH)

        _exchange_all(lambda k: row_copy(k, me_lin), lambda k: row_copy(k, _linear(_flip(me, k))))

        g_all = grecv[:, 0, :]
        g_pad = jnp.concatenate([g_all, jnp.zeros((sct_ref.shape[1] - N_DEV, cols), F32)], axis=0).astype(BF16)
        gadaw_ref[...] = _dot(sct_ref[...], g_pad)
        bufv[me_lin] = vec_ref[...]
        bufv[me_lin, GB_ROW:GB_ROW + 1, 0:cols] = _colsum(g_all)
        bufg[me_lin] = gate_ref[...]

        def gather_copy(buf, sends, recvs, k, src_lin):
            return pltpu.make_async_remote_copy(
                src_ref=buf.at[src_lin], dst_ref=buf.at[src_lin], send_sem=sends.at[k - 1], recv_sem=recvs.at[k - 1],
                device_id=_flip(me, k), device_id_type=MESH)

        for buf, sends, recvs in ((bufv, send_v, recv_v), (bufg, send_g, recv_g)):
            _exchange_all(lambda k: gather_copy(buf, sends, recvs, k, me_lin),
                          lambda k: gather_copy(buf, sends, recvs, k, _linear(_flip(me, k))))

        sv = bufv[0]
        sg = bufg[0]
        for d in range(1, N_DEV):
            sv = sv + bufv[d]
            sg = sg + bufg[d]
        sumv_ref[...] = sv
        sumg_ref[...] = sg
        gb_ref[...] = bufv[:, GB_ROW, :]

    return pl.pallas_call(
        body,
        name="grad_exchange",
        in_specs=[WHOLE] * 4,
        out_specs=[WHOLE] * 4,
        out_shape=[jax.ShapeDtypeStruct((D, cols), F32), jax.ShapeDtypeStruct(msg_vec.shape, F32),
                   jax.ShapeDtypeStruct(msg_gate.shape, F32), jax.ShapeDtypeStruct((N_DEV, D), F32)],
        scratch_shapes=[pltpu.VMEM((N_DEV, HALO, cols), F32), pltpu.VMEM((N_DEV, HALO, cols), F32),
                        pltpu.VMEM((N_DEV,) + msg_vec.shape, F32), pltpu.VMEM((N_DEV,) + msg_gate.shape, F32)]
        + [pltpu.SemaphoreType.DMA((N_DEV - 1,))] * 6,
        compiler_params=pltpu.CompilerParams(vmem_limit_bytes=VMEM_LIMIT),
    )(gmod8, sc_t, msg_vec, msg_gate)


def _adamw_math(w, g, m, v):
    m = ADAM_B1 * m + (1.0 - ADAM_B1) * g
    v = ADAM_B2 * v + (1.0 - ADAM_B2) * (g * g)
    m_hat = m / (1.0 - ADAM_B1 ** ADAM_STEP)
    v_hat = v / (1.0 - ADAM_B2 ** ADAM_STEP)
    delta = -ADAM_LR * (m_hat / (jnp.sqrt(v_hat) + ADAM_EPS) + ADAM_WD * w)
    return delta, m, v


def _adamw(name, w, g, m, v):
    rows, cols = w.shape
    rb = 256 if rows % 256 == 0 else rows

    def body(w_ref, g_ref, m_ref, v_ref, d_ref, mo_ref, vo_ref):
        d_ref[...], mo_ref[...], vo_ref[...] = _adamw_math(w_ref[...], g_ref[...], m_ref[...], v_ref[...])

    spec = pl.BlockSpec((rb, cols), lambda r: (r, 0))
    return pl.pallas_call(
        body,
        name="adamw_" + name,
        grid=(rows // rb,),
        in_specs=[spec] * 4,
        out_specs=[spec] * 3,
        out_shape=[jax.ShapeDtypeStruct((rows, cols), F32)] * 3,
        compiler_params=pltpu.CompilerParams(dimension_semantics=("arbitrary",)),
    )(w, g, m, v)


def _adamw_small(ws, gs, ms, vs, sigmoid_scaled):
    n = len(ws)

    def body(*refs):
        w_refs, g_refs, m_refs, v_refs = (refs[i * n:(i + 1) * n] for i in range(4))
        outs = refs[4 * n:]
        for i in range(n):
            w = w_refs[i][...]
            g = g_refs[i][...]
            if sigmoid_scaled[i]:
                g = g * _sigmoid(w)
            delta, m, v = _adamw_math(w, g, m_refs[i][...], v_refs[i][...])
            outs[4 * i][...] = g
            outs[4 * i + 1][...] = delta
            outs[4 * i + 2][...] = m
            outs[4 * i + 3][...] = v

    shapes = [jax.ShapeDtypeStruct(w.shape, F32) for w in ws for _ in range(4)]
    outs = pl.pallas_call(
        body,
        name="adamw_small",
        in_specs=[WHOLE] * (4 * n),
        out_specs=[WHOLE] * (4 * n),
        out_shape=shapes,
    )(*ws, *gs, *ms, *vs)
    return [outs[4 * i:4 * i + 4] for i in range(n)]


_WEIGHT_NAMES = ("ada_w", "ada_b", "norm1_g", "w_in", "lru_conv_w", "lru_conv_b", "gate_a_w", "gate_a_b", "gate_x_w",
                 "gate_x_b", "a_param", "short_conv_w", "lru_out_g", "conv_out_g", "w_out", "norm2_g", "w_mlp1",
                 "w_mlp2", "final_g")
_BIG = ("ada_w", "w_in", "w_out", "w_mlp1", "w_mlp2")


def kernel(x, c, ada_w, ada_b, norm1_g, w_in, lru_conv_w, lru_conv_b, gate_a_w, gate_a_b, gate_x_w, gate_x_b, a_param, short_conv_w, lru_out_g, conv_out_g, w_out, norm2_g, w_mlp1, w_mlp2, final_g, loss_target, m_ada_w, m_ada_b, m_norm1_g, m_w_in, m_lru_conv_w, m_lru_conv_b, m_gate_a_w, m_gate_a_b, m_gate_x_w, m_gate_x_b, m_a_param, m_short_conv_w, m_lru_out_g, m_conv_out_g, m_w_out, m_norm2_g, m_w_mlp1, m_w_mlp2, m_final_g, v_ada_w, v_ada_b, v_norm1_g, v_w_in, v_lru_conv_w, v_lru_conv_b, v_gate_a_w, v_gate_a_b, v_gate_x_w, v_gate_x_b, v_a_param, v_short_conv_w, v_lru_out_g, v_conv_out_g, v_w_out, v_norm2_g, v_w_mlp1, v_w_mlp2, v_final_g):
    given = dict(locals())
    weights = {n: given[n] for n in _WEIGHT_NAMES}
    xi, yi, ci = _position()
    me_lin = _linear((xi, yi, ci))
    hd = W // N_DEV

    msg = (jnp.pad(c, ((0, HALO - 1), (0, 0)))
           + jnp.pad(lru_conv_w[0], ((1, HALO - 1 - CONV_L), (0, D - hd)))
           + jnp.pad(short_conv_w[0], ((1 + CONV_L, 0), (0, D - hd))))
    gath, mod_all = _mod_exchange(msg, ada_w[0])
    sc_all = gath[:, 0, :]
    wl = jnp.transpose(gath[:, 1:1 + CONV_L, :hd], (1, 0, 2)).reshape(CONV_L, W)
    ws = jnp.transpose(gath[:, 1 + CONV_L:HALO, :hd], (1, 0, 2)).reshape(CONV_S, W)
    modraw = _pad_rows(mod_all[:, 0, :].reshape(6, D), HALO)
    adab = _pad_rows(ada_b.reshape(6, D), HALO)

    mixer_block = jnp.concatenate([w_out[0], w_in[0].T], axis=0).astype(BF16)
    mlp_block = jnp.concatenate([w_mlp1[0].T, w_mlp2[0]], axis=0).astype(BF16)
    wmix = _weight_gather(mixer_block)

    x2d, tgt = x[0], loss_target[0]
    gf = final_g.reshape(1, D)
    bda = _block_diag(gate_a_w[0]).astype(BF16)
    bdx = _block_diag(gate_x_w[0]).astype(BF16)
    avg = _block_diag(jnp.full((8, W // 8, W // 8), 8.0 / W, F32)).astype(BF16)
    wl8 = _pad_rows(wl, HALO)
    ws8 = _pad_rows(ws, HALO)
    mixer_small = (wl8, lru_conv_b, bda, bdx, gate_a_b, gate_x_b, a_param, ws8, lru_out_g, conv_out_g, avg)
    proj, hl, mixed, wmlp = _mixer_fwd(x2d, modraw, adab, norm1_g, *mixer_small, wmix, mlp_block)
    wmlp = _sibling_forward(wmlp)
    h2t, f, dx2, dz, dzt, vec2, loss8 = _mlp_fwd(x2d, mixed, tgt, modraw, adab, norm2_g, gf, wmlp)
    dh2, dw1, dw2t = _mlp_bwd(h2t, f, dz, dzt, wmlp)

    pos = jnp.stack([ci, 2 * xi + yi]).astype(jnp.int32)
    by_dest = lambda g: g.reshape((4, 2, -1) + g.shape[-1:])
    mlp_parts = [by_dest(dw1), by_dest(dw2t)]
    mlp_sib = _grads_to_sibling("mlp", mlp_parts)
    mlp_sums = [_pair_sum(pos, p, r) for p, r in zip(mlp_parts, mlp_sib)]
    gx, vec, hb, dproj_t, dmixed, ycat_t, xl_t, dgate, *mlp_chips = _mixer_bwd(
        x2d, mixed, dh2, dx2, proj, hl, modraw, adab, norm1_g, norm2_g, *mixer_small, wmix, mlp_sums)
    dwint = _matmul("wgrad_in", dproj_t, hb)
    dwout = _matmul("wgrad_out", ycat_t, dmixed)
    dgates = _matmul("wgrad_gate", xl_t, dgate)
    mix_parts = [by_dest(dwout), by_dest(dwint)]
    mix_sib = _grads_to_sibling("mixer", mix_parts)
    mix_sums = [_pair_sum(pos, p, r) for p, r in zip(mix_parts, mix_sib)]
    mix_chips = _grads_to_chips(mix_sums)
    g_w1, g_w2t, g_wout, g_wint = (_final_sum(pos, p, r, q) for p, r, q in zip(
        mlp_parts + mix_parts, list(mlp_sib) + list(mix_sib), list(mlp_chips) + list(mix_chips)))

    gmod8 = (jnp.pad(vec[0:5], ((0, 1), (0, 0))) + jnp.pad(vec2[0:1], ((5, 0), (0, 0)))).reshape(N_DEV, 6 * D // N_DEV)
    sc_t = jnp.pad(sc_all.T, ((0, 0), (0, 128 - N_DEV))).astype(BF16)
    loss_rows = jnp.pad(loss8[0:1], ((0, HALO - 1), (0, D - loss8.shape[1])))
    msg_vec = jnp.concatenate([vec, vec2, loss_rows], axis=0)
    msg_gate = jnp.stack([_diag_blocks(dgates[:, :W]), _diag_blocks(dgates[:, W:])]).reshape(W, 128)
    g_adaw, sum_vec, sum_gate, gb = _grad_exchange(gmod8, sc_t, msg_vec, msg_gate)
    loss = sum_vec[LOSS_ROW, 0]
    sum_gate = sum_gate.reshape(2, W, W // 8)
    lo, hi = slice(0, W), slice(W, 2 * W)
    wl_full = sum_vec[V_WL01:V_WL23 + 1].reshape(CONV_L, W)
    ws_full = sum_vec[V_WS01:V_WS2 + 1].reshape(CONV_S + 1, W)[:CONV_S]
    row = lambda r, cols: sum_vec[r:r + 1, cols]
    small_grads = {
        "ada_b": gb[:, :6 * D // N_DEV].reshape(1, 6 * D),
        "norm1_g": row(V_G1, slice(0, D)),
        "lru_conv_w": lax.dynamic_slice(wl_full, (0, me_lin * hd), (CONV_L, hd)),
        "lru_conv_b": row(V_BL_BA, lo),
        "gate_a_w": sum_gate[0],
        "gate_a_b": row(V_BL_BA, hi),
        "gate_x_w": sum_gate[1],
        "gate_x_b": row(V_BX_SP, lo),
        "a_param": row(V_BX_SP, hi),
        "short_conv_w": lax.dynamic_slice(ws_full, (0, me_lin * hd), (CONV_S, hd)),
        "lru_out_g": row(V_GL_GC, lo),
        "conv_out_g": row(V_GL_GC, hi),
        "norm2_g": row(V_G2, slice(0, D)),
        "final_g": sum_vec[V_ROWS + 1:V_ROWS + 2, :],
    }
    names = list(small_grads)
    as2d = lambda a, n: a.reshape(small_grads[n].shape)
    small = _adamw_small([as2d(weights[n], n) for n in names], [small_grads[n] for n in names],
                         [as2d(given["m_" + n], n) for n in names], [as2d(given["v_" + n], n) for n in names],
                         [n == "a_param" for n in names])
    result = {n: tuple(o.reshape(weights[n].shape) for o in outs) for n, outs in zip(names, small)}

    big_grads = {"ada_w": g_adaw, "w_in": g_wint.T, "w_out": g_wout, "w_mlp1": g_w1, "w_mlp2": g_w2t.T}
    for n in _BIG:
        g = big_grads[n]
        delta, new_m, new_v = _adamw(n, weights[n][0], g, given["m_" + n][0], given["v_" + n][0])
        result[n] = tuple(o[None] for o in (g, delta, new_m, new_v))

    return (loss, gx[None], *[result[n][0] for n in _WEIGHT_NAMES], *[result[n][1] for n in _WEIGHT_NAMES],
            *[result[n][2] for n in _WEIGHT_NAMES], *[result[n][3] for n in _WEIGHT_NAMES])
```

```python
import functools

import jax
import jax.numpy as jnp
from jax import lax
from jax.experimental import pallas as pl
from jax.experimental.pallas import tpu as pltpu

F32 = jnp.float32
BF16 = jnp.bfloat16
MESH = pl.DeviceIdType.MESH

N_DEV = 8
D = 1024
W = 512
D_IN = 5 * W
D_FF = 4096
FF_BLK = D_FF // N_DEV
EPS = 1e-6
C_GATE = 8.0
CONV_L = 4
CONV_S = 3
HALO = 8

ROWS_W1T, ROWS_W2, ROWS_WOUT, ROWS_WIN = FF_BLK, FF_BLK, D // N_DEV, D_IN // N_DEV
OFF_WOUT = 0
OFF_WIN = OFF_WOUT + ROWS_WOUT
MIX_ROWS = OFF_WIN + ROWS_WIN
OFF_W1T = 0
OFF_W2 = OFF_W1T + ROWS_W1T
MLP_ROWS = OFF_W2 + ROWS_W2
CHIP_FLIPS = (4, 2, 6)

ADAM_LR = 0.001
ADAM_B1 = 0.9
ADAM_B2 = 0.999
ADAM_EPS = 1e-08
ADAM_WD = 0.01
ADAM_STEP = 10

VMEM_LIMIT = 56 * 1024 * 1024

TB_MIX = 256
TB_MIXB = 256
TB_MLP = 256
TB_MLPB = 512

ANY = pl.BlockSpec(memory_space=pl.ANY)
WHOLE = pl.BlockSpec(memory_space=pltpu.VMEM)


def _dot(a, b):
    return jnp.dot(a, b, preferred_element_type=F32)


def _dot_nt(a, b):
    return lax.dot_general(a, b, (((1,), (1,)), ((), ())), preferred_element_type=F32)


def _dot_tn(a, b):
    return lax.dot_general(a, b, (((0,), (0,)), ((), ())), preferred_element_type=F32)


def _sigmoid(v):
    return 1.0 / (1.0 + jnp.exp(-v))


def _softplus(v):
    t = jnp.exp(-jnp.abs(v))
    small = t * (1.0 - t * (0.5 - t * (1.0 / 3.0)))
    return jnp.maximum(v, 0.0) + jnp.where(t < 1e-2, small, jnp.log(1.0 + t))


def _one_minus_sq(a, log_a):
    return -jnp.tanh(log_a) * (a * a + 1.0)


_GELU_K = 0.7978845608028654
_GELU_C = 0.044715


def _gelu(u):
    th = jnp.tanh(_GELU_K * (u + _GELU_C * u * u * u))
    return 0.5 * u * (1.0 + th), th


def _gelu_grad(u, th):
    return 0.5 * (1.0 + th) + 0.5 * u * (1.0 - th * th) * _GELU_K * (1.0 + 3.0 * _GELU_C * u * u)


def _group_mean(v, avg):
    hi = v.astype(BF16)
    lo = (v - hi.astype(F32)).astype(BF16)
    return _dot(hi, avg) + _dot(lo, avg)


def _colsum(v):
    return jnp.sum(v, axis=0, keepdims=True)


def _rowmean(v):
    return jnp.mean(v, axis=-1, keepdims=True)


def _load_packed(wpack_hbm, off, rows, dst, sem):
    copies = [
        pltpu.make_async_copy(wpack_hbm.at[d, pl.ds(off, rows), :], dst.at[pl.ds(d * rows, rows), :], sem)
        for d in range(N_DEV)
    ]
    for cp in copies:
        cp.start()
    return copies


def _scan_groups(n_groups, a_ref, b_ref, out_ref, carry_ref, reverse):
    row = lax.broadcasted_iota(jnp.int32, (HALO, W), 0)

    def step(k, carry):
        g = (n_groups - 1 - k) if reverse else k
        rows = pl.ds(pl.multiple_of(g * HALO, HALO), HALO)
        a = a_ref[rows, :]
        b = b_ref[rows, :]
        for s in (1, 2, 4):
            if reverse:
                keep = row < HALO - s
                sh = HALO - s
            else:
                keep = row >= s
                sh = s
            a_sh = pltpu.roll(a, sh, axis=0)
            b_sh = pltpu.roll(b, sh, axis=0)
            b = jnp.where(keep, a * b_sh + b, b)
            a = jnp.where(keep, a * a_sh, a)
        h = b + a * carry
        out_ref[rows, :] = h
        edge = h[0:1, :] if reverse else h[HALO - 1:HALO, :]
        return jnp.broadcast_to(edge, (HALO, W))

    carry_ref[...] = lax.fori_loop(0, n_groups, step, carry_ref[...])


def _route_peers(me):
    x, y, c = me
    first = ((x + 1 - c) % 2, (y + c) % 2, c)
    second = ((x + c) % 2, (y + 1 - c) % 2, c)
    return first, second, (1 - x, 1 - y, c)


def _chip_gather_copies(block_hbm, out_hbm, send_sems, recv_sems):
    me = _position()
    first, second, diag = _route_peers(me)

    def copy(j, src, slot_of, to):
        return pltpu.make_async_remote_copy(
            src_ref=src, dst_ref=out_hbm.at[_linear(slot_of)], send_sem=send_sems.at[j], recv_sem=recv_sems.at[j],
            device_id=to, device_id_type=MESH)

    own_sends = [copy(0, block_hbm, me, first), copy(1, block_hbm, me, second)]
    forward = copy(2, out_hbm.at[_linear(first)], first, second)
    arrivals = [copy(0, block_hbm, first, first), copy(1, block_hbm, second, second), copy(2, block_hbm, diag, second)]
    return own_sends, forward, arrivals


def _mixer_fwd(x, modraw, adab, g1, wl, bl, bda, bdx, ba, bxb, ap, ws, gl, gc, avg, wpack, mlp_block):
    t_len = x.shape[0]
    tb = TB_MIX
    nb = t_len // tb

    def body(x_ref, modraw_ref, adab_ref, g1_ref, wl_ref, bl_ref, bda_ref, bdx_ref, ba_ref, bxb_ref, ap_ref,
             ws_ref, gl_ref, gc_ref, avg_ref, wpack_hbm, block_hbm, proj_ref, hl_ref, mixed_ref, wmlp_hbm,
             win_v, wout_v, sem, ulx_ext, cv_ext, hcar, a_s, b_s, send_sems, recv_sems, local_sem):
        i = pl.program_id(0)
        own = pltpu.make_async_copy(block_hbm, wmlp_hbm.at[_linear(_position())], local_sem)
        sends, forward, arrivals = _chip_gather_copies(block_hbm, wmlp_hbm, send_sems, recv_sems)

        @pl.when(i == 0)
        def _():
            own.start()
            for cp in sends:
                cp.start()

        @pl.when(i == nb - 1)
        def _():
            arrivals[0].wait_recv()
            forward.start()

        @pl.when(i == 0)
        def _():
            cps = _load_packed(wpack_hbm, OFF_WIN, ROWS_WIN, win_v, sem.at[0])
            cps += _load_packed(wpack_hbm, OFF_WOUT, ROWS_WOUT, wout_v, sem.at[1])
            ulx_ext[0:HALO, :] = jnp.zeros((HALO, W), F32)
            cv_ext[0:HALO, :] = jnp.zeros((HALO, W), F32)
            hcar[...] = jnp.zeros((HALO, W), F32)
            for cp in cps:
                cp.wait()

        mod = modraw_ref[...] + adab_ref[...]
        shift1, scale1, gate1 = mod[0:1], mod[1:2], mod[2:3]
        x = x_ref[...]
        r1 = lax.rsqrt(_rowmean(x * x) + EPS)
        h = (x * r1 * g1_ref[...]) * (1.0 + scale1) + shift1
        proj = _dot_nt(h.astype(BF16), win_v[...])
        proj_ref[...] = proj
        u_lx, u_ly, u_b, u_c, u_v = (proj[:, k * W:(k + 1) * W] for k in range(5))

        ulx_ext[HALO:HALO + tb, :] = u_lx
        xl = bl_ref[...] + wl_ref[CONV_L - 1:CONV_L, :] * u_lx
        for k in range(CONV_L - 1):
            xl = xl + wl_ref[k:k + 1, :] * ulx_ext[pl.ds(HALO - (CONV_L - 1) + k, tb), :]
        ulx_ext[0:HALO, :] = ulx_ext[tb:tb + HALO, :]
        xlb = xl.astype(BF16)
        r = _sigmoid(_dot(xlb, bda_ref[...]) + ba_ref[...])
        ig = _sigmoid(_dot(xlb, bdx_ref[...]) + bxb_ref[...])
        log_a = (-C_GATE) * r * _softplus(ap_ref[...])
        a = jnp.exp(log_a)
        mult = jnp.sqrt(_one_minus_sq(a, log_a))
        grow = i * tb + lax.broadcasted_iota(jnp.int32, (tb, W), 0)
        mult = jnp.where(grow == 0, 1.0, mult)
        a_s[...] = a
        b_s[...] = mult * (ig * xl)
        _scan_groups(tb // HALO, a_s, b_s, hl_ref, hcar, reverse=False)
        hl = hl_ref[...]
        ge, _ = _gelu(u_ly)
        p = ge * hl
        y_lru = p * lax.rsqrt(_group_mean(p * p, avg_ref[...]) + EPS) * gl_ref[...]

        cv = u_c * u_v
        cv_ext[HALO:HALO + tb, :] = cv
        cc = ws_ref[CONV_S - 1:CONV_S, :] * cv
        for k in range(CONV_S - 1):
            cc = cc + ws_ref[k:k + 1, :] * cv_ext[pl.ds(HALO - (CONV_S - 1) + k, tb), :]
        cv_ext[0:HALO, :] = cv_ext[tb:tb + HALO, :]
        q = u_b * cc
        y_conv = q * lax.rsqrt(_group_mean(q * q, avg_ref[...]) + EPS) * gc_ref[...]

        mixed_ref[...] = (_dot(y_lru.astype(BF16), wout_v[0:W, :]) + _dot(y_conv.astype(BF16), wout_v[W:2 * W, :]))

        @pl.when(i == nb - 1)
        def _():
            for cp in arrivals[1:]:
                cp.wait_recv()
            for cp in sends + [forward]:
                cp.wait_send()
            own.wait()

    tok = lambda cols: pl.BlockSpec((tb, cols), lambda i: (i, 0))
    full = lambda a: pl.BlockSpec(a.shape, lambda i: (0,) * a.ndim)
    small = (modraw, adab, g1, wl, bl, bda, bdx, ba, bxb, ap, ws, gl, gc, avg)
    n_chips = len(CHIP_FLIPS)
    return pl.pallas_call(
        body,
        name="mixer_fwd",
        grid=(nb,),
        in_specs=[tok(D)] + [full(a) for a in small] + [ANY, ANY],
        out_specs=[tok(D_IN), tok(W), tok(D), ANY],
        out_shape=[jax.ShapeDtypeStruct((t_len, D_IN), F32), jax.ShapeDtypeStruct((t_len, W), F32),
                   jax.ShapeDtypeStruct((t_len, D), F32), jax.ShapeDtypeStruct((N_DEV,) + mlp_block.shape, BF16)],
        scratch_shapes=[pltpu.VMEM((D_IN, D), BF16), pltpu.VMEM((D, D), BF16), pltpu.SemaphoreType.DMA((2,)),
                        pltpu.VMEM((tb + HALO, W), F32), pltpu.VMEM((tb + HALO, W), F32), pltpu.VMEM((HALO, W), F32),
                        pltpu.VMEM((tb, W), F32), pltpu.VMEM((tb, W), F32),
                        pltpu.SemaphoreType.DMA((n_chips,)), pltpu.SemaphoreType.DMA((n_chips,)), pltpu.SemaphoreType.DMA],
        compiler_params=pltpu.CompilerParams(dimension_semantics=("arbitrary",), vmem_limit_bytes=VMEM_LIMIT),
    )(x, *small, wpack, mlp_block)


def _sibling_forward(wmlp):
    def body(in_hbm, out_hbm, send_sems, recv_sems):
        x, y, c = _position()
        copies, arrivals = [], []
        for j, k in enumerate((0,) + CHIP_FLIPS):
            mine = out_hbm.at[_linear(_flip((x, y, c), k))]
            theirs = out_hbm.at[_linear(_flip((x, y, 1 - c), k))]
            copies.append(pltpu.make_async_remote_copy(
                src_ref=mine, dst_ref=mine, send_sem=send_sems.at[j], recv_sem=recv_sems.at[j],
                device_id=(x, y, 1 - c), device_id_type=MESH))
            arrivals.append(pltpu.make_async_remote_copy(
                src_ref=theirs, dst_ref=theirs, send_sem=send_sems.at[j], recv_sem=recv_sems.at[j],
                device_id=(x, y, 1 - c), device_id_type=MESH))
        for cp in copies:
            cp.start()
        for cp in arrivals:
            cp.wait_recv()
        for cp in copies:
            cp.wait_send()

    return pl.pallas_call(
        body,
        name="sibling_forward",
        in_specs=[ANY],
        out_specs=ANY,
        out_shape=jax.ShapeDtypeStruct(wmlp.shape, wmlp.dtype),
        input_output_aliases={0: 0},
        scratch_shapes=[pltpu.SemaphoreType.DMA((4,)), pltpu.SemaphoreType.DMA((4,))],
    )(wmlp)


def _mlp_fwd(x, mixed, tgt, modraw, adab, g2, gf, wpack):
    t_len = x.shape[0]
    tb = TB_MLP
    nb = t_len // tb

    def body(x_ref, mixed_ref, tgt_ref, modraw_ref, adab_ref, g2_ref, gf_ref, wpack_hbm,
             h2t_ref, f_ref, dx2_ref, dz_ref, dzt_ref, vec_ref, loss_ref, w1t_v, w2_v, sem):
        i = pl.program_id(0)

        @pl.when(i == 0)
        def _():
            cps = _load_packed(wpack_hbm, OFF_W1T, ROWS_W1T, w1t_v, sem.at[0])
            cps += _load_packed(wpack_hbm, OFF_W2, ROWS_W2, w2_v, sem.at[1])
            vec_ref[...] = jnp.zeros(vec_ref.shape, F32)
            loss_ref[...] = jnp.zeros(loss_ref.shape, F32)
            for cp in cps:
                cp.wait()

        mod = modraw_ref[...] + adab_ref[...]
        gate1, shift2, scale2, gate2 = mod[2:3], mod[3:4], mod[4:5], mod[5:6]
        x1 = x_ref[...] + gate1 * mixed_ref[...]
        r2 = lax.rsqrt(_rowmean(x1 * x1) + EPS)
        h2 = (x1 * r2 * g2_ref[...]) * (1.0 + scale2) + shift2
        h2b = h2.astype(BF16)
        h2t_ref[...] = h2.T.astype(BF16)
        z = jnp.zeros((tb, D), F32)
        for j in range(N_DEV):
            cols = slice(j * FF_BLK, (j + 1) * FF_BLK)
            fj = _dot_nt(h2b, w1t_v[cols, :])
            f_ref[:, cols] = fj
            rf = jnp.maximum(fj, 0.0)
            z = z + _dot((rf * rf).astype(BF16), w2_v[cols, :])
        x2 = x1 + gate2 * z
        r3 = lax.rsqrt(_rowmean(x2 * x2) + EPS)
        xn3 = x2 * r3
        diff = xn3 * gf_ref[...] - tgt_ref[...]
        sq = _colsum(diff * diff)
        loss_ref[...] += jnp.broadcast_to(jnp.sum(sq, axis=1, keepdims=True) * (0.5 / D), loss_ref.shape)
        dy = diff * (1.0 / D)
        dyn = dy * gf_ref[...]
        dx2 = r3 * (dyn - xn3 * _rowmean(dyn * xn3))
        dx2_ref[...] = dx2
        dz = gate2 * dx2
        dz_ref[...] = dz.astype(BF16)
        dzt_ref[...] = dz.T.astype(BF16)
        vec_ref[0:1, :] += _colsum(dx2 * z)
        vec_ref[1:2, :] += _colsum(dy * xn3)

    tok = lambda cols: pl.BlockSpec((tb, cols), lambda i: (i, 0))
    tok_t = pl.BlockSpec((D, tb), lambda i: (0, i))
    full = lambda a: pl.BlockSpec(a.shape, lambda i: (0,) * a.ndim)
    small = (modraw, adab, g2, gf)
    return pl.pallas_call(
        body,
        name="mlp_fwd",
        grid=(nb,),
        in_specs=[tok(D), tok(D), tok(D)] + [full(a) for a in small] + [ANY],
        out_specs=[tok_t, tok(D_FF), tok(D), tok(D), tok_t, pl.BlockSpec((8, D), lambda i: (0, 0)),
                   pl.BlockSpec((8, 128), lambda i: (0, 0))],
        out_shape=[jax.ShapeDtypeStruct((D, t_len), BF16), jax.ShapeDtypeStruct((t_len, D_FF), F32),
                   jax.ShapeDtypeStruct((t_len, D), F32), jax.ShapeDtypeStruct((t_len, D), BF16),
                   jax.ShapeDtypeStruct((D, t_len), BF16),
                   jax.ShapeDtypeStruct((8, D), F32), jax.ShapeDtypeStruct((8, 128), F32)],
        scratch_shapes=[pltpu.VMEM((D_FF, D), BF16), pltpu.VMEM((D_FF, D), BF16), pltpu.SemaphoreType.DMA((2,))],
        compiler_params=pltpu.CompilerParams(dimension_semantics=("arbitrary",), vmem_limit_bytes=VMEM_LIMIT),
    )(x, mixed, tgt, *small, wpack)


def _mlp_bwd(h2t, f, dz, dzt, wpack):
    t_len = dz.shape[0]
    tb = TB_MLPB
    nb = t_len // tb

    def body(h2t_ref, f_ref, dz_ref, dzt_ref, w1t_ref, w2_ref, dh2_ref, dw1_ref, dw2t_ref):
        j = pl.program_id(0)
        t = pl.program_id(1)
        rows = pl.ds(pl.multiple_of(t * tb, tb), tb)
        w1t = w1t_ref[0]
        w2 = w2_ref[0]
        rf = jnp.maximum(f_ref[...], 0.0)
        a2 = (rf * rf).astype(BF16)
        df = (_dot_nt(dz_ref[...], w2) * (2.0 * rf)).astype(BF16)
        g2 = _dot(dzt_ref[...], a2)
        g1 = _dot(h2t_ref[...], df)
        dh = _dot(df, w1t)

        @pl.when(t == 0)
        def _():
            dw2t_ref[0] = g2
            dw1_ref[0] = g1

        @pl.when(t != 0)
        def _():
            dw2t_ref[0] += g2
            dw1_ref[0] += g1

        @pl.when(j == 0)
        def _():
            dh2_ref[rows, :] = dh

        @pl.when(j != 0)
        def _():
            dh2_ref[rows, :] += dh

    return pl.pallas_call(
        body,
        name="mlp_bwd",
        grid=(N_DEV, nb),
        in_specs=[pl.BlockSpec((D, tb), lambda j, t: (0, t)),
                  pl.BlockSpec((tb, FF_BLK), lambda j, t: (t, j)),
                  pl.BlockSpec((tb, D), lambda j, t: (t, 0)),
                  pl.BlockSpec((D, tb), lambda j, t: (0, t)),
                  pl.BlockSpec((1, ROWS_W1T, D), lambda j, t: (j, OFF_W1T // ROWS_W1T, 0)),
                  pl.BlockSpec((1, ROWS_W2, D), lambda j, t: (j, OFF_W2 // ROWS_W2, 0))],
        out_specs=[pl.BlockSpec((t_len, D), lambda j, t: (0, 0)),
                   pl.BlockSpec((1, D, FF_BLK), lambda j, t: (j, 0, 0)),
                   pl.BlockSpec((1, D, FF_BLK), lambda j, t: (j, 0, 0))],
        out_shape=[jax.ShapeDtypeStruct((t_len, D), F32), jax.ShapeDtypeStruct((N_DEV, D, FF_BLK), F32),
                   jax.ShapeDtypeStruct((N_DEV, D, FF_BLK), F32)],
        compiler_params=pltpu.CompilerParams(dimension_semantics=("arbitrary", "arbitrary"),
                                             vmem_limit_bytes=VMEM_LIMIT),
    )(h2t, f, dz, dzt, wpack, wpack)


V_SHIFT1, V_SCALE1, V_GATE1, V_SHIFT2, V_SCALE2, V_G1, V_G2 = 0, 1, 2, 3, 4, 6, 7
V_BL_BA, V_BX_SP, V_GL_GC, V_WL01, V_WL23, V_WS01, V_WS2 = 8, 9, 10, 11, 12, 13, 14
V_ROWS = 16


def _chip_scatter_copies(srcs, dsts, send_sems, recv_sems):
    me = _position()
    copies = []
    for a, (src, dst) in enumerate(zip(srcs, dsts)):
        for j, k in enumerate(CHIP_FLIPS):
            peer = _flip(me, k)
            copies.append(pltpu.make_async_remote_copy(
                src_ref=src.at[2 * peer[0] + peer[1]], dst_ref=dst.at[j], send_sem=send_sems.at[len(CHIP_FLIPS) * a + j],
                recv_sem=recv_sems.at[len(CHIP_FLIPS) * a + j], device_id=peer, device_id_type=MESH))
    return copies


def _mixer_bwd(x, mixed, dh2, dx2, proj, hl, modraw, adab, g1, g2, wl, bl, bda, bdx, ba, bxb, ap, ws, gl, gc, avg, wpack,
               chip_sums):
    t_len = x.shape[0]
    tb = TB_MIXB
    nb = t_len // tb
    hb = tb // HALO
    n_sums = len(chip_sums)

    def body(x_ref, mixed_ref, dh2_ref, dx2_ref, proj_ref, projh_ref, hl_ref, hlh_ref,
             modraw_ref, adab_ref, g1_ref, g2_ref, wl_ref, bl_ref, bda_ref, bdx_ref, ba_ref, bxb_ref, ap_ref,
             ws_ref, gl_ref, gc_ref, avg_ref, wpack_hbm, *rest):
        sums_hbm, rest = rest[:n_sums], rest[n_sums:]
        gx_ref, vec_ref, hb_ref, dprojt_ref, dmixed_ref, ycatt_ref, xlt_ref, dgate_ref = rest[:8]
        landed_hbm, rest = rest[8:8 + n_sums], rest[8 + n_sums:]
        (win_v, wout_v, sem, ulx_ext, cv_ext, hl_ext, a_ext, dxl_ext, dcc_ext, dcar, an_s, g_s, dh_s,
         send_sems, recv_sems) = rest
        i = pl.program_id(0)
        blk = nb - 1 - i
        chip_copies = _chip_scatter_copies(sums_hbm, landed_hbm, send_sems, recv_sems)

        @pl.when(i == 0)
        def _():
            for cp in chip_copies:
                cp.start()
            cps = _load_packed(wpack_hbm, OFF_WIN, ROWS_WIN, win_v, sem.at[0])
            cps += _load_packed(wpack_hbm, OFF_WOUT, ROWS_WOUT, wout_v, sem.at[1])
            vec_ref[...] = jnp.zeros(vec_ref.shape, F32)
            zero = jnp.zeros((HALO, W), F32)
            a_ext[tb:tb + HALO, :] = zero
            dxl_ext[tb:tb + HALO, :] = zero
            dcc_ext[tb:tb + HALO, :] = zero
            dcar[...] = zero
            for cp in cps:
                cp.wait()

        mod = modraw_ref[...] + adab_ref[...]
        shift1, scale1, gate1, scale2 = mod[0:1], mod[1:2], mod[2:3], mod[4:5]
        x = x_ref[...]
        mixed = mixed_ref[...]

        x1 = x + gate1 * mixed
        r2 = lax.rsqrt(_rowmean(x1 * x1) + EPS)
        xn2 = x1 * r2
        dh2 = dh2_ref[...]
        vec_ref[V_SHIFT2:V_SHIFT2 + 1, :] += _colsum(dh2)
        vec_ref[V_SCALE2:V_SCALE2 + 1, :] += _colsum(dh2 * xn2 * g2_ref[...])
        vec_ref[V_G2:V_G2 + 1, :] += _colsum(dh2 * (1.0 + scale2) * xn2)
        dxn2 = dh2 * g2_ref[...] * (1.0 + scale2)
        dx1 = dx2_ref[...] + r2 * (dxn2 - xn2 * _rowmean(dxn2 * xn2))
        vec_ref[V_GATE1:V_GATE1 + 1, :] += _colsum(dx1 * mixed)
        dmixed = (gate1 * dx1).astype(BF16)

        proj = proj_ref[...]
        u_lx, u_ly, u_b, u_c, u_v = (proj[:, k * W:(k + 1) * W] for k in range(5))
        has_prev = (blk > 0).astype(F32)
        projh = projh_ref[...]
        ulx_ext[0:HALO, :] = projh[:, 0:W] * has_prev
        ulx_ext[HALO:HALO + tb, :] = u_lx
        xl = bl_ref[...] + wl_ref[CONV_L - 1:CONV_L, :] * u_lx
        for k in range(CONV_L - 1):
            xl = xl + wl_ref[k:k + 1, :] * ulx_ext[pl.ds(HALO - (CONV_L - 1) + k, tb), :]
        xlb = xl.astype(BF16)
        r = _sigmoid(_dot(xlb, bda_ref[...]) + ba_ref[...])
        ig = _sigmoid(_dot(xlb, bdx_ref[...]) + bxb_ref[...])
        sp = _softplus(ap_ref[...])
        log_a = (-C_GATE) * r * sp
        a = jnp.exp(log_a)
        mult_raw = jnp.sqrt(_one_minus_sq(a, log_a))
        first = (blk * tb + lax.broadcasted_iota(jnp.int32, (tb, W), 0)) == 0
        mult = jnp.where(first, 1.0, mult_raw)
        hl = hl_ref[...]
        ge, th = _gelu(u_ly)
        p = ge * hl
        rp = lax.rsqrt(_group_mean(p * p, avg_ref[...]) + EPS)
        pn = p * rp
        cv = u_c * u_v
        cv_ext[0:HALO, :] = projh[:, 3 * W:4 * W] * projh[:, 4 * W:5 * W] * has_prev
        cv_ext[HALO:HALO + tb, :] = cv
        cc = ws_ref[CONV_S - 1:CONV_S, :] * cv
        for k in range(CONV_S - 1):
            cc = cc + ws_ref[k:k + 1, :] * cv_ext[pl.ds(HALO - (CONV_S - 1) + k, tb), :]
        q = u_b * cc
        rq = lax.rsqrt(_group_mean(q * q, avg_ref[...]) + EPS)
        qn = q * rq

        dmixed_ref[...] = dmixed
        ycatt_ref[0:W, :] = (pn * gl_ref[...]).T.astype(BF16)
        ycatt_ref[W:2 * W, :] = (qn * gc_ref[...]).T.astype(BF16)
        dyl = _dot_nt(dmixed, wout_v[0:W, :])
        dyc = _dot_nt(dmixed, wout_v[W:2 * W, :])

        dqn = dyc * gc_ref[...]
        dq = rq * (dqn - qn * _group_mean(dqn * qn, avg_ref[...]))
        du_b = dq * cc
        dcc = dq * u_b
        dcc_ext[0:tb, :] = dcc
        dcv = ws_ref[CONV_S - 1:CONV_S, :] * dcc
        for k in range(CONV_S - 1):
            dcv = dcv + ws_ref[k:k + 1, :] * dcc_ext[pl.ds(CONV_S - 1 - k, tb), :]
        dcc_ext[tb:tb + HALO, :] = dcc_ext[0:HALO, :]
        du_c = dcv * u_v
        du_v = dcv * u_c
        dws = [_colsum(dcc * cv_ext[pl.ds(HALO - (CONV_S - 1) + k, tb), :]) for k in range(CONV_S)]

        dpn = dyl * gl_ref[...]
        dp = rp * (dpn - pn * _group_mean(dpn * pn, avg_ref[...]))
        du_ly = dp * hl * _gelu_grad(u_ly, th)
        g_s[...] = dp * ge
        a_ext[0:tb, :] = a
        an_s[...] = a_ext[pl.ds(1, tb), :]
        _scan_groups(hb, an_s, g_s, dh_s, dcar, reverse=True)
        a_ext[tb:tb + HALO, :] = a_ext[0:HALO, :]
        dh = dh_s[...]
        hl_ext[0:HALO, :] = hlh_ref[...] * has_prev
        hl_ext[HALO:HALO + tb, :] = hl
        da = dh * hl_ext[pl.ds(HALO - 1, tb), :]
        dmult = dh * (ig * xl)
        dig = dh * (mult * xl)
        dxl = dh * (mult * ig)
        dlog = da * a - jnp.where(first, 0.0, dmult * (a * a) / mult_raw)
        dr = dlog * ((-C_GATE) * sp)
        dsp = _colsum(dlog * ((-C_GATE) * r))
        dga = dr * r * (1.0 - r)
        dgx = dig * ig * (1.0 - ig)
        dgab = dga.astype(BF16)
        dgxb = dgx.astype(BF16)
        xlt_ref[...] = xl.T.astype(BF16)
        dgate_ref[:, 0:W] = dgab
        dgate_ref[:, W:2 * W] = dgxb
        dxl = dxl + _dot_nt(dgab, bda_ref[...]) + _dot_nt(dgxb, bdx_ref[...])
        dxl_ext[0:tb, :] = dxl
        du_lx = wl_ref[CONV_L - 1:CONV_L, :] * dxl
        for k in range(CONV_L - 1):
            du_lx = du_lx + wl_ref[k:k + 1, :] * dxl_ext[pl.ds(CONV_L - 1 - k, tb), :]
        dxl_ext[tb:tb + HALO, :] = dxl_ext[0:HALO, :]
        dwl = [_colsum(dxl * ulx_ext[pl.ds(HALO - (CONV_L - 1) + k, tb), :]) for k in range(CONV_L)]

        cat = lambda u, v: jnp.concatenate([u, v], axis=1)
        vec_ref[V_BL_BA:V_BL_BA + 1, :] += cat(_colsum(dxl), _colsum(dga))
        vec_ref[V_BX_SP:V_BX_SP + 1, :] += cat(_colsum(dgx), dsp)
        vec_ref[V_GL_GC:V_GL_GC + 1, :] += cat(_colsum(dyl * pn), _colsum(dyc * qn))
        vec_ref[V_WL01:V_WL01 + 1, :] += cat(dwl[0], dwl[1])
        vec_ref[V_WL23:V_WL23 + 1, :] += cat(dwl[2], dwl[3])
        vec_ref[V_WS01:V_WS01 + 1, :] += cat(dws[0], dws[1])
        vec_ref[V_WS2:V_WS2 + 1, 0:W] += dws[2]

        r1 = lax.rsqrt(_rowmean(x * x) + EPS)
        xn1 = x * r1
        hb_ref[...] = ((xn1 * g1_ref[...]) * (1.0 + scale1) + shift1).astype(BF16)
        dh_in = jnp.zeros((tb, D), F32)
        for k, du in enumerate((du_lx, du_ly, du_b, du_c, du_v)):
            dprojt_ref[k * W:(k + 1) * W, :] = du.T.astype(BF16)
            dh_in = dh_in + _dot(du.astype(BF16), win_v[k * W:(k + 1) * W, :])
        vec_ref[V_SHIFT1:V_SHIFT1 + 1, :] += _colsum(dh_in)
        vec_ref[V_SCALE1:V_SCALE1 + 1, :] += _colsum(dh_in * xn1 * g1_ref[...])
        vec_ref[V_G1:V_G1 + 1, :] += _colsum(dh_in * (1.0 + scale1) * xn1)
        dxn1 = dh_in * g1_ref[...] * (1.0 + scale1)
        gx_ref[...] = dx1 + r1 * (dxn1 - xn1 * _rowmean(dxn1 * xn1))

        @pl.when(i == nb - 1)
        def _():
            for cp in chip_copies:
                cp.wait_recv()
            for cp in chip_copies:
                cp.wait_send()

    rev = lambda cols: pl.BlockSpec((tb, cols), lambda i: (nb - 1 - i, 0))
    rev_t = lambda rows: pl.BlockSpec((rows, tb), lambda i: (0, nb - 1 - i))
    halo = lambda cols: pl.BlockSpec((HALO, cols), lambda i: (jnp.maximum((nb - 1 - i) * hb - 1, 0), 0))
    full = lambda a: pl.BlockSpec(a.shape, lambda i: (0,) * a.ndim)
    small = (modraw, adab, g1, g2, wl, bl, bda, bdx, ba, bxb, ap, ws, gl, gc, avg)
    ext = pltpu.VMEM((tb + HALO, W), F32)
    n_sems = len(CHIP_FLIPS) * n_sums
    return pl.pallas_call(
        body,
        name="mixer_bwd",
        grid=(nb,),
        in_specs=[rev(D), rev(D), rev(D), rev(D), rev(D_IN), halo(D_IN), rev(W), halo(W)]
        + [full(a) for a in small] + [ANY] * (1 + n_sums),
        out_specs=[rev(D), pl.BlockSpec((V_ROWS, D), lambda i: (0, 0)), rev(D), rev_t(D_IN), rev(D), rev_t(D),
                   rev_t(W), rev(2 * W)] + [ANY] * n_sums,
        out_shape=[jax.ShapeDtypeStruct((t_len, D), F32), jax.ShapeDtypeStruct((V_ROWS, D), F32),
                   jax.ShapeDtypeStruct((t_len, D), BF16), jax.ShapeDtypeStruct((D_IN, t_len), BF16),
                   jax.ShapeDtypeStruct((t_len, D), BF16), jax.ShapeDtypeStruct((D, t_len), BF16),
                   jax.ShapeDtypeStruct((W, t_len), BF16), jax.ShapeDtypeStruct((t_len, 2 * W), BF16)]
        + [jax.ShapeDtypeStruct((len(CHIP_FLIPS),) + s.shape[1:], s.dtype) for s in chip_sums],
        scratch_shapes=[pltpu.VMEM((D_IN, D), BF16), pltpu.VMEM((D, D), BF16), pltpu.SemaphoreType.DMA((2,)),
                        ext, ext, ext, ext, ext, ext, pltpu.VMEM((HALO, W), F32),
                        pltpu.VMEM((tb, W), F32), pltpu.VMEM((tb, W), F32), pltpu.VMEM((tb, W), F32),
                        pltpu.SemaphoreType.DMA((n_sems,)), pltpu.SemaphoreType.DMA((n_sems,))],
        compiler_params=pltpu.CompilerParams(dimension_semantics=("arbitrary",), vmem_limit_bytes=VMEM_LIMIT),
    )(x, mixed, dh2, dx2, proj, proj, hl, hl, *small, wpack, *chip_sums)


def _matmul(name, a, b, tm=512):
    m, k = a.shape
    n = b.shape[1]

    def body(a_ref, b_ref, o_ref):
        o_ref[...] = _dot(a_ref[...], b_ref[...])

    return pl.pallas_call(
        body,
        name=name,
        grid=(m // tm,),
        in_specs=[pl.BlockSpec((tm, k), lambda i: (i, 0)), pl.BlockSpec((k, n), lambda i: (0, 0))],
        out_specs=pl.BlockSpec((tm, n), lambda i: (i, 0)),
        out_shape=jax.ShapeDtypeStruct((m, n), F32),
        compiler_params=pltpu.CompilerParams(dimension_semantics=("arbitrary",), vmem_limit_bytes=VMEM_LIMIT),
    )(a, b)


def _block_diag(w):
    n, m, _ = w.shape
    eye = jnp.eye(n, dtype=w.dtype)
    return (w[:, :, None, :] * eye[:, None, :, None]).reshape(n * m, n * m)


def _diag_blocks(mat, n=8):
    m = mat.shape[0] // n
    return jnp.stack([mat[h * m:(h + 1) * m, h * m:(h + 1) * m] for h in range(n)])


def _pad_rows(a, rows):
    return jnp.pad(a, ((0, rows - a.shape[0]),) + ((0, 0),) * (a.ndim - 1))


def _position():
    return lax.axis_index("x"), lax.axis_index("y"), lax.axis_index("c")


def _linear(pos):
    return 4 * pos[0] + 2 * pos[1] + pos[2]


def _flip(pos, k):
    return tuple(1 - p if k & bit else p for p, bit in zip(pos, (4, 2, 1)))


def _exchange_all(make_copy, make_arrival):
    copies = [make_copy(k) for k in range(1, N_DEV)]
    for cp in copies:
        cp.start()
    for k in range(1, N_DEV):
        make_arrival(k).wait_recv()
    for cp in copies:
        cp.wait_send()


def _mod_exchange_steps(cols):
    def steps(msg_ref, adaw_ref, gath_ref, mod_ref, sendbuf, send_a, recv_a, send_b, recv_b):
        me = _position()
        me_lin = _linear(me)
        m = msg_ref[...]
        row = lax.broadcasted_iota(jnp.int32, m.shape, 0)
        gath_ref[me_lin] = jnp.where(row == 0, m * _sigmoid(m), m)

        def gather_copy(k, src_lin):
            return pltpu.make_async_remote_copy(
                src_ref=gath_ref.at[src_lin], dst_ref=gath_ref.at[src_lin], send_sem=send_a.at[k - 1],
                recv_sem=recv_a.at[k - 1], device_id=_flip(me, k), device_id_type=MESH)

        _exchange_all(lambda k: gather_copy(k, me_lin), lambda k: gather_copy(k, _linear(_flip(me, k))))

        sc_all = gath_ref[:, 0, :]
        scb = jnp.concatenate([sc_all, jnp.zeros_like(sc_all)], axis=0).astype(BF16)
        prod = _dot(scb, adaw_ref[...].astype(BF16))
        for b in range(N_DEV):
            sendbuf[b] = jnp.broadcast_to(prod[b:b + 1, :], (HALO, cols))
        mod_ref[me_lin] = sendbuf[me_lin]

        def row_copy(k, dst_lin):
            peer = _flip(me, k)
            return pltpu.make_async_remote_copy(
                src_ref=sendbuf.at[_linear(peer)], dst_ref=mod_ref.at[dst_lin], send_sem=send_b.at[k - 1],
                recv_sem=recv_b.at[k - 1], device_id=peer, device_id_type=MESH)

        _exchange_all(lambda k: row_copy(k, me_lin), lambda k: row_copy(k, _linear(_flip(me, k))))

    return steps


def _gather_and_mod(msg, ada_w, block):
    rows, cols = block.shape
    mod_cols = ada_w.shape[1]
    mod_steps = _mod_exchange_steps(mod_cols)

    def body(msg_ref, adaw_ref, x_ref, gath_ref, mod_ref, out_ref, sendbuf, send_a, recv_a, send_b, recv_b,
             send_sems, recv_sems, sib_send_sems, sib_recv_sems, local_sem):
        x, y, c = _position()
        me, sibling = (x, y, c), (x, y, 1 - c)
        sends, forward, arrivals = _chip_gather_copies(x_ref, out_ref, send_sems, recv_sems)

        def to_sibling(j, block_of, src=None):
            dst = out_ref.at[_linear(block_of)]
            return pltpu.make_async_remote_copy(
                src_ref=dst if src is None else src, dst_ref=dst, send_sem=sib_send_sems.at[j],
                recv_sem=sib_recv_sems.at[j], device_id=sibling, device_id_type=MESH)

        mine = pltpu.make_async_copy(x_ref, out_ref.at[_linear(me)], local_sem)
        mine.start()
        passes = [to_sibling(0, me, src=x_ref)] + [to_sibling(1 + j, p) for j, p in enumerate(_route_peers(me))]
        passes[0].start()
        for cp in sends:
            cp.start()
        mod_steps(msg_ref, adaw_ref, gath_ref, mod_ref, sendbuf, send_a, recv_a, send_b, recv_b)
        arrivals[0].wait_recv()
        forward.start()
        passes[1].start()
        arrivals[1].wait_recv()
        passes[2].start()
        arrivals[2].wait_recv()
        passes[3].start()
        for j, p in enumerate((sibling,) + _route_peers(sibling)):
            to_sibling(j, p).wait_recv()
        for cp in sends + [forward] + passes:
            cp.wait_send()
        mine.wait()

    return pl.pallas_call(
        body,
        name="gather_and_mod",
        in_specs=[WHOLE, WHOLE, ANY],
        out_specs=[WHOLE, WHOLE, ANY],
        out_shape=[jax.ShapeDtypeStruct((N_DEV, HALO, D), F32), jax.ShapeDtypeStruct((N_DEV, HALO, mod_cols), F32),
                   jax.ShapeDtypeStruct((N_DEV, rows, cols), block.dtype)],
        scratch_shapes=[pltpu.VMEM((N_DEV, HALO, mod_cols), F32)] + [pltpu.SemaphoreType.DMA((N_DEV - 1,))] * 4
        + [pltpu.SemaphoreType.DMA((3,)), pltpu.SemaphoreType.DMA((3,)), pltpu.SemaphoreType.DMA((4,)),
           pltpu.SemaphoreType.DMA((4,)), pltpu.SemaphoreType.DMA],
        compiler_params=pltpu.CompilerParams(vmem_limit_bytes=VMEM_LIMIT),
    )(msg, ada_w, block)


def _grads_to_sibling(which, arrs):
    n = len(arrs)

    def body(*refs):
        srcs, dsts = refs[:n], refs[n:2 * n]
        send_sems, recv_sems = refs[2 * n:]
        copies = _sibling_copies(srcs, dsts, send_sems, recv_sems)
        for cp in copies:
            cp.start()
        for cp in copies:
            cp.wait_recv()
        for cp in copies:
            cp.wait_send()

    return pl.pallas_call(
        body,
        name=which + "_grads_to_sibling",
        in_specs=[ANY] * n,
        out_specs=[ANY] * n,
        out_shape=[jax.ShapeDtypeStruct((4,) + a.shape[2:], a.dtype) for a in arrs],
        scratch_shapes=[pltpu.SemaphoreType.DMA((4 * n,)), pltpu.SemaphoreType.DMA((4 * n,))],
    )(*arrs)


def _sibling_copies(srcs, dsts, send_sems, recv_sems):
    x, y, c = _position()
    copies = []
    for a, (src, dst) in enumerate(zip(srcs, dsts)):
        for k in range(4):
            copies.append(pltpu.make_async_remote_copy(
                src_ref=src.at[k, 1 - c], dst_ref=dst.at[k], send_sem=send_sems.at[4 * a + k],
                recv_sem=recv_sems.at[4 * a + k], device_id=(x, y, 1 - c), device_id_type=MESH))
    return copies


def _row_block(rows):
    return 256 if rows % 256 == 0 else rows // 2


def _pair_sum(pos, mine, recv):
    _, _, rows, cols = mine.shape
    rb = _row_block(rows)

    def body(pos_ref, mine_ref, recv_ref, out_ref):
        out_ref[0] = (mine_ref[0, 0] + recv_ref[0]).astype(BF16)

    other = lambda k, pos: jnp.bitwise_xor(pos[1], k + 1)
    return pl.pallas_call(
        body,
        name="grad_pair_sum",
        grid_spec=pltpu.PrefetchScalarGridSpec(
            num_scalar_prefetch=1, grid=(3, rows // rb),
            in_specs=[pl.BlockSpec((1, 1, rb, cols), lambda k, r, pos: (other(k, pos), pos[0], r, 0)),
                      pl.BlockSpec((1, rb, cols), lambda k, r, pos: (other(k, pos), r, 0))],
            out_specs=pl.BlockSpec((1, rb, cols), lambda k, r, pos: (other(k, pos), r, 0))),
        out_shape=jax.ShapeDtypeStruct((4, rows, cols), BF16),
        compiler_params=pltpu.CompilerParams(dimension_semantics=("arbitrary", "arbitrary")),
    )(pos, mine, recv)


def _final_sum(pos, mine, recv, chips):
    _, _, rows, cols = mine.shape
    rb = _row_block(rows)

    def body(pos_ref, mine_ref, recv_ref, chips_ref, out_ref):
        g = mine_ref[0, 0] + recv_ref[0]
        for j in range(3):
            g = g + chips_ref[j].astype(F32)
        out_ref[...] = g

    return pl.pallas_call(
        body,
        name="grad_final_sum",
        grid_spec=pltpu.PrefetchScalarGridSpec(
            num_scalar_prefetch=1, grid=(rows // rb,),
            in_specs=[pl.BlockSpec((1, 1, rb, cols), lambda r, pos: (pos[1], pos[0], r, 0)),
                      pl.BlockSpec((1, rb, cols), lambda r, pos: (pos[1], r, 0)),
                      pl.BlockSpec((3, rb, cols), lambda r, pos: (0, r, 0))],
            out_specs=pl.BlockSpec((rb, cols), lambda r, pos: (r, 0))),
        out_shape=jax.ShapeDtypeStruct((rows, cols), F32),
        compiler_params=pltpu.CompilerParams(dimension_semantics=("arbitrary",)),
    )(pos, mine, recv, chips)


LOSS_ROW = V_ROWS + 8
GB_ROW = LOSS_ROW + 7


def _sibling_and_route(gmod8, sc_t, parts):
    cols = gmod8.shape[1]
    n = len(parts)

    def body(gmod_ref, sct_ref, *refs):
        srcs, (gadaw_ref, gb_ref), dsts = refs[:n], refs[n:n + 2], refs[n + 2:2 * n + 2]
        sendbuf, grecv, send_a, recv_a, sib_send, sib_recv = refs[2 * n + 2:]
        sib_copies = _sibling_copies(srcs, dsts, sib_send, sib_recv)
        for cp in sib_copies:
            cp.start()
        me = _position()
        me_lin = _linear(me)
        gm = gmod_ref[...]
        for b in range(N_DEV):
            sendbuf[b] = jnp.broadcast_to(gm[b:b + 1, :], (HALO, cols))
        grecv[me_lin] = sendbuf[me_lin]

        def row_copy(k, dst_lin):
            peer = _flip(me, k)
            return pltpu.make_async_remote_copy(
                src_ref=sendbuf.at[_linear(peer)], dst_ref=grecv.at[dst_lin], send_sem=send_a.at[k - 1],
                recv_sem=recv_a.at[k - 1], device_id=peer, device_id_type=MESH)

        _exchange_all(lambda k: row_copy(k, me_lin), lambda k: row_copy(k, _linear(_flip(me, k))))
        g_all = grecv[:, 0, :]
        g_pad = jnp.concatenate([g_all, jnp.zeros((sct_ref.shape[1] - N_DEV, cols), F32)], axis=0).astype(BF16)
        gadaw_ref[...] = _dot(sct_ref[...], g_pad)
        gb_ref[...] = jnp.broadcast_to(_colsum(g_all), (HALO, cols))
        for cp in sib_copies:
            cp.wait_recv()
        for cp in sib_copies:
            cp.wait_send()

    return pl.pallas_call(
        body,
        name="sibling_and_route",
        in_specs=[WHOLE, WHOLE] + [ANY] * n,
        out_specs=[WHOLE, WHOLE] + [ANY] * n,
        out_shape=[jax.ShapeDtypeStruct((D, cols), F32), jax.ShapeDtypeStruct((HALO, cols), F32)]
        + [jax.ShapeDtypeStruct((4,) + p.shape[2:], p.dtype) for p in parts],
        scratch_shapes=[pltpu.VMEM((N_DEV, HALO, cols), F32), pltpu.VMEM((N_DEV, HALO, cols), F32),
                        pltpu.SemaphoreType.DMA((N_DEV - 1,)), pltpu.SemaphoreType.DMA((N_DEV - 1,)),
                        pltpu.SemaphoreType.DMA((4 * n,)), pltpu.SemaphoreType.DMA((4 * n,))],
        compiler_params=pltpu.CompilerParams(vmem_limit_bytes=VMEM_LIMIT),
    )(gmod8, sc_t, *parts)


def _chips_and_gather(msg_vec, msg_gate, gb_rows, chip_sums):
    cols = gb_rows.shape[1]
    n = len(chip_sums)

    def body(vec_ref, gate_ref, gb_ref, *refs):
        srcs, (sumv_ref, sumg_ref, gball_ref), dsts = refs[:n], refs[n:n + 3], refs[n + 3:2 * n + 3]
        bufv, bufg, send_v, recv_v, send_g, recv_g, chip_send, chip_recv = refs[2 * n + 3:]
        chip_copies = _chip_scatter_copies(srcs, dsts, chip_send, chip_recv)
        for cp in chip_copies:
            cp.start()
        me = _position()
        me_lin = _linear(me)
        bufv[me_lin] = vec_ref[...]
        bufv[me_lin, GB_ROW:GB_ROW + 1, 0:cols] = gb_ref[0:1, :]
        bufg[me_lin] = gate_ref[...]

        def gather_copy(buf, sends, recvs, k, src_lin):
            return pltpu.make_async_remote_copy(
                src_ref=buf.at[src_lin], dst_ref=buf.at[src_lin], send_sem=sends.at[k - 1], recv_sem=recvs.at[k - 1],
                device_id=_flip(me, k), device_id_type=MESH)

        bufs = ((bufv, send_v, recv_v), (bufg, send_g, recv_g))
        copies = [gather_copy(*b, k, me_lin) for b in bufs for k in range(1, N_DEV)]
        for cp in copies:
            cp.start()
        for b in bufs:
            for k in range(1, N_DEV):
                gather_copy(*b, k, _linear(_flip(me, k))).wait_recv()
        for cp in copies:
            cp.wait_send()
        sv = bufv[0]
        sg = bufg[0]
        for d in range(1, N_DEV):
            sv = sv + bufv[d]
            sg = sg + bufg[d]
        sumv_ref[...] = sv
        sumg_ref[...] = sg
        gball_ref[...] = bufv[:, GB_ROW, :]
        for cp in chip_copies:
            cp.wait_recv()
        for cp in chip_copies:
            cp.wait_send()

    n_chip = len(CHIP_FLIPS) * n
    return pl.pallas_call(
        body,
        name="chips_and_gather",
        in_specs=[WHOLE] * 3 + [ANY] * n,
        out_specs=[WHOLE] * 3 + [ANY] * n,
        out_shape=[jax.ShapeDtypeStruct(msg_vec.shape, F32), jax.ShapeDtypeStruct(msg_gate.shape, F32),
                   jax.ShapeDtypeStruct((N_DEV, D), F32)]
        + [jax.ShapeDtypeStruct((len(CHIP_FLIPS),) + s.shape[1:], s.dtype) for s in chip_sums],
        scratch_shapes=[pltpu.VMEM((N_DEV,) + msg_vec.shape, F32), pltpu.VMEM((N_DEV,) + msg_gate.shape, F32)]
        + [pltpu.SemaphoreType.DMA((N_DEV - 1,))] * 4 + [pltpu.SemaphoreType.DMA((n_chip,))] * 2,
        compiler_params=pltpu.CompilerParams(vmem_limit_bytes=VMEM_LIMIT),
    )(msg_vec, msg_gate, gb_rows, *chip_sums)


def _adamw_math(w, g, m, v):
    m = ADAM_B1 * m + (1.0 - ADAM_B1) * g
    v = ADAM_B2 * v + (1.0 - ADAM_B2) * (g * g)
    m_hat = m / (1.0 - ADAM_B1 ** ADAM_STEP)
    v_hat = v / (1.0 - ADAM_B2 ** ADAM_STEP)
    delta = -ADAM_LR * (m_hat / (jnp.sqrt(v_hat) + ADAM_EPS) + ADAM_WD * w)
    return delta, m, v


def _adamw(name, w, g, m, v):
    rows, cols = w.shape
    rb = 256 if rows % 256 == 0 else rows

    def body(w_ref, g_ref, m_ref, v_ref, d_ref, mo_ref, vo_ref):
        d_ref[...], mo_ref[...], vo_ref[...] = _adamw_math(w_ref[...], g_ref[...], m_ref[...], v_ref[...])

    spec = pl.BlockSpec((rb, cols), lambda r: (r, 0))
    return pl.pallas_call(
        body,
        name="adamw_" + name,
        grid=(rows // rb,),
        in_specs=[spec] * 4,
        out_specs=[spec] * 3,
        out_shape=[jax.ShapeDtypeStruct((rows, cols), F32)] * 3,
        compiler_params=pltpu.CompilerParams(dimension_semantics=("arbitrary",)),
    )(w, g, m, v)


def _adamw_small(ws, gs, ms, vs, sigmoid_scaled):
    n = len(ws)

    def body(*refs):
        w_refs, g_refs, m_refs, v_refs = (refs[i * n:(i + 1) * n] for i in range(4))
        outs = refs[4 * n:]
        for i in range(n):
            w = w_refs[i][...]
            g = g_refs[i][...]
            if sigmoid_scaled[i]:
                g = g * _sigmoid(w)
            delta, m, v = _adamw_math(w, g, m_refs[i][...], v_refs[i][...])
            outs[4 * i][...] = g
            outs[4 * i + 1][...] = delta
            outs[4 * i + 2][...] = m
            outs[4 * i + 3][...] = v

    shapes = [jax.ShapeDtypeStruct(w.shape, F32) for w in ws for _ in range(4)]
    outs = pl.pallas_call(
        body,
        name="adamw_small",
        in_specs=[WHOLE] * (4 * n),
        out_specs=[WHOLE] * (4 * n),
        out_shape=shapes,
    )(*ws, *gs, *ms, *vs)
    return [outs[4 * i:4 * i + 4] for i in range(n)]


_WEIGHT_NAMES = ("ada_w", "ada_b", "norm1_g", "w_in", "lru_conv_w", "lru_conv_b", "gate_a_w", "gate_a_b", "gate_x_w",
                 "gate_x_b", "a_param", "short_conv_w", "lru_out_g", "conv_out_g", "w_out", "norm2_g", "w_mlp1",
                 "w_mlp2", "final_g")
_BIG = ("ada_w", "w_in", "w_out", "w_mlp1", "w_mlp2")


def kernel(x, c, ada_w, ada_b, norm1_g, w_in, lru_conv_w, lru_conv_b, gate_a_w, gate_a_b, gate_x_w, gate_x_b, a_param, short_conv_w, lru_out_g, conv_out_g, w_out, norm2_g, w_mlp1, w_mlp2, final_g, loss_target, m_ada_w, m_ada_b, m_norm1_g, m_w_in, m_lru_conv_w, m_lru_conv_b, m_gate_a_w, m_gate_a_b, m_gate_x_w, m_gate_x_b, m_a_param, m_short_conv_w, m_lru_out_g, m_conv_out_g, m_w_out, m_norm2_g, m_w_mlp1, m_w_mlp2, m_final_g, v_ada_w, v_ada_b, v_norm1_g, v_w_in, v_lru_conv_w, v_lru_conv_b, v_gate_a_w, v_gate_a_b, v_gate_x_w, v_gate_x_b, v_a_param, v_short_conv_w, v_lru_out_g, v_conv_out_g, v_w_out, v_norm2_g, v_w_mlp1, v_w_mlp2, v_final_g):
    given = dict(locals())
    weights = {n: given[n] for n in _WEIGHT_NAMES}
    xi, yi, ci = _position()
    me_lin = _linear((xi, yi, ci))
    hd = W // N_DEV

    mixer_block = jnp.concatenate([w_out[0], w_in[0].T], axis=0).astype(BF16)
    mlp_block = jnp.concatenate([w_mlp1[0].T, w_mlp2[0]], axis=0).astype(BF16)

    msg = (jnp.pad(c, ((0, HALO - 1), (0, 0)))
           + jnp.pad(lru_conv_w[0], ((1, HALO - 1 - CONV_L), (0, D - hd)))
           + jnp.pad(short_conv_w[0], ((1 + CONV_L, 0), (0, D - hd))))
    gath, mod_all, wmix = _gather_and_mod(msg, ada_w[0], mixer_block)
    sc_all = gath[:, 0, :]
    wl = jnp.transpose(gath[:, 1:1 + CONV_L, :hd], (1, 0, 2)).reshape(CONV_L, W)
    ws = jnp.transpose(gath[:, 1 + CONV_L:HALO, :hd], (1, 0, 2)).reshape(CONV_S, W)
    modraw = _pad_rows(mod_all[:, 0, :].reshape(6, D), HALO)
    adab = _pad_rows(ada_b.reshape(6, D), HALO)

    x2d, tgt = x[0], loss_target[0]
    gf = final_g.reshape(1, D)
    bda = _block_diag(gate_a_w[0]).astype(BF16)
    bdx = _block_diag(gate_x_w[0]).astype(BF16)
    avg = _block_diag(jnp.full((8, W // 8, W // 8), 8.0 / W, F32)).astype(BF16)
    wl8 = _pad_rows(wl, HALO)
    ws8 = _pad_rows(ws, HALO)
    mixer_small = (wl8, lru_conv_b, bda, bdx, gate_a_b, gate_x_b, a_param, ws8, lru_out_g, conv_out_g, avg)
    proj, hl, mixed, wmlp = _mixer_fwd(x2d, modraw, adab, norm1_g, *mixer_small, wmix, mlp_block)
    wmlp = _sibling_forward(wmlp)
    h2t, f, dx2, dz, dzt, vec2, loss8 = _mlp_fwd(x2d, mixed, tgt, modraw, adab, norm2_g, gf, wmlp)
    dh2, dw1, dw2t = _mlp_bwd(h2t, f, dz, dzt, wmlp)

    pos = jnp.stack([ci, 2 * xi + yi]).astype(jnp.int32)
    by_dest = lambda g: g.reshape((4, 2, -1) + g.shape[-1:])
    mlp_parts = [by_dest(dw1), by_dest(dw2t)]
    mlp_sib = _grads_to_sibling("mlp", mlp_parts)
    mlp_sums = [_pair_sum(pos, p, r) for p, r in zip(mlp_parts, mlp_sib)]
    gx, vec, hb, dproj_t, dmixed, ycat_t, xl_t, dgate, *mlp_chips = _mixer_bwd(
        x2d, mixed, dh2, dx2, proj, hl, modraw, adab, norm1_g, norm2_g, *mixer_small, wmix, mlp_sums)
    dwint = _matmul("wgrad_in", dproj_t, hb)
    dwout = _matmul("wgrad_out", ycat_t, dmixed)
    dgates = _matmul("wgrad_gate", xl_t, dgate)
    mix_parts = [by_dest(dwout), by_dest(dwint)]
    gmod8 = (jnp.pad(vec[0:5], ((0, 1), (0, 0))) + jnp.pad(vec2[0:1], ((5, 0), (0, 0)))).reshape(N_DEV, 6 * D // N_DEV)
    sc_t = jnp.pad(sc_all.T, ((0, 0), (0, 128 - N_DEV))).astype(BF16)
    g_adaw, gb_rows, *mix_sib = _sibling_and_route(gmod8, sc_t, mix_parts)
    mix_sums = [_pair_sum(pos, p, r) for p, r in zip(mix_parts, mix_sib)]
    loss_rows = jnp.pad(loss8[0:1], ((0, HALO - 1), (0, D - loss8.shape[1])))
    msg_vec = jnp.concatenate([vec, vec2, loss_rows], axis=0)
    msg_gate = jnp.stack([_diag_blocks(dgates[:, :W]), _diag_blocks(dgates[:, W:])]).reshape(W, 128)
    sum_vec, sum_gate, gb, *mix_chips = _chips_and_gather(msg_vec, msg_gate, gb_rows, mix_sums)
    g_w1, g_w2t, g_wout, g_wint = (_final_sum(pos, p, r, q) for p, r, q in zip(
        mlp_parts + mix_parts, list(mlp_sib) + list(mix_sib), list(mlp_chips) + list(mix_chips)))
    loss = sum_vec[LOSS_ROW, 0]
    sum_gate = sum_gate.reshape(2, W, W // 8)
    lo, hi = slice(0, W), slice(W, 2 * W)
    wl_full = sum_vec[V_WL01:V_WL23 + 1].reshape(CONV_L, W)
    ws_full = sum_vec[V_WS01:V_WS2 + 1].reshape(CONV_S + 1, W)[:CONV_S]
    row = lambda r, cols: sum_vec[r:r + 1, cols]
    small_grads = {
        "ada_b": gb[:, :6 * D // N_DEV].reshape(1, 6 * D),
        "norm1_g": row(V_G1, slice(0, D)),
        "lru_conv_w": lax.dynamic_slice(wl_full, (0, me_lin * hd), (CONV_L, hd)),
        "lru_conv_b": row(V_BL_BA, lo),
        "gate_a_w": sum_gate[0],
        "gate_a_b": row(V_BL_BA, hi),
        "gate_x_w": sum_gate[1],
        "gate_x_b": row(V_BX_SP, lo),
        "a_param": row(V_BX_SP, hi),
        "short_conv_w": lax.dynamic_slice(ws_full, (0, me_lin * hd), (CONV_S, hd)),
        "lru_out_g": row(V_GL_GC, lo),
        "conv_out_g": row(V_GL_GC, hi),
        "norm2_g": row(V_G2, slice(0, D)),
        "final_g": sum_vec[V_ROWS + 1:V_ROWS + 2, :],
    }
    names = list(small_grads)
    as2d = lambda a, n: a.reshape(small_grads[n].shape)
    small = _adamw_small([as2d(weights[n], n) for n in names], [small_grads[n] for n in names],
                         [as2d(given["m_" + n], n) for n in names], [as2d(given["v_" + n], n) for n in names],
                         [n == "a_param" for n in names])
    result = {n: tuple(o.reshape(weights[n].shape) for o in outs) for n, outs in zip(names, small)}

    big_grads = {"ada_w": g_adaw, "w_in": g_wint.T, "w_out": g_wout, "w_mlp1": g_w1, "w_mlp2": g_w2t.T}
    for n in _BIG:
        g = big_grads[n]
        delta, new_m, new_v = _adamw(n, weights[n][0], g, given["m_" + n][0], given["v_" + n][0])
        result[n] = tuple(o[None] for o in (g, delta, new_m, new_v))

    return (loss, gx[None], *[result[n][0] for n in _WEIGHT_NAMES], *[result[n][1] for n in _WEIGHT_NAMES],
            *[result[n][2] for n in _WEIGHT_NAMES], *[result[n][3] for n in _WEIGHT_NAMES])
```

```python
import functools

import jax
import jax.numpy as jnp
from jax import lax
from jax.experimental import pallas as pl
from jax.experimental.pallas import tpu as pltpu

F32 = jnp.float32
BF16 = jnp.bfloat16
MESH = pl.DeviceIdType.MESH

N_DEV = 8
D = 1024
W = 512
D_IN = 5 * W
D_FF = 4096
FF_BLK = D_FF // N_DEV
EPS = 1e-6
C_GATE = 8.0
CONV_L = 4
CONV_S = 3
HALO = 8

ROWS_W1T, ROWS_W2, ROWS_WOUT, ROWS_WIN = FF_BLK, FF_BLK, D // N_DEV, D_IN // N_DEV
OFF_WOUT = 0
OFF_WIN = OFF_WOUT + ROWS_WOUT
MIX_ROWS = OFF_WIN + ROWS_WIN
OFF_W1T = 0
OFF_W2 = OFF_W1T + ROWS_W1T
MLP_ROWS = OFF_W2 + ROWS_W2
CHIP_FLIPS = (4, 2, 6)

ADAM_LR = 0.001
ADAM_B1 = 0.9
ADAM_B2 = 0.999
ADAM_EPS = 1e-08
ADAM_WD = 0.01
ADAM_STEP = 10

VMEM_LIMIT = 56 * 1024 * 1024

TB_MIX = 256
TB_MIXB = 256
TB_MLP = 256
TB_MLPB = 512

ANY = pl.BlockSpec(memory_space=pl.ANY)
WHOLE = pl.BlockSpec(memory_space=pltpu.VMEM)


def _dot(a, b):
    return jnp.dot(a, b, preferred_element_type=F32)


def _dot_nt(a, b):
    return lax.dot_general(a, b, (((1,), (1,)), ((), ())), preferred_element_type=F32)


def _dot_tn(a, b):
    return lax.dot_general(a, b, (((0,), (0,)), ((), ())), preferred_element_type=F32)


def _sigmoid(v):
    return 1.0 / (1.0 + jnp.exp(-v))


def _softplus(v):
    t = jnp.exp(-jnp.abs(v))
    small = t * (1.0 - t * (0.5 - t * (1.0 / 3.0)))
    return jnp.maximum(v, 0.0) + jnp.where(t < 1e-2, small, jnp.log(1.0 + t))


def _one_minus_sq(a, log_a):
    return -jnp.tanh(log_a) * (a * a + 1.0)


_GELU_K = 0.7978845608028654
_GELU_C = 0.044715


def _gelu(u):
    th = jnp.tanh(_GELU_K * (u + _GELU_C * u * u * u))
    return 0.5 * u * (1.0 + th), th


def _gelu_grad(u, th):
    return 0.5 * (1.0 + th) + 0.5 * u * (1.0 - th * th) * _GELU_K * (1.0 + 3.0 * _GELU_C * u * u)


def _group_mean(v, avg):
    hi = v.astype(BF16)
    lo = (v - hi.astype(F32)).astype(BF16)
    return _dot(hi, avg) + _dot(lo, avg)


def _colsum(v):
    return jnp.sum(v, axis=0, keepdims=True)


def _rowmean(v):
    return jnp.mean(v, axis=-1, keepdims=True)


def _load_packed(wpack_hbm, off, rows, dst, sem):
    copies = [
        pltpu.make_async_copy(wpack_hbm.at[d, pl.ds(off, rows), :], dst.at[pl.ds(d * rows, rows), :], sem)
        for d in range(N_DEV)
    ]
    for cp in copies:
        cp.start()
    return copies


def _scan_groups(n_groups, a_ref, b_ref, out_ref, carry_ref, reverse):
    row = lax.broadcasted_iota(jnp.int32, (HALO, W), 0)

    def step(k, carry):
        g = (n_groups - 1 - k) if reverse else k
        rows = pl.ds(pl.multiple_of(g * HALO, HALO), HALO)
        a = a_ref[rows, :]
        b = b_ref[rows, :]
        for s in (1, 2, 4):
            if reverse:
                keep = row < HALO - s
                sh = HALO - s
            else:
                keep = row >= s
                sh = s
            a_sh = pltpu.roll(a, sh, axis=0)
            b_sh = pltpu.roll(b, sh, axis=0)
            b = jnp.where(keep, a * b_sh + b, b)
            a = jnp.where(keep, a * a_sh, a)
        h = b + a * carry
        out_ref[rows, :] = h
        edge = h[0:1, :] if reverse else h[HALO - 1:HALO, :]
        return jnp.broadcast_to(edge, (HALO, W))

    carry_ref[...] = lax.fori_loop(0, n_groups, step, carry_ref[...])


def _route_peers(me):
    x, y, c = me
    first = ((x + 1 - c) % 2, (y + c) % 2, c)
    second = ((x + c) % 2, (y + 1 - c) % 2, c)
    return first, second, (1 - x, 1 - y, c)


def _chip_gather_copies(block_hbm, out_hbm, send_sems, recv_sems):
    me = _position()
    first, second, diag = _route_peers(me)

    def copy(j, src, slot_of, to):
        return pltpu.make_async_remote_copy(
            src_ref=src, dst_ref=out_hbm.at[_linear(slot_of)], send_sem=send_sems.at[j], recv_sem=recv_sems.at[j],
            device_id=to, device_id_type=MESH)

    own_sends = [copy(0, block_hbm, me, first), copy(1, block_hbm, me, second)]
    forward = copy(2, out_hbm.at[_linear(first)], first, second)
    arrivals = [copy(0, block_hbm, first, first), copy(1, block_hbm, second, second), copy(2, block_hbm, diag, second)]
    return own_sends, forward, arrivals


def _mixer_fwd(x, modraw, adab, g1, wl, bl, bda, bdx, ba, bxb, ap, ws, gl, gc, avg, wpack, mlp_block):
    t_len = x.shape[0]
    tb = TB_MIX
    nb = t_len // tb

    def body(x_ref, modraw_ref, adab_ref, g1_ref, wl_ref, bl_ref, bda_ref, bdx_ref, ba_ref, bxb_ref, ap_ref,
             ws_ref, gl_ref, gc_ref, avg_ref, wpack_hbm, block_hbm, proj_ref, hl_ref, mixed_ref, wmlp_hbm,
             win_v, wout_v, sem, ulx_ext, cv_ext, hcar, a_s, b_s, send_sems, recv_sems, local_sem):
        i = pl.program_id(0)
        own = pltpu.make_async_copy(block_hbm, wmlp_hbm.at[_linear(_position())], local_sem)
        sends, forward, arrivals = _chip_gather_copies(block_hbm, wmlp_hbm, send_sems, recv_sems)

        @pl.when(i == 0)
        def _():
            own.start()
            for cp in sends:
                cp.start()

        @pl.when(i == nb - 1)
        def _():
            arrivals[0].wait_recv()
            forward.start()

        @pl.when(i == 0)
        def _():
            cps = _load_packed(wpack_hbm, OFF_WIN, ROWS_WIN, win_v, sem.at[0])
            cps += _load_packed(wpack_hbm, OFF_WOUT, ROWS_WOUT, wout_v, sem.at[1])
            ulx_ext[0:HALO, :] = jnp.zeros((HALO, W), F32)
            cv_ext[0:HALO, :] = jnp.zeros((HALO, W), F32)
            hcar[...] = jnp.zeros((HALO, W), F32)
            for cp in cps:
                cp.wait()

        mod = modraw_ref[...] + adab_ref[...]
        shift1, scale1, gate1 = mod[0:1], mod[1:2], mod[2:3]
        x = x_ref[...]
        r1 = lax.rsqrt(_rowmean(x * x) + EPS)
        h = (x * r1 * g1_ref[...]) * (1.0 + scale1) + shift1
        proj = _dot_nt(h.astype(BF16), win_v[...])
        proj_ref[...] = proj
        u_lx, u_ly, u_b, u_c, u_v = (proj[:, k * W:(k + 1) * W] for k in range(5))

        ulx_ext[HALO:HALO + tb, :] = u_lx
        xl = bl_ref[...] + wl_ref[CONV_L - 1:CONV_L, :] * u_lx
        for k in range(CONV_L - 1):
            xl = xl + wl_ref[k:k + 1, :] * ulx_ext[pl.ds(HALO - (CONV_L - 1) + k, tb), :]
        ulx_ext[0:HALO, :] = ulx_ext[tb:tb + HALO, :]
        xlb = xl.astype(BF16)
        r = _sigmoid(_dot(xlb, bda_ref[...]) + ba_ref[...])
        ig = _sigmoid(_dot(xlb, bdx_ref[...]) + bxb_ref[...])
        log_a = (-C_GATE) * r * _softplus(ap_ref[...])
        a = jnp.exp(log_a)
        mult = jnp.sqrt(_one_minus_sq(a, log_a))
        grow = i * tb + lax.broadcasted_iota(jnp.int32, (tb, W), 0)
        mult = jnp.where(grow == 0, 1.0, mult)
        a_s[...] = a
        b_s[...] = mult * (ig * xl)
        _scan_groups(tb // HALO, a_s, b_s, hl_ref, hcar, reverse=False)
        hl = hl_ref[...]
        ge, _ = _gelu(u_ly)
        p = ge * hl
        y_lru = p * lax.rsqrt(_group_mean(p * p, avg_ref[...]) + EPS) * gl_ref[...]

        cv = u_c * u_v
        cv_ext[HALO:HALO + tb, :] = cv
        cc = ws_ref[CONV_S - 1:CONV_S, :] * cv
        for k in range(CONV_S - 1):
            cc = cc + ws_ref[k:k + 1, :] * cv_ext[pl.ds(HALO - (CONV_S - 1) + k, tb), :]
        cv_ext[0:HALO, :] = cv_ext[tb:tb + HALO, :]
        q = u_b * cc
        y_conv = q * lax.rsqrt(_group_mean(q * q, avg_ref[...]) + EPS) * gc_ref[...]

        mixed_ref[...] = (_dot(y_lru.astype(BF16), wout_v[0:W, :]) + _dot(y_conv.astype(BF16), wout_v[W:2 * W, :]))

        @pl.when(i == nb - 1)
        def _():
            for cp in arrivals[1:]:
                cp.wait_recv()
            for cp in sends + [forward]:
                cp.wait_send()
            own.wait()

    tok = lambda cols: pl.BlockSpec((tb, cols), lambda i: (i, 0))
    full = lambda a: pl.BlockSpec(a.shape, lambda i: (0,) * a.ndim)
    small = (modraw, adab, g1, wl, bl, bda, bdx, ba, bxb, ap, ws, gl, gc, avg)
    n_chips = len(CHIP_FLIPS)
    return pl.pallas_call(
        body,
        name="mixer_fwd",
        grid=(nb,),
        in_specs=[tok(D)] + [full(a) for a in small] + [ANY, ANY],
        out_specs=[tok(D_IN), tok(W), tok(D), ANY],
        out_shape=[jax.ShapeDtypeStruct((t_len, D_IN), F32), jax.ShapeDtypeStruct((t_len, W), F32),
                   jax.ShapeDtypeStruct((t_len, D), F32), jax.ShapeDtypeStruct((N_DEV,) + mlp_block.shape, BF16)],
        scratch_shapes=[pltpu.VMEM((D_IN, D), BF16), pltpu.VMEM((D, D), BF16), pltpu.SemaphoreType.DMA((2,)),
                        pltpu.VMEM((tb + HALO, W), F32), pltpu.VMEM((tb + HALO, W), F32), pltpu.VMEM((HALO, W), F32),
                        pltpu.VMEM((tb, W), F32), pltpu.VMEM((tb, W), F32),
                        pltpu.SemaphoreType.DMA((n_chips,)), pltpu.SemaphoreType.DMA((n_chips,)), pltpu.SemaphoreType.DMA],
        compiler_params=pltpu.CompilerParams(dimension_semantics=("arbitrary",), vmem_limit_bytes=VMEM_LIMIT),
    )(x, *small, wpack, mlp_block)


def _sibling_forward(wmlp):
    def body(in_hbm, out_hbm, send_sems, recv_sems):
        x, y, c = _position()
        copies, arrivals = [], []
        for j, k in enumerate((0,) + CHIP_FLIPS):
            mine = out_hbm.at[_linear(_flip((x, y, c), k))]
            theirs = out_hbm.at[_linear(_flip((x, y, 1 - c), k))]
            copies.append(pltpu.make_async_remote_copy(
                src_ref=mine, dst_ref=mine, send_sem=send_sems.at[j], recv_sem=recv_sems.at[j],
                device_id=(x, y, 1 - c), device_id_type=MESH))
            arrivals.append(pltpu.make_async_remote_copy(
                src_ref=theirs, dst_ref=theirs, send_sem=send_sems.at[j], recv_sem=recv_sems.at[j],
                device_id=(x, y, 1 - c), device_id_type=MESH))
        for cp in copies:
            cp.start()
        for cp in arrivals:
            cp.wait_recv()
        for cp in copies:
            cp.wait_send()

    return pl.pallas_call(
        body,
        name="sibling_forward",
        in_specs=[ANY],
        out_specs=ANY,
        out_shape=jax.ShapeDtypeStruct(wmlp.shape, wmlp.dtype),
        input_output_aliases={0: 0},
        scratch_shapes=[pltpu.SemaphoreType.DMA((4,)), pltpu.SemaphoreType.DMA((4,))],
    )(wmlp)


def _mlp_fwd(x, mixed, tgt, modraw, adab, g2, gf, wpack):
    t_len = x.shape[0]
    tb = TB_MLP
    nb = t_len // tb

    def body(x_ref, mixed_ref, tgt_ref, modraw_ref, adab_ref, g2_ref, gf_ref, wpack_hbm,
             h2t_ref, f_ref, dx2_ref, dz_ref, dzt_ref, vec_ref, loss_ref, w1t_v, w2_v, sem):
        i = pl.program_id(0)

        @pl.when(i == 0)
        def _():
            cps = _load_packed(wpack_hbm, OFF_W1T, ROWS_W1T, w1t_v, sem.at[0])
            cps += _load_packed(wpack_hbm, OFF_W2, ROWS_W2, w2_v, sem.at[1])
            vec_ref[...] = jnp.zeros(vec_ref.shape, F32)
            loss_ref[...] = jnp.zeros(loss_ref.shape, F32)
            for cp in cps:
                cp.wait()

        mod = modraw_ref[...] + adab_ref[...]
        gate1, shift2, scale2, gate2 = mod[2:3], mod[3:4], mod[4:5], mod[5:6]
        x1 = x_ref[...] + gate1 * mixed_ref[...]
        r2 = lax.rsqrt(_rowmean(x1 * x1) + EPS)
        h2 = (x1 * r2 * g2_ref[...]) * (1.0 + scale2) + shift2
        h2b = h2.astype(BF16)
        h2t_ref[...] = h2.T.astype(BF16)
        z = jnp.zeros((tb, D), F32)
        for j in range(N_DEV):
            cols = slice(j * FF_BLK, (j + 1) * FF_BLK)
            fj = _dot_nt(h2b, w1t_v[cols, :])
            f_ref[:, cols] = fj
            rf = jnp.maximum(fj, 0.0)
            z = z + _dot((rf * rf).astype(BF16), w2_v[cols, :])
        x2 = x1 + gate2 * z
        r3 = lax.rsqrt(_rowmean(x2 * x2) + EPS)
        xn3 = x2 * r3
        diff = xn3 * gf_ref[...] - tgt_ref[...]
        sq = _colsum(diff * diff)
        loss_ref[...] += jnp.broadcast_to(jnp.sum(sq, axis=1, keepdims=True) * (0.5 / D), loss_ref.shape)
        dy = diff * (1.0 / D)
        dyn = dy * gf_ref[...]
        dx2 = r3 * (dyn - xn3 * _rowmean(dyn * xn3))
        dx2_ref[...] = dx2
        dz = gate2 * dx2
        dz_ref[...] = dz.astype(BF16)
        dzt_ref[...] = dz.T.astype(BF16)
        vec_ref[0:1, :] += _colsum(dx2 * z)
        vec_ref[1:2, :] += _colsum(dy * xn3)

    tok = lambda cols: pl.BlockSpec((tb, cols), lambda i: (i, 0))
    tok_t = pl.BlockSpec((D, tb), lambda i: (0, i))
    full = lambda a: pl.BlockSpec(a.shape, lambda i: (0,) * a.ndim)
    small = (modraw, adab, g2, gf)
    return pl.pallas_call(
        body,
        name="mlp_fwd",
        grid=(nb,),
        in_specs=[tok(D), tok(D), tok(D)] + [full(a) for a in small] + [ANY],
        out_specs=[tok_t, tok(D_FF), tok(D), tok(D), tok_t, pl.BlockSpec((8, D), lambda i: (0, 0)),
                   pl.BlockSpec((8, 128), lambda i: (0, 0))],
        out_shape=[jax.ShapeDtypeStruct((D, t_len), BF16), jax.ShapeDtypeStruct((t_len, D_FF), F32),
                   jax.ShapeDtypeStruct((t_len, D), F32), jax.ShapeDtypeStruct((t_len, D), BF16),
                   jax.ShapeDtypeStruct((D, t_len), BF16),
                   jax.ShapeDtypeStruct((8, D), F32), jax.ShapeDtypeStruct((8, 128), F32)],
        scratch_shapes=[pltpu.VMEM((D_FF, D), BF16), pltpu.VMEM((D_FF, D), BF16), pltpu.SemaphoreType.DMA((2,))],
        compiler_params=pltpu.CompilerParams(dimension_semantics=("arbitrary",), vmem_limit_bytes=VMEM_LIMIT),
    )(x, mixed, tgt, *small, wpack)


def _mlp_bwd_half(pos, h2t, f, dz, dzt, wpack, prior=None):
    t_len = dz.shape[0]
    tb = TB_MLPB
    nb = t_len // tb
    first = prior is None
    flip = 1 if first else 0

    def body(pos_ref, h2t_ref, f_ref, dz_ref, dzt_ref, w1t_ref, w2_ref, *rest):
        if first:
            dh2_ref, dw1_ref, dw2t_ref = rest
        else:
            dh2in_ref, send1_hbm, send2_hbm, dh2_ref, dw1_ref, dw2t_ref, land1_hbm, land2_hbm, send_sems, recv_sems = rest
            x, y, c = _position()
            copies = [pltpu.make_async_remote_copy(
                src_ref=src, dst_ref=dst, send_sem=send_sems.at[a], recv_sem=recv_sems.at[a], device_id=(x, y, 1 - c),
                device_id_type=MESH) for a, (src, dst) in enumerate(((send1_hbm, land1_hbm), (send2_hbm, land2_hbm)))]
        k = pl.program_id(0)
        t = pl.program_id(1)

        if not first:
            @pl.when((k == 0) & (t == 0))
            def _():
                for cp in copies:
                    cp.start()

        rows = pl.ds(pl.multiple_of(t * tb, tb), tb)
        w1t = w1t_ref[0]
        w2 = w2_ref[0]
        rf = jnp.maximum(f_ref[...], 0.0)
        a2 = (rf * rf).astype(BF16)
        df = (_dot_nt(dz_ref[...], w2) * (2.0 * rf)).astype(BF16)
        g2 = _dot(dzt_ref[...], a2)
        g1 = _dot(h2t_ref[...], df)
        dh = _dot(df, w1t)

        @pl.when(t == 0)
        def _():
            dw2t_ref[0] = g2
            dw1_ref[0] = g1

        @pl.when(t != 0)
        def _():
            dw2t_ref[0] += g2
            dw1_ref[0] += g1

        @pl.when(k == 0)
        def _():
            dh2_ref[rows, :] = dh if first else dh2in_ref[...] + dh

        @pl.when(k != 0)
        def _():
            dh2_ref[rows, :] += dh

        if not first:
            @pl.when((k == 3) & (t == nb - 1))
            def _():
                for cp in copies:
                    cp.wait_recv()
                for cp in copies:
                    cp.wait_send()

    blk = lambda k, pos: 2 * k + jnp.bitwise_xor(pos[0], flip)
    in_specs = [pl.BlockSpec((D, tb), lambda k, t, pos: (0, t)),
                pl.BlockSpec((tb, FF_BLK), lambda k, t, pos: (t, blk(k, pos))),
                pl.BlockSpec((tb, D), lambda k, t, pos: (t, 0)),
                pl.BlockSpec((D, tb), lambda k, t, pos: (0, t)),
                pl.BlockSpec((1, ROWS_W1T, D), lambda k, t, pos: (blk(k, pos), OFF_W1T // ROWS_W1T, 0)),
                pl.BlockSpec((1, ROWS_W2, D), lambda k, t, pos: (blk(k, pos), OFF_W2 // ROWS_W2, 0))]
    grad_spec = pl.BlockSpec((1, D, FF_BLK), lambda k, t, pos: (k, 0, 0))
    out_specs = [pl.BlockSpec((t_len, D), lambda k, t, pos: (0, 0)), grad_spec, grad_spec]
    grad_shape = jax.ShapeDtypeStruct((4, D, FF_BLK), F32)
    out_shape = [jax.ShapeDtypeStruct((t_len, D), F32), grad_shape, grad_shape]
    args = [pos, h2t, f, dz, dzt, wpack, wpack]
    scratch = []
    if not first:
        in_specs += [pl.BlockSpec((tb, D), lambda k, t, pos: (jnp.where(k == 0, t, nb - 1), 0)), ANY, ANY]
        out_specs += [ANY, ANY]
        out_shape += [grad_shape, grad_shape]
        args += list(prior)
        scratch = [pltpu.SemaphoreType.DMA((2,)), pltpu.SemaphoreType.DMA((2,))]
    return pl.pallas_call(
        body,
        name="mlp_bwd_first" if first else "mlp_bwd_second",
        grid_spec=pltpu.PrefetchScalarGridSpec(num_scalar_prefetch=1, grid=(4, nb), in_specs=in_specs,
                                               out_specs=out_specs, scratch_shapes=scratch),
        out_shape=out_shape,
        compiler_params=pltpu.CompilerParams(dimension_semantics=("arbitrary", "arbitrary"),
                                             vmem_limit_bytes=VMEM_LIMIT),
    )(*args)


V_SHIFT1, V_SCALE1, V_GATE1, V_SHIFT2, V_SCALE2, V_G1, V_G2 = 0, 1, 2, 3, 4, 6, 7
V_BL_BA, V_BX_SP, V_GL_GC, V_WL01, V_WL23, V_WS01, V_WS2 = 8, 9, 10, 11, 12, 13, 14
V_ROWS = 16


def _chip_scatter_copies(srcs, dsts, send_sems, recv_sems):
    me = _position()
    copies = []
    for a, (src, dst) in enumerate(zip(srcs, dsts)):
        for j, k in enumerate(CHIP_FLIPS):
            peer = _flip(me, k)
            copies.append(pltpu.make_async_remote_copy(
                src_ref=src.at[2 * peer[0] + peer[1]], dst_ref=dst.at[j], send_sem=send_sems.at[len(CHIP_FLIPS) * a + j],
                recv_sem=recv_sems.at[len(CHIP_FLIPS) * a + j], device_id=peer, device_id_type=MESH))
    return copies


def _mixer_bwd(x, mixed, dh2, dx2, proj, hl, modraw, adab, g1, g2, wl, bl, bda, bdx, ba, bxb, ap, ws, gl, gc, avg, wpack,
               chip_sums):
    t_len = x.shape[0]
    tb = TB_MIXB
    nb = t_len // tb
    hb = tb // HALO
    n_sums = len(chip_sums)

    def body(x_ref, mixed_ref, dh2_ref, dx2_ref, proj_ref, projh_ref, hl_ref, hlh_ref,
             modraw_ref, adab_ref, g1_ref, g2_ref, wl_ref, bl_ref, bda_ref, bdx_ref, ba_ref, bxb_ref, ap_ref,
             ws_ref, gl_ref, gc_ref, avg_ref, wpack_hbm, *rest):
        sums_hbm, rest = rest[:n_sums], rest[n_sums:]
        gx_ref, vec_ref, hb_ref, dprojt_ref, dmixed_ref, ycatt_ref, xlt_ref, dgate_ref = rest[:8]
        landed_hbm, rest = rest[8:8 + n_sums], rest[8 + n_sums:]
        (win_v, wout_v, sem, ulx_ext, cv_ext, hl_ext, a_ext, dxl_ext, dcc_ext, dcar, an_s, g_s, dh_s,
         send_sems, recv_sems) = rest
        i = pl.program_id(0)
        blk = nb - 1 - i
        chip_copies = _chip_scatter_copies(sums_hbm, landed_hbm, send_sems, recv_sems)

        @pl.when(i == 0)
        def _():
            for cp in chip_copies:
                cp.start()
            cps = _load_packed(wpack_hbm, OFF_WIN, ROWS_WIN, win_v, sem.at[0])
            cps += _load_packed(wpack_hbm, OFF_WOUT, ROWS_WOUT, wout_v, sem.at[1])
            vec_ref[...] = jnp.zeros(vec_ref.shape, F32)
            zero = jnp.zeros((HALO, W), F32)
            a_ext[tb:tb + HALO, :] = zero
            dxl_ext[tb:tb + HALO, :] = zero
            dcc_ext[tb:tb + HALO, :] = zero
            dcar[...] = zero
            for cp in cps:
                cp.wait()

        mod = modraw_ref[...] + adab_ref[...]
        shift1, scale1, gate1, scale2 = mod[0:1], mod[1:2], mod[2:3], mod[4:5]
        x = x_ref[...]
        mixed = mixed_ref[...]

        x1 = x + gate1 * mixed
        r2 = lax.rsqrt(_rowmean(x1 * x1) + EPS)
        xn2 = x1 * r2
        dh2 = dh2_ref[...]
        vec_ref[V_SHIFT2:V_SHIFT2 + 1, :] += _colsum(dh2)
        vec_ref[V_SCALE2:V_SCALE2 + 1, :] += _colsum(dh2 * xn2 * g2_ref[...])
        vec_ref[V_G2:V_G2 + 1, :] += _colsum(dh2 * (1.0 + scale2) * xn2)
        dxn2 = dh2 * g2_ref[...] * (1.0 + scale2)
        dx1 = dx2_ref[...] + r2 * (dxn2 - xn2 * _rowmean(dxn2 * xn2))
        vec_ref[V_GATE1:V_GATE1 + 1, :] += _colsum(dx1 * mixed)
        dmixed = (gate1 * dx1).astype(BF16)

        proj = proj_ref[...]
        u_lx, u_ly, u_b, u_c, u_v = (proj[:, k * W:(k + 1) * W] for k in range(5))
        has_prev = (blk > 0).astype(F32)
        projh = projh_ref[...]
        ulx_ext[0:HALO, :] = projh[:, 0:W] * has_prev
        ulx_ext[HALO:HALO + tb, :] = u_lx
        xl = bl_ref[...] + wl_ref[CONV_L - 1:CONV_L, :] * u_lx
        for k in range(CONV_L - 1):
            xl = xl + wl_ref[k:k + 1, :] * ulx_ext[pl.ds(HALO - (CONV_L - 1) + k, tb), :]
        xlb = xl.astype(BF16)
        r = _sigmoid(_dot(xlb, bda_ref[...]) + ba_ref[...])
        ig = _sigmoid(_dot(xlb, bdx_ref[...]) + bxb_ref[...])
        sp = _softplus(ap_ref[...])
        log_a = (-C_GATE) * r * sp
        a = jnp.exp(log_a)
        mult_raw = jnp.sqrt(_one_minus_sq(a, log_a))
        first = (blk * tb + lax.broadcasted_iota(jnp.int32, (tb, W), 0)) == 0
        mult = jnp.where(first, 1.0, mult_raw)
        hl = hl_ref[...]
        ge, th = _gelu(u_ly)
        p = ge * hl
        rp = lax.rsqrt(_group_mean(p * p, avg_ref[...]) + EPS)
        pn = p * rp
        cv = u_c * u_v
        cv_ext[0:HALO, :] = projh[:, 3 * W:4 * W] * projh[:, 4 * W:5 * W] * has_prev
        cv_ext[HALO:HALO + tb, :] = cv
        cc = ws_ref[CONV_S - 1:CONV_S, :] * cv
        for k in range(CONV_S - 1):
            cc = cc + ws_ref[k:k + 1, :] * cv_ext[pl.ds(HALO - (CONV_S - 1) + k, tb), :]
        q = u_b * cc
        rq = lax.rsqrt(_group_mean(q * q, avg_ref[...]) + EPS)
        qn = q * rq

        dmixed_ref[...] = dmixed
        ycatt_ref[0:W, :] = (pn * gl_ref[...]).T.astype(BF16)
        ycatt_ref[W:2 * W, :] = (qn * gc_ref[...]).T.astype(BF16)
        dyl = _dot_nt(dmixed, wout_v[0:W, :])
        dyc = _dot_nt(dmixed, wout_v[W:2 * W, :])

        dqn = dyc * gc_ref[...]
        dq = rq * (dqn - qn * _group_mean(dqn * qn, avg_ref[...]))
        du_b = dq * cc
        dcc = dq * u_b
        dcc_ext[0:tb, :] = dcc
        dcv = ws_ref[CONV_S - 1:CONV_S, :] * dcc
        for k in range(CONV_S - 1):
            dcv = dcv + ws_ref[k:k + 1, :] * dcc_ext[pl.ds(CONV_S - 1 - k, tb), :]
        dcc_ext[tb:tb + HALO, :] = dcc_ext[0:HALO, :]
        du_c = dcv * u_v
        du_v = dcv * u_c
        dws = [_colsum(dcc * cv_ext[pl.ds(HALO - (CONV_S - 1) + k, tb), :]) for k in range(CONV_S)]

        dpn = dyl * gl_ref[...]
        dp = rp * (dpn - pn * _group_mean(dpn * pn, avg_ref[...]))
        du_ly = dp * hl * _gelu_grad(u_ly, th)
        g_s[...] = dp * ge
        a_ext[0:tb, :] = a
        an_s[...] = a_ext[pl.ds(1, tb), :]
        _scan_groups(hb, an_s, g_s, dh_s, dcar, reverse=True)
        a_ext[tb:tb + HALO, :] = a_ext[0:HALO, :]
        dh = dh_s[...]
        hl_ext[0:HALO, :] = hlh_ref[...] * has_prev
        hl_ext[HALO:HALO + tb, :] = hl
        da = dh * hl_ext[pl.ds(HALO - 1, tb), :]
        dmult = dh * (ig * xl)
        dig = dh * (mult * xl)
        dxl = dh * (mult * ig)
        dlog = da * a - jnp.where(first, 0.0, dmult * (a * a) / mult_raw)
        dr = dlog * ((-C_GATE) * sp)
        dsp = _colsum(dlog * ((-C_GATE) * r))
        dga = dr * r * (1.0 - r)
        dgx = dig * ig * (1.0 - ig)
        dgab = dga.astype(BF16)
        dgxb = dgx.astype(BF16)
        xlt_ref[...] = xl.T.astype(BF16)
        dgate_ref[:, 0:W] = dgab
        dgate_ref[:, W:2 * W] = dgxb
        dxl = dxl + _dot_nt(dgab, bda_ref[...]) + _dot_nt(dgxb, bdx_ref[...])
        dxl_ext[0:tb, :] = dxl
        du_lx = wl_ref[CONV_L - 1:CONV_L, :] * dxl
        for k in range(CONV_L - 1):
            du_lx = du_lx + wl_ref[k:k + 1, :] * dxl_ext[pl.ds(CONV_L - 1 - k, tb), :]
        dxl_ext[tb:tb + HALO, :] = dxl_ext[0:HALO, :]
        dwl = [_colsum(dxl * ulx_ext[pl.ds(HALO - (CONV_L - 1) + k, tb), :]) for k in range(CONV_L)]

        cat = lambda u, v: jnp.concatenate([u, v], axis=1)
        vec_ref[V_BL_BA:V_BL_BA + 1, :] += cat(_colsum(dxl), _colsum(dga))
        vec_ref[V_BX_SP:V_BX_SP + 1, :] += cat(_colsum(dgx), dsp)
        vec_ref[V_GL_GC:V_GL_GC + 1, :] += cat(_colsum(dyl * pn), _colsum(dyc * qn))
        vec_ref[V_WL01:V_WL01 + 1, :] += cat(dwl[0], dwl[1])
        vec_ref[V_WL23:V_WL23 + 1, :] += cat(dwl[2], dwl[3])
        vec_ref[V_WS01:V_WS01 + 1, :] += cat(dws[0], dws[1])
        vec_ref[V_WS2:V_WS2 + 1, 0:W] += dws[2]

        r1 = lax.rsqrt(_rowmean(x * x) + EPS)
        xn1 = x * r1
        hb_ref[...] = ((xn1 * g1_ref[...]) * (1.0 + scale1) + shift1).astype(BF16)
        dh_in = jnp.zeros((tb, D), F32)
        for k, du in enumerate((du_lx, du_ly, du_b, du_c, du_v)):
            dprojt_ref[k * W:(k + 1) * W, :] = du.T.astype(BF16)
            dh_in = dh_in + _dot(du.astype(BF16), win_v[k * W:(k + 1) * W, :])
        vec_ref[V_SHIFT1:V_SHIFT1 + 1, :] += _colsum(dh_in)
        vec_ref[V_SCALE1:V_SCALE1 + 1, :] += _colsum(dh_in * xn1 * g1_ref[...])
        vec_ref[V_G1:V_G1 + 1, :] += _colsum(dh_in * (1.0 + scale1) * xn1)
        dxn1 = dh_in * g1_ref[...] * (1.0 + scale1)
        gx_ref[...] = dx1 + r1 * (dxn1 - xn1 * _rowmean(dxn1 * xn1))

        @pl.when(i == nb - 1)
        def _():
            for cp in chip_copies:
                cp.wait_recv()
            for cp in chip_copies:
                cp.wait_send()

    rev = lambda cols: pl.BlockSpec((tb, cols), lambda i: (nb - 1 - i, 0))
    rev_t = lambda rows: pl.BlockSpec((rows, tb), lambda i: (0, nb - 1 - i))
    halo = lambda cols: pl.BlockSpec((HALO, cols), lambda i: (jnp.maximum((nb - 1 - i) * hb - 1, 0), 0))
    full = lambda a: pl.BlockSpec(a.shape, lambda i: (0,) * a.ndim)
    small = (modraw, adab, g1, g2, wl, bl, bda, bdx, ba, bxb, ap, ws, gl, gc, avg)
    ext = pltpu.VMEM((tb + HALO, W), F32)
    n_sems = len(CHIP_FLIPS) * n_sums
    return pl.pallas_call(
        body,
        name="mixer_bwd",
        grid=(nb,),
        in_specs=[rev(D), rev(D), rev(D), rev(D), rev(D_IN), halo(D_IN), rev(W), halo(W)]
        + [full(a) for a in small] + [ANY] * (1 + n_sums),
        out_specs=[rev(D), pl.BlockSpec((V_ROWS, D), lambda i: (0, 0)), rev(D), rev_t(D_IN), rev(D), rev_t(D),
                   rev_t(W), rev(2 * W)] + [ANY] * n_sums,
        out_shape=[jax.ShapeDtypeStruct((t_len, D), F32), jax.ShapeDtypeStruct((V_ROWS, D), F32),
                   jax.ShapeDtypeStruct((t_len, D), BF16), jax.ShapeDtypeStruct((D_IN, t_len), BF16),
                   jax.ShapeDtypeStruct((t_len, D), BF16), jax.ShapeDtypeStruct((D, t_len), BF16),
                   jax.ShapeDtypeStruct((W, t_len), BF16), jax.ShapeDtypeStruct((t_len, 2 * W), BF16)]
        + [jax.ShapeDtypeStruct((len(CHIP_FLIPS),) + s.shape[1:], s.dtype) for s in chip_sums],
        scratch_shapes=[pltpu.VMEM((D_IN, D), BF16), pltpu.VMEM((D, D), BF16), pltpu.SemaphoreType.DMA((2,)),
                        ext, ext, ext, ext, ext, ext, pltpu.VMEM((HALO, W), F32),
                        pltpu.VMEM((tb, W), F32), pltpu.VMEM((tb, W), F32), pltpu.VMEM((tb, W), F32),
                        pltpu.SemaphoreType.DMA((n_sems,)), pltpu.SemaphoreType.DMA((n_sems,))],
        compiler_params=pltpu.CompilerParams(dimension_semantics=("arbitrary",), vmem_limit_bytes=VMEM_LIMIT),
    )(x, mixed, dh2, dx2, proj, proj, hl, hl, *small, wpack, *chip_sums)


def _matmul(name, a, b, tm=512):
    m, k = a.shape
    n = b.shape[1]

    def body(a_ref, b_ref, o_ref):
        o_ref[...] = _dot(a_ref[...], b_ref[...])

    return pl.pallas_call(
        body,
        name=name,
        grid=(m // tm,),
        in_specs=[pl.BlockSpec((tm, k), lambda i: (i, 0)), pl.BlockSpec((k, n), lambda i: (0, 0))],
        out_specs=pl.BlockSpec((tm, n), lambda i: (i, 0)),
        out_shape=jax.ShapeDtypeStruct((m, n), F32),
        compiler_params=pltpu.CompilerParams(dimension_semantics=("arbitrary",), vmem_limit_bytes=VMEM_LIMIT),
    )(a, b)


def _block_diag(w):
    n, m, _ = w.shape
    eye = jnp.eye(n, dtype=w.dtype)
    return (w[:, :, None, :] * eye[:, None, :, None]).reshape(n * m, n * m)


def _diag_blocks(mat, n=8):
    m = mat.shape[0] // n
    return jnp.stack([mat[h * m:(h + 1) * m, h * m:(h + 1) * m] for h in range(n)])


def _pad_rows(a, rows):
    return jnp.pad(a, ((0, rows - a.shape[0]),) + ((0, 0),) * (a.ndim - 1))


def _position():
    return lax.axis_index("x"), lax.axis_index("y"), lax.axis_index("c")


def _linear(pos):
    return 4 * pos[0] + 2 * pos[1] + pos[2]


def _flip(pos, k):
    return tuple(1 - p if k & bit else p for p, bit in zip(pos, (4, 2, 1)))


def _exchange_all(make_copy, make_arrival):
    copies = [make_copy(k) for k in range(1, N_DEV)]
    for cp in copies:
        cp.start()
    for k in range(1, N_DEV):
        make_arrival(k).wait_recv()
    for cp in copies:
        cp.wait_send()


def _mod_exchange_steps(cols):
    def steps(msg_ref, adaw_ref, gath_ref, mod_ref, sendbuf, send_a, recv_a, send_b, recv_b):
        me = _position()
        me_lin = _linear(me)
        m = msg_ref[...]
        row = lax.broadcasted_iota(jnp.int32, m.shape, 0)
        gath_ref[me_lin] = jnp.where(row == 0, m * _sigmoid(m), m)

        def gather_copy(k, src_lin):
            return pltpu.make_async_remote_copy(
                src_ref=gath_ref.at[src_lin], dst_ref=gath_ref.at[src_lin], send_sem=send_a.at[k - 1],
                recv_sem=recv_a.at[k - 1], device_id=_flip(me, k), device_id_type=MESH)

        _exchange_all(lambda k: gather_copy(k, me_lin), lambda k: gather_copy(k, _linear(_flip(me, k))))

        sc_all = gath_ref[:, 0, :]
        scb = jnp.concatenate([sc_all, jnp.zeros_like(sc_all)], axis=0).astype(BF16)
        prod = _dot(scb, adaw_ref[...].astype(BF16))
        for b in range(N_DEV):
            sendbuf[b] = jnp.broadcast_to(prod[b:b + 1, :], (HALO, cols))
        mod_ref[me_lin] = sendbuf[me_lin]

        def row_copy(k, dst_lin):
            peer = _flip(me, k)
            return pltpu.make_async_remote_copy(
                src_ref=sendbuf.at[_linear(peer)], dst_ref=mod_ref.at[dst_lin], send_sem=send_b.at[k - 1],
                recv_sem=recv_b.at[k - 1], device_id=peer, device_id_type=MESH)

        _exchange_all(lambda k: row_copy(k, me_lin), lambda k: row_copy(k, _linear(_flip(me, k))))

    return steps


def _gather_and_mod(msg, ada_w, block):
    rows, cols = block.shape
    mod_cols = ada_w.shape[1]
    mod_steps = _mod_exchange_steps(mod_cols)

    def body(msg_ref, adaw_ref, x_ref, gath_ref, mod_ref, out_ref, sendbuf, send_a, recv_a, send_b, recv_b,
             send_sems, recv_sems, sib_send_sems, sib_recv_sems, local_sem):
        x, y, c = _position()
        me, sibling = (x, y, c), (x, y, 1 - c)
        sends, forward, arrivals = _chip_gather_copies(x_ref, out_ref, send_sems, recv_sems)

        def to_sibling(j, block_of, src=None):
            dst = out_ref.at[_linear(block_of)]
            return pltpu.make_async_remote_copy(
                src_ref=dst if src is None else src, dst_ref=dst, send_sem=sib_send_sems.at[j],
                recv_sem=sib_recv_sems.at[j], device_id=sibling, device_id_type=MESH)

        mine = pltpu.make_async_copy(x_ref, out_ref.at[_linear(me)], local_sem)
        mine.start()
        passes = [to_sibling(0, me, src=x_ref)] + [to_sibling(1 + j, p) for j, p in enumerate(_route_peers(me))]
        passes[0].start()
        for cp in sends:
            cp.start()
        mod_steps(msg_ref, adaw_ref, gath_ref, mod_ref, sendbuf, send_a, recv_a, send_b, recv_b)
        arrivals[0].wait_recv()
        forward.start()
        passes[1].start()
        arrivals[1].wait_recv()
        passes[2].start()
        arrivals[2].wait_recv()
        passes[3].start()
        for j, p in enumerate((sibling,) + _route_peers(sibling)):
            to_sibling(j, p).wait_recv()
        for cp in sends + [forward] + passes:
            cp.wait_send()
        mine.wait()

    return pl.pallas_call(
        body,
        name="gather_and_mod",
        in_specs=[WHOLE, WHOLE, ANY],
        out_specs=[WHOLE, WHOLE, ANY],
        out_shape=[jax.ShapeDtypeStruct((N_DEV, HALO, D), F32), jax.ShapeDtypeStruct((N_DEV, HALO, mod_cols), F32),
                   jax.ShapeDtypeStruct((N_DEV, rows, cols), block.dtype)],
        scratch_shapes=[pltpu.VMEM((N_DEV, HALO, mod_cols), F32)] + [pltpu.SemaphoreType.DMA((N_DEV - 1,))] * 4
        + [pltpu.SemaphoreType.DMA((3,)), pltpu.SemaphoreType.DMA((3,)), pltpu.SemaphoreType.DMA((4,)),
           pltpu.SemaphoreType.DMA((4,)), pltpu.SemaphoreType.DMA],
        compiler_params=pltpu.CompilerParams(vmem_limit_bytes=VMEM_LIMIT),
    )(msg, ada_w, block)


def _sibling_copies(srcs, dsts, send_sems, recv_sems):
    x, y, c = _position()
    copies = []
    for a, (src, dst) in enumerate(zip(srcs, dsts)):
        for k in range(4):
            copies.append(pltpu.make_async_remote_copy(
                src_ref=src.at[k, 1 - c], dst_ref=dst.at[k], send_sem=send_sems.at[4 * a + k],
                recv_sem=recv_sems.at[4 * a + k], device_id=(x, y, 1 - c), device_id_type=MESH))
    return copies


def _row_block(rows):
    return 256 if rows % 256 == 0 else rows // 2


def _pair_sum(pos, mine, recv):
    _, cores, rows, cols = mine.shape
    rb = _row_block(rows)

    def body(pos_ref, mine_ref, recv_ref, out_ref):
        out_ref[0] = (mine_ref[0, 0] + recv_ref[0]).astype(BF16)

    other = lambda k, pos: jnp.bitwise_xor(pos[1], k + 1)
    core = lambda pos: pos[0] * (cores - 1)
    return pl.pallas_call(
        body,
        name="grad_pair_sum",
        grid_spec=pltpu.PrefetchScalarGridSpec(
            num_scalar_prefetch=1, grid=(3, rows // rb),
            in_specs=[pl.BlockSpec((1, 1, rb, cols), lambda k, r, pos: (other(k, pos), core(pos), r, 0)),
                      pl.BlockSpec((1, rb, cols), lambda k, r, pos: (other(k, pos), r, 0))],
            out_specs=pl.BlockSpec((1, rb, cols), lambda k, r, pos: (other(k, pos), r, 0))),
        out_shape=jax.ShapeDtypeStruct((4, rows, cols), BF16),
        compiler_params=pltpu.CompilerParams(dimension_semantics=("arbitrary", "arbitrary")),
    )(pos, mine, recv)


def _final_sum(pos, mine, recv, chips):
    _, cores, rows, cols = mine.shape
    rb = _row_block(rows)

    def body(pos_ref, mine_ref, recv_ref, chips_ref, out_ref):
        g = mine_ref[0, 0] + recv_ref[0]
        for j in range(3):
            g = g + chips_ref[j].astype(F32)
        out_ref[...] = g

    return pl.pallas_call(
        body,
        name="grad_final_sum",
        grid_spec=pltpu.PrefetchScalarGridSpec(
            num_scalar_prefetch=1, grid=(rows // rb,),
            in_specs=[pl.BlockSpec((1, 1, rb, cols), lambda r, pos: (pos[1], pos[0] * (cores - 1), r, 0)),
                      pl.BlockSpec((1, rb, cols), lambda r, pos: (pos[1], r, 0)),
                      pl.BlockSpec((3, rb, cols), lambda r, pos: (0, r, 0))],
            out_specs=pl.BlockSpec((rb, cols), lambda r, pos: (r, 0))),
        out_shape=jax.ShapeDtypeStruct((rows, cols), F32),
        compiler_params=pltpu.CompilerParams(dimension_semantics=("arbitrary",)),
    )(pos, mine, recv, chips)


LOSS_ROW = V_ROWS + 8
GB_BASE = LOSS_ROW + 8


def _sibling_and_route(gmod8, sc_t, parts):
    cols = gmod8.shape[1]
    n = len(parts)

    def body(gmod_ref, sct_ref, *refs):
        srcs, (gadaw_ref, gb_ref), dsts = refs[:n], refs[n:n + 2], refs[n + 2:2 * n + 2]
        sendbuf, grecv, send_a, recv_a, sib_send, sib_recv = refs[2 * n + 2:]
        sib_copies = _sibling_copies(srcs, dsts, sib_send, sib_recv)
        for cp in sib_copies:
            cp.start()
        me = _position()
        me_lin = _linear(me)
        gm = gmod_ref[...]
        for b in range(N_DEV):
            sendbuf[b] = jnp.broadcast_to(gm[b:b + 1, :], (HALO, cols))
        grecv[me_lin] = sendbuf[me_lin]

        def row_copy(k, dst_lin):
            peer = _flip(me, k)
            return pltpu.make_async_remote_copy(
                src_ref=sendbuf.at[_linear(peer)], dst_ref=grecv.at[dst_lin], send_sem=send_a.at[k - 1],
                recv_sem=recv_a.at[k - 1], device_id=peer, device_id_type=MESH)

        _exchange_all(lambda k: row_copy(k, me_lin), lambda k: row_copy(k, _linear(_flip(me, k))))
        g_all = grecv[:, 0, :]
        g_pad = jnp.concatenate([g_all, jnp.zeros((sct_ref.shape[1] - N_DEV, cols), F32)], axis=0).astype(BF16)
        gadaw_ref[...] = _dot(sct_ref[...], g_pad)
        gb_ref[...] = jnp.broadcast_to(_colsum(g_all), (HALO, cols))
        for cp in sib_copies:
            cp.wait_recv()
        for cp in sib_copies:
            cp.wait_send()

    return pl.pallas_call(
        body,
        name="sibling_and_route",
        in_specs=[WHOLE, WHOLE] + [ANY] * n,
        out_specs=[WHOLE, WHOLE] + [ANY] * n,
        out_shape=[jax.ShapeDtypeStruct((D, cols), F32), jax.ShapeDtypeStruct((HALO, cols), F32)]
        + [jax.ShapeDtypeStruct((4,) + p.shape[2:], p.dtype) for p in parts],
        scratch_shapes=[pltpu.VMEM((N_DEV, HALO, cols), F32), pltpu.VMEM((N_DEV, HALO, cols), F32),
                        pltpu.SemaphoreType.DMA((N_DEV - 1,)), pltpu.SemaphoreType.DMA((N_DEV - 1,)),
                        pltpu.SemaphoreType.DMA((4 * n,)), pltpu.SemaphoreType.DMA((4 * n,))],
        compiler_params=pltpu.CompilerParams(vmem_limit_bytes=VMEM_LIMIT),
    )(gmod8, sc_t, *parts)


def _chips_and_gather(msg_vec, msg_gate, gb_rows, chip_sums):
    cols = gb_rows.shape[1]
    n = len(chip_sums)
    vec_rows = GB_BASE + N_DEV

    def body(vec_ref, gate_ref, gb_ref, *refs):
        srcs, (sumv_ref, sumg_ref), dsts = refs[:n], refs[n:n + 2], refs[n + 2:2 * n + 2]
        (myv, myg, sibv, sibg, chipv, chipg, sib_send, sib_recv, peer_send, peer_recv,
         chip_send, chip_recv) = refs[2 * n + 2:]
        chip_copies = _chip_scatter_copies(srcs, dsts, chip_send, chip_recv)
        for cp in chip_copies:
            cp.start()
        x, y, c = me = _position()
        my_chip = 2 * x + y
        myv[0:GB_BASE, :] = vec_ref[...]
        slot = lax.broadcasted_iota(jnp.int32, (N_DEV, D), 0) == _linear(me)
        gb_wide = jnp.concatenate([gb_ref[...], jnp.zeros((N_DEV, D - cols), F32)], axis=1)
        myv[GB_BASE:vec_rows, :] = jnp.where(slot, gb_wide, 0.0)
        myg[...] = gate_ref[...]

        swaps = [pltpu.make_async_remote_copy(
            src_ref=src, dst_ref=dst, send_sem=sib_send.at[a], recv_sem=sib_recv.at[a], device_id=(x, y, 1 - c),
            device_id_type=MESH) for a, (src, dst) in enumerate(((myv, sibv), (myg, sibg)))]
        for cp in swaps:
            cp.start()
        for cp in swaps:
            cp.wait_recv()
        chipv[my_chip] = myv[...] + sibv[...]
        chipg[my_chip] = myg[...] + sibg[...]

        def chip_copy(a, buf, j, k, slot_chip):
            peer = _flip(me, k)
            return pltpu.make_async_remote_copy(
                src_ref=buf.at[slot_chip], dst_ref=buf.at[slot_chip], send_sem=peer_send.at[3 * a + j],
                recv_sem=peer_recv.at[3 * a + j], device_id=peer, device_id_type=MESH)

        sends = [chip_copy(a, buf, j, k, my_chip) for a, buf in enumerate((chipv, chipg)) for j, k in enumerate(CHIP_FLIPS)]
        for cp in sends:
            cp.start()
        for a, buf in enumerate((chipv, chipg)):
            for j, k in enumerate(CHIP_FLIPS):
                peer = _flip(me, k)
                chip_copy(a, buf, j, k, 2 * peer[0] + peer[1]).wait_recv()
        sumv_ref[...] = ((chipv[0] + chipv[1]) + chipv[2]) + chipv[3]
        sumg_ref[...] = ((chipg[0] + chipg[1]) + chipg[2]) + chipg[3]
        for cp in swaps + sends:
            cp.wait_send()
        for cp in chip_copies:
            cp.wait_recv()
        for cp in chip_copies:
            cp.wait_send()

    n_chip = len(CHIP_FLIPS) * n
    vshape, gshape = (vec_rows, D), msg_gate.shape
    return pl.pallas_call(
        body,
        name="chips_and_gather",
        in_specs=[WHOLE] * 3 + [ANY] * n,
        out_specs=[WHOLE] * 2 + [ANY] * n,
        out_shape=[jax.ShapeDtypeStruct(vshape, F32), jax.ShapeDtypeStruct(gshape, F32)]
        + [jax.ShapeDtypeStruct((len(CHIP_FLIPS),) + s.shape[1:], s.dtype) for s in chip_sums],
        scratch_shapes=[pltpu.VMEM(vshape, F32), pltpu.VMEM(gshape, F32), pltpu.VMEM(vshape, F32),
                        pltpu.VMEM(gshape, F32), pltpu.VMEM((4,) + vshape, F32), pltpu.VMEM((4,) + gshape, F32),
                        pltpu.SemaphoreType.DMA((2,)), pltpu.SemaphoreType.DMA((2,)),
                        pltpu.SemaphoreType.DMA((2 * len(CHIP_FLIPS),)), pltpu.SemaphoreType.DMA((2 * len(CHIP_FLIPS),)),
                        pltpu.SemaphoreType.DMA((n_chip,)), pltpu.SemaphoreType.DMA((n_chip,))],
        compiler_params=pltpu.CompilerParams(vmem_limit_bytes=VMEM_LIMIT),
    )(msg_vec, msg_gate, gb_rows, *chip_sums)


def _adamw_math(w, g, m, v):
    m = ADAM_B1 * m + (1.0 - ADAM_B1) * g
    v = ADAM_B2 * v + (1.0 - ADAM_B2) * (g * g)
    m_hat = m / (1.0 - ADAM_B1 ** ADAM_STEP)
    v_hat = v / (1.0 - ADAM_B2 ** ADAM_STEP)
    delta = -ADAM_LR * (m_hat / (jnp.sqrt(v_hat) + ADAM_EPS) + ADAM_WD * w)
    return delta, m, v


def _adamw(name, w, g, m, v):
    rows, cols = w.shape
    rb = 256 if rows % 256 == 0 else rows

    def body(w_ref, g_ref, m_ref, v_ref, d_ref, mo_ref, vo_ref):
        d_ref[...], mo_ref[...], vo_ref[...] = _adamw_math(w_ref[...], g_ref[...], m_ref[...], v_ref[...])

    spec = pl.BlockSpec((rb, cols), lambda r: (r, 0))
    return pl.pallas_call(
        body,
        name="adamw_" + name,
        grid=(rows // rb,),
        in_specs=[spec] * 4,
        out_specs=[spec] * 3,
        out_shape=[jax.ShapeDtypeStruct((rows, cols), F32)] * 3,
        compiler_params=pltpu.CompilerParams(dimension_semantics=("arbitrary",)),
    )(w, g, m, v)


def _adamw_small(ws, gs, ms, vs, sigmoid_scaled):
    n = len(ws)

    def body(*refs):
        w_refs, g_refs, m_refs, v_refs = (refs[i * n:(i + 1) * n] for i in range(4))
        outs = refs[4 * n:]
        for i in range(n):
            w = w_refs[i][...]
            g = g_refs[i][...]
            if sigmoid_scaled[i]:
                g = g * _sigmoid(w)
            delta, m, v = _adamw_math(w, g, m_refs[i][...], v_refs[i][...])
            outs[4 * i][...] = g
            outs[4 * i + 1][...] = delta
            outs[4 * i + 2][...] = m
            outs[4 * i + 3][...] = v

    shapes = [jax.ShapeDtypeStruct(w.shape, F32) for w in ws for _ in range(4)]
    outs = pl.pallas_call(
        body,
        name="adamw_small",
        in_specs=[WHOLE] * (4 * n),
        out_specs=[WHOLE] * (4 * n),
        out_shape=shapes,
    )(*ws, *gs, *ms, *vs)
    return [outs[4 * i:4 * i + 4] for i in range(n)]


_WEIGHT_NAMES = ("ada_w", "ada_b", "norm1_g", "w_in", "lru_conv_w", "lru_conv_b", "gate_a_w", "gate_a_b", "gate_x_w",
                 "gate_x_b", "a_param", "short_conv_w", "lru_out_g", "conv_out_g", "w_out", "norm2_g", "w_mlp1",
                 "w_mlp2", "final_g")
_BIG = ("ada_w", "w_in", "w_out", "w_mlp1", "w_mlp2")


def kernel(x, c, ada_w, ada_b, norm1_g, w_in, lru_conv_w, lru_conv_b, gate_a_w, gate_a_b, gate_x_w, gate_x_b, a_param, short_conv_w, lru_out_g, conv_out_g, w_out, norm2_g, w_mlp1, w_mlp2, final_g, loss_target, m_ada_w, m_ada_b, m_norm1_g, m_w_in, m_lru_conv_w, m_lru_conv_b, m_gate_a_w, m_gate_a_b, m_gate_x_w, m_gate_x_b, m_a_param, m_short_conv_w, m_lru_out_g, m_conv_out_g, m_w_out, m_norm2_g, m_w_mlp1, m_w_mlp2, m_final_g, v_ada_w, v_ada_b, v_norm1_g, v_w_in, v_lru_conv_w, v_lru_conv_b, v_gate_a_w, v_gate_a_b, v_gate_x_w, v_gate_x_b, v_a_param, v_short_conv_w, v_lru_out_g, v_conv_out_g, v_w_out, v_norm2_g, v_w_mlp1, v_w_mlp2, v_final_g):
    given = dict(locals())
    weights = {n: given[n] for n in _WEIGHT_NAMES}
    xi, yi, ci = _position()
    me_lin = _linear((xi, yi, ci))
    hd = W // N_DEV

    mixer_block = jnp.concatenate([w_out[0], w_in[0].T], axis=0).astype(BF16)
    mlp_block = jnp.concatenate([w_mlp1[0].T, w_mlp2[0]], axis=0).astype(BF16)

    msg = (jnp.pad(c, ((0, HALO - 1), (0, 0)))
           + jnp.pad(lru_conv_w[0], ((1, HALO - 1 - CONV_L), (0, D - hd)))
           + jnp.pad(short_conv_w[0], ((1 + CONV_L, 0), (0, D - hd))))
    gath, mod_all, wmix = _gather_and_mod(msg, ada_w[0], mixer_block)
    sc_all = gath[:, 0, :]
    wl = jnp.transpose(gath[:, 1:1 + CONV_L, :hd], (1, 0, 2)).reshape(CONV_L, W)
    ws = jnp.transpose(gath[:, 1 + CONV_L:HALO, :hd], (1, 0, 2)).reshape(CONV_S, W)
    modraw = _pad_rows(mod_all[:, 0, :].reshape(6, D), HALO)
    adab = _pad_rows(ada_b.reshape(6, D), HALO)

    x2d, tgt = x[0], loss_target[0]
    gf = final_g.reshape(1, D)
    bda = _block_diag(gate_a_w[0]).astype(BF16)
    bdx = _block_diag(gate_x_w[0]).astype(BF16)
    avg = _block_diag(jnp.full((8, W // 8, W // 8), 8.0 / W, F32)).astype(BF16)
    wl8 = _pad_rows(wl, HALO)
    ws8 = _pad_rows(ws, HALO)
    mixer_small = (wl8, lru_conv_b, bda, bdx, gate_a_b, gate_x_b, a_param, ws8, lru_out_g, conv_out_g, avg)
    proj, hl, mixed, wmlp = _mixer_fwd(x2d, modraw, adab, norm1_g, *mixer_small, wmix, mlp_block)
    wmlp = _sibling_forward(wmlp)
    h2t, f, dx2, dz, dzt, vec2, loss8 = _mlp_fwd(x2d, mixed, tgt, modraw, adab, norm2_g, gf, wmlp)
    pos = jnp.stack([ci, 2 * xi + yi]).astype(jnp.int32)
    by_dest = lambda g: g.reshape((4, 2, -1) + g.shape[-1:])
    dh2, dw1, dw2t, sib1, sib2 = _mlp_bwd_half(pos, h2t, f, dz, dzt, wmlp,
                                               prior=_mlp_bwd_half(pos, h2t, f, dz, dzt, wmlp))
    mlp_parts = [dw1[:, None], dw2t[:, None]]
    mlp_sib = [sib1, sib2]
    mlp_sums = [_pair_sum(pos, p, r) for p, r in zip(mlp_parts, mlp_sib)]
    gx, vec, hb, dproj_t, dmixed, ycat_t, xl_t, dgate, *mlp_chips = _mixer_bwd(
        x2d, mixed, dh2, dx2, proj, hl, modraw, adab, norm1_g, norm2_g, *mixer_small, wmix, mlp_sums)
    dwint = _matmul("wgrad_in", dproj_t, hb)
    dwout = _matmul("wgrad_out", ycat_t, dmixed)
    dgates = _matmul("wgrad_gate", xl_t, dgate)
    mix_parts = [by_dest(dwout), by_dest(dwint)]
    gmod8 = (jnp.pad(vec[0:5], ((0, 1), (0, 0))) + jnp.pad(vec2[0:1], ((5, 0), (0, 0)))).reshape(N_DEV, 6 * D // N_DEV)
    sc_t = jnp.pad(sc_all.T, ((0, 0), (0, 128 - N_DEV))).astype(BF16)
    g_adaw, gb_rows, *mix_sib = _sibling_and_route(gmod8, sc_t, mix_parts)
    mix_sums = [_pair_sum(pos, p, r) for p, r in zip(mix_parts, mix_sib)]
    loss_rows = jnp.pad(loss8[0:1], ((0, HALO - 1), (0, D - loss8.shape[1])))
    msg_vec = jnp.concatenate([vec, vec2, loss_rows], axis=0)
    msg_gate = jnp.stack([_diag_blocks(dgates[:, :W]), _diag_blocks(dgates[:, W:])]).reshape(W, 128)
    sum_vec, sum_gate, *mix_chips = _chips_and_gather(msg_vec, msg_gate, gb_rows, mix_sums)
    g_w1, g_w2t, g_wout, g_wint = (_final_sum(pos, p, r, q) for p, r, q in zip(
        mlp_parts + mix_parts, list(mlp_sib) + list(mix_sib), list(mlp_chips) + list(mix_chips)))
    loss = sum_vec[LOSS_ROW, 0]
    sum_gate = sum_gate.reshape(2, W, W // 8)
    lo, hi = slice(0, W), slice(W, 2 * W)
    wl_full = sum_vec[V_WL01:V_WL23 + 1].reshape(CONV_L, W)
    ws_full = sum_vec[V_WS01:V_WS2 + 1].reshape(CONV_S + 1, W)[:CONV_S]
    row = lambda r, cols: sum_vec[r:r + 1, cols]
    small_grads = {
        "ada_b": sum_vec[GB_BASE:GB_BASE + N_DEV, :6 * D // N_DEV].reshape(1, 6 * D),
        "norm1_g": row(V_G1, slice(0, D)),
        "lru_conv_w": lax.dynamic_slice(wl_full, (0, me_lin * hd), (CONV_L, hd)),
        "lru_conv_b": row(V_BL_BA, lo),
        "gate_a_w": sum_gate[0],
        "gate_a_b": row(V_BL_BA, hi),
        "gate_x_w": sum_gate[1],
        "gate_x_b": row(V_BX_SP, lo),
        "a_param": row(V_BX_SP, hi),
        "short_conv_w": lax.dynamic_slice(ws_full, (0, me_lin * hd), (CONV_S, hd)),
        "lru_out_g": row(V_GL_GC, lo),
        "conv_out_g": row(V_GL_GC, hi),
        "norm2_g": row(V_G2, slice(0, D)),
        "final_g": sum_vec[V_ROWS + 1:V_ROWS + 2, :],
    }
    names = list(small_grads)
    as2d = lambda a, n: a.reshape(small_grads[n].shape)
    small = _adamw_small([as2d(weights[n], n) for n in names], [small_grads[n] for n in names],
                         [as2d(given["m_" + n], n) for n in names], [as2d(given["v_" + n], n) for n in names],
                         [n == "a_param" for n in names])
    result = {n: tuple(o.reshape(weights[n].shape) for o in outs) for n, outs in zip(names, small)}

    big_grads = {"ada_w": g_adaw, "w_in": g_wint.T, "w_out": g_wout, "w_mlp1": g_w1, "w_mlp2": g_w2t.T}
    for n in _BIG:
        g = big_grads[n]
        delta, new_m, new_v = _adamw(n, weights[n][0], g, given["m_" + n][0], given["v_" + n][0])
        result[n] = tuple(o[None] for o in (g, delta, new_m, new_v))

    return (loss, gx[None], *[result[n][0] for n in _WEIGHT_NAMES], *[result[n][1] for n in _WEIGHT_NAMES],
            *[result[n][2] for n in _WEIGHT_NAMES], *[result[n][3] for n in _WEIGHT_NAMES])
```

```python
import functools

import jax
import jax.numpy as jnp
from jax import lax
from jax.experimental import pallas as pl
from jax.experimental.pallas import tpu as pltpu

F32 = jnp.float32
BF16 = jnp.bfloat16
MESH = pl.DeviceIdType.MESH

N_DEV = 8
D = 1024
W = 512
D_IN = 5 * W
D_FF = 4096
FF_BLK = D_FF // N_DEV
EPS = 1e-6
C_GATE = 8.0
CONV_L = 4
CONV_S = 3
HALO = 8

ROWS_W1T, ROWS_W2, ROWS_WOUT, ROWS_WIN = FF_BLK, FF_BLK, D // N_DEV, D_IN // N_DEV
OFF_WOUT = 0
OFF_WIN = OFF_WOUT + ROWS_WOUT
MIX_ROWS = OFF_WIN + ROWS_WIN
OFF_W1T = 0
OFF_W2 = OFF_W1T + ROWS_W1T
MLP_ROWS = OFF_W2 + ROWS_W2
CHIP_FLIPS = (4, 2, 6)

ADAM_LR = 0.001
ADAM_B1 = 0.9
ADAM_B2 = 0.999
ADAM_EPS = 1e-08
ADAM_WD = 0.01
ADAM_STEP = 10

VMEM_LIMIT = 56 * 1024 * 1024

TB_MIX = 256
TB_MIXB = 256
TB_MLP = 256
TB_MLPB = 512

ANY = pl.BlockSpec(memory_space=pl.ANY)
WHOLE = pl.BlockSpec(memory_space=pltpu.VMEM)


def _dot(a, b):
    return jnp.dot(a, b, preferred_element_type=F32)


def _dot_nt(a, b):
    return lax.dot_general(a, b, (((1,), (1,)), ((), ())), preferred_element_type=F32)


def _dot_tn(a, b):
    return lax.dot_general(a, b, (((0,), (0,)), ((), ())), preferred_element_type=F32)


def _sigmoid(v):
    return 1.0 / (1.0 + jnp.exp(-v))


def _softplus(v):
    t = jnp.exp(-jnp.abs(v))
    small = t * (1.0 - t * (0.5 - t * (1.0 / 3.0)))
    return jnp.maximum(v, 0.0) + jnp.where(t < 1e-2, small, jnp.log(1.0 + t))


def _one_minus_sq(a, log_a):
    return -jnp.tanh(log_a) * (a * a + 1.0)


_GELU_K = 0.7978845608028654
_GELU_C = 0.044715


def _gelu(u):
    th = jnp.tanh(_GELU_K * (u + _GELU_C * u * u * u))
    return 0.5 * u * (1.0 + th), th


def _gelu_grad(u, th):
    return 0.5 * (1.0 + th) + 0.5 * u * (1.0 - th * th) * _GELU_K * (1.0 + 3.0 * _GELU_C * u * u)


def _group_mean(v, avg):
    hi = v.astype(BF16)
    lo = (v - hi.astype(F32)).astype(BF16)
    return _dot(hi, avg) + _dot(lo, avg)


def _colsum(v):
    return jnp.sum(v, axis=0, keepdims=True)


def _rowmean(v):
    return jnp.mean(v, axis=-1, keepdims=True)


def _load_packed(wpack_hbm, off, rows, dst, sem):
    copies = [
        pltpu.make_async_copy(wpack_hbm.at[d, pl.ds(off, rows), :], dst.at[pl.ds(d * rows, rows), :], sem)
        for d in range(N_DEV)
    ]
    for cp in copies:
        cp.start()
    return copies


def _scan_groups(n_groups, a_ref, b_ref, out_ref, carry_ref, reverse):
    row = lax.broadcasted_iota(jnp.int32, (HALO, W), 0)

    def step(k, carry):
        g = (n_groups - 1 - k) if reverse else k
        rows = pl.ds(pl.multiple_of(g * HALO, HALO), HALO)
        a = a_ref[rows, :]
        b = b_ref[rows, :]
        for s in (1, 2, 4):
            if reverse:
                keep = row < HALO - s
                sh = HALO - s
            else:
                keep = row >= s
                sh = s
            a_sh = pltpu.roll(a, sh, axis=0)
            b_sh = pltpu.roll(b, sh, axis=0)
            b = jnp.where(keep, a * b_sh + b, b)
            a = jnp.where(keep, a * a_sh, a)
        h = b + a * carry
        out_ref[rows, :] = h
        edge = h[0:1, :] if reverse else h[HALO - 1:HALO, :]
        return jnp.broadcast_to(edge, (HALO, W))

    carry_ref[...] = lax.fori_loop(0, n_groups, step, carry_ref[...])


def _route_peers(me):
    x, y, c = me
    first = ((x + 1 - c) % 2, (y + c) % 2, c)
    second = ((x + c) % 2, (y + 1 - c) % 2, c)
    return first, second, (1 - x, 1 - y, c)


def _chip_gather_copies(block_hbm, out_hbm, send_sems, recv_sems):
    me = _position()
    first, second, diag = _route_peers(me)

    def copy(j, src, slot_of, to):
        return pltpu.make_async_remote_copy(
            src_ref=src, dst_ref=out_hbm.at[_linear(slot_of)], send_sem=send_sems.at[j], recv_sem=recv_sems.at[j],
            device_id=to, device_id_type=MESH)

    own_sends = [copy(0, block_hbm, me, first), copy(1, block_hbm, me, second)]
    forward = copy(2, out_hbm.at[_linear(first)], first, second)
    arrivals = [copy(0, block_hbm, first, first), copy(1, block_hbm, second, second), copy(2, block_hbm, diag, second)]
    return own_sends, forward, arrivals


def _mixer_fwd(x, modraw, adab, g1, wl, bl, bda, bdx, ba, bxb, ap, ws, gl, gc, avg, wpack, mlp_block):
    t_len = x.shape[0]
    tb = TB_MIX
    nb = t_len // tb

    def body(x_ref, modraw_ref, adab_ref, g1_ref, wl_ref, bl_ref, bda_ref, bdx_ref, ba_ref, bxb_ref, ap_ref,
             ws_ref, gl_ref, gc_ref, avg_ref, wpack_hbm, block_hbm, proj_ref, hl_ref, mixed_ref, wmlp_hbm,
             win_v, wout_v, sem, ulx_ext, cv_ext, hcar, a_s, b_s, send_sems, recv_sems, local_sem):
        i = pl.program_id(0)
        own = pltpu.make_async_copy(block_hbm, wmlp_hbm.at[_linear(_position())], local_sem)
        sends, forward, arrivals = _chip_gather_copies(block_hbm, wmlp_hbm, send_sems, recv_sems)

        @pl.when(i == 0)
        def _():
            own.start()
            for cp in sends:
                cp.start()

        @pl.when(i == nb - 1)
        def _():
            arrivals[0].wait_recv()
            forward.start()

        @pl.when(i == 0)
        def _():
            cps = _load_packed(wpack_hbm, OFF_WIN, ROWS_WIN, win_v, sem.at[0])
            cps += _load_packed(wpack_hbm, OFF_WOUT, ROWS_WOUT, wout_v, sem.at[1])
            ulx_ext[0:HALO, :] = jnp.zeros((HALO, W), F32)
            cv_ext[0:HALO, :] = jnp.zeros((HALO, W), F32)
            hcar[...] = jnp.zeros((HALO, W), F32)
            for cp in cps:
                cp.wait()

        mod = modraw_ref[...] + adab_ref[...]
        shift1, scale1, gate1 = mod[0:1], mod[1:2], mod[2:3]
        x = x_ref[...]
        r1 = lax.rsqrt(_rowmean(x * x) + EPS)
        h = (x * r1 * g1_ref[...]) * (1.0 + scale1) + shift1
        proj = _dot_nt(h.astype(BF16), win_v[...])
        proj_ref[...] = proj
        u_lx, u_ly, u_b, u_c, u_v = (proj[:, k * W:(k + 1) * W] for k in range(5))

        ulx_ext[HALO:HALO + tb, :] = u_lx
        xl = bl_ref[...] + wl_ref[CONV_L - 1:CONV_L, :] * u_lx
        for k in range(CONV_L - 1):
            xl = xl + wl_ref[k:k + 1, :] * ulx_ext[pl.ds(HALO - (CONV_L - 1) + k, tb), :]
        ulx_ext[0:HALO, :] = ulx_ext[tb:tb + HALO, :]
        xlb = xl.astype(BF16)
        r = _sigmoid(_dot(xlb, bda_ref[...]) + ba_ref[...])
        ig = _sigmoid(_dot(xlb, bdx_ref[...]) + bxb_ref[...])
        log_a = (-C_GATE) * r * _softplus(ap_ref[...])
        a = jnp.exp(log_a)
        mult = jnp.sqrt(_one_minus_sq(a, log_a))
        grow = i * tb + lax.broadcasted_iota(jnp.int32, (tb, W), 0)
        mult = jnp.where(grow == 0, 1.0, mult)
        a_s[...] = a
        b_s[...] = mult * (ig * xl)
        _scan_groups(tb // HALO, a_s, b_s, hl_ref, hcar, reverse=False)
        hl = hl_ref[...]
        ge, _ = _gelu(u_ly)
        p = ge * hl
        y_lru = p * lax.rsqrt(_group_mean(p * p, avg_ref[...]) + EPS) * gl_ref[...]

        cv = u_c * u_v
        cv_ext[HALO:HALO + tb, :] = cv
        cc = ws_ref[CONV_S - 1:CONV_S, :] * cv
        for k in range(CONV_S - 1):
            cc = cc + ws_ref[k:k + 1, :] * cv_ext[pl.ds(HALO - (CONV_S - 1) + k, tb), :]
        cv_ext[0:HALO, :] = cv_ext[tb:tb + HALO, :]
        q = u_b * cc
        y_conv = q * lax.rsqrt(_group_mean(q * q, avg_ref[...]) + EPS) * gc_ref[...]

        mixed_ref[...] = (_dot(y_lru.astype(BF16), wout_v[0:W, :]) + _dot(y_conv.astype(BF16), wout_v[W:2 * W, :]))

        @pl.when(i == nb - 1)
        def _():
            for cp in arrivals[1:]:
                cp.wait_recv()
            for cp in sends + [forward]:
                cp.wait_send()
            own.wait()

    tok = lambda cols: pl.BlockSpec((tb, cols), lambda i: (i, 0))
    full = lambda a: pl.BlockSpec(a.shape, lambda i: (0,) * a.ndim)
    small = (modraw, adab, g1, wl, bl, bda, bdx, ba, bxb, ap, ws, gl, gc, avg)
    n_chips = len(CHIP_FLIPS)
    return pl.pallas_call(
        body,
        name="mixer_fwd",
        grid=(nb,),
        in_specs=[tok(D)] + [full(a) for a in small] + [ANY, ANY],
        out_specs=[tok(D_IN), tok(W), tok(D), ANY],
        out_shape=[jax.ShapeDtypeStruct((t_len, D_IN), F32), jax.ShapeDtypeStruct((t_len, W), F32),
                   jax.ShapeDtypeStruct((t_len, D), F32), jax.ShapeDtypeStruct((N_DEV,) + mlp_block.shape, BF16)],
        scratch_shapes=[pltpu.VMEM((D_IN, D), BF16), pltpu.VMEM((D, D), BF16), pltpu.SemaphoreType.DMA((2,)),
                        pltpu.VMEM((tb + HALO, W), F32), pltpu.VMEM((tb + HALO, W), F32), pltpu.VMEM((HALO, W), F32),
                        pltpu.VMEM((tb, W), F32), pltpu.VMEM((tb, W), F32),
                        pltpu.SemaphoreType.DMA((n_chips,)), pltpu.SemaphoreType.DMA((n_chips,)), pltpu.SemaphoreType.DMA],
        compiler_params=pltpu.CompilerParams(dimension_semantics=("arbitrary",), vmem_limit_bytes=VMEM_LIMIT),
    )(x, *small, wpack, mlp_block)


def _sibling_forward(wmlp):
    def body(in_hbm, out_hbm, send_sems, recv_sems):
        x, y, c = _position()
        copies, arrivals = [], []
        for j, k in enumerate((0,) + CHIP_FLIPS):
            mine = out_hbm.at[_linear(_flip((x, y, c), k))]
            theirs = out_hbm.at[_linear(_flip((x, y, 1 - c), k))]
            copies.append(pltpu.make_async_remote_copy(
                src_ref=mine, dst_ref=mine, send_sem=send_sems.at[j], recv_sem=recv_sems.at[j],
                device_id=(x, y, 1 - c), device_id_type=MESH))
            arrivals.append(pltpu.make_async_remote_copy(
                src_ref=theirs, dst_ref=theirs, send_sem=send_sems.at[j], recv_sem=recv_sems.at[j],
                device_id=(x, y, 1 - c), device_id_type=MESH))
        for cp in copies:
            cp.start()
        for cp in arrivals:
            cp.wait_recv()
        for cp in copies:
            cp.wait_send()

    return pl.pallas_call(
        body,
        name="sibling_forward",
        in_specs=[ANY],
        out_specs=ANY,
        out_shape=jax.ShapeDtypeStruct(wmlp.shape, wmlp.dtype),
        input_output_aliases={0: 0},
        scratch_shapes=[pltpu.SemaphoreType.DMA((4,)), pltpu.SemaphoreType.DMA((4,))],
    )(wmlp)


def _mlp_fwd(x, mixed, tgt, modraw, adab, g2, gf, wpack):
    t_len = x.shape[0]
    tb = TB_MLP
    nb = t_len // tb

    def body(x_ref, mixed_ref, tgt_ref, modraw_ref, adab_ref, g2_ref, gf_ref, wpack_hbm,
             h2t_ref, f_ref, dx2_ref, dz_ref, vec_ref, loss_ref, w1t_v, w2_v, sem):
        i = pl.program_id(0)

        @pl.when(i == 0)
        def _():
            cps = _load_packed(wpack_hbm, OFF_W1T, ROWS_W1T, w1t_v, sem.at[0])
            cps += _load_packed(wpack_hbm, OFF_W2, ROWS_W2, w2_v, sem.at[1])
            vec_ref[...] = jnp.zeros(vec_ref.shape, F32)
            loss_ref[...] = jnp.zeros(loss_ref.shape, F32)
            for cp in cps:
                cp.wait()

        mod = modraw_ref[...] + adab_ref[...]
        gate1, shift2, scale2, gate2 = mod[2:3], mod[3:4], mod[4:5], mod[5:6]
        x1 = x_ref[...] + gate1 * mixed_ref[...]
        r2 = lax.rsqrt(_rowmean(x1 * x1) + EPS)
        h2 = (x1 * r2 * g2_ref[...]) * (1.0 + scale2) + shift2
        h2b = h2.astype(BF16)
        h2t_ref[...] = h2.T.astype(BF16)
        z = jnp.zeros((tb, D), F32)
        for j in range(N_DEV):
            cols = slice(j * FF_BLK, (j + 1) * FF_BLK)
            fj = _dot_nt(h2b, w1t_v[cols, :])
            f_ref[:, cols] = fj
            rf = jnp.maximum(fj, 0.0)
            z = z + _dot((rf * rf).astype(BF16), w2_v[cols, :])
        x2 = x1 + gate2 * z
        r3 = lax.rsqrt(_rowmean(x2 * x2) + EPS)
        xn3 = x2 * r3
        diff = xn3 * gf_ref[...] - tgt_ref[...]
        sq = _colsum(diff * diff)
        loss_ref[...] += jnp.broadcast_to(jnp.sum(sq, axis=1, keepdims=True) * (0.5 / D), loss_ref.shape)
        dy = diff * (1.0 / D)
        dyn = dy * gf_ref[...]
        dx2 = r3 * (dyn - xn3 * _rowmean(dyn * xn3))
        dx2_ref[...] = dx2
        dz_ref[...] = (gate2 * dx2).astype(BF16)
        vec_ref[0:1, :] += _colsum(dx2 * z)
        vec_ref[1:2, :] += _colsum(dy * xn3)

    tok = lambda cols: pl.BlockSpec((tb, cols), lambda i: (i, 0))
    tok_t = pl.BlockSpec((D, tb), lambda i: (0, i))
    full = lambda a: pl.BlockSpec(a.shape, lambda i: (0,) * a.ndim)
    small = (modraw, adab, g2, gf)
    return pl.pallas_call(
        body,
        name="mlp_fwd",
        grid=(nb,),
        in_specs=[tok(D), tok(D), tok(D)] + [full(a) for a in small] + [ANY],
        out_specs=[tok_t, tok(D_FF), tok(D), tok(D), pl.BlockSpec((8, D), lambda i: (0, 0)),
                   pl.BlockSpec((8, 128), lambda i: (0, 0))],
        out_shape=[jax.ShapeDtypeStruct((D, t_len), BF16), jax.ShapeDtypeStruct((t_len, D_FF), F32),
                   jax.ShapeDtypeStruct((t_len, D), F32), jax.ShapeDtypeStruct((t_len, D), BF16),
                   jax.ShapeDtypeStruct((8, D), F32), jax.ShapeDtypeStruct((8, 128), F32)],
        scratch_shapes=[pltpu.VMEM((D_FF, D), BF16), pltpu.VMEM((D_FF, D), BF16), pltpu.SemaphoreType.DMA((2,))],
        compiler_params=pltpu.CompilerParams(dimension_semantics=("arbitrary",), vmem_limit_bytes=VMEM_LIMIT),
    )(x, mixed, tgt, *small, wpack)


def _mlp_bwd_half(pos, h2t, f, dz, wpack, prior=None):
    t_len = dz.shape[0]
    tb = TB_MLPB
    nb = t_len // tb
    first = prior is None
    flip = 1 if first else 0

    def body(pos_ref, h2t_ref, f_ref, dz_ref, w1t_ref, w2_ref, *rest):
        if first:
            dh2_ref, dw1_ref, dw2_ref = rest
        else:
            dh2in_ref, send1_hbm, send2_hbm, dh2_ref, dw1_ref, dw2_ref, land1_hbm, land2_hbm, send_sems, recv_sems = rest
            x, y, c = _position()
            copies = [pltpu.make_async_remote_copy(
                src_ref=src, dst_ref=dst, send_sem=send_sems.at[a], recv_sem=recv_sems.at[a], device_id=(x, y, 1 - c),
                device_id_type=MESH) for a, (src, dst) in enumerate(((send1_hbm, land1_hbm), (send2_hbm, land2_hbm)))]
        k = pl.program_id(0)
        t = pl.program_id(1)

        if not first:
            @pl.when((k == 0) & (t == 0))
            def _():
                for cp in copies:
                    cp.start()

        rows = pl.ds(pl.multiple_of(t * tb, tb), tb)
        w1t = w1t_ref[0]
        w2 = w2_ref[0]
        rf = jnp.maximum(f_ref[...], 0.0)
        a2 = (rf * rf).astype(BF16)
        dz = dz_ref[...]
        df = (_dot_nt(dz, w2) * (2.0 * rf)).astype(BF16)
        g2 = _dot_tn(a2, dz)
        g1 = _dot(h2t_ref[...], df)
        dh = _dot(df, w1t)

        @pl.when(t == 0)
        def _():
            dw2_ref[0] = g2
            dw1_ref[0] = g1

        @pl.when(t != 0)
        def _():
            dw2_ref[0] += g2
            dw1_ref[0] += g1

        @pl.when(k == 0)
        def _():
            dh2_ref[rows, :] = dh if first else dh2in_ref[...] + dh

        @pl.when(k != 0)
        def _():
            dh2_ref[rows, :] += dh

        if not first:
            @pl.when((k == 3) & (t == nb - 1))
            def _():
                for cp in copies:
                    cp.wait_recv()
                for cp in copies:
                    cp.wait_send()

    blk = lambda k, pos: 2 * k + jnp.bitwise_xor(pos[0], flip)
    in_specs = [pl.BlockSpec((D, tb), lambda k, t, pos: (0, t)),
                pl.BlockSpec((tb, FF_BLK), lambda k, t, pos: (t, blk(k, pos))),
                pl.BlockSpec((tb, D), lambda k, t, pos: (t, 0)),
                pl.BlockSpec((1, ROWS_W1T, D), lambda k, t, pos: (blk(k, pos), OFF_W1T // ROWS_W1T, 0)),
                pl.BlockSpec((1, ROWS_W2, D), lambda k, t, pos: (blk(k, pos), OFF_W2 // ROWS_W2, 0))]
    grad_specs = [pl.BlockSpec((1, D, FF_BLK), lambda k, t, pos: (k, 0, 0)),
                  pl.BlockSpec((1, FF_BLK, D), lambda k, t, pos: (k, 0, 0))]
    out_specs = [pl.BlockSpec((t_len, D), lambda k, t, pos: (0, 0))] + grad_specs
    grad_shapes = [jax.ShapeDtypeStruct((4, D, FF_BLK), F32), jax.ShapeDtypeStruct((4, FF_BLK, D), F32)]
    out_shape = [jax.ShapeDtypeStruct((t_len, D), F32)] + grad_shapes
    args = [pos, h2t, f, dz, wpack, wpack]
    scratch = []
    if not first:
        in_specs += [pl.BlockSpec((tb, D), lambda k, t, pos: (jnp.where(k == 0, t, nb - 1), 0)), ANY, ANY]
        out_specs += [ANY, ANY]
        out_shape += grad_shapes
        args += list(prior)
        scratch = [pltpu.SemaphoreType.DMA((2,)), pltpu.SemaphoreType.DMA((2,))]
    return pl.pallas_call(
        body,
        name="mlp_bwd_first" if first else "mlp_bwd_second",
        grid_spec=pltpu.PrefetchScalarGridSpec(num_scalar_prefetch=1, grid=(4, nb), in_specs=in_specs,
                                               out_specs=out_specs, scratch_shapes=scratch),
        out_shape=out_shape,
        compiler_params=pltpu.CompilerParams(dimension_semantics=("arbitrary", "arbitrary"),
                                             vmem_limit_bytes=VMEM_LIMIT),
    )(*args)


V_SHIFT1, V_SCALE1, V_GATE1, V_SHIFT2, V_SCALE2, V_G1, V_G2 = 0, 1, 2, 3, 4, 6, 7
V_BL_BA, V_BX_SP, V_GL_GC, V_WL01, V_WL23, V_WS01, V_WS2 = 8, 9, 10, 11, 12, 13, 14
V_ROWS = 16


def _chip_scatter_copies(srcs, dsts, send_sems, recv_sems):
    me = _position()
    copies = []
    for a, (src, dst) in enumerate(zip(srcs, dsts)):
        for j, k in enumerate(CHIP_FLIPS):
            peer = _flip(me, k)
            copies.append(pltpu.make_async_remote_copy(
                src_ref=src.at[2 * peer[0] + peer[1]], dst_ref=dst.at[j], send_sem=send_sems.at[len(CHIP_FLIPS) * a + j],
                recv_sem=recv_sems.at[len(CHIP_FLIPS) * a + j], device_id=peer, device_id_type=MESH))
    return copies


def _mixer_bwd(x, mixed, dh2, dx2, proj, hl, modraw, adab, g1, g2, wl, bl, bda, bdx, ba, bxb, ap, ws, gl, gc, avg, wpack,
               chip_sums):
    t_len = x.shape[0]
    tb = TB_MIXB
    nb = t_len // tb
    hb = tb // HALO
    n_sums = len(chip_sums)

    def body(x_ref, mixed_ref, dh2_ref, dx2_ref, proj_ref, projh_ref, hl_ref, hlh_ref,
             modraw_ref, adab_ref, g1_ref, g2_ref, wl_ref, bl_ref, bda_ref, bdx_ref, ba_ref, bxb_ref, ap_ref,
             ws_ref, gl_ref, gc_ref, avg_ref, wpack_hbm, *rest):
        sums_hbm, rest = rest[:n_sums], rest[n_sums:]
        gx_ref, vec_ref, hb_ref, dprojt_ref, dmixed_ref, ycatt_ref, xlt_ref, dgate_ref = rest[:8]
        landed_hbm, rest = rest[8:8 + n_sums], rest[8 + n_sums:]
        (win_v, wout_v, sem, ulx_ext, cv_ext, hl_ext, a_ext, dxl_ext, dcc_ext, dcar, an_s, g_s, dh_s,
         send_sems, recv_sems) = rest
        i = pl.program_id(0)
        blk = nb - 1 - i
        chip_copies = _chip_scatter_copies(sums_hbm, landed_hbm, send_sems, recv_sems)

        @pl.when(i == 0)
        def _():
            for cp in chip_copies:
                cp.start()
            cps = _load_packed(wpack_hbm, OFF_WIN, ROWS_WIN, win_v, sem.at[0])
            cps += _load_packed(wpack_hbm, OFF_WOUT, ROWS_WOUT, wout_v, sem.at[1])
            vec_ref[...] = jnp.zeros(vec_ref.shape, F32)
            zero = jnp.zeros((HALO, W), F32)
            a_ext[tb:tb + HALO, :] = zero
            dxl_ext[tb:tb + HALO, :] = zero
            dcc_ext[tb:tb + HALO, :] = zero
            dcar[...] = zero
            for cp in cps:
                cp.wait()

        mod = modraw_ref[...] + adab_ref[...]
        shift1, scale1, gate1, scale2 = mod[0:1], mod[1:2], mod[2:3], mod[4:5]
        x = x_ref[...]
        mixed = mixed_ref[...]

        x1 = x + gate1 * mixed
        r2 = lax.rsqrt(_rowmean(x1 * x1) + EPS)
        xn2 = x1 * r2
        dh2 = dh2_ref[...]
        vec_ref[V_SHIFT2:V_SHIFT2 + 1, :] += _colsum(dh2)
        vec_ref[V_SCALE2:V_SCALE2 + 1, :] += _colsum(dh2 * xn2 * g2_ref[...])
        vec_ref[V_G2:V_G2 + 1, :] += _colsum(dh2 * (1.0 + scale2) * xn2)
        dxn2 = dh2 * g2_ref[...] * (1.0 + scale2)
        dx1 = dx2_ref[...] + r2 * (dxn2 - xn2 * _rowmean(dxn2 * xn2))
        vec_ref[V_GATE1:V_GATE1 + 1, :] += _colsum(dx1 * mixed)
        dmixed = (gate1 * dx1).astype(BF16)

        proj = proj_ref[...]
        u_lx, u_ly, u_b, u_c, u_v = (proj[:, k * W:(k + 1) * W] for k in range(5))
        has_prev = (blk > 0).astype(F32)
        projh = projh_ref[...]
        ulx_ext[0:HALO, :] = projh[:, 0:W] * has_prev
        ulx_ext[HALO:HALO + tb, :] = u_lx
        xl = bl_ref[...] + wl_ref[CONV_L - 1:CONV_L, :] * u_lx
        for k in range(CONV_L - 1):
            xl = xl + wl_ref[k:k + 1, :] * ulx_ext[pl.ds(HALO - (CONV_L - 1) + k, tb), :]
        xlb = xl.astype(BF16)
        r = _sigmoid(_dot(xlb, bda_ref[...]) + ba_ref[...])
        ig = _sigmoid(_dot(xlb, bdx_ref[...]) + bxb_ref[...])
        sp = _softplus(ap_ref[...])
        log_a = (-C_GATE) * r * sp
        a = jnp.exp(log_a)
        mult_raw = jnp.sqrt(_one_minus_sq(a, log_a))
        first = (blk * tb + lax.broadcasted_iota(jnp.int32, (tb, W), 0)) == 0
        mult = jnp.where(first, 1.0, mult_raw)
        hl = hl_ref[...]
        ge, th = _gelu(u_ly)
        p = ge * hl
        rp = lax.rsqrt(_group_mean(p * p, avg_ref[...]) + EPS)
        pn = p * rp
        cv = u_c * u_v
        cv_ext[0:HALO, :] = projh[:, 3 * W:4 * W] * projh[:, 4 * W:5 * W] * has_prev
        cv_ext[HALO:HALO + tb, :] = cv
        cc = ws_ref[CONV_S - 1:CONV_S, :] * cv
        for k in range(CONV_S - 1):
            cc = cc + ws_ref[k:k + 1, :] * cv_ext[pl.ds(HALO - (CONV_S - 1) + k, tb), :]
        q = u_b * cc
        rq = lax.rsqrt(_group_mean(q * q, avg_ref[...]) + EPS)
        qn = q * rq

        dmixed_ref[...] = dmixed
        ycatt_ref[0:W, :] = (pn * gl_ref[...]).T.astype(BF16)
        ycatt_ref[W:2 * W, :] = (qn * gc_ref[...]).T.astype(BF16)
        dyl = _dot_nt(dmixed, wout_v[0:W, :])
        dyc = _dot_nt(dmixed, wout_v[W:2 * W, :])

        dqn = dyc * gc_ref[...]
        dq = rq * (dqn - qn * _group_mean(dqn * qn, avg_ref[...]))
        du_b = dq * cc
        dcc = dq * u_b
        dcc_ext[0:tb, :] = dcc
        dcv = ws_ref[CONV_S - 1:CONV_S, :] * dcc
        for k in range(CONV_S - 1):
            dcv = dcv + ws_ref[k:k + 1, :] * dcc_ext[pl.ds(CONV_S - 1 - k, tb), :]
        dcc_ext[tb:tb + HALO, :] = dcc_ext[0:HALO, :]
        du_c = dcv * u_v
        du_v = dcv * u_c
        dws = [_colsum(dcc * cv_ext[pl.ds(HALO - (CONV_S - 1) + k, tb), :]) for k in range(CONV_S)]

        dpn = dyl * gl_ref[...]
        dp = rp * (dpn - pn * _group_mean(dpn * pn, avg_ref[...]))
        du_ly = dp * hl * _gelu_grad(u_ly, th)
        g_s[...] = dp * ge
        a_ext[0:tb, :] = a
        an_s[...] = a_ext[pl.ds(1, tb), :]
        _scan_groups(hb, an_s, g_s, dh_s, dcar, reverse=True)
        a_ext[tb:tb + HALO, :] = a_ext[0:HALO, :]
        dh = dh_s[...]
        hl_ext[0:HALO, :] = hlh_ref[...] * has_prev
        hl_ext[HALO:HALO + tb, :] = hl
        da = dh * hl_ext[pl.ds(HALO - 1, tb), :]
        dmult = dh * (ig * xl)
        dig = dh * (mult * xl)
        dxl = dh * (mult * ig)
        dlog = da * a - jnp.where(first, 0.0, dmult * (a * a) / mult_raw)
        dr = dlog * ((-C_GATE) * sp)
        dsp = _colsum(dlog * ((-C_GATE) * r))
        dga = dr * r * (1.0 - r)
        dgx = dig * ig * (1.0 - ig)
        dgab = dga.astype(BF16)
        dgxb = dgx.astype(BF16)
        xlt_ref[...] = xl.T.astype(BF16)
        dgate_ref[:, 0:W] = dgab
        dgate_ref[:, W:2 * W] = dgxb
        dxl = dxl + _dot_nt(dgab, bda_ref[...]) + _dot_nt(dgxb, bdx_ref[...])
        dxl_ext[0:tb, :] = dxl
        du_lx = wl_ref[CONV_L - 1:CONV_L, :] * dxl
        for k in range(CONV_L - 1):
            du_lx = du_lx + wl_ref[k:k + 1, :] * dxl_ext[pl.ds(CONV_L - 1 - k, tb), :]
        dxl_ext[tb:tb + HALO, :] = dxl_ext[0:HALO, :]
        dwl = [_colsum(dxl * ulx_ext[pl.ds(HALO - (CONV_L - 1) + k, tb), :]) for k in range(CONV_L)]

        cat = lambda u, v: jnp.concatenate([u, v], axis=1)
        vec_ref[V_BL_BA:V_BL_BA + 1, :] += cat(_colsum(dxl), _colsum(dga))
        vec_ref[V_BX_SP:V_BX_SP + 1, :] += cat(_colsum(dgx), dsp)
        vec_ref[V_GL_GC:V_GL_GC + 1, :] += cat(_colsum(dyl * pn), _colsum(dyc * qn))
        vec_ref[V_WL01:V_WL01 + 1, :] += cat(dwl[0], dwl[1])
        vec_ref[V_WL23:V_WL23 + 1, :] += cat(dwl[2], dwl[3])
        vec_ref[V_WS01:V_WS01 + 1, :] += cat(dws[0], dws[1])
        vec_ref[V_WS2:V_WS2 + 1, 0:W] += dws[2]

        r1 = lax.rsqrt(_rowmean(x * x) + EPS)
        xn1 = x * r1
        hb_ref[...] = ((xn1 * g1_ref[...]) * (1.0 + scale1) + shift1).astype(BF16)
        dh_in = jnp.zeros((tb, D), F32)
        for k, du in enumerate((du_lx, du_ly, du_b, du_c, du_v)):
            dprojt_ref[k * W:(k + 1) * W, :] = du.T.astype(BF16)
            dh_in = dh_in + _dot(du.astype(BF16), win_v[k * W:(k + 1) * W, :])
        vec_ref[V_SHIFT1:V_SHIFT1 + 1, :] += _colsum(dh_in)
        vec_ref[V_SCALE1:V_SCALE1 + 1, :] += _colsum(dh_in * xn1 * g1_ref[...])
        vec_ref[V_G1:V_G1 + 1, :] += _colsum(dh_in * (1.0 + scale1) * xn1)
        dxn1 = dh_in * g1_ref[...] * (1.0 + scale1)
        gx_ref[...] = dx1 + r1 * (dxn1 - xn1 * _rowmean(dxn1 * xn1))

        @pl.when(i == nb - 1)
        def _():
            for cp in chip_copies:
                cp.wait_recv()
            for cp in chip_copies:
                cp.wait_send()

    rev = lambda cols: pl.BlockSpec((tb, cols), lambda i: (nb - 1 - i, 0))
    rev_t = lambda rows: pl.BlockSpec((rows, tb), lambda i: (0, nb - 1 - i))
    halo = lambda cols: pl.BlockSpec((HALO, cols), lambda i: (jnp.maximum((nb - 1 - i) * hb - 1, 0), 0))
    full = lambda a: pl.BlockSpec(a.shape, lambda i: (0,) * a.ndim)
    small = (modraw, adab, g1, g2, wl, bl, bda, bdx, ba, bxb, ap, ws, gl, gc, avg)
    ext = pltpu.VMEM((tb + HALO, W), F32)
    n_sems = len(CHIP_FLIPS) * n_sums
    return pl.pallas_call(
        body,
        name="mixer_bwd",
        grid=(nb,),
        in_specs=[rev(D), rev(D), rev(D), rev(D), rev(D_IN), halo(D_IN), rev(W), halo(W)]
        + [full(a) for a in small] + [ANY] * (1 + n_sums),
        out_specs=[rev(D), pl.BlockSpec((V_ROWS, D), lambda i: (0, 0)), rev(D), rev_t(D_IN), rev(D), rev_t(D),
                   rev_t(W), rev(2 * W)] + [ANY] * n_sums,
        out_shape=[jax.ShapeDtypeStruct((t_len, D), F32), jax.ShapeDtypeStruct((V_ROWS, D), F32),
                   jax.ShapeDtypeStruct((t_len, D), BF16), jax.ShapeDtypeStruct((D_IN, t_len), BF16),
                   jax.ShapeDtypeStruct((t_len, D), BF16), jax.ShapeDtypeStruct((D, t_len), BF16),
                   jax.ShapeDtypeStruct((W, t_len), BF16), jax.ShapeDtypeStruct((t_len, 2 * W), BF16)]
        + [jax.ShapeDtypeStruct((len(CHIP_FLIPS),) + s.shape[1:], s.dtype) for s in chip_sums],
        scratch_shapes=[pltpu.VMEM((D_IN, D), BF16), pltpu.VMEM((D, D), BF16), pltpu.SemaphoreType.DMA((2,)),
                        ext, ext, ext, ext, ext, ext, pltpu.VMEM((HALO, W), F32),
                        pltpu.VMEM((tb, W), F32), pltpu.VMEM((tb, W), F32), pltpu.VMEM((tb, W), F32),
                        pltpu.SemaphoreType.DMA((n_sems,)), pltpu.SemaphoreType.DMA((n_sems,))],
        compiler_params=pltpu.CompilerParams(dimension_semantics=("arbitrary",), vmem_limit_bytes=VMEM_LIMIT),
    )(x, mixed, dh2, dx2, proj, proj, hl, hl, *small, wpack, *chip_sums)


def _matmul(name, a, b, tm=512):
    m, k = a.shape
    n = b.shape[1]

    def body(a_ref, b_ref, o_ref):
        o_ref[...] = _dot(a_ref[...], b_ref[...])

    return pl.pallas_call(
        body,
        name=name,
        grid=(m // tm,),
        in_specs=[pl.BlockSpec((tm, k), lambda i: (i, 0)), pl.BlockSpec((k, n), lambda i: (0, 0))],
        out_specs=pl.BlockSpec((tm, n), lambda i: (i, 0)),
        out_shape=jax.ShapeDtypeStruct((m, n), F32),
        compiler_params=pltpu.CompilerParams(dimension_semantics=("arbitrary",), vmem_limit_bytes=VMEM_LIMIT),
    )(a, b)


def _gate_wgrad(xl_t, dgate, avg):
    hd = W // 8

    def body(a_ref, b_ref, avg_ref, o_ref):
        full = _dot(a_ref[...], b_ref[...])
        row = lax.broadcasted_iota(jnp.int32, (W, hd), 0)
        col = lax.broadcasted_iota(jnp.int32, (W, hd), 1)
        fold = ((row & (hd - 1)) == col).astype(BF16)
        keep = avg_ref[...] != 0
        for g in range(2):
            m = jnp.where(keep, full[:, g * W:(g + 1) * W], 0.0)
            hi = m.astype(BF16)
            rest = m - hi.astype(F32)
            mid = rest.astype(BF16)
            lo = (rest - mid.astype(F32)).astype(BF16)
            o_ref[g] = _dot(hi, fold) + _dot(mid, fold) + _dot(lo, fold)

    return pl.pallas_call(
        body,
        name="wgrad_gate",
        in_specs=[WHOLE] * 3,
        out_specs=WHOLE,
        out_shape=jax.ShapeDtypeStruct((2, W, hd), F32),
        compiler_params=pltpu.CompilerParams(vmem_limit_bytes=VMEM_LIMIT),
    )(xl_t, dgate, avg)


def _block_diag(w):
    n, m, _ = w.shape
    eye = jnp.eye(n, dtype=w.dtype)
    return (w[:, :, None, :] * eye[:, None, :, None]).reshape(n * m, n * m)


def _pad_rows(a, rows):
    return jnp.pad(a, ((0, rows - a.shape[0]),) + ((0, 0),) * (a.ndim - 1))


def _position():
    return lax.axis_index("x"), lax.axis_index("y"), lax.axis_index("c")


def _linear(pos):
    return 4 * pos[0] + 2 * pos[1] + pos[2]


def _flip(pos, k):
    return tuple(1 - p if k & bit else p for p, bit in zip(pos, (4, 2, 1)))


def _exchange_all(make_copy, make_arrival):
    copies = [make_copy(k) for k in range(1, N_DEV)]
    for cp in copies:
        cp.start()
    for k in range(1, N_DEV):
        make_arrival(k).wait_recv()
    for cp in copies:
        cp.wait_send()


def _mod_exchange_steps(cols):
    def steps(msg_ref, adaw_ref, gath_ref, mod_ref, sendbuf, send_a, recv_a, send_b, recv_b):
        me = _position()
        me_lin = _linear(me)
        m = msg_ref[...]
        row = lax.broadcasted_iota(jnp.int32, m.shape, 0)
        gath_ref[me_lin] = jnp.where(row == 0, m * _sigmoid(m), m)

        def gather_copy(k, src_lin):
            return pltpu.make_async_remote_copy(
                src_ref=gath_ref.at[src_lin], dst_ref=gath_ref.at[src_lin], send_sem=send_a.at[k - 1],
                recv_sem=recv_a.at[k - 1], device_id=_flip(me, k), device_id_type=MESH)

        _exchange_all(lambda k: gather_copy(k, me_lin), lambda k: gather_copy(k, _linear(_flip(me, k))))

        sc_all = gath_ref[:, 0, :]
        scb = jnp.concatenate([sc_all, jnp.zeros_like(sc_all)], axis=0).astype(BF16)
        prod = _dot(scb, adaw_ref[...].astype(BF16))
        for b in range(N_DEV):
            sendbuf[b] = jnp.broadcast_to(prod[b:b + 1, :], (HALO, cols))
        mod_ref[me_lin] = sendbuf[me_lin]

        def row_copy(k, dst_lin):
            peer = _flip(me, k)
            return pltpu.make_async_remote_copy(
                src_ref=sendbuf.at[_linear(peer)], dst_ref=mod_ref.at[dst_lin], send_sem=send_b.at[k - 1],
                recv_sem=recv_b.at[k - 1], device_id=peer, device_id_type=MESH)

        _exchange_all(lambda k: row_copy(k, me_lin), lambda k: row_copy(k, _linear(_flip(me, k))))

    return steps


def _gather_and_mod(msg, ada_w, block):
    rows, cols = block.shape
    mod_cols = ada_w.shape[1]
    mod_steps = _mod_exchange_steps(mod_cols)

    def body(msg_ref, adaw_ref, x_ref, gath_ref, mod_ref, out_ref, sendbuf, send_a, recv_a, send_b, recv_b,
             send_sems, recv_sems, sib_send_sems, sib_recv_sems, local_sem):
        x, y, c = _position()
        me, sibling = (x, y, c), (x, y, 1 - c)
        sends, forward, arrivals = _chip_gather_copies(x_ref, out_ref, send_sems, recv_sems)

        def to_sibling(j, block_of, src=None):
            dst = out_ref.at[_linear(block_of)]
            return pltpu.make_async_remote_copy(
                src_ref=dst if src is None else src, dst_ref=dst, send_sem=sib_send_sems.at[j],
                recv_sem=sib_recv_sems.at[j], device_id=sibling, device_id_type=MESH)

        mine = pltpu.make_async_copy(x_ref, out_ref.at[_linear(me)], local_sem)
        mine.start()
        passes = [to_sibling(0, me, src=x_ref)] + [to_sibling(1 + j, p) for j, p in enumerate(_route_peers(me))]
        passes[0].start()
        for cp in sends:
            cp.start()
        mod_steps(msg_ref, adaw_ref, gath_ref, mod_ref, sendbuf, send_a, recv_a, send_b, recv_b)
        arrivals[0].wait_recv()
        forward.start()
        passes[1].start()
        arrivals[1].wait_recv()
        passes[2].start()
        arrivals[2].wait_recv()
        passes[3].start()
        for j, p in enumerate((sibling,) + _route_peers(sibling)):
            to_sibling(j, p).wait_recv()
        for cp in sends + [forward] + passes:
            cp.wait_send()
        mine.wait()

    return pl.pallas_call(
        body,
        name="gather_and_mod",
        in_specs=[WHOLE, WHOLE, ANY],
        out_specs=[WHOLE, WHOLE, ANY],
        out_shape=[jax.ShapeDtypeStruct((N_DEV, HALO, D), F32), jax.ShapeDtypeStruct((N_DEV, HALO, mod_cols), F32),
                   jax.ShapeDtypeStruct((N_DEV, rows, cols), block.dtype)],
        scratch_shapes=[pltpu.VMEM((N_DEV, HALO, mod_cols), F32)] + [pltpu.SemaphoreType.DMA((N_DEV - 1,))] * 4
        + [pltpu.SemaphoreType.DMA((3,)), pltpu.SemaphoreType.DMA((3,)), pltpu.SemaphoreType.DMA((4,)),
           pltpu.SemaphoreType.DMA((4,)), pltpu.SemaphoreType.DMA],
        compiler_params=pltpu.CompilerParams(vmem_limit_bytes=VMEM_LIMIT),
    )(msg, ada_w, block)


def _sibling_copies(srcs, dsts, send_sems, recv_sems):
    x, y, c = _position()
    copies = []
    for a, (src, dst) in enumerate(zip(srcs, dsts)):
        for k in range(4):
            copies.append(pltpu.make_async_remote_copy(
                src_ref=src.at[k, 1 - c], dst_ref=dst.at[k], send_sem=send_sems.at[4 * a + k],
                recv_sem=recv_sems.at[4 * a + k], device_id=(x, y, 1 - c), device_id_type=MESH))
    return copies


def _row_block(rows):
    return 256 if rows % 256 == 0 else rows // 2


def _pair_sum(pos, mine, recv):
    _, cores, rows, cols = mine.shape
    rb = _row_block(rows)

    def body(pos_ref, mine_ref, recv_ref, out_ref):
        out_ref[0] = (mine_ref[0, 0] + recv_ref[0]).astype(BF16)

    other = lambda k, pos: jnp.bitwise_xor(pos[1], k + 1)
    core = lambda pos: pos[0] * (cores - 1)
    return pl.pallas_call(
        body,
        name="grad_pair_sum",
        grid_spec=pltpu.PrefetchScalarGridSpec(
            num_scalar_prefetch=1, grid=(3, rows // rb),
            in_specs=[pl.BlockSpec((1, 1, rb, cols), lambda k, r, pos: (other(k, pos), core(pos), r, 0)),
                      pl.BlockSpec((1, rb, cols), lambda k, r, pos: (other(k, pos), r, 0))],
            out_specs=pl.BlockSpec((1, rb, cols), lambda k, r, pos: (other(k, pos), r, 0))),
        out_shape=jax.ShapeDtypeStruct((4, rows, cols), BF16),
        compiler_params=pltpu.CompilerParams(dimension_semantics=("arbitrary", "arbitrary")),
    )(pos, mine, recv)


def _final_sum(pos, mine, recv, chips):
    _, cores, rows, cols = mine.shape
    rb = _row_block(rows)

    def body(pos_ref, mine_ref, recv_ref, chips_ref, out_ref):
        g = mine_ref[0, 0] + recv_ref[0]
        for j in range(3):
            g = g + chips_ref[j].astype(F32)
        out_ref[...] = g

    return pl.pallas_call(
        body,
        name="grad_final_sum",
        grid_spec=pltpu.PrefetchScalarGridSpec(
            num_scalar_prefetch=1, grid=(rows // rb,),
            in_specs=[pl.BlockSpec((1, 1, rb, cols), lambda r, pos: (pos[1], pos[0] * (cores - 1), r, 0)),
                      pl.BlockSpec((1, rb, cols), lambda r, pos: (pos[1], r, 0)),
                      pl.BlockSpec((3, rb, cols), lambda r, pos: (0, r, 0))],
            out_specs=pl.BlockSpec((rb, cols), lambda r, pos: (r, 0))),
        out_shape=jax.ShapeDtypeStruct((rows, cols), F32),
        compiler_params=pltpu.CompilerParams(dimension_semantics=("arbitrary",)),
    )(pos, mine, recv, chips)


LOSS_ROW = V_ROWS + 8
GB_BASE = LOSS_ROW + 8


def _sibling_and_route(gmod8, sc_t, parts):
    cols = gmod8.shape[1]
    n = len(parts)

    def body(gmod_ref, sct_ref, *refs):
        srcs, (gadaw_ref, gb_ref), dsts = refs[:n], refs[n:n + 2], refs[n + 2:2 * n + 2]
        sendbuf, grecv, send_a, recv_a, sib_send, sib_recv = refs[2 * n + 2:]
        sib_copies = _sibling_copies(srcs, dsts, sib_send, sib_recv)
        for cp in sib_copies:
            cp.start()
        me = _position()
        me_lin = _linear(me)
        gm = gmod_ref[...]
        for b in range(N_DEV):
            sendbuf[b] = jnp.broadcast_to(gm[b:b + 1, :], (HALO, cols))
        grecv[me_lin] = sendbuf[me_lin]

        def row_copy(k, dst_lin):
            peer = _flip(me, k)
            return pltpu.make_async_remote_copy(
                src_ref=sendbuf.at[_linear(peer)], dst_ref=grecv.at[dst_lin], send_sem=send_a.at[k - 1],
                recv_sem=recv_a.at[k - 1], device_id=peer, device_id_type=MESH)

        _exchange_all(lambda k: row_copy(k, me_lin), lambda k: row_copy(k, _linear(_flip(me, k))))
        g_all = grecv[:, 0, :]
        g_pad = jnp.concatenate([g_all, jnp.zeros((sct_ref.shape[1] - N_DEV, cols), F32)], axis=0).astype(BF16)
        gadaw_ref[...] = _dot(sct_ref[...], g_pad)
        gb_ref[...] = jnp.broadcast_to(_colsum(g_all), (HALO, cols))
        for cp in sib_copies:
            cp.wait_recv()
        for cp in sib_copies:
            cp.wait_send()

    return pl.pallas_call(
        body,
        name="sibling_and_route",
        in_specs=[WHOLE, WHOLE] + [ANY] * n,
        out_specs=[WHOLE, WHOLE] + [ANY] * n,
        out_shape=[jax.ShapeDtypeStruct((D, cols), F32), jax.ShapeDtypeStruct((HALO, cols), F32)]
        + [jax.ShapeDtypeStruct((4,) + p.shape[2:], p.dtype) for p in parts],
        scratch_shapes=[pltpu.VMEM((N_DEV, HALO, cols), F32), pltpu.VMEM((N_DEV, HALO, cols), F32),
                        pltpu.SemaphoreType.DMA((N_DEV - 1,)), pltpu.SemaphoreType.DMA((N_DEV - 1,)),
                        pltpu.SemaphoreType.DMA((4 * n,)), pltpu.SemaphoreType.DMA((4 * n,))],
        compiler_params=pltpu.CompilerParams(vmem_limit_bytes=VMEM_LIMIT),
    )(gmod8, sc_t, *parts)


def _chips_and_gather(msg_vec, msg_gate, gb_rows, chip_sums):
    cols = gb_rows.shape[1]
    n = len(chip_sums)
    vec_rows = GB_BASE + N_DEV

    def body(vec_ref, gate_ref, gb_ref, *refs):
        srcs, (sumv_ref, sumg_ref), dsts = refs[:n], refs[n:n + 2], refs[n + 2:2 * n + 2]
        (myv, myg, sibv, sibg, chipv, chipg, sib_send, sib_recv, peer_send, peer_recv,
         chip_send, chip_recv) = refs[2 * n + 2:]
        chip_copies = _chip_scatter_copies(srcs, dsts, chip_send, chip_recv)
        for cp in chip_copies:
            cp.start()
        x, y, c = me = _position()
        my_chip = 2 * x + y
        myv[0:GB_BASE, :] = vec_ref[...]
        slot = lax.broadcasted_iota(jnp.int32, (N_DEV, D), 0) == _linear(me)
        gb_wide = jnp.concatenate([gb_ref[...], jnp.zeros((N_DEV, D - cols), F32)], axis=1)
        myv[GB_BASE:vec_rows, :] = jnp.where(slot, gb_wide, 0.0)
        myg[...] = gate_ref[...]

        swaps = [pltpu.make_async_remote_copy(
            src_ref=src, dst_ref=dst, send_sem=sib_send.at[a], recv_sem=sib_recv.at[a], device_id=(x, y, 1 - c),
            device_id_type=MESH) for a, (src, dst) in enumerate(((myv, sibv), (myg, sibg)))]
        for cp in swaps:
            cp.start()
        for cp in swaps:
            cp.wait_recv()
        chipv[my_chip] = myv[...] + sibv[...]
        chipg[my_chip] = myg[...] + sibg[...]

        def chip_copy(a, buf, j, k, slot_chip):
            peer = _flip(me, k)
            return pltpu.make_async_remote_copy(
                src_ref=buf.at[slot_chip], dst_ref=buf.at[slot_chip], send_sem=peer_send.at[3 * a + j],
                recv_sem=peer_recv.at[3 * a + j], device_id=peer, device_id_type=MESH)

        sends = [chip_copy(a, buf, j, k, my_chip) for a, buf in enumerate((chipv, chipg)) for j, k in enumerate(CHIP_FLIPS)]
        for cp in sends:
            cp.start()
        for a, buf in enumerate((chipv, chipg)):
            for j, k in enumerate(CHIP_FLIPS):
                peer = _flip(me, k)
                chip_copy(a, buf, j, k, 2 * peer[0] + peer[1]).wait_recv()
        sumv_ref[...] = ((chipv[0] + chipv[1]) + chipv[2]) + chipv[3]
        sumg_ref[...] = ((chipg[0] + chipg[1]) + chipg[2]) + chipg[3]
        for cp in swaps + sends:
            cp.wait_send()
        for cp in chip_copies:
            cp.wait_recv()
        for cp in chip_copies:
            cp.wait_send()

    n_chip = max(len(CHIP_FLIPS) * n, 1)
    vshape, gshape = (vec_rows, D), msg_gate.shape
    return pl.pallas_call(
        body,
        name="chips_and_gather",
        in_specs=[WHOLE] * 3 + [ANY] * n,
        out_specs=[WHOLE] * 2 + [ANY] * n,
        out_shape=[jax.ShapeDtypeStruct(vshape, F32), jax.ShapeDtypeStruct(gshape, F32)]
        + [jax.ShapeDtypeStruct((len(CHIP_FLIPS),) + s.shape[1:], s.dtype) for s in chip_sums],
        scratch_shapes=[pltpu.VMEM(vshape, F32), pltpu.VMEM(gshape, F32), pltpu.VMEM(vshape, F32),
                        pltpu.VMEM(gshape, F32), pltpu.VMEM((4,) + vshape, F32), pltpu.VMEM((4,) + gshape, F32),
                        pltpu.SemaphoreType.DMA((2,)), pltpu.SemaphoreType.DMA((2,)),
                        pltpu.SemaphoreType.DMA((2 * len(CHIP_FLIPS),)), pltpu.SemaphoreType.DMA((2 * len(CHIP_FLIPS),)),
                        pltpu.SemaphoreType.DMA((n_chip,)), pltpu.SemaphoreType.DMA((n_chip,))],
        compiler_params=pltpu.CompilerParams(vmem_limit_bytes=VMEM_LIMIT),
    )(msg_vec, msg_gate, gb_rows, *chip_sums)


def _adamw_math(w, g, m, v):
    m = ADAM_B1 * m + (1.0 - ADAM_B1) * g
    v = ADAM_B2 * v + (1.0 - ADAM_B2) * (g * g)
    m_hat = m / (1.0 - ADAM_B1 ** ADAM_STEP)
    v_hat = v / (1.0 - ADAM_B2 ** ADAM_STEP)
    delta = -ADAM_LR * (m_hat / (jnp.sqrt(v_hat) + ADAM_EPS) + ADAM_WD * w)
    return delta, m, v


def _adamw(name, w, g, m, v):
    rows, cols = w.shape
    rb = 256 if rows % 256 == 0 else rows

    def body(w_ref, g_ref, m_ref, v_ref, d_ref, mo_ref, vo_ref):
        d_ref[...], mo_ref[...], vo_ref[...] = _adamw_math(w_ref[...], g_ref[...], m_ref[...], v_ref[...])

    spec = pl.BlockSpec((rb, cols), lambda r: (r, 0))
    return pl.pallas_call(
        body,
        name="adamw_" + name,
        grid=(rows // rb,),
        in_specs=[spec] * 4,
        out_specs=[spec] * 3,
        out_shape=[jax.ShapeDtypeStruct((rows, cols), F32)] * 3,
        compiler_params=pltpu.CompilerParams(dimension_semantics=("arbitrary",)),
    )(w, g, m, v)


def _update_beside_chips(pos, sum_jobs, plain_jobs, chip_sums):
    rb = 256
    jobs = [("sum", j) for j in sum_jobs] + [("plain", j) for j in plain_jobs]
    offs, total = [], 0
    for _, j in jobs:
        offs.append(total)
        total += j[-1].shape[0] // rb
    n_chip = len(chip_sums)
    n_in = sum(len(j) for _, j in jobs)
    n_out = 4 * len(sum_jobs) + 3 * len(plain_jobs)

    def body(pos_ref, *refs):
        ins, srcs = refs[:n_in], refs[n_in:n_in + n_chip]
        outs = refs[n_in + n_chip:n_in + n_chip + n_out]
        dsts = refs[n_in + n_chip + n_out:n_in + 2 * n_chip + n_out]
        send_sems, recv_sems = refs[n_in + 2 * n_chip + n_out:]
        s = pl.program_id(0)
        copies = _chip_scatter_copies(srcs, dsts, send_sems, recv_sems)

        @pl.when(s == 0)
        def _():
            for cp in copies:
                cp.start()

        i_in = i_out = 0
        for (kind, j), off in zip(jobs, offs):
            steps = j[-1].shape[0] // rb
            j_in = ins[i_in:i_in + len(j)]
            i_in += len(j)
            j_out = outs[i_out:i_out + (4 if kind == "sum" else 3)]
            i_out += len(j_out)

            @pl.when((s >= off) & (s < off + steps))
            def _(kind=kind, j_in=j_in, j_out=j_out):
                if kind == "sum":
                    mine_ref, recv_ref, chips_ref, w_ref, m_ref, v_ref = j_in
                    g = mine_ref[0, 0] + recv_ref[0]
                    for q in range(len(CHIP_FLIPS)):
                        g = g + chips_ref[q].astype(F32)
                    j_out[0][...] = g
                    rest = j_out[1:]
                else:
                    g_ref, w_ref, m_ref, v_ref = j_in
                    g = g_ref[...]
                    rest = j_out
                rest[0][...], rest[1][...], rest[2][...] = _adamw_math(w_ref[...], g, m_ref[...], v_ref[...])

        @pl.when(s == total - 1)
        def _():
            for cp in copies:
                cp.wait_recv()
            for cp in copies:
                cp.wait_send()

    in_specs, out_specs, out_shape, args = [], [], [], []
    for (kind, j), off in zip(jobs, offs):
        rows, cols = j[-1].shape
        steps = rows // rb
        blk = lambda s, off=off, steps=steps: jnp.clip(s - off, 0, steps - 1)
        flat = pl.BlockSpec((rb, cols), lambda s, pos, blk=blk: (blk(s), 0))
        if kind == "sum":
            in_specs += [pl.BlockSpec((1, 1, rb, cols), lambda s, pos, blk=blk: (pos[1], 0, blk(s), 0)),
                         pl.BlockSpec((1, rb, cols), lambda s, pos, blk=blk: (pos[1], blk(s), 0)),
                         pl.BlockSpec((len(CHIP_FLIPS), rb, cols), lambda s, pos, blk=blk: (0, blk(s), 0))]
            in_specs += [flat] * 3
        else:
            in_specs += [flat] * 4
        n_res = 4 if kind == "sum" else 3
        out_specs += [flat] * n_res
        out_shape += [jax.ShapeDtypeStruct((rows, cols), F32)] * n_res
        args += list(j)
    in_specs += [ANY] * n_chip
    out_specs += [ANY] * n_chip
    out_shape += [jax.ShapeDtypeStruct((len(CHIP_FLIPS),) + c.shape[1:], c.dtype) for c in chip_sums]
    n_sems = len(CHIP_FLIPS) * n_chip
    outs = pl.pallas_call(
        body,
        name="update_beside_chips",
        grid_spec=pltpu.PrefetchScalarGridSpec(
            num_scalar_prefetch=1, grid=(total,), in_specs=in_specs, out_specs=out_specs,
            scratch_shapes=[pltpu.SemaphoreType.DMA((n_sems,)), pltpu.SemaphoreType.DMA((n_sems,))]),
        out_shape=out_shape,
        compiler_params=pltpu.CompilerParams(dimension_semantics=("arbitrary",), vmem_limit_bytes=VMEM_LIMIT),
    )(pos, *args, *chip_sums)
    sums = [tuple(outs[4 * i:4 * i + 4]) for i in range(len(sum_jobs))]
    base = 4 * len(sum_jobs)
    plains = [tuple(outs[base + 3 * i:base + 3 * i + 3]) for i in range(len(plain_jobs))]
    return sums, plains, list(outs[n_out:])


def _adamw_small(ws, gs, ms, vs, sigmoid_scaled):
    n = len(ws)

    def body(*refs):
        w_refs, g_refs, m_refs, v_refs = (refs[i * n:(i + 1) * n] for i in range(4))
        outs = refs[4 * n:]
        for i in range(n):
            w = w_refs[i][...]
            g = g_refs[i][...]
            if sigmoid_scaled[i]:
                g = g * _sigmoid(w)
            delta, m, v = _adamw_math(w, g, m_refs[i][...], v_refs[i][...])
            outs[4 * i][...] = g
            outs[4 * i + 1][...] = delta
            outs[4 * i + 2][...] = m
            outs[4 * i + 3][...] = v

    shapes = [jax.ShapeDtypeStruct(w.shape, F32) for w in ws for _ in range(4)]
    outs = pl.pallas_call(
        body,
        name="adamw_small",
        in_specs=[WHOLE] * (4 * n),
        out_specs=[WHOLE] * (4 * n),
        out_shape=shapes,
    )(*ws, *gs, *ms, *vs)
    return [outs[4 * i:4 * i + 4] for i in range(n)]


_WEIGHT_NAMES = ("ada_w", "ada_b", "norm1_g", "w_in", "lru_conv_w", "lru_conv_b", "gate_a_w", "gate_a_b", "gate_x_w",
                 "gate_x_b", "a_param", "short_conv_w", "lru_out_g", "conv_out_g", "w_out", "norm2_g", "w_mlp1",
                 "w_mlp2", "final_g")


def kernel(x, c, ada_w, ada_b, norm1_g, w_in, lru_conv_w, lru_conv_b, gate_a_w, gate_a_b, gate_x_w, gate_x_b, a_param, short_conv_w, lru_out_g, conv_out_g, w_out, norm2_g, w_mlp1, w_mlp2, final_g, loss_target, m_ada_w, m_ada_b, m_norm1_g, m_w_in, m_lru_conv_w, m_lru_conv_b, m_gate_a_w, m_gate_a_b, m_gate_x_w, m_gate_x_b, m_a_param, m_short_conv_w, m_lru_out_g, m_conv_out_g, m_w_out, m_norm2_g, m_w_mlp1, m_w_mlp2, m_final_g, v_ada_w, v_ada_b, v_norm1_g, v_w_in, v_lru_conv_w, v_lru_conv_b, v_gate_a_w, v_gate_a_b, v_gate_x_w, v_gate_x_b, v_a_param, v_short_conv_w, v_lru_out_g, v_conv_out_g, v_w_out, v_norm2_g, v_w_mlp1, v_w_mlp2, v_final_g):
    given = dict(locals())
    weights = {n: given[n] for n in _WEIGHT_NAMES}
    xi, yi, ci = _position()
    me_lin = _linear((xi, yi, ci))
    hd = W // N_DEV

    mixer_block = jnp.concatenate([w_out[0], w_in[0].T], axis=0).astype(BF16)
    mlp_block = jnp.concatenate([w_mlp1[0].T, w_mlp2[0]], axis=0).astype(BF16)

    msg = (jnp.pad(c, ((0, HALO - 1), (0, 0)))
           + jnp.pad(lru_conv_w[0], ((1, HALO - 1 - CONV_L), (0, D - hd)))
           + jnp.pad(short_conv_w[0], ((1 + CONV_L, 0), (0, D - hd))))
    gath, mod_all, wmix = _gather_and_mod(msg, ada_w[0], mixer_block)
    sc_all = gath[:, 0, :]
    wl = jnp.transpose(gath[:, 1:1 + CONV_L, :hd], (1, 0, 2)).reshape(CONV_L, W)
    ws = jnp.transpose(gath[:, 1 + CONV_L:HALO, :hd], (1, 0, 2)).reshape(CONV_S, W)
    modraw = _pad_rows(mod_all[:, 0, :].reshape(6, D), HALO)
    adab = _pad_rows(ada_b.reshape(6, D), HALO)

    x2d, tgt = x[0], loss_target[0]
    gf = final_g.reshape(1, D)
    bda = _block_diag(gate_a_w[0]).astype(BF16)
    bdx = _block_diag(gate_x_w[0]).astype(BF16)
    avg = _block_diag(jnp.full((8, W // 8, W // 8), 8.0 / W, F32)).astype(BF16)
    wl8 = _pad_rows(wl, HALO)
    ws8 = _pad_rows(ws, HALO)
    mixer_small = (wl8, lru_conv_b, bda, bdx, gate_a_b, gate_x_b, a_param, ws8, lru_out_g, conv_out_g, avg)
    proj, hl, mixed, wmlp = _mixer_fwd(x2d, modraw, adab, norm1_g, *mixer_small, wmix, mlp_block)
    wmlp = _sibling_forward(wmlp)
    h2t, f, dx2, dz, vec2, loss8 = _mlp_fwd(x2d, mixed, tgt, modraw, adab, norm2_g, gf, wmlp)
    pos = jnp.stack([ci, 2 * xi + yi]).astype(jnp.int32)
    by_dest = lambda g: g.reshape((4, 2, -1) + g.shape[-1:])
    dh2, dw1, dw2, sib1, sib2 = _mlp_bwd_half(pos, h2t, f, dz, wmlp, prior=_mlp_bwd_half(pos, h2t, f, dz, wmlp))
    mlp_parts = [dw1[:, None], dw2[:, None]]
    mlp_sib = [sib1, sib2]
    mlp_sums = [_pair_sum(pos, p, r) for p, r in zip(mlp_parts, mlp_sib)]
    gx, vec, hb, dproj_t, dmixed, ycat_t, xl_t, dgate, *mlp_chips = _mixer_bwd(
        x2d, mixed, dh2, dx2, proj, hl, modraw, adab, norm1_g, norm2_g, *mixer_small, wmix, mlp_sums)
    dwint = _matmul("wgrad_in", dproj_t, hb)
    dwout = _matmul("wgrad_out", ycat_t, dmixed)
    msg_gate = _gate_wgrad(xl_t, dgate, avg).reshape(W, 128)
    mix_parts = [by_dest(dwout), by_dest(dwint)]
    gmod8 = (jnp.pad(vec[0:5], ((0, 1), (0, 0))) + jnp.pad(vec2[0:1], ((5, 0), (0, 0)))).reshape(N_DEV, 6 * D // N_DEV)
    sc_t = jnp.pad(sc_all.T, ((0, 0), (0, 128 - N_DEV))).astype(BF16)
    g_adaw, gb_rows, *mix_sib = _sibling_and_route(gmod8, sc_t, mix_parts)
    mix_sums = [_pair_sum(pos, p, r) for p, r in zip(mix_parts, mix_sib)]
    loss_rows = jnp.pad(loss8[0:1], ((0, HALO - 1), (0, D - loss8.shape[1])))
    msg_vec = jnp.concatenate([vec, vec2, loss_rows], axis=0)
    state = lambda n: (weights[n][0], given["m_" + n][0], given["v_" + n][0])
    mlp_jobs = [(p, r, q, *state(n)) for p, r, q, n in zip(mlp_parts, mlp_sib, mlp_chips, ("w_mlp1", "w_mlp2"))]
    mlp_done, (adaw_done,), mix_chips = _update_beside_chips(pos, mlp_jobs, [(g_adaw, *state("ada_w"))], mix_sums)
    sum_vec, sum_gate = _chips_and_gather(msg_vec, msg_gate, gb_rows, [])
    g_wout, g_wint = (_final_sum(pos, p, r, q) for p, r, q in zip(mix_parts, mix_sib, mix_chips))
    loss = sum_vec[LOSS_ROW, 0]
    sum_gate = sum_gate.reshape(2, W, W // 8)
    lo, hi = slice(0, W), slice(W, 2 * W)
    wl_full = sum_vec[V_WL01:V_WL23 + 1].reshape(CONV_L, W)
    ws_full = sum_vec[V_WS01:V_WS2 + 1].reshape(CONV_S + 1, W)[:CONV_S]
    row = lambda r, cols: sum_vec[r:r + 1, cols]
    small_grads = {
        "ada_b": sum_vec[GB_BASE:GB_BASE + N_DEV, :6 * D // N_DEV].reshape(1, 6 * D),
        "norm1_g": row(V_G1, slice(0, D)),
        "lru_conv_w": lax.dynamic_slice(wl_full, (0, me_lin * hd), (CONV_L, hd)),
        "lru_conv_b": row(V_BL_BA, lo),
        "gate_a_w": sum_gate[0],
        "gate_a_b": row(V_BL_BA, hi),
        "gate_x_w": sum_gate[1],
        "gate_x_b": row(V_BX_SP, lo),
        "a_param": row(V_BX_SP, hi),
        "short_conv_w": lax.dynamic_slice(ws_full, (0, me_lin * hd), (CONV_S, hd)),
        "lru_out_g": row(V_GL_GC, lo),
        "conv_out_g": row(V_GL_GC, hi),
        "norm2_g": row(V_G2, slice(0, D)),
        "final_g": sum_vec[V_ROWS + 1:V_ROWS + 2, :],
    }
    names = list(small_grads)
    as2d = lambda a, n: a.reshape(small_grads[n].shape)
    small = _adamw_small([as2d(weights[n], n) for n in names], [small_grads[n] for n in names],
                         [as2d(given["m_" + n], n) for n in names], [as2d(given["v_" + n], n) for n in names],
                         [n == "a_param" for n in names])
    result = {n: tuple(o.reshape(weights[n].shape) for o in outs) for n, outs in zip(names, small)}

    for n, g in (("w_in", g_wint.T), ("w_out", g_wout)):
        w, m, v = state(n)
        result[n] = (g[None],) + tuple(o[None] for o in _adamw(n, w, g, m, v))
    result["w_mlp1"], result["w_mlp2"] = (tuple(o[None] for o in done) for done in mlp_done)
    result["ada_w"] = (g_adaw[None],) + tuple(o[None] for o in adaw_done)

    return (loss, gx[None], *[result[n][0] for n in _WEIGHT_NAMES], *[result[n][1] for n in _WEIGHT_NAMES],
            *[result[n][2] for n in _WEIGHT_NAMES], *[result[n][3] for n in _WEIGHT_NAMES])
```

```python
import functools

import jax
import jax.numpy as jnp
from jax import lax
from jax.experimental import pallas as pl
from jax.experimental.pallas import tpu as pltpu

F32 = jnp.float32
BF16 = jnp.bfloat16
MESH = pl.DeviceIdType.MESH

N_DEV = 8
D = 1024
W = 512
D_IN = 5 * W
D_FF = 4096
FF_BLK = D_FF // N_DEV
EPS = 1e-6
C_GATE = 8.0
CONV_L = 4
CONV_S = 3
HALO = 8

ROWS_W1T, ROWS_W2, ROWS_WOUT, ROWS_WIN = FF_BLK, FF_BLK, D // N_DEV, D_IN // N_DEV
OFF_WOUT = 0
OFF_WIN = OFF_WOUT + ROWS_WOUT
MIX_ROWS = OFF_WIN + ROWS_WIN
OFF_W1T = 0
OFF_W2 = OFF_W1T + ROWS_W1T
MLP_ROWS = OFF_W2 + ROWS_W2
CHIP_FLIPS = (4, 2, 6)
N_KEPT = 6

ADAM_LR = 0.001
ADAM_B1 = 0.9
ADAM_B2 = 0.999
ADAM_EPS = 1e-08
ADAM_WD = 0.01
ADAM_STEP = 10

VMEM_LIMIT = 56 * 1024 * 1024

TB_MIX = 256
TB_MIXB = 256
TB_MLP = 256
TB_MLPB = 512

ANY = pl.BlockSpec(memory_space=pl.ANY)
WHOLE = pl.BlockSpec(memory_space=pltpu.VMEM)


def _dot(a, b):
    return jnp.dot(a, b, preferred_element_type=F32)


def _dot_nt(a, b):
    return lax.dot_general(a, b, (((1,), (1,)), ((), ())), preferred_element_type=F32)


def _dot_tn(a, b):
    return lax.dot_general(a, b, (((0,), (0,)), ((), ())), preferred_element_type=F32)


def _sigmoid(v):
    return 1.0 / (1.0 + jnp.exp(-v))


def _softplus(v):
    t = jnp.exp(-jnp.abs(v))
    small = t * (1.0 - t * (0.5 - t * (1.0 / 3.0)))
    return jnp.maximum(v, 0.0) + jnp.where(t < 1e-2, small, jnp.log(1.0 + t))


def _one_minus_sq(a, log_a):
    return -jnp.tanh(log_a) * (a * a + 1.0)


_GELU_K = 0.7978845608028654
_GELU_C = 0.044715


def _gelu(u):
    th = jnp.tanh(_GELU_K * (u + _GELU_C * u * u * u))
    return 0.5 * u * (1.0 + th), th


def _gelu_grad(u, th):
    return 0.5 * (1.0 + th) + 0.5 * u * (1.0 - th * th) * _GELU_K * (1.0 + 3.0 * _GELU_C * u * u)


def _group_mean(v, avg):
    hi = v.astype(BF16)
    lo = (v - hi.astype(F32)).astype(BF16)
    return _dot(hi, avg) + _dot(lo, avg)


def _colsum(v):
    return jnp.sum(v, axis=0, keepdims=True)


def _rowmean(v):
    return jnp.mean(v, axis=-1, keepdims=True)


def _load_packed(wpack_hbm, off, rows, dst, sem):
    copies = [
        pltpu.make_async_copy(wpack_hbm.at[d, pl.ds(off, rows), :], dst.at[pl.ds(d * rows, rows), :], sem)
        for d in range(N_DEV)
    ]
    for cp in copies:
        cp.start()
    return copies


def _scan_groups(n_groups, a_ref, b_ref, out_ref, carry_ref, reverse):
    row = lax.broadcasted_iota(jnp.int32, (HALO, W), 0)

    def step(k, carry):
        g = (n_groups - 1 - k) if reverse else k
        rows = pl.ds(pl.multiple_of(g * HALO, HALO), HALO)
        a = a_ref[rows, :]
        b = b_ref[rows, :]
        for s in (1, 2, 4):
            if reverse:
                keep = row < HALO - s
                sh = HALO - s
            else:
                keep = row >= s
                sh = s
            a_sh = pltpu.roll(a, sh, axis=0)
            b_sh = pltpu.roll(b, sh, axis=0)
            b = jnp.where(keep, a * b_sh + b, b)
            a = jnp.where(keep, a * a_sh, a)
        h = b + a * carry
        out_ref[rows, :] = h
        edge = h[0:1, :] if reverse else h[HALO - 1:HALO, :]
        return jnp.broadcast_to(edge, (HALO, W))

    carry_ref[...] = lax.fori_loop(0, n_groups, step, carry_ref[...])


def _route_peers(me):
    x, y, c = me
    first = ((x + 1 - c) % 2, (y + c) % 2, c)
    second = ((x + c) % 2, (y + 1 - c) % 2, c)
    return first, second, (1 - x, 1 - y, c)


def _chip_gather_copies(block_hbm, out_hbm, send_sems, recv_sems):
    me = _position()
    first, second, diag = _route_peers(me)

    def copy(j, src, slot_of, to):
        return pltpu.make_async_remote_copy(
            src_ref=src, dst_ref=out_hbm.at[_linear(slot_of)], send_sem=send_sems.at[j], recv_sem=recv_sems.at[j],
            device_id=to, device_id_type=MESH)

    own_sends = [copy(0, block_hbm, me, first), copy(1, block_hbm, me, second)]
    forward = copy(2, out_hbm.at[_linear(first)], first, second)
    arrivals = [copy(0, block_hbm, first, first), copy(1, block_hbm, second, second), copy(2, block_hbm, diag, second)]
    return own_sends, forward, arrivals


def _mixer_fwd(x, modraw, adab, g1, wl, bl, bda, bdx, ba, bxb, ap, ws, gl, gc, avg, wpack, mlp_block):
    t_len = x.shape[0]
    tb = TB_MIX
    nb = t_len // tb

    def body(x_ref, modraw_ref, adab_ref, g1_ref, wl_ref, bl_ref, bda_ref, bdx_ref, ba_ref, bxb_ref, ap_ref,
             ws_ref, gl_ref, gc_ref, avg_ref, wpack_hbm, block_hbm, proj_ref, hl_ref, mixed_ref, kept_ref, wmlp_hbm,
             win_v, wout_v, sem, ulx_ext, cv_ext, hcar, a_s, b_s, send_sems, recv_sems, local_sem):
        i = pl.program_id(0)
        own = pltpu.make_async_copy(block_hbm, wmlp_hbm.at[_linear(_position())], local_sem)
        sends, forward, arrivals = _chip_gather_copies(block_hbm, wmlp_hbm, send_sems, recv_sems)

        @pl.when(i == 0)
        def _():
            own.start()
            for cp in sends:
                cp.start()

        @pl.when(i == nb - 1)
        def _():
            arrivals[0].wait_recv()
            forward.start()

        @pl.when(i == 0)
        def _():
            cps = _load_packed(wpack_hbm, OFF_WIN, ROWS_WIN, win_v, sem.at[0])
            cps += _load_packed(wpack_hbm, OFF_WOUT, ROWS_WOUT, wout_v, sem.at[1])
            ulx_ext[0:HALO, :] = jnp.zeros((HALO, W), F32)
            cv_ext[0:HALO, :] = jnp.zeros((HALO, W), F32)
            hcar[...] = jnp.zeros((HALO, W), F32)
            for cp in cps:
                cp.wait()

        mod = modraw_ref[...] + adab_ref[...]
        shift1, scale1, gate1 = mod[0:1], mod[1:2], mod[2:3]
        x = x_ref[...]
        r1 = lax.rsqrt(_rowmean(x * x) + EPS)
        h = (x * r1 * g1_ref[...]) * (1.0 + scale1) + shift1
        proj = _dot_nt(h.astype(BF16), win_v[...])
        proj_ref[...] = proj
        u_lx, u_ly, u_b, u_c, u_v = (proj[:, k * W:(k + 1) * W] for k in range(5))

        ulx_ext[HALO:HALO + tb, :] = u_lx
        xl = bl_ref[...] + wl_ref[CONV_L - 1:CONV_L, :] * u_lx
        for k in range(CONV_L - 1):
            xl = xl + wl_ref[k:k + 1, :] * ulx_ext[pl.ds(HALO - (CONV_L - 1) + k, tb), :]
        ulx_ext[0:HALO, :] = ulx_ext[tb:tb + HALO, :]
        xlb = xl.astype(BF16)
        r = _sigmoid(_dot(xlb, bda_ref[...]) + ba_ref[...])
        ig = _sigmoid(_dot(xlb, bdx_ref[...]) + bxb_ref[...])
        log_a = (-C_GATE) * r * _softplus(ap_ref[...])
        a = jnp.exp(log_a)
        mult = jnp.sqrt(_one_minus_sq(a, log_a))
        grow = i * tb + lax.broadcasted_iota(jnp.int32, (tb, W), 0)
        mult = jnp.where(grow == 0, 1.0, mult)
        a_s[...] = a
        b_s[...] = mult * (ig * xl)
        _scan_groups(tb // HALO, a_s, b_s, hl_ref, hcar, reverse=False)
        hl = hl_ref[...]
        ge, _ = _gelu(u_ly)
        p = ge * hl
        rp = lax.rsqrt(_group_mean(p * p, avg_ref[...]) + EPS)
        y_lru = p * rp * gl_ref[...]

        cv = u_c * u_v
        cv_ext[HALO:HALO + tb, :] = cv
        cc = ws_ref[CONV_S - 1:CONV_S, :] * cv
        for k in range(CONV_S - 1):
            cc = cc + ws_ref[k:k + 1, :] * cv_ext[pl.ds(HALO - (CONV_S - 1) + k, tb), :]
        cv_ext[0:HALO, :] = cv_ext[tb:tb + HALO, :]
        q = u_b * cc
        rq = lax.rsqrt(_group_mean(q * q, avg_ref[...]) + EPS)
        y_conv = q * rq * gc_ref[...]
        for k, kept in enumerate((xl, r, ig, rp, rq, cc)):
            kept_ref[:, k * W:(k + 1) * W] = kept

        mixed_ref[...] = (_dot(y_lru.astype(BF16), wout_v[0:W, :]) + _dot(y_conv.astype(BF16), wout_v[W:2 * W, :]))

        @pl.when(i == nb - 1)
        def _():
            for cp in arrivals[1:]:
                cp.wait_recv()
            for cp in sends + [forward]:
                cp.wait_send()
            own.wait()

    tok = lambda cols: pl.BlockSpec((tb, cols), lambda i: (i, 0))
    full = lambda a: pl.BlockSpec(a.shape, lambda i: (0,) * a.ndim)
    small = (modraw, adab, g1, wl, bl, bda, bdx, ba, bxb, ap, ws, gl, gc, avg)
    n_chips = len(CHIP_FLIPS)
    return pl.pallas_call(
        body,
        name="mixer_fwd",
        grid=(nb,),
        in_specs=[tok(D)] + [full(a) for a in small] + [ANY, ANY],
        out_specs=[tok(D_IN), tok(W), tok(D), tok(N_KEPT * W), ANY],
        out_shape=[jax.ShapeDtypeStruct((t_len, D_IN), F32), jax.ShapeDtypeStruct((t_len, W), F32),
                   jax.ShapeDtypeStruct((t_len, D), F32), jax.ShapeDtypeStruct((t_len, N_KEPT * W), F32),
                   jax.ShapeDtypeStruct((N_DEV,) + mlp_block.shape, BF16)],
        scratch_shapes=[pltpu.VMEM((D_IN, D), BF16), pltpu.VMEM((D, D), BF16), pltpu.SemaphoreType.DMA((2,)),
                        pltpu.VMEM((tb + HALO, W), F32), pltpu.VMEM((tb + HALO, W), F32), pltpu.VMEM((HALO, W), F32),
                        pltpu.VMEM((tb, W), F32), pltpu.VMEM((tb, W), F32),
                        pltpu.SemaphoreType.DMA((n_chips,)), pltpu.SemaphoreType.DMA((n_chips,)), pltpu.SemaphoreType.DMA],
        compiler_params=pltpu.CompilerParams(dimension_semantics=("arbitrary",), vmem_limit_bytes=VMEM_LIMIT),
    )(x, *small, wpack, mlp_block)


def _sibling_forward(wmlp):
    def body(in_hbm, out_hbm, send_sems, recv_sems):
        x, y, c = _position()
        copies, arrivals = [], []
        for j, k in enumerate((0,) + CHIP_FLIPS):
            mine = out_hbm.at[_linear(_flip((x, y, c), k))]
            theirs = out_hbm.at[_linear(_flip((x, y, 1 - c), k))]
            copies.append(pltpu.make_async_remote_copy(
                src_ref=mine, dst_ref=mine, send_sem=send_sems.at[j], recv_sem=recv_sems.at[j],
                device_id=(x, y, 1 - c), device_id_type=MESH))
            arrivals.append(pltpu.make_async_remote_copy(
                src_ref=theirs, dst_ref=theirs, send_sem=send_sems.at[j], recv_sem=recv_sems.at[j],
                device_id=(x, y, 1 - c), device_id_type=MESH))
        for cp in copies:
            cp.start()
        for cp in arrivals:
            cp.wait_recv()
        for cp in copies:
            cp.wait_send()

    return pl.pallas_call(
        body,
        name="sibling_forward",
        in_specs=[ANY],
        out_specs=ANY,
        out_shape=jax.ShapeDtypeStruct(wmlp.shape, wmlp.dtype),
        input_output_aliases={0: 0},
        scratch_shapes=[pltpu.SemaphoreType.DMA((4,)), pltpu.SemaphoreType.DMA((4,))],
    )(wmlp)


def _mlp_fwd(x, mixed, tgt, modraw, adab, g2, gf, wpack):
    t_len = x.shape[0]
    tb = TB_MLP
    nb = t_len // tb

    def body(x_ref, mixed_ref, tgt_ref, modraw_ref, adab_ref, g2_ref, gf_ref, wpack_hbm,
             h2t_ref, f_ref, dx2_ref, dz_ref, vec_ref, loss_ref, w1t_v, w2_v, sem):
        i = pl.program_id(0)

        @pl.when(i == 0)
        def _():
            cps = _load_packed(wpack_hbm, OFF_W1T, ROWS_W1T, w1t_v, sem.at[0])
            cps += _load_packed(wpack_hbm, OFF_W2, ROWS_W2, w2_v, sem.at[1])
            vec_ref[...] = jnp.zeros(vec_ref.shape, F32)
            loss_ref[...] = jnp.zeros(loss_ref.shape, F32)
            for cp in cps:
                cp.wait()

        mod = modraw_ref[...] + adab_ref[...]
        gate1, shift2, scale2, gate2 = mod[2:3], mod[3:4], mod[4:5], mod[5:6]
        x1 = x_ref[...] + gate1 * mixed_ref[...]
        r2 = lax.rsqrt(_rowmean(x1 * x1) + EPS)
        h2 = (x1 * r2 * g2_ref[...]) * (1.0 + scale2) + shift2
        h2b = h2.astype(BF16)
        h2t_ref[...] = h2.T.astype(BF16)
        z = jnp.zeros((tb, D), F32)
        for j in range(N_DEV):
            cols = slice(j * FF_BLK, (j + 1) * FF_BLK)
            fj = _dot_nt(h2b, w1t_v[cols, :])
            f_ref[:, cols] = fj
            rf = jnp.maximum(fj, 0.0)
            z = z + _dot((rf * rf).astype(BF16), w2_v[cols, :])
        x2 = x1 + gate2 * z
        r3 = lax.rsqrt(_rowmean(x2 * x2) + EPS)
        xn3 = x2 * r3
        diff = xn3 * gf_ref[...] - tgt_ref[...]
        sq = _colsum(diff * diff)
        loss_ref[...] += jnp.broadcast_to(jnp.sum(sq, axis=1, keepdims=True) * (0.5 / D), loss_ref.shape)
        dy = diff * (1.0 / D)
        dyn = dy * gf_ref[...]
        dx2 = r3 * (dyn - xn3 * _rowmean(dyn * xn3))
        dx2_ref[...] = dx2
        dz_ref[...] = (gate2 * dx2).astype(BF16)
        vec_ref[0:1, :] += _colsum(dx2 * z)
        vec_ref[1:2, :] += _colsum(dy * xn3)

    tok = lambda cols: pl.BlockSpec((tb, cols), lambda i: (i, 0))
    tok_t = pl.BlockSpec((D, tb), lambda i: (0, i))
    full = lambda a: pl.BlockSpec(a.shape, lambda i: (0,) * a.ndim)
    small = (modraw, adab, g2, gf)
    return pl.pallas_call(
        body,
        name="mlp_fwd",
        grid=(nb,),
        in_specs=[tok(D), tok(D), tok(D)] + [full(a) for a in small] + [ANY],
        out_specs=[tok_t, tok(D_FF), tok(D), tok(D), pl.BlockSpec((8, D), lambda i: (0, 0)),
                   pl.BlockSpec((8, 128), lambda i: (0, 0))],
        out_shape=[jax.ShapeDtypeStruct((D, t_len), BF16), jax.ShapeDtypeStruct((t_len, D_FF), F32),
                   jax.ShapeDtypeStruct((t_len, D), F32), jax.ShapeDtypeStruct((t_len, D), BF16),
                   jax.ShapeDtypeStruct((8, D), F32), jax.ShapeDtypeStruct((8, 128), F32)],
        scratch_shapes=[pltpu.VMEM((D_FF, D), BF16), pltpu.VMEM((D_FF, D), BF16), pltpu.SemaphoreType.DMA((2,))],
        compiler_params=pltpu.CompilerParams(dimension_semantics=("arbitrary",), vmem_limit_bytes=VMEM_LIMIT),
    )(x, mixed, tgt, *small, wpack)


def _mlp_bwd_half(pos, h2t, f, dz, wpack, prior=None):
    t_len = dz.shape[0]
    tb = TB_MLPB
    nb = t_len // tb
    first = prior is None
    flip = 1 if first else 0

    def body(pos_ref, h2t_ref, f_ref, dz_ref, w1t_ref, w2_ref, *rest):
        if first:
            dh2_ref, dw1_ref, dw2_ref = rest
        else:
            dh2in_ref, send1_hbm, send2_hbm, dh2_ref, dw1_ref, dw2_ref, land1_hbm, land2_hbm, send_sems, recv_sems = rest
            x, y, c = _position()
            copies = [pltpu.make_async_remote_copy(
                src_ref=src, dst_ref=dst, send_sem=send_sems.at[a], recv_sem=recv_sems.at[a], device_id=(x, y, 1 - c),
                device_id_type=MESH) for a, (src, dst) in enumerate(((send1_hbm, land1_hbm), (send2_hbm, land2_hbm)))]
        k = pl.program_id(0)
        t = pl.program_id(1)

        if not first:
            @pl.when((k == 0) & (t == 0))
            def _():
                for cp in copies:
                    cp.start()

        rows = pl.ds(pl.multiple_of(t * tb, tb), tb)
        w1t = w1t_ref[0]
        w2 = w2_ref[0]
        dz = dz_ref[...]
        rf = jnp.maximum(f_ref[...], 0.0)
        df = (_dot_nt(dz, w2) * (2.0 * rf)).astype(BF16)
        dh = _dot(df, w1t)
        g1 = _dot(h2t_ref[...], df)
        g2 = _dot_tn((rf * rf).astype(BF16), dz)

        @pl.when(t == 0)
        def _():
            dw2_ref[0] = g2
            dw1_ref[0] = g1

        @pl.when(t != 0)
        def _():
            dw2_ref[0] += g2
            dw1_ref[0] += g1

        @pl.when(k == 0)
        def _():
            dh2_ref[rows, :] = dh if first else dh2in_ref[...] + dh

        @pl.when(k != 0)
        def _():
            dh2_ref[rows, :] += dh

        if not first:
            @pl.when((k == 3) & (t == nb - 1))
            def _():
                for cp in copies:
                    cp.wait_recv()
                for cp in copies:
                    cp.wait_send()

    blk = lambda k, pos: 2 * k + jnp.bitwise_xor(pos[0], flip)
    in_specs = [pl.BlockSpec((D, tb), lambda k, t, pos: (0, t)),
                pl.BlockSpec((tb, FF_BLK), lambda k, t, pos: (t, blk(k, pos))),
                pl.BlockSpec((tb, D), lambda k, t, pos: (t, 0)),
                pl.BlockSpec((1, ROWS_W1T, D), lambda k, t, pos: (blk(k, pos), OFF_W1T // ROWS_W1T, 0)),
                pl.BlockSpec((1, ROWS_W2, D), lambda k, t, pos: (blk(k, pos), OFF_W2 // ROWS_W2, 0))]
    grad_specs = [pl.BlockSpec((1, D, FF_BLK), lambda k, t, pos: (k, 0, 0)),
                  pl.BlockSpec((1, FF_BLK, D), lambda k, t, pos: (k, 0, 0))]
    out_specs = [pl.BlockSpec((t_len, D), lambda k, t, pos: (0, 0))] + grad_specs
    grad_shapes = [jax.ShapeDtypeStruct((4, D, FF_BLK), F32), jax.ShapeDtypeStruct((4, FF_BLK, D), F32)]
    out_shape = [jax.ShapeDtypeStruct((t_len, D), F32)] + grad_shapes
    args = [pos, h2t, f, dz, wpack, wpack]
    scratch = []
    if not first:
        in_specs += [pl.BlockSpec((tb, D), lambda k, t, pos: (jnp.where(k == 0, t, nb - 1), 0)), ANY, ANY]
        out_specs += [ANY, ANY]
        out_shape += grad_shapes
        args += list(prior)
        scratch = [pltpu.SemaphoreType.DMA((2,)), pltpu.SemaphoreType.DMA((2,))]
    return pl.pallas_call(
        body,
        name="mlp_bwd_first" if first else "mlp_bwd_second",
        grid_spec=pltpu.PrefetchScalarGridSpec(num_scalar_prefetch=1, grid=(4, nb), in_specs=in_specs,
                                               out_specs=out_specs, scratch_shapes=scratch),
        out_shape=out_shape,
        compiler_params=pltpu.CompilerParams(dimension_semantics=("arbitrary", "arbitrary"),
                                             vmem_limit_bytes=VMEM_LIMIT),
    )(*args)


V_SHIFT1, V_SCALE1, V_GATE1, V_SHIFT2, V_SCALE2, V_G1, V_G2 = 0, 1, 2, 3, 4, 6, 7
V_BL_BA, V_BX_SP, V_GL_GC, V_WL01, V_WL23, V_WS01, V_WS2 = 8, 9, 10, 11, 12, 13, 14
V_ROWS = 16


def _chip_scatter_copies(srcs, dsts, send_sems, recv_sems, row_ranges=None):
    me = _position()
    copies = []
    for a, (src, dst) in enumerate(zip(srcs, dsts)):
        band = pl.ds(*row_ranges[a]) if row_ranges else slice(None)
        for j, k in enumerate(CHIP_FLIPS):
            peer = _flip(me, k)
            copies.append(pltpu.make_async_remote_copy(
                src_ref=src.at[2 * peer[0] + peer[1], band], dst_ref=dst.at[j, band],
                send_sem=send_sems.at[len(CHIP_FLIPS) * a + j], recv_sem=recv_sems.at[len(CHIP_FLIPS) * a + j],
                device_id=peer, device_id_type=MESH))
    return copies


def _mixer_bwd(x, mixed, dh2, dx2, proj, hl, kept, modraw, adab, g1, g2, wl, bl, bda, bdx, ba, bxb, ap, ws, gl, gc, avg, wpack,
               chip_sums, chip_rows):
    t_len = x.shape[0]
    tb = TB_MIXB
    nb = t_len // tb
    hb = tb // HALO
    n_sums = len(chip_sums)

    def body(x_ref, mixed_ref, dh2_ref, dx2_ref, proj_ref, projh_ref, hl_ref, hlh_ref, kept_ref,
             modraw_ref, adab_ref, g1_ref, g2_ref, wl_ref, bl_ref, bda_ref, bdx_ref, ba_ref, bxb_ref, ap_ref,
             ws_ref, gl_ref, gc_ref, avg_ref, wpack_hbm, *rest):
        sums_hbm, rest = rest[:n_sums], rest[n_sums:]
        gx_ref, vec_ref, hb_ref, dprojt_ref, dmixed_ref, ycatt_ref, xlt_ref, dgate_ref = rest[:8]
        landed_hbm, rest = rest[8:8 + n_sums], rest[8 + n_sums:]
        (win_v, wout_v, sem, ulx_ext, cv_ext, hl_ext, a_ext, dxl_ext, dcc_ext, dcar, an_s, g_s, dh_s,
         send_sems, recv_sems) = rest
        i = pl.program_id(0)
        blk = nb - 1 - i
        chip_copies = _chip_scatter_copies(sums_hbm, landed_hbm, send_sems, recv_sems, chip_rows)

        @pl.when(i == 0)
        def _():
            for cp in chip_copies:
                cp.start()
            cps = _load_packed(wpack_hbm, OFF_WIN, ROWS_WIN, win_v, sem.at[0])
            cps += _load_packed(wpack_hbm, OFF_WOUT, ROWS_WOUT, wout_v, sem.at[1])
            vec_ref[...] = jnp.zeros(vec_ref.shape, F32)
            zero = jnp.zeros((HALO, W), F32)
            a_ext[tb:tb + HALO, :] = zero
            dxl_ext[tb:tb + HALO, :] = zero
            dcc_ext[tb:tb + HALO, :] = zero
            dcar[...] = zero
            for cp in cps:
                cp.wait()

        mod = modraw_ref[...] + adab_ref[...]
        shift1, scale1, gate1, scale2 = mod[0:1], mod[1:2], mod[2:3], mod[4:5]
        x = x_ref[...]
        mixed = mixed_ref[...]

        x1 = x + gate1 * mixed
        r2 = lax.rsqrt(_rowmean(x1 * x1) + EPS)
        xn2 = x1 * r2
        dh2 = dh2_ref[...]
        vec_ref[V_SHIFT2:V_SHIFT2 + 1, :] += _colsum(dh2)
        vec_ref[V_SCALE2:V_SCALE2 + 1, :] += _colsum(dh2 * xn2 * g2_ref[...])
        vec_ref[V_G2:V_G2 + 1, :] += _colsum(dh2 * (1.0 + scale2) * xn2)
        dxn2 = dh2 * g2_ref[...] * (1.0 + scale2)
        dx1 = dx2_ref[...] + r2 * (dxn2 - xn2 * _rowmean(dxn2 * xn2))
        vec_ref[V_GATE1:V_GATE1 + 1, :] += _colsum(dx1 * mixed)
        dmixed = (gate1 * dx1).astype(BF16)

        proj = proj_ref[...]
        u_lx, u_ly, u_b, u_c, u_v = (proj[:, k * W:(k + 1) * W] for k in range(5))
        has_prev = (blk > 0).astype(F32)
        projh = projh_ref[...]
        ulx_ext[0:HALO, :] = projh[:, 0:W] * has_prev
        ulx_ext[HALO:HALO + tb, :] = u_lx
        xl, r, ig, rp, rq, cc = (kept_ref[:, k * W:(k + 1) * W] for k in range(N_KEPT))
        sp = _softplus(ap_ref[...])
        log_a = (-C_GATE) * r * sp
        a = jnp.exp(log_a)
        mult_raw = jnp.sqrt(_one_minus_sq(a, log_a))
        first = (blk * tb + lax.broadcasted_iota(jnp.int32, (tb, W), 0)) == 0
        mult = jnp.where(first, 1.0, mult_raw)
        hl = hl_ref[...]
        ge, th = _gelu(u_ly)
        pn = ge * hl * rp
        cv = u_c * u_v
        cv_ext[0:HALO, :] = projh[:, 3 * W:4 * W] * projh[:, 4 * W:5 * W] * has_prev
        cv_ext[HALO:HALO + tb, :] = cv
        qn = u_b * cc * rq

        dmixed_ref[...] = dmixed
        ycatt_ref[0:W, :] = (pn * gl_ref[...]).T.astype(BF16)
        ycatt_ref[W:2 * W, :] = (qn * gc_ref[...]).T.astype(BF16)
        dyl = _dot_nt(dmixed, wout_v[0:W, :])
        dyc = _dot_nt(dmixed, wout_v[W:2 * W, :])

        dqn = dyc * gc_ref[...]
        dq = rq * (dqn - qn * _group_mean(dqn * qn, avg_ref[...]))
        du_b = dq * cc
        dcc = dq * u_b
        dcc_ext[0:tb, :] = dcc
        dcv = ws_ref[CONV_S - 1:CONV_S, :] * dcc
        for k in range(CONV_S - 1):
            dcv = dcv + ws_ref[k:k + 1, :] * dcc_ext[pl.ds(CONV_S - 1 - k, tb), :]
        dcc_ext[tb:tb + HALO, :] = dcc_ext[0:HALO, :]
        du_c = dcv * u_v
        du_v = dcv * u_c
        dws = [_colsum(dcc * cv_ext[pl.ds(HALO - (CONV_S - 1) + k, tb), :]) for k in range(CONV_S)]

        dpn = dyl * gl_ref[...]
        dp = rp * (dpn - pn * _group_mean(dpn * pn, avg_ref[...]))
        du_ly = dp * hl * _gelu_grad(u_ly, th)
        g_s[...] = dp * ge
        a_ext[0:tb, :] = a
        an_s[...] = a_ext[pl.ds(1, tb), :]
        _scan_groups(hb, an_s, g_s, dh_s, dcar, reverse=True)
        a_ext[tb:tb + HALO, :] = a_ext[0:HALO, :]
        dh = dh_s[...]
        hl_ext[0:HALO, :] = hlh_ref[...] * has_prev
        hl_ext[HALO:HALO + tb, :] = hl
        da = dh * hl_ext[pl.ds(HALO - 1, tb), :]
        dmult = dh * (ig * xl)
        dig = dh * (mult * xl)
        dxl = dh * (mult * ig)
        dlog = da * a - jnp.where(first, 0.0, dmult * (a * a) / mult_raw)
        dr = dlog * ((-C_GATE) * sp)
        dsp = _colsum(dlog * ((-C_GATE) * r))
        dga = dr * r * (1.0 - r)
        dgx = dig * ig * (1.0 - ig)
        dgab = dga.astype(BF16)
        dgxb = dgx.astype(BF16)
        xlt_ref[...] = xl.T.astype(BF16)
        dgate_ref[:, 0:W] = dgab
        dgate_ref[:, W:2 * W] = dgxb
        dxl = dxl + _dot_nt(dgab, bda_ref[...]) + _dot_nt(dgxb, bdx_ref[...])
        dxl_ext[0:tb, :] = dxl
        du_lx = wl_ref[CONV_L - 1:CONV_L, :] * dxl
        for k in range(CONV_L - 1):
            du_lx = du_lx + wl_ref[k:k + 1, :] * dxl_ext[pl.ds(CONV_L - 1 - k, tb), :]
        dxl_ext[tb:tb + HALO, :] = dxl_ext[0:HALO, :]
        dwl = [_colsum(dxl * ulx_ext[pl.ds(HALO - (CONV_L - 1) + k, tb), :]) for k in range(CONV_L)]

        cat = lambda u, v: jnp.concatenate([u, v], axis=1)
        vec_ref[V_BL_BA:V_BL_BA + 1, :] += cat(_colsum(dxl), _colsum(dga))
        vec_ref[V_BX_SP:V_BX_SP + 1, :] += cat(_colsum(dgx), dsp)
        vec_ref[V_GL_GC:V_GL_GC + 1, :] += cat(_colsum(dyl * pn), _colsum(dyc * qn))
        vec_ref[V_WL01:V_WL01 + 1, :] += cat(dwl[0], dwl[1])
        vec_ref[V_WL23:V_WL23 + 1, :] += cat(dwl[2], dwl[3])
        vec_ref[V_WS01:V_WS01 + 1, :] += cat(dws[0], dws[1])
        vec_ref[V_WS2:V_WS2 + 1, 0:W] += dws[2]

        r1 = lax.rsqrt(_rowmean(x * x) + EPS)
        xn1 = x * r1
        hb_ref[...] = ((xn1 * g1_ref[...]) * (1.0 + scale1) + shift1).astype(BF16)
        dh_in = jnp.zeros((tb, D), F32)
        for k, du in enumerate((du_lx, du_ly, du_b, du_c, du_v)):
            dprojt_ref[k * W:(k + 1) * W, :] = du.T.astype(BF16)
            dh_in = dh_in + _dot(du.astype(BF16), win_v[k * W:(k + 1) * W, :])
        vec_ref[V_SHIFT1:V_SHIFT1 + 1, :] += _colsum(dh_in)
        vec_ref[V_SCALE1:V_SCALE1 + 1, :] += _colsum(dh_in * xn1 * g1_ref[...])
        vec_ref[V_G1:V_G1 + 1, :] += _colsum(dh_in * (1.0 + scale1) * xn1)
        dxn1 = dh_in * g1_ref[...] * (1.0 + scale1)
        gx_ref[...] = dx1 + r1 * (dxn1 - xn1 * _rowmean(dxn1 * xn1))

        @pl.when(i == nb - 1)
        def _():
            for cp in chip_copies:
                cp.wait_recv()
            for cp in chip_copies:
                cp.wait_send()

    rev = lambda cols: pl.BlockSpec((tb, cols), lambda i: (nb - 1 - i, 0))
    rev_t = lambda rows: pl.BlockSpec((rows, tb), lambda i: (0, nb - 1 - i))
    halo = lambda cols: pl.BlockSpec((HALO, cols), lambda i: (jnp.maximum((nb - 1 - i) * hb - 1, 0), 0))
    full = lambda a: pl.BlockSpec(a.shape, lambda i: (0,) * a.ndim)
    small = (modraw, adab, g1, g2, wl, bl, bda, bdx, ba, bxb, ap, ws, gl, gc, avg)
    ext = pltpu.VMEM((tb + HALO, W), F32)
    n_sems = len(CHIP_FLIPS) * n_sums
    return pl.pallas_call(
        body,
        name="mixer_bwd",
        grid=(nb,),
        in_specs=[rev(D), rev(D), rev(D), rev(D), rev(D_IN), halo(D_IN), rev(W), halo(W), rev(N_KEPT * W)]
        + [full(a) for a in small] + [ANY] * (1 + n_sums),
        out_specs=[rev(D), pl.BlockSpec((V_ROWS, D), lambda i: (0, 0)), rev(D), rev_t(D_IN), rev(D), rev_t(D),
                   rev_t(W), rev(2 * W)] + [ANY] * n_sums,
        out_shape=[jax.ShapeDtypeStruct((t_len, D), F32), jax.ShapeDtypeStruct((V_ROWS, D), F32),
                   jax.ShapeDtypeStruct((t_len, D), BF16), jax.ShapeDtypeStruct((D_IN, t_len), BF16),
                   jax.ShapeDtypeStruct((t_len, D), BF16), jax.ShapeDtypeStruct((D, t_len), BF16),
                   jax.ShapeDtypeStruct((W, t_len), BF16), jax.ShapeDtypeStruct((t_len, 2 * W), BF16)]
        + [jax.ShapeDtypeStruct((len(CHIP_FLIPS),) + s.shape[1:], s.dtype) for s in chip_sums],
        scratch_shapes=[pltpu.VMEM((D_IN, D), BF16), pltpu.VMEM((D, D), BF16), pltpu.SemaphoreType.DMA((2,)),
                        ext, ext, ext, ext, ext, ext, pltpu.VMEM((HALO, W), F32),
                        pltpu.VMEM((tb, W), F32), pltpu.VMEM((tb, W), F32), pltpu.VMEM((tb, W), F32),
                        pltpu.SemaphoreType.DMA((n_sems,)), pltpu.SemaphoreType.DMA((n_sems,))],
        compiler_params=pltpu.CompilerParams(dimension_semantics=("arbitrary",), vmem_limit_bytes=VMEM_LIMIT),
    )(x, mixed, dh2, dx2, proj, proj, hl, hl, kept, *small, wpack, *chip_sums)


def _matmul(name, a, b, tm=512):
    m, k = a.shape
    n = b.shape[1]

    def body(a_ref, b_ref, o_ref):
        o_ref[...] = _dot(a_ref[...], b_ref[...])

    return pl.pallas_call(
        body,
        name=name,
        grid=(m // tm,),
        in_specs=[pl.BlockSpec((tm, k), lambda i: (i, 0)), pl.BlockSpec((k, n), lambda i: (0, 0))],
        out_specs=pl.BlockSpec((tm, n), lambda i: (i, 0)),
        out_shape=jax.ShapeDtypeStruct((m, n), F32),
        compiler_params=pltpu.CompilerParams(dimension_semantics=("arbitrary",), vmem_limit_bytes=VMEM_LIMIT),
    )(a, b)


def _matmul_beside_chips(name, a, b, chip_sum, landed, rows, tm=512):
    m, k = a.shape
    n = b.shape[1]
    steps = m // tm

    def body(a_ref, b_ref, src_hbm, landed_in, o_ref, landed_hbm, send_sems, recv_sems):
        i = pl.program_id(0)
        copies = _chip_scatter_copies([src_hbm], [landed_hbm], send_sems, recv_sems, [rows])

        @pl.when(i == 0)
        def _():
            for cp in copies:
                cp.start()

        o_ref[...] = _dot(a_ref[...], b_ref[...])

        @pl.when(i == steps - 1)
        def _():
            for cp in copies:
                cp.wait_recv()
            for cp in copies:
                cp.wait_send()

    n_sems = len(CHIP_FLIPS)
    return pl.pallas_call(
        body,
        name=name,
        grid=(steps,),
        in_specs=[pl.BlockSpec((tm, k), lambda i: (i, 0)), pl.BlockSpec((k, n), lambda i: (0, 0)), ANY, ANY],
        out_specs=[pl.BlockSpec((tm, n), lambda i: (i, 0)), ANY],
        out_shape=[jax.ShapeDtypeStruct((m, n), F32), jax.ShapeDtypeStruct(landed.shape, landed.dtype)],
        input_output_aliases={3: 1},
        scratch_shapes=[pltpu.SemaphoreType.DMA((n_sems,)), pltpu.SemaphoreType.DMA((n_sems,))],
        compiler_params=pltpu.CompilerParams(dimension_semantics=("arbitrary",), vmem_limit_bytes=VMEM_LIMIT),
    )(a, b, chip_sum, landed)


def _gate_wgrad(xl_t, dgate, avg):
    hd = W // 8

    def body(a_ref, b_ref, avg_ref, o_ref):
        full = _dot(a_ref[...], b_ref[...])
        row = lax.broadcasted_iota(jnp.int32, (W, hd), 0)
        col = lax.broadcasted_iota(jnp.int32, (W, hd), 1)
        fold = ((row & (hd - 1)) == col).astype(BF16)
        keep = avg_ref[...] != 0
        for g in range(2):
            m = jnp.where(keep, full[:, g * W:(g + 1) * W], 0.0)
            hi = m.astype(BF16)
            rest = m - hi.astype(F32)
            mid = rest.astype(BF16)
            lo = (rest - mid.astype(F32)).astype(BF16)
            o_ref[g] = _dot(hi, fold) + _dot(mid, fold) + _dot(lo, fold)

    return pl.pallas_call(
        body,
        name="wgrad_gate",
        in_specs=[WHOLE] * 3,
        out_specs=WHOLE,
        out_shape=jax.ShapeDtypeStruct((2, W, hd), F32),
        compiler_params=pltpu.CompilerParams(vmem_limit_bytes=VMEM_LIMIT),
    )(xl_t, dgate, avg)


def _block_diag(w):
    n, m, _ = w.shape
    eye = jnp.eye(n, dtype=w.dtype)
    return (w[:, :, None, :] * eye[:, None, :, None]).reshape(n * m, n * m)


def _pad_rows(a, rows):
    return jnp.pad(a, ((0, rows - a.shape[0]),) + ((0, 0),) * (a.ndim - 1))


def _position():
    return lax.axis_index("x"), lax.axis_index("y"), lax.axis_index("c")


def _linear(pos):
    return 4 * pos[0] + 2 * pos[1] + pos[2]


def _flip(pos, k):
    return tuple(1 - p if k & bit else p for p, bit in zip(pos, (4, 2, 1)))


def _exchange_all(make_copy, make_arrival):
    copies = [make_copy(k) for k in range(1, N_DEV)]
    for cp in copies:
        cp.start()
    for k in range(1, N_DEV):
        make_arrival(k).wait_recv()
    for cp in copies:
        cp.wait_send()


def _mod_exchange_steps(cols):
    def steps(msg_ref, adaw_ref, gath_ref, mod_ref, sendbuf, send_a, recv_a, send_b, recv_b):
        me = _position()
        me_lin = _linear(me)
        m = msg_ref[...]
        row = lax.broadcasted_iota(jnp.int32, m.shape, 0)
        gath_ref[me_lin] = jnp.where(row == 0, m * _sigmoid(m), m)

        def gather_copy(k, src_lin):
            return pltpu.make_async_remote_copy(
                src_ref=gath_ref.at[src_lin], dst_ref=gath_ref.at[src_lin], send_sem=send_a.at[k - 1],
                recv_sem=recv_a.at[k - 1], device_id=_flip(me, k), device_id_type=MESH)

        _exchange_all(lambda k: gather_copy(k, me_lin), lambda k: gather_copy(k, _linear(_flip(me, k))))

        sc_all = gath_ref[:, 0, :]
        scb = jnp.concatenate([sc_all, jnp.zeros_like(sc_all)], axis=0).astype(BF16)
        prod = _dot(scb, adaw_ref[...].astype(BF16))
        for b in range(N_DEV):
            sendbuf[b] = jnp.broadcast_to(prod[b:b + 1, :], (HALO, cols))
        mod_ref[me_lin] = sendbuf[me_lin]

        def row_copy(k, dst_lin):
            peer = _flip(me, k)
            return pltpu.make_async_remote_copy(
                src_ref=sendbuf.at[_linear(peer)], dst_ref=mod_ref.at[dst_lin], send_sem=send_b.at[k - 1],
                recv_sem=recv_b.at[k - 1], device_id=peer, device_id_type=MESH)

        _exchange_all(lambda k: row_copy(k, me_lin), lambda k: row_copy(k, _linear(_flip(me, k))))

    return steps


def _gather_and_mod(msg, ada_w, block):
    rows, cols = block.shape
    mod_cols = ada_w.shape[1]
    mod_steps = _mod_exchange_steps(mod_cols)

    def body(msg_ref, adaw_ref, x_ref, gath_ref, mod_ref, out_ref, sendbuf, send_a, recv_a, send_b, recv_b,
             send_sems, recv_sems, sib_send_sems, sib_recv_sems, local_sem):
        x, y, c = _position()
        me, sibling = (x, y, c), (x, y, 1 - c)
        sends, forward, arrivals = _chip_gather_copies(x_ref, out_ref, send_sems, recv_sems)

        def to_sibling(j, block_of, src=None):
            dst = out_ref.at[_linear(block_of)]
            return pltpu.make_async_remote_copy(
                src_ref=dst if src is None else src, dst_ref=dst, send_sem=sib_send_sems.at[j],
                recv_sem=sib_recv_sems.at[j], device_id=sibling, device_id_type=MESH)

        mine = pltpu.make_async_copy(x_ref, out_ref.at[_linear(me)], local_sem)
        mine.start()
        passes = [to_sibling(0, me, src=x_ref)] + [to_sibling(1 + j, p) for j, p in enumerate(_route_peers(me))]
        passes[0].start()
        for cp in sends:
            cp.start()
        mod_steps(msg_ref, adaw_ref, gath_ref, mod_ref, sendbuf, send_a, recv_a, send_b, recv_b)
        arrivals[0].wait_recv()
        forward.start()
        passes[1].start()
        arrivals[1].wait_recv()
        passes[2].start()
        arrivals[2].wait_recv()
        passes[3].start()
        for j, p in enumerate((sibling,) + _route_peers(sibling)):
            to_sibling(j, p).wait_recv()
        for cp in sends + [forward] + passes:
            cp.wait_send()
        mine.wait()

    return pl.pallas_call(
        body,
        name="gather_and_mod",
        in_specs=[WHOLE, WHOLE, ANY],
        out_specs=[WHOLE, WHOLE, ANY],
        out_shape=[jax.ShapeDtypeStruct((N_DEV, HALO, D), F32), jax.ShapeDtypeStruct((N_DEV, HALO, mod_cols), F32),
                   jax.ShapeDtypeStruct((N_DEV, rows, cols), block.dtype)],
        scratch_shapes=[pltpu.VMEM((N_DEV, HALO, mod_cols), F32)] + [pltpu.SemaphoreType.DMA((N_DEV - 1,))] * 4
        + [pltpu.SemaphoreType.DMA((3,)), pltpu.SemaphoreType.DMA((3,)), pltpu.SemaphoreType.DMA((4,)),
           pltpu.SemaphoreType.DMA((4,)), pltpu.SemaphoreType.DMA],
        compiler_params=pltpu.CompilerParams(vmem_limit_bytes=VMEM_LIMIT),
    )(msg, ada_w, block)


def _sibling_copies(srcs, dsts, send_sems, recv_sems):
    x, y, c = _position()
    copies = []
    for a, (src, dst) in enumerate(zip(srcs, dsts)):
        for k in range(4):
            copies.append(pltpu.make_async_remote_copy(
                src_ref=src.at[k, 1 - c], dst_ref=dst.at[k], send_sem=send_sems.at[4 * a + k],
                recv_sem=recv_sems.at[4 * a + k], device_id=(x, y, 1 - c), device_id_type=MESH))
    return copies


def _row_block(rows):
    return 256 if rows % 256 == 0 else rows // 2


def _pair_sum(pos, mine, recv):
    _, cores, rows, cols = mine.shape
    rb = _row_block(rows)

    def body(pos_ref, mine_ref, recv_ref, out_ref):
        out_ref[0] = (mine_ref[0, 0] + recv_ref[0]).astype(BF16)

    other = lambda k, pos: jnp.bitwise_xor(pos[1], k + 1)
    core = lambda pos: pos[0] * (cores - 1)
    return pl.pallas_call(
        body,
        name="grad_pair_sum",
        grid_spec=pltpu.PrefetchScalarGridSpec(
            num_scalar_prefetch=1, grid=(3, rows // rb),
            in_specs=[pl.BlockSpec((1, 1, rb, cols), lambda k, r, pos: (other(k, pos), core(pos), r, 0)),
                      pl.BlockSpec((1, rb, cols), lambda k, r, pos: (other(k, pos), r, 0))],
            out_specs=pl.BlockSpec((1, rb, cols), lambda k, r, pos: (other(k, pos), r, 0))),
        out_shape=jax.ShapeDtypeStruct((4, rows, cols), BF16),
        compiler_params=pltpu.CompilerParams(dimension_semantics=("arbitrary", "arbitrary")),
    )(pos, mine, recv)


def _final_sum(pos, mine, recv, chips):
    _, cores, rows, cols = mine.shape
    rb = _row_block(rows)

    def body(pos_ref, mine_ref, recv_ref, chips_ref, out_ref):
        g = mine_ref[0, 0] + recv_ref[0]
        for j in range(3):
            g = g + chips_ref[j].astype(F32)
        out_ref[...] = g

    return pl.pallas_call(
        body,
        name="grad_final_sum",
        grid_spec=pltpu.PrefetchScalarGridSpec(
            num_scalar_prefetch=1, grid=(rows // rb,),
            in_specs=[pl.BlockSpec((1, 1, rb, cols), lambda r, pos: (pos[1], pos[0] * (cores - 1), r, 0)),
                      pl.BlockSpec((1, rb, cols), lambda r, pos: (pos[1], r, 0)),
                      pl.BlockSpec((3, rb, cols), lambda r, pos: (0, r, 0))],
            out_specs=pl.BlockSpec((rb, cols), lambda r, pos: (r, 0))),
        out_shape=jax.ShapeDtypeStruct((rows, cols), F32),
        compiler_params=pltpu.CompilerParams(dimension_semantics=("arbitrary",)),
    )(pos, mine, recv, chips)


LOSS_ROW = V_ROWS + 8
GB_BASE = LOSS_ROW + 8


def _sibling_and_route(gmod8, sc_t, parts):
    cols = gmod8.shape[1]
    n = len(parts)

    def body(gmod_ref, sct_ref, *refs):
        srcs, (gadaw_ref, gb_ref), dsts = refs[:n], refs[n:n + 2], refs[n + 2:2 * n + 2]
        sendbuf, grecv, send_a, recv_a, sib_send, sib_recv = refs[2 * n + 2:]
        sib_copies = _sibling_copies(srcs, dsts, sib_send, sib_recv)
        for cp in sib_copies:
            cp.start()
        me = _position()
        me_lin = _linear(me)
        gm = gmod_ref[...]
        for b in range(N_DEV):
            sendbuf[b] = jnp.broadcast_to(gm[b:b + 1, :], (HALO, cols))
        grecv[me_lin] = sendbuf[me_lin]

        def row_copy(k, dst_lin):
            peer = _flip(me, k)
            return pltpu.make_async_remote_copy(
                src_ref=sendbuf.at[_linear(peer)], dst_ref=grecv.at[dst_lin], send_sem=send_a.at[k - 1],
                recv_sem=recv_a.at[k - 1], device_id=peer, device_id_type=MESH)

        _exchange_all(lambda k: row_copy(k, me_lin), lambda k: row_copy(k, _linear(_flip(me, k))))
        g_all = grecv[:, 0, :]
        g_pad = jnp.concatenate([g_all, jnp.zeros((sct_ref.shape[1] - N_DEV, cols), F32)], axis=0).astype(BF16)
        gadaw_ref[...] = _dot(sct_ref[...], g_pad)
        gb_ref[...] = jnp.broadcast_to(_colsum(g_all), (HALO, cols))
        for cp in sib_copies:
            cp.wait_recv()
        for cp in sib_copies:
            cp.wait_send()

    return pl.pallas_call(
        body,
        name="sibling_and_route",
        in_specs=[WHOLE, WHOLE] + [ANY] * n,
        out_specs=[WHOLE, WHOLE] + [ANY] * n,
        out_shape=[jax.ShapeDtypeStruct((D, cols), F32), jax.ShapeDtypeStruct((HALO, cols), F32)]
        + [jax.ShapeDtypeStruct((4,) + p.shape[2:], p.dtype) for p in parts],
        scratch_shapes=[pltpu.VMEM((N_DEV, HALO, cols), F32), pltpu.VMEM((N_DEV, HALO, cols), F32),
                        pltpu.SemaphoreType.DMA((N_DEV - 1,)), pltpu.SemaphoreType.DMA((N_DEV - 1,)),
                        pltpu.SemaphoreType.DMA((4 * n,)), pltpu.SemaphoreType.DMA((4 * n,))],
        compiler_params=pltpu.CompilerParams(vmem_limit_bytes=VMEM_LIMIT),
    )(gmod8, sc_t, *parts)


def _chips_and_gather(msg_vec, msg_gate, gb_rows, chip_sums):
    cols = gb_rows.shape[1]
    n = len(chip_sums)
    vec_rows = GB_BASE + N_DEV

    def body(vec_ref, gate_ref, gb_ref, *refs):
        srcs, (sumv_ref, sumg_ref), dsts = refs[:n], refs[n:n + 2], refs[n + 2:2 * n + 2]
        (myv, myg, sibv, sibg, chipv, chipg, sib_send, sib_recv, peer_send, peer_recv,
         chip_send, chip_recv) = refs[2 * n + 2:]
        chip_copies = _chip_scatter_copies(srcs, dsts, chip_send, chip_recv)
        for cp in chip_copies:
            cp.start()
        x, y, c = me = _position()
        my_chip = 2 * x + y
        myv[0:GB_BASE, :] = vec_ref[...]
        slot = lax.broadcasted_iota(jnp.int32, (N_DEV, D), 0) == _linear(me)
        gb_wide = jnp.concatenate([gb_ref[...], jnp.zeros((N_DEV, D - cols), F32)], axis=1)
        myv[GB_BASE:vec_rows, :] = jnp.where(slot, gb_wide, 0.0)
        myg[...] = gate_ref[...]

        swaps = [pltpu.make_async_remote_copy(
            src_ref=src, dst_ref=dst, send_sem=sib_send.at[a], recv_sem=sib_recv.at[a], device_id=(x, y, 1 - c),
            device_id_type=MESH) for a, (src, dst) in enumerate(((myv, sibv), (myg, sibg)))]
        for cp in swaps:
            cp.start()
        for cp in swaps:
            cp.wait_recv()
        chipv[my_chip] = myv[...] + sibv[...]
        chipg[my_chip] = myg[...] + sibg[...]

        def chip_copy(a, buf, j, k, slot_chip):
            peer = _flip(me, k)
            return pltpu.make_async_remote_copy(
                src_ref=buf.at[slot_chip], dst_ref=buf.at[slot_chip], send_sem=peer_send.at[3 * a + j],
                recv_sem=peer_recv.at[3 * a + j], device_id=peer, device_id_type=MESH)

        sends = [chip_copy(a, buf, j, k, my_chip) for a, buf in enumerate((chipv, chipg)) for j, k in enumerate(CHIP_FLIPS)]
        for cp in sends:
            cp.start()
        for a, buf in enumerate((chipv, chipg)):
            for j, k in enumerate(CHIP_FLIPS):
                peer = _flip(me, k)
                chip_copy(a, buf, j, k, 2 * peer[0] + peer[1]).wait_recv()
        sumv_ref[...] = ((chipv[0] + chipv[1]) + chipv[2]) + chipv[3]
        sumg_ref[...] = ((chipg[0] + chipg[1]) + chipg[2]) + chipg[3]
        for cp in swaps + sends:
            cp.wait_send()
        for cp in chip_copies:
            cp.wait_recv()
        for cp in chip_copies:
            cp.wait_send()

    n_chip = max(len(CHIP_FLIPS) * n, 1)
    vshape, gshape = (vec_rows, D), msg_gate.shape
    return pl.pallas_call(
        body,
        name="chips_and_gather",
        in_specs=[WHOLE] * 3 + [ANY] * n,
        out_specs=[WHOLE] * 2 + [ANY] * n,
        out_shape=[jax.ShapeDtypeStruct(vshape, F32), jax.ShapeDtypeStruct(gshape, F32)]
        + [jax.ShapeDtypeStruct((len(CHIP_FLIPS),) + s.shape[1:], s.dtype) for s in chip_sums],
        scratch_shapes=[pltpu.VMEM(vshape, F32), pltpu.VMEM(gshape, F32), pltpu.VMEM(vshape, F32),
                        pltpu.VMEM(gshape, F32), pltpu.VMEM((4,) + vshape, F32), pltpu.VMEM((4,) + gshape, F32),
                        pltpu.SemaphoreType.DMA((2,)), pltpu.SemaphoreType.DMA((2,)),
                        pltpu.SemaphoreType.DMA((2 * len(CHIP_FLIPS),)), pltpu.SemaphoreType.DMA((2 * len(CHIP_FLIPS),)),
                        pltpu.SemaphoreType.DMA((n_chip,)), pltpu.SemaphoreType.DMA((n_chip,))],
        compiler_params=pltpu.CompilerParams(vmem_limit_bytes=VMEM_LIMIT),
    )(msg_vec, msg_gate, gb_rows, *chip_sums)


def _adamw_math(w, g, m, v):
    m = ADAM_B1 * m + (1.0 - ADAM_B1) * g
    v = ADAM_B2 * v + (1.0 - ADAM_B2) * (g * g)
    m_hat = m / (1.0 - ADAM_B1 ** ADAM_STEP)
    v_hat = v / (1.0 - ADAM_B2 ** ADAM_STEP)
    delta = -ADAM_LR * (m_hat / (jnp.sqrt(v_hat) + ADAM_EPS) + ADAM_WD * w)
    return delta, m, v


def _adamw(name, w, g, m, v):
    rows, cols = w.shape
    rb = 256 if rows % 256 == 0 else rows

    def body(w_ref, g_ref, m_ref, v_ref, d_ref, mo_ref, vo_ref):
        d_ref[...], mo_ref[...], vo_ref[...] = _adamw_math(w_ref[...], g_ref[...], m_ref[...], v_ref[...])

    spec = pl.BlockSpec((rb, cols), lambda r: (r, 0))
    return pl.pallas_call(
        body,
        name="adamw_" + name,
        grid=(rows // rb,),
        in_specs=[spec] * 4,
        out_specs=[spec] * 3,
        out_shape=[jax.ShapeDtypeStruct((rows, cols), F32)] * 3,
        compiler_params=pltpu.CompilerParams(dimension_semantics=("arbitrary",)),
    )(w, g, m, v)


def _update_beside_chips(pos, sum_jobs, plain_jobs, chip_sums):
    rb = 256
    jobs = [("sum", j) for j in sum_jobs] + [("plain", j) for j in plain_jobs]
    offs, total = [], 0
    for _, j in jobs:
        offs.append(total)
        total += j[-1].shape[0] // rb
    n_chip = len(chip_sums)
    n_in = sum(len(j) for _, j in jobs)
    n_out = 4 * len(sum_jobs) + 3 * len(plain_jobs)

    def body(pos_ref, *refs):
        ins, srcs = refs[:n_in], refs[n_in:n_in + n_chip]
        outs = refs[n_in + n_chip:n_in + n_chip + n_out]
        dsts = refs[n_in + n_chip + n_out:n_in + 2 * n_chip + n_out]
        send_sems, recv_sems = refs[n_in + 2 * n_chip + n_out:]
        s = pl.program_id(0)
        copies = _chip_scatter_copies(srcs, dsts, send_sems, recv_sems)

        @pl.when(s == 0)
        def _():
            for cp in copies:
                cp.start()

        i_in = i_out = 0
        for (kind, j), off in zip(jobs, offs):
            steps = j[-1].shape[0] // rb
            j_in = ins[i_in:i_in + len(j)]
            i_in += len(j)
            j_out = outs[i_out:i_out + (4 if kind == "sum" else 3)]
            i_out += len(j_out)

            @pl.when((s >= off) & (s < off + steps))
            def _(kind=kind, j_in=j_in, j_out=j_out):
                if kind == "sum":
                    mine_ref, recv_ref, chips_ref, w_ref, m_ref, v_ref = j_in
                    g = mine_ref[0, 0] + recv_ref[0]
                    for q in range(len(CHIP_FLIPS)):
                        g = g + chips_ref[q].astype(F32)
                    j_out[0][...] = g
                    rest = j_out[1:]
                else:
                    g_ref, w_ref, m_ref, v_ref = j_in
                    g = g_ref[...]
                    rest = j_out
                rest[0][...], rest[1][...], rest[2][...] = _adamw_math(w_ref[...], g, m_ref[...], v_ref[...])

        @pl.when(s == total - 1)
        def _():
            for cp in copies:
                cp.wait_recv()
            for cp in copies:
                cp.wait_send()

    in_specs, out_specs, out_shape, args = [], [], [], []
    for (kind, j), off in zip(jobs, offs):
        rows, cols = j[-1].shape
        steps = rows // rb
        blk = lambda s, off=off, steps=steps: jnp.clip(s - off, 0, steps - 1)
        flat = pl.BlockSpec((rb, cols), lambda s, pos, blk=blk: (blk(s), 0))
        if kind == "sum":
            in_specs += [pl.BlockSpec((1, 1, rb, cols), lambda s, pos, blk=blk: (pos[1], 0, blk(s), 0)),
                         pl.BlockSpec((1, rb, cols), lambda s, pos, blk=blk: (pos[1], blk(s), 0)),
                         pl.BlockSpec((len(CHIP_FLIPS), rb, cols), lambda s, pos, blk=blk: (0, blk(s), 0))]
            in_specs += [flat] * 3
        else:
            in_specs += [flat] * 4
        n_res = 4 if kind == "sum" else 3
        out_specs += [flat] * n_res
        out_shape += [jax.ShapeDtypeStruct((rows, cols), F32)] * n_res
        args += list(j)
    in_specs += [ANY] * n_chip
    out_specs += [ANY] * n_chip
    out_shape += [jax.ShapeDtypeStruct((len(CHIP_FLIPS),) + c.shape[1:], c.dtype) for c in chip_sums]
    n_sems = len(CHIP_FLIPS) * n_chip
    outs = pl.pallas_call(
        body,
        name="update_beside_chips",
        grid_spec=pltpu.PrefetchScalarGridSpec(
            num_scalar_prefetch=1, grid=(total,), in_specs=in_specs, out_specs=out_specs,
            scratch_shapes=[pltpu.SemaphoreType.DMA((n_sems,)), pltpu.SemaphoreType.DMA((n_sems,))]),
        out_shape=out_shape,
        compiler_params=pltpu.CompilerParams(dimension_semantics=("arbitrary",), vmem_limit_bytes=VMEM_LIMIT),
    )(pos, *args, *chip_sums)
    sums = [tuple(outs[4 * i:4 * i + 4]) for i in range(len(sum_jobs))]
    base = 4 * len(sum_jobs)
    plains = [tuple(outs[base + 3 * i:base + 3 * i + 3]) for i in range(len(plain_jobs))]
    return sums, plains, list(outs[n_out:])


def _adamw_small(ws, gs, ms, vs, sigmoid_scaled):
    n = len(ws)

    def body(*refs):
        w_refs, g_refs, m_refs, v_refs = (refs[i * n:(i + 1) * n] for i in range(4))
        outs = refs[4 * n:]
        for i in range(n):
            w = w_refs[i][...]
            g = g_refs[i][...]
            if sigmoid_scaled[i]:
                g = g * _sigmoid(w)
            delta, m, v = _adamw_math(w, g, m_refs[i][...], v_refs[i][...])
            outs[4 * i][...] = g
            outs[4 * i + 1][...] = delta
            outs[4 * i + 2][...] = m
            outs[4 * i + 3][...] = v

    shapes = [jax.ShapeDtypeStruct(w.shape, F32) for w in ws for _ in range(4)]
    outs = pl.pallas_call(
        body,
        name="adamw_small",
        in_specs=[WHOLE] * (4 * n),
        out_specs=[WHOLE] * (4 * n),
        out_shape=shapes,
    )(*ws, *gs, *ms, *vs)
    return [outs[4 * i:4 * i + 4] for i in range(n)]


_WEIGHT_NAMES = ("ada_w", "ada_b", "norm1_g", "w_in", "lru_conv_w", "lru_conv_b", "gate_a_w", "gate_a_b", "gate_x_w",
                 "gate_x_b", "a_param", "short_conv_w", "lru_out_g", "conv_out_g", "w_out", "norm2_g", "w_mlp1",
                 "w_mlp2", "final_g")


def kernel(x, c, ada_w, ada_b, norm1_g, w_in, lru_conv_w, lru_conv_b, gate_a_w, gate_a_b, gate_x_w, gate_x_b, a_param, short_conv_w, lru_out_g, conv_out_g, w_out, norm2_g, w_mlp1, w_mlp2, final_g, loss_target, m_ada_w, m_ada_b, m_norm1_g, m_w_in, m_lru_conv_w, m_lru_conv_b, m_gate_a_w, m_gate_a_b, m_gate_x_w, m_gate_x_b, m_a_param, m_short_conv_w, m_lru_out_g, m_conv_out_g, m_w_out, m_norm2_g, m_w_mlp1, m_w_mlp2, m_final_g, v_ada_w, v_ada_b, v_norm1_g, v_w_in, v_lru_conv_w, v_lru_conv_b, v_gate_a_w, v_gate_a_b, v_gate_x_w, v_gate_x_b, v_a_param, v_short_conv_w, v_lru_out_g, v_conv_out_g, v_w_out, v_norm2_g, v_w_mlp1, v_w_mlp2, v_final_g):
    given = dict(locals())
    weights = {n: given[n] for n in _WEIGHT_NAMES}
    xi, yi, ci = _position()
    me_lin = _linear((xi, yi, ci))
    hd = W // N_DEV

    mixer_block = jnp.concatenate([w_out[0], w_in[0].T], axis=0).astype(BF16)
    mlp_block = jnp.concatenate([w_mlp1[0].T, w_mlp2[0]], axis=0).astype(BF16)

    msg = (jnp.pad(c, ((0, HALO - 1), (0, 0)))
           + jnp.pad(lru_conv_w[0], ((1, HALO - 1 - CONV_L), (0, D - hd)))
           + jnp.pad(short_conv_w[0], ((1 + CONV_L, 0), (0, D - hd))))
    gath, mod_all, wmix = _gather_and_mod(msg, ada_w[0], mixer_block)
    sc_all = gath[:, 0, :]
    wl = jnp.transpose(gath[:, 1:1 + CONV_L, :hd], (1, 0, 2)).reshape(CONV_L, W)
    ws = jnp.transpose(gath[:, 1 + CONV_L:HALO, :hd], (1, 0, 2)).reshape(CONV_S, W)
    modraw = _pad_rows(mod_all[:, 0, :].reshape(6, D), HALO)
    adab = _pad_rows(ada_b.reshape(6, D), HALO)

    x2d, tgt = x[0], loss_target[0]
    gf = final_g.reshape(1, D)
    bda = _block_diag(gate_a_w[0]).astype(BF16)
    bdx = _block_diag(gate_x_w[0]).astype(BF16)
    avg = _block_diag(jnp.full((8, W // 8, W // 8), 8.0 / W, F32)).astype(BF16)
    wl8 = _pad_rows(wl, HALO)
    ws8 = _pad_rows(ws, HALO)
    mixer_small = (wl8, lru_conv_b, bda, bdx, gate_a_b, gate_x_b, a_param, ws8, lru_out_g, conv_out_g, avg)
    proj, hl, mixed, kept, wmlp = _mixer_fwd(x2d, modraw, adab, norm1_g, *mixer_small, wmix, mlp_block)
    wmlp = _sibling_forward(wmlp)
    h2t, f, dx2, dz, vec2, loss8 = _mlp_fwd(x2d, mixed, tgt, modraw, adab, norm2_g, gf, wmlp)
    pos = jnp.stack([ci, 2 * xi + yi]).astype(jnp.int32)
    by_dest = lambda g: g.reshape((4, 2, -1) + g.shape[-1:])
    dh2, dw1, dw2, sib1, sib2 = _mlp_bwd_half(pos, h2t, f, dz, wmlp, prior=_mlp_bwd_half(pos, h2t, f, dz, wmlp))
    mlp_parts = [dw1[:, None], dw2[:, None]]
    mlp_sib = [sib1, sib2]
    mlp_sums = [_pair_sum(pos, p, r) for p, r in zip(mlp_parts, mlp_sib)]
    gx, vec, hb, dproj_t, dmixed, ycat_t, xl_t, dgate, *mlp_chips = _mixer_bwd(
        x2d, mixed, dh2, dx2, proj, hl, kept, modraw, adab, norm1_g, norm2_g, *mixer_small, wmix, mlp_sums,
        [(0, D), (0, FF_BLK // 2)])
    dwint, mlp_chips[1] = _matmul_beside_chips("wgrad_in", dproj_t, hb, mlp_sums[1], mlp_chips[1],
                                               (FF_BLK // 2, FF_BLK // 2))
    dwout = _matmul("wgrad_out", ycat_t, dmixed)
    msg_gate = _gate_wgrad(xl_t, dgate, avg).reshape(W, 128)
    mix_parts = [by_dest(dwout), by_dest(dwint)]
    gmod8 = (jnp.pad(vec[0:5], ((0, 1), (0, 0))) + jnp.pad(vec2[0:1], ((5, 0), (0, 0)))).reshape(N_DEV, 6 * D // N_DEV)
    sc_t = jnp.pad(sc_all.T, ((0, 0), (0, 128 - N_DEV))).astype(BF16)
    g_adaw, gb_rows, *mix_sib = _sibling_and_route(gmod8, sc_t, mix_parts)
    mix_sums = [_pair_sum(pos, p, r) for p, r in zip(mix_parts, mix_sib)]
    loss_rows = jnp.pad(loss8[0:1], ((0, HALO - 1), (0, D - loss8.shape[1])))
    msg_vec = jnp.concatenate([vec, vec2, loss_rows], axis=0)
    state = lambda n: (weights[n][0], given["m_" + n][0], given["v_" + n][0])
    mlp_jobs = [(p, r, q, *state(n)) for p, r, q, n in zip(mlp_parts, mlp_sib, mlp_chips, ("w_mlp1", "w_mlp2"))]
    mlp_done, (adaw_done,), mix_chips = _update_beside_chips(pos, mlp_jobs, [(g_adaw, *state("ada_w"))], mix_sums)
    sum_vec, sum_gate = _chips_and_gather(msg_vec, msg_gate, gb_rows, [])
    g_wout, g_wint = (_final_sum(pos, p, r, q) for p, r, q in zip(mix_parts, mix_sib, mix_chips))
    loss = sum_vec[LOSS_ROW, 0]
    sum_gate = sum_gate.reshape(2, W, W // 8)
    lo, hi = slice(0, W), slice(W, 2 * W)
    wl_full = sum_vec[V_WL01:V_WL23 + 1].reshape(CONV_L, W)
    ws_full = sum_vec[V_WS01:V_WS2 + 1].reshape(CONV_S + 1, W)[:CONV_S]
    row = lambda r, cols: sum_vec[r:r + 1, cols]
    small_grads = {
        "ada_b": sum_vec[GB_BASE:GB_BASE + N_DEV, :6 * D // N_DEV].reshape(1, 6 * D),
        "norm1_g": row(V_G1, slice(0, D)),
        "lru_conv_w": lax.dynamic_slice(wl_full, (0, me_lin * hd), (CONV_L, hd)),
        "lru_conv_b": row(V_BL_BA, lo),
        "gate_a_w": sum_gate[0],
        "gate_a_b": row(V_BL_BA, hi),
        "gate_x_w": sum_gate[1],
        "gate_x_b": row(V_BX_SP, lo),
        "a_param": row(V_BX_SP, hi),
        "short_conv_w": lax.dynamic_slice(ws_full, (0, me_lin * hd), (CONV_S, hd)),
        "lru_out_g": row(V_GL_GC, lo),
        "conv_out_g": row(V_GL_GC, hi),
        "norm2_g": row(V_G2, slice(0, D)),
        "final_g": sum_vec[V_ROWS + 1:V_ROWS + 2, :],
    }
    names = list(small_grads)
    as2d = lambda a, n: a.reshape(small_grads[n].shape)
    small = _adamw_small([as2d(weights[n], n) for n in names], [small_grads[n] for n in names],
                         [as2d(given["m_" + n], n) for n in names], [as2d(given["v_" + n], n) for n in names],
                         [n == "a_param" for n in names])
    result = {n: tuple(o.reshape(weights[n].shape) for o in outs) for n, outs in zip(names, small)}

    for n, g in (("w_in", g_wint.T), ("w_out", g_wout)):
        w, m, v = state(n)
        result[n] = (g[None],) + tuple(o[None] for o in _adamw(n, w, g, m, v))
    result["w_mlp1"], result["w_mlp2"] = (tuple(o[None] for o in done) for done in mlp_done)
    result["ada_w"] = (g_adaw[None],) + tuple(o[None] for o in adaw_done)

    return (loss, gx[None], *[result[n][0] for n in _WEIGHT_NAMES], *[result[n][1] for n in _WEIGHT_NAMES],
            *[result[n][2] for n in _WEIGHT_NAMES], *[result[n][3] for n in _WEIGHT_NAMES])
```

```python
import functools

import jax
import jax.numpy as jnp
from jax import lax
from jax.experimental import pallas as pl
from jax.experimental.pallas import tpu as pltpu

F32 = jnp.float32
BF16 = jnp.bfloat16
MESH = pl.DeviceIdType.MESH

N_DEV = 8
D = 1024
W = 512
D_IN = 5 * W
D_FF = 4096
FF_BLK = D_FF // N_DEV
EPS = 1e-6
C_GATE = 8.0
CONV_L = 4
CONV_S = 3
HALO = 8

ROWS_W1T, ROWS_W2, ROWS_WOUT, ROWS_WIN = FF_BLK, FF_BLK, D // N_DEV, D_IN // N_DEV
OFF_WOUT = 0
OFF_WIN = OFF_WOUT + ROWS_WOUT
MIX_ROWS = OFF_WIN + ROWS_WIN
OFF_W1T = 0
OFF_W2 = OFF_W1T + ROWS_W1T
MLP_ROWS = OFF_W2 + ROWS_W2
CHIP_FLIPS = (4, 2, 6)
N_KEPT = 6

ADAM_LR = 0.001
ADAM_B1 = 0.9
ADAM_B2 = 0.999
ADAM_EPS = 1e-08
ADAM_WD = 0.01
ADAM_STEP = 10

VMEM_LIMIT = 56 * 1024 * 1024

TB_MIX = 256
TB_MIXB = 256
TB_MLP = 256
TB_MLPB = 512

ANY = pl.BlockSpec(memory_space=pl.ANY)
WHOLE = pl.BlockSpec(memory_space=pltpu.VMEM)


def _dot(a, b):
    return jnp.dot(a, b, preferred_element_type=F32)


def _dot_nt(a, b):
    return lax.dot_general(a, b, (((1,), (1,)), ((), ())), preferred_element_type=F32)


def _dot_tn(a, b):
    return lax.dot_general(a, b, (((0,), (0,)), ((), ())), preferred_element_type=F32)


def _sigmoid(v):
    return 1.0 / (1.0 + jnp.exp(-v))


def _softplus(v):
    t = jnp.exp(-jnp.abs(v))
    small = t * (1.0 - t * (0.5 - t * (1.0 / 3.0)))
    return jnp.maximum(v, 0.0) + jnp.where(t < 1e-2, small, jnp.log(1.0 + t))


def _one_minus_sq(a, log_a):
    return -jnp.tanh(log_a) * (a * a + 1.0)


_GELU_K = 0.7978845608028654
_GELU_C = 0.044715


def _gelu(u):
    th = jnp.tanh(_GELU_K * (u + _GELU_C * u * u * u))
    return 0.5 * u * (1.0 + th), th


def _gelu_grad(u, th):
    return 0.5 * (1.0 + th) + 0.5 * u * (1.0 - th * th) * _GELU_K * (1.0 + 3.0 * _GELU_C * u * u)


def _group_mean(v, avg):
    hi = v.astype(BF16)
    lo = (v - hi.astype(F32)).astype(BF16)
    return _dot(hi, avg) + _dot(lo, avg)


def _colsum(v):
    return jnp.sum(v, axis=0, keepdims=True)


def _rowmean(v):
    return jnp.mean(v, axis=-1, keepdims=True)


def _load_packed(wpack_hbm, off, rows, dst, sem):
    copies = [
        pltpu.make_async_copy(wpack_hbm.at[d, pl.ds(off, rows), :], dst.at[pl.ds(d * rows, rows), :], sem)
        for d in range(N_DEV)
    ]
    for cp in copies:
        cp.start()
    return copies


def _scan_groups(n_groups, a_ref, b_ref, out_ref, carry_ref, reverse):
    row = lax.broadcasted_iota(jnp.int32, (HALO, W), 0)

    def step(k, carry):
        g = (n_groups - 1 - k) if reverse else k
        rows = pl.ds(pl.multiple_of(g * HALO, HALO), HALO)
        a = a_ref[rows, :]
        b = b_ref[rows, :]
        for s in (1, 2, 4):
            if reverse:
                keep = row < HALO - s
                sh = HALO - s
            else:
                keep = row >= s
                sh = s
            a_sh = pltpu.roll(a, sh, axis=0)
            b_sh = pltpu.roll(b, sh, axis=0)
            b = jnp.where(keep, a * b_sh + b, b)
            a = jnp.where(keep, a * a_sh, a)
        h = b + a * carry
        out_ref[rows, :] = h
        edge = h[0:1, :] if reverse else h[HALO - 1:HALO, :]
        return jnp.broadcast_to(edge, (HALO, W))

    carry_ref[...] = lax.fori_loop(0, n_groups, step, carry_ref[...])


def _route_peers(me):
    x, y, c = me
    first = ((x + 1 - c) % 2, (y + c) % 2, c)
    second = ((x + c) % 2, (y + 1 - c) % 2, c)
    return first, second, (1 - x, 1 - y, c)


def _chip_gather_copies(block_hbm, out_hbm, send_sems, recv_sems):
    me = _position()
    first, second, diag = _route_peers(me)

    def copy(j, src, slot_of, to):
        return pltpu.make_async_remote_copy(
            src_ref=src, dst_ref=out_hbm.at[_linear(slot_of)], send_sem=send_sems.at[j], recv_sem=recv_sems.at[j],
            device_id=to, device_id_type=MESH)

    own_sends = [copy(0, block_hbm, me, first), copy(1, block_hbm, me, second)]
    forward = copy(2, out_hbm.at[_linear(first)], first, second)
    arrivals = [copy(0, block_hbm, first, first), copy(1, block_hbm, second, second), copy(2, block_hbm, diag, second)]
    return own_sends, forward, arrivals


def _mixer_fwd(x, modraw, adab, g1, wl, bl, bda, bdx, ba, bxb, ap, ws, gl, gc, avg, wpack, mlp_block):
    t_len = x.shape[0]
    tb = TB_MIX
    nb = t_len // tb

    def body(x_ref, modraw_ref, adab_ref, g1_ref, wl_ref, bl_ref, bda_ref, bdx_ref, ba_ref, bxb_ref, ap_ref,
             ws_ref, gl_ref, gc_ref, avg_ref, wpack_hbm, block_hbm, proj_ref, hl_ref, mixed_ref, kept_ref, wmlp_hbm,
             win_v, wout_v, sem, ulx_ext, cv_ext, hcar, a_s, b_s, send_sems, recv_sems, local_sem):
        i = pl.program_id(0)
        own = pltpu.make_async_copy(block_hbm, wmlp_hbm.at[_linear(_position())], local_sem)
        sends, forward, arrivals = _chip_gather_copies(block_hbm, wmlp_hbm, send_sems, recv_sems)

        @pl.when(i == 0)
        def _():
            own.start()
            for cp in sends:
                cp.start()

        @pl.when(i == nb - 1)
        def _():
            arrivals[0].wait_recv()
            forward.start()

        @pl.when(i == 0)
        def _():
            cps = _load_packed(wpack_hbm, OFF_WIN, ROWS_WIN, win_v, sem.at[0])
            cps += _load_packed(wpack_hbm, OFF_WOUT, ROWS_WOUT, wout_v, sem.at[1])
            ulx_ext[0:HALO, :] = jnp.zeros((HALO, W), F32)
            cv_ext[0:HALO, :] = jnp.zeros((HALO, W), F32)
            hcar[...] = jnp.zeros((HALO, W), F32)
            for cp in cps:
                cp.wait()

        mod = modraw_ref[...] + adab_ref[...]
        shift1, scale1, gate1 = mod[0:1], mod[1:2], mod[2:3]
        x = x_ref[...]
        r1 = lax.rsqrt(_rowmean(x * x) + EPS)
        h = (x * r1 * g1_ref[...]) * (1.0 + scale1) + shift1
        proj = _dot_nt(h.astype(BF16), win_v[...])
        proj_ref[...] = proj
        u_lx, u_ly, u_b, u_c, u_v = (proj[:, k * W:(k + 1) * W] for k in range(5))

        ulx_ext[HALO:HALO + tb, :] = u_lx
        xl = bl_ref[...] + wl_ref[CONV_L - 1:CONV_L, :] * u_lx
        for k in range(CONV_L - 1):
            xl = xl + wl_ref[k:k + 1, :] * ulx_ext[pl.ds(HALO - (CONV_L - 1) + k, tb), :]
        ulx_ext[0:HALO, :] = ulx_ext[tb:tb + HALO, :]
        xlb = xl.astype(BF16)
        r = _sigmoid(_dot(xlb, bda_ref[...]) + ba_ref[...])
        ig = _sigmoid(_dot(xlb, bdx_ref[...]) + bxb_ref[...])
        log_a = (-C_GATE) * r * _softplus(ap_ref[...])
        a = jnp.exp(log_a)
        mult = jnp.sqrt(_one_minus_sq(a, log_a))
        grow = i * tb + lax.broadcasted_iota(jnp.int32, (tb, W), 0)
        mult = jnp.where(grow == 0, 1.0, mult)
        a_s[...] = a
        b_s[...] = mult * (ig * xl)
        _scan_groups(tb // HALO, a_s, b_s, hl_ref, hcar, reverse=False)
        hl = hl_ref[...]
        ge, _ = _gelu(u_ly)
        p = ge * hl
        rp = lax.rsqrt(_group_mean(p * p, avg_ref[...]) + EPS)
        y_lru = p * rp * gl_ref[...]

        cv = u_c * u_v
        cv_ext[HALO:HALO + tb, :] = cv
        cc = ws_ref[CONV_S - 1:CONV_S, :] * cv
        for k in range(CONV_S - 1):
            cc = cc + ws_ref[k:k + 1, :] * cv_ext[pl.ds(HALO - (CONV_S - 1) + k, tb), :]
        cv_ext[0:HALO, :] = cv_ext[tb:tb + HALO, :]
        q = u_b * cc
        rq = lax.rsqrt(_group_mean(q * q, avg_ref[...]) + EPS)
        y_conv = q * rq * gc_ref[...]
        for k, kept in enumerate((xl, r, ig, rp, rq, cc)):
            kept_ref[:, k * W:(k + 1) * W] = kept

        mixed_ref[...] = (_dot(y_lru.astype(BF16), wout_v[0:W, :]) + _dot(y_conv.astype(BF16), wout_v[W:2 * W, :]))

        @pl.when(i == nb - 1)
        def _():
            for cp in arrivals[1:]:
                cp.wait_recv()
            for cp in sends + [forward]:
                cp.wait_send()
            own.wait()

    tok = lambda cols: pl.BlockSpec((tb, cols), lambda i: (i, 0))
    full = lambda a: pl.BlockSpec(a.shape, lambda i: (0,) * a.ndim)
    small = (modraw, adab, g1, wl, bl, bda, bdx, ba, bxb, ap, ws, gl, gc, avg)
    n_chips = len(CHIP_FLIPS)
    return pl.pallas_call(
        body,
        name="mixer_fwd",
        grid=(nb,),
        in_specs=[tok(D)] + [full(a) for a in small] + [ANY, ANY],
        out_specs=[tok(D_IN), tok(W), tok(D), tok(N_KEPT * W), ANY],
        out_shape=[jax.ShapeDtypeStruct((t_len, D_IN), F32), jax.ShapeDtypeStruct((t_len, W), F32),
                   jax.ShapeDtypeStruct((t_len, D), F32), jax.ShapeDtypeStruct((t_len, N_KEPT * W), F32),
                   jax.ShapeDtypeStruct((N_DEV,) + mlp_block.shape, BF16)],
        scratch_shapes=[pltpu.VMEM((D_IN, D), BF16), pltpu.VMEM((D, D), BF16), pltpu.SemaphoreType.DMA((2,)),
                        pltpu.VMEM((tb + HALO, W), F32), pltpu.VMEM((tb + HALO, W), F32), pltpu.VMEM((HALO, W), F32),
                        pltpu.VMEM((tb, W), F32), pltpu.VMEM((tb, W), F32),
                        pltpu.SemaphoreType.DMA((n_chips,)), pltpu.SemaphoreType.DMA((n_chips,)), pltpu.SemaphoreType.DMA],
        compiler_params=pltpu.CompilerParams(dimension_semantics=("arbitrary",), vmem_limit_bytes=VMEM_LIMIT),
    )(x, *small, wpack, mlp_block)


def _sibling_forward(wmlp):
    def body(in_hbm, out_hbm, send_sems, recv_sems):
        x, y, c = _position()
        copies, arrivals = [], []
        for j, k in enumerate((0,) + CHIP_FLIPS):
            mine = out_hbm.at[_linear(_flip((x, y, c), k))]
            theirs = out_hbm.at[_linear(_flip((x, y, 1 - c), k))]
            copies.append(pltpu.make_async_remote_copy(
                src_ref=mine, dst_ref=mine, send_sem=send_sems.at[j], recv_sem=recv_sems.at[j],
                device_id=(x, y, 1 - c), device_id_type=MESH))
            arrivals.append(pltpu.make_async_remote_copy(
                src_ref=theirs, dst_ref=theirs, send_sem=send_sems.at[j], recv_sem=recv_sems.at[j],
                device_id=(x, y, 1 - c), device_id_type=MESH))
        for cp in copies:
            cp.start()
        for cp in arrivals:
            cp.wait_recv()
        for cp in copies:
            cp.wait_send()

    return pl.pallas_call(
        body,
        name="sibling_forward",
        in_specs=[ANY],
        out_specs=ANY,
        out_shape=jax.ShapeDtypeStruct(wmlp.shape, wmlp.dtype),
        input_output_aliases={0: 0},
        scratch_shapes=[pltpu.SemaphoreType.DMA((4,)), pltpu.SemaphoreType.DMA((4,))],
    )(wmlp)


def _mlp_fwd(x, mixed, tgt, modraw, adab, g2, gf, wpack):
    t_len = x.shape[0]
    tb = TB_MLP
    nb = t_len // tb

    def body(x_ref, mixed_ref, tgt_ref, modraw_ref, adab_ref, g2_ref, gf_ref, wpack_hbm,
             h2t_ref, f_ref, dx2_ref, dz_ref, vec_ref, loss_ref, w1t_v, w2_v, sem):
        i = pl.program_id(0)

        @pl.when(i == 0)
        def _():
            cps = _load_packed(wpack_hbm, OFF_W1T, ROWS_W1T, w1t_v, sem.at[0])
            cps += _load_packed(wpack_hbm, OFF_W2, ROWS_W2, w2_v, sem.at[1])
            vec_ref[...] = jnp.zeros(vec_ref.shape, F32)
            loss_ref[...] = jnp.zeros(loss_ref.shape, F32)
            for cp in cps:
                cp.wait()

        mod = modraw_ref[...] + adab_ref[...]
        gate1, shift2, scale2, gate2 = mod[2:3], mod[3:4], mod[4:5], mod[5:6]
        x1 = x_ref[...] + gate1 * mixed_ref[...]
        r2 = lax.rsqrt(_rowmean(x1 * x1) + EPS)
        h2 = (x1 * r2 * g2_ref[...]) * (1.0 + scale2) + shift2
        h2b = h2.astype(BF16)
        h2t_ref[...] = h2.T.astype(BF16)
        z = jnp.zeros((tb, D), F32)
        for j in range(N_DEV):
            cols = slice(j * FF_BLK, (j + 1) * FF_BLK)
            fj = _dot_nt(h2b, w1t_v[cols, :])
            f_ref[:, cols] = fj
            rf = jnp.maximum(fj, 0.0)
            z = z + _dot((rf * rf).astype(BF16), w2_v[cols, :])
        x2 = x1 + gate2 * z
        r3 = lax.rsqrt(_rowmean(x2 * x2) + EPS)
        xn3 = x2 * r3
        diff = xn3 * gf_ref[...] - tgt_ref[...]
        sq = _colsum(diff * diff)
        loss_ref[...] += jnp.broadcast_to(jnp.sum(sq, axis=1, keepdims=True) * (0.5 / D), loss_ref.shape)
        dy = diff * (1.0 / D)
        dyn = dy * gf_ref[...]
        dx2 = r3 * (dyn - xn3 * _rowmean(dyn * xn3))
        dx2_ref[...] = dx2
        dz_ref[...] = (gate2 * dx2).astype(BF16)
        vec_ref[0:1, :] += _colsum(dx2 * z)
        vec_ref[1:2, :] += _colsum(dy * xn3)

    tok = lambda cols: pl.BlockSpec((tb, cols), lambda i: (i, 0))
    tok_t = pl.BlockSpec((D, tb), lambda i: (0, i))
    full = lambda a: pl.BlockSpec(a.shape, lambda i: (0,) * a.ndim)
    small = (modraw, adab, g2, gf)
    return pl.pallas_call(
        body,
        name="mlp_fwd",
        grid=(nb,),
        in_specs=[tok(D), tok(D), tok(D)] + [full(a) for a in small] + [ANY],
        out_specs=[tok_t, tok(D_FF), tok(D), tok(D), pl.BlockSpec((8, D), lambda i: (0, 0)),
                   pl.BlockSpec((8, 128), lambda i: (0, 0))],
        out_shape=[jax.ShapeDtypeStruct((D, t_len), BF16), jax.ShapeDtypeStruct((t_len, D_FF), F32),
                   jax.ShapeDtypeStruct((t_len, D), F32), jax.ShapeDtypeStruct((t_len, D), BF16),
                   jax.ShapeDtypeStruct((8, D), F32), jax.ShapeDtypeStruct((8, 128), F32)],
        scratch_shapes=[pltpu.VMEM((D_FF, D), BF16), pltpu.VMEM((D_FF, D), BF16), pltpu.SemaphoreType.DMA((2,))],
        compiler_params=pltpu.CompilerParams(dimension_semantics=("arbitrary",), vmem_limit_bytes=VMEM_LIMIT),
    )(x, mixed, tgt, *small, wpack)


def _mlp_bwd_half(pos, h2t, f, dz, wpack, prior=None):
    t_len = dz.shape[0]
    tb = TB_MLPB
    nb = t_len // tb
    first = prior is None
    flip = 1 if first else 0

    def body(pos_ref, h2t_ref, f_ref, dz_ref, w1t_ref, w2_ref, *rest):
        if first:
            dh2_ref, dw1_ref, dw2_ref = rest
        else:
            dh2in_ref, send1_hbm, send2_hbm, dh2_ref, dw1_ref, dw2_ref, land1_hbm, land2_hbm, send_sems, recv_sems = rest
            x, y, c = _position()
            copies = [pltpu.make_async_remote_copy(
                src_ref=src, dst_ref=dst, send_sem=send_sems.at[a], recv_sem=recv_sems.at[a], device_id=(x, y, 1 - c),
                device_id_type=MESH) for a, (src, dst) in enumerate(((send1_hbm, land1_hbm), (send2_hbm, land2_hbm)))]
        k = pl.program_id(0)
        t = pl.program_id(1)

        if not first:
            @pl.when((k == 0) & (t == 0))
            def _():
                for cp in copies:
                    cp.start()

        rows = pl.ds(pl.multiple_of(t * tb, tb), tb)
        w1t = w1t_ref[0]
        w2 = w2_ref[0]
        dz = dz_ref[...]
        rf = jnp.maximum(f_ref[...], 0.0)
        df = (_dot_nt(dz, w2) * (2.0 * rf)).astype(BF16)
        dh = _dot(df, w1t)
        g1 = _dot(h2t_ref[...], df)
        g2 = _dot_tn((rf * rf).astype(BF16), dz)

        @pl.when(t == 0)
        def _():
            dw2_ref[0] = g2
            dw1_ref[0] = g1

        @pl.when(t != 0)
        def _():
            dw2_ref[0] += g2
            dw1_ref[0] += g1

        @pl.when(k == 0)
        def _():
            dh2_ref[rows, :] = dh if first else dh2in_ref[...] + dh

        @pl.when(k != 0)
        def _():
            dh2_ref[rows, :] += dh

        if not first:
            @pl.when((k == 3) & (t == nb - 1))
            def _():
                for cp in copies:
                    cp.wait_recv()
                for cp in copies:
                    cp.wait_send()

    blk = lambda k, pos: 2 * k + jnp.bitwise_xor(pos[0], flip)
    in_specs = [pl.BlockSpec((D, tb), lambda k, t, pos: (0, t)),
                pl.BlockSpec((tb, FF_BLK), lambda k, t, pos: (t, blk(k, pos))),
                pl.BlockSpec((tb, D), lambda k, t, pos: (t, 0)),
                pl.BlockSpec((1, ROWS_W1T, D), lambda k, t, pos: (blk(k, pos), OFF_W1T // ROWS_W1T, 0)),
                pl.BlockSpec((1, ROWS_W2, D), lambda k, t, pos: (blk(k, pos), OFF_W2 // ROWS_W2, 0))]
    grad_specs = [pl.BlockSpec((1, D, FF_BLK), lambda k, t, pos: (k, 0, 0)),
                  pl.BlockSpec((1, FF_BLK, D), lambda k, t, pos: (k, 0, 0))]
    out_specs = [pl.BlockSpec((t_len, D), lambda k, t, pos: (0, 0))] + grad_specs
    grad_shapes = [jax.ShapeDtypeStruct((4, D, FF_BLK), F32), jax.ShapeDtypeStruct((4, FF_BLK, D), F32)]
    out_shape = [jax.ShapeDtypeStruct((t_len, D), F32)] + grad_shapes
    args = [pos, h2t, f, dz, wpack, wpack]
    scratch = []
    if not first:
        in_specs += [pl.BlockSpec((tb, D), lambda k, t, pos: (jnp.where(k == 0, t, nb - 1), 0)), ANY, ANY]
        out_specs += [ANY, ANY]
        out_shape += grad_shapes
        args += list(prior)
        scratch = [pltpu.SemaphoreType.DMA((2,)), pltpu.SemaphoreType.DMA((2,))]
    return pl.pallas_call(
        body,
        name="mlp_bwd_first" if first else "mlp_bwd_second",
        grid_spec=pltpu.PrefetchScalarGridSpec(num_scalar_prefetch=1, grid=(4, nb), in_specs=in_specs,
                                               out_specs=out_specs, scratch_shapes=scratch),
        out_shape=out_shape,
        compiler_params=pltpu.CompilerParams(dimension_semantics=("arbitrary", "arbitrary"),
                                             vmem_limit_bytes=VMEM_LIMIT),
    )(*args)


V_SHIFT1, V_SCALE1, V_GATE1, V_SHIFT2, V_SCALE2, V_G1, V_G2 = 0, 1, 2, 3, 4, 6, 7
V_BL_BA, V_BX_SP, V_GL_GC, V_WL01, V_WL23, V_WS01, V_WS2 = 8, 9, 10, 11, 12, 13, 14
V_ROWS = 16


def _chip_scatter_copies(srcs, dsts, send_sems, recv_sems, row_ranges=None):
    me = _position()
    copies = []
    for a, (src, dst) in enumerate(zip(srcs, dsts)):
        band = pl.ds(*row_ranges[a]) if row_ranges else slice(None)
        for j, k in enumerate(CHIP_FLIPS):
            peer = _flip(me, k)
            copies.append(pltpu.make_async_remote_copy(
                src_ref=src.at[2 * peer[0] + peer[1], band], dst_ref=dst.at[j, band],
                send_sem=send_sems.at[len(CHIP_FLIPS) * a + j], recv_sem=recv_sems.at[len(CHIP_FLIPS) * a + j],
                device_id=peer, device_id_type=MESH))
    return copies


def _mixer_bwd(x, mixed, dh2, dx2, proj, hl, kept, modraw, adab, g1, g2, wl, bl, bda, bdx, ba, bxb, ap, ws, gl, gc, avg, wpack,
               chip_sums, chip_rows):
    t_len = x.shape[0]
    tb = TB_MIXB
    nb = t_len // tb
    hb = tb // HALO
    n_sums = len(chip_sums)

    def body(x_ref, mixed_ref, dh2_ref, dx2_ref, proj_ref, projh_ref, hl_ref, hlh_ref, kept_ref,
             modraw_ref, adab_ref, g1_ref, g2_ref, wl_ref, bl_ref, bda_ref, bdx_ref, ba_ref, bxb_ref, ap_ref,
             ws_ref, gl_ref, gc_ref, avg_ref, wpack_hbm, *rest):
        sums_hbm, rest = rest[:n_sums], rest[n_sums:]
        gx_ref, vec_ref, hb_ref, dprojt_ref, dmixed_ref, ycatt_ref, xlt_ref, dgate_ref = rest[:8]
        landed_hbm, rest = rest[8:8 + n_sums], rest[8 + n_sums:]
        (win_v, wout_v, sem, ulx_ext, cv_ext, hl_ext, a_ext, dxl_ext, dcc_ext, dcar, an_s, g_s, dh_s,
         send_sems, recv_sems) = rest
        i = pl.program_id(0)
        blk = nb - 1 - i
        chip_copies = _chip_scatter_copies(sums_hbm, landed_hbm, send_sems, recv_sems, chip_rows)

        @pl.when(i == 0)
        def _():
            for cp in chip_copies:
                cp.start()
            cps = _load_packed(wpack_hbm, OFF_WIN, ROWS_WIN, win_v, sem.at[0])
            cps += _load_packed(wpack_hbm, OFF_WOUT, ROWS_WOUT, wout_v, sem.at[1])
            vec_ref[...] = jnp.zeros(vec_ref.shape, F32)
            zero = jnp.zeros((HALO, W), F32)
            a_ext[tb:tb + HALO, :] = zero
            dxl_ext[tb:tb + HALO, :] = zero
            dcc_ext[tb:tb + HALO, :] = zero
            dcar[...] = zero
            for cp in cps:
                cp.wait()

        mod = modraw_ref[...] + adab_ref[...]
        shift1, scale1, gate1, scale2 = mod[0:1], mod[1:2], mod[2:3], mod[4:5]
        x = x_ref[...]
        mixed = mixed_ref[...]

        x1 = x + gate1 * mixed
        r2 = lax.rsqrt(_rowmean(x1 * x1) + EPS)
        xn2 = x1 * r2
        dh2 = dh2_ref[...]
        vec_ref[V_SHIFT2:V_SHIFT2 + 1, :] += _colsum(dh2)
        vec_ref[V_SCALE2:V_SCALE2 + 1, :] += _colsum(dh2 * xn2 * g2_ref[...])
        vec_ref[V_G2:V_G2 + 1, :] += _colsum(dh2 * (1.0 + scale2) * xn2)
        dxn2 = dh2 * g2_ref[...] * (1.0 + scale2)
        dx1 = dx2_ref[...] + r2 * (dxn2 - xn2 * _rowmean(dxn2 * xn2))
        vec_ref[V_GATE1:V_GATE1 + 1, :] += _colsum(dx1 * mixed)
        dmixed = (gate1 * dx1).astype(BF16)

        proj = proj_ref[...]
        u_lx, u_ly, u_b, u_c, u_v = (proj[:, k * W:(k + 1) * W] for k in range(5))
        has_prev = (blk > 0).astype(F32)
        projh = projh_ref[...]
        ulx_ext[0:HALO, :] = projh[:, 0:W] * has_prev
        ulx_ext[HALO:HALO + tb, :] = u_lx
        xl, r, ig, rp, rq, cc = (kept_ref[:, k * W:(k + 1) * W] for k in range(N_KEPT))
        sp = _softplus(ap_ref[...])
        log_a = (-C_GATE) * r * sp
        a = jnp.exp(log_a)
        mult_raw = jnp.sqrt(_one_minus_sq(a, log_a))
        first = (blk * tb + lax.broadcasted_iota(jnp.int32, (tb, W), 0)) == 0
        mult = jnp.where(first, 1.0, mult_raw)
        hl = hl_ref[...]
        ge, th = _gelu(u_ly)
        pn = ge * hl * rp
        cv = u_c * u_v
        cv_ext[0:HALO, :] = projh[:, 3 * W:4 * W] * projh[:, 4 * W:5 * W] * has_prev
        cv_ext[HALO:HALO + tb, :] = cv
        qn = u_b * cc * rq

        dmixed_ref[...] = dmixed
        ycatt_ref[0:W, :] = (pn * gl_ref[...]).T.astype(BF16)
        ycatt_ref[W:2 * W, :] = (qn * gc_ref[...]).T.astype(BF16)
        dyl = _dot_nt(dmixed, wout_v[0:W, :])
        dyc = _dot_nt(dmixed, wout_v[W:2 * W, :])

        dqn = dyc * gc_ref[...]
        dq = rq * (dqn - qn * _group_mean(dqn * qn, avg_ref[...]))
        du_b = dq * cc
        dcc = dq * u_b
        dcc_ext[0:tb, :] = dcc
        dcv = ws_ref[CONV_S - 1:CONV_S, :] * dcc
        for k in range(CONV_S - 1):
            dcv = dcv + ws_ref[k:k + 1, :] * dcc_ext[pl.ds(CONV_S - 1 - k, tb), :]
        dcc_ext[tb:tb + HALO, :] = dcc_ext[0:HALO, :]
        du_c = dcv * u_v
        du_v = dcv * u_c
        dws = [_colsum(dcc * cv_ext[pl.ds(HALO - (CONV_S - 1) + k, tb), :]) for k in range(CONV_S)]

        dpn = dyl * gl_ref[...]
        dp = rp * (dpn - pn * _group_mean(dpn * pn, avg_ref[...]))
        du_ly = dp * hl * _gelu_grad(u_ly, th)
        g_s[...] = dp * ge
        a_ext[0:tb, :] = a
        an_s[...] = a_ext[pl.ds(1, tb), :]
        _scan_groups(hb, an_s, g_s, dh_s, dcar, reverse=True)
        a_ext[tb:tb + HALO, :] = a_ext[0:HALO, :]
        dh = dh_s[...]
        hl_ext[0:HALO, :] = hlh_ref[...] * has_prev
        hl_ext[HALO:HALO + tb, :] = hl
        da = dh * hl_ext[pl.ds(HALO - 1, tb), :]
        dmult = dh * (ig * xl)
        dig = dh * (mult * xl)
        dxl = dh * (mult * ig)
        dlog = da * a - jnp.where(first, 0.0, dmult * (a * a) / mult_raw)
        dr = dlog * ((-C_GATE) * sp)
        dsp = _colsum(dlog * ((-C_GATE) * r))
        dga = dr * r * (1.0 - r)
        dgx = dig * ig * (1.0 - ig)
        dgab = dga.astype(BF16)
        dgxb = dgx.astype(BF16)
        xlt_ref[...] = xl.T.astype(BF16)
        dgate_ref[:, 0:W] = dgab
        dgate_ref[:, W:2 * W] = dgxb
        dxl = dxl + _dot_nt(dgab, bda_ref[...]) + _dot_nt(dgxb, bdx_ref[...])
        dxl_ext[0:tb, :] = dxl
        du_lx = wl_ref[CONV_L - 1:CONV_L, :] * dxl
        for k in range(CONV_L - 1):
            du_lx = du_lx + wl_ref[k:k + 1, :] * dxl_ext[pl.ds(CONV_L - 1 - k, tb), :]
        dxl_ext[tb:tb + HALO, :] = dxl_ext[0:HALO, :]
        dwl = [_colsum(dxl * ulx_ext[pl.ds(HALO - (CONV_L - 1) + k, tb), :]) for k in range(CONV_L)]

        cat = lambda u, v: jnp.concatenate([u, v], axis=1)
        vec_ref[V_BL_BA:V_BL_BA + 1, :] += cat(_colsum(dxl), _colsum(dga))
        vec_ref[V_BX_SP:V_BX_SP + 1, :] += cat(_colsum(dgx), dsp)
        vec_ref[V_GL_GC:V_GL_GC + 1, :] += cat(_colsum(dyl * pn), _colsum(dyc * qn))
        vec_ref[V_WL01:V_WL01 + 1, :] += cat(dwl[0], dwl[1])
        vec_ref[V_WL23:V_WL23 + 1, :] += cat(dwl[2], dwl[3])
        vec_ref[V_WS01:V_WS01 + 1, :] += cat(dws[0], dws[1])
        vec_ref[V_WS2:V_WS2 + 1, 0:W] += dws[2]

        r1 = lax.rsqrt(_rowmean(x * x) + EPS)
        xn1 = x * r1
        hb_ref[...] = ((xn1 * g1_ref[...]) * (1.0 + scale1) + shift1).astype(BF16)
        dh_in = jnp.zeros((tb, D), F32)
        for k, du in enumerate((du_lx, du_ly, du_b, du_c, du_v)):
            dprojt_ref[k * W:(k + 1) * W, :] = du.T.astype(BF16)
            dh_in = dh_in + _dot(du.astype(BF16), win_v[k * W:(k + 1) * W, :])
        vec_ref[V_SHIFT1:V_SHIFT1 + 1, :] += _colsum(dh_in)
        vec_ref[V_SCALE1:V_SCALE1 + 1, :] += _colsum(dh_in * xn1 * g1_ref[...])
        vec_ref[V_G1:V_G1 + 1, :] += _colsum(dh_in * (1.0 + scale1) * xn1)
        dxn1 = dh_in * g1_ref[...] * (1.0 + scale1)
        gx_ref[...] = dx1 + r1 * (dxn1 - xn1 * _rowmean(dxn1 * xn1))

        @pl.when(i == nb - 1)
        def _():
            for cp in chip_copies:
                cp.wait_recv()
            for cp in chip_copies:
                cp.wait_send()

    rev = lambda cols: pl.BlockSpec((tb, cols), lambda i: (nb - 1 - i, 0))
    rev_t = lambda rows: pl.BlockSpec((rows, tb), lambda i: (0, nb - 1 - i))
    halo = lambda cols: pl.BlockSpec((HALO, cols), lambda i: (jnp.maximum((nb - 1 - i) * hb - 1, 0), 0))
    full = lambda a: pl.BlockSpec(a.shape, lambda i: (0,) * a.ndim)
    small = (modraw, adab, g1, g2, wl, bl, bda, bdx, ba, bxb, ap, ws, gl, gc, avg)
    ext = pltpu.VMEM((tb + HALO, W), F32)
    n_sems = max(len(CHIP_FLIPS) * n_sums, 1)
    return pl.pallas_call(
        body,
        name="mixer_bwd",
        grid=(nb,),
        in_specs=[rev(D), rev(D), rev(D), rev(D), rev(D_IN), halo(D_IN), rev(W), halo(W), rev(N_KEPT * W)]
        + [full(a) for a in small] + [ANY] * (1 + n_sums),
        out_specs=[rev(D), pl.BlockSpec((V_ROWS, D), lambda i: (0, 0)), rev(D), rev_t(D_IN), rev(D), rev_t(D),
                   rev_t(W), rev(2 * W)] + [ANY] * n_sums,
        out_shape=[jax.ShapeDtypeStruct((t_len, D), F32), jax.ShapeDtypeStruct((V_ROWS, D), F32),
                   jax.ShapeDtypeStruct((t_len, D), BF16), jax.ShapeDtypeStruct((D_IN, t_len), BF16),
                   jax.ShapeDtypeStruct((t_len, D), BF16), jax.ShapeDtypeStruct((D, t_len), BF16),
                   jax.ShapeDtypeStruct((W, t_len), BF16), jax.ShapeDtypeStruct((t_len, 2 * W), BF16)]
        + [jax.ShapeDtypeStruct((len(CHIP_FLIPS),) + s.shape[1:], s.dtype) for s in chip_sums],
        scratch_shapes=[pltpu.VMEM((D_IN, D), BF16), pltpu.VMEM((D, D), BF16), pltpu.SemaphoreType.DMA((2,)),
                        ext, ext, ext, ext, ext, ext, pltpu.VMEM((HALO, W), F32),
                        pltpu.VMEM((tb, W), F32), pltpu.VMEM((tb, W), F32), pltpu.VMEM((tb, W), F32),
                        pltpu.SemaphoreType.DMA((n_sems,)), pltpu.SemaphoreType.DMA((n_sems,))],
        compiler_params=pltpu.CompilerParams(dimension_semantics=("arbitrary",), vmem_limit_bytes=VMEM_LIMIT),
    )(x, mixed, dh2, dx2, proj, proj, hl, hl, kept, *small, wpack, *chip_sums)


def _matmul(name, a, b, tm=512):
    m, k = a.shape
    n = b.shape[1]

    def body(a_ref, b_ref, o_ref):
        o_ref[...] = _dot(a_ref[...], b_ref[...])

    return pl.pallas_call(
        body,
        name=name,
        grid=(m // tm,),
        in_specs=[pl.BlockSpec((tm, k), lambda i: (i, 0)), pl.BlockSpec((k, n), lambda i: (0, 0))],
        out_specs=pl.BlockSpec((tm, n), lambda i: (i, 0)),
        out_shape=jax.ShapeDtypeStruct((m, n), F32),
        compiler_params=pltpu.CompilerParams(dimension_semantics=("arbitrary",), vmem_limit_bytes=VMEM_LIMIT),
    )(a, b)


def _matmul_beside_chips(name, a, b, chip_sum, landed, rows, tm=512):
    m, k = a.shape
    n = b.shape[1]
    steps = m // tm

    def body(a_ref, b_ref, src_hbm, landed_in, o_ref, landed_hbm, send_sems, recv_sems):
        i = pl.program_id(0)
        copies = _chip_scatter_copies([src_hbm], [landed_hbm], send_sems, recv_sems, [rows])

        @pl.when(i == 0)
        def _():
            for cp in copies:
                cp.start()

        o_ref[...] = _dot(a_ref[...], b_ref[...])

        @pl.when(i == steps - 1)
        def _():
            for cp in copies:
                cp.wait_recv()
            for cp in copies:
                cp.wait_send()

    n_sems = len(CHIP_FLIPS)
    return pl.pallas_call(
        body,
        name=name,
        grid=(steps,),
        in_specs=[pl.BlockSpec((tm, k), lambda i: (i, 0)), pl.BlockSpec((k, n), lambda i: (0, 0)), ANY, ANY],
        out_specs=[pl.BlockSpec((tm, n), lambda i: (i, 0)), ANY],
        out_shape=[jax.ShapeDtypeStruct((m, n), F32), jax.ShapeDtypeStruct(landed.shape, landed.dtype)],
        input_output_aliases={3: 1},
        scratch_shapes=[pltpu.SemaphoreType.DMA((n_sems,)), pltpu.SemaphoreType.DMA((n_sems,))],
        compiler_params=pltpu.CompilerParams(dimension_semantics=("arbitrary",), vmem_limit_bytes=VMEM_LIMIT),
    )(a, b, chip_sum, landed)


def _gate_wgrad(xl_t, dgate, avg):
    hd = W // 8

    def body(a_ref, b_ref, avg_ref, o_ref):
        full = _dot(a_ref[...], b_ref[...])
        row = lax.broadcasted_iota(jnp.int32, (W, hd), 0)
        col = lax.broadcasted_iota(jnp.int32, (W, hd), 1)
        fold = ((row & (hd - 1)) == col).astype(BF16)
        keep = avg_ref[...] != 0
        for g in range(2):
            m = jnp.where(keep, full[:, g * W:(g + 1) * W], 0.0)
            hi = m.astype(BF16)
            rest = m - hi.astype(F32)
            mid = rest.astype(BF16)
            lo = (rest - mid.astype(F32)).astype(BF16)
            o_ref[g] = _dot(hi, fold) + _dot(mid, fold) + _dot(lo, fold)

    return pl.pallas_call(
        body,
        name="wgrad_gate",
        in_specs=[WHOLE] * 3,
        out_specs=WHOLE,
        out_shape=jax.ShapeDtypeStruct((2, W, hd), F32),
        compiler_params=pltpu.CompilerParams(vmem_limit_bytes=VMEM_LIMIT),
    )(xl_t, dgate, avg)


def _block_diag(w):
    n, m, _ = w.shape
    eye = jnp.eye(n, dtype=w.dtype)
    return (w[:, :, None, :] * eye[:, None, :, None]).reshape(n * m, n * m)


def _pad_rows(a, rows):
    return jnp.pad(a, ((0, rows - a.shape[0]),) + ((0, 0),) * (a.ndim - 1))


def _position():
    return lax.axis_index("x"), lax.axis_index("y"), lax.axis_index("c")


def _linear(pos):
    return 4 * pos[0] + 2 * pos[1] + pos[2]


def _flip(pos, k):
    return tuple(1 - p if k & bit else p for p, bit in zip(pos, (4, 2, 1)))


def _exchange_all(make_copy, make_arrival):
    copies = [make_copy(k) for k in range(1, N_DEV)]
    for cp in copies:
        cp.start()
    for k in range(1, N_DEV):
        make_arrival(k).wait_recv()
    for cp in copies:
        cp.wait_send()


def _mod_exchange_steps(cols):
    def steps(msg_ref, adaw_ref, gath_ref, mod_ref, sendbuf, send_a, recv_a, send_b, recv_b):
        me = _position()
        me_lin = _linear(me)
        m = msg_ref[...]
        row = lax.broadcasted_iota(jnp.int32, m.shape, 0)
        gath_ref[me_lin] = jnp.where(row == 0, m * _sigmoid(m), m)

        def gather_copy(k, src_lin):
            return pltpu.make_async_remote_copy(
                src_ref=gath_ref.at[src_lin], dst_ref=gath_ref.at[src_lin], send_sem=send_a.at[k - 1],
                recv_sem=recv_a.at[k - 1], device_id=_flip(me, k), device_id_type=MESH)

        _exchange_all(lambda k: gather_copy(k, me_lin), lambda k: gather_copy(k, _linear(_flip(me, k))))

        sc_all = gath_ref[:, 0, :]
        scb = jnp.concatenate([sc_all, jnp.zeros_like(sc_all)], axis=0).astype(BF16)
        prod = _dot(scb, adaw_ref[...].astype(BF16))
        for b in range(N_DEV):
            sendbuf[b] = jnp.broadcast_to(prod[b:b + 1, :], (HALO, cols))
        mod_ref[me_lin] = sendbuf[me_lin]

        def row_copy(k, dst_lin):
            peer = _flip(me, k)
            return pltpu.make_async_remote_copy(
                src_ref=sendbuf.at[_linear(peer)], dst_ref=mod_ref.at[dst_lin], send_sem=send_b.at[k - 1],
                recv_sem=recv_b.at[k - 1], device_id=peer, device_id_type=MESH)

        _exchange_all(lambda k: row_copy(k, me_lin), lambda k: row_copy(k, _linear(_flip(me, k))))

    return steps


def _gather_and_mod(msg, ada_w, block):
    rows, cols = block.shape
    mod_cols = ada_w.shape[1]
    mod_steps = _mod_exchange_steps(mod_cols)

    def body(msg_ref, adaw_ref, x_ref, gath_ref, mod_ref, out_ref, sendbuf, send_a, recv_a, send_b, recv_b,
             send_sems, recv_sems, sib_send_sems, sib_recv_sems, local_sem):
        x, y, c = _position()
        me, sibling = (x, y, c), (x, y, 1 - c)
        sends, forward, arrivals = _chip_gather_copies(x_ref, out_ref, send_sems, recv_sems)

        def to_sibling(j, block_of, src=None):
            dst = out_ref.at[_linear(block_of)]
            return pltpu.make_async_remote_copy(
                src_ref=dst if src is None else src, dst_ref=dst, send_sem=sib_send_sems.at[j],
                recv_sem=sib_recv_sems.at[j], device_id=sibling, device_id_type=MESH)

        mine = pltpu.make_async_copy(x_ref, out_ref.at[_linear(me)], local_sem)
        mine.start()
        passes = [to_sibling(0, me, src=x_ref)] + [to_sibling(1 + j, p) for j, p in enumerate(_route_peers(me))]
        passes[0].start()
        for cp in sends:
            cp.start()
        mod_steps(msg_ref, adaw_ref, gath_ref, mod_ref, sendbuf, send_a, recv_a, send_b, recv_b)
        arrivals[0].wait_recv()
        forward.start()
        passes[1].start()
        arrivals[1].wait_recv()
        passes[2].start()
        arrivals[2].wait_recv()
        passes[3].start()
        for j, p in enumerate((sibling,) + _route_peers(sibling)):
            to_sibling(j, p).wait_recv()
        for cp in sends + [forward] + passes:
            cp.wait_send()
        mine.wait()

    return pl.pallas_call(
        body,
        name="gather_and_mod",
        in_specs=[WHOLE, WHOLE, ANY],
        out_specs=[WHOLE, WHOLE, ANY],
        out_shape=[jax.ShapeDtypeStruct((N_DEV, HALO, D), F32), jax.ShapeDtypeStruct((N_DEV, HALO, mod_cols), F32),
                   jax.ShapeDtypeStruct((N_DEV, rows, cols), block.dtype)],
        scratch_shapes=[pltpu.VMEM((N_DEV, HALO, mod_cols), F32)] + [pltpu.SemaphoreType.DMA((N_DEV - 1,))] * 4
        + [pltpu.SemaphoreType.DMA((3,)), pltpu.SemaphoreType.DMA((3,)), pltpu.SemaphoreType.DMA((4,)),
           pltpu.SemaphoreType.DMA((4,)), pltpu.SemaphoreType.DMA],
        compiler_params=pltpu.CompilerParams(vmem_limit_bytes=VMEM_LIMIT),
    )(msg, ada_w, block)


HBM = pl.BlockSpec(memory_space=pltpu.HBM)
SEM = pl.BlockSpec(memory_space=pltpu.SEMAPHORE)
EFFECT = pltpu.SideEffectType.DATAFLOW_SIDE_EFFECTING


def _chips_start(which, chip_sums):
    n = len(chip_sums)
    n_sems = len(CHIP_FLIPS) * n

    def body(*refs):
        srcs, dsts = refs[:n], refs[n:2 * n]
        send_sems, recv_sems = refs[2 * n:2 * n + 2]
        token = refs[-1]
        for cp in _chip_scatter_copies(srcs, dsts, send_sems, recv_sems):
            cp.start()
        token[...] = jnp.zeros(token.shape, token.dtype)

    landing = [jax.ShapeDtypeStruct((len(CHIP_FLIPS),) + s.shape[1:], s.dtype) for s in chip_sums]
    outs = pl.pallas_call(
        body,
        name=which + "_chips_start",
        in_specs=[HBM] * (2 * n),
        out_specs=[SEM, SEM] + [HBM] * (2 * n) + [WHOLE],
        out_shape=[pltpu.SemaphoreType.DMA((n_sems,)), pltpu.SemaphoreType.DMA((n_sems,))]
        + [pltpu.HBM(s.shape, s.dtype) for s in chip_sums] + [pltpu.HBM(s.shape, s.dtype) for s in landing]
        + [jax.ShapeDtypeStruct((HALO, 128), F32)],
        input_output_aliases={i: 2 + i for i in range(2 * n)},
        compiler_params=pltpu.CompilerParams(has_side_effects=EFFECT),
    )(*[pltpu.with_memory_space_constraint(s, pltpu.HBM) for s in chip_sums],
      *[pltpu.with_memory_space_constraint(lax.empty(s.shape, s.dtype), pltpu.HBM) for s in landing])
    return outs[0], outs[1], outs[2:2 + n], outs[2 + n:2 + 2 * n], outs[-1]


def _chips_wait(which, send_sems, recv_sems, srcs, landed, after):
    n = len(srcs)

    def body(*refs):
        src_refs, dst_refs = refs[:n], refs[n:2 * n]
        sends, recvs = refs[2 * n:2 * n + 2]
        copies = _chip_scatter_copies(src_refs, dst_refs, sends, recvs)
        for cp in copies:
            cp.wait_send()
        for cp in copies:
            cp.wait_recv()

    outs = pl.pallas_call(
        body,
        name=which + "_chips_wait",
        in_specs=[HBM] * (2 * n) + [SEM, SEM, ANY],
        out_specs=[HBM] * (2 * n),
        out_shape=[pltpu.HBM(s.shape, s.dtype) for s in list(srcs) + list(landed)],
        input_output_aliases={i: i for i in range(2 * n)},
        compiler_params=pltpu.CompilerParams(has_side_effects=EFFECT),
    )(*srcs, *landed, send_sems, recv_sems, after)
    return list(outs[n:])


def _sibling_copies(srcs, dsts, send_sems, recv_sems):
    x, y, c = _position()
    copies = []
    for a, (src, dst) in enumerate(zip(srcs, dsts)):
        for k in range(4):
            copies.append(pltpu.make_async_remote_copy(
                src_ref=src.at[k, 1 - c], dst_ref=dst.at[k], send_sem=send_sems.at[4 * a + k],
                recv_sem=recv_sems.at[4 * a + k], device_id=(x, y, 1 - c), device_id_type=MESH))
    return copies


def _row_block(rows):
    return 256 if rows % 256 == 0 else rows // 2


def _pair_sum(pos, mine, recv):
    _, cores, rows, cols = mine.shape
    rb = _row_block(rows)

    def body(pos_ref, mine_ref, recv_ref, out_ref):
        out_ref[0] = (mine_ref[0, 0] + recv_ref[0]).astype(BF16)

    other = lambda k, pos: jnp.bitwise_xor(pos[1], k + 1)
    core = lambda pos: pos[0] * (cores - 1)
    return pl.pallas_call(
        body,
        name="grad_pair_sum",
        grid_spec=pltpu.PrefetchScalarGridSpec(
            num_scalar_prefetch=1, grid=(3, rows // rb),
            in_specs=[pl.BlockSpec((1, 1, rb, cols), lambda k, r, pos: (other(k, pos), core(pos), r, 0)),
                      pl.BlockSpec((1, rb, cols), lambda k, r, pos: (other(k, pos), r, 0))],
            out_specs=pl.BlockSpec((1, rb, cols), lambda k, r, pos: (other(k, pos), r, 0))),
        out_shape=jax.ShapeDtypeStruct((4, rows, cols), BF16),
        compiler_params=pltpu.CompilerParams(dimension_semantics=("arbitrary", "arbitrary")),
    )(pos, mine, recv)


def _final_sum(pos, mine, recv, chips):
    _, cores, rows, cols = mine.shape
    rb = _row_block(rows)

    def body(pos_ref, mine_ref, recv_ref, chips_ref, out_ref):
        g = mine_ref[0, 0] + recv_ref[0]
        for j in range(3):
            g = g + chips_ref[j].astype(F32)
        out_ref[...] = g

    return pl.pallas_call(
        body,
        name="grad_final_sum",
        grid_spec=pltpu.PrefetchScalarGridSpec(
            num_scalar_prefetch=1, grid=(rows // rb,),
            in_specs=[pl.BlockSpec((1, 1, rb, cols), lambda r, pos: (pos[1], pos[0] * (cores - 1), r, 0)),
                      pl.BlockSpec((1, rb, cols), lambda r, pos: (pos[1], r, 0)),
                      pl.BlockSpec((3, rb, cols), lambda r, pos: (0, r, 0))],
            out_specs=pl.BlockSpec((rb, cols), lambda r, pos: (r, 0))),
        out_shape=jax.ShapeDtypeStruct((rows, cols), F32),
        compiler_params=pltpu.CompilerParams(dimension_semantics=("arbitrary",)),
    )(pos, mine, recv, chips)


LOSS_ROW = V_ROWS + 8
GB_BASE = LOSS_ROW + 8


def _sibling_and_route(gmod8, sc_t, parts):
    cols = gmod8.shape[1]
    n = len(parts)

    def body(gmod_ref, sct_ref, *refs):
        srcs, (gadaw_ref, gb_ref), dsts = refs[:n], refs[n:n + 2], refs[n + 2:2 * n + 2]
        sendbuf, grecv, send_a, recv_a, sib_send, sib_recv = refs[2 * n + 2:]
        sib_copies = _sibling_copies(srcs, dsts, sib_send, sib_recv)
        for cp in sib_copies:
            cp.start()
        me = _position()
        me_lin = _linear(me)
        gm = gmod_ref[...]
        for b in range(N_DEV):
            sendbuf[b] = jnp.broadcast_to(gm[b:b + 1, :], (HALO, cols))
        grecv[me_lin] = sendbuf[me_lin]

        def row_copy(k, dst_lin):
            peer = _flip(me, k)
            return pltpu.make_async_remote_copy(
                src_ref=sendbuf.at[_linear(peer)], dst_ref=grecv.at[dst_lin], send_sem=send_a.at[k - 1],
                recv_sem=recv_a.at[k - 1], device_id=peer, device_id_type=MESH)

        _exchange_all(lambda k: row_copy(k, me_lin), lambda k: row_copy(k, _linear(_flip(me, k))))
        g_all = grecv[:, 0, :]
        g_pad = jnp.concatenate([g_all, jnp.zeros((sct_ref.shape[1] - N_DEV, cols), F32)], axis=0).astype(BF16)
        gadaw_ref[...] = _dot(sct_ref[...], g_pad)
        gb_ref[...] = jnp.broadcast_to(_colsum(g_all), (HALO, cols))
        for cp in sib_copies:
            cp.wait_recv()
        for cp in sib_copies:
            cp.wait_send()

    return pl.pallas_call(
        body,
        name="sibling_and_route",
        in_specs=[WHOLE, WHOLE] + [ANY] * n,
        out_specs=[WHOLE, WHOLE] + [ANY] * n,
        out_shape=[jax.ShapeDtypeStruct((D, cols), F32), jax.ShapeDtypeStruct((HALO, cols), F32)]
        + [jax.ShapeDtypeStruct((4,) + p.shape[2:], p.dtype) for p in parts],
        scratch_shapes=[pltpu.VMEM((N_DEV, HALO, cols), F32), pltpu.VMEM((N_DEV, HALO, cols), F32),
                        pltpu.SemaphoreType.DMA((N_DEV - 1,)), pltpu.SemaphoreType.DMA((N_DEV - 1,)),
                        pltpu.SemaphoreType.DMA((4 * n,)), pltpu.SemaphoreType.DMA((4 * n,))],
        compiler_params=pltpu.CompilerParams(vmem_limit_bytes=VMEM_LIMIT),
    )(gmod8, sc_t, *parts)


def _chips_and_gather(msg_vec, msg_gate, gb_rows, chip_sums):
    cols = gb_rows.shape[1]
    n = len(chip_sums)
    vec_rows = GB_BASE + N_DEV

    def body(vec_ref, gate_ref, gb_ref, *refs):
        srcs, (sumv_ref, sumg_ref), dsts = refs[:n], refs[n:n + 2], refs[n + 2:2 * n + 2]
        (myv, myg, sibv, sibg, chipv, chipg, sib_send, sib_recv, peer_send, peer_recv,
         chip_send, chip_recv) = refs[2 * n + 2:]
        chip_copies = _chip_scatter_copies(srcs, dsts, chip_send, chip_recv)
        for cp in chip_copies:
            cp.start()
        x, y, c = me = _position()
        my_chip = 2 * x + y
        myv[0:GB_BASE, :] = vec_ref[...]
        slot = lax.broadcasted_iota(jnp.int32, (N_DEV, D), 0) == _linear(me)
        gb_wide = jnp.concatenate([gb_ref[...], jnp.zeros((N_DEV, D - cols), F32)], axis=1)
        myv[GB_BASE:vec_rows, :] = jnp.where(slot, gb_wide, 0.0)
        myg[...] = gate_ref[...]

        swaps = [pltpu.make_async_remote_copy(
            src_ref=src, dst_ref=dst, send_sem=sib_send.at[a], recv_sem=sib_recv.at[a], device_id=(x, y, 1 - c),
            device_id_type=MESH) for a, (src, dst) in enumerate(((myv, sibv), (myg, sibg)))]
        for cp in swaps:
            cp.start()
        for cp in swaps:
            cp.wait_recv()
        chipv[my_chip] = myv[...] + sibv[...]
        chipg[my_chip] = myg[...] + sibg[...]

        def chip_copy(a, buf, j, k, slot_chip):
            peer = _flip(me, k)
            return pltpu.make_async_remote_copy(
                src_ref=buf.at[slot_chip], dst_ref=buf.at[slot_chip], send_sem=peer_send.at[3 * a + j],
                recv_sem=peer_recv.at[3 * a + j], device_id=peer, device_id_type=MESH)

        sends = [chip_copy(a, buf, j, k, my_chip) for a, buf in enumerate((chipv, chipg)) for j, k in enumerate(CHIP_FLIPS)]
        for cp in sends:
            cp.start()
        for a, buf in enumerate((chipv, chipg)):
            for j, k in enumerate(CHIP_FLIPS):
                peer = _flip(me, k)
                chip_copy(a, buf, j, k, 2 * peer[0] + peer[1]).wait_recv()
        sumv_ref[...] = ((chipv[0] + chipv[1]) + chipv[2]) + chipv[3]
        sumg_ref[...] = ((chipg[0] + chipg[1]) + chipg[2]) + chipg[3]
        for cp in swaps + sends:
            cp.wait_send()
        for cp in chip_copies:
            cp.wait_recv()
        for cp in chip_copies:
            cp.wait_send()

    n_chip = max(len(CHIP_FLIPS) * n, 1)
    vshape, gshape = (vec_rows, D), msg_gate.shape
    return pl.pallas_call(
        body,
        name="chips_and_gather",
        in_specs=[WHOLE] * 3 + [ANY] * n,
        out_specs=[WHOLE] * 2 + [ANY] * n,
        out_shape=[jax.ShapeDtypeStruct(vshape, F32), jax.ShapeDtypeStruct(gshape, F32)]
        + [jax.ShapeDtypeStruct((len(CHIP_FLIPS),) + s.shape[1:], s.dtype) for s in chip_sums],
        scratch_shapes=[pltpu.VMEM(vshape, F32), pltpu.VMEM(gshape, F32), pltpu.VMEM(vshape, F32),
                        pltpu.VMEM(gshape, F32), pltpu.VMEM((4,) + vshape, F32), pltpu.VMEM((4,) + gshape, F32),
                        pltpu.SemaphoreType.DMA((2,)), pltpu.SemaphoreType.DMA((2,)),
                        pltpu.SemaphoreType.DMA((2 * len(CHIP_FLIPS),)), pltpu.SemaphoreType.DMA((2 * len(CHIP_FLIPS),)),
                        pltpu.SemaphoreType.DMA((n_chip,)), pltpu.SemaphoreType.DMA((n_chip,))],
        compiler_params=pltpu.CompilerParams(vmem_limit_bytes=VMEM_LIMIT),
    )(msg_vec, msg_gate, gb_rows, *chip_sums)


def _adamw_math(w, g, m, v):
    m = ADAM_B1 * m + (1.0 - ADAM_B1) * g
    v = ADAM_B2 * v + (1.0 - ADAM_B2) * (g * g)
    m_hat = m / (1.0 - ADAM_B1 ** ADAM_STEP)
    v_hat = v / (1.0 - ADAM_B2 ** ADAM_STEP)
    delta = -ADAM_LR * (m_hat / (jnp.sqrt(v_hat) + ADAM_EPS) + ADAM_WD * w)
    return delta, m, v


def _adamw(name, w, g, m, v):
    rows, cols = w.shape
    rb = 256 if rows % 256 == 0 else rows

    def body(w_ref, g_ref, m_ref, v_ref, d_ref, mo_ref, vo_ref):
        d_ref[...], mo_ref[...], vo_ref[...] = _adamw_math(w_ref[...], g_ref[...], m_ref[...], v_ref[...])

    spec = pl.BlockSpec((rb, cols), lambda r: (r, 0))
    return pl.pallas_call(
        body,
        name="adamw_" + name,
        grid=(rows // rb,),
        in_specs=[spec] * 4,
        out_specs=[spec] * 3,
        out_shape=[jax.ShapeDtypeStruct((rows, cols), F32)] * 3,
        compiler_params=pltpu.CompilerParams(dimension_semantics=("arbitrary",)),
    )(w, g, m, v)


def _update_beside_chips(pos, sum_jobs, plain_jobs, chip_sums):
    rb = 256
    jobs = [("sum", j) for j in sum_jobs] + [("plain", j) for j in plain_jobs]
    offs, total = [], 0
    for _, j in jobs:
        offs.append(total)
        total += j[-1].shape[0] // rb
    n_chip = len(chip_sums)
    n_in = sum(len(j) for _, j in jobs)
    n_out = 4 * len(sum_jobs) + 3 * len(plain_jobs)

    def body(pos_ref, *refs):
        ins, srcs = refs[:n_in], refs[n_in:n_in + n_chip]
        outs = refs[n_in + n_chip:n_in + n_chip + n_out]
        dsts = refs[n_in + n_chip + n_out:n_in + 2 * n_chip + n_out]
        send_sems, recv_sems = refs[n_in + 2 * n_chip + n_out:]
        s = pl.program_id(0)
        copies = _chip_scatter_copies(srcs, dsts, send_sems, recv_sems)

        @pl.when(s == 0)
        def _():
            for cp in copies:
                cp.start()

        i_in = i_out = 0
        for (kind, j), off in zip(jobs, offs):
            steps = j[-1].shape[0] // rb
            j_in = ins[i_in:i_in + len(j)]
            i_in += len(j)
            j_out = outs[i_out:i_out + (4 if kind == "sum" else 3)]
            i_out += len(j_out)

            @pl.when((s >= off) & (s < off + steps))
            def _(kind=kind, j_in=j_in, j_out=j_out):
                if kind == "sum":
                    mine_ref, recv_ref, chips_ref, w_ref, m_ref, v_ref = j_in
                    g = mine_ref[0, 0] + recv_ref[0]
                    for q in range(len(CHIP_FLIPS)):
                        g = g + chips_ref[q].astype(F32)
                    j_out[0][...] = g
                    rest = j_out[1:]
                else:
                    g_ref, w_ref, m_ref, v_ref = j_in
                    g = g_ref[...]
                    rest = j_out
                rest[0][...], rest[1][...], rest[2][...] = _adamw_math(w_ref[...], g, m_ref[...], v_ref[...])

        @pl.when(s == total - 1)
        def _():
            for cp in copies:
                cp.wait_recv()
            for cp in copies:
                cp.wait_send()

    in_specs, out_specs, out_shape, args = [], [], [], []
    for (kind, j), off in zip(jobs, offs):
        rows, cols = j[-1].shape
        steps = rows // rb
        blk = lambda s, off=off, steps=steps: jnp.clip(s - off, 0, steps - 1)
        flat = pl.BlockSpec((rb, cols), lambda s, pos, blk=blk: (blk(s), 0))
        if kind == "sum":
            in_specs += [pl.BlockSpec((1, 1, rb, cols), lambda s, pos, blk=blk: (pos[1], 0, blk(s), 0)),
                         pl.BlockSpec((1, rb, cols), lambda s, pos, blk=blk: (pos[1], blk(s), 0)),
                         pl.BlockSpec((len(CHIP_FLIPS), rb, cols), lambda s, pos, blk=blk: (0, blk(s), 0))]
            in_specs += [flat] * 3
        else:
            in_specs += [flat] * 4
        n_res = 4 if kind == "sum" else 3
        out_specs += [flat] * n_res
        out_shape += [jax.ShapeDtypeStruct((rows, cols), F32)] * n_res
        args += list(j)
    in_specs += [ANY] * n_chip
    out_specs += [ANY] * n_chip
    out_shape += [jax.ShapeDtypeStruct((len(CHIP_FLIPS),) + c.shape[1:], c.dtype) for c in chip_sums]
    n_sems = len(CHIP_FLIPS) * n_chip
    outs = pl.pallas_call(
        body,
        name="update_beside_chips",
        grid_spec=pltpu.PrefetchScalarGridSpec(
            num_scalar_prefetch=1, grid=(total,), in_specs=in_specs, out_specs=out_specs,
            scratch_shapes=[pltpu.SemaphoreType.DMA((n_sems,)), pltpu.SemaphoreType.DMA((n_sems,))]),
        out_shape=out_shape,
        compiler_params=pltpu.CompilerParams(dimension_semantics=("arbitrary",), vmem_limit_bytes=VMEM_LIMIT),
    )(pos, *args, *chip_sums)
    sums = [tuple(outs[4 * i:4 * i + 4]) for i in range(len(sum_jobs))]
    base = 4 * len(sum_jobs)
    plains = [tuple(outs[base + 3 * i:base + 3 * i + 3]) for i in range(len(plain_jobs))]
    return sums, plains, list(outs[n_out:])


def _adamw_small(ws, gs, ms, vs, sigmoid_scaled):
    n = len(ws)

    def body(*refs):
        w_refs, g_refs, m_refs, v_refs = (refs[i * n:(i + 1) * n] for i in range(4))
        outs = refs[4 * n:]
        for i in range(n):
            w = w_refs[i][...]
            g = g_refs[i][...]
            if sigmoid_scaled[i]:
                g = g * _sigmoid(w)
            delta, m, v = _adamw_math(w, g, m_refs[i][...], v_refs[i][...])
            outs[4 * i][...] = g
            outs[4 * i + 1][...] = delta
            outs[4 * i + 2][...] = m
            outs[4 * i + 3][...] = v

    shapes = [jax.ShapeDtypeStruct(w.shape, F32) for w in ws for _ in range(4)]
    outs = pl.pallas_call(
        body,
        name="adamw_small",
        in_specs=[WHOLE] * (4 * n),
        out_specs=[WHOLE] * (4 * n),
        out_shape=shapes,
    )(*ws, *gs, *ms, *vs)
    return [outs[4 * i:4 * i + 4] for i in range(n)]


_WEIGHT_NAMES = ("ada_w", "ada_b", "norm1_g", "w_in", "lru_conv_w", "lru_conv_b", "gate_a_w", "gate_a_b", "gate_x_w",
                 "gate_x_b", "a_param", "short_conv_w", "lru_out_g", "conv_out_g", "w_out", "norm2_g", "w_mlp1",
                 "w_mlp2", "final_g")


def kernel(x, c, ada_w, ada_b, norm1_g, w_in, lru_conv_w, lru_conv_b, gate_a_w, gate_a_b, gate_x_w, gate_x_b, a_param, short_conv_w, lru_out_g, conv_out_g, w_out, norm2_g, w_mlp1, w_mlp2, final_g, loss_target, m_ada_w, m_ada_b, m_norm1_g, m_w_in, m_lru_conv_w, m_lru_conv_b, m_gate_a_w, m_gate_a_b, m_gate_x_w, m_gate_x_b, m_a_param, m_short_conv_w, m_lru_out_g, m_conv_out_g, m_w_out, m_norm2_g, m_w_mlp1, m_w_mlp2, m_final_g, v_ada_w, v_ada_b, v_norm1_g, v_w_in, v_lru_conv_w, v_lru_conv_b, v_gate_a_w, v_gate_a_b, v_gate_x_w, v_gate_x_b, v_a_param, v_short_conv_w, v_lru_out_g, v_conv_out_g, v_w_out, v_norm2_g, v_w_mlp1, v_w_mlp2, v_final_g):
    given = dict(locals())
    weights = {n: given[n] for n in _WEIGHT_NAMES}
    xi, yi, ci = _position()
    me_lin = _linear((xi, yi, ci))
    hd = W // N_DEV

    mixer_block = jnp.concatenate([w_out[0], w_in[0].T], axis=0).astype(BF16)
    mlp_block = jnp.concatenate([w_mlp1[0].T, w_mlp2[0]], axis=0).astype(BF16)

    msg = (jnp.pad(c, ((0, HALO - 1), (0, 0)))
           + jnp.pad(lru_conv_w[0], ((1, HALO - 1 - CONV_L), (0, D - hd)))
           + jnp.pad(short_conv_w[0], ((1 + CONV_L, 0), (0, D - hd))))
    gath, mod_all, wmix = _gather_and_mod(msg, ada_w[0], mixer_block)
    sc_all = gath[:, 0, :]
    wl = jnp.transpose(gath[:, 1:1 + CONV_L, :hd], (1, 0, 2)).reshape(CONV_L, W)
    ws = jnp.transpose(gath[:, 1 + CONV_L:HALO, :hd], (1, 0, 2)).reshape(CONV_S, W)
    modraw = _pad_rows(mod_all[:, 0, :].reshape(6, D), HALO)
    adab = _pad_rows(ada_b.reshape(6, D), HALO)

    x2d, tgt = x[0], loss_target[0]
    gf = final_g.reshape(1, D)
    bda = _block_diag(gate_a_w[0]).astype(BF16)
    bdx = _block_diag(gate_x_w[0]).astype(BF16)
    avg = _block_diag(jnp.full((8, W // 8, W // 8), 8.0 / W, F32)).astype(BF16)
    wl8 = _pad_rows(wl, HALO)
    ws8 = _pad_rows(ws, HALO)
    mixer_small = (wl8, lru_conv_b, bda, bdx, gate_a_b, gate_x_b, a_param, ws8, lru_out_g, conv_out_g, avg)
    proj, hl, mixed, kept, wmlp = _mixer_fwd(x2d, modraw, adab, norm1_g, *mixer_small, wmix, mlp_block)
    wmlp = _sibling_forward(wmlp)
    h2t, f, dx2, dz, vec2, loss8 = _mlp_fwd(x2d, mixed, tgt, modraw, adab, norm2_g, gf, wmlp)
    pos = jnp.stack([ci, 2 * xi + yi]).astype(jnp.int32)
    by_dest = lambda g: g.reshape((4, 2, -1) + g.shape[-1:])
    dh2, dw1, dw2, sib1, sib2 = _mlp_bwd_half(pos, h2t, f, dz, wmlp, prior=_mlp_bwd_half(pos, h2t, f, dz, wmlp))
    mlp_parts = [dw1[:, None], dw2[:, None]]
    mlp_sib = [sib1, sib2]
    mlp_sums = [_pair_sum(pos, p, r) for p, r in zip(mlp_parts, mlp_sib)]
    mlp_send, mlp_recv, mlp_thru, mlp_land, token = _chips_start("mlp", mlp_sums)
    modraw_after = modraw + jnp.tile(token, (1, D // token.shape[1]))
    gx, vec, hb, dproj_t, dmixed, ycat_t, xl_t, dgate = _mixer_bwd(
        x2d, mixed, dh2, dx2, proj, hl, kept, modraw_after, adab, norm1_g, norm2_g, *mixer_small, wmix, [], None)
    dwint = _matmul("wgrad_in", dproj_t, hb)
    dwout = _matmul("wgrad_out", ycat_t, dmixed)
    gate_blocks = _gate_wgrad(xl_t, dgate, avg)
    msg_gate = gate_blocks.reshape(W, 128)
    done = dwint[0:HALO, 0:128] + dwout[0:HALO, 0:128] + gate_blocks[0, 0:HALO, :].sum() + gx[0:HALO, 0:128]
    mlp_chips = _chips_wait("mlp", mlp_send, mlp_recv, mlp_thru, mlp_land, done)
    mix_parts = [by_dest(dwout), by_dest(dwint)]
    gmod8 = (jnp.pad(vec[0:5], ((0, 1), (0, 0))) + jnp.pad(vec2[0:1], ((5, 0), (0, 0)))).reshape(N_DEV, 6 * D // N_DEV)
    sc_t = jnp.pad(sc_all.T, ((0, 0), (0, 128 - N_DEV))).astype(BF16)
    g_adaw, gb_rows, *mix_sib = _sibling_and_route(gmod8, sc_t, mix_parts)
    mix_sums = [_pair_sum(pos, p, r) for p, r in zip(mix_parts, mix_sib)]
    loss_rows = jnp.pad(loss8[0:1], ((0, HALO - 1), (0, D - loss8.shape[1])))
    msg_vec = jnp.concatenate([vec, vec2, loss_rows], axis=0)
    state = lambda n: (weights[n][0], given["m_" + n][0], given["v_" + n][0])
    mlp_jobs = [(p, r, q, *state(n)) for p, r, q, n in zip(mlp_parts, mlp_sib, mlp_chips, ("w_mlp1", "w_mlp2"))]
    mlp_done, (adaw_done,), mix_chips = _update_beside_chips(pos, mlp_jobs, [(g_adaw, *state("ada_w"))], mix_sums)
    sum_vec, sum_gate = _chips_and_gather(msg_vec, msg_gate, gb_rows, [])
    g_wout, g_wint = (_final_sum(pos, p, r, q) for p, r, q in zip(mix_parts, mix_sib, mix_chips))
    loss = sum_vec[LOSS_ROW, 0]
    sum_gate = sum_gate.reshape(2, W, W // 8)
    lo, hi = slice(0, W), slice(W, 2 * W)
    wl_full = sum_vec[V_WL01:V_WL23 + 1].reshape(CONV_L, W)
    ws_full = sum_vec[V_WS01:V_WS2 + 1].reshape(CONV_S + 1, W)[:CONV_S]
    row = lambda r, cols: sum_vec[r:r + 1, cols]
    small_grads = {
        "ada_b": sum_vec[GB_BASE:GB_BASE + N_DEV, :6 * D // N_DEV].reshape(1, 6 * D),
        "norm1_g": row(V_G1, slice(0, D)),
        "lru_conv_w": lax.dynamic_slice(wl_full, (0, me_lin * hd), (CONV_L, hd)),
        "lru_conv_b": row(V_BL_BA, lo),
        "gate_a_w": sum_gate[0],
        "gate_a_b": row(V_BL_BA, hi),
        "gate_x_w": sum_gate[1],
        "gate_x_b": row(V_BX_SP, lo),
        "a_param": row(V_BX_SP, hi),
        "short_conv_w": lax.dynamic_slice(ws_full, (0, me_lin * hd), (CONV_S, hd)),
        "lru_out_g": row(V_GL_GC, lo),
        "conv_out_g": row(V_GL_GC, hi),
        "norm2_g": row(V_G2, slice(0, D)),
        "final_g": sum_vec[V_ROWS + 1:V_ROWS + 2, :],
    }
    names = list(small_grads)
    as2d = lambda a, n: a.reshape(small_grads[n].shape)
    small = _adamw_small([as2d(weights[n], n) for n in names], [small_grads[n] for n in names],
                         [as2d(given["m_" + n], n) for n in names], [as2d(given["v_" + n], n) for n in names],
                         [n == "a_param" for n in names])
    result = {n: tuple(o.reshape(weights[n].shape) for o in outs) for n, outs in zip(names, small)}

    for n, g in (("w_in", g_wint.T), ("w_out", g_wout)):
        w, m, v = state(n)
        result[n] = (g[None],) + tuple(o[None] for o in _adamw(n, w, g, m, v))
    result["w_mlp1"], result["w_mlp2"] = (tuple(o[None] for o in done) for done in mlp_done)
    result["ada_w"] = (g_adaw[None],) + tuple(o[None] for o in adaw_done)

    return (loss, gx[None], *[result[n][0] for n in _WEIGHT_NAMES], *[result[n][1] for n in _WEIGHT_NAMES],
            *[result[n][2] for n in _WEIGHT_NAMES], *[result[n][3] for n in _WEIGHT_NAMES])
```

```python
import functools

import jax
import jax.numpy as jnp
from jax import lax
from jax.experimental import pallas as pl
from jax.experimental.pallas import tpu as pltpu

F32 = jnp.float32
BF16 = jnp.bfloat16
MESH = pl.DeviceIdType.MESH

N_DEV = 8
D = 1024
W = 512
D_IN = 5 * W
D_FF = 4096
FF_BLK = D_FF // N_DEV
EPS = 1e-6
C_GATE = 8.0
CONV_L = 4
CONV_S = 3
HALO = 8

ROWS_W1T, ROWS_W2, ROWS_WOUT, ROWS_WIN = FF_BLK, FF_BLK, D // N_DEV, D_IN // N_DEV
OFF_WOUT = 0
OFF_WIN = OFF_WOUT + ROWS_WOUT
MIX_ROWS = OFF_WIN + ROWS_WIN
OFF_W1T = 0
OFF_W2 = OFF_W1T + ROWS_W1T
MLP_ROWS = OFF_W2 + ROWS_W2
CHIP_FLIPS = (4, 2, 6)
N_KEPT = 6

ADAM_LR = 0.001
ADAM_B1 = 0.9
ADAM_B2 = 0.999
ADAM_EPS = 1e-08
ADAM_WD = 0.01
ADAM_STEP = 10

VMEM_LIMIT = 56 * 1024 * 1024

TB_MIX = 256
TB_MIXB = 256
TB_MLP = 256
TB_MLPB = 512

ANY = pl.BlockSpec(memory_space=pl.ANY)
WHOLE = pl.BlockSpec(memory_space=pltpu.VMEM)


def _dot(a, b):
    return jnp.dot(a, b, preferred_element_type=F32)


def _dot_nt(a, b):
    return lax.dot_general(a, b, (((1,), (1,)), ((), ())), preferred_element_type=F32)


def _dot_tn(a, b):
    return lax.dot_general(a, b, (((0,), (0,)), ((), ())), preferred_element_type=F32)


def _sigmoid(v):
    return 1.0 / (1.0 + jnp.exp(-v))


def _softplus(v):
    t = jnp.exp(-jnp.abs(v))
    small = t * (1.0 - t * (0.5 - t * (1.0 / 3.0)))
    return jnp.maximum(v, 0.0) + jnp.where(t < 1e-2, small, jnp.log(1.0 + t))


def _one_minus_sq(a, log_a):
    return -jnp.tanh(log_a) * (a * a + 1.0)


_GELU_K = 0.7978845608028654
_GELU_C = 0.044715


def _gelu(u):
    th = jnp.tanh(_GELU_K * (u + _GELU_C * u * u * u))
    return 0.5 * u * (1.0 + th), th


def _gelu_grad(u, th):
    return 0.5 * (1.0 + th) + 0.5 * u * (1.0 - th * th) * _GELU_K * (1.0 + 3.0 * _GELU_C * u * u)


def _group_mean(v, avg):
    hi = v.astype(BF16)
    lo = (v - hi.astype(F32)).astype(BF16)
    return _dot(hi, avg) + _dot(lo, avg)


def _colsum(v):
    return jnp.sum(v, axis=0, keepdims=True)


def _rowmean(v):
    return jnp.mean(v, axis=-1, keepdims=True)


def _load_packed(wpack_hbm, off, rows, dst, sem):
    copies = [
        pltpu.make_async_copy(wpack_hbm.at[d, pl.ds(off, rows), :], dst.at[pl.ds(d * rows, rows), :], sem)
        for d in range(N_DEV)
    ]
    for cp in copies:
        cp.start()
    return copies


def _scan_groups(n_groups, a_ref, b_ref, out_ref, carry_ref, reverse):
    row = lax.broadcasted_iota(jnp.int32, (HALO, W), 0)

    def step(k, carry):
        g = (n_groups - 1 - k) if reverse else k
        rows = pl.ds(pl.multiple_of(g * HALO, HALO), HALO)
        a = a_ref[rows, :]
        b = b_ref[rows, :]
        for s in (1, 2, 4):
            if reverse:
                keep = row < HALO - s
                sh = HALO - s
            else:
                keep = row >= s
                sh = s
            a_sh = pltpu.roll(a, sh, axis=0)
            b_sh = pltpu.roll(b, sh, axis=0)
            b = jnp.where(keep, a * b_sh + b, b)
            a = jnp.where(keep, a * a_sh, a)
        h = b + a * carry
        out_ref[rows, :] = h
        edge = h[0:1, :] if reverse else h[HALO - 1:HALO, :]
        return jnp.broadcast_to(edge, (HALO, W))

    carry_ref[...] = lax.fori_loop(0, n_groups, step, carry_ref[...])


def _route_peers(me):
    x, y, c = me
    first = ((x + 1 - c) % 2, (y + c) % 2, c)
    second = ((x + c) % 2, (y + 1 - c) % 2, c)
    return first, second, (1 - x, 1 - y, c)


def _chip_gather_copies(block_hbm, out_hbm, send_sems, recv_sems):
    me = _position()
    first, second, diag = _route_peers(me)

    def copy(j, src, slot_of, to):
        return pltpu.make_async_remote_copy(
            src_ref=src, dst_ref=out_hbm.at[_linear(slot_of)], send_sem=send_sems.at[j], recv_sem=recv_sems.at[j],
            device_id=to, device_id_type=MESH)

    own_sends = [copy(0, block_hbm, me, first), copy(1, block_hbm, me, second)]
    forward = copy(2, out_hbm.at[_linear(first)], first, second)
    arrivals = [copy(0, block_hbm, first, first), copy(1, block_hbm, second, second), copy(2, block_hbm, diag, second)]
    return own_sends, forward, arrivals


def _mixer_fwd(x, modraw, adab, g1, wl, bl, bda, bdx, ba, bxb, ap, ws, gl, gc, avg, wpack, mlp_block):
    t_len = x.shape[0]
    tb = TB_MIX
    nb = t_len // tb

    def body(x_ref, modraw_ref, adab_ref, g1_ref, wl_ref, bl_ref, bda_ref, bdx_ref, ba_ref, bxb_ref, ap_ref,
             ws_ref, gl_ref, gc_ref, avg_ref, wpack_hbm, block_hbm, proj_ref, hl_ref, mixed_ref, kept_ref, wmlp_hbm,
             win_v, wout_v, sem, ulx_ext, cv_ext, hcar, a_s, b_s, send_sems, recv_sems, local_sem):
        i = pl.program_id(0)
        own = pltpu.make_async_copy(block_hbm, wmlp_hbm.at[_linear(_position())], local_sem)
        sends, forward, arrivals = _chip_gather_copies(block_hbm, wmlp_hbm, send_sems, recv_sems)

        @pl.when(i == 0)
        def _():
            own.start()
            for cp in sends:
                cp.start()

        @pl.when(i == nb - 1)
        def _():
            arrivals[0].wait_recv()
            forward.start()

        @pl.when(i == 0)
        def _():
            cps = _load_packed(wpack_hbm, OFF_WIN, ROWS_WIN, win_v, sem.at[0])
            cps += _load_packed(wpack_hbm, OFF_WOUT, ROWS_WOUT, wout_v, sem.at[1])
            ulx_ext[0:HALO, :] = jnp.zeros((HALO, W), F32)
            cv_ext[0:HALO, :] = jnp.zeros((HALO, W), F32)
            hcar[...] = jnp.zeros((HALO, W), F32)
            for cp in cps:
                cp.wait()

        mod = modraw_ref[...] + adab_ref[...]
        shift1, scale1, gate1 = mod[0:1], mod[1:2], mod[2:3]
        x = x_ref[...]
        r1 = lax.rsqrt(_rowmean(x * x) + EPS)
        h = (x * r1 * g1_ref[...]) * (1.0 + scale1) + shift1
        proj = _dot_nt(h.astype(BF16), win_v[...])
        proj_ref[...] = proj
        u_lx, u_ly, u_b, u_c, u_v = (proj[:, k * W:(k + 1) * W] for k in range(5))

        ulx_ext[HALO:HALO + tb, :] = u_lx
        xl = bl_ref[...] + wl_ref[CONV_L - 1:CONV_L, :] * u_lx
        for k in range(CONV_L - 1):
            xl = xl + wl_ref[k:k + 1, :] * ulx_ext[pl.ds(HALO - (CONV_L - 1) + k, tb), :]
        ulx_ext[0:HALO, :] = ulx_ext[tb:tb + HALO, :]
        xlb = xl.astype(BF16)
        r = _sigmoid(_dot(xlb, bda_ref[...]) + ba_ref[...])
        ig = _sigmoid(_dot(xlb, bdx_ref[...]) + bxb_ref[...])
        log_a = (-C_GATE) * r * _softplus(ap_ref[...])
        a = jnp.exp(log_a)
        mult = jnp.sqrt(_one_minus_sq(a, log_a))
        grow = i * tb + lax.broadcasted_iota(jnp.int32, (tb, W), 0)
        mult = jnp.where(grow == 0, 1.0, mult)
        a_s[...] = a
        b_s[...] = mult * (ig * xl)
        _scan_groups(tb // HALO, a_s, b_s, hl_ref, hcar, reverse=False)
        hl = hl_ref[...]
        ge, _ = _gelu(u_ly)
        p = ge * hl
        rp = lax.rsqrt(_group_mean(p * p, avg_ref[...]) + EPS)
        y_lru = p * rp * gl_ref[...]

        cv = u_c * u_v
        cv_ext[HALO:HALO + tb, :] = cv
        cc = ws_ref[CONV_S - 1:CONV_S, :] * cv
        for k in range(CONV_S - 1):
            cc = cc + ws_ref[k:k + 1, :] * cv_ext[pl.ds(HALO - (CONV_S - 1) + k, tb), :]
        cv_ext[0:HALO, :] = cv_ext[tb:tb + HALO, :]
        q = u_b * cc
        rq = lax.rsqrt(_group_mean(q * q, avg_ref[...]) + EPS)
        y_conv = q * rq * gc_ref[...]
        for k, kept in enumerate((xl, r, ig, rp, rq, cc)):
            kept_ref[:, k * W:(k + 1) * W] = kept

        mixed_ref[...] = (_dot(y_lru.astype(BF16), wout_v[0:W, :]) + _dot(y_conv.astype(BF16), wout_v[W:2 * W, :]))

        @pl.when(i == nb - 1)
        def _():
            for cp in arrivals[1:]:
                cp.wait_recv()
            for cp in sends + [forward]:
                cp.wait_send()
            own.wait()

    tok = lambda cols: pl.BlockSpec((tb, cols), lambda i: (i, 0))
    full = lambda a: pl.BlockSpec(a.shape, lambda i: (0,) * a.ndim)
    small = (modraw, adab, g1, wl, bl, bda, bdx, ba, bxb, ap, ws, gl, gc, avg)
    n_chips = len(CHIP_FLIPS)
    return pl.pallas_call(
        body,
        name="mixer_fwd",
        grid=(nb,),
        in_specs=[tok(D)] + [full(a) for a in small] + [ANY, ANY],
        out_specs=[tok(D_IN), tok(W), tok(D), tok(N_KEPT * W), ANY],
        out_shape=[jax.ShapeDtypeStruct((t_len, D_IN), F32), jax.ShapeDtypeStruct((t_len, W), F32),
                   jax.ShapeDtypeStruct((t_len, D), F32), jax.ShapeDtypeStruct((t_len, N_KEPT * W), F32),
                   jax.ShapeDtypeStruct((N_DEV,) + mlp_block.shape, BF16)],
        scratch_shapes=[pltpu.VMEM((D_IN, D), BF16), pltpu.VMEM((D, D), BF16), pltpu.SemaphoreType.DMA((2,)),
                        pltpu.VMEM((tb + HALO, W), F32), pltpu.VMEM((tb + HALO, W), F32), pltpu.VMEM((HALO, W), F32),
                        pltpu.VMEM((tb, W), F32), pltpu.VMEM((tb, W), F32),
                        pltpu.SemaphoreType.DMA((n_chips,)), pltpu.SemaphoreType.DMA((n_chips,)), pltpu.SemaphoreType.DMA],
        compiler_params=pltpu.CompilerParams(dimension_semantics=("arbitrary",), vmem_limit_bytes=VMEM_LIMIT),
    )(x, *small, wpack, mlp_block)


def _sibling_forward(wmlp):
    def body(in_hbm, out_hbm, send_sems, recv_sems):
        x, y, c = _position()
        copies, arrivals = [], []
        for j, k in enumerate((0,) + CHIP_FLIPS):
            mine = out_hbm.at[_linear(_flip((x, y, c), k))]
            theirs = out_hbm.at[_linear(_flip((x, y, 1 - c), k))]
            copies.append(pltpu.make_async_remote_copy(
                src_ref=mine, dst_ref=mine, send_sem=send_sems.at[j], recv_sem=recv_sems.at[j],
                device_id=(x, y, 1 - c), device_id_type=MESH))
            arrivals.append(pltpu.make_async_remote_copy(
                src_ref=theirs, dst_ref=theirs, send_sem=send_sems.at[j], recv_sem=recv_sems.at[j],
                device_id=(x, y, 1 - c), device_id_type=MESH))
        for cp in copies:
            cp.start()
        for cp in arrivals:
            cp.wait_recv()
        for cp in copies:
            cp.wait_send()

    return pl.pallas_call(
        body,
        name="sibling_forward",
        in_specs=[ANY],
        out_specs=ANY,
        out_shape=jax.ShapeDtypeStruct(wmlp.shape, wmlp.dtype),
        input_output_aliases={0: 0},
        scratch_shapes=[pltpu.SemaphoreType.DMA((4,)), pltpu.SemaphoreType.DMA((4,))],
    )(wmlp)


def _mlp_fwd(x, mixed, tgt, modraw, adab, g2, gf, wpack):
    t_len = x.shape[0]
    tb = TB_MLP
    nb = t_len // tb

    def body(x_ref, mixed_ref, tgt_ref, modraw_ref, adab_ref, g2_ref, gf_ref, wpack_hbm,
             h2t_ref, f_ref, dx2_ref, dz_ref, vec_ref, loss_ref, w1t_v, w2_v, sem):
        i = pl.program_id(0)

        @pl.when(i == 0)
        def _():
            cps = _load_packed(wpack_hbm, OFF_W1T, ROWS_W1T, w1t_v, sem.at[0])
            cps += _load_packed(wpack_hbm, OFF_W2, ROWS_W2, w2_v, sem.at[1])
            vec_ref[...] = jnp.zeros(vec_ref.shape, F32)
            loss_ref[...] = jnp.zeros(loss_ref.shape, F32)
            for cp in cps:
                cp.wait()

        mod = modraw_ref[...] + adab_ref[...]
        gate1, shift2, scale2, gate2 = mod[2:3], mod[3:4], mod[4:5], mod[5:6]
        x1 = x_ref[...] + gate1 * mixed_ref[...]
        r2 = lax.rsqrt(_rowmean(x1 * x1) + EPS)
        h2 = (x1 * r2 * g2_ref[...]) * (1.0 + scale2) + shift2
        h2b = h2.astype(BF16)
        h2t_ref[...] = h2.T.astype(BF16)
        z = jnp.zeros((tb, D), F32)
        for j in range(N_DEV):
            cols = slice(j * FF_BLK, (j + 1) * FF_BLK)
            fj = _dot_nt(h2b, w1t_v[cols, :])
            f_ref[:, cols] = fj
            rf = jnp.maximum(fj, 0.0)
            z = z + _dot((rf * rf).astype(BF16), w2_v[cols, :])
        x2 = x1 + gate2 * z
        r3 = lax.rsqrt(_rowmean(x2 * x2) + EPS)
        xn3 = x2 * r3
        diff = xn3 * gf_ref[...] - tgt_ref[...]
        sq = _colsum(diff * diff)
        loss_ref[...] += jnp.broadcast_to(jnp.sum(sq, axis=1, keepdims=True) * (0.5 / D), loss_ref.shape)
        dy = diff * (1.0 / D)
        dyn = dy * gf_ref[...]
        dx2 = r3 * (dyn - xn3 * _rowmean(dyn * xn3))
        dx2_ref[...] = dx2
        dz_ref[...] = (gate2 * dx2).astype(BF16)
        vec_ref[0:1, :] += _colsum(dx2 * z)
        vec_ref[1:2, :] += _colsum(dy * xn3)

    tok = lambda cols: pl.BlockSpec((tb, cols), lambda i: (i, 0))
    tok_t = pl.BlockSpec((D, tb), lambda i: (0, i))
    full = lambda a: pl.BlockSpec(a.shape, lambda i: (0,) * a.ndim)
    small = (modraw, adab, g2, gf)
    return pl.pallas_call(
        body,
        name="mlp_fwd",
        grid=(nb,),
        in_specs=[tok(D), tok(D), tok(D)] + [full(a) for a in small] + [ANY],
        out_specs=[tok_t, tok(D_FF), tok(D), tok(D), pl.BlockSpec((8, D), lambda i: (0, 0)),
                   pl.BlockSpec((8, 128), lambda i: (0, 0))],
        out_shape=[jax.ShapeDtypeStruct((D, t_len), BF16), jax.ShapeDtypeStruct((t_len, D_FF), F32),
                   jax.ShapeDtypeStruct((t_len, D), F32), jax.ShapeDtypeStruct((t_len, D), BF16),
                   jax.ShapeDtypeStruct((8, D), F32), jax.ShapeDtypeStruct((8, 128), F32)],
        scratch_shapes=[pltpu.VMEM((D_FF, D), BF16), pltpu.VMEM((D_FF, D), BF16), pltpu.SemaphoreType.DMA((2,))],
        compiler_params=pltpu.CompilerParams(dimension_semantics=("arbitrary",), vmem_limit_bytes=VMEM_LIMIT),
    )(x, mixed, tgt, *small, wpack)


def _mlp_bwd_half(pos, h2t, f, dz, wpack, prior=None):
    t_len = dz.shape[0]
    tb = TB_MLPB
    nb = t_len // tb
    first = prior is None
    flip = 1 if first else 0

    def body(pos_ref, h2t_ref, f_ref, dz_ref, w1t_ref, w2_ref, *rest):
        if first:
            dh2_ref, dw1_ref, dw2_ref = rest
        else:
            dh2in_ref, send1_hbm, send2_hbm, dh2_ref, dw1_ref, dw2_ref, land1_hbm, land2_hbm, send_sems, recv_sems = rest
            x, y, c = _position()
            copies = [pltpu.make_async_remote_copy(
                src_ref=src, dst_ref=dst, send_sem=send_sems.at[a], recv_sem=recv_sems.at[a], device_id=(x, y, 1 - c),
                device_id_type=MESH) for a, (src, dst) in enumerate(((send1_hbm, land1_hbm), (send2_hbm, land2_hbm)))]
        k = pl.program_id(0)
        t = pl.program_id(1)

        if not first:
            @pl.when((k == 0) & (t == 0))
            def _():
                for cp in copies:
                    cp.start()

        rows = pl.ds(pl.multiple_of(t * tb, tb), tb)
        w1t = w1t_ref[0]
        w2 = w2_ref[0]
        dz = dz_ref[...]
        rf = jnp.maximum(f_ref[...], 0.0)
        df = (_dot_nt(dz, w2) * (2.0 * rf)).astype(BF16)
        dh = _dot(df, w1t)
        g1 = _dot(h2t_ref[...], df)
        g2 = _dot_tn((rf * rf).astype(BF16), dz)

        @pl.when(t == 0)
        def _():
            dw2_ref[0] = g2
            dw1_ref[0] = g1

        @pl.when(t != 0)
        def _():
            dw2_ref[0] += g2
            dw1_ref[0] += g1

        @pl.when(k == 0)
        def _():
            dh2_ref[rows, :] = dh if first else dh2in_ref[...] + dh

        @pl.when(k != 0)
        def _():
            dh2_ref[rows, :] += dh

        if not first:
            @pl.when((k == 3) & (t == nb - 1))
            def _():
                for cp in copies:
                    cp.wait_recv()
                for cp in copies:
                    cp.wait_send()

    blk = lambda k, pos: 2 * k + jnp.bitwise_xor(pos[0], flip)
    in_specs = [pl.BlockSpec((D, tb), lambda k, t, pos: (0, t)),
                pl.BlockSpec((tb, FF_BLK), lambda k, t, pos: (t, blk(k, pos))),
                pl.BlockSpec((tb, D), lambda k, t, pos: (t, 0)),
                pl.BlockSpec((1, ROWS_W1T, D), lambda k, t, pos: (blk(k, pos), OFF_W1T // ROWS_W1T, 0)),
                pl.BlockSpec((1, ROWS_W2, D), lambda k, t, pos: (blk(k, pos), OFF_W2 // ROWS_W2, 0))]
    grad_specs = [pl.BlockSpec((1, D, FF_BLK), lambda k, t, pos: (k, 0, 0)),
                  pl.BlockSpec((1, FF_BLK, D), lambda k, t, pos: (k, 0, 0))]
    out_specs = [pl.BlockSpec((t_len, D), lambda k, t, pos: (0, 0))] + grad_specs
    grad_shapes = [jax.ShapeDtypeStruct((4, D, FF_BLK), F32), jax.ShapeDtypeStruct((4, FF_BLK, D), F32)]
    out_shape = [jax.ShapeDtypeStruct((t_len, D), F32)] + grad_shapes
    args = [pos, h2t, f, dz, wpack, wpack]
    scratch = []
    if not first:
        in_specs += [pl.BlockSpec((tb, D), lambda k, t, pos: (jnp.where(k == 0, t, nb - 1), 0)), ANY, ANY]
        out_specs += [ANY, ANY]
        out_shape += grad_shapes
        args += list(prior)
        scratch = [pltpu.SemaphoreType.DMA((2,)), pltpu.SemaphoreType.DMA((2,))]
    return pl.pallas_call(
        body,
        name="mlp_bwd_first" if first else "mlp_bwd_second",
        grid_spec=pltpu.PrefetchScalarGridSpec(num_scalar_prefetch=1, grid=(4, nb), in_specs=in_specs,
                                               out_specs=out_specs, scratch_shapes=scratch),
        out_shape=out_shape,
        compiler_params=pltpu.CompilerParams(dimension_semantics=("arbitrary", "arbitrary"),
                                             vmem_limit_bytes=VMEM_LIMIT),
    )(*args)


V_SHIFT1, V_SCALE1, V_GATE1, V_SHIFT2, V_SCALE2, V_G1, V_G2 = 0, 1, 2, 3, 4, 6, 7
V_BL_BA, V_BX_SP, V_GL_GC, V_WL01, V_WL23, V_WS01, V_WS2 = 8, 9, 10, 11, 12, 13, 14
V_ROWS = 16


def _chip_scatter_copies(srcs, dsts, send_sems, recv_sems, row_ranges=None):
    me = _position()
    copies = []
    for a, (src, dst) in enumerate(zip(srcs, dsts)):
        band = pl.ds(*row_ranges[a]) if row_ranges else slice(None)
        for j, k in enumerate(CHIP_FLIPS):
            peer = _flip(me, k)
            copies.append(pltpu.make_async_remote_copy(
                src_ref=src.at[2 * peer[0] + peer[1], band], dst_ref=dst.at[j, band],
                send_sem=send_sems.at[len(CHIP_FLIPS) * a + j], recv_sem=recv_sems.at[len(CHIP_FLIPS) * a + j],
                device_id=peer, device_id_type=MESH))
    return copies


def _mixer_bwd(x, mixed, dh2, dx2, proj, hl, kept, modraw, adab, g1, g2, wl, bl, bda, bdx, ba, bxb, ap, ws, gl, gc, avg, wpack,
               chip_sums, chip_rows):
    t_len = x.shape[0]
    tb = TB_MIXB
    nb = t_len // tb
    hb = tb // HALO
    n_sums = len(chip_sums)

    def body(x_ref, mixed_ref, dh2_ref, dx2_ref, proj_ref, projh_ref, hl_ref, hlh_ref, kept_ref,
             modraw_ref, adab_ref, g1_ref, g2_ref, wl_ref, bl_ref, bda_ref, bdx_ref, ba_ref, bxb_ref, ap_ref,
             ws_ref, gl_ref, gc_ref, avg_ref, wpack_hbm, *rest):
        sums_hbm, rest = rest[:n_sums], rest[n_sums:]
        gx_ref, vec_ref, hb_ref, dprojt_ref, dmixed_ref, ycatt_ref, xlt_ref, dgate_ref = rest[:8]
        landed_hbm, rest = rest[8:8 + n_sums], rest[8 + n_sums:]
        (win_v, wout_v, sem, ulx_ext, cv_ext, hl_ext, a_ext, dxl_ext, dcc_ext, dcar, an_s, g_s, dh_s,
         send_sems, recv_sems) = rest
        i = pl.program_id(0)
        blk = nb - 1 - i
        chip_copies = _chip_scatter_copies(sums_hbm, landed_hbm, send_sems, recv_sems, chip_rows)

        @pl.when(i == 0)
        def _():
            for cp in chip_copies:
                cp.start()
            cps = _load_packed(wpack_hbm, OFF_WIN, ROWS_WIN, win_v, sem.at[0])
            cps += _load_packed(wpack_hbm, OFF_WOUT, ROWS_WOUT, wout_v, sem.at[1])
            vec_ref[...] = jnp.zeros(vec_ref.shape, F32)
            zero = jnp.zeros((HALO, W), F32)
            a_ext[tb:tb + HALO, :] = zero
            dxl_ext[tb:tb + HALO, :] = zero
            dcc_ext[tb:tb + HALO, :] = zero
            dcar[...] = zero
            for cp in cps:
                cp.wait()

        mod = modraw_ref[...] + adab_ref[...]
        shift1, scale1, gate1, scale2 = mod[0:1], mod[1:2], mod[2:3], mod[4:5]
        x = x_ref[...]
        mixed = mixed_ref[...]

        x1 = x + gate1 * mixed
        r2 = lax.rsqrt(_rowmean(x1 * x1) + EPS)
        xn2 = x1 * r2
        dh2 = dh2_ref[...]
        vec_ref[V_SHIFT2:V_SHIFT2 + 1, :] += _colsum(dh2)
        vec_ref[V_SCALE2:V_SCALE2 + 1, :] += _colsum(dh2 * xn2 * g2_ref[...])
        vec_ref[V_G2:V_G2 + 1, :] += _colsum(dh2 * (1.0 + scale2) * xn2)
        dxn2 = dh2 * g2_ref[...] * (1.0 + scale2)
        dx1 = dx2_ref[...] + r2 * (dxn2 - xn2 * _rowmean(dxn2 * xn2))
        vec_ref[V_GATE1:V_GATE1 + 1, :] += _colsum(dx1 * mixed)
        dmixed = (gate1 * dx1).astype(BF16)

        proj = proj_ref[...]
        u_lx, u_ly, u_b, u_c, u_v = (proj[:, k * W:(k + 1) * W] for k in range(5))
        has_prev = (blk > 0).astype(F32)
        projh = projh_ref[...]
        ulx_ext[0:HALO, :] = projh[:, 0:W] * has_prev
        ulx_ext[HALO:HALO + tb, :] = u_lx
        xl, r, ig, rp, rq, cc = (kept_ref[:, k * W:(k + 1) * W] for k in range(N_KEPT))
        sp = _softplus(ap_ref[...])
        log_a = (-C_GATE) * r * sp
        a = jnp.exp(log_a)
        mult_raw = jnp.sqrt(_one_minus_sq(a, log_a))
        first = (blk * tb + lax.broadcasted_iota(jnp.int32, (tb, W), 0)) == 0
        mult = jnp.where(first, 1.0, mult_raw)
        hl = hl_ref[...]
        ge, th = _gelu(u_ly)
        pn = ge * hl * rp
        cv = u_c * u_v
        cv_ext[0:HALO, :] = projh[:, 3 * W:4 * W] * projh[:, 4 * W:5 * W] * has_prev
        cv_ext[HALO:HALO + tb, :] = cv
        qn = u_b * cc * rq

        dmixed_ref[...] = dmixed
        ycatt_ref[0:W, :] = (pn * gl_ref[...]).T.astype(BF16)
        ycatt_ref[W:2 * W, :] = (qn * gc_ref[...]).T.astype(BF16)
        dyl = _dot_nt(dmixed, wout_v[0:W, :])
        dyc = _dot_nt(dmixed, wout_v[W:2 * W, :])

        dqn = dyc * gc_ref[...]
        dq = rq * (dqn - qn * _group_mean(dqn * qn, avg_ref[...]))
        du_b = dq * cc
        dcc = dq * u_b
        dcc_ext[0:tb, :] = dcc
        dcv = ws_ref[CONV_S - 1:CONV_S, :] * dcc
        for k in range(CONV_S - 1):
            dcv = dcv + ws_ref[k:k + 1, :] * dcc_ext[pl.ds(CONV_S - 1 - k, tb), :]
        dcc_ext[tb:tb + HALO, :] = dcc_ext[0:HALO, :]
        du_c = dcv * u_v
        du_v = dcv * u_c
        dws = [_colsum(dcc * cv_ext[pl.ds(HALO - (CONV_S - 1) + k, tb), :]) for k in range(CONV_S)]

        dpn = dyl * gl_ref[...]
        dp = rp * (dpn - pn * _group_mean(dpn * pn, avg_ref[...]))
        du_ly = dp * hl * _gelu_grad(u_ly, th)
        g_s[...] = dp * ge
        a_ext[0:tb, :] = a
        an_s[...] = a_ext[pl.ds(1, tb), :]
        _scan_groups(hb, an_s, g_s, dh_s, dcar, reverse=True)
        a_ext[tb:tb + HALO, :] = a_ext[0:HALO, :]
        dh = dh_s[...]
        hl_ext[0:HALO, :] = hlh_ref[...] * has_prev
        hl_ext[HALO:HALO + tb, :] = hl
        da = dh * hl_ext[pl.ds(HALO - 1, tb), :]
        dmult = dh * (ig * xl)
        dig = dh * (mult * xl)
        dxl = dh * (mult * ig)
        dlog = da * a - jnp.where(first, 0.0, dmult * (a * a) / mult_raw)
        dr = dlog * ((-C_GATE) * sp)
        dsp = _colsum(dlog * ((-C_GATE) * r))
        dga = dr * r * (1.0 - r)
        dgx = dig * ig * (1.0 - ig)
        dgab = dga.astype(BF16)
        dgxb = dgx.astype(BF16)
        xlt_ref[...] = xl.T.astype(BF16)
        dgate_ref[:, 0:W] = dgab
        dgate_ref[:, W:2 * W] = dgxb
        dxl = dxl + _dot_nt(dgab, bda_ref[...]) + _dot_nt(dgxb, bdx_ref[...])
        dxl_ext[0:tb, :] = dxl
        du_lx = wl_ref[CONV_L - 1:CONV_L, :] * dxl
        for k in range(CONV_L - 1):
            du_lx = du_lx + wl_ref[k:k + 1, :] * dxl_ext[pl.ds(CONV_L - 1 - k, tb), :]
        dxl_ext[tb:tb + HALO, :] = dxl_ext[0:HALO, :]
        dwl = [_colsum(dxl * ulx_ext[pl.ds(HALO - (CONV_L - 1) + k, tb), :]) for k in range(CONV_L)]

        cat = lambda u, v: jnp.concatenate([u, v], axis=1)
        vec_ref[V_BL_BA:V_BL_BA + 1, :] += cat(_colsum(dxl), _colsum(dga))
        vec_ref[V_BX_SP:V_BX_SP + 1, :] += cat(_colsum(dgx), dsp)
        vec_ref[V_GL_GC:V_GL_GC + 1, :] += cat(_colsum(dyl * pn), _colsum(dyc * qn))
        vec_ref[V_WL01:V_WL01 + 1, :] += cat(dwl[0], dwl[1])
        vec_ref[V_WL23:V_WL23 + 1, :] += cat(dwl[2], dwl[3])
        vec_ref[V_WS01:V_WS01 + 1, :] += cat(dws[0], dws[1])
        vec_ref[V_WS2:V_WS2 + 1, 0:W] += dws[2]

        r1 = lax.rsqrt(_rowmean(x * x) + EPS)
        xn1 = x * r1
        hb_ref[...] = ((xn1 * g1_ref[...]) * (1.0 + scale1) + shift1).astype(BF16)
        dh_in = jnp.zeros((tb, D), F32)
        for k, du in enumerate((du_lx, du_ly, du_b, du_c, du_v)):
            dprojt_ref[k * W:(k + 1) * W, :] = du.T.astype(BF16)
            dh_in = dh_in + _dot(du.astype(BF16), win_v[k * W:(k + 1) * W, :])
        vec_ref[V_SHIFT1:V_SHIFT1 + 1, :] += _colsum(dh_in)
        vec_ref[V_SCALE1:V_SCALE1 + 1, :] += _colsum(dh_in * xn1 * g1_ref[...])
        vec_ref[V_G1:V_G1 + 1, :] += _colsum(dh_in * (1.0 + scale1) * xn1)
        dxn1 = dh_in * g1_ref[...] * (1.0 + scale1)
        gx_ref[...] = dx1 + r1 * (dxn1 - xn1 * _rowmean(dxn1 * xn1))

        @pl.when(i == nb - 1)
        def _():
            for cp in chip_copies:
                cp.wait_recv()
            for cp in chip_copies:
                cp.wait_send()

    rev = lambda cols: pl.BlockSpec((tb, cols), lambda i: (nb - 1 - i, 0))
    rev_t = lambda rows: pl.BlockSpec((rows, tb), lambda i: (0, nb - 1 - i))
    halo = lambda cols: pl.BlockSpec((HALO, cols), lambda i: (jnp.maximum((nb - 1 - i) * hb - 1, 0), 0))
    full = lambda a: pl.BlockSpec(a.shape, lambda i: (0,) * a.ndim)
    small = (modraw, adab, g1, g2, wl, bl, bda, bdx, ba, bxb, ap, ws, gl, gc, avg)
    ext = pltpu.VMEM((tb + HALO, W), F32)
    n_sems = max(len(CHIP_FLIPS) * n_sums, 1)
    return pl.pallas_call(
        body,
        name="mixer_bwd",
        grid=(nb,),
        in_specs=[rev(D), rev(D), rev(D), rev(D), rev(D_IN), halo(D_IN), rev(W), halo(W), rev(N_KEPT * W)]
        + [full(a) for a in small] + [ANY] * (1 + n_sums),
        out_specs=[rev(D), pl.BlockSpec((V_ROWS, D), lambda i: (0, 0)), rev(D), rev_t(D_IN), rev(D), rev_t(D),
                   rev_t(W), rev(2 * W)] + [ANY] * n_sums,
        out_shape=[jax.ShapeDtypeStruct((t_len, D), F32), jax.ShapeDtypeStruct((V_ROWS, D), F32),
                   jax.ShapeDtypeStruct((t_len, D), BF16), jax.ShapeDtypeStruct((D_IN, t_len), BF16),
                   jax.ShapeDtypeStruct((t_len, D), BF16), jax.ShapeDtypeStruct((D, t_len), BF16),
                   jax.ShapeDtypeStruct((W, t_len), BF16), jax.ShapeDtypeStruct((t_len, 2 * W), BF16)]
        + [jax.ShapeDtypeStruct((len(CHIP_FLIPS),) + s.shape[1:], s.dtype) for s in chip_sums],
        scratch_shapes=[pltpu.VMEM((D_IN, D), BF16), pltpu.VMEM((D, D), BF16), pltpu.SemaphoreType.DMA((2,)),
                        ext, ext, ext, ext, ext, ext, pltpu.VMEM((HALO, W), F32),
                        pltpu.VMEM((tb, W), F32), pltpu.VMEM((tb, W), F32), pltpu.VMEM((tb, W), F32),
                        pltpu.SemaphoreType.DMA((n_sems,)), pltpu.SemaphoreType.DMA((n_sems,))],
        compiler_params=pltpu.CompilerParams(dimension_semantics=("arbitrary",), vmem_limit_bytes=VMEM_LIMIT),
    )(x, mixed, dh2, dx2, proj, proj, hl, hl, kept, *small, wpack, *chip_sums)


def _matmul(name, a, b, tm=512):
    m, k = a.shape
    n = b.shape[1]

    def body(a_ref, b_ref, o_ref):
        o_ref[...] = _dot(a_ref[...], b_ref[...])

    return pl.pallas_call(
        body,
        name=name,
        grid=(m // tm,),
        in_specs=[pl.BlockSpec((tm, k), lambda i: (i, 0)), pl.BlockSpec((k, n), lambda i: (0, 0))],
        out_specs=pl.BlockSpec((tm, n), lambda i: (i, 0)),
        out_shape=jax.ShapeDtypeStruct((m, n), F32),
        compiler_params=pltpu.CompilerParams(dimension_semantics=("arbitrary",), vmem_limit_bytes=VMEM_LIMIT),
    )(a, b)


def _gate_wgrad(xl_t, dgate, avg):
    hd = W // 8

    def body(a_ref, b_ref, avg_ref, o_ref):
        full = _dot(a_ref[...], b_ref[...])
        row = lax.broadcasted_iota(jnp.int32, (W, hd), 0)
        col = lax.broadcasted_iota(jnp.int32, (W, hd), 1)
        fold = ((row & (hd - 1)) == col).astype(BF16)
        keep = avg_ref[...] != 0
        for g in range(2):
            m = jnp.where(keep, full[:, g * W:(g + 1) * W], 0.0)
            hi = m.astype(BF16)
            rest = m - hi.astype(F32)
            mid = rest.astype(BF16)
            lo = (rest - mid.astype(F32)).astype(BF16)
            o_ref[g] = _dot(hi, fold) + _dot(mid, fold) + _dot(lo, fold)

    return pl.pallas_call(
        body,
        name="wgrad_gate",
        in_specs=[WHOLE] * 3,
        out_specs=WHOLE,
        out_shape=jax.ShapeDtypeStruct((2, W, hd), F32),
        compiler_params=pltpu.CompilerParams(vmem_limit_bytes=VMEM_LIMIT),
    )(xl_t, dgate, avg)


def _block_diag(w):
    n, m, _ = w.shape
    eye = jnp.eye(n, dtype=w.dtype)
    return (w[:, :, None, :] * eye[:, None, :, None]).reshape(n * m, n * m)


def _pad_rows(a, rows):
    return jnp.pad(a, ((0, rows - a.shape[0]),) + ((0, 0),) * (a.ndim - 1))


def _position():
    return lax.axis_index("x"), lax.axis_index("y"), lax.axis_index("c")


def _linear(pos):
    return 4 * pos[0] + 2 * pos[1] + pos[2]


def _flip(pos, k):
    return tuple(1 - p if k & bit else p for p, bit in zip(pos, (4, 2, 1)))


def _exchange_all(make_copy, make_arrival):
    copies = [make_copy(k) for k in range(1, N_DEV)]
    for cp in copies:
        cp.start()
    for k in range(1, N_DEV):
        make_arrival(k).wait_recv()
    for cp in copies:
        cp.wait_send()


def _mod_exchange_steps(cols):
    def steps(msg_ref, adaw_ref, gath_ref, mod_ref, sendbuf, send_a, recv_a, send_b, recv_b):
        me = _position()
        me_lin = _linear(me)
        m = msg_ref[...]
        row = lax.broadcasted_iota(jnp.int32, m.shape, 0)
        gath_ref[me_lin] = jnp.where(row == 0, m * _sigmoid(m), m)

        def gather_copy(k, src_lin):
            return pltpu.make_async_remote_copy(
                src_ref=gath_ref.at[src_lin], dst_ref=gath_ref.at[src_lin], send_sem=send_a.at[k - 1],
                recv_sem=recv_a.at[k - 1], device_id=_flip(me, k), device_id_type=MESH)

        _exchange_all(lambda k: gather_copy(k, me_lin), lambda k: gather_copy(k, _linear(_flip(me, k))))

        sc_all = gath_ref[:, 0, :]
        scb = jnp.concatenate([sc_all, jnp.zeros_like(sc_all)], axis=0).astype(BF16)
        prod = _dot(scb, adaw_ref[...].astype(BF16))
        for b in range(N_DEV):
            sendbuf[b] = jnp.broadcast_to(prod[b:b + 1, :], (HALO, cols))
        mod_ref[me_lin] = sendbuf[me_lin]

        def row_copy(k, dst_lin):
            peer = _flip(me, k)
            return pltpu.make_async_remote_copy(
                src_ref=sendbuf.at[_linear(peer)], dst_ref=mod_ref.at[dst_lin], send_sem=send_b.at[k - 1],
                recv_sem=recv_b.at[k - 1], device_id=peer, device_id_type=MESH)

        _exchange_all(lambda k: row_copy(k, me_lin), lambda k: row_copy(k, _linear(_flip(me, k))))

    return steps


def _gather_and_mod(msg, ada_w, block):
    rows, cols = block.shape
    mod_cols = ada_w.shape[1]
    mod_steps = _mod_exchange_steps(mod_cols)

    def body(msg_ref, adaw_ref, x_ref, gath_ref, mod_ref, out_ref, sendbuf, send_a, recv_a, send_b, recv_b,
             send_sems, recv_sems, sib_send_sems, sib_recv_sems, local_sem):
        x, y, c = _position()
        me, sibling = (x, y, c), (x, y, 1 - c)
        sends, forward, arrivals = _chip_gather_copies(x_ref, out_ref, send_sems, recv_sems)

        def to_sibling(j, block_of, src=None):
            dst = out_ref.at[_linear(block_of)]
            return pltpu.make_async_remote_copy(
                src_ref=dst if src is None else src, dst_ref=dst, send_sem=sib_send_sems.at[j],
                recv_sem=sib_recv_sems.at[j], device_id=sibling, device_id_type=MESH)

        mine = pltpu.make_async_copy(x_ref, out_ref.at[_linear(me)], local_sem)
        mine.start()
        passes = [to_sibling(0, me, src=x_ref)] + [to_sibling(1 + j, p) for j, p in enumerate(_route_peers(me))]
        passes[0].start()
        for cp in sends:
            cp.start()
        mod_steps(msg_ref, adaw_ref, gath_ref, mod_ref, sendbuf, send_a, recv_a, send_b, recv_b)
        arrivals[0].wait_recv()
        forward.start()
        passes[1].start()
        arrivals[1].wait_recv()
        passes[2].start()
        arrivals[2].wait_recv()
        passes[3].start()
        for j, p in enumerate((sibling,) + _route_peers(sibling)):
            to_sibling(j, p).wait_recv()
        for cp in sends + [forward] + passes:
            cp.wait_send()
        mine.wait()

    return pl.pallas_call(
        body,
        name="gather_and_mod",
        in_specs=[WHOLE, WHOLE, ANY],
        out_specs=[WHOLE, WHOLE, ANY],
        out_shape=[jax.ShapeDtypeStruct((N_DEV, HALO, D), F32), jax.ShapeDtypeStruct((N_DEV, HALO, mod_cols), F32),
                   jax.ShapeDtypeStruct((N_DEV, rows, cols), block.dtype)],
        scratch_shapes=[pltpu.VMEM((N_DEV, HALO, mod_cols), F32)] + [pltpu.SemaphoreType.DMA((N_DEV - 1,))] * 4
        + [pltpu.SemaphoreType.DMA((3,)), pltpu.SemaphoreType.DMA((3,)), pltpu.SemaphoreType.DMA((4,)),
           pltpu.SemaphoreType.DMA((4,)), pltpu.SemaphoreType.DMA],
        compiler_params=pltpu.CompilerParams(vmem_limit_bytes=VMEM_LIMIT),
    )(msg, ada_w, block)


HBM = pl.BlockSpec(memory_space=pltpu.HBM)
SEM = pl.BlockSpec(memory_space=pltpu.SEMAPHORE)
EFFECT = pltpu.SideEffectType.DATAFLOW_SIDE_EFFECTING


def _chips_start(which, chip_sums):
    n = len(chip_sums)
    n_sems = len(CHIP_FLIPS) * n

    def body(*refs):
        srcs, dsts = refs[:n], refs[n:2 * n]
        send_sems, recv_sems = refs[2 * n:2 * n + 2]
        token = refs[-1]
        for cp in _chip_scatter_copies(srcs, dsts, send_sems, recv_sems):
            cp.start()
        token[...] = jnp.zeros(token.shape, token.dtype)

    landing = [jax.ShapeDtypeStruct((len(CHIP_FLIPS),) + s.shape[1:], s.dtype) for s in chip_sums]
    outs = pl.pallas_call(
        body,
        name=which + "_chips_start",
        in_specs=[HBM] * (2 * n),
        out_specs=[SEM, SEM] + [HBM] * (2 * n) + [WHOLE],
        out_shape=[pltpu.SemaphoreType.DMA((n_sems,)), pltpu.SemaphoreType.DMA((n_sems,))]
        + [pltpu.HBM(s.shape, s.dtype) for s in chip_sums] + [pltpu.HBM(s.shape, s.dtype) for s in landing]
        + [jax.ShapeDtypeStruct((HALO, 128), F32)],
        input_output_aliases={i: 2 + i for i in range(2 * n)},
        compiler_params=pltpu.CompilerParams(has_side_effects=EFFECT),
    )(*[pltpu.with_memory_space_constraint(s, pltpu.HBM) for s in chip_sums],
      *[pltpu.with_memory_space_constraint(lax.empty(s.shape, s.dtype), pltpu.HBM) for s in landing])
    return outs[0], outs[1], outs[2:2 + n], outs[2 + n:2 + 2 * n], outs[-1]


def _chips_wait(which, send_sems, recv_sems, srcs, landed, after):
    n = len(srcs)

    def body(*refs):
        src_refs, dst_refs = refs[:n], refs[n:2 * n]
        sends, recvs = refs[2 * n:2 * n + 2]
        copies = _chip_scatter_copies(src_refs, dst_refs, sends, recvs)
        for cp in copies:
            cp.wait_send()
        for cp in copies:
            cp.wait_recv()

    outs = pl.pallas_call(
        body,
        name=which + "_chips_wait",
        in_specs=[HBM] * (2 * n) + [SEM, SEM, ANY],
        out_specs=[HBM] * (2 * n),
        out_shape=[pltpu.HBM(s.shape, s.dtype) for s in list(srcs) + list(landed)],
        input_output_aliases={i: i for i in range(2 * n)},
        compiler_params=pltpu.CompilerParams(has_side_effects=EFFECT),
    )(*srcs, *landed, send_sems, recv_sems, after)
    return list(outs[n:])


def _sibling_copies(srcs, dsts, send_sems, recv_sems):
    x, y, c = _position()
    copies = []
    for a, (src, dst) in enumerate(zip(srcs, dsts)):
        for k in range(4):
            copies.append(pltpu.make_async_remote_copy(
                src_ref=src.at[k, 1 - c], dst_ref=dst.at[k], send_sem=send_sems.at[4 * a + k],
                recv_sem=recv_sems.at[4 * a + k], device_id=(x, y, 1 - c), device_id_type=MESH))
    return copies


def _row_block(rows):
    return 256 if rows % 256 == 0 else rows // 2


def _pair_sum(pos, mine, recv):
    _, cores, rows, cols = mine.shape
    rb = _row_block(rows)

    def body(pos_ref, mine_ref, recv_ref, out_ref):
        out_ref[0] = (mine_ref[0, 0] + recv_ref[0]).astype(BF16)

    other = lambda k, pos: jnp.bitwise_xor(pos[1], k + 1)
    core = lambda pos: pos[0] * (cores - 1)
    return pl.pallas_call(
        body,
        name="grad_pair_sum",
        grid_spec=pltpu.PrefetchScalarGridSpec(
            num_scalar_prefetch=1, grid=(3, rows // rb),
            in_specs=[pl.BlockSpec((1, 1, rb, cols), lambda k, r, pos: (other(k, pos), core(pos), r, 0)),
                      pl.BlockSpec((1, rb, cols), lambda k, r, pos: (other(k, pos), r, 0))],
            out_specs=pl.BlockSpec((1, rb, cols), lambda k, r, pos: (other(k, pos), r, 0))),
        out_shape=jax.ShapeDtypeStruct((4, rows, cols), BF16),
        compiler_params=pltpu.CompilerParams(dimension_semantics=("arbitrary", "arbitrary")),
    )(pos, mine, recv)


def _final_sum(pos, mine, recv, chips):
    _, cores, rows, cols = mine.shape
    rb = _row_block(rows)

    def body(pos_ref, mine_ref, recv_ref, chips_ref, out_ref):
        g = mine_ref[0, 0] + recv_ref[0]
        for j in range(3):
            g = g + chips_ref[j].astype(F32)
        out_ref[...] = g

    return pl.pallas_call(
        body,
        name="grad_final_sum",
        grid_spec=pltpu.PrefetchScalarGridSpec(
            num_scalar_prefetch=1, grid=(rows // rb,),
            in_specs=[pl.BlockSpec((1, 1, rb, cols), lambda r, pos: (pos[1], pos[0] * (cores - 1), r, 0)),
                      pl.BlockSpec((1, rb, cols), lambda r, pos: (pos[1], r, 0)),
                      pl.BlockSpec((3, rb, cols), lambda r, pos: (0, r, 0))],
            out_specs=pl.BlockSpec((rb, cols), lambda r, pos: (r, 0))),
        out_shape=jax.ShapeDtypeStruct((rows, cols), F32),
        compiler_params=pltpu.CompilerParams(dimension_semantics=("arbitrary",)),
    )(pos, mine, recv, chips)


LOSS_ROW = V_ROWS + 8
GB_BASE = LOSS_ROW + 8


def _sibling_and_route(gmod8, sc_t, parts):
    cols = gmod8.shape[1]
    n = len(parts)

    def body(gmod_ref, sct_ref, *refs):
        srcs, (gadaw_ref, gb_ref), dsts = refs[:n], refs[n:n + 2], refs[n + 2:2 * n + 2]
        sendbuf, grecv, send_a, recv_a, sib_send, sib_recv = refs[2 * n + 2:]
        sib_copies = _sibling_copies(srcs, dsts, sib_send, sib_recv)
        for cp in sib_copies:
            cp.start()
        me = _position()
        me_lin = _linear(me)
        gm = gmod_ref[...]
        for b in range(N_DEV):
            sendbuf[b] = jnp.broadcast_to(gm[b:b + 1, :], (HALO, cols))
        grecv[me_lin] = sendbuf[me_lin]

        def row_copy(k, dst_lin):
            peer = _flip(me, k)
            return pltpu.make_async_remote_copy(
                src_ref=sendbuf.at[_linear(peer)], dst_ref=grecv.at[dst_lin], send_sem=send_a.at[k - 1],
                recv_sem=recv_a.at[k - 1], device_id=peer, device_id_type=MESH)

        _exchange_all(lambda k: row_copy(k, me_lin), lambda k: row_copy(k, _linear(_flip(me, k))))
        g_all = grecv[:, 0, :]
        g_pad = jnp.concatenate([g_all, jnp.zeros((sct_ref.shape[1] - N_DEV, cols), F32)], axis=0).astype(BF16)
        gadaw_ref[...] = _dot(sct_ref[...], g_pad)
        gb_ref[...] = jnp.broadcast_to(_colsum(g_all), (HALO, cols))
        for cp in sib_copies:
            cp.wait_recv()
        for cp in sib_copies:
            cp.wait_send()

    return pl.pallas_call(
        body,
        name="sibling_and_route",
        in_specs=[WHOLE, WHOLE] + [ANY] * n,
        out_specs=[WHOLE, WHOLE] + [ANY] * n,
        out_shape=[jax.ShapeDtypeStruct((D, cols), F32), jax.ShapeDtypeStruct((HALO, cols), F32)]
        + [jax.ShapeDtypeStruct((4,) + p.shape[2:], p.dtype) for p in parts],
        scratch_shapes=[pltpu.VMEM((N_DEV, HALO, cols), F32), pltpu.VMEM((N_DEV, HALO, cols), F32),
                        pltpu.SemaphoreType.DMA((N_DEV - 1,)), pltpu.SemaphoreType.DMA((N_DEV - 1,)),
                        pltpu.SemaphoreType.DMA((4 * n,)), pltpu.SemaphoreType.DMA((4 * n,))],
        compiler_params=pltpu.CompilerParams(vmem_limit_bytes=VMEM_LIMIT),
    )(gmod8, sc_t, *parts)


def _chips_and_gather(msg_vec, msg_gate, gb_rows, chip_sums):
    cols = gb_rows.shape[1]
    n = len(chip_sums)
    vec_rows = GB_BASE + N_DEV

    def body(vec_ref, gate_ref, gb_ref, *refs):
        srcs, (sumv_ref, sumg_ref), dsts = refs[:n], refs[n:n + 2], refs[n + 2:2 * n + 2]
        (myv, myg, sibv, sibg, chipv, chipg, sib_send, sib_recv, peer_send, peer_recv,
         chip_send, chip_recv) = refs[2 * n + 2:]
        chip_copies = _chip_scatter_copies(srcs, dsts, chip_send, chip_recv)
        for cp in chip_copies:
            cp.start()
        x, y, c = me = _position()
        my_chip = 2 * x + y
        myv[0:GB_BASE, :] = vec_ref[...]
        slot = lax.broadcasted_iota(jnp.int32, (N_DEV, D), 0) == _linear(me)
        gb_wide = jnp.concatenate([gb_ref[...], jnp.zeros((N_DEV, D - cols), F32)], axis=1)
        myv[GB_BASE:vec_rows, :] = jnp.where(slot, gb_wide, 0.0)
        myg[...] = gate_ref[...]

        swaps = [pltpu.make_async_remote_copy(
            src_ref=src, dst_ref=dst, send_sem=sib_send.at[a], recv_sem=sib_recv.at[a], device_id=(x, y, 1 - c),
            device_id_type=MESH) for a, (src, dst) in enumerate(((myv, sibv), (myg, sibg)))]
        for cp in swaps:
            cp.start()
        for cp in swaps:
            cp.wait_recv()
        chipv[my_chip] = myv[...] + sibv[...]
        chipg[my_chip] = myg[...] + sibg[...]

        def chip_copy(a, buf, j, k, slot_chip):
            peer = _flip(me, k)
            return pltpu.make_async_remote_copy(
                src_ref=buf.at[slot_chip], dst_ref=buf.at[slot_chip], send_sem=peer_send.at[3 * a + j],
                recv_sem=peer_recv.at[3 * a + j], device_id=peer, device_id_type=MESH)

        sends = [chip_copy(a, buf, j, k, my_chip) for a, buf in enumerate((chipv, chipg)) for j, k in enumerate(CHIP_FLIPS)]
        for cp in sends:
            cp.start()
        for a, buf in enumerate((chipv, chipg)):
            for j, k in enumerate(CHIP_FLIPS):
                peer = _flip(me, k)
                chip_copy(a, buf, j, k, 2 * peer[0] + peer[1]).wait_recv()
        sumv_ref[...] = ((chipv[0] + chipv[1]) + chipv[2]) + chipv[3]
        sumg_ref[...] = ((chipg[0] + chipg[1]) + chipg[2]) + chipg[3]
        for cp in swaps + sends:
            cp.wait_send()
        for cp in chip_copies:
            cp.wait_recv()
        for cp in chip_copies:
            cp.wait_send()

    n_chip = max(len(CHIP_FLIPS) * n, 1)
    vshape, gshape = (vec_rows, D), msg_gate.shape
    return pl.pallas_call(
        body,
        name="chips_and_gather",
        in_specs=[WHOLE] * 3 + [ANY] * n,
        out_specs=[WHOLE] * 2 + [ANY] * n,
        out_shape=[jax.ShapeDtypeStruct(vshape, F32), jax.ShapeDtypeStruct(gshape, F32)]
        + [jax.ShapeDtypeStruct((len(CHIP_FLIPS),) + s.shape[1:], s.dtype) for s in chip_sums],
        scratch_shapes=[pltpu.VMEM(vshape, F32), pltpu.VMEM(gshape, F32), pltpu.VMEM(vshape, F32),
                        pltpu.VMEM(gshape, F32), pltpu.VMEM((4,) + vshape, F32), pltpu.VMEM((4,) + gshape, F32),
                        pltpu.SemaphoreType.DMA((2,)), pltpu.SemaphoreType.DMA((2,)),
                        pltpu.SemaphoreType.DMA((2 * len(CHIP_FLIPS),)), pltpu.SemaphoreType.DMA((2 * len(CHIP_FLIPS),)),
                        pltpu.SemaphoreType.DMA((n_chip,)), pltpu.SemaphoreType.DMA((n_chip,))],
        compiler_params=pltpu.CompilerParams(vmem_limit_bytes=VMEM_LIMIT),
    )(msg_vec, msg_gate, gb_rows, *chip_sums)


def _adamw_math(w, g, m, v):
    m = ADAM_B1 * m + (1.0 - ADAM_B1) * g
    v = ADAM_B2 * v + (1.0 - ADAM_B2) * (g * g)
    m_hat = m / (1.0 - ADAM_B1 ** ADAM_STEP)
    v_hat = v / (1.0 - ADAM_B2 ** ADAM_STEP)
    delta = -ADAM_LR * (m_hat / (jnp.sqrt(v_hat) + ADAM_EPS) + ADAM_WD * w)
    return delta, m, v


def _adamw(name, w, g, m, v):
    rows, cols = w.shape
    rb = 256 if rows % 256 == 0 else rows

    def body(w_ref, g_ref, m_ref, v_ref, d_ref, mo_ref, vo_ref):
        d_ref[...], mo_ref[...], vo_ref[...] = _adamw_math(w_ref[...], g_ref[...], m_ref[...], v_ref[...])

    spec = pl.BlockSpec((rb, cols), lambda r: (r, 0))
    return pl.pallas_call(
        body,
        name="adamw_" + name,
        grid=(rows // rb,),
        in_specs=[spec] * 4,
        out_specs=[spec] * 3,
        out_shape=[jax.ShapeDtypeStruct((rows, cols), F32)] * 3,
        compiler_params=pltpu.CompilerParams(dimension_semantics=("arbitrary",)),
    )(w, g, m, v)


def _update(pos, sum_jobs, plain_jobs, after):
    rb = 256
    jobs = [("sum", j) for j in sum_jobs] + [("plain", j) for j in plain_jobs]
    offs, total = [], 0
    for _, j in jobs:
        offs.append(total)
        total += j[-1].shape[0] // rb
    n_in = sum(len(j) for _, j in jobs)

    def body(pos_ref, *refs):
        ins, outs = refs[:n_in], refs[n_in + 1:]
        s = pl.program_id(0)
        i_in = i_out = 0
        for (kind, j), off in zip(jobs, offs):
            steps = j[-1].shape[0] // rb
            j_in = ins[i_in:i_in + len(j)]
            i_in += len(j)
            j_out = outs[i_out:i_out + (4 if kind == "sum" else 3)]
            i_out += len(j_out)

            @pl.when((s >= off) & (s < off + steps))
            def _(kind=kind, j_in=j_in, j_out=j_out):
                if kind == "sum":
                    mine_ref, recv_ref, chips_ref, w_ref, m_ref, v_ref = j_in
                    g = mine_ref[0, 0] + recv_ref[0]
                    for q in range(len(CHIP_FLIPS)):
                        g = g + chips_ref[q].astype(F32)
                    j_out[0][...] = g
                    rest = j_out[1:]
                else:
                    g_ref, w_ref, m_ref, v_ref = j_in
                    g = g_ref[...]
                    rest = j_out
                rest[0][...], rest[1][...], rest[2][...] = _adamw_math(w_ref[...], g, m_ref[...], v_ref[...])

    in_specs, out_specs, out_shape, args = [], [], [], []
    for (kind, j), off in zip(jobs, offs):
        rows, cols = j[-1].shape
        steps = rows // rb
        blk = lambda s, off=off, steps=steps: jnp.clip(s - off, 0, steps - 1)
        flat = pl.BlockSpec((rb, cols), lambda s, pos, blk=blk: (blk(s), 0))
        if kind == "sum":
            in_specs += [pl.BlockSpec((1, 1, rb, cols), lambda s, pos, blk=blk: (pos[1], 0, blk(s), 0)),
                         pl.BlockSpec((1, rb, cols), lambda s, pos, blk=blk: (pos[1], blk(s), 0)),
                         pl.BlockSpec((len(CHIP_FLIPS), rb, cols), lambda s, pos, blk=blk: (0, blk(s), 0))]
            in_specs += [flat] * 3
        else:
            in_specs += [flat] * 4
        n_res = 4 if kind == "sum" else 3
        out_specs += [flat] * n_res
        out_shape += [jax.ShapeDtypeStruct((rows, cols), F32)] * n_res
        args += list(j)
    in_specs += [pl.BlockSpec(after.shape, lambda s, pos: (0,) * after.ndim)]
    outs = pl.pallas_call(
        body,
        name="update",
        grid_spec=pltpu.PrefetchScalarGridSpec(
            num_scalar_prefetch=1, grid=(total,), in_specs=in_specs, out_specs=out_specs),
        out_shape=out_shape,
        compiler_params=pltpu.CompilerParams(dimension_semantics=("arbitrary",), vmem_limit_bytes=VMEM_LIMIT),
    )(pos, *args, after)
    sums = [tuple(outs[4 * i:4 * i + 4]) for i in range(len(sum_jobs))]
    base = 4 * len(sum_jobs)
    plains = [tuple(outs[base + 3 * i:base + 3 * i + 3]) for i in range(len(plain_jobs))]
    return sums, plains


def _adamw_small(ws, gs, ms, vs, sigmoid_scaled):
    n = len(ws)

    def body(*refs):
        w_refs, g_refs, m_refs, v_refs = (refs[i * n:(i + 1) * n] for i in range(4))
        outs = refs[4 * n:]
        for i in range(n):
            w = w_refs[i][...]
            g = g_refs[i][...]
            if sigmoid_scaled[i]:
                g = g * _sigmoid(w)
            delta, m, v = _adamw_math(w, g, m_refs[i][...], v_refs[i][...])
            outs[4 * i][...] = g
            outs[4 * i + 1][...] = delta
            outs[4 * i + 2][...] = m
            outs[4 * i + 3][...] = v

    shapes = [jax.ShapeDtypeStruct(w.shape, F32) for w in ws for _ in range(4)]
    outs = pl.pallas_call(
        body,
        name="adamw_small",
        in_specs=[WHOLE] * (4 * n),
        out_specs=[WHOLE] * (4 * n),
        out_shape=shapes,
    )(*ws, *gs, *ms, *vs)
    return [outs[4 * i:4 * i + 4] for i in range(n)]


_WEIGHT_NAMES = ("ada_w", "ada_b", "norm1_g", "w_in", "lru_conv_w", "lru_conv_b", "gate_a_w", "gate_a_b", "gate_x_w",
                 "gate_x_b", "a_param", "short_conv_w", "lru_out_g", "conv_out_g", "w_out", "norm2_g", "w_mlp1",
                 "w_mlp2", "final_g")


def kernel(x, c, ada_w, ada_b, norm1_g, w_in, lru_conv_w, lru_conv_b, gate_a_w, gate_a_b, gate_x_w, gate_x_b, a_param, short_conv_w, lru_out_g, conv_out_g, w_out, norm2_g, w_mlp1, w_mlp2, final_g, loss_target, m_ada_w, m_ada_b, m_norm1_g, m_w_in, m_lru_conv_w, m_lru_conv_b, m_gate_a_w, m_gate_a_b, m_gate_x_w, m_gate_x_b, m_a_param, m_short_conv_w, m_lru_out_g, m_conv_out_g, m_w_out, m_norm2_g, m_w_mlp1, m_w_mlp2, m_final_g, v_ada_w, v_ada_b, v_norm1_g, v_w_in, v_lru_conv_w, v_lru_conv_b, v_gate_a_w, v_gate_a_b, v_gate_x_w, v_gate_x_b, v_a_param, v_short_conv_w, v_lru_out_g, v_conv_out_g, v_w_out, v_norm2_g, v_w_mlp1, v_w_mlp2, v_final_g):
    given = dict(locals())
    weights = {n: given[n] for n in _WEIGHT_NAMES}
    xi, yi, ci = _position()
    me_lin = _linear((xi, yi, ci))
    hd = W // N_DEV

    mixer_block = jnp.concatenate([w_out[0], w_in[0].T], axis=0).astype(BF16)
    mlp_block = jnp.concatenate([w_mlp1[0].T, w_mlp2[0]], axis=0).astype(BF16)

    msg = (jnp.pad(c, ((0, HALO - 1), (0, 0)))
           + jnp.pad(lru_conv_w[0], ((1, HALO - 1 - CONV_L), (0, D - hd)))
           + jnp.pad(short_conv_w[0], ((1 + CONV_L, 0), (0, D - hd))))
    gath, mod_all, wmix = _gather_and_mod(msg, ada_w[0], mixer_block)
    sc_all = gath[:, 0, :]
    wl = jnp.transpose(gath[:, 1:1 + CONV_L, :hd], (1, 0, 2)).reshape(CONV_L, W)
    ws = jnp.transpose(gath[:, 1 + CONV_L:HALO, :hd], (1, 0, 2)).reshape(CONV_S, W)
    modraw = _pad_rows(mod_all[:, 0, :].reshape(6, D), HALO)
    adab = _pad_rows(ada_b.reshape(6, D), HALO)

    x2d, tgt = x[0], loss_target[0]
    gf = final_g.reshape(1, D)
    bda = _block_diag(gate_a_w[0]).astype(BF16)
    bdx = _block_diag(gate_x_w[0]).astype(BF16)
    avg = _block_diag(jnp.full((8, W // 8, W // 8), 8.0 / W, F32)).astype(BF16)
    wl8 = _pad_rows(wl, HALO)
    ws8 = _pad_rows(ws, HALO)
    mixer_small = (wl8, lru_conv_b, bda, bdx, gate_a_b, gate_x_b, a_param, ws8, lru_out_g, conv_out_g, avg)
    proj, hl, mixed, kept, wmlp = _mixer_fwd(x2d, modraw, adab, norm1_g, *mixer_small, wmix, mlp_block)
    wmlp = _sibling_forward(wmlp)
    h2t, f, dx2, dz, vec2, loss8 = _mlp_fwd(x2d, mixed, tgt, modraw, adab, norm2_g, gf, wmlp)
    pos = jnp.stack([ci, 2 * xi + yi]).astype(jnp.int32)
    by_dest = lambda g: g.reshape((4, 2, -1) + g.shape[-1:])
    dh2, dw1, dw2, sib1, sib2 = _mlp_bwd_half(pos, h2t, f, dz, wmlp, prior=_mlp_bwd_half(pos, h2t, f, dz, wmlp))
    mlp_parts = [dw1[:, None], dw2[:, None]]
    mlp_sib = [sib1, sib2]
    mlp_sums = [_pair_sum(pos, p, r) for p, r in zip(mlp_parts, mlp_sib)]
    mlp_send, mlp_recv, mlp_thru, mlp_land, token = _chips_start("mlp", mlp_sums)
    modraw_after = modraw + jnp.tile(token, (1, D // token.shape[1]))
    gx, vec, hb, dproj_t, dmixed, ycat_t, xl_t, dgate = _mixer_bwd(
        x2d, mixed, dh2, dx2, proj, hl, kept, modraw_after, adab, norm1_g, norm2_g, *mixer_small, wmix, [], None)
    dwint = _matmul("wgrad_in", dproj_t, hb)
    dwout = _matmul("wgrad_out", ycat_t, dmixed)
    gate_blocks = _gate_wgrad(xl_t, dgate, avg)
    msg_gate = gate_blocks.reshape(W, 128)
    done = dwint[0:HALO, 0:128] + dwout[0:HALO, 0:128] + gate_blocks[0, 0:HALO, :].sum() + gx[0:HALO, 0:128]
    mlp_chips = _chips_wait("mlp", mlp_send, mlp_recv, mlp_thru, mlp_land, done)
    mix_parts = [by_dest(dwout), by_dest(dwint)]
    gmod8 = (jnp.pad(vec[0:5], ((0, 1), (0, 0))) + jnp.pad(vec2[0:1], ((5, 0), (0, 0)))).reshape(N_DEV, 6 * D // N_DEV)
    sc_t = jnp.pad(sc_all.T, ((0, 0), (0, 128 - N_DEV))).astype(BF16)
    g_adaw, gb_rows, *mix_sib = _sibling_and_route(gmod8, sc_t, mix_parts)
    mix_sums = [_pair_sum(pos, p, r) for p, r in zip(mix_parts, mix_sib)]
    loss_rows = jnp.pad(loss8[0:1], ((0, HALO - 1), (0, D - loss8.shape[1])))
    msg_vec = jnp.concatenate([vec, vec2, loss_rows], axis=0)
    state = lambda n: (weights[n][0], given["m_" + n][0], given["v_" + n][0])
    mlp_jobs = [(p, r, q, *state(n)) for p, r, q, n in zip(mlp_parts, mlp_sib, mlp_chips, ("w_mlp1", "w_mlp2"))]
    mix_send, mix_recv, mix_thru, mix_land, token = _chips_start("mixer", mix_sums)
    mlp_done, (adaw_done,) = _update(pos, mlp_jobs, [(g_adaw, *state("ada_w"))], token)
    sum_vec, sum_gate = _chips_and_gather(msg_vec, msg_gate, gb_rows, [])
    done = sum_vec[0:HALO, 0:128] + mlp_done[0][2][0:HALO, 0:128] + mlp_done[1][2][0:HALO, 0:128] + adaw_done[1][0:HALO, 0:128]
    mix_chips = _chips_wait("mixer", mix_send, mix_recv, mix_thru, mix_land, done)
    g_wout, g_wint = (_final_sum(pos, p, r, q) for p, r, q in zip(mix_parts, mix_sib, mix_chips))
    loss = sum_vec[LOSS_ROW, 0]
    sum_gate = sum_gate.reshape(2, W, W // 8)
    lo, hi = slice(0, W), slice(W, 2 * W)
    wl_full = sum_vec[V_WL01:V_WL23 + 1].reshape(CONV_L, W)
    ws_full = sum_vec[V_WS01:V_WS2 + 1].reshape(CONV_S + 1, W)[:CONV_S]
    row = lambda r, cols: sum_vec[r:r + 1, cols]
    small_grads = {
        "ada_b": sum_vec[GB_BASE:GB_BASE + N_DEV, :6 * D // N_DEV].reshape(1, 6 * D),
        "norm1_g": row(V_G1, slice(0, D)),
        "lru_conv_w": lax.dynamic_slice(wl_full, (0, me_lin * hd), (CONV_L, hd)),
        "lru_conv_b": row(V_BL_BA, lo),
        "gate_a_w": sum_gate[0],
        "gate_a_b": row(V_BL_BA, hi),
        "gate_x_w": sum_gate[1],
        "gate_x_b": row(V_BX_SP, lo),
        "a_param": row(V_BX_SP, hi),
        "short_conv_w": lax.dynamic_slice(ws_full, (0, me_lin * hd), (CONV_S, hd)),
        "lru_out_g": row(V_GL_GC, lo),
        "conv_out_g": row(V_GL_GC, hi),
        "norm2_g": row(V_G2, slice(0, D)),
        "final_g": sum_vec[V_ROWS + 1:V_ROWS + 2, :],
    }
    names = list(small_grads)
    as2d = lambda a, n: a.reshape(small_grads[n].shape)
    small = _adamw_small([as2d(weights[n], n) for n in names], [small_grads[n] for n in names],
                         [as2d(given["m_" + n], n) for n in names], [as2d(given["v_" + n], n) for n in names],
                         [n == "a_param" for n in names])
    result = {n: tuple(o.reshape(weights[n].shape) for o in outs) for n, outs in zip(names, small)}

    for n, g in (("w_in", g_wint.T), ("w_out", g_wout)):
        w, m, v = state(n)
        result[n] = (g[None],) + tuple(o[None] for o in _adamw(n, w, g, m, v))
    result["w_mlp1"], result["w_mlp2"] = (tuple(o[None] for o in done) for done in mlp_done)
    result["ada_w"] = (g_adaw[None],) + tuple(o[None] for o in adaw_done)

    return (loss, gx[None], *[result[n][0] for n in _WEIGHT_NAMES], *[result[n][1] for n in _WEIGHT_NAMES],
            *[result[n][2] for n in _WEIGHT_NAMES], *[result[n][3] for n in _WEIGHT_NAMES])
```

```python
import functools

import jax
import jax.numpy as jnp
from jax import lax
from jax.experimental import pallas as pl
from jax.experimental.pallas import tpu as pltpu

F32 = jnp.float32
BF16 = jnp.bfloat16
MESH = pl.DeviceIdType.MESH

N_DEV = 8
D = 1024
W = 512
D_IN = 5 * W
D_FF = 4096
FF_BLK = D_FF // N_DEV
EPS = 1e-6
C_GATE = 8.0
CONV_L = 4
CONV_S = 3
HALO = 8

ROWS_W1T, ROWS_W2, ROWS_WOUT, ROWS_WIN = FF_BLK, FF_BLK, D // N_DEV, D_IN // N_DEV
OFF_WOUT = 0
OFF_WIN = OFF_WOUT + ROWS_WOUT
MIX_ROWS = OFF_WIN + ROWS_WIN
OFF_W1T = 0
OFF_W2 = OFF_W1T + ROWS_W1T
MLP_ROWS = OFF_W2 + ROWS_W2
CHIP_FLIPS = (4, 2, 6)
N_KEPT = 6

ADAM_LR = 0.001
ADAM_B1 = 0.9
ADAM_B2 = 0.999
ADAM_EPS = 1e-08
ADAM_WD = 0.01
ADAM_STEP = 10

VMEM_LIMIT = 56 * 1024 * 1024

TB_MIX = 256
TB_MIXB = 256
TB_MLP = 256
TB_MLPB = 512

ANY = pl.BlockSpec(memory_space=pl.ANY)
WHOLE = pl.BlockSpec(memory_space=pltpu.VMEM)


def _dot(a, b):
    return jnp.dot(a, b, preferred_element_type=F32)


def _dot_nt(a, b):
    return lax.dot_general(a, b, (((1,), (1,)), ((), ())), preferred_element_type=F32)


def _dot_tn(a, b):
    return lax.dot_general(a, b, (((0,), (0,)), ((), ())), preferred_element_type=F32)


def _sigmoid(v):
    return 1.0 / (1.0 + jnp.exp(-v))


def _softplus(v):
    t = jnp.exp(-jnp.abs(v))
    small = t * (1.0 - t * (0.5 - t * (1.0 / 3.0)))
    return jnp.maximum(v, 0.0) + jnp.where(t < 1e-2, small, jnp.log(1.0 + t))


def _one_minus_sq(a, log_a):
    return -jnp.tanh(log_a) * (a * a + 1.0)


_GELU_K = 0.7978845608028654
_GELU_C = 0.044715


def _gelu(u):
    th = jnp.tanh(_GELU_K * (u + _GELU_C * u * u * u))
    return 0.5 * u * (1.0 + th), th


def _gelu_grad(u, th):
    return 0.5 * (1.0 + th) + 0.5 * u * (1.0 - th * th) * _GELU_K * (1.0 + 3.0 * _GELU_C * u * u)


def _group_mean(v, avg):
    hi = v.astype(BF16)
    lo = (v - hi.astype(F32)).astype(BF16)
    return _dot(hi, avg) + _dot(lo, avg)


def _colsum(v):
    return jnp.sum(v, axis=0, keepdims=True)


def _rowmean(v):
    return jnp.mean(v, axis=-1, keepdims=True)


def _load_packed(wpack_hbm, off, rows, dst, sem):
    copies = [
        pltpu.make_async_copy(wpack_hbm.at[d, pl.ds(off, rows), :], dst.at[pl.ds(d * rows, rows), :], sem)
        for d in range(N_DEV)
    ]
    for cp in copies:
        cp.start()
    return copies


def _scan_groups(n_groups, a_ref, b_ref, out_ref, carry_ref, reverse):
    row = lax.broadcasted_iota(jnp.int32, (HALO, W), 0)

    def step(k, carry):
        g = (n_groups - 1 - k) if reverse else k
        rows = pl.ds(pl.multiple_of(g * HALO, HALO), HALO)
        a = a_ref[rows, :]
        b = b_ref[rows, :]
        for s in (1, 2, 4):
            if reverse:
                keep = row < HALO - s
                sh = HALO - s
            else:
                keep = row >= s
                sh = s
            a_sh = pltpu.roll(a, sh, axis=0)
            b_sh = pltpu.roll(b, sh, axis=0)
            b = jnp.where(keep, a * b_sh + b, b)
            a = jnp.where(keep, a * a_sh, a)
        h = b + a * carry
        out_ref[rows, :] = h
        edge = h[0:1, :] if reverse else h[HALO - 1:HALO, :]
        return jnp.broadcast_to(edge, (HALO, W))

    carry_ref[...] = lax.fori_loop(0, n_groups, step, carry_ref[...])


def _route_peers(me):
    x, y, c = me
    first = ((x + 1 - c) % 2, (y + c) % 2, c)
    second = ((x + c) % 2, (y + 1 - c) % 2, c)
    return first, second, (1 - x, 1 - y, c)


def _chip_gather_copies(block_hbm, out_hbm, send_sems, recv_sems):
    me = _position()
    first, second, diag = _route_peers(me)

    def copy(j, src, slot_of, to):
        return pltpu.make_async_remote_copy(
            src_ref=src, dst_ref=out_hbm.at[_linear(slot_of)], send_sem=send_sems.at[j], recv_sem=recv_sems.at[j],
            device_id=to, device_id_type=MESH)

    own_sends = [copy(0, block_hbm, me, first), copy(1, block_hbm, me, second)]
    forward = copy(2, out_hbm.at[_linear(first)], first, second)
    arrivals = [copy(0, block_hbm, first, first), copy(1, block_hbm, second, second), copy(2, block_hbm, diag, second)]
    return own_sends, forward, arrivals


def _mixer_fwd(x, modraw, adab, g1, wl, bl, bda, bdx, ba, bxb, ap, ws, gl, gc, avg, wpack, mlp_block):
    t_len = x.shape[0]
    tb = TB_MIX
    nb = t_len // tb

    def body(x_ref, modraw_ref, adab_ref, g1_ref, wl_ref, bl_ref, bda_ref, bdx_ref, ba_ref, bxb_ref, ap_ref,
             ws_ref, gl_ref, gc_ref, avg_ref, wpack_hbm, block_hbm, proj_ref, hl_ref, mixed_ref, kept_ref, wmlp_hbm,
             win_v, wout_v, sem, ulx_ext, cv_ext, hcar, a_s, b_s, send_sems, recv_sems, local_sem):
        i = pl.program_id(0)
        own = pltpu.make_async_copy(block_hbm, wmlp_hbm.at[_linear(_position())], local_sem)
        sends, forward, arrivals = _chip_gather_copies(block_hbm, wmlp_hbm, send_sems, recv_sems)

        @pl.when(i == 0)
        def _():
            own.start()
            for cp in sends:
                cp.start()

        @pl.when(i == nb - 1)
        def _():
            arrivals[0].wait_recv()
            forward.start()

        @pl.when(i == 0)
        def _():
            cps = _load_packed(wpack_hbm, OFF_WIN, ROWS_WIN, win_v, sem.at[0])
            cps += _load_packed(wpack_hbm, OFF_WOUT, ROWS_WOUT, wout_v, sem.at[1])
            ulx_ext[0:HALO, :] = jnp.zeros((HALO, W), F32)
            cv_ext[0:HALO, :] = jnp.zeros((HALO, W), F32)
            hcar[...] = jnp.zeros((HALO, W), F32)
            for cp in cps:
                cp.wait()

        mod = modraw_ref[...] + adab_ref[...]
        shift1, scale1, gate1 = mod[0:1], mod[1:2], mod[2:3]
        x = x_ref[...]
        r1 = lax.rsqrt(_rowmean(x * x) + EPS)
        h = (x * r1 * g1_ref[...]) * (1.0 + scale1) + shift1
        proj = _dot_nt(h.astype(BF16), win_v[...])
        proj_ref[...] = proj
        u_lx, u_ly, u_b, u_c, u_v = (proj[:, k * W:(k + 1) * W] for k in range(5))

        ulx_ext[HALO:HALO + tb, :] = u_lx
        xl = bl_ref[...] + wl_ref[CONV_L - 1:CONV_L, :] * u_lx
        for k in range(CONV_L - 1):
            xl = xl + wl_ref[k:k + 1, :] * ulx_ext[pl.ds(HALO - (CONV_L - 1) + k, tb), :]
        ulx_ext[0:HALO, :] = ulx_ext[tb:tb + HALO, :]
        xlb = xl.astype(BF16)
        r = _sigmoid(_dot(xlb, bda_ref[...]) + ba_ref[...])
        ig = _sigmoid(_dot(xlb, bdx_ref[...]) + bxb_ref[...])
        log_a = (-C_GATE) * r * _softplus(ap_ref[...])
        a = jnp.exp(log_a)
        mult = jnp.sqrt(_one_minus_sq(a, log_a))
        grow = i * tb + lax.broadcasted_iota(jnp.int32, (tb, W), 0)
        mult = jnp.where(grow == 0, 1.0, mult)
        a_s[...] = a
        b_s[...] = mult * (ig * xl)
        _scan_groups(tb // HALO, a_s, b_s, hl_ref, hcar, reverse=False)
        hl = hl_ref[...]
        ge, _ = _gelu(u_ly)
        p = ge * hl
        rp = lax.rsqrt(_group_mean(p * p, avg_ref[...]) + EPS)
        y_lru = p * rp * gl_ref[...]

        cv = u_c * u_v
        cv_ext[HALO:HALO + tb, :] = cv
        cc = ws_ref[CONV_S - 1:CONV_S, :] * cv
        for k in range(CONV_S - 1):
            cc = cc + ws_ref[k:k + 1, :] * cv_ext[pl.ds(HALO - (CONV_S - 1) + k, tb), :]
        cv_ext[0:HALO, :] = cv_ext[tb:tb + HALO, :]
        q = u_b * cc
        rq = lax.rsqrt(_group_mean(q * q, avg_ref[...]) + EPS)
        y_conv = q * rq * gc_ref[...]
        for k, kept in enumerate((xl, r, ig, rp, rq, cc)):
            kept_ref[:, k * W:(k + 1) * W] = kept

        mixed_ref[...] = (_dot(y_lru.astype(BF16), wout_v[0:W, :]) + _dot(y_conv.astype(BF16), wout_v[W:2 * W, :]))

        @pl.when(i == nb - 1)
        def _():
            for cp in arrivals[1:]:
                cp.wait_recv()
            for cp in sends + [forward]:
                cp.wait_send()
            own.wait()

    tok = lambda cols: pl.BlockSpec((tb, cols), lambda i: (i, 0))
    full = lambda a: pl.BlockSpec(a.shape, lambda i: (0,) * a.ndim)
    small = (modraw, adab, g1, wl, bl, bda, bdx, ba, bxb, ap, ws, gl, gc, avg)
    n_chips = len(CHIP_FLIPS)
    return pl.pallas_call(
        body,
        name="mixer_fwd",
        grid=(nb,),
        in_specs=[tok(D)] + [full(a) for a in small] + [ANY, ANY],
        out_specs=[tok(D_IN), tok(W), tok(D), tok(N_KEPT * W), ANY],
        out_shape=[jax.ShapeDtypeStruct((t_len, D_IN), F32), jax.ShapeDtypeStruct((t_len, W), F32),
                   jax.ShapeDtypeStruct((t_len, D), F32), jax.ShapeDtypeStruct((t_len, N_KEPT * W), F32),
                   jax.ShapeDtypeStruct((N_DEV,) + mlp_block.shape, BF16)],
        scratch_shapes=[pltpu.VMEM((D_IN, D), BF16), pltpu.VMEM((D, D), BF16), pltpu.SemaphoreType.DMA((2,)),
                        pltpu.VMEM((tb + HALO, W), F32), pltpu.VMEM((tb + HALO, W), F32), pltpu.VMEM((HALO, W), F32),
                        pltpu.VMEM((tb, W), F32), pltpu.VMEM((tb, W), F32),
                        pltpu.SemaphoreType.DMA((n_chips,)), pltpu.SemaphoreType.DMA((n_chips,)), pltpu.SemaphoreType.DMA],
        compiler_params=pltpu.CompilerParams(dimension_semantics=("arbitrary",), vmem_limit_bytes=VMEM_LIMIT),
    )(x, *small, wpack, mlp_block)


def _sibling_forward(wmlp):
    def body(in_hbm, out_hbm, send_sems, recv_sems):
        x, y, c = _position()
        copies, arrivals = [], []
        for j, k in enumerate((0,) + CHIP_FLIPS):
            mine = out_hbm.at[_linear(_flip((x, y, c), k))]
            theirs = out_hbm.at[_linear(_flip((x, y, 1 - c), k))]
            copies.append(pltpu.make_async_remote_copy(
                src_ref=mine, dst_ref=mine, send_sem=send_sems.at[j], recv_sem=recv_sems.at[j],
                device_id=(x, y, 1 - c), device_id_type=MESH))
            arrivals.append(pltpu.make_async_remote_copy(
                src_ref=theirs, dst_ref=theirs, send_sem=send_sems.at[j], recv_sem=recv_sems.at[j],
                device_id=(x, y, 1 - c), device_id_type=MESH))
        for cp in copies:
            cp.start()
        for cp in arrivals:
            cp.wait_recv()
        for cp in copies:
            cp.wait_send()

    return pl.pallas_call(
        body,
        name="sibling_forward",
        in_specs=[ANY],
        out_specs=ANY,
        out_shape=jax.ShapeDtypeStruct(wmlp.shape, wmlp.dtype),
        input_output_aliases={0: 0},
        scratch_shapes=[pltpu.SemaphoreType.DMA((4,)), pltpu.SemaphoreType.DMA((4,))],
    )(wmlp)


def _mlp_fwd(x, mixed, tgt, modraw, adab, g2, gf, wpack):
    t_len = x.shape[0]
    tb = TB_MLP
    nb = t_len // tb

    def body(x_ref, mixed_ref, tgt_ref, modraw_ref, adab_ref, g2_ref, gf_ref, wpack_hbm,
             h2t_ref, f_ref, dx2_ref, dz_ref, vec_ref, loss_ref, w1t_v, w2_v, sem):
        i = pl.program_id(0)

        @pl.when(i == 0)
        def _():
            cps = _load_packed(wpack_hbm, OFF_W1T, ROWS_W1T, w1t_v, sem.at[0])
            cps += _load_packed(wpack_hbm, OFF_W2, ROWS_W2, w2_v, sem.at[1])
            vec_ref[...] = jnp.zeros(vec_ref.shape, F32)
            loss_ref[...] = jnp.zeros(loss_ref.shape, F32)
            for cp in cps:
                cp.wait()

        mod = modraw_ref[...] + adab_ref[...]
        gate1, shift2, scale2, gate2 = mod[2:3], mod[3:4], mod[4:5], mod[5:6]
        x1 = x_ref[...] + gate1 * mixed_ref[...]
        r2 = lax.rsqrt(_rowmean(x1 * x1) + EPS)
        h2 = (x1 * r2 * g2_ref[...]) * (1.0 + scale2) + shift2
        h2b = h2.astype(BF16)
        h2t_ref[...] = h2.T.astype(BF16)
        z = jnp.zeros((tb, D), F32)
        for j in range(N_DEV):
            cols = slice(j * FF_BLK, (j + 1) * FF_BLK)
            fj = _dot_nt(h2b, w1t_v[cols, :])
            f_ref[:, cols] = fj
            rf = jnp.maximum(fj, 0.0)
            z = z + _dot((rf * rf).astype(BF16), w2_v[cols, :])
        x2 = x1 + gate2 * z
        r3 = lax.rsqrt(_rowmean(x2 * x2) + EPS)
        xn3 = x2 * r3
        diff = xn3 * gf_ref[...] - tgt_ref[...]
        sq = _colsum(diff * diff)
        loss_ref[...] += jnp.broadcast_to(jnp.sum(sq, axis=1, keepdims=True) * (0.5 / D), loss_ref.shape)
        dy = diff * (1.0 / D)
        dyn = dy * gf_ref[...]
        dx2 = r3 * (dyn - xn3 * _rowmean(dyn * xn3))
        dx2_ref[...] = dx2
        dz_ref[...] = (gate2 * dx2).astype(BF16)
        vec_ref[0:1, :] += _colsum(dx2 * z)
        vec_ref[1:2, :] += _colsum(dy * xn3)

    tok = lambda cols: pl.BlockSpec((tb, cols), lambda i: (i, 0))
    tok_t = pl.BlockSpec((D, tb), lambda i: (0, i))
    full = lambda a: pl.BlockSpec(a.shape, lambda i: (0,) * a.ndim)
    small = (modraw, adab, g2, gf)
    return pl.pallas_call(
        body,
        name="mlp_fwd",
        grid=(nb,),
        in_specs=[tok(D), tok(D), tok(D)] + [full(a) for a in small] + [ANY],
        out_specs=[tok_t, tok(D_FF), tok(D), tok(D), pl.BlockSpec((8, D), lambda i: (0, 0)),
                   pl.BlockSpec((8, 128), lambda i: (0, 0))],
        out_shape=[jax.ShapeDtypeStruct((D, t_len), BF16), jax.ShapeDtypeStruct((t_len, D_FF), F32),
                   jax.ShapeDtypeStruct((t_len, D), F32), jax.ShapeDtypeStruct((t_len, D), BF16),
                   jax.ShapeDtypeStruct((8, D), F32), jax.ShapeDtypeStruct((8, 128), F32)],
        scratch_shapes=[pltpu.VMEM((D_FF, D), BF16), pltpu.VMEM((D_FF, D), BF16), pltpu.SemaphoreType.DMA((2,))],
        compiler_params=pltpu.CompilerParams(dimension_semantics=("arbitrary",), vmem_limit_bytes=VMEM_LIMIT),
    )(x, mixed, tgt, *small, wpack)


def _mlp_bwd_half(pos, h2t, f, dz, wpack, prior=None):
    t_len = dz.shape[0]
    tb = TB_MLPB
    nb = t_len // tb
    first = prior is None
    flip = 1 if first else 0

    def body(pos_ref, h2t_ref, f_ref, dz_ref, w1t_ref, w2_ref, *rest):
        if first:
            dh2_ref, dw1_ref, dw2_ref = rest
        else:
            dh2in_ref, _, dh2_ref, dw1_ref, dw2_ref = rest
        k = pl.program_id(0)
        t = pl.program_id(1)
        rows = pl.ds(pl.multiple_of(t * tb, tb), tb)
        w1t = w1t_ref[0]
        w2 = w2_ref[0]
        dz = dz_ref[...]
        rf = jnp.maximum(f_ref[...], 0.0)
        df = (_dot_nt(dz, w2) * (2.0 * rf)).astype(BF16)
        dh = _dot(df, w1t)
        g1 = _dot(h2t_ref[...], df)
        g2 = _dot_tn((rf * rf).astype(BF16), dz)

        @pl.when(t == 0)
        def _():
            dw2_ref[0] = g2
            dw1_ref[0] = g1

        @pl.when(t != 0)
        def _():
            dw2_ref[0] += g2
            dw1_ref[0] += g1

        @pl.when(k == 0)
        def _():
            dh2_ref[rows, :] = dh if first else dh2in_ref[...] + dh

        @pl.when(k != 0)
        def _():
            dh2_ref[rows, :] += dh

    blk = lambda k, pos: 2 * k + jnp.bitwise_xor(pos[0], flip)
    in_specs = [pl.BlockSpec((D, tb), lambda k, t, pos: (0, t)),
                pl.BlockSpec((tb, FF_BLK), lambda k, t, pos: (t, blk(k, pos))),
                pl.BlockSpec((tb, D), lambda k, t, pos: (t, 0)),
                pl.BlockSpec((1, ROWS_W1T, D), lambda k, t, pos: (blk(k, pos), OFF_W1T // ROWS_W1T, 0)),
                pl.BlockSpec((1, ROWS_W2, D), lambda k, t, pos: (blk(k, pos), OFF_W2 // ROWS_W2, 0))]
    grad_specs = [pl.BlockSpec((1, D, FF_BLK), lambda k, t, pos: (k, 0, 0)),
                  pl.BlockSpec((1, FF_BLK, D), lambda k, t, pos: (k, 0, 0))]
    out_specs = [pl.BlockSpec((t_len, D), lambda k, t, pos: (0, 0))] + grad_specs
    grad_shapes = [jax.ShapeDtypeStruct((4, D, FF_BLK), F32), jax.ShapeDtypeStruct((4, FF_BLK, D), F32)]
    out_shape = [jax.ShapeDtypeStruct((t_len, D), F32)] + grad_shapes
    args = [pos, h2t, f, dz, wpack, wpack]
    if not first:
        in_specs += [pl.BlockSpec((tb, D), lambda k, t, pos: (jnp.where(k == 0, t, nb - 1), 0)),
                     pl.BlockSpec(prior[1].shape, lambda k, t, pos: (0,) * prior[1].ndim)]
        args += list(prior)
    return pl.pallas_call(
        body,
        name="mlp_bwd_first" if first else "mlp_bwd_second",
        grid_spec=pltpu.PrefetchScalarGridSpec(num_scalar_prefetch=1, grid=(4, nb), in_specs=in_specs,
                                               out_specs=out_specs),
        out_shape=out_shape,
        compiler_params=pltpu.CompilerParams(dimension_semantics=("arbitrary", "arbitrary"),
                                             vmem_limit_bytes=VMEM_LIMIT),
    )(*args)


V_SHIFT1, V_SCALE1, V_GATE1, V_SHIFT2, V_SCALE2, V_G1, V_G2 = 0, 1, 2, 3, 4, 6, 7
V_BL_BA, V_BX_SP, V_GL_GC, V_WL01, V_WL23, V_WS01, V_WS2 = 8, 9, 10, 11, 12, 13, 14
V_ROWS = 16


def _chip_scatter_copies(srcs, dsts, send_sems, recv_sems, row_ranges=None):
    me = _position()
    copies = []
    for a, (src, dst) in enumerate(zip(srcs, dsts)):
        band = pl.ds(*row_ranges[a]) if row_ranges else slice(None)
        for j, k in enumerate(CHIP_FLIPS):
            peer = _flip(me, k)
            copies.append(pltpu.make_async_remote_copy(
                src_ref=src.at[2 * peer[0] + peer[1], band], dst_ref=dst.at[j, band],
                send_sem=send_sems.at[len(CHIP_FLIPS) * a + j], recv_sem=recv_sems.at[len(CHIP_FLIPS) * a + j],
                device_id=peer, device_id_type=MESH))
    return copies


def _mixer_bwd(x, mixed, dh2, dx2, proj, hl, kept, modraw, adab, g1, g2, wl, bl, bda, bdx, ba, bxb, ap, ws, gl, gc, avg, wpack,
               chip_sums, chip_rows):
    t_len = x.shape[0]
    tb = TB_MIXB
    nb = t_len // tb
    hb = tb // HALO
    n_sums = len(chip_sums)

    def body(x_ref, mixed_ref, dh2_ref, dx2_ref, proj_ref, projh_ref, hl_ref, hlh_ref, kept_ref,
             modraw_ref, adab_ref, g1_ref, g2_ref, wl_ref, bl_ref, bda_ref, bdx_ref, ba_ref, bxb_ref, ap_ref,
             ws_ref, gl_ref, gc_ref, avg_ref, wpack_hbm, *rest):
        sums_hbm, rest = rest[:n_sums], rest[n_sums:]
        gx_ref, vec_ref, hb_ref, dprojt_ref, dmixed_ref, ycatt_ref, xlt_ref, dgate_ref = rest[:8]
        landed_hbm, rest = rest[8:8 + n_sums], rest[8 + n_sums:]
        (win_v, wout_v, sem, ulx_ext, cv_ext, hl_ext, a_ext, dxl_ext, dcc_ext, dcar, an_s, g_s, dh_s,
         send_sems, recv_sems) = rest
        i = pl.program_id(0)
        blk = nb - 1 - i
        chip_copies = _chip_scatter_copies(sums_hbm, landed_hbm, send_sems, recv_sems, chip_rows)

        @pl.when(i == 0)
        def _():
            for cp in chip_copies:
                cp.start()
            cps = _load_packed(wpack_hbm, OFF_WIN, ROWS_WIN, win_v, sem.at[0])
            cps += _load_packed(wpack_hbm, OFF_WOUT, ROWS_WOUT, wout_v, sem.at[1])
            vec_ref[...] = jnp.zeros(vec_ref.shape, F32)
            zero = jnp.zeros((HALO, W), F32)
            a_ext[tb:tb + HALO, :] = zero
            dxl_ext[tb:tb + HALO, :] = zero
            dcc_ext[tb:tb + HALO, :] = zero
            dcar[...] = zero
            for cp in cps:
                cp.wait()

        mod = modraw_ref[...] + adab_ref[...]
        shift1, scale1, gate1, scale2 = mod[0:1], mod[1:2], mod[2:3], mod[4:5]
        x = x_ref[...]
        mixed = mixed_ref[...]

        x1 = x + gate1 * mixed
        r2 = lax.rsqrt(_rowmean(x1 * x1) + EPS)
        xn2 = x1 * r2
        dh2 = dh2_ref[...]
        vec_ref[V_SHIFT2:V_SHIFT2 + 1, :] += _colsum(dh2)
        vec_ref[V_SCALE2:V_SCALE2 + 1, :] += _colsum(dh2 * xn2 * g2_ref[...])
        vec_ref[V_G2:V_G2 + 1, :] += _colsum(dh2 * (1.0 + scale2) * xn2)
        dxn2 = dh2 * g2_ref[...] * (1.0 + scale2)
        dx1 = dx2_ref[...] + r2 * (dxn2 - xn2 * _rowmean(dxn2 * xn2))
        vec_ref[V_GATE1:V_GATE1 + 1, :] += _colsum(dx1 * mixed)
        dmixed = (gate1 * dx1).astype(BF16)

        proj = proj_ref[...]
        u_lx, u_ly, u_b, u_c, u_v = (proj[:, k * W:(k + 1) * W] for k in range(5))
        has_prev = (blk > 0).astype(F32)
        projh = projh_ref[...]
        ulx_ext[0:HALO, :] = projh[:, 0:W] * has_prev
        ulx_ext[HALO:HALO + tb, :] = u_lx
        xl, r, ig, rp, rq, cc = (kept_ref[:, k * W:(k + 1) * W] for k in range(N_KEPT))
        sp = _softplus(ap_ref[...])
        log_a = (-C_GATE) * r * sp
        a = jnp.exp(log_a)
        mult_raw = jnp.sqrt(_one_minus_sq(a, log_a))
        first = (blk * tb + lax.broadcasted_iota(jnp.int32, (tb, W), 0)) == 0
        mult = jnp.where(first, 1.0, mult_raw)
        hl = hl_ref[...]
        ge, th = _gelu(u_ly)
        pn = ge * hl * rp
        cv = u_c * u_v
        cv_ext[0:HALO, :] = projh[:, 3 * W:4 * W] * projh[:, 4 * W:5 * W] * has_prev
        cv_ext[HALO:HALO + tb, :] = cv
        qn = u_b * cc * rq

        dmixed_ref[...] = dmixed
        ycatt_ref[0:W, :] = (pn * gl_ref[...]).T.astype(BF16)
        ycatt_ref[W:2 * W, :] = (qn * gc_ref[...]).T.astype(BF16)
        dyl = _dot_nt(dmixed, wout_v[0:W, :])
        dyc = _dot_nt(dmixed, wout_v[W:2 * W, :])

        dqn = dyc * gc_ref[...]
        dq = rq * (dqn - qn * _group_mean(dqn * qn, avg_ref[...]))
        du_b = dq * cc
        dcc = dq * u_b
        dcc_ext[0:tb, :] = dcc
        dcv = ws_ref[CONV_S - 1:CONV_S, :] * dcc
        for k in range(CONV_S - 1):
            dcv = dcv + ws_ref[k:k + 1, :] * dcc_ext[pl.ds(CONV_S - 1 - k, tb), :]
        dcc_ext[tb:tb + HALO, :] = dcc_ext[0:HALO, :]
        du_c = dcv * u_v
        du_v = dcv * u_c
        dws = [_colsum(dcc * cv_ext[pl.ds(HALO - (CONV_S - 1) + k, tb), :]) for k in range(CONV_S)]

        dpn = dyl * gl_ref[...]
        dp = rp * (dpn - pn * _group_mean(dpn * pn, avg_ref[...]))
        du_ly = dp * hl * _gelu_grad(u_ly, th)
        g_s[...] = dp * ge
        a_ext[0:tb, :] = a
        an_s[...] = a_ext[pl.ds(1, tb), :]
        _scan_groups(hb, an_s, g_s, dh_s, dcar, reverse=True)
        a_ext[tb:tb + HALO, :] = a_ext[0:HALO, :]
        dh = dh_s[...]
        hl_ext[0:HALO, :] = hlh_ref[...] * has_prev
        hl_ext[HALO:HALO + tb, :] = hl
        da = dh * hl_ext[pl.ds(HALO - 1, tb), :]
        dmult = dh * (ig * xl)
        dig = dh * (mult * xl)
        dxl = dh * (mult * ig)
        dlog = da * a - jnp.where(first, 0.0, dmult * (a * a) / mult_raw)
        dr = dlog * ((-C_GATE) * sp)
        dsp = _colsum(dlog * ((-C_GATE) * r))
        dga = dr * r * (1.0 - r)
        dgx = dig * ig * (1.0 - ig)
        dgab = dga.astype(BF16)
        dgxb = dgx.astype(BF16)
        xlt_ref[...] = xl.T.astype(BF16)
        dgate_ref[:, 0:W] = dgab
        dgate_ref[:, W:2 * W] = dgxb
        dxl = dxl + _dot_nt(dgab, bda_ref[...]) + _dot_nt(dgxb, bdx_ref[...])
        dxl_ext[0:tb, :] = dxl
        du_lx = wl_ref[CONV_L - 1:CONV_L, :] * dxl
        for k in range(CONV_L - 1):
            du_lx = du_lx + wl_ref[k:k + 1, :] * dxl_ext[pl.ds(CONV_L - 1 - k, tb), :]
        dxl_ext[tb:tb + HALO, :] = dxl_ext[0:HALO, :]
        dwl = [_colsum(dxl * ulx_ext[pl.ds(HALO - (CONV_L - 1) + k, tb), :]) for k in range(CONV_L)]

        cat = lambda u, v: jnp.concatenate([u, v], axis=1)
        vec_ref[V_BL_BA:V_BL_BA + 1, :] += cat(_colsum(dxl), _colsum(dga))
        vec_ref[V_BX_SP:V_BX_SP + 1, :] += cat(_colsum(dgx), dsp)
        vec_ref[V_GL_GC:V_GL_GC + 1, :] += cat(_colsum(dyl * pn), _colsum(dyc * qn))
        vec_ref[V_WL01:V_WL01 + 1, :] += cat(dwl[0], dwl[1])
        vec_ref[V_WL23:V_WL23 + 1, :] += cat(dwl[2], dwl[3])
        vec_ref[V_WS01:V_WS01 + 1, :] += cat(dws[0], dws[1])
        vec_ref[V_WS2:V_WS2 + 1, 0:W] += dws[2]

        r1 = lax.rsqrt(_rowmean(x * x) + EPS)
        xn1 = x * r1
        hb_ref[...] = ((xn1 * g1_ref[...]) * (1.0 + scale1) + shift1).astype(BF16)
        dh_in = jnp.zeros((tb, D), F32)
        for k, du in enumerate((du_lx, du_ly, du_b, du_c, du_v)):
            dprojt_ref[k * W:(k + 1) * W, :] = du.T.astype(BF16)
            dh_in = dh_in + _dot(du.astype(BF16), win_v[k * W:(k + 1) * W, :])
        vec_ref[V_SHIFT1:V_SHIFT1 + 1, :] += _colsum(dh_in)
        vec_ref[V_SCALE1:V_SCALE1 + 1, :] += _colsum(dh_in * xn1 * g1_ref[...])
        vec_ref[V_G1:V_G1 + 1, :] += _colsum(dh_in * (1.0 + scale1) * xn1)
        dxn1 = dh_in * g1_ref[...] * (1.0 + scale1)
        gx_ref[...] = dx1 + r1 * (dxn1 - xn1 * _rowmean(dxn1 * xn1))

        @pl.when(i == nb - 1)
        def _():
            for cp in chip_copies:
                cp.wait_recv()
            for cp in chip_copies:
                cp.wait_send()

    rev = lambda cols: pl.BlockSpec((tb, cols), lambda i: (nb - 1 - i, 0))
    rev_t = lambda rows: pl.BlockSpec((rows, tb), lambda i: (0, nb - 1 - i))
    halo = lambda cols: pl.BlockSpec((HALO, cols), lambda i: (jnp.maximum((nb - 1 - i) * hb - 1, 0), 0))
    full = lambda a: pl.BlockSpec(a.shape, lambda i: (0,) * a.ndim)
    small = (modraw, adab, g1, g2, wl, bl, bda, bdx, ba, bxb, ap, ws, gl, gc, avg)
    ext = pltpu.VMEM((tb + HALO, W), F32)
    n_sems = max(len(CHIP_FLIPS) * n_sums, 1)
    return pl.pallas_call(
        body,
        name="mixer_bwd",
        grid=(nb,),
        in_specs=[rev(D), rev(D), rev(D), rev(D), rev(D_IN), halo(D_IN), rev(W), halo(W), rev(N_KEPT * W)]
        + [full(a) for a in small] + [ANY] * (1 + n_sums),
        out_specs=[rev(D), pl.BlockSpec((V_ROWS, D), lambda i: (0, 0)), rev(D), rev_t(D_IN), rev(D), rev_t(D),
                   rev_t(W), rev(2 * W)] + [ANY] * n_sums,
        out_shape=[jax.ShapeDtypeStruct((t_len, D), F32), jax.ShapeDtypeStruct((V_ROWS, D), F32),
                   jax.ShapeDtypeStruct((t_len, D), BF16), jax.ShapeDtypeStruct((D_IN, t_len), BF16),
                   jax.ShapeDtypeStruct((t_len, D), BF16), jax.ShapeDtypeStruct((D, t_len), BF16),
                   jax.ShapeDtypeStruct((W, t_len), BF16), jax.ShapeDtypeStruct((t_len, 2 * W), BF16)]
        + [jax.ShapeDtypeStruct((len(CHIP_FLIPS),) + s.shape[1:], s.dtype) for s in chip_sums],
        scratch_shapes=[pltpu.VMEM((D_IN, D), BF16), pltpu.VMEM((D, D), BF16), pltpu.SemaphoreType.DMA((2,)),
                        ext, ext, ext, ext, ext, ext, pltpu.VMEM((HALO, W), F32),
                        pltpu.VMEM((tb, W), F32), pltpu.VMEM((tb, W), F32), pltpu.VMEM((tb, W), F32),
                        pltpu.SemaphoreType.DMA((n_sems,)), pltpu.SemaphoreType.DMA((n_sems,))],
        compiler_params=pltpu.CompilerParams(dimension_semantics=("arbitrary",), vmem_limit_bytes=VMEM_LIMIT),
    )(x, mixed, dh2, dx2, proj, proj, hl, hl, kept, *small, wpack, *chip_sums)


def _matmul(name, a, b, tm=512):
    m, k = a.shape
    n = b.shape[1]

    def body(a_ref, b_ref, o_ref):
        o_ref[...] = _dot(a_ref[...], b_ref[...])

    return pl.pallas_call(
        body,
        name=name,
        grid=(m // tm,),
        in_specs=[pl.BlockSpec((tm, k), lambda i: (i, 0)), pl.BlockSpec((k, n), lambda i: (0, 0))],
        out_specs=pl.BlockSpec((tm, n), lambda i: (i, 0)),
        out_shape=jax.ShapeDtypeStruct((m, n), F32),
        compiler_params=pltpu.CompilerParams(dimension_semantics=("arbitrary",), vmem_limit_bytes=VMEM_LIMIT),
    )(a, b)


def _gate_wgrad(xl_t, dgate, avg):
    hd = W // 8

    def body(a_ref, b_ref, avg_ref, o_ref):
        full = _dot(a_ref[...], b_ref[...])
        row = lax.broadcasted_iota(jnp.int32, (W, hd), 0)
        col = lax.broadcasted_iota(jnp.int32, (W, hd), 1)
        fold = ((row & (hd - 1)) == col).astype(BF16)
        keep = avg_ref[...] != 0
        for g in range(2):
            m = jnp.where(keep, full[:, g * W:(g + 1) * W], 0.0)
            hi = m.astype(BF16)
            rest = m - hi.astype(F32)
            mid = rest.astype(BF16)
            lo = (rest - mid.astype(F32)).astype(BF16)
            o_ref[g] = _dot(hi, fold) + _dot(mid, fold) + _dot(lo, fold)

    return pl.pallas_call(
        body,
        name="wgrad_gate",
        in_specs=[WHOLE] * 3,
        out_specs=WHOLE,
        out_shape=jax.ShapeDtypeStruct((2, W, hd), F32),
        compiler_params=pltpu.CompilerParams(vmem_limit_bytes=VMEM_LIMIT),
    )(xl_t, dgate, avg)


def _block_diag(w):
    n, m, _ = w.shape
    eye = jnp.eye(n, dtype=w.dtype)
    return (w[:, :, None, :] * eye[:, None, :, None]).reshape(n * m, n * m)


def _pad_rows(a, rows):
    return jnp.pad(a, ((0, rows - a.shape[0]),) + ((0, 0),) * (a.ndim - 1))


def _position():
    return lax.axis_index("x"), lax.axis_index("y"), lax.axis_index("c")


def _linear(pos):
    return 4 * pos[0] + 2 * pos[1] + pos[2]


def _flip(pos, k):
    return tuple(1 - p if k & bit else p for p, bit in zip(pos, (4, 2, 1)))


def _exchange_all(make_copy, make_arrival):
    copies = [make_copy(k) for k in range(1, N_DEV)]
    for cp in copies:
        cp.start()
    for k in range(1, N_DEV):
        make_arrival(k).wait_recv()
    for cp in copies:
        cp.wait_send()


def _mod_exchange_steps(cols):
    def steps(msg_ref, adaw_ref, gath_ref, mod_ref, sendbuf, send_a, recv_a, send_b, recv_b):
        me = _position()
        me_lin = _linear(me)
        m = msg_ref[...]
        row = lax.broadcasted_iota(jnp.int32, m.shape, 0)
        gath_ref[me_lin] = jnp.where(row == 0, m * _sigmoid(m), m)

        def gather_copy(k, src_lin):
            return pltpu.make_async_remote_copy(
                src_ref=gath_ref.at[src_lin], dst_ref=gath_ref.at[src_lin], send_sem=send_a.at[k - 1],
                recv_sem=recv_a.at[k - 1], device_id=_flip(me, k), device_id_type=MESH)

        _exchange_all(lambda k: gather_copy(k, me_lin), lambda k: gather_copy(k, _linear(_flip(me, k))))

        sc_all = gath_ref[:, 0, :]
        scb = jnp.concatenate([sc_all, jnp.zeros_like(sc_all)], axis=0).astype(BF16)
        prod = _dot(scb, adaw_ref[...].astype(BF16))
        for b in range(N_DEV):
            sendbuf[b] = jnp.broadcast_to(prod[b:b + 1, :], (HALO, cols))
        mod_ref[me_lin] = sendbuf[me_lin]

        def row_copy(k, dst_lin):
            peer = _flip(me, k)
            return pltpu.make_async_remote_copy(
                src_ref=sendbuf.at[_linear(peer)], dst_ref=mod_ref.at[dst_lin], send_sem=send_b.at[k - 1],
                recv_sem=recv_b.at[k - 1], device_id=peer, device_id_type=MESH)

        _exchange_all(lambda k: row_copy(k, me_lin), lambda k: row_copy(k, _linear(_flip(me, k))))

    return steps


def _gather_and_mod(msg, ada_w, block):
    rows, cols = block.shape
    mod_cols = ada_w.shape[1]
    mod_steps = _mod_exchange_steps(mod_cols)

    def body(msg_ref, adaw_ref, x_ref, gath_ref, mod_ref, out_ref, sendbuf, send_a, recv_a, send_b, recv_b,
             send_sems, recv_sems, sib_send_sems, sib_recv_sems, local_sem):
        x, y, c = _position()
        me, sibling = (x, y, c), (x, y, 1 - c)
        sends, forward, arrivals = _chip_gather_copies(x_ref, out_ref, send_sems, recv_sems)

        def to_sibling(j, block_of, src=None):
            dst = out_ref.at[_linear(block_of)]
            return pltpu.make_async_remote_copy(
                src_ref=dst if src is None else src, dst_ref=dst, send_sem=sib_send_sems.at[j],
                recv_sem=sib_recv_sems.at[j], device_id=sibling, device_id_type=MESH)

        mine = pltpu.make_async_copy(x_ref, out_ref.at[_linear(me)], local_sem)
        mine.start()
        passes = [to_sibling(0, me, src=x_ref)] + [to_sibling(1 + j, p) for j, p in enumerate(_route_peers(me))]
        passes[0].start()
        for cp in sends:
            cp.start()
        mod_steps(msg_ref, adaw_ref, gath_ref, mod_ref, sendbuf, send_a, recv_a, send_b, recv_b)
        arrivals[0].wait_recv()
        forward.start()
        passes[1].start()
        arrivals[1].wait_recv()
        passes[2].start()
        arrivals[2].wait_recv()
        passes[3].start()
        for j, p in enumerate((sibling,) + _route_peers(sibling)):
            to_sibling(j, p).wait_recv()
        for cp in sends + [forward] + passes:
            cp.wait_send()
        mine.wait()

    return pl.pallas_call(
        body,
        name="gather_and_mod",
        in_specs=[WHOLE, WHOLE, ANY],
        out_specs=[WHOLE, WHOLE, ANY],
        out_shape=[jax.ShapeDtypeStruct((N_DEV, HALO, D), F32), jax.ShapeDtypeStruct((N_DEV, HALO, mod_cols), F32),
                   jax.ShapeDtypeStruct((N_DEV, rows, cols), block.dtype)],
        scratch_shapes=[pltpu.VMEM((N_DEV, HALO, mod_cols), F32)] + [pltpu.SemaphoreType.DMA((N_DEV - 1,))] * 4
        + [pltpu.SemaphoreType.DMA((3,)), pltpu.SemaphoreType.DMA((3,)), pltpu.SemaphoreType.DMA((4,)),
           pltpu.SemaphoreType.DMA((4,)), pltpu.SemaphoreType.DMA],
        compiler_params=pltpu.CompilerParams(vmem_limit_bytes=VMEM_LIMIT),
    )(msg, ada_w, block)


HBM = pl.BlockSpec(memory_space=pltpu.HBM)
SEM = pl.BlockSpec(memory_space=pltpu.SEMAPHORE)
EFFECT = pltpu.SideEffectType.DATAFLOW_SIDE_EFFECTING


def _stage_copies(stage):
    return {"chips": (_chip_scatter_copies, len(CHIP_FLIPS), len(CHIP_FLIPS)), "sibling": (_sibling_copies, 4, 4)}[stage]


def _chips_start(which, chip_sums, stage="chips"):
    n = len(chip_sums)
    make_copies, per_array, slots = _stage_copies(stage)
    n_sems = per_array * n

    def body(*refs):
        srcs, dsts = refs[:n], refs[n:2 * n]
        send_sems, recv_sems = refs[2 * n:2 * n + 2]
        token = refs[-1]
        for cp in make_copies(srcs, dsts, send_sems, recv_sems):
            cp.start()
        token[...] = jnp.zeros(token.shape, token.dtype)

    landing = [jax.ShapeDtypeStruct((slots,) + s.shape[-2:], s.dtype) for s in chip_sums]
    outs = pl.pallas_call(
        body,
        name=which + "_" + stage + "_start",
        in_specs=[HBM] * (2 * n),
        out_specs=[SEM, SEM] + [HBM] * (2 * n) + [WHOLE],
        out_shape=[pltpu.SemaphoreType.DMA((n_sems,)), pltpu.SemaphoreType.DMA((n_sems,))]
        + [pltpu.HBM(s.shape, s.dtype) for s in chip_sums] + [pltpu.HBM(s.shape, s.dtype) for s in landing]
        + [jax.ShapeDtypeStruct((HALO, 128), F32)],
        input_output_aliases={i: 2 + i for i in range(2 * n)},
        compiler_params=pltpu.CompilerParams(has_side_effects=EFFECT),
    )(*[pltpu.with_memory_space_constraint(s, pltpu.HBM) for s in chip_sums],
      *[pltpu.with_memory_space_constraint(lax.empty(s.shape, s.dtype), pltpu.HBM) for s in landing])
    return outs[0], outs[1], outs[2:2 + n], outs[2 + n:2 + 2 * n], outs[-1]


def _chips_wait(which, send_sems, recv_sems, srcs, landed, after, stage="chips"):
    n = len(srcs)
    make_copies = _stage_copies(stage)[0]

    def body(*refs):
        src_refs, dst_refs = refs[:n], refs[n:2 * n]
        sends, recvs = refs[2 * n:2 * n + 2]
        copies = make_copies(src_refs, dst_refs, sends, recvs)
        for cp in copies:
            cp.wait_send()
        for cp in copies:
            cp.wait_recv()

    outs = pl.pallas_call(
        body,
        name=which + "_" + stage + "_wait",
        in_specs=[HBM] * (2 * n) + [SEM, SEM, ANY],
        out_specs=[HBM] * (2 * n),
        out_shape=[pltpu.HBM(s.shape, s.dtype) for s in list(srcs) + list(landed)],
        input_output_aliases={i: i for i in range(2 * n)},
        compiler_params=pltpu.CompilerParams(has_side_effects=EFFECT),
    )(*srcs, *landed, send_sems, recv_sems, after)
    return list(outs[n:])


def _sibling_copies(srcs, dsts, send_sems, recv_sems):
    x, y, c = _position()
    copies = []
    for a, (src, dst) in enumerate(zip(srcs, dsts)):
        for k in range(4):
            copies.append(pltpu.make_async_remote_copy(
                src_ref=src.at[k, 1 - c] if len(src.shape) == 4 else src.at[k], dst_ref=dst.at[k],
                send_sem=send_sems.at[4 * a + k],
                recv_sem=recv_sems.at[4 * a + k], device_id=(x, y, 1 - c), device_id_type=MESH))
    return copies


def _row_block(rows):
    return 256 if rows % 256 == 0 else rows // 2


def _pair_sum(pos, mine, recv):
    _, cores, rows, cols = mine.shape
    rb = _row_block(rows)

    def body(pos_ref, mine_ref, recv_ref, out_ref):
        out_ref[0] = (mine_ref[0, 0] + recv_ref[0]).astype(BF16)

    other = lambda k, pos: jnp.bitwise_xor(pos[1], k + 1)
    core = lambda pos: pos[0] * (cores - 1)
    return pl.pallas_call(
        body,
        name="grad_pair_sum",
        grid_spec=pltpu.PrefetchScalarGridSpec(
            num_scalar_prefetch=1, grid=(3, rows // rb),
            in_specs=[pl.BlockSpec((1, 1, rb, cols), lambda k, r, pos: (other(k, pos), core(pos), r, 0)),
                      pl.BlockSpec((1, rb, cols), lambda k, r, pos: (other(k, pos), r, 0))],
            out_specs=pl.BlockSpec((1, rb, cols), lambda k, r, pos: (other(k, pos), r, 0))),
        out_shape=jax.ShapeDtypeStruct((4, rows, cols), BF16),
        compiler_params=pltpu.CompilerParams(dimension_semantics=("arbitrary", "arbitrary")),
    )(pos, mine, recv)


def _final_sum(pos, mine, recv, chips):
    _, cores, rows, cols = mine.shape
    rb = _row_block(rows)

    def body(pos_ref, mine_ref, recv_ref, chips_ref, out_ref):
        g = mine_ref[0, 0] + recv_ref[0]
        for j in range(3):
            g = g + chips_ref[j].astype(F32)
        out_ref[...] = g

    return pl.pallas_call(
        body,
        name="grad_final_sum",
        grid_spec=pltpu.PrefetchScalarGridSpec(
            num_scalar_prefetch=1, grid=(rows // rb,),
            in_specs=[pl.BlockSpec((1, 1, rb, cols), lambda r, pos: (pos[1], pos[0] * (cores - 1), r, 0)),
                      pl.BlockSpec((1, rb, cols), lambda r, pos: (pos[1], r, 0)),
                      pl.BlockSpec((3, rb, cols), lambda r, pos: (0, r, 0))],
            out_specs=pl.BlockSpec((rb, cols), lambda r, pos: (r, 0))),
        out_shape=jax.ShapeDtypeStruct((rows, cols), F32),
        compiler_params=pltpu.CompilerParams(dimension_semantics=("arbitrary",)),
    )(pos, mine, recv, chips)


LOSS_ROW = V_ROWS + 8
GB_BASE = LOSS_ROW + 8


def _route_mod_grads(gmod8, sc_t, after):
    cols = gmod8.shape[1]

    def body(gmod_ref, sct_ref, after_ref, gadaw_ref, gb_ref, sendbuf, grecv, send_a, recv_a):
        me = _position()
        me_lin = _linear(me)
        gm = gmod_ref[...]
        for b in range(N_DEV):
            sendbuf[b] = jnp.broadcast_to(gm[b:b + 1, :], (HALO, cols))
        grecv[me_lin] = sendbuf[me_lin]

        def row_copy(k, dst_lin):
            peer = _flip(me, k)
            return pltpu.make_async_remote_copy(
                src_ref=sendbuf.at[_linear(peer)], dst_ref=grecv.at[dst_lin], send_sem=send_a.at[k - 1],
                recv_sem=recv_a.at[k - 1], device_id=peer, device_id_type=MESH)

        _exchange_all(lambda k: row_copy(k, me_lin), lambda k: row_copy(k, _linear(_flip(me, k))))
        g_all = grecv[:, 0, :]
        g_pad = jnp.concatenate([g_all, jnp.zeros((sct_ref.shape[1] - N_DEV, cols), F32)], axis=0).astype(BF16)
        gadaw_ref[...] = _dot(sct_ref[...], g_pad)
        gb_ref[...] = jnp.broadcast_to(_colsum(g_all), (HALO, cols))

    return pl.pallas_call(
        body,
        name="route_mod_grads",
        in_specs=[WHOLE] * 3,
        out_specs=[WHOLE, WHOLE],
        out_shape=[jax.ShapeDtypeStruct((D, cols), F32), jax.ShapeDtypeStruct((HALO, cols), F32)],
        scratch_shapes=[pltpu.VMEM((N_DEV, HALO, cols), F32), pltpu.VMEM((N_DEV, HALO, cols), F32),
                        pltpu.SemaphoreType.DMA((N_DEV - 1,)), pltpu.SemaphoreType.DMA((N_DEV - 1,))],
        compiler_params=pltpu.CompilerParams(vmem_limit_bytes=VMEM_LIMIT),
    )(gmod8, sc_t, after)


def _chips_and_gather(msg_vec, msg_gate, gb_rows, chip_sums):
    cols = gb_rows.shape[1]
    n = len(chip_sums)
    vec_rows = GB_BASE + N_DEV

    def body(vec_ref, gate_ref, gb_ref, *refs):
        srcs, (sumv_ref, sumg_ref), dsts = refs[:n], refs[n:n + 2], refs[n + 2:2 * n + 2]
        (myv, myg, sibv, sibg, chipv, chipg, sib_send, sib_recv, peer_send, peer_recv,
         chip_send, chip_recv) = refs[2 * n + 2:]
        chip_copies = _chip_scatter_copies(srcs, dsts, chip_send, chip_recv)
        for cp in chip_copies:
            cp.start()
        x, y, c = me = _position()
        my_chip = 2 * x + y
        myv[0:GB_BASE, :] = vec_ref[...]
        slot = lax.broadcasted_iota(jnp.int32, (N_DEV, D), 0) == _linear(me)
        gb_wide = jnp.concatenate([gb_ref[...], jnp.zeros((N_DEV, D - cols), F32)], axis=1)
        myv[GB_BASE:vec_rows, :] = jnp.where(slot, gb_wide, 0.0)
        myg[...] = gate_ref[...]

        swaps = [pltpu.make_async_remote_copy(
            src_ref=src, dst_ref=dst, send_sem=sib_send.at[a], recv_sem=sib_recv.at[a], device_id=(x, y, 1 - c),
            device_id_type=MESH) for a, (src, dst) in enumerate(((myv, sibv), (myg, sibg)))]
        for cp in swaps:
            cp.start()
        for cp in swaps:
            cp.wait_recv()
        chipv[my_chip] = myv[...] + sibv[...]
        chipg[my_chip] = myg[...] + sibg[...]

        def chip_copy(a, buf, j, k, slot_chip):
            peer = _flip(me, k)
            return pltpu.make_async_remote_copy(
                src_ref=buf.at[slot_chip], dst_ref=buf.at[slot_chip], send_sem=peer_send.at[3 * a + j],
                recv_sem=peer_recv.at[3 * a + j], device_id=peer, device_id_type=MESH)

        sends = [chip_copy(a, buf, j, k, my_chip) for a, buf in enumerate((chipv, chipg)) for j, k in enumerate(CHIP_FLIPS)]
        for cp in sends:
            cp.start()
        for a, buf in enumerate((chipv, chipg)):
            for j, k in enumerate(CHIP_FLIPS):
                peer = _flip(me, k)
                chip_copy(a, buf, j, k, 2 * peer[0] + peer[1]).wait_recv()
        sumv_ref[...] = ((chipv[0] + chipv[1]) + chipv[2]) + chipv[3]
        sumg_ref[...] = ((chipg[0] + chipg[1]) + chipg[2]) + chipg[3]
        for cp in swaps + sends:
            cp.wait_send()
        for cp in chip_copies:
            cp.wait_recv()
        for cp in chip_copies:
            cp.wait_send()

    n_chip = max(len(CHIP_FLIPS) * n, 1)
    vshape, gshape = (vec_rows, D), msg_gate.shape
    return pl.pallas_call(
        body,
        name="chips_and_gather",
        in_specs=[WHOLE] * 3 + [ANY] * n,
        out_specs=[WHOLE] * 2 + [ANY] * n,
        out_shape=[jax.ShapeDtypeStruct(vshape, F32), jax.ShapeDtypeStruct(gshape, F32)]
        + [jax.ShapeDtypeStruct((len(CHIP_FLIPS),) + s.shape[1:], s.dtype) for s in chip_sums],
        scratch_shapes=[pltpu.VMEM(vshape, F32), pltpu.VMEM(gshape, F32), pltpu.VMEM(vshape, F32),
                        pltpu.VMEM(gshape, F32), pltpu.VMEM((4,) + vshape, F32), pltpu.VMEM((4,) + gshape, F32),
                        pltpu.SemaphoreType.DMA((2,)), pltpu.SemaphoreType.DMA((2,)),
                        pltpu.SemaphoreType.DMA((2 * len(CHIP_FLIPS),)), pltpu.SemaphoreType.DMA((2 * len(CHIP_FLIPS),)),
                        pltpu.SemaphoreType.DMA((n_chip,)), pltpu.SemaphoreType.DMA((n_chip,))],
        compiler_params=pltpu.CompilerParams(vmem_limit_bytes=VMEM_LIMIT),
    )(msg_vec, msg_gate, gb_rows, *chip_sums)


def _adamw_math(w, g, m, v):
    m = ADAM_B1 * m + (1.0 - ADAM_B1) * g
    v = ADAM_B2 * v + (1.0 - ADAM_B2) * (g * g)
    m_hat = m / (1.0 - ADAM_B1 ** ADAM_STEP)
    v_hat = v / (1.0 - ADAM_B2 ** ADAM_STEP)
    delta = -ADAM_LR * (m_hat / (jnp.sqrt(v_hat) + ADAM_EPS) + ADAM_WD * w)
    return delta, m, v


def _adamw(name, w, g, m, v):
    rows, cols = w.shape
    rb = 256 if rows % 256 == 0 else rows

    def body(w_ref, g_ref, m_ref, v_ref, d_ref, mo_ref, vo_ref):
        d_ref[...], mo_ref[...], vo_ref[...] = _adamw_math(w_ref[...], g_ref[...], m_ref[...], v_ref[...])

    spec = pl.BlockSpec((rb, cols), lambda r: (r, 0))
    return pl.pallas_call(
        body,
        name="adamw_" + name,
        grid=(rows // rb,),
        in_specs=[spec] * 4,
        out_specs=[spec] * 3,
        out_shape=[jax.ShapeDtypeStruct((rows, cols), F32)] * 3,
        compiler_params=pltpu.CompilerParams(dimension_semantics=("arbitrary",)),
    )(w, g, m, v)


def _update(pos, sum_jobs, plain_jobs, after):
    rb = 256
    jobs = [("sum", j) for j in sum_jobs] + [("plain", j) for j in plain_jobs]
    offs, total = [], 0
    for _, j in jobs:
        offs.append(total)
        total += j[-1].shape[0] // rb
    n_in = sum(len(j) for _, j in jobs)

    def body(pos_ref, *refs):
        ins, outs = refs[:n_in], refs[n_in + 1:]
        s = pl.program_id(0)
        i_in = i_out = 0
        for (kind, j), off in zip(jobs, offs):
            steps = j[-1].shape[0] // rb
            j_in = ins[i_in:i_in + len(j)]
            i_in += len(j)
            j_out = outs[i_out:i_out + (4 if kind == "sum" else 3)]
            i_out += len(j_out)

            @pl.when((s >= off) & (s < off + steps))
            def _(kind=kind, j_in=j_in, j_out=j_out):
                if kind == "sum":
                    mine_ref, recv_ref, chips_ref, w_ref, m_ref, v_ref = j_in
                    g = mine_ref[0, 0] + recv_ref[0]
                    for q in range(len(CHIP_FLIPS)):
                        g = g + chips_ref[q].astype(F32)
                    j_out[0][...] = g
                    rest = j_out[1:]
                else:
                    g_ref, w_ref, m_ref, v_ref = j_in
                    g = g_ref[...]
                    rest = j_out
                rest[0][...], rest[1][...], rest[2][...] = _adamw_math(w_ref[...], g, m_ref[...], v_ref[...])

    in_specs, out_specs, out_shape, args = [], [], [], []
    for (kind, j), off in zip(jobs, offs):
        rows, cols = j[-1].shape
        steps = rows // rb
        blk = lambda s, off=off, steps=steps: jnp.clip(s - off, 0, steps - 1)
        flat = pl.BlockSpec((rb, cols), lambda s, pos, blk=blk: (blk(s), 0))
        if kind == "sum":
            in_specs += [pl.BlockSpec((1, 1, rb, cols), lambda s, pos, blk=blk: (pos[1], 0, blk(s), 0)),
                         pl.BlockSpec((1, rb, cols), lambda s, pos, blk=blk: (pos[1], blk(s), 0)),
                         pl.BlockSpec((len(CHIP_FLIPS), rb, cols), lambda s, pos, blk=blk: (0, blk(s), 0))]
            in_specs += [flat] * 3
        else:
            in_specs += [flat] * 4
        n_res = 4 if kind == "sum" else 3
        out_specs += [flat] * n_res
        out_shape += [jax.ShapeDtypeStruct((rows, cols), F32)] * n_res
        args += list(j)
    in_specs += [pl.BlockSpec(after.shape, lambda s, pos: (0,) * after.ndim)]
    outs = pl.pallas_call(
        body,
        name="update",
        grid_spec=pltpu.PrefetchScalarGridSpec(
            num_scalar_prefetch=1, grid=(total,), in_specs=in_specs, out_specs=out_specs),
        out_shape=out_shape,
        compiler_params=pltpu.CompilerParams(dimension_semantics=("arbitrary",), vmem_limit_bytes=VMEM_LIMIT),
    )(pos, *args, after)
    sums = [tuple(outs[4 * i:4 * i + 4]) for i in range(len(sum_jobs))]
    base = 4 * len(sum_jobs)
    plains = [tuple(outs[base + 3 * i:base + 3 * i + 3]) for i in range(len(plain_jobs))]
    return sums, plains


def _adamw_small(ws, gs, ms, vs, sigmoid_scaled):
    n = len(ws)

    def body(*refs):
        w_refs, g_refs, m_refs, v_refs = (refs[i * n:(i + 1) * n] for i in range(4))
        outs = refs[4 * n:]
        for i in range(n):
            w = w_refs[i][...]
            g = g_refs[i][...]
            if sigmoid_scaled[i]:
                g = g * _sigmoid(w)
            delta, m, v = _adamw_math(w, g, m_refs[i][...], v_refs[i][...])
            outs[4 * i][...] = g
            outs[4 * i + 1][...] = delta
            outs[4 * i + 2][...] = m
            outs[4 * i + 3][...] = v

    shapes = [jax.ShapeDtypeStruct(w.shape, F32) for w in ws for _ in range(4)]
    outs = pl.pallas_call(
        body,
        name="adamw_small",
        in_specs=[WHOLE] * (4 * n),
        out_specs=[WHOLE] * (4 * n),
        out_shape=shapes,
    )(*ws, *gs, *ms, *vs)
    return [outs[4 * i:4 * i + 4] for i in range(n)]


_WEIGHT_NAMES = ("ada_w", "ada_b", "norm1_g", "w_in", "lru_conv_w", "lru_conv_b", "gate_a_w", "gate_a_b", "gate_x_w",
                 "gate_x_b", "a_param", "short_conv_w", "lru_out_g", "conv_out_g", "w_out", "norm2_g", "w_mlp1",
                 "w_mlp2", "final_g")


def kernel(x, c, ada_w, ada_b, norm1_g, w_in, lru_conv_w, lru_conv_b, gate_a_w, gate_a_b, gate_x_w, gate_x_b, a_param, short_conv_w, lru_out_g, conv_out_g, w_out, norm2_g, w_mlp1, w_mlp2, final_g, loss_target, m_ada_w, m_ada_b, m_norm1_g, m_w_in, m_lru_conv_w, m_lru_conv_b, m_gate_a_w, m_gate_a_b, m_gate_x_w, m_gate_x_b, m_a_param, m_short_conv_w, m_lru_out_g, m_conv_out_g, m_w_out, m_norm2_g, m_w_mlp1, m_w_mlp2, m_final_g, v_ada_w, v_ada_b, v_norm1_g, v_w_in, v_lru_conv_w, v_lru_conv_b, v_gate_a_w, v_gate_a_b, v_gate_x_w, v_gate_x_b, v_a_param, v_short_conv_w, v_lru_out_g, v_conv_out_g, v_w_out, v_norm2_g, v_w_mlp1, v_w_mlp2, v_final_g):
    given = dict(locals())
    weights = {n: given[n] for n in _WEIGHT_NAMES}
    xi, yi, ci = _position()
    me_lin = _linear((xi, yi, ci))
    hd = W // N_DEV

    mixer_block = jnp.concatenate([w_out[0], w_in[0].T], axis=0).astype(BF16)
    mlp_block = jnp.concatenate([w_mlp1[0].T, w_mlp2[0]], axis=0).astype(BF16)

    msg = (jnp.pad(c, ((0, HALO - 1), (0, 0)))
           + jnp.pad(lru_conv_w[0], ((1, HALO - 1 - CONV_L), (0, D - hd)))
           + jnp.pad(short_conv_w[0], ((1 + CONV_L, 0), (0, D - hd))))
    gath, mod_all, wmix = _gather_and_mod(msg, ada_w[0], mixer_block)
    sc_all = gath[:, 0, :]
    wl = jnp.transpose(gath[:, 1:1 + CONV_L, :hd], (1, 0, 2)).reshape(CONV_L, W)
    ws = jnp.transpose(gath[:, 1 + CONV_L:HALO, :hd], (1, 0, 2)).reshape(CONV_S, W)
    modraw = _pad_rows(mod_all[:, 0, :].reshape(6, D), HALO)
    adab = _pad_rows(ada_b.reshape(6, D), HALO)

    x2d, tgt = x[0], loss_target[0]
    gf = final_g.reshape(1, D)
    bda = _block_diag(gate_a_w[0]).astype(BF16)
    bdx = _block_diag(gate_x_w[0]).astype(BF16)
    avg = _block_diag(jnp.full((8, W // 8, W // 8), 8.0 / W, F32)).astype(BF16)
    wl8 = _pad_rows(wl, HALO)
    ws8 = _pad_rows(ws, HALO)
    mixer_small = (wl8, lru_conv_b, bda, bdx, gate_a_b, gate_x_b, a_param, ws8, lru_out_g, conv_out_g, avg)
    proj, hl, mixed, kept, wmlp = _mixer_fwd(x2d, modraw, adab, norm1_g, *mixer_small, wmix, mlp_block)
    wmlp = _sibling_forward(wmlp)
    h2t, f, dx2, dz, vec2, loss8 = _mlp_fwd(x2d, mixed, tgt, modraw, adab, norm2_g, gf, wmlp)
    pos = jnp.stack([ci, 2 * xi + yi]).astype(jnp.int32)
    by_dest = lambda g: g.reshape((4, 2, -1) + g.shape[-1:])
    dh2_first, *for_sibling = _mlp_bwd_half(pos, h2t, f, dz, wmlp)
    sib_send, sib_recv, sib_thru, sib_land, token = _chips_start("mlp", for_sibling, stage="sibling")
    dh2, dw1, dw2 = _mlp_bwd_half(pos, h2t, f, dz, wmlp, prior=(dh2_first, token))
    done = dh2[0:HALO, 0:128] + dw1[0, 0:HALO, 0:128] + dw2[0, 0:HALO, 0:128]
    mlp_sib = _chips_wait("mlp", sib_send, sib_recv, sib_thru, sib_land, done, stage="sibling")
    mlp_parts = [dw1[:, None], dw2[:, None]]
    mlp_sums = [_pair_sum(pos, p, r) for p, r in zip(mlp_parts, mlp_sib)]
    mlp_send, mlp_recv, mlp_thru, mlp_land, token = _chips_start("mlp", mlp_sums)
    modraw_after = modraw + jnp.tile(token, (1, D // token.shape[1]))
    gx, vec, hb, dproj_t, dmixed, ycat_t, xl_t, dgate = _mixer_bwd(
        x2d, mixed, dh2, dx2, proj, hl, kept, modraw_after, adab, norm1_g, norm2_g, *mixer_small, wmix, [], None)
    dwint = _matmul("wgrad_in", dproj_t, hb)
    dwout = _matmul("wgrad_out", ycat_t, dmixed)
    gate_blocks = _gate_wgrad(xl_t, dgate, avg)
    msg_gate = gate_blocks.reshape(W, 128)
    done = dwint[0:HALO, 0:128] + dwout[0:HALO, 0:128] + gate_blocks[0, 0:HALO, :].sum() + gx[0:HALO, 0:128]
    mlp_chips = _chips_wait("mlp", mlp_send, mlp_recv, mlp_thru, mlp_land, done)
    mix_parts = [by_dest(dwout), by_dest(dwint)]
    gmod8 = (jnp.pad(vec[0:5], ((0, 1), (0, 0))) + jnp.pad(vec2[0:1], ((5, 0), (0, 0)))).reshape(N_DEV, 6 * D // N_DEV)
    sc_t = jnp.pad(sc_all.T, ((0, 0), (0, 128 - N_DEV))).astype(BF16)
    sib_send, sib_recv, sib_thru, sib_land, token = _chips_start("mixer", mix_parts, stage="sibling")
    g_adaw, gb_rows = _route_mod_grads(gmod8, sc_t, token)
    mix_sib = _chips_wait("mixer", sib_send, sib_recv, sib_thru, sib_land, gb_rows, stage="sibling")
    mix_sums = [_pair_sum(pos, p, r) for p, r in zip(mix_parts, mix_sib)]
    loss_rows = jnp.pad(loss8[0:1], ((0, HALO - 1), (0, D - loss8.shape[1])))
    msg_vec = jnp.concatenate([vec, vec2, loss_rows], axis=0)
    state = lambda n: (weights[n][0], given["m_" + n][0], given["v_" + n][0])
    mlp_jobs = [(p, r, q, *state(n)) for p, r, q, n in zip(mlp_parts, mlp_sib, mlp_chips, ("w_mlp1", "w_mlp2"))]
    mix_send, mix_recv, mix_thru, mix_land, token = _chips_start("mixer", mix_sums)
    mlp_done, (adaw_done,) = _update(pos, mlp_jobs, [(g_adaw, *state("ada_w"))], token)
    sum_vec, sum_gate = _chips_and_gather(msg_vec, msg_gate, gb_rows, [])
    done = sum_vec[0:HALO, 0:128] + mlp_done[0][2][0:HALO, 0:128] + mlp_done[1][2][0:HALO, 0:128] + adaw_done[1][0:HALO, 0:128]
    mix_chips = _chips_wait("mixer", mix_send, mix_recv, mix_thru, mix_land, done)
    g_wout, g_wint = (_final_sum(pos, p, r, q) for p, r, q in zip(mix_parts, mix_sib, mix_chips))
    loss = sum_vec[LOSS_ROW, 0]
    sum_gate = sum_gate.reshape(2, W, W // 8)
    lo, hi = slice(0, W), slice(W, 2 * W)
    wl_full = sum_vec[V_WL01:V_WL23 + 1].reshape(CONV_L, W)
    ws_full = sum_vec[V_WS01:V_WS2 + 1].reshape(CONV_S + 1, W)[:CONV_S]
    row = lambda r, cols: sum_vec[r:r + 1, cols]
    small_grads = {
        "ada_b": sum_vec[GB_BASE:GB_BASE + N_DEV, :6 * D // N_DEV].reshape(1, 6 * D),
        "norm1_g": row(V_G1, slice(0, D)),
        "lru_conv_w": lax.dynamic_slice(wl_full, (0, me_lin * hd), (CONV_L, hd)),
        "lru_conv_b": row(V_BL_BA, lo),
        "gate_a_w": sum_gate[0],
        "gate_a_b": row(V_BL_BA, hi),
        "gate_x_w": sum_gate[1],
        "gate_x_b": row(V_BX_SP, lo),
        "a_param": row(V_BX_SP, hi),
        "short_conv_w": lax.dynamic_slice(ws_full, (0, me_lin * hd), (CONV_S, hd)),
        "lru_out_g": row(V_GL_GC, lo),
        "conv_out_g": row(V_GL_GC, hi),
        "norm2_g": row(V_G2, slice(0, D)),
        "final_g": sum_vec[V_ROWS + 1:V_ROWS + 2, :],
    }
    names = list(small_grads)
    as2d = lambda a, n: a.reshape(small_grads[n].shape)
    small = _adamw_small([as2d(weights[n], n) for n in names], [small_grads[n] for n in names],
                         [as2d(given["m_" + n], n) for n in names], [as2d(given["v_" + n], n) for n in names],
                         [n == "a_param" for n in names])
    result = {n: tuple(o.reshape(weights[n].shape) for o in outs) for n, outs in zip(names, small)}

    for n, g in (("w_in", g_wint.T), ("w_out", g_wout)):
        w, m, v = state(n)
        result[n] = (g[None],) + tuple(o[None] for o in _adamw(n, w, g, m, v))
    result["w_mlp1"], result["w_mlp2"] = (tuple(o[None] for o in done) for done in mlp_done)
    result["ada_w"] = (g_adaw[None],) + tuple(o[None] for o in adaw_done)

    return (loss, gx[None], *[result[n][0] for n in _WEIGHT_NAMES], *[result[n][1] for n in _WEIGHT_NAMES],
            *[result[n][2] for n in _WEIGHT_NAMES], *[result[n][3] for n in _WEIGHT_NAMES])
```

```python
import functools

import jax
import jax.numpy as jnp
from jax import lax
from jax.experimental import pallas as pl
from jax.experimental.pallas import tpu as pltpu

F32 = jnp.float32
BF16 = jnp.bfloat16
MESH = pl.DeviceIdType.MESH

N_DEV = 8
D = 1024
W = 512
D_IN = 5 * W
D_FF = 4096
FF_BLK = D_FF // N_DEV
EPS = 1e-6
C_GATE = 8.0
CONV_L = 4
CONV_S = 3
HALO = 8

ROWS_W1T, ROWS_W2, ROWS_WOUT, ROWS_WIN = FF_BLK, FF_BLK, D // N_DEV, D_IN // N_DEV
OFF_WOUT = 0
OFF_WIN = OFF_WOUT + ROWS_WOUT
MIX_ROWS = OFF_WIN + ROWS_WIN
OFF_W1T = 0
OFF_W2 = OFF_W1T + ROWS_W1T
MLP_ROWS = OFF_W2 + ROWS_W2
CHIP_FLIPS = (4, 2, 6)
N_KEPT = 6

ADAM_LR = 0.001
ADAM_B1 = 0.9
ADAM_B2 = 0.999
ADAM_EPS = 1e-08
ADAM_WD = 0.01
ADAM_STEP = 10

VMEM_LIMIT = 56 * 1024 * 1024

TB_MIX = 256
TB_MIXB = 256
TB_MLP = 256
TB_MLPB = 512

ANY = pl.BlockSpec(memory_space=pl.ANY)
WHOLE = pl.BlockSpec(memory_space=pltpu.VMEM)


def _dot(a, b):
    return jnp.dot(a, b, preferred_element_type=F32)


def _dot_nt(a, b):
    return lax.dot_general(a, b, (((1,), (1,)), ((), ())), preferred_element_type=F32)


def _dot_tn(a, b):
    return lax.dot_general(a, b, (((0,), (0,)), ((), ())), preferred_element_type=F32)


def _sigmoid(v):
    return 1.0 / (1.0 + jnp.exp(-v))


def _softplus(v):
    t = jnp.exp(-jnp.abs(v))
    small = t * (1.0 - t * (0.5 - t * (1.0 / 3.0)))
    return jnp.maximum(v, 0.0) + jnp.where(t < 1e-2, small, jnp.log(1.0 + t))


def _one_minus_sq(a, log_a):
    return -jnp.tanh(log_a) * (a * a + 1.0)


_GELU_K = 0.7978845608028654
_GELU_C = 0.044715


def _gelu(u):
    th = jnp.tanh(_GELU_K * (u + _GELU_C * u * u * u))
    return 0.5 * u * (1.0 + th), th


def _gelu_grad(u, th):
    return 0.5 * (1.0 + th) + 0.5 * u * (1.0 - th * th) * _GELU_K * (1.0 + 3.0 * _GELU_C * u * u)


def _group_mean(v, avg):
    hi = v.astype(BF16)
    lo = (v - hi.astype(F32)).astype(BF16)
    return _dot(hi, avg) + _dot(lo, avg)


def _colsum(v):
    return jnp.sum(v, axis=0, keepdims=True)


def _rowmean(v):
    return jnp.mean(v, axis=-1, keepdims=True)


def _load_packed(wpack_hbm, off, rows, dst, sem):
    copies = [
        pltpu.make_async_copy(wpack_hbm.at[d, pl.ds(off, rows), :], dst.at[pl.ds(d * rows, rows), :], sem)
        for d in range(N_DEV)
    ]
    for cp in copies:
        cp.start()
    return copies


def _scan_groups(n_groups, a_ref, b_ref, out_ref, carry_ref, reverse):
    row = lax.broadcasted_iota(jnp.int32, (HALO, W), 0)

    def step(k, carry):
        g = (n_groups - 1 - k) if reverse else k
        rows = pl.ds(pl.multiple_of(g * HALO, HALO), HALO)
        a = a_ref[rows, :]
        b = b_ref[rows, :]
        for s in (1, 2, 4):
            if reverse:
                keep = row < HALO - s
                sh = HALO - s
            else:
                keep = row >= s
                sh = s
            a_sh = pltpu.roll(a, sh, axis=0)
            b_sh = pltpu.roll(b, sh, axis=0)
            b = jnp.where(keep, a * b_sh + b, b)
            a = jnp.where(keep, a * a_sh, a)
        h = b + a * carry
        out_ref[rows, :] = h
        edge = h[0:1, :] if reverse else h[HALO - 1:HALO, :]
        return jnp.broadcast_to(edge, (HALO, W))

    carry_ref[...] = lax.fori_loop(0, n_groups, step, carry_ref[...])


def _route_peers(me):
    x, y, c = me
    first = ((x + 1 - c) % 2, (y + c) % 2, c)
    second = ((x + c) % 2, (y + 1 - c) % 2, c)
    return first, second, (1 - x, 1 - y, c)


def _chip_gather_copies(block_hbm, out_hbm, send_sems, recv_sems):
    me = _position()
    first, second, diag = _route_peers(me)

    def copy(j, src, slot_of, to):
        return pltpu.make_async_remote_copy(
            src_ref=src, dst_ref=out_hbm.at[_linear(slot_of)], send_sem=send_sems.at[j], recv_sem=recv_sems.at[j],
            device_id=to, device_id_type=MESH)

    own_sends = [copy(0, block_hbm, me, first), copy(1, block_hbm, me, second)]
    forward = copy(2, out_hbm.at[_linear(first)], first, second)
    arrivals = [copy(0, block_hbm, first, first), copy(1, block_hbm, second, second), copy(2, block_hbm, diag, second)]
    return own_sends, forward, arrivals


def _mixer_fwd(x, modraw, adab, g1, wl, bl, bda, bdx, ba, bxb, ap, ws, gl, gc, avg, wpack, mlp_block):
    t_len = x.shape[0]
    tb = TB_MIX
    nb = t_len // tb

    def body(x_ref, modraw_ref, adab_ref, g1_ref, wl_ref, bl_ref, bda_ref, bdx_ref, ba_ref, bxb_ref, ap_ref,
             ws_ref, gl_ref, gc_ref, avg_ref, wpack_hbm, block_hbm, proj_ref, hl_ref, mixed_ref, kept_ref, wmlp_hbm,
             win_v, wout_v, sem, ulx_ext, cv_ext, hcar, a_s, b_s, send_sems, recv_sems, local_sem):
        i = pl.program_id(0)
        own = pltpu.make_async_copy(block_hbm, wmlp_hbm.at[_linear(_position())], local_sem)
        sends, forward, arrivals = _chip_gather_copies(block_hbm, wmlp_hbm, send_sems, recv_sems)

        @pl.when(i == 0)
        def _():
            own.start()
            for cp in sends:
                cp.start()

        @pl.when(i == nb - 1)
        def _():
            arrivals[0].wait_recv()
            forward.start()

        @pl.when(i == 0)
        def _():
            cps = _load_packed(wpack_hbm, OFF_WIN, ROWS_WIN, win_v, sem.at[0])
            cps += _load_packed(wpack_hbm, OFF_WOUT, ROWS_WOUT, wout_v, sem.at[1])
            ulx_ext[0:HALO, :] = jnp.zeros((HALO, W), F32)
            cv_ext[0:HALO, :] = jnp.zeros((HALO, W), F32)
            hcar[...] = jnp.zeros((HALO, W), F32)
            for cp in cps:
                cp.wait()

        mod = modraw_ref[...] + adab_ref[...]
        shift1, scale1, gate1 = mod[0:1], mod[1:2], mod[2:3]
        x = x_ref[...]
        r1 = lax.rsqrt(_rowmean(x * x) + EPS)
        h = (x * r1 * g1_ref[...]) * (1.0 + scale1) + shift1
        proj = _dot_nt(h.astype(BF16), win_v[...])
        proj_ref[...] = proj
        u_lx, u_ly, u_b, u_c, u_v = (proj[:, k * W:(k + 1) * W] for k in range(5))

        ulx_ext[HALO:HALO + tb, :] = u_lx
        xl = bl_ref[...] + wl_ref[CONV_L - 1:CONV_L, :] * u_lx
        for k in range(CONV_L - 1):
            xl = xl + wl_ref[k:k + 1, :] * ulx_ext[pl.ds(HALO - (CONV_L - 1) + k, tb), :]
        ulx_ext[0:HALO, :] = ulx_ext[tb:tb + HALO, :]
        xlb = xl.astype(BF16)
        r = _sigmoid(_dot(xlb, bda_ref[...]) + ba_ref[...])
        ig = _sigmoid(_dot(xlb, bdx_ref[...]) + bxb_ref[...])
        log_a = (-C_GATE) * r * _softplus(ap_ref[...])
        a = jnp.exp(log_a)
        mult = jnp.sqrt(_one_minus_sq(a, log_a))
        grow = i * tb + lax.broadcasted_iota(jnp.int32, (tb, W), 0)
        mult = jnp.where(grow == 0, 1.0, mult)
        a_s[...] = a
        b_s[...] = mult * (ig * xl)
        _scan_groups(tb // HALO, a_s, b_s, hl_ref, hcar, reverse=False)
        hl = hl_ref[...]
        ge, _ = _gelu(u_ly)
        p = ge * hl
        rp = lax.rsqrt(_group_mean(p * p, avg_ref[...]) + EPS)
        y_lru = p * rp * gl_ref[...]

        cv = u_c * u_v
        cv_ext[HALO:HALO + tb, :] = cv
        cc = ws_ref[CONV_S - 1:CONV_S, :] * cv
        for k in range(CONV_S - 1):
            cc = cc + ws_ref[k:k + 1, :] * cv_ext[pl.ds(HALO - (CONV_S - 1) + k, tb), :]
        cv_ext[0:HALO, :] = cv_ext[tb:tb + HALO, :]
        q = u_b * cc
        rq = lax.rsqrt(_group_mean(q * q, avg_ref[...]) + EPS)
        y_conv = q * rq * gc_ref[...]
        for k, kept in enumerate((xl, r, ig, rp, rq, cc)):
            kept_ref[:, k * W:(k + 1) * W] = kept

        mixed_ref[...] = (_dot(y_lru.astype(BF16), wout_v[0:W, :]) + _dot(y_conv.astype(BF16), wout_v[W:2 * W, :]))

        @pl.when(i == nb - 1)
        def _():
            for cp in arrivals[1:]:
                cp.wait_recv()
            for cp in sends + [forward]:
                cp.wait_send()
            own.wait()

    tok = lambda cols: pl.BlockSpec((tb, cols), lambda i: (i, 0))
    full = lambda a: pl.BlockSpec(a.shape, lambda i: (0,) * a.ndim)
    small = (modraw, adab, g1, wl, bl, bda, bdx, ba, bxb, ap, ws, gl, gc, avg)
    n_chips = len(CHIP_FLIPS)
    return pl.pallas_call(
        body,
        name="mixer_fwd",
        grid=(nb,),
        in_specs=[tok(D)] + [full(a) for a in small] + [ANY, ANY],
        out_specs=[tok(D_IN), tok(W), tok(D), tok(N_KEPT * W), ANY],
        out_shape=[jax.ShapeDtypeStruct((t_len, D_IN), F32), jax.ShapeDtypeStruct((t_len, W), F32),
                   jax.ShapeDtypeStruct((t_len, D), F32), jax.ShapeDtypeStruct((t_len, N_KEPT * W), F32),
                   jax.ShapeDtypeStruct((N_DEV,) + mlp_block.shape, BF16)],
        scratch_shapes=[pltpu.VMEM((D_IN, D), BF16), pltpu.VMEM((D, D), BF16), pltpu.SemaphoreType.DMA((2,)),
                        pltpu.VMEM((tb + HALO, W), F32), pltpu.VMEM((tb + HALO, W), F32), pltpu.VMEM((HALO, W), F32),
                        pltpu.VMEM((tb, W), F32), pltpu.VMEM((tb, W), F32),
                        pltpu.SemaphoreType.DMA((n_chips,)), pltpu.SemaphoreType.DMA((n_chips,)), pltpu.SemaphoreType.DMA],
        compiler_params=pltpu.CompilerParams(dimension_semantics=("arbitrary",), vmem_limit_bytes=VMEM_LIMIT),
    )(x, *small, wpack, mlp_block)


def _sibling_forward(wmlp):
    def body(in_hbm, out_hbm, send_sems, recv_sems):
        x, y, c = _position()
        copies, arrivals = [], []
        for j, k in enumerate((0,) + CHIP_FLIPS):
            mine = out_hbm.at[_linear(_flip((x, y, c), k))]
            theirs = out_hbm.at[_linear(_flip((x, y, 1 - c), k))]
            copies.append(pltpu.make_async_remote_copy(
                src_ref=mine, dst_ref=mine, send_sem=send_sems.at[j], recv_sem=recv_sems.at[j],
                device_id=(x, y, 1 - c), device_id_type=MESH))
            arrivals.append(pltpu.make_async_remote_copy(
                src_ref=theirs, dst_ref=theirs, send_sem=send_sems.at[j], recv_sem=recv_sems.at[j],
                device_id=(x, y, 1 - c), device_id_type=MESH))
        for cp in copies:
            cp.start()
        for cp in arrivals:
            cp.wait_recv()
        for cp in copies:
            cp.wait_send()

    return pl.pallas_call(
        body,
        name="sibling_forward",
        in_specs=[ANY],
        out_specs=ANY,
        out_shape=jax.ShapeDtypeStruct(wmlp.shape, wmlp.dtype),
        input_output_aliases={0: 0},
        scratch_shapes=[pltpu.SemaphoreType.DMA((4,)), pltpu.SemaphoreType.DMA((4,))],
    )(wmlp)


def _mlp_fwd(x, mixed, tgt, modraw, adab, g2, gf, wpack):
    t_len = x.shape[0]
    tb = TB_MLP
    nb = t_len // tb

    def body(x_ref, mixed_ref, tgt_ref, modraw_ref, adab_ref, g2_ref, gf_ref, wpack_hbm,
             h2t_ref, f_ref, dx2_ref, dz_ref, vec_ref, loss_ref, w1t_v, w2_v, sem):
        i = pl.program_id(0)

        @pl.when(i == 0)
        def _():
            cps = _load_packed(wpack_hbm, OFF_W1T, ROWS_W1T, w1t_v, sem.at[0])
            cps += _load_packed(wpack_hbm, OFF_W2, ROWS_W2, w2_v, sem.at[1])
            vec_ref[...] = jnp.zeros(vec_ref.shape, F32)
            loss_ref[...] = jnp.zeros(loss_ref.shape, F32)
            for cp in cps:
                cp.wait()

        mod = modraw_ref[...] + adab_ref[...]
        gate1, shift2, scale2, gate2 = mod[2:3], mod[3:4], mod[4:5], mod[5:6]
        x1 = x_ref[...] + gate1 * mixed_ref[...]
        r2 = lax.rsqrt(_rowmean(x1 * x1) + EPS)
        h2 = (x1 * r2 * g2_ref[...]) * (1.0 + scale2) + shift2
        h2b = h2.astype(BF16)
        h2t_ref[...] = h2.T.astype(BF16)
        z = jnp.zeros((tb, D), F32)
        for j in range(N_DEV):
            cols = slice(j * FF_BLK, (j + 1) * FF_BLK)
            fj = _dot_nt(h2b, w1t_v[cols, :])
            f_ref[:, cols] = fj
            rf = jnp.maximum(fj, 0.0)
            z = z + _dot((rf * rf).astype(BF16), w2_v[cols, :])
        x2 = x1 + gate2 * z
        r3 = lax.rsqrt(_rowmean(x2 * x2) + EPS)
        xn3 = x2 * r3
        diff = xn3 * gf_ref[...] - tgt_ref[...]
        sq = _colsum(diff * diff)
        loss_ref[...] += jnp.broadcast_to(jnp.sum(sq, axis=1, keepdims=True) * (0.5 / D), loss_ref.shape)
        dy = diff * (1.0 / D)
        dyn = dy * gf_ref[...]
        dx2 = r3 * (dyn - xn3 * _rowmean(dyn * xn3))
        dx2_ref[...] = dx2
        dz_ref[...] = (gate2 * dx2).astype(BF16)
        vec_ref[0:1, :] += _colsum(dx2 * z)
        vec_ref[1:2, :] += _colsum(dy * xn3)

    tok = lambda cols: pl.BlockSpec((tb, cols), lambda i: (i, 0))
    tok_t = pl.BlockSpec((D, tb), lambda i: (0, i))
    full = lambda a: pl.BlockSpec(a.shape, lambda i: (0,) * a.ndim)
    small = (modraw, adab, g2, gf)
    return pl.pallas_call(
        body,
        name="mlp_fwd",
        grid=(nb,),
        in_specs=[tok(D), tok(D), tok(D)] + [full(a) for a in small] + [ANY],
        out_specs=[tok_t, tok(D_FF), tok(D), tok(D), pl.BlockSpec((8, D), lambda i: (0, 0)),
                   pl.BlockSpec((8, 128), lambda i: (0, 0))],
        out_shape=[jax.ShapeDtypeStruct((D, t_len), BF16), jax.ShapeDtypeStruct((t_len, D_FF), F32),
                   jax.ShapeDtypeStruct((t_len, D), F32), jax.ShapeDtypeStruct((t_len, D), BF16),
                   jax.ShapeDtypeStruct((8, D), F32), jax.ShapeDtypeStruct((8, 128), F32)],
        scratch_shapes=[pltpu.VMEM((D_FF, D), BF16), pltpu.VMEM((D_FF, D), BF16), pltpu.SemaphoreType.DMA((2,))],
        compiler_params=pltpu.CompilerParams(dimension_semantics=("arbitrary",), vmem_limit_bytes=VMEM_LIMIT),
    )(x, mixed, tgt, *small, wpack)


def _mlp_bwd_half(pos, h2t, f, dz, wpack, prior=None):
    t_len = dz.shape[0]
    tb = TB_MLPB
    nb = t_len // tb
    first = prior is None
    flip = 1 if first else 0

    def body(pos_ref, h2t_ref, f_ref, dz_ref, w1t_ref, w2_ref, *rest):
        if first:
            dh2_ref, dw1_ref, dw2_ref = rest
        else:
            dh2in_ref, _, dh2_ref, dw1_ref, dw2_ref = rest
        k = pl.program_id(0)
        t = pl.program_id(1)
        rows = pl.ds(pl.multiple_of(t * tb, tb), tb)
        w1t = w1t_ref[0]
        w2 = w2_ref[0]
        dz = dz_ref[...]
        rf = jnp.maximum(f_ref[...], 0.0)
        df = (_dot_nt(dz, w2) * (2.0 * rf)).astype(BF16)
        dh = _dot(df, w1t)
        g1 = _dot(h2t_ref[...], df)
        g2 = _dot_tn((rf * rf).astype(BF16), dz)

        @pl.when(t == 0)
        def _():
            dw2_ref[0] = g2
            dw1_ref[0] = g1

        @pl.when(t != 0)
        def _():
            dw2_ref[0] += g2
            dw1_ref[0] += g1

        @pl.when(k == 0)
        def _():
            dh2_ref[rows, :] = dh if first else dh2in_ref[...] + dh

        @pl.when(k != 0)
        def _():
            dh2_ref[rows, :] += dh

    blk = lambda k, pos: 2 * k + jnp.bitwise_xor(pos[0], flip)
    in_specs = [pl.BlockSpec((D, tb), lambda k, t, pos: (0, t)),
                pl.BlockSpec((tb, FF_BLK), lambda k, t, pos: (t, blk(k, pos))),
                pl.BlockSpec((tb, D), lambda k, t, pos: (t, 0)),
                pl.BlockSpec((1, ROWS_W1T, D), lambda k, t, pos: (blk(k, pos), OFF_W1T // ROWS_W1T, 0)),
                pl.BlockSpec((1, ROWS_W2, D), lambda k, t, pos: (blk(k, pos), OFF_W2 // ROWS_W2, 0))]
    grad_specs = [pl.BlockSpec((1, D, FF_BLK), lambda k, t, pos: (k, 0, 0)),
                  pl.BlockSpec((1, FF_BLK, D), lambda k, t, pos: (k, 0, 0))]
    out_specs = [pl.BlockSpec((t_len, D), lambda k, t, pos: (0, 0))] + grad_specs
    grad_shapes = [jax.ShapeDtypeStruct((4, D, FF_BLK), F32), jax.ShapeDtypeStruct((4, FF_BLK, D), F32)]
    out_shape = [jax.ShapeDtypeStruct((t_len, D), F32)] + grad_shapes
    args = [pos, h2t, f, dz, wpack, wpack]
    if not first:
        in_specs += [pl.BlockSpec((tb, D), lambda k, t, pos: (jnp.where(k == 0, t, nb - 1), 0)),
                     pl.BlockSpec(prior[1].shape, lambda k, t, pos: (0,) * prior[1].ndim)]
        args += list(prior)
    return pl.pallas_call(
        body,
        name="mlp_bwd_first" if first else "mlp_bwd_second",
        grid_spec=pltpu.PrefetchScalarGridSpec(num_scalar_prefetch=1, grid=(4, nb), in_specs=in_specs,
                                               out_specs=out_specs),
        out_shape=out_shape,
        compiler_params=pltpu.CompilerParams(dimension_semantics=("arbitrary", "arbitrary"),
                                             vmem_limit_bytes=VMEM_LIMIT),
    )(*args)


V_SHIFT1, V_SCALE1, V_GATE1, V_SHIFT2, V_SCALE2, V_G1, V_G2 = 0, 1, 2, 3, 4, 6, 7
V_BL_BA, V_BX_SP, V_GL_GC, V_WL01, V_WL23, V_WS01, V_WS2 = 8, 9, 10, 11, 12, 13, 14
V_ROWS = 16


def _chip_scatter_copies(srcs, dsts, send_sems, recv_sems, row_ranges=None):
    me = _position()
    copies = []
    for a, (src, dst) in enumerate(zip(srcs, dsts)):
        band = pl.ds(*row_ranges[a]) if row_ranges else slice(None)
        for j, k in enumerate(CHIP_FLIPS):
            peer = _flip(me, k)
            copies.append(pltpu.make_async_remote_copy(
                src_ref=src.at[2 * peer[0] + peer[1], band], dst_ref=dst.at[j, band],
                send_sem=send_sems.at[len(CHIP_FLIPS) * a + j], recv_sem=recv_sems.at[len(CHIP_FLIPS) * a + j],
                device_id=peer, device_id_type=MESH))
    return copies


def _mixer_bwd(x, mixed, dh2, dx2, proj, hl, kept, modraw, adab, g1, g2, wl, bl, bda, bdx, ba, bxb, ap, ws, gl, gc, avg, wpack,
               chip_sums, chip_rows):
    t_len = x.shape[0]
    tb = TB_MIXB
    nb = t_len // tb
    hb = tb // HALO
    n_sums = len(chip_sums)

    def body(x_ref, mixed_ref, dh2_ref, dx2_ref, proj_ref, projh_ref, hl_ref, hlh_ref, kept_ref,
             modraw_ref, adab_ref, g1_ref, g2_ref, wl_ref, bl_ref, bda_ref, bdx_ref, ba_ref, bxb_ref, ap_ref,
             ws_ref, gl_ref, gc_ref, avg_ref, wpack_hbm, *rest):
        sums_hbm, rest = rest[:n_sums], rest[n_sums:]
        gx_ref, vec_ref, hb_ref, dprojt_ref, dmixed_ref, ycatt_ref, xlt_ref, dgate_ref = rest[:8]
        landed_hbm, rest = rest[8:8 + n_sums], rest[8 + n_sums:]
        (win_v, wout_v, sem, ulx_ext, cv_ext, hl_ext, a_ext, dxl_ext, dcc_ext, dcar, an_s, g_s, dh_s,
         send_sems, recv_sems) = rest
        i = pl.program_id(0)
        blk = nb - 1 - i
        chip_copies = _chip_scatter_copies(sums_hbm, landed_hbm, send_sems, recv_sems, chip_rows)

        @pl.when(i == 0)
        def _():
            for cp in chip_copies:
                cp.start()
            cps = _load_packed(wpack_hbm, OFF_WIN, ROWS_WIN, win_v, sem.at[0])
            cps += _load_packed(wpack_hbm, OFF_WOUT, ROWS_WOUT, wout_v, sem.at[1])
            vec_ref[...] = jnp.zeros(vec_ref.shape, F32)
            zero = jnp.zeros((HALO, W), F32)
            a_ext[tb:tb + HALO, :] = zero
            dxl_ext[tb:tb + HALO, :] = zero
            dcc_ext[tb:tb + HALO, :] = zero
            dcar[...] = zero
            for cp in cps:
                cp.wait()

        mod = modraw_ref[...] + adab_ref[...]
        shift1, scale1, gate1, scale2 = mod[0:1], mod[1:2], mod[2:3], mod[4:5]
        x = x_ref[...]
        mixed = mixed_ref[...]

        x1 = x + gate1 * mixed
        r2 = lax.rsqrt(_rowmean(x1 * x1) + EPS)
        xn2 = x1 * r2
        dh2 = dh2_ref[...]
        vec_ref[V_SHIFT2:V_SHIFT2 + 1, :] += _colsum(dh2)
        vec_ref[V_SCALE2:V_SCALE2 + 1, :] += _colsum(dh2 * xn2 * g2_ref[...])
        vec_ref[V_G2:V_G2 + 1, :] += _colsum(dh2 * (1.0 + scale2) * xn2)
        dxn2 = dh2 * g2_ref[...] * (1.0 + scale2)
        dx1 = dx2_ref[...] + r2 * (dxn2 - xn2 * _rowmean(dxn2 * xn2))
        vec_ref[V_GATE1:V_GATE1 + 1, :] += _colsum(dx1 * mixed)
        dmixed = (gate1 * dx1).astype(BF16)

        proj = proj_ref[...]
        u_lx, u_ly, u_b, u_c, u_v = (proj[:, k * W:(k + 1) * W] for k in range(5))
        has_prev = (blk > 0).astype(F32)
        projh = projh_ref[...]
        ulx_ext[0:HALO, :] = projh[:, 0:W] * has_prev
        ulx_ext[HALO:HALO + tb, :] = u_lx
        xl, r, ig, rp, rq, cc = (kept_ref[:, k * W:(k + 1) * W] for k in range(N_KEPT))
        sp = _softplus(ap_ref[...])
        log_a = (-C_GATE) * r * sp
        a = jnp.exp(log_a)
        mult_raw = jnp.sqrt(_one_minus_sq(a, log_a))
        first = (blk * tb + lax.broadcasted_iota(jnp.int32, (tb, W), 0)) == 0
        mult = jnp.where(first, 1.0, mult_raw)
        hl = hl_ref[...]
        ge, th = _gelu(u_ly)
        pn = ge * hl * rp
        cv = u_c * u_v
        cv_ext[0:HALO, :] = projh[:, 3 * W:4 * W] * projh[:, 4 * W:5 * W] * has_prev
        cv_ext[HALO:HALO + tb, :] = cv
        qn = u_b * cc * rq

        dmixed_ref[...] = dmixed
        ycatt_ref[0:W, :] = (pn * gl_ref[...]).T.astype(BF16)
        ycatt_ref[W:2 * W, :] = (qn * gc_ref[...]).T.astype(BF16)
        dyl = _dot_nt(dmixed, wout_v[0:W, :])
        dyc = _dot_nt(dmixed, wout_v[W:2 * W, :])

        dqn = dyc * gc_ref[...]
        dq = rq * (dqn - qn * _group_mean(dqn * qn, avg_ref[...]))
        du_b = dq * cc
        dcc = dq * u_b
        dcc_ext[0:tb, :] = dcc
        dcv = ws_ref[CONV_S - 1:CONV_S, :] * dcc
        for k in range(CONV_S - 1):
            dcv = dcv + ws_ref[k:k + 1, :] * dcc_ext[pl.ds(CONV_S - 1 - k, tb), :]
        dcc_ext[tb:tb + HALO, :] = dcc_ext[0:HALO, :]
        du_c = dcv * u_v
        du_v = dcv * u_c
        dws = [_colsum(dcc * cv_ext[pl.ds(HALO - (CONV_S - 1) + k, tb), :]) for k in range(CONV_S)]

        dpn = dyl * gl_ref[...]
        dp = rp * (dpn - pn * _group_mean(dpn * pn, avg_ref[...]))
        du_ly = dp * hl * _gelu_grad(u_ly, th)
        g_s[...] = dp * ge
        a_ext[0:tb, :] = a
        an_s[...] = a_ext[pl.ds(1, tb), :]
        _scan_groups(hb, an_s, g_s, dh_s, dcar, reverse=True)
        a_ext[tb:tb + HALO, :] = a_ext[0:HALO, :]
        dh = dh_s[...]
        hl_ext[0:HALO, :] = hlh_ref[...] * has_prev
        hl_ext[HALO:HALO + tb, :] = hl
        da = dh * hl_ext[pl.ds(HALO - 1, tb), :]
        dmult = dh * (ig * xl)
        dig = dh * (mult * xl)
        dxl = dh * (mult * ig)
        dlog = da * a - jnp.where(first, 0.0, dmult * (a * a) / mult_raw)
        dr = dlog * ((-C_GATE) * sp)
        dsp = _colsum(dlog * ((-C_GATE) * r))
        dga = dr * r * (1.0 - r)
        dgx = dig * ig * (1.0 - ig)
        dgab = dga.astype(BF16)
        dgxb = dgx.astype(BF16)
        xlt_ref[...] = xl.T.astype(BF16)
        dgate_ref[:, 0:W] = dgab
        dgate_ref[:, W:2 * W] = dgxb
        dxl = dxl + _dot_nt(dgab, bda_ref[...]) + _dot_nt(dgxb, bdx_ref[...])
        dxl_ext[0:tb, :] = dxl
        du_lx = wl_ref[CONV_L - 1:CONV_L, :] * dxl
        for k in range(CONV_L - 1):
            du_lx = du_lx + wl_ref[k:k + 1, :] * dxl_ext[pl.ds(CONV_L - 1 - k, tb), :]
        dxl_ext[tb:tb + HALO, :] = dxl_ext[0:HALO, :]
        dwl = [_colsum(dxl * ulx_ext[pl.ds(HALO - (CONV_L - 1) + k, tb), :]) for k in range(CONV_L)]

        cat = lambda u, v: jnp.concatenate([u, v], axis=1)
        vec_ref[V_BL_BA:V_BL_BA + 1, :] += cat(_colsum(dxl), _colsum(dga))
        vec_ref[V_BX_SP:V_BX_SP + 1, :] += cat(_colsum(dgx), dsp)
        vec_ref[V_GL_GC:V_GL_GC + 1, :] += cat(_colsum(dyl * pn), _colsum(dyc * qn))
        vec_ref[V_WL01:V_WL01 + 1, :] += cat(dwl[0], dwl[1])
        vec_ref[V_WL23:V_WL23 + 1, :] += cat(dwl[2], dwl[3])
        vec_ref[V_WS01:V_WS01 + 1, :] += cat(dws[0], dws[1])
        vec_ref[V_WS2:V_WS2 + 1, 0:W] += dws[2]

        r1 = lax.rsqrt(_rowmean(x * x) + EPS)
        xn1 = x * r1
        hb_ref[...] = ((xn1 * g1_ref[...]) * (1.0 + scale1) + shift1).astype(BF16)
        dh_in = jnp.zeros((tb, D), F32)
        for k, du in enumerate((du_lx, du_ly, du_b, du_c, du_v)):
            dprojt_ref[k * W:(k + 1) * W, :] = du.T.astype(BF16)
            dh_in = dh_in + _dot(du.astype(BF16), win_v[k * W:(k + 1) * W, :])
        vec_ref[V_SHIFT1:V_SHIFT1 + 1, :] += _colsum(dh_in)
        vec_ref[V_SCALE1:V_SCALE1 + 1, :] += _colsum(dh_in * xn1 * g1_ref[...])
        vec_ref[V_G1:V_G1 + 1, :] += _colsum(dh_in * (1.0 + scale1) * xn1)
        dxn1 = dh_in * g1_ref[...] * (1.0 + scale1)
        gx_ref[...] = dx1 + r1 * (dxn1 - xn1 * _rowmean(dxn1 * xn1))

        @pl.when(i == nb - 1)
        def _():
            for cp in chip_copies:
                cp.wait_recv()
            for cp in chip_copies:
                cp.wait_send()

    rev = lambda cols: pl.BlockSpec((tb, cols), lambda i: (nb - 1 - i, 0))
    rev_t = lambda rows: pl.BlockSpec((rows, tb), lambda i: (0, nb - 1 - i))
    halo = lambda cols: pl.BlockSpec((HALO, cols), lambda i: (jnp.maximum((nb - 1 - i) * hb - 1, 0), 0))
    full = lambda a: pl.BlockSpec(a.shape, lambda i: (0,) * a.ndim)
    small = (modraw, adab, g1, g2, wl, bl, bda, bdx, ba, bxb, ap, ws, gl, gc, avg)
    ext = pltpu.VMEM((tb + HALO, W), F32)
    n_sems = max(len(CHIP_FLIPS) * n_sums, 1)
    return pl.pallas_call(
        body,
        name="mixer_bwd",
        grid=(nb,),
        in_specs=[rev(D), rev(D), rev(D), rev(D), rev(D_IN), halo(D_IN), rev(W), halo(W), rev(N_KEPT * W)]
        + [full(a) for a in small] + [ANY] * (1 + n_sums),
        out_specs=[rev(D), pl.BlockSpec((V_ROWS, D), lambda i: (0, 0)), rev(D), rev_t(D_IN), rev(D), rev_t(D),
                   rev_t(W), rev(2 * W)] + [ANY] * n_sums,
        out_shape=[jax.ShapeDtypeStruct((t_len, D), F32), jax.ShapeDtypeStruct((V_ROWS, D), F32),
                   jax.ShapeDtypeStruct((t_len, D), BF16), jax.ShapeDtypeStruct((D_IN, t_len), BF16),
                   jax.ShapeDtypeStruct((t_len, D), BF16), jax.ShapeDtypeStruct((D, t_len), BF16),
                   jax.ShapeDtypeStruct((W, t_len), BF16), jax.ShapeDtypeStruct((t_len, 2 * W), BF16)]
        + [jax.ShapeDtypeStruct((len(CHIP_FLIPS),) + s.shape[1:], s.dtype) for s in chip_sums],
        scratch_shapes=[pltpu.VMEM((D_IN, D), BF16), pltpu.VMEM((D, D), BF16), pltpu.SemaphoreType.DMA((2,)),
                        ext, ext, ext, ext, ext, ext, pltpu.VMEM((HALO, W), F32),
                        pltpu.VMEM((tb, W), F32), pltpu.VMEM((tb, W), F32), pltpu.VMEM((tb, W), F32),
                        pltpu.SemaphoreType.DMA((n_sems,)), pltpu.SemaphoreType.DMA((n_sems,))],
        compiler_params=pltpu.CompilerParams(dimension_semantics=("arbitrary",), vmem_limit_bytes=VMEM_LIMIT),
    )(x, mixed, dh2, dx2, proj, proj, hl, hl, kept, *small, wpack, *chip_sums)


def _matmul(name, a, b, tm=512):
    m, k = a.shape
    n = b.shape[1]

    def body(a_ref, b_ref, o_ref):
        o_ref[...] = _dot(a_ref[...], b_ref[...])

    return pl.pallas_call(
        body,
        name=name,
        grid=(m // tm,),
        in_specs=[pl.BlockSpec((tm, k), lambda i: (i, 0)), pl.BlockSpec((k, n), lambda i: (0, 0))],
        out_specs=pl.BlockSpec((tm, n), lambda i: (i, 0)),
        out_shape=jax.ShapeDtypeStruct((m, n), F32),
        compiler_params=pltpu.CompilerParams(dimension_semantics=("arbitrary",), vmem_limit_bytes=VMEM_LIMIT),
    )(a, b)


def _gate_wgrad(xl_t, dgate, avg):
    hd = W // 8

    def body(a_ref, b_ref, avg_ref, o_ref):
        full = _dot(a_ref[...], b_ref[...])
        row = lax.broadcasted_iota(jnp.int32, (W, hd), 0)
        col = lax.broadcasted_iota(jnp.int32, (W, hd), 1)
        fold = ((row & (hd - 1)) == col).astype(BF16)
        keep = avg_ref[...] != 0
        for g in range(2):
            m = jnp.where(keep, full[:, g * W:(g + 1) * W], 0.0)
            hi = m.astype(BF16)
            rest = m - hi.astype(F32)
            mid = rest.astype(BF16)
            lo = (rest - mid.astype(F32)).astype(BF16)
            o_ref[g] = _dot(hi, fold) + _dot(mid, fold) + _dot(lo, fold)

    return pl.pallas_call(
        body,
        name="wgrad_gate",
        in_specs=[WHOLE] * 3,
        out_specs=WHOLE,
        out_shape=jax.ShapeDtypeStruct((2, W, hd), F32),
        compiler_params=pltpu.CompilerParams(vmem_limit_bytes=VMEM_LIMIT),
    )(xl_t, dgate, avg)


def _block_diag(w):
    n, m, _ = w.shape
    eye = jnp.eye(n, dtype=w.dtype)
    return (w[:, :, None, :] * eye[:, None, :, None]).reshape(n * m, n * m)


def _pad_rows(a, rows):
    return jnp.pad(a, ((0, rows - a.shape[0]),) + ((0, 0),) * (a.ndim - 1))


def _position():
    return lax.axis_index("x"), lax.axis_index("y"), lax.axis_index("c")


def _linear(pos):
    return 4 * pos[0] + 2 * pos[1] + pos[2]


def _flip(pos, k):
    return tuple(1 - p if k & bit else p for p, bit in zip(pos, (4, 2, 1)))


def _exchange_all(make_copy, make_arrival):
    copies = [make_copy(k) for k in range(1, N_DEV)]
    for cp in copies:
        cp.start()
    for k in range(1, N_DEV):
        make_arrival(k).wait_recv()
    for cp in copies:
        cp.wait_send()


def _mod_exchange_steps(cols):
    def steps(msg_ref, adaw_ref, gath_ref, mod_ref, sendbuf, send_a, recv_a, send_b, recv_b):
        me = _position()
        me_lin = _linear(me)
        m = msg_ref[...]
        row = lax.broadcasted_iota(jnp.int32, m.shape, 0)
        gath_ref[me_lin] = jnp.where(row == 0, m * _sigmoid(m), m)

        def gather_copy(k, src_lin):
            return pltpu.make_async_remote_copy(
                src_ref=gath_ref.at[src_lin], dst_ref=gath_ref.at[src_lin], send_sem=send_a.at[k - 1],
                recv_sem=recv_a.at[k - 1], device_id=_flip(me, k), device_id_type=MESH)

        _exchange_all(lambda k: gather_copy(k, me_lin), lambda k: gather_copy(k, _linear(_flip(me, k))))

        sc_all = gath_ref[:, 0, :]
        scb = jnp.concatenate([sc_all, jnp.zeros_like(sc_all)], axis=0).astype(BF16)
        prod = _dot(scb, adaw_ref[...].astype(BF16))
        for b in range(N_DEV):
            sendbuf[b] = jnp.broadcast_to(prod[b:b + 1, :], (HALO, cols))
        mod_ref[me_lin] = sendbuf[me_lin]

        def row_copy(k, dst_lin):
            peer = _flip(me, k)
            return pltpu.make_async_remote_copy(
                src_ref=sendbuf.at[_linear(peer)], dst_ref=mod_ref.at[dst_lin], send_sem=send_b.at[k - 1],
                recv_sem=recv_b.at[k - 1], device_id=peer, device_id_type=MESH)

        _exchange_all(lambda k: row_copy(k, me_lin), lambda k: row_copy(k, _linear(_flip(me, k))))

    return steps


def _gather_and_mod(msg, ada_w, block):
    rows, cols = block.shape
    mod_cols = ada_w.shape[1]
    mod_steps = _mod_exchange_steps(mod_cols)

    def body(msg_ref, adaw_ref, x_ref, gath_ref, mod_ref, out_ref, sendbuf, send_a, recv_a, send_b, recv_b,
             send_sems, recv_sems, sib_send_sems, sib_recv_sems, local_sem):
        x, y, c = _position()
        me, sibling = (x, y, c), (x, y, 1 - c)
        sends, forward, arrivals = _chip_gather_copies(x_ref, out_ref, send_sems, recv_sems)

        def to_sibling(j, block_of, src=None):
            dst = out_ref.at[_linear(block_of)]
            return pltpu.make_async_remote_copy(
                src_ref=dst if src is None else src, dst_ref=dst, send_sem=sib_send_sems.at[j],
                recv_sem=sib_recv_sems.at[j], device_id=sibling, device_id_type=MESH)

        mine = pltpu.make_async_copy(x_ref, out_ref.at[_linear(me)], local_sem)
        mine.start()
        passes = [to_sibling(0, me, src=x_ref)] + [to_sibling(1 + j, p) for j, p in enumerate(_route_peers(me))]
        passes[0].start()
        for cp in sends:
            cp.start()
        mod_steps(msg_ref, adaw_ref, gath_ref, mod_ref, sendbuf, send_a, recv_a, send_b, recv_b)
        arrivals[0].wait_recv()
        forward.start()
        passes[1].start()
        arrivals[1].wait_recv()
        passes[2].start()
        arrivals[2].wait_recv()
        passes[3].start()
        for j, p in enumerate((sibling,) + _route_peers(sibling)):
            to_sibling(j, p).wait_recv()
        for cp in sends + [forward] + passes:
            cp.wait_send()
        mine.wait()

    return pl.pallas_call(
        body,
        name="gather_and_mod",
        in_specs=[WHOLE, WHOLE, ANY],
        out_specs=[WHOLE, WHOLE, ANY],
        out_shape=[jax.ShapeDtypeStruct((N_DEV, HALO, D), F32), jax.ShapeDtypeStruct((N_DEV, HALO, mod_cols), F32),
                   jax.ShapeDtypeStruct((N_DEV, rows, cols), block.dtype)],
        scratch_shapes=[pltpu.VMEM((N_DEV, HALO, mod_cols), F32)] + [pltpu.SemaphoreType.DMA((N_DEV - 1,))] * 4
        + [pltpu.SemaphoreType.DMA((3,)), pltpu.SemaphoreType.DMA((3,)), pltpu.SemaphoreType.DMA((4,)),
           pltpu.SemaphoreType.DMA((4,)), pltpu.SemaphoreType.DMA],
        compiler_params=pltpu.CompilerParams(vmem_limit_bytes=VMEM_LIMIT),
    )(msg, ada_w, block)


HBM = pl.BlockSpec(memory_space=pltpu.HBM)
SEM = pl.BlockSpec(memory_space=pltpu.SEMAPHORE)
EFFECT = pltpu.SideEffectType.DATAFLOW_SIDE_EFFECTING


def _stage_copies(stage):
    return {"chips": (_chip_scatter_copies, len(CHIP_FLIPS), len(CHIP_FLIPS)), "sibling": (_sibling_copies, 4, 4)}[stage]


def _chips_start(which, chip_sums, stage="chips"):
    n = len(chip_sums)
    make_copies, per_array, slots = _stage_copies(stage)
    n_sems = per_array * n

    def body(*refs):
        srcs, dsts = refs[:n], refs[n:2 * n]
        send_sems, recv_sems = refs[2 * n:2 * n + 2]
        token = refs[-1]
        for cp in make_copies(srcs, dsts, send_sems, recv_sems):
            cp.start()
        token[...] = jnp.zeros(token.shape, token.dtype)

    landing = [jax.ShapeDtypeStruct((slots,) + s.shape[-2:], s.dtype) for s in chip_sums]
    outs = pl.pallas_call(
        body,
        name=which + "_" + stage + "_start",
        in_specs=[HBM] * (2 * n),
        out_specs=[SEM, SEM] + [HBM] * (2 * n) + [WHOLE],
        out_shape=[pltpu.SemaphoreType.DMA((n_sems,)), pltpu.SemaphoreType.DMA((n_sems,))]
        + [pltpu.HBM(s.shape, s.dtype) for s in chip_sums] + [pltpu.HBM(s.shape, s.dtype) for s in landing]
        + [jax.ShapeDtypeStruct((HALO, 128), F32)],
        input_output_aliases={i: 2 + i for i in range(2 * n)},
        compiler_params=pltpu.CompilerParams(has_side_effects=EFFECT),
    )(*[pltpu.with_memory_space_constraint(s, pltpu.HBM) for s in chip_sums],
      *[pltpu.with_memory_space_constraint(lax.empty(s.shape, s.dtype), pltpu.HBM) for s in landing])
    return outs[0], outs[1], outs[2:2 + n], outs[2 + n:2 + 2 * n], outs[-1]


def _chips_wait(which, send_sems, recv_sems, srcs, landed, after, stage="chips"):
    n = len(srcs)
    make_copies = _stage_copies(stage)[0]

    def body(*refs):
        src_refs, dst_refs = refs[:n], refs[n:2 * n]
        sends, recvs = refs[2 * n:2 * n + 2]
        copies = make_copies(src_refs, dst_refs, sends, recvs)
        for cp in copies:
            cp.wait_send()
        for cp in copies:
            cp.wait_recv()

    outs = pl.pallas_call(
        body,
        name=which + "_" + stage + "_wait",
        in_specs=[HBM] * (2 * n) + [SEM, SEM, ANY],
        out_specs=[HBM] * (2 * n),
        out_shape=[pltpu.HBM(s.shape, s.dtype) for s in list(srcs) + list(landed)],
        input_output_aliases={i: i for i in range(2 * n)},
        compiler_params=pltpu.CompilerParams(has_side_effects=EFFECT),
    )(*srcs, *landed, send_sems, recv_sems, after)
    return list(outs[:n]), list(outs[n:])


def _sibling_copies(srcs, dsts, send_sems, recv_sems):
    x, y, c = _position()
    copies = []
    for a, (src, dst) in enumerate(zip(srcs, dsts)):
        for k in range(4):
            copies.append(pltpu.make_async_remote_copy(
                src_ref=src.at[k, 1 - c] if len(src.shape) == 4 else src.at[k], dst_ref=dst.at[k],
                send_sem=send_sems.at[4 * a + k],
                recv_sem=recv_sems.at[4 * a + k], device_id=(x, y, 1 - c), device_id_type=MESH))
    return copies


def _row_block(rows):
    return min(rows, 512)


def _pair_sum(pos, mine, recv):
    _, cores, rows, cols = mine.shape
    rb = _row_block(rows)

    def body(pos_ref, mine_ref, recv_ref, out_ref):
        out_ref[0] = (mine_ref[0, 0] + recv_ref[0]).astype(BF16)

    other = lambda k, pos: jnp.bitwise_xor(pos[1], k + 1)
    core = lambda pos: pos[0] * (cores - 1)
    return pl.pallas_call(
        body,
        name="grad_pair_sum",
        grid_spec=pltpu.PrefetchScalarGridSpec(
            num_scalar_prefetch=1, grid=(3, rows // rb),
            in_specs=[pl.BlockSpec((1, 1, rb, cols), lambda k, r, pos: (other(k, pos), core(pos), r, 0)),
                      pl.BlockSpec((1, rb, cols), lambda k, r, pos: (other(k, pos), r, 0))],
            out_specs=pl.BlockSpec((1, rb, cols), lambda k, r, pos: (other(k, pos), r, 0))),
        out_shape=jax.ShapeDtypeStruct((4, rows, cols), BF16),
        compiler_params=pltpu.CompilerParams(dimension_semantics=("arbitrary", "arbitrary")),
    )(pos, mine, recv)


def _final_sum(pos, mine, recv, chips):
    _, cores, rows, cols = mine.shape
    rb = _row_block(rows)

    def body(pos_ref, mine_ref, recv_ref, chips_ref, out_ref):
        g = mine_ref[0, 0] + recv_ref[0]
        for j in range(3):
            g = g + chips_ref[j].astype(F32)
        out_ref[...] = g

    return pl.pallas_call(
        body,
        name="grad_final_sum",
        grid_spec=pltpu.PrefetchScalarGridSpec(
            num_scalar_prefetch=1, grid=(rows // rb,),
            in_specs=[pl.BlockSpec((1, 1, rb, cols), lambda r, pos: (pos[1], pos[0] * (cores - 1), r, 0)),
                      pl.BlockSpec((1, rb, cols), lambda r, pos: (pos[1], r, 0)),
                      pl.BlockSpec((3, rb, cols), lambda r, pos: (0, r, 0))],
            out_specs=pl.BlockSpec((rb, cols), lambda r, pos: (r, 0))),
        out_shape=jax.ShapeDtypeStruct((rows, cols), F32),
        compiler_params=pltpu.CompilerParams(dimension_semantics=("arbitrary",)),
    )(pos, mine, recv, chips)


LOSS_ROW = V_ROWS + 8
GB_BASE = LOSS_ROW + 8


def _route_mod_grads(gmod8, sc_t, after):
    cols = gmod8.shape[1]

    def body(gmod_ref, sct_ref, after_ref, gadaw_ref, gb_ref, sendbuf, grecv, send_a, recv_a):
        me = _position()
        me_lin = _linear(me)
        gm = gmod_ref[...]
        for b in range(N_DEV):
            sendbuf[b] = jnp.broadcast_to(gm[b:b + 1, :], (HALO, cols))
        grecv[me_lin] = sendbuf[me_lin]

        def row_copy(k, dst_lin):
            peer = _flip(me, k)
            return pltpu.make_async_remote_copy(
                src_ref=sendbuf.at[_linear(peer)], dst_ref=grecv.at[dst_lin], send_sem=send_a.at[k - 1],
                recv_sem=recv_a.at[k - 1], device_id=peer, device_id_type=MESH)

        _exchange_all(lambda k: row_copy(k, me_lin), lambda k: row_copy(k, _linear(_flip(me, k))))
        g_all = grecv[:, 0, :]
        g_pad = jnp.concatenate([g_all, jnp.zeros((sct_ref.shape[1] - N_DEV, cols), F32)], axis=0).astype(BF16)
        gadaw_ref[...] = _dot(sct_ref[...], g_pad)
        gb_ref[...] = jnp.broadcast_to(_colsum(g_all), (HALO, cols))

    return pl.pallas_call(
        body,
        name="route_mod_grads",
        in_specs=[WHOLE] * 3,
        out_specs=[WHOLE, WHOLE],
        out_shape=[jax.ShapeDtypeStruct((D, cols), F32), jax.ShapeDtypeStruct((HALO, cols), F32)],
        scratch_shapes=[pltpu.VMEM((N_DEV, HALO, cols), F32), pltpu.VMEM((N_DEV, HALO, cols), F32),
                        pltpu.SemaphoreType.DMA((N_DEV - 1,)), pltpu.SemaphoreType.DMA((N_DEV - 1,))],
        compiler_params=pltpu.CompilerParams(vmem_limit_bytes=VMEM_LIMIT),
    )(gmod8, sc_t, after)


def _chips_and_gather(msg_vec, msg_gate, gb_rows, chip_sums):
    cols = gb_rows.shape[1]
    n = len(chip_sums)
    vec_rows = GB_BASE + N_DEV

    def body(vec_ref, gate_ref, gb_ref, *refs):
        srcs, (sumv_ref, sumg_ref), dsts = refs[:n], refs[n:n + 2], refs[n + 2:2 * n + 2]
        (myv, myg, sibv, sibg, chipv, chipg, sib_send, sib_recv, peer_send, peer_recv,
         chip_send, chip_recv) = refs[2 * n + 2:]
        chip_copies = _chip_scatter_copies(srcs, dsts, chip_send, chip_recv)
        for cp in chip_copies:
            cp.start()
        x, y, c = me = _position()
        my_chip = 2 * x + y
        myv[0:GB_BASE, :] = vec_ref[...]
        slot = lax.broadcasted_iota(jnp.int32, (N_DEV, D), 0) == _linear(me)
        gb_wide = jnp.concatenate([gb_ref[...], jnp.zeros((N_DEV, D - cols), F32)], axis=1)
        myv[GB_BASE:vec_rows, :] = jnp.where(slot, gb_wide, 0.0)
        myg[...] = gate_ref[...]

        swaps = [pltpu.make_async_remote_copy(
            src_ref=src, dst_ref=dst, send_sem=sib_send.at[a], recv_sem=sib_recv.at[a], device_id=(x, y, 1 - c),
            device_id_type=MESH) for a, (src, dst) in enumerate(((myv, sibv), (myg, sibg)))]
        for cp in swaps:
            cp.start()
        for cp in swaps:
            cp.wait_recv()
        chipv[my_chip] = myv[...] + sibv[...]
        chipg[my_chip] = myg[...] + sibg[...]

        def chip_copy(a, buf, j, k, slot_chip):
            peer = _flip(me, k)
            return pltpu.make_async_remote_copy(
                src_ref=buf.at[slot_chip], dst_ref=buf.at[slot_chip], send_sem=peer_send.at[3 * a + j],
                recv_sem=peer_recv.at[3 * a + j], device_id=peer, device_id_type=MESH)

        sends = [chip_copy(a, buf, j, k, my_chip) for a, buf in enumerate((chipv, chipg)) for j, k in enumerate(CHIP_FLIPS)]
        for cp in sends:
            cp.start()
        for a, buf in enumerate((chipv, chipg)):
            for j, k in enumerate(CHIP_FLIPS):
                peer = _flip(me, k)
                chip_copy(a, buf, j, k, 2 * peer[0] + peer[1]).wait_recv()
        sumv_ref[...] = ((chipv[0] + chipv[1]) + chipv[2]) + chipv[3]
        sumg_ref[...] = ((chipg[0] + chipg[1]) + chipg[2]) + chipg[3]
        for cp in swaps + sends:
            cp.wait_send()
        for cp in chip_copies:
            cp.wait_recv()
        for cp in chip_copies:
            cp.wait_send()

    n_chip = max(len(CHIP_FLIPS) * n, 1)
    vshape, gshape = (vec_rows, D), msg_gate.shape
    return pl.pallas_call(
        body,
        name="chips_and_gather",
        in_specs=[WHOLE] * 3 + [ANY] * n,
        out_specs=[WHOLE] * 2 + [ANY] * n,
        out_shape=[jax.ShapeDtypeStruct(vshape, F32), jax.ShapeDtypeStruct(gshape, F32)]
        + [jax.ShapeDtypeStruct((len(CHIP_FLIPS),) + s.shape[1:], s.dtype) for s in chip_sums],
        scratch_shapes=[pltpu.VMEM(vshape, F32), pltpu.VMEM(gshape, F32), pltpu.VMEM(vshape, F32),
                        pltpu.VMEM(gshape, F32), pltpu.VMEM((4,) + vshape, F32), pltpu.VMEM((4,) + gshape, F32),
                        pltpu.SemaphoreType.DMA((2,)), pltpu.SemaphoreType.DMA((2,)),
                        pltpu.SemaphoreType.DMA((2 * len(CHIP_FLIPS),)), pltpu.SemaphoreType.DMA((2 * len(CHIP_FLIPS),)),
                        pltpu.SemaphoreType.DMA((n_chip,)), pltpu.SemaphoreType.DMA((n_chip,))],
        compiler_params=pltpu.CompilerParams(vmem_limit_bytes=VMEM_LIMIT),
    )(msg_vec, msg_gate, gb_rows, *chip_sums)


def _adamw_math(w, g, m, v):
    m = ADAM_B1 * m + (1.0 - ADAM_B1) * g
    v = ADAM_B2 * v + (1.0 - ADAM_B2) * (g * g)
    m_hat = m / (1.0 - ADAM_B1 ** ADAM_STEP)
    v_hat = v / (1.0 - ADAM_B2 ** ADAM_STEP)
    delta = -ADAM_LR * (m_hat / (jnp.sqrt(v_hat) + ADAM_EPS) + ADAM_WD * w)
    return delta, m, v


def _adamw(name, w, g, m, v):
    rows, cols = w.shape
    rb = 256 if rows % 256 == 0 else rows

    def body(w_ref, g_ref, m_ref, v_ref, d_ref, mo_ref, vo_ref):
        d_ref[...], mo_ref[...], vo_ref[...] = _adamw_math(w_ref[...], g_ref[...], m_ref[...], v_ref[...])

    spec = pl.BlockSpec((rb, cols), lambda r: (r, 0))
    return pl.pallas_call(
        body,
        name="adamw_" + name,
        grid=(rows // rb,),
        in_specs=[spec] * 4,
        out_specs=[spec] * 3,
        out_shape=[jax.ShapeDtypeStruct((rows, cols), F32)] * 3,
        compiler_params=pltpu.CompilerParams(dimension_semantics=("arbitrary",)),
    )(w, g, m, v)


def _update(pos, sum_jobs, plain_jobs, after):
    rb = 256
    jobs = [("sum", j) for j in sum_jobs] + [("plain", j) for j in plain_jobs]
    offs, total = [], 0
    for _, j in jobs:
        offs.append(total)
        total += j[-1].shape[0] // rb
    n_in = sum(len(j) for _, j in jobs)

    def body(pos_ref, *refs):
        ins, outs = refs[:n_in], refs[n_in + 1:]
        s = pl.program_id(0)
        i_in = i_out = 0
        for (kind, j), off in zip(jobs, offs):
            steps = j[-1].shape[0] // rb
            j_in = ins[i_in:i_in + len(j)]
            i_in += len(j)
            j_out = outs[i_out:i_out + (4 if kind == "sum" else 3)]
            i_out += len(j_out)

            @pl.when((s >= off) & (s < off + steps))
            def _(kind=kind, j_in=j_in, j_out=j_out):
                if kind == "sum":
                    mine_ref, recv_ref, chips_ref, w_ref, m_ref, v_ref = j_in
                    g = mine_ref[0, 0] + recv_ref[0]
                    for q in range(len(CHIP_FLIPS)):
                        g = g + chips_ref[q].astype(F32)
                    j_out[0][...] = g
                    rest = j_out[1:]
                else:
                    g_ref, w_ref, m_ref, v_ref = j_in
                    g = g_ref[...]
                    rest = j_out
                rest[0][...], rest[1][...], rest[2][...] = _adamw_math(w_ref[...], g, m_ref[...], v_ref[...])

    in_specs, out_specs, out_shape, args = [], [], [], []
    for (kind, j), off in zip(jobs, offs):
        rows, cols = j[-1].shape
        steps = rows // rb
        blk = lambda s, off=off, steps=steps: jnp.clip(s - off, 0, steps - 1)
        flat = pl.BlockSpec((rb, cols), lambda s, pos, blk=blk: (blk(s), 0))
        if kind == "sum":
            in_specs += [pl.BlockSpec((1, 1, rb, cols), lambda s, pos, blk=blk: (pos[1], 0, blk(s), 0)),
                         pl.BlockSpec((1, rb, cols), lambda s, pos, blk=blk: (pos[1], blk(s), 0)),
                         pl.BlockSpec((len(CHIP_FLIPS), rb, cols), lambda s, pos, blk=blk: (0, blk(s), 0))]
            in_specs += [flat] * 3
        else:
            in_specs += [flat] * 4
        n_res = 4 if kind == "sum" else 3
        out_specs += [flat] * n_res
        out_shape += [jax.ShapeDtypeStruct((rows, cols), F32)] * n_res
        args += list(j)
    in_specs += [pl.BlockSpec(after.shape, lambda s, pos: (0,) * after.ndim)]
    outs = pl.pallas_call(
        body,
        name="update",
        grid_spec=pltpu.PrefetchScalarGridSpec(
            num_scalar_prefetch=1, grid=(total,), in_specs=in_specs, out_specs=out_specs),
        out_shape=out_shape,
        compiler_params=pltpu.CompilerParams(dimension_semantics=("arbitrary",), vmem_limit_bytes=VMEM_LIMIT),
    )(pos, *args, after)
    sums = [tuple(outs[4 * i:4 * i + 4]) for i in range(len(sum_jobs))]
    base = 4 * len(sum_jobs)
    plains = [tuple(outs[base + 3 * i:base + 3 * i + 3]) for i in range(len(plain_jobs))]
    return sums, plains


def _adamw_small(ws, gs, ms, vs, sigmoid_scaled):
    n = len(ws)

    def body(*refs):
        w_refs, g_refs, m_refs, v_refs = (refs[i * n:(i + 1) * n] for i in range(4))
        outs = refs[4 * n:]
        for i in range(n):
            w = w_refs[i][...]
            g = g_refs[i][...]
            if sigmoid_scaled[i]:
                g = g * _sigmoid(w)
            delta, m, v = _adamw_math(w, g, m_refs[i][...], v_refs[i][...])
            outs[4 * i][...] = g
            outs[4 * i + 1][...] = delta
            outs[4 * i + 2][...] = m
            outs[4 * i + 3][...] = v

    shapes = [jax.ShapeDtypeStruct(w.shape, F32) for w in ws for _ in range(4)]
    outs = pl.pallas_call(
        body,
        name="adamw_small",
        in_specs=[WHOLE] * (4 * n),
        out_specs=[WHOLE] * (4 * n),
        out_shape=shapes,
    )(*ws, *gs, *ms, *vs)
    return [outs[4 * i:4 * i + 4] for i in range(n)]


_WEIGHT_NAMES = ("ada_w", "ada_b", "norm1_g", "w_in", "lru_conv_w", "lru_conv_b", "gate_a_w", "gate_a_b", "gate_x_w",
                 "gate_x_b", "a_param", "short_conv_w", "lru_out_g", "conv_out_g", "w_out", "norm2_g", "w_mlp1",
                 "w_mlp2", "final_g")


def kernel(x, c, ada_w, ada_b, norm1_g, w_in, lru_conv_w, lru_conv_b, gate_a_w, gate_a_b, gate_x_w, gate_x_b, a_param, short_conv_w, lru_out_g, conv_out_g, w_out, norm2_g, w_mlp1, w_mlp2, final_g, loss_target, m_ada_w, m_ada_b, m_norm1_g, m_w_in, m_lru_conv_w, m_lru_conv_b, m_gate_a_w, m_gate_a_b, m_gate_x_w, m_gate_x_b, m_a_param, m_short_conv_w, m_lru_out_g, m_conv_out_g, m_w_out, m_norm2_g, m_w_mlp1, m_w_mlp2, m_final_g, v_ada_w, v_ada_b, v_norm1_g, v_w_in, v_lru_conv_w, v_lru_conv_b, v_gate_a_w, v_gate_a_b, v_gate_x_w, v_gate_x_b, v_a_param, v_short_conv_w, v_lru_out_g, v_conv_out_g, v_w_out, v_norm2_g, v_w_mlp1, v_w_mlp2, v_final_g):
    given = dict(locals())
    weights = {n: given[n] for n in _WEIGHT_NAMES}
    xi, yi, ci = _position()
    me_lin = _linear((xi, yi, ci))
    hd = W // N_DEV

    mixer_block = jnp.concatenate([w_out[0], w_in[0].T], axis=0).astype(BF16)
    mlp_block = jnp.concatenate([w_mlp1[0].T, w_mlp2[0]], axis=0).astype(BF16)

    msg = (jnp.pad(c, ((0, HALO - 1), (0, 0)))
           + jnp.pad(lru_conv_w[0], ((1, HALO - 1 - CONV_L), (0, D - hd)))
           + jnp.pad(short_conv_w[0], ((1 + CONV_L, 0), (0, D - hd))))
    gath, mod_all, wmix = _gather_and_mod(msg, ada_w[0], mixer_block)
    sc_all = gath[:, 0, :]
    wl = jnp.transpose(gath[:, 1:1 + CONV_L, :hd], (1, 0, 2)).reshape(CONV_L, W)
    ws = jnp.transpose(gath[:, 1 + CONV_L:HALO, :hd], (1, 0, 2)).reshape(CONV_S, W)
    modraw = _pad_rows(mod_all[:, 0, :].reshape(6, D), HALO)
    adab = _pad_rows(ada_b.reshape(6, D), HALO)

    x2d, tgt = x[0], loss_target[0]
    gf = final_g.reshape(1, D)
    bda = _block_diag(gate_a_w[0]).astype(BF16)
    bdx = _block_diag(gate_x_w[0]).astype(BF16)
    avg = _block_diag(jnp.full((8, W // 8, W // 8), 8.0 / W, F32)).astype(BF16)
    wl8 = _pad_rows(wl, HALO)
    ws8 = _pad_rows(ws, HALO)
    mixer_small = (wl8, lru_conv_b, bda, bdx, gate_a_b, gate_x_b, a_param, ws8, lru_out_g, conv_out_g, avg)
    proj, hl, mixed, kept, wmlp = _mixer_fwd(x2d, modraw, adab, norm1_g, *mixer_small, wmix, mlp_block)
    wmlp = _sibling_forward(wmlp)
    h2t, f, dx2, dz, vec2, loss8 = _mlp_fwd(x2d, mixed, tgt, modraw, adab, norm2_g, gf, wmlp)
    pos = jnp.stack([ci, 2 * xi + yi]).astype(jnp.int32)
    by_dest = lambda g: g.reshape((4, 2, -1) + g.shape[-1:])
    dh2_first, *for_sibling = _mlp_bwd_half(pos, h2t, f, dz, wmlp)
    sib_send, sib_recv, sib_thru, sib_land, token = _chips_start("mlp", for_sibling, stage="sibling")
    dh2, dw1, dw2 = _mlp_bwd_half(pos, h2t, f, dz, wmlp, prior=(dh2_first, token))
    done = dh2[0:HALO, 0:128] + dw1[0, 0:HALO, 0:128] + dw2[0, 0:HALO, 0:128]
    _, mlp_sib = _chips_wait("mlp", sib_send, sib_recv, sib_thru, sib_land, done, stage="sibling")
    mlp_parts = [dw1[:, None], dw2[:, None]]
    mlp_sums = [_pair_sum(pos, p, r) for p, r in zip(mlp_parts, mlp_sib)]
    mlp_send, mlp_recv, mlp_thru, mlp_land, token = _chips_start("mlp", mlp_sums)
    modraw_after = modraw + jnp.tile(token, (1, D // token.shape[1]))
    gx, vec, hb, dproj_t, dmixed, ycat_t, xl_t, dgate = _mixer_bwd(
        x2d, mixed, dh2, dx2, proj, hl, kept, modraw_after, adab, norm1_g, norm2_g, *mixer_small, wmix, [], None)
    dwint = _matmul("wgrad_in", dproj_t, hb)
    dwout = _matmul("wgrad_out", ycat_t, dmixed)
    gate_blocks = _gate_wgrad(xl_t, dgate, avg)
    msg_gate = gate_blocks.reshape(W, 128)
    done = dwint[0:HALO, 0:128] + dwout[0:HALO, 0:128] + gate_blocks[0, 0:HALO, :].sum() + gx[0:HALO, 0:128]
    _, mlp_chips = _chips_wait("mlp", mlp_send, mlp_recv, mlp_thru, mlp_land, done)
    mix_parts = [by_dest(dwout), by_dest(dwint)]
    gmod8 = (jnp.pad(vec[0:5], ((0, 1), (0, 0))) + jnp.pad(vec2[0:1], ((5, 0), (0, 0)))).reshape(N_DEV, 6 * D // N_DEV)
    sc_t = jnp.pad(sc_all.T, ((0, 0), (0, 128 - N_DEV))).astype(BF16)
    sib_send, sib_recv, sib_thru, sib_land, token = _chips_start("mixer", mix_parts, stage="sibling")
    g_adaw, gb_rows = _route_mod_grads(gmod8, sc_t, token)
    mix_parts, mix_sib = _chips_wait("mixer", sib_send, sib_recv, sib_thru, sib_land, gb_rows, stage="sibling")
    mix_sums = [_pair_sum(pos, p, r) for p, r in zip(mix_parts, mix_sib)]
    loss_rows = jnp.pad(loss8[0:1], ((0, HALO - 1), (0, D - loss8.shape[1])))
    msg_vec = jnp.concatenate([vec, vec2, loss_rows], axis=0)
    state = lambda n: (weights[n][0], given["m_" + n][0], given["v_" + n][0])
    mlp_jobs = [(p, r, q, *state(n)) for p, r, q, n in zip(mlp_parts, mlp_sib, mlp_chips, ("w_mlp1", "w_mlp2"))]
    mix_send, mix_recv, mix_thru, mix_land, token = _chips_start("mixer", mix_sums)
    mlp_done, (adaw_done,) = _update(pos, mlp_jobs, [(g_adaw, *state("ada_w"))], token)
    sum_vec, sum_gate = _chips_and_gather(msg_vec, msg_gate, gb_rows, [])
    loss = sum_vec[LOSS_ROW, 0]
    sum_gate = sum_gate.reshape(2, W, W // 8)
    lo, hi = slice(0, W), slice(W, 2 * W)
    wl_full = sum_vec[V_WL01:V_WL23 + 1].reshape(CONV_L, W)
    ws_full = sum_vec[V_WS01:V_WS2 + 1].reshape(CONV_S + 1, W)[:CONV_S]
    row = lambda r, cols: sum_vec[r:r + 1, cols]
    small_grads = {
        "ada_b": sum_vec[GB_BASE:GB_BASE + N_DEV, :6 * D // N_DEV].reshape(1, 6 * D),
        "norm1_g": row(V_G1, slice(0, D)),
        "lru_conv_w": lax.dynamic_slice(wl_full, (0, me_lin * hd), (CONV_L, hd)),
        "lru_conv_b": row(V_BL_BA, lo),
        "gate_a_w": sum_gate[0],
        "gate_a_b": row(V_BL_BA, hi),
        "gate_x_w": sum_gate[1],
        "gate_x_b": row(V_BX_SP, lo),
        "a_param": row(V_BX_SP, hi),
        "short_conv_w": lax.dynamic_slice(ws_full, (0, me_lin * hd), (CONV_S, hd)),
        "lru_out_g": row(V_GL_GC, lo),
        "conv_out_g": row(V_GL_GC, hi),
        "norm2_g": row(V_G2, slice(0, D)),
        "final_g": sum_vec[V_ROWS + 1:V_ROWS + 2, :],
    }
    names = list(small_grads)
    as2d = lambda a, n: a.reshape(small_grads[n].shape)
    small = _adamw_small([as2d(weights[n], n) for n in names], [small_grads[n] for n in names],
                         [as2d(given["m_" + n], n) for n in names], [as2d(given["v_" + n], n) for n in names],
                         [n == "a_param" for n in names])
    result = {n: tuple(o.reshape(weights[n].shape) for o in outs) for n, outs in zip(names, small)}

    done = (small[0][1][:, 0:128] + mlp_done[0][2][0:HALO, 0:128] + mlp_done[1][2][0:HALO, 0:128]
            + adaw_done[1][0:HALO, 0:128])
    _, mix_chips = _chips_wait("mixer", mix_send, mix_recv, mix_thru, mix_land, done)
    g_wout, g_wint = (_final_sum(pos, p, r, q) for p, r, q in zip(mix_parts, mix_sib, mix_chips))
    for n, g in (("w_in", g_wint.T), ("w_out", g_wout)):
        w, m, v = state(n)
        result[n] = (g[None],) + tuple(o[None] for o in _adamw(n, w, g, m, v))
    result["w_mlp1"], result["w_mlp2"] = (tuple(o[None] for o in done) for done in mlp_done)
    result["ada_w"] = (g_adaw[None],) + tuple(o[None] for o in adaw_done)

    return (loss, gx[None], *[result[n][0] for n in _WEIGHT_NAMES], *[result[n][1] for n in _WEIGHT_NAMES],
            *[result[n][2] for n in _WEIGHT_NAMES], *[result[n][3] for n in _WEIGHT_NAMES])
```

```python
import functools

import jax
import jax.numpy as jnp
from jax import lax
from jax.experimental import pallas as pl
from jax.experimental.pallas import tpu as pltpu

F32 = jnp.float32
BF16 = jnp.bfloat16
MESH = pl.DeviceIdType.MESH

N_DEV = 8
D = 1024
W = 512
D_IN = 5 * W
D_FF = 4096
FF_BLK = D_FF // N_DEV
EPS = 1e-6
C_GATE = 8.0
CONV_L = 4
CONV_S = 3
HALO = 8

ROWS_W1T, ROWS_W2, ROWS_WOUT, ROWS_WIN = FF_BLK, FF_BLK, D // N_DEV, D_IN // N_DEV
OFF_WOUT = 0
OFF_WIN = OFF_WOUT + ROWS_WOUT
MIX_ROWS = OFF_WIN + ROWS_WIN
OFF_W1T = 0
OFF_W2 = OFF_W1T + ROWS_W1T
MLP_ROWS = OFF_W2 + ROWS_W2
CHIP_FLIPS = (4, 2, 6)
N_KEPT = 6

ADAM_LR = 0.001
ADAM_B1 = 0.9
ADAM_B2 = 0.999
ADAM_EPS = 1e-08
ADAM_WD = 0.01
ADAM_STEP = 10

VMEM_LIMIT = 56 * 1024 * 1024

TB_MIX = 256
TB_MIXB = 256
TB_MLP = 256
TB_MLPB = 512

ANY = pl.BlockSpec(memory_space=pl.ANY)
WHOLE = pl.BlockSpec(memory_space=pltpu.VMEM)


def _dot(a, b):
    return jnp.dot(a, b, preferred_element_type=F32)


def _dot_nt(a, b):
    return lax.dot_general(a, b, (((1,), (1,)), ((), ())), preferred_element_type=F32)


def _dot_tn(a, b):
    return lax.dot_general(a, b, (((0,), (0,)), ((), ())), preferred_element_type=F32)


def _sigmoid(v):
    return 1.0 / (1.0 + jnp.exp(-v))


def _softplus(v):
    t = jnp.exp(-jnp.abs(v))
    small = t * (1.0 - t * (0.5 - t * (1.0 / 3.0)))
    return jnp.maximum(v, 0.0) + jnp.where(t < 1e-2, small, jnp.log(1.0 + t))


def _one_minus_sq(a, log_a):
    return -jnp.tanh(log_a) * (a * a + 1.0)


_GELU_K = 0.7978845608028654
_GELU_C = 0.044715


def _gelu(u):
    th = jnp.tanh(_GELU_K * (u + _GELU_C * u * u * u))
    return 0.5 * u * (1.0 + th), th


def _gelu_grad(u, th):
    return 0.5 * (1.0 + th) + 0.5 * u * (1.0 - th * th) * _GELU_K * (1.0 + 3.0 * _GELU_C * u * u)


def _group_mean(v, avg):
    hi = v.astype(BF16)
    lo = (v - hi.astype(F32)).astype(BF16)
    return _dot(hi, avg) + _dot(lo, avg)


def _colsum(v):
    return jnp.sum(v, axis=0, keepdims=True)


def _rowmean(v):
    return jnp.mean(v, axis=-1, keepdims=True)


def _load_packed(wpack_hbm, off, rows, dst, sem):
    copies = [
        pltpu.make_async_copy(wpack_hbm.at[d, pl.ds(off, rows), :], dst.at[pl.ds(d * rows, rows), :], sem)
        for d in range(N_DEV)
    ]
    for cp in copies:
        cp.start()
    return copies


def _scan_groups(n_groups, a_ref, b_ref, out_ref, carry_ref, reverse):
    row = lax.broadcasted_iota(jnp.int32, (HALO, W), 0)

    def step(k, carry):
        g = (n_groups - 1 - k) if reverse else k
        rows = pl.ds(pl.multiple_of(g * HALO, HALO), HALO)
        a = a_ref[rows, :]
        b = b_ref[rows, :]
        for s in (1, 2, 4):
            if reverse:
                keep = row < HALO - s
                sh = HALO - s
            else:
                keep = row >= s
                sh = s
            a_sh = pltpu.roll(a, sh, axis=0)
            b_sh = pltpu.roll(b, sh, axis=0)
            b = jnp.where(keep, a * b_sh + b, b)
            a = jnp.where(keep, a * a_sh, a)
        h = b + a * carry
        out_ref[rows, :] = h
        edge = h[0:1, :] if reverse else h[HALO - 1:HALO, :]
        return jnp.broadcast_to(edge, (HALO, W))

    carry_ref[...] = lax.fori_loop(0, n_groups, step, carry_ref[...])


def _route_peers(me):
    x, y, c = me
    first = ((x + 1 - c) % 2, (y + c) % 2, c)
    second = ((x + c) % 2, (y + 1 - c) % 2, c)
    return first, second, (1 - x, 1 - y, c)


def _chip_gather_copies(block_hbm, out_hbm, send_sems, recv_sems):
    me = _position()
    first, second, diag = _route_peers(me)

    def copy(j, src, slot_of, to):
        return pltpu.make_async_remote_copy(
            src_ref=src, dst_ref=out_hbm.at[_linear(slot_of)], send_sem=send_sems.at[j], recv_sem=recv_sems.at[j],
            device_id=to, device_id_type=MESH)

    own_sends = [copy(0, block_hbm, me, first), copy(1, block_hbm, me, second)]
    forward = copy(2, out_hbm.at[_linear(first)], first, second)
    arrivals = [copy(0, block_hbm, first, first), copy(1, block_hbm, second, second), copy(2, block_hbm, diag, second)]
    return own_sends, forward, arrivals


def _mixer_fwd(x, modraw, adab, g1, wl, bl, bda, bdx, ba, bxb, ap, ws, gl, gc, avg, wpack, mlp_block):
    t_len = x.shape[0]
    tb = TB_MIX
    nb = t_len // tb

    def body(x_ref, modraw_ref, adab_ref, g1_ref, wl_ref, bl_ref, bda_ref, bdx_ref, ba_ref, bxb_ref, ap_ref,
             ws_ref, gl_ref, gc_ref, avg_ref, wpack_hbm, block_hbm, proj_ref, hl_ref, mixed_ref, kept_ref, wmlp_hbm,
             win_v, wout_v, sem, ulx_ext, cv_ext, hcar, a_s, b_s, send_sems, recv_sems, local_sem):
        i = pl.program_id(0)
        own = pltpu.make_async_copy(block_hbm, wmlp_hbm.at[_linear(_position())], local_sem)
        sends, forward, arrivals = _chip_gather_copies(block_hbm, wmlp_hbm, send_sems, recv_sems)

        @pl.when(i == 0)
        def _():
            own.start()
            for cp in sends:
                cp.start()

        @pl.when(i == nb - 1)
        def _():
            arrivals[0].wait_recv()
            forward.start()

        @pl.when(i == 0)
        def _():
            cps = _load_packed(wpack_hbm, OFF_WIN, ROWS_WIN, win_v, sem.at[0])
            cps += _load_packed(wpack_hbm, OFF_WOUT, ROWS_WOUT, wout_v, sem.at[1])
            ulx_ext[0:HALO, :] = jnp.zeros((HALO, W), F32)
            cv_ext[0:HALO, :] = jnp.zeros((HALO, W), F32)
            hcar[...] = jnp.zeros((HALO, W), F32)
            for cp in cps:
                cp.wait()

        mod = modraw_ref[...] + adab_ref[...]
        shift1, scale1, gate1 = mod[0:1], mod[1:2], mod[2:3]
        x = x_ref[...]
        r1 = lax.rsqrt(_rowmean(x * x) + EPS)
        h = (x * r1 * g1_ref[...]) * (1.0 + scale1) + shift1
        proj = _dot_nt(h.astype(BF16), win_v[...])
        proj_ref[...] = proj
        u_lx, u_ly, u_b, u_c, u_v = (proj[:, k * W:(k + 1) * W] for k in range(5))

        ulx_ext[HALO:HALO + tb, :] = u_lx
        xl = bl_ref[...] + wl_ref[CONV_L - 1:CONV_L, :] * u_lx
        for k in range(CONV_L - 1):
            xl = xl + wl_ref[k:k + 1, :] * ulx_ext[pl.ds(HALO - (CONV_L - 1) + k, tb), :]
        ulx_ext[0:HALO, :] = ulx_ext[tb:tb + HALO, :]
        xlb = xl.astype(BF16)
        r = _sigmoid(_dot(xlb, bda_ref[...]) + ba_ref[...])
        ig = _sigmoid(_dot(xlb, bdx_ref[...]) + bxb_ref[...])
        log_a = (-C_GATE) * r * _softplus(ap_ref[...])
        a = jnp.exp(log_a)
        mult = jnp.sqrt(_one_minus_sq(a, log_a))
        grow = i * tb + lax.broadcasted_iota(jnp.int32, (tb, W), 0)
        mult = jnp.where(grow == 0, 1.0, mult)
        a_s[...] = a
        b_s[...] = mult * (ig * xl)
        _scan_groups(tb // HALO, a_s, b_s, hl_ref, hcar, reverse=False)
        hl = hl_ref[...]
        ge, _ = _gelu(u_ly)
        p = ge * hl
        rp = lax.rsqrt(_group_mean(p * p, avg_ref[...]) + EPS)
        y_lru = p * rp * gl_ref[...]

        cv = u_c * u_v
        cv_ext[HALO:HALO + tb, :] = cv
        cc = ws_ref[CONV_S - 1:CONV_S, :] * cv
        for k in range(CONV_S - 1):
            cc = cc + ws_ref[k:k + 1, :] * cv_ext[pl.ds(HALO - (CONV_S - 1) + k, tb), :]
        cv_ext[0:HALO, :] = cv_ext[tb:tb + HALO, :]
        q = u_b * cc
        rq = lax.rsqrt(_group_mean(q * q, avg_ref[...]) + EPS)
        y_conv = q * rq * gc_ref[...]
        for k, kept in enumerate((xl, r, ig, rp, rq, cc)):
            kept_ref[:, k * W:(k + 1) * W] = kept

        mixed_ref[...] = (_dot(y_lru.astype(BF16), wout_v[0:W, :]) + _dot(y_conv.astype(BF16), wout_v[W:2 * W, :]))

        @pl.when(i == nb - 1)
        def _():
            for cp in arrivals[1:]:
                cp.wait_recv()
            for cp in sends + [forward]:
                cp.wait_send()
            own.wait()

    tok = lambda cols: pl.BlockSpec((tb, cols), lambda i: (i, 0))
    full = lambda a: pl.BlockSpec(a.shape, lambda i: (0,) * a.ndim)
    small = (modraw, adab, g1, wl, bl, bda, bdx, ba, bxb, ap, ws, gl, gc, avg)
    n_chips = len(CHIP_FLIPS)
    return pl.pallas_call(
        body,
        name="mixer_fwd",
        grid=(nb,),
        in_specs=[tok(D)] + [full(a) for a in small] + [ANY, ANY],
        out_specs=[tok(D_IN), tok(W), tok(D), tok(N_KEPT * W), ANY],
        out_shape=[jax.ShapeDtypeStruct((t_len, D_IN), F32), jax.ShapeDtypeStruct((t_len, W), F32),
                   jax.ShapeDtypeStruct((t_len, D), F32), jax.ShapeDtypeStruct((t_len, N_KEPT * W), F32),
                   jax.ShapeDtypeStruct((N_DEV,) + mlp_block.shape, BF16)],
        scratch_shapes=[pltpu.VMEM((D_IN, D), BF16), pltpu.VMEM((D, D), BF16), pltpu.SemaphoreType.DMA((2,)),
                        pltpu.VMEM((tb + HALO, W), F32), pltpu.VMEM((tb + HALO, W), F32), pltpu.VMEM((HALO, W), F32),
                        pltpu.VMEM((tb, W), F32), pltpu.VMEM((tb, W), F32),
                        pltpu.SemaphoreType.DMA((n_chips,)), pltpu.SemaphoreType.DMA((n_chips,)), pltpu.SemaphoreType.DMA],
        compiler_params=pltpu.CompilerParams(dimension_semantics=("arbitrary",), vmem_limit_bytes=VMEM_LIMIT),
    )(x, *small, wpack, mlp_block)


def _sibling_forward(wmlp):
    def body(in_hbm, out_hbm, send_sems, recv_sems):
        x, y, c = _position()
        copies, arrivals = [], []
        for j, k in enumerate((0,) + CHIP_FLIPS):
            mine = out_hbm.at[_linear(_flip((x, y, c), k))]
            theirs = out_hbm.at[_linear(_flip((x, y, 1 - c), k))]
            copies.append(pltpu.make_async_remote_copy(
                src_ref=mine, dst_ref=mine, send_sem=send_sems.at[j], recv_sem=recv_sems.at[j],
                device_id=(x, y, 1 - c), device_id_type=MESH))
            arrivals.append(pltpu.make_async_remote_copy(
                src_ref=theirs, dst_ref=theirs, send_sem=send_sems.at[j], recv_sem=recv_sems.at[j],
                device_id=(x, y, 1 - c), device_id_type=MESH))
        for cp in copies:
            cp.start()
        for cp in arrivals:
            cp.wait_recv()
        for cp in copies:
            cp.wait_send()

    return pl.pallas_call(
        body,
        name="sibling_forward",
        in_specs=[ANY],
        out_specs=ANY,
        out_shape=jax.ShapeDtypeStruct(wmlp.shape, wmlp.dtype),
        input_output_aliases={0: 0},
        scratch_shapes=[pltpu.SemaphoreType.DMA((4,)), pltpu.SemaphoreType.DMA((4,))],
    )(wmlp)


def _mlp_fwd(x, mixed, tgt, modraw, adab, g2, gf, wpack):
    t_len = x.shape[0]
    tb = TB_MLP
    nb = t_len // tb

    def body(x_ref, mixed_ref, tgt_ref, modraw_ref, adab_ref, g2_ref, gf_ref, wpack_hbm,
             h2t_ref, f_ref, dx2_ref, dz_ref, vec_ref, loss_ref, w1t_v, w2_v, sem):
        i = pl.program_id(0)

        @pl.when(i == 0)
        def _():
            cps = _load_packed(wpack_hbm, OFF_W1T, ROWS_W1T, w1t_v, sem.at[0])
            cps += _load_packed(wpack_hbm, OFF_W2, ROWS_W2, w2_v, sem.at[1])
            vec_ref[...] = jnp.zeros(vec_ref.shape, F32)
            loss_ref[...] = jnp.zeros(loss_ref.shape, F32)
            for cp in cps:
                cp.wait()

        mod = modraw_ref[...] + adab_ref[...]
        gate1, shift2, scale2, gate2 = mod[2:3], mod[3:4], mod[4:5], mod[5:6]
        x1 = x_ref[...] + gate1 * mixed_ref[...]
        r2 = lax.rsqrt(_rowmean(x1 * x1) + EPS)
        h2 = (x1 * r2 * g2_ref[...]) * (1.0 + scale2) + shift2
        h2b = h2.astype(BF16)
        h2t_ref[...] = h2.T.astype(BF16)
        z = jnp.zeros((tb, D), F32)
        for j in range(N_DEV):
            cols = slice(j * FF_BLK, (j + 1) * FF_BLK)
            fj = _dot_nt(h2b, w1t_v[cols, :])
            f_ref[:, cols] = fj
            rf = jnp.maximum(fj, 0.0)
            z = z + _dot((rf * rf).astype(BF16), w2_v[cols, :])
        x2 = x1 + gate2 * z
        r3 = lax.rsqrt(_rowmean(x2 * x2) + EPS)
        xn3 = x2 * r3
        diff = xn3 * gf_ref[...] - tgt_ref[...]
        sq = _colsum(diff * diff)
        loss_ref[...] += jnp.broadcast_to(jnp.sum(sq, axis=1, keepdims=True) * (0.5 / D), loss_ref.shape)
        dy = diff * (1.0 / D)
        dyn = dy * gf_ref[...]
        dx2 = r3 * (dyn - xn3 * _rowmean(dyn * xn3))
        dx2_ref[...] = dx2
        dz_ref[...] = (gate2 * dx2).astype(BF16)
        vec_ref[0:1, :] += _colsum(dx2 * z)
        vec_ref[1:2, :] += _colsum(dy * xn3)

    tok = lambda cols: pl.BlockSpec((tb, cols), lambda i: (i, 0))
    tok_t = pl.BlockSpec((D, tb), lambda i: (0, i))
    full = lambda a: pl.BlockSpec(a.shape, lambda i: (0,) * a.ndim)
    small = (modraw, adab, g2, gf)
    return pl.pallas_call(
        body,
        name="mlp_fwd",
        grid=(nb,),
        in_specs=[tok(D), tok(D), tok(D)] + [full(a) for a in small] + [ANY],
        out_specs=[tok_t, tok(D_FF), tok(D), tok(D), pl.BlockSpec((8, D), lambda i: (0, 0)),
                   pl.BlockSpec((8, 128), lambda i: (0, 0))],
        out_shape=[jax.ShapeDtypeStruct((D, t_len), BF16), jax.ShapeDtypeStruct((t_len, D_FF), F32),
                   jax.ShapeDtypeStruct((t_len, D), F32), jax.ShapeDtypeStruct((t_len, D), BF16),
                   jax.ShapeDtypeStruct((8, D), F32), jax.ShapeDtypeStruct((8, 128), F32)],
        scratch_shapes=[pltpu.VMEM((D_FF, D), BF16), pltpu.VMEM((D_FF, D), BF16), pltpu.SemaphoreType.DMA((2,))],
        compiler_params=pltpu.CompilerParams(dimension_semantics=("arbitrary",), vmem_limit_bytes=VMEM_LIMIT),
    )(x, mixed, tgt, *small, wpack)


def _mlp_bwd_half(pos, h2t, f, dz, wpack, prior=None):
    t_len = dz.shape[0]
    tb = TB_MLPB
    nb = t_len // tb
    first = prior is None
    flip = 1 if first else 0

    def body(pos_ref, h2t_ref, f_ref, dz_ref, w1t_ref, w2_ref, *rest):
        if first:
            dh2_ref, dw1_ref, dw2_ref = rest
        else:
            dh2in_ref, _, dh2_ref, dw1_ref, dw2_ref = rest
        k = pl.program_id(0)
        t = pl.program_id(1)
        rows = pl.ds(pl.multiple_of(t * tb, tb), tb)
        w1t = w1t_ref[0]
        w2 = w2_ref[0]
        dz = dz_ref[...]
        rf = jnp.maximum(f_ref[...], 0.0)
        df = (_dot_nt(dz, w2) * (2.0 * rf)).astype(BF16)
        dh = _dot(df, w1t)
        g1 = _dot(h2t_ref[...], df)
        g2 = _dot_tn((rf * rf).astype(BF16), dz)

        @pl.when(t == 0)
        def _():
            dw2_ref[0] = g2
            dw1_ref[0] = g1

        @pl.when(t != 0)
        def _():
            dw2_ref[0] += g2
            dw1_ref[0] += g1

        @pl.when(k == 0)
        def _():
            dh2_ref[rows, :] = dh if first else dh2in_ref[...] + dh

        @pl.when(k != 0)
        def _():
            dh2_ref[rows, :] += dh

    blk = lambda k, pos: 2 * k + jnp.bitwise_xor(pos[0], flip)
    in_specs = [pl.BlockSpec((D, tb), lambda k, t, pos: (0, t)),
                pl.BlockSpec((tb, FF_BLK), lambda k, t, pos: (t, blk(k, pos))),
                pl.BlockSpec((tb, D), lambda k, t, pos: (t, 0)),
                pl.BlockSpec((1, ROWS_W1T, D), lambda k, t, pos: (blk(k, pos), OFF_W1T // ROWS_W1T, 0)),
                pl.BlockSpec((1, ROWS_W2, D), lambda k, t, pos: (blk(k, pos), OFF_W2 // ROWS_W2, 0))]
    grad_specs = [pl.BlockSpec((1, D, FF_BLK), lambda k, t, pos: (k, 0, 0)),
                  pl.BlockSpec((1, FF_BLK, D), lambda k, t, pos: (k, 0, 0))]
    out_specs = [pl.BlockSpec((t_len, D), lambda k, t, pos: (0, 0))] + grad_specs
    grad_shapes = [jax.ShapeDtypeStruct((4, D, FF_BLK), F32), jax.ShapeDtypeStruct((4, FF_BLK, D), F32)]
    out_shape = [jax.ShapeDtypeStruct((t_len, D), F32)] + grad_shapes
    args = [pos, h2t, f, dz, wpack, wpack]
    if not first:
        in_specs += [pl.BlockSpec((tb, D), lambda k, t, pos: (jnp.where(k == 0, t, nb - 1), 0)),
                     pl.BlockSpec(prior[1].shape, lambda k, t, pos: (0,) * prior[1].ndim)]
        args += list(prior)
    return pl.pallas_call(
        body,
        name="mlp_bwd_first" if first else "mlp_bwd_second",
        grid_spec=pltpu.PrefetchScalarGridSpec(num_scalar_prefetch=1, grid=(4, nb), in_specs=in_specs,
                                               out_specs=out_specs),
        out_shape=out_shape,
        compiler_params=pltpu.CompilerParams(dimension_semantics=("arbitrary", "arbitrary"),
                                             vmem_limit_bytes=VMEM_LIMIT),
    )(*args)


V_SHIFT1, V_SCALE1, V_GATE1, V_SHIFT2, V_SCALE2, V_G1, V_G2 = 0, 1, 2, 3, 4, 6, 7
V_BL_BA, V_BX_SP, V_GL_GC, V_WL01, V_WL23, V_WS01, V_WS2 = 8, 9, 10, 11, 12, 13, 14
V_ROWS = 16


def _chip_scatter_copies(srcs, dsts, send_sems, recv_sems, row_ranges=None):
    me = _position()
    copies = []
    for a, (src, dst) in enumerate(zip(srcs, dsts)):
        band = pl.ds(*row_ranges[a]) if row_ranges else slice(None)
        for j, k in enumerate(CHIP_FLIPS):
            peer = _flip(me, k)
            copies.append(pltpu.make_async_remote_copy(
                src_ref=src.at[2 * peer[0] + peer[1], band], dst_ref=dst.at[j, band],
                send_sem=send_sems.at[len(CHIP_FLIPS) * a + j], recv_sem=recv_sems.at[len(CHIP_FLIPS) * a + j],
                device_id=peer, device_id_type=MESH))
    return copies


def _mixer_bwd(x, mixed, dh2, dx2, proj, hl, kept, modraw, adab, g1, g2, wl, bl, bda, bdx, ba, bxb, ap, ws, gl, gc, avg, wpack,
               chip_sums, chip_rows):
    t_len = x.shape[0]
    tb = TB_MIXB
    nb = t_len // tb
    hb = tb // HALO
    n_sums = len(chip_sums)

    def body(x_ref, mixed_ref, dh2_ref, dx2_ref, proj_ref, projh_ref, hl_ref, hlh_ref, kept_ref,
             modraw_ref, adab_ref, g1_ref, g2_ref, wl_ref, bl_ref, bda_ref, bdx_ref, ba_ref, bxb_ref, ap_ref,
             ws_ref, gl_ref, gc_ref, avg_ref, wpack_hbm, *rest):
        sums_hbm, rest = rest[:n_sums], rest[n_sums:]
        gx_ref, vec_ref, hb_ref, dprojt_ref, dmixed_ref, ycatt_ref, xlt_ref, dgate_ref = rest[:8]
        landed_hbm, rest = rest[8:8 + n_sums], rest[8 + n_sums:]
        (win_v, wout_v, sem, ulx_ext, cv_ext, hl_ext, a_ext, dxl_ext, dcc_ext, dcar, an_s, g_s, dh_s,
         send_sems, recv_sems) = rest
        i = pl.program_id(0)
        blk = nb - 1 - i
        chip_copies = _chip_scatter_copies(sums_hbm, landed_hbm, send_sems, recv_sems, chip_rows)

        @pl.when(i == 0)
        def _():
            for cp in chip_copies:
                cp.start()
            cps = _load_packed(wpack_hbm, OFF_WIN, ROWS_WIN, win_v, sem.at[0])
            cps += _load_packed(wpack_hbm, OFF_WOUT, ROWS_WOUT, wout_v, sem.at[1])
            vec_ref[...] = jnp.zeros(vec_ref.shape, F32)
            zero = jnp.zeros((HALO, W), F32)
            a_ext[tb:tb + HALO, :] = zero
            dxl_ext[tb:tb + HALO, :] = zero
            dcc_ext[tb:tb + HALO, :] = zero
            dcar[...] = zero
            for cp in cps:
                cp.wait()

        mod = modraw_ref[...] + adab_ref[...]
        shift1, scale1, gate1, scale2 = mod[0:1], mod[1:2], mod[2:3], mod[4:5]
        x = x_ref[...]
        mixed = mixed_ref[...]

        x1 = x + gate1 * mixed
        r2 = lax.rsqrt(_rowmean(x1 * x1) + EPS)
        xn2 = x1 * r2
        dh2 = dh2_ref[...]
        vec_ref[V_SHIFT2:V_SHIFT2 + 1, :] += _colsum(dh2)
        vec_ref[V_SCALE2:V_SCALE2 + 1, :] += _colsum(dh2 * xn2 * g2_ref[...])
        vec_ref[V_G2:V_G2 + 1, :] += _colsum(dh2 * (1.0 + scale2) * xn2)
        dxn2 = dh2 * g2_ref[...] * (1.0 + scale2)
        dx1 = dx2_ref[...] + r2 * (dxn2 - xn2 * _rowmean(dxn2 * xn2))
        vec_ref[V_GATE1:V_GATE1 + 1, :] += _colsum(dx1 * mixed)
        dmixed = (gate1 * dx1).astype(BF16)

        proj = proj_ref[...]
        u_lx, u_ly, u_b, u_c, u_v = (proj[:, k * W:(k + 1) * W] for k in range(5))
        has_prev = (blk > 0).astype(F32)
        projh = projh_ref[...]
        ulx_ext[0:HALO, :] = projh[:, 0:W] * has_prev
        ulx_ext[HALO:HALO + tb, :] = u_lx
        xl, r, ig, rp, rq, cc = (kept_ref[:, k * W:(k + 1) * W] for k in range(N_KEPT))
        sp = _softplus(ap_ref[...])
        log_a = (-C_GATE) * r * sp
        a = jnp.exp(log_a)
        mult_raw = jnp.sqrt(_one_minus_sq(a, log_a))
        first = (blk * tb + lax.broadcasted_iota(jnp.int32, (tb, W), 0)) == 0
        mult = jnp.where(first, 1.0, mult_raw)
        hl = hl_ref[...]
        ge, th = _gelu(u_ly)
        pn = ge * hl * rp
        cv = u_c * u_v
        cv_ext[0:HALO, :] = projh[:, 3 * W:4 * W] * projh[:, 4 * W:5 * W] * has_prev
        cv_ext[HALO:HALO + tb, :] = cv
        qn = u_b * cc * rq

        dmixed_ref[...] = dmixed
        ycatt_ref[0:W, :] = (pn * gl_ref[...]).T.astype(BF16)
        ycatt_ref[W:2 * W, :] = (qn * gc_ref[...]).T.astype(BF16)
        dyl = _dot_nt(dmixed, wout_v[0:W, :])
        dyc = _dot_nt(dmixed, wout_v[W:2 * W, :])

        dqn = dyc * gc_ref[...]
        dq = rq * (dqn - qn * _group_mean(dqn * qn, avg_ref[...]))
        du_b = dq * cc
        dcc = dq * u_b
        dcc_ext[0:tb, :] = dcc
        dcv = ws_ref[CONV_S - 1:CONV_S, :] * dcc
        for k in range(CONV_S - 1):
            dcv = dcv + ws_ref[k:k + 1, :] * dcc_ext[pl.ds(CONV_S - 1 - k, tb), :]
        dcc_ext[tb:tb + HALO, :] = dcc_ext[0:HALO, :]
        du_c = dcv * u_v
        du_v = dcv * u_c
        dws = [_colsum(dcc * cv_ext[pl.ds(HALO - (CONV_S - 1) + k, tb), :]) for k in range(CONV_S)]

        dpn = dyl * gl_ref[...]
        dp = rp * (dpn - pn * _group_mean(dpn * pn, avg_ref[...]))
        du_ly = dp * hl * _gelu_grad(u_ly, th)
        g_s[...] = dp * ge
        a_ext[0:tb, :] = a
        an_s[...] = a_ext[pl.ds(1, tb), :]
        _scan_groups(hb, an_s, g_s, dh_s, dcar, reverse=True)
        a_ext[tb:tb + HALO, :] = a_ext[0:HALO, :]
        dh = dh_s[...]
        hl_ext[0:HALO, :] = hlh_ref[...] * has_prev
        hl_ext[HALO:HALO + tb, :] = hl
        da = dh * hl_ext[pl.ds(HALO - 1, tb), :]
        dmult = dh * (ig * xl)
        dig = dh * (mult * xl)
        dxl = dh * (mult * ig)
        dlog = da * a - jnp.where(first, 0.0, dmult * (a * a) / mult_raw)
        dr = dlog * ((-C_GATE) * sp)
        dsp = _colsum(dlog * ((-C_GATE) * r))
        dga = dr * r * (1.0 - r)
        dgx = dig * ig * (1.0 - ig)
        dgab = dga.astype(BF16)
        dgxb = dgx.astype(BF16)
        xlt_ref[...] = xl.T.astype(BF16)
        dgate_ref[:, 0:W] = dgab
        dgate_ref[:, W:2 * W] = dgxb
        dxl = dxl + _dot_nt(dgab, bda_ref[...]) + _dot_nt(dgxb, bdx_ref[...])
        dxl_ext[0:tb, :] = dxl
        du_lx = wl_ref[CONV_L - 1:CONV_L, :] * dxl
        for k in range(CONV_L - 1):
            du_lx = du_lx + wl_ref[k:k + 1, :] * dxl_ext[pl.ds(CONV_L - 1 - k, tb), :]
        dxl_ext[tb:tb + HALO, :] = dxl_ext[0:HALO, :]
        dwl = [_colsum(dxl * ulx_ext[pl.ds(HALO - (CONV_L - 1) + k, tb), :]) for k in range(CONV_L)]

        cat = lambda u, v: jnp.concatenate([u, v], axis=1)
        vec_ref[V_BL_BA:V_BL_BA + 1, :] += cat(_colsum(dxl), _colsum(dga))
        vec_ref[V_BX_SP:V_BX_SP + 1, :] += cat(_colsum(dgx), dsp)
        vec_ref[V_GL_GC:V_GL_GC + 1, :] += cat(_colsum(dyl * pn), _colsum(dyc * qn))
        vec_ref[V_WL01:V_WL01 + 1, :] += cat(dwl[0], dwl[1])
        vec_ref[V_WL23:V_WL23 + 1, :] += cat(dwl[2], dwl[3])
        vec_ref[V_WS01:V_WS01 + 1, :] += cat(dws[0], dws[1])
        vec_ref[V_WS2:V_WS2 + 1, 0:W] += dws[2]

        r1 = lax.rsqrt(_rowmean(x * x) + EPS)
        xn1 = x * r1
        hb_ref[...] = ((xn1 * g1_ref[...]) * (1.0 + scale1) + shift1).astype(BF16)
        dh_in = jnp.zeros((tb, D), F32)
        for k, du in enumerate((du_lx, du_ly, du_b, du_c, du_v)):
            dprojt_ref[k * W:(k + 1) * W, :] = du.T.astype(BF16)
            dh_in = dh_in + _dot(du.astype(BF16), win_v[k * W:(k + 1) * W, :])
        vec_ref[V_SHIFT1:V_SHIFT1 + 1, :] += _colsum(dh_in)
        vec_ref[V_SCALE1:V_SCALE1 + 1, :] += _colsum(dh_in * xn1 * g1_ref[...])
        vec_ref[V_G1:V_G1 + 1, :] += _colsum(dh_in * (1.0 + scale1) * xn1)
        dxn1 = dh_in * g1_ref[...] * (1.0 + scale1)
        gx_ref[...] = dx1 + r1 * (dxn1 - xn1 * _rowmean(dxn1 * xn1))

        @pl.when(i == nb - 1)
        def _():
            for cp in chip_copies:
                cp.wait_recv()
            for cp in chip_copies:
                cp.wait_send()

    rev = lambda cols: pl.BlockSpec((tb, cols), lambda i: (nb - 1 - i, 0))
    rev_t = lambda rows: pl.BlockSpec((rows, tb), lambda i: (0, nb - 1 - i))
    halo = lambda cols: pl.BlockSpec((HALO, cols), lambda i: (jnp.maximum((nb - 1 - i) * hb - 1, 0), 0))
    full = lambda a: pl.BlockSpec(a.shape, lambda i: (0,) * a.ndim)
    small = (modraw, adab, g1, g2, wl, bl, bda, bdx, ba, bxb, ap, ws, gl, gc, avg)
    ext = pltpu.VMEM((tb + HALO, W), F32)
    n_sems = max(len(CHIP_FLIPS) * n_sums, 1)
    return pl.pallas_call(
        body,
        name="mixer_bwd",
        grid=(nb,),
        in_specs=[rev(D), rev(D), rev(D), rev(D), rev(D_IN), halo(D_IN), rev(W), halo(W), rev(N_KEPT * W)]
        + [full(a) for a in small] + [ANY] * (1 + n_sums),
        out_specs=[rev(D), pl.BlockSpec((V_ROWS, D), lambda i: (0, 0)), rev(D), rev_t(D_IN), rev(D), rev_t(D),
                   rev_t(W), rev(2 * W)] + [ANY] * n_sums,
        out_shape=[jax.ShapeDtypeStruct((t_len, D), F32), jax.ShapeDtypeStruct((V_ROWS, D), F32),
                   jax.ShapeDtypeStruct((t_len, D), BF16), jax.ShapeDtypeStruct((D_IN, t_len), BF16),
                   jax.ShapeDtypeStruct((t_len, D), BF16), jax.ShapeDtypeStruct((D, t_len), BF16),
                   jax.ShapeDtypeStruct((W, t_len), BF16), jax.ShapeDtypeStruct((t_len, 2 * W), BF16)]
        + [jax.ShapeDtypeStruct((len(CHIP_FLIPS),) + s.shape[1:], s.dtype) for s in chip_sums],
        scratch_shapes=[pltpu.VMEM((D_IN, D), BF16), pltpu.VMEM((D, D), BF16), pltpu.SemaphoreType.DMA((2,)),
                        ext, ext, ext, ext, ext, ext, pltpu.VMEM((HALO, W), F32),
                        pltpu.VMEM((tb, W), F32), pltpu.VMEM((tb, W), F32), pltpu.VMEM((tb, W), F32),
                        pltpu.SemaphoreType.DMA((n_sems,)), pltpu.SemaphoreType.DMA((n_sems,))],
        compiler_params=pltpu.CompilerParams(dimension_semantics=("arbitrary",), vmem_limit_bytes=VMEM_LIMIT),
    )(x, mixed, dh2, dx2, proj, proj, hl, hl, kept, *small, wpack, *chip_sums)


def _matmul(name, a, b, tm=512):
    m, k = a.shape
    n = b.shape[1]

    def body(a_ref, b_ref, o_ref):
        o_ref[...] = _dot(a_ref[...], b_ref[...])

    return pl.pallas_call(
        body,
        name=name,
        grid=(m // tm,),
        in_specs=[pl.BlockSpec((tm, k), lambda i: (i, 0)), pl.BlockSpec((k, n), lambda i: (0, 0))],
        out_specs=pl.BlockSpec((tm, n), lambda i: (i, 0)),
        out_shape=jax.ShapeDtypeStruct((m, n), F32),
        compiler_params=pltpu.CompilerParams(dimension_semantics=("arbitrary",), vmem_limit_bytes=VMEM_LIMIT),
    )(a, b)


def _gate_wgrad(xl_t, dgate, avg):
    hd = W // 8

    def body(a_ref, b_ref, avg_ref, o_ref):
        full = _dot(a_ref[...], b_ref[...])
        row = lax.broadcasted_iota(jnp.int32, (W, hd), 0)
        col = lax.broadcasted_iota(jnp.int32, (W, hd), 1)
        fold = ((row & (hd - 1)) == col).astype(BF16)
        keep = avg_ref[...] != 0
        for g in range(2):
            m = jnp.where(keep, full[:, g * W:(g + 1) * W], 0.0)
            hi = m.astype(BF16)
            rest = m - hi.astype(F32)
            mid = rest.astype(BF16)
            lo = (rest - mid.astype(F32)).astype(BF16)
            o_ref[g] = _dot(hi, fold) + _dot(mid, fold) + _dot(lo, fold)

    return pl.pallas_call(
        body,
        name="wgrad_gate",
        in_specs=[WHOLE] * 3,
        out_specs=WHOLE,
        out_shape=jax.ShapeDtypeStruct((2, W, hd), F32),
        compiler_params=pltpu.CompilerParams(vmem_limit_bytes=VMEM_LIMIT),
    )(xl_t, dgate, avg)


def _block_diag(w):
    n, m, _ = w.shape
    eye = jnp.eye(n, dtype=w.dtype)
    return (w[:, :, None, :] * eye[:, None, :, None]).reshape(n * m, n * m)


def _pad_rows(a, rows):
    return jnp.pad(a, ((0, rows - a.shape[0]),) + ((0, 0),) * (a.ndim - 1))


def _position():
    return lax.axis_index("x"), lax.axis_index("y"), lax.axis_index("c")


def _linear(pos):
    return 4 * pos[0] + 2 * pos[1] + pos[2]


def _flip(pos, k):
    return tuple(1 - p if k & bit else p for p, bit in zip(pos, (4, 2, 1)))


def _exchange_all(make_copy, make_arrival):
    copies = [make_copy(k) for k in range(1, N_DEV)]
    for cp in copies:
        cp.start()
    for k in range(1, N_DEV):
        make_arrival(k).wait_recv()
    for cp in copies:
        cp.wait_send()


def _mod_exchange_steps(cols):
    def steps(msg_ref, adaw_ref, gath_ref, mod_ref, sendbuf, send_a, recv_a, send_b, recv_b):
        me = _position()
        me_lin = _linear(me)
        m = msg_ref[...]
        row = lax.broadcasted_iota(jnp.int32, m.shape, 0)
        gath_ref[me_lin] = jnp.where(row == 0, m * _sigmoid(m), m)

        def gather_copy(k, src_lin):
            return pltpu.make_async_remote_copy(
                src_ref=gath_ref.at[src_lin], dst_ref=gath_ref.at[src_lin], send_sem=send_a.at[k - 1],
                recv_sem=recv_a.at[k - 1], device_id=_flip(me, k), device_id_type=MESH)

        _exchange_all(lambda k: gather_copy(k, me_lin), lambda k: gather_copy(k, _linear(_flip(me, k))))

        sc_all = gath_ref[:, 0, :]
        scb = jnp.concatenate([sc_all, jnp.zeros_like(sc_all)], axis=0).astype(BF16)
        prod = _dot(scb, adaw_ref[...].astype(BF16))
        for b in range(N_DEV):
            sendbuf[b] = jnp.broadcast_to(prod[b:b + 1, :], (HALO, cols))
        mod_ref[me_lin] = sendbuf[me_lin]

        def row_copy(k, dst_lin):
            peer = _flip(me, k)
            return pltpu.make_async_remote_copy(
                src_ref=sendbuf.at[_linear(peer)], dst_ref=mod_ref.at[dst_lin], send_sem=send_b.at[k - 1],
                recv_sem=recv_b.at[k - 1], device_id=peer, device_id_type=MESH)

        _exchange_all(lambda k: row_copy(k, me_lin), lambda k: row_copy(k, _linear(_flip(me, k))))

    return steps


def _gather_and_mod(msg, ada_w, block):
    rows, cols = block.shape
    mod_cols = ada_w.shape[1]
    mod_steps = _mod_exchange_steps(mod_cols)

    def body(msg_ref, adaw_ref, x_ref, gath_ref, mod_ref, out_ref, sendbuf, send_a, recv_a, send_b, recv_b,
             send_sems, recv_sems, sib_send_sems, sib_recv_sems, local_sem):
        x, y, c = _position()
        me, sibling = (x, y, c), (x, y, 1 - c)
        sends, forward, arrivals = _chip_gather_copies(x_ref, out_ref, send_sems, recv_sems)

        def to_sibling(j, block_of, src=None):
            dst = out_ref.at[_linear(block_of)]
            return pltpu.make_async_remote_copy(
                src_ref=dst if src is None else src, dst_ref=dst, send_sem=sib_send_sems.at[j],
                recv_sem=sib_recv_sems.at[j], device_id=sibling, device_id_type=MESH)

        mine = pltpu.make_async_copy(x_ref, out_ref.at[_linear(me)], local_sem)
        mine.start()
        passes = [to_sibling(0, me, src=x_ref)] + [to_sibling(1 + j, p) for j, p in enumerate(_route_peers(me))]
        passes[0].start()
        for cp in sends:
            cp.start()
        mod_steps(msg_ref, adaw_ref, gath_ref, mod_ref, sendbuf, send_a, recv_a, send_b, recv_b)
        arrivals[0].wait_recv()
        forward.start()
        passes[1].start()
        arrivals[1].wait_recv()
        passes[2].start()
        arrivals[2].wait_recv()
        passes[3].start()
        for j, p in enumerate((sibling,) + _route_peers(sibling)):
            to_sibling(j, p).wait_recv()
        for cp in sends + [forward] + passes:
            cp.wait_send()
        mine.wait()

    return pl.pallas_call(
        body,
        name="gather_and_mod",
        in_specs=[WHOLE, WHOLE, ANY],
        out_specs=[WHOLE, WHOLE, ANY],
        out_shape=[jax.ShapeDtypeStruct((N_DEV, HALO, D), F32), jax.ShapeDtypeStruct((N_DEV, HALO, mod_cols), F32),
                   jax.ShapeDtypeStruct((N_DEV, rows, cols), block.dtype)],
        scratch_shapes=[pltpu.VMEM((N_DEV, HALO, mod_cols), F32)] + [pltpu.SemaphoreType.DMA((N_DEV - 1,))] * 4
        + [pltpu.SemaphoreType.DMA((3,)), pltpu.SemaphoreType.DMA((3,)), pltpu.SemaphoreType.DMA((4,)),
           pltpu.SemaphoreType.DMA((4,)), pltpu.SemaphoreType.DMA],
        compiler_params=pltpu.CompilerParams(vmem_limit_bytes=VMEM_LIMIT),
    )(msg, ada_w, block)


HBM = pl.BlockSpec(memory_space=pltpu.HBM)
SEM = pl.BlockSpec(memory_space=pltpu.SEMAPHORE)
EFFECT = pltpu.SideEffectType.DATAFLOW_SIDE_EFFECTING


def _stage_copies(stage):
    return {"chips": (_chip_scatter_copies, len(CHIP_FLIPS), len(CHIP_FLIPS)), "sibling": (_sibling_copies, 4, 4)}[stage]


def _chips_start(which, chip_sums, stage="chips"):
    n = len(chip_sums)
    make_copies, per_array, slots = _stage_copies(stage)
    n_sems = per_array * n

    def body(*refs):
        srcs, dsts = refs[:n], refs[n:2 * n]
        send_sems, recv_sems = refs[2 * n:2 * n + 2]
        token = refs[-1]
        for cp in make_copies(srcs, dsts, send_sems, recv_sems):
            cp.start()
        token[...] = jnp.zeros(token.shape, token.dtype)

    landing = [jax.ShapeDtypeStruct((slots,) + s.shape[-2:], s.dtype) for s in chip_sums]
    outs = pl.pallas_call(
        body,
        name=which + "_" + stage + "_start",
        in_specs=[HBM] * (2 * n),
        out_specs=[SEM, SEM] + [HBM] * (2 * n) + [WHOLE],
        out_shape=[pltpu.SemaphoreType.DMA((n_sems,)), pltpu.SemaphoreType.DMA((n_sems,))]
        + [pltpu.HBM(s.shape, s.dtype) for s in chip_sums] + [pltpu.HBM(s.shape, s.dtype) for s in landing]
        + [jax.ShapeDtypeStruct((HALO, 128), F32)],
        input_output_aliases={i: 2 + i for i in range(2 * n)},
        compiler_params=pltpu.CompilerParams(has_side_effects=EFFECT),
    )(*[pltpu.with_memory_space_constraint(s, pltpu.HBM) for s in chip_sums],
      *[pltpu.with_memory_space_constraint(lax.empty(s.shape, s.dtype), pltpu.HBM) for s in landing])
    return outs[0], outs[1], outs[2:2 + n], outs[2 + n:2 + 2 * n], outs[-1]


def _chips_wait(which, send_sems, recv_sems, srcs, landed, after, stage="chips"):
    n = len(srcs)
    make_copies = _stage_copies(stage)[0]

    def body(*refs):
        src_refs, dst_refs = refs[:n], refs[n:2 * n]
        sends, recvs = refs[2 * n:2 * n + 2]
        copies = make_copies(src_refs, dst_refs, sends, recvs)
        for cp in copies:
            cp.wait_send()
        for cp in copies:
            cp.wait_recv()

    outs = pl.pallas_call(
        body,
        name=which + "_" + stage + "_wait",
        in_specs=[HBM] * (2 * n) + [SEM, SEM, ANY],
        out_specs=[HBM] * (2 * n),
        out_shape=[pltpu.HBM(s.shape, s.dtype) for s in list(srcs) + list(landed)],
        input_output_aliases={i: i for i in range(2 * n)},
        compiler_params=pltpu.CompilerParams(has_side_effects=EFFECT),
    )(*srcs, *landed, send_sems, recv_sems, after)
    return list(outs[:n]), list(outs[n:])


def _sibling_copies(srcs, dsts, send_sems, recv_sems):
    x, y, c = _position()
    copies = []
    for a, (src, dst) in enumerate(zip(srcs, dsts)):
        for k in range(4):
            copies.append(pltpu.make_async_remote_copy(
                src_ref=src.at[k, 1 - c] if len(src.shape) == 4 else src.at[k], dst_ref=dst.at[k],
                send_sem=send_sems.at[4 * a + k],
                recv_sem=recv_sems.at[4 * a + k], device_id=(x, y, 1 - c), device_id_type=MESH))
    return copies


def _row_block(rows):
    return min(rows, 512)


def _pair_sum(pos, mine, recv):
    _, cores, rows, cols = mine.shape
    rb = _row_block(rows)

    def body(pos_ref, mine_ref, recv_ref, out_ref):
        out_ref[0] = (mine_ref[0, 0] + recv_ref[0]).astype(BF16)

    other = lambda k, pos: jnp.bitwise_xor(pos[1], k + 1)
    core = lambda pos: pos[0] * (cores - 1)
    return pl.pallas_call(
        body,
        name="grad_pair_sum",
        grid_spec=pltpu.PrefetchScalarGridSpec(
            num_scalar_prefetch=1, grid=(3, rows // rb),
            in_specs=[pl.BlockSpec((1, 1, rb, cols), lambda k, r, pos: (other(k, pos), core(pos), r, 0)),
                      pl.BlockSpec((1, rb, cols), lambda k, r, pos: (other(k, pos), r, 0))],
            out_specs=pl.BlockSpec((1, rb, cols), lambda k, r, pos: (other(k, pos), r, 0))),
        out_shape=jax.ShapeDtypeStruct((4, rows, cols), BF16),
        compiler_params=pltpu.CompilerParams(dimension_semantics=("arbitrary", "arbitrary")),
    )(pos, mine, recv)


def _final_sum(pos, mine, recv, chips):
    _, cores, rows, cols = mine.shape
    rb = _row_block(rows)

    def body(pos_ref, mine_ref, recv_ref, chips_ref, out_ref):
        g = mine_ref[0, 0] + recv_ref[0]
        for j in range(3):
            g = g + chips_ref[j].astype(F32)
        out_ref[...] = g

    return pl.pallas_call(
        body,
        name="grad_final_sum",
        grid_spec=pltpu.PrefetchScalarGridSpec(
            num_scalar_prefetch=1, grid=(rows // rb,),
            in_specs=[pl.BlockSpec((1, 1, rb, cols), lambda r, pos: (pos[1], pos[0] * (cores - 1), r, 0)),
                      pl.BlockSpec((1, rb, cols), lambda r, pos: (pos[1], r, 0)),
                      pl.BlockSpec((3, rb, cols), lambda r, pos: (0, r, 0))],
            out_specs=pl.BlockSpec((rb, cols), lambda r, pos: (r, 0))),
        out_shape=jax.ShapeDtypeStruct((rows, cols), F32),
        compiler_params=pltpu.CompilerParams(dimension_semantics=("arbitrary",)),
    )(pos, mine, recv, chips)


LOSS_ROW = V_ROWS + 8
GB_BASE = LOSS_ROW + 8


def _route_mod_grad_steps(cols):
    def steps(gmod_ref, sct_ref, gadaw_ref, sendbuf, grecv, send_a, recv_a):
        me = _position()
        me_lin = _linear(me)
        gm = gmod_ref[...]
        for b in range(N_DEV):
            sendbuf[b] = jnp.broadcast_to(gm[b:b + 1, :], (HALO, cols))
        grecv[me_lin] = sendbuf[me_lin]

        def row_copy(k, dst_lin):
            peer = _flip(me, k)
            return pltpu.make_async_remote_copy(
                src_ref=sendbuf.at[_linear(peer)], dst_ref=grecv.at[dst_lin], send_sem=send_a.at[k - 1],
                recv_sem=recv_a.at[k - 1], device_id=peer, device_id_type=MESH)

        _exchange_all(lambda k: row_copy(k, me_lin), lambda k: row_copy(k, _linear(_flip(me, k))))
        g_all = grecv[:, 0, :]
        g_pad = jnp.concatenate([g_all, jnp.zeros((sct_ref.shape[1] - N_DEV, cols), F32)], axis=0).astype(BF16)
        gadaw_ref[...] = _dot(sct_ref[...], g_pad)
        return _colsum(g_all)

    return steps


def _small_grad_exchange(gmod8, sc_t, msg_vec, msg_gate, after):
    cols = gmod8.shape[1]
    vec_rows = GB_BASE + N_DEV
    route_steps = _route_mod_grad_steps(cols)

    def body(gmod_ref, sct_ref, vec_ref, gate_ref, after_ref, gadaw_ref, sumv_ref, sumg_ref,
             sendbuf, grecv, send_a, recv_a, myv, myg, sibv, sibg, chipv, chipg, sib_send, sib_recv, peer_send, peer_recv):
        gb = route_steps(gmod_ref, sct_ref, gadaw_ref, sendbuf, grecv, send_a, recv_a)
        x, y, c = me = _position()
        my_chip = 2 * x + y
        myv[0:GB_BASE, :] = vec_ref[...]
        slot = lax.broadcasted_iota(jnp.int32, (N_DEV, D), 0) == _linear(me)
        gb_wide = jnp.concatenate([jnp.broadcast_to(gb, (N_DEV, cols)), jnp.zeros((N_DEV, D - cols), F32)], axis=1)
        myv[GB_BASE:vec_rows, :] = jnp.where(slot, gb_wide, 0.0)
        myg[...] = gate_ref[...]

        swaps = [pltpu.make_async_remote_copy(
            src_ref=src, dst_ref=dst, send_sem=sib_send.at[a], recv_sem=sib_recv.at[a], device_id=(x, y, 1 - c),
            device_id_type=MESH) for a, (src, dst) in enumerate(((myv, sibv), (myg, sibg)))]
        for cp in swaps:
            cp.start()
        for cp in swaps:
            cp.wait_recv()
        chipv[my_chip] = myv[...] + sibv[...]
        chipg[my_chip] = myg[...] + sibg[...]

        def chip_copy(a, buf, j, k, slot_chip):
            peer = _flip(me, k)
            return pltpu.make_async_remote_copy(
                src_ref=buf.at[slot_chip], dst_ref=buf.at[slot_chip], send_sem=peer_send.at[3 * a + j],
                recv_sem=peer_recv.at[3 * a + j], device_id=peer, device_id_type=MESH)

        sends = [chip_copy(a, buf, j, k, my_chip) for a, buf in enumerate((chipv, chipg)) for j, k in enumerate(CHIP_FLIPS)]
        for cp in sends:
            cp.start()
        for a, buf in enumerate((chipv, chipg)):
            for j, k in enumerate(CHIP_FLIPS):
                peer = _flip(me, k)
                chip_copy(a, buf, j, k, 2 * peer[0] + peer[1]).wait_recv()
        sumv_ref[...] = ((chipv[0] + chipv[1]) + chipv[2]) + chipv[3]
        sumg_ref[...] = ((chipg[0] + chipg[1]) + chipg[2]) + chipg[3]
        for cp in swaps + sends:
            cp.wait_send()

    vshape, gshape = (vec_rows, D), msg_gate.shape
    return pl.pallas_call(
        body,
        name="small_grad_exchange",
        in_specs=[WHOLE] * 5,
        out_specs=[WHOLE] * 3,
        out_shape=[jax.ShapeDtypeStruct((D, cols), F32), jax.ShapeDtypeStruct(vshape, F32),
                   jax.ShapeDtypeStruct(gshape, F32)],
        scratch_shapes=[pltpu.VMEM((N_DEV, HALO, cols), F32), pltpu.VMEM((N_DEV, HALO, cols), F32),
                        pltpu.SemaphoreType.DMA((N_DEV - 1,)), pltpu.SemaphoreType.DMA((N_DEV - 1,)),
                        pltpu.VMEM(vshape, F32), pltpu.VMEM(gshape, F32), pltpu.VMEM(vshape, F32),
                        pltpu.VMEM(gshape, F32), pltpu.VMEM((4,) + vshape, F32), pltpu.VMEM((4,) + gshape, F32),
                        pltpu.SemaphoreType.DMA((2,)), pltpu.SemaphoreType.DMA((2,)),
                        pltpu.SemaphoreType.DMA((2 * len(CHIP_FLIPS),)), pltpu.SemaphoreType.DMA((2 * len(CHIP_FLIPS),))],
        compiler_params=pltpu.CompilerParams(vmem_limit_bytes=VMEM_LIMIT),
    )(gmod8, sc_t, msg_vec, msg_gate, after)


def _adamw_math(w, g, m, v):
    m = ADAM_B1 * m + (1.0 - ADAM_B1) * g
    v = ADAM_B2 * v + (1.0 - ADAM_B2) * (g * g)
    m_hat = m / (1.0 - ADAM_B1 ** ADAM_STEP)
    v_hat = v / (1.0 - ADAM_B2 ** ADAM_STEP)
    delta = -ADAM_LR * (m_hat / (jnp.sqrt(v_hat) + ADAM_EPS) + ADAM_WD * w)
    return delta, m, v


def _adamw(name, w, g, m, v):
    rows, cols = w.shape
    rb = 256 if rows % 256 == 0 else rows

    def body(w_ref, g_ref, m_ref, v_ref, d_ref, mo_ref, vo_ref):
        d_ref[...], mo_ref[...], vo_ref[...] = _adamw_math(w_ref[...], g_ref[...], m_ref[...], v_ref[...])

    spec = pl.BlockSpec((rb, cols), lambda r: (r, 0))
    return pl.pallas_call(
        body,
        name="adamw_" + name,
        grid=(rows // rb,),
        in_specs=[spec] * 4,
        out_specs=[spec] * 3,
        out_shape=[jax.ShapeDtypeStruct((rows, cols), F32)] * 3,
        compiler_params=pltpu.CompilerParams(dimension_semantics=("arbitrary",)),
    )(w, g, m, v)


def _update(pos, sum_jobs, plain_jobs, after):
    rb = 256
    jobs = [("sum", j) for j in sum_jobs] + [("plain", j) for j in plain_jobs]
    offs, total = [], 0
    for _, j in jobs:
        offs.append(total)
        total += j[-1].shape[0] // rb
    n_in = sum(len(j) for _, j in jobs)

    def body(pos_ref, *refs):
        ins, outs = refs[:n_in], refs[n_in + 1:]
        s = pl.program_id(0)
        i_in = i_out = 0
        for (kind, j), off in zip(jobs, offs):
            steps = j[-1].shape[0] // rb
            j_in = ins[i_in:i_in + len(j)]
            i_in += len(j)
            j_out = outs[i_out:i_out + (4 if kind == "sum" else 3)]
            i_out += len(j_out)

            @pl.when((s >= off) & (s < off + steps))
            def _(kind=kind, j_in=j_in, j_out=j_out):
                if kind == "sum":
                    mine_ref, recv_ref, chips_ref, w_ref, m_ref, v_ref = j_in
                    g = mine_ref[0, 0] + recv_ref[0]
                    for q in range(len(CHIP_FLIPS)):
                        g = g + chips_ref[q].astype(F32)
                    j_out[0][...] = g
                    rest = j_out[1:]
                else:
                    g_ref, w_ref, m_ref, v_ref = j_in
                    g = g_ref[...]
                    rest = j_out
                rest[0][...], rest[1][...], rest[2][...] = _adamw_math(w_ref[...], g, m_ref[...], v_ref[...])

    in_specs, out_specs, out_shape, args = [], [], [], []
    for (kind, j), off in zip(jobs, offs):
        rows, cols = j[-1].shape
        steps = rows // rb
        blk = lambda s, off=off, steps=steps: jnp.clip(s - off, 0, steps - 1)
        flat = pl.BlockSpec((rb, cols), lambda s, pos, blk=blk: (blk(s), 0))
        if kind == "sum":
            in_specs += [pl.BlockSpec((1, 1, rb, cols), lambda s, pos, blk=blk: (pos[1], 0, blk(s), 0)),
                         pl.BlockSpec((1, rb, cols), lambda s, pos, blk=blk: (pos[1], blk(s), 0)),
                         pl.BlockSpec((len(CHIP_FLIPS), rb, cols), lambda s, pos, blk=blk: (0, blk(s), 0))]
            in_specs += [flat] * 3
        else:
            in_specs += [flat] * 4
        n_res = 4 if kind == "sum" else 3
        out_specs += [flat] * n_res
        out_shape += [jax.ShapeDtypeStruct((rows, cols), F32)] * n_res
        args += list(j)
    in_specs += [pl.BlockSpec(after.shape, lambda s, pos: (0,) * after.ndim)]
    outs = pl.pallas_call(
        body,
        name="update",
        grid_spec=pltpu.PrefetchScalarGridSpec(
            num_scalar_prefetch=1, grid=(total,), in_specs=in_specs, out_specs=out_specs),
        out_shape=out_shape,
        compiler_params=pltpu.CompilerParams(dimension_semantics=("arbitrary",), vmem_limit_bytes=VMEM_LIMIT),
    )(pos, *args, after)
    sums = [tuple(outs[4 * i:4 * i + 4]) for i in range(len(sum_jobs))]
    base = 4 * len(sum_jobs)
    plains = [tuple(outs[base + 3 * i:base + 3 * i + 3]) for i in range(len(plain_jobs))]
    return sums, plains


def _adamw_small(ws, gs, ms, vs, sigmoid_scaled):
    n = len(ws)

    def body(*refs):
        w_refs, g_refs, m_refs, v_refs = (refs[i * n:(i + 1) * n] for i in range(4))
        outs = refs[4 * n:]
        for i in range(n):
            w = w_refs[i][...]
            g = g_refs[i][...]
            if sigmoid_scaled[i]:
                g = g * _sigmoid(w)
            delta, m, v = _adamw_math(w, g, m_refs[i][...], v_refs[i][...])
            outs[4 * i][...] = g
            outs[4 * i + 1][...] = delta
            outs[4 * i + 2][...] = m
            outs[4 * i + 3][...] = v

    shapes = [jax.ShapeDtypeStruct(w.shape, F32) for w in ws for _ in range(4)]
    outs = pl.pallas_call(
        body,
        name="adamw_small",
        in_specs=[WHOLE] * (4 * n),
        out_specs=[WHOLE] * (4 * n),
        out_shape=shapes,
    )(*ws, *gs, *ms, *vs)
    return [outs[4 * i:4 * i + 4] for i in range(n)]


_WEIGHT_NAMES = ("ada_w", "ada_b", "norm1_g", "w_in", "lru_conv_w", "lru_conv_b", "gate_a_w", "gate_a_b", "gate_x_w",
                 "gate_x_b", "a_param", "short_conv_w", "lru_out_g", "conv_out_g", "w_out", "norm2_g", "w_mlp1",
                 "w_mlp2", "final_g")


def kernel(x, c, ada_w, ada_b, norm1_g, w_in, lru_conv_w, lru_conv_b, gate_a_w, gate_a_b, gate_x_w, gate_x_b, a_param, short_conv_w, lru_out_g, conv_out_g, w_out, norm2_g, w_mlp1, w_mlp2, final_g, loss_target, m_ada_w, m_ada_b, m_norm1_g, m_w_in, m_lru_conv_w, m_lru_conv_b, m_gate_a_w, m_gate_a_b, m_gate_x_w, m_gate_x_b, m_a_param, m_short_conv_w, m_lru_out_g, m_conv_out_g, m_w_out, m_norm2_g, m_w_mlp1, m_w_mlp2, m_final_g, v_ada_w, v_ada_b, v_norm1_g, v_w_in, v_lru_conv_w, v_lru_conv_b, v_gate_a_w, v_gate_a_b, v_gate_x_w, v_gate_x_b, v_a_param, v_short_conv_w, v_lru_out_g, v_conv_out_g, v_w_out, v_norm2_g, v_w_mlp1, v_w_mlp2, v_final_g):
    given = dict(locals())
    weights = {n: given[n] for n in _WEIGHT_NAMES}
    xi, yi, ci = _position()
    me_lin = _linear((xi, yi, ci))
    hd = W // N_DEV

    mixer_block = jnp.concatenate([w_out[0], w_in[0].T], axis=0).astype(BF16)
    mlp_block = jnp.concatenate([w_mlp1[0].T, w_mlp2[0]], axis=0).astype(BF16)

    msg = (jnp.pad(c, ((0, HALO - 1), (0, 0)))
           + jnp.pad(lru_conv_w[0], ((1, HALO - 1 - CONV_L), (0, D - hd)))
           + jnp.pad(short_conv_w[0], ((1 + CONV_L, 0), (0, D - hd))))
    gath, mod_all, wmix = _gather_and_mod(msg, ada_w[0], mixer_block)
    sc_all = gath[:, 0, :]
    wl = jnp.transpose(gath[:, 1:1 + CONV_L, :hd], (1, 0, 2)).reshape(CONV_L, W)
    ws = jnp.transpose(gath[:, 1 + CONV_L:HALO, :hd], (1, 0, 2)).reshape(CONV_S, W)
    modraw = _pad_rows(mod_all[:, 0, :].reshape(6, D), HALO)
    adab = _pad_rows(ada_b.reshape(6, D), HALO)

    x2d, tgt = x[0], loss_target[0]
    gf = final_g.reshape(1, D)
    bda = _block_diag(gate_a_w[0]).astype(BF16)
    bdx = _block_diag(gate_x_w[0]).astype(BF16)
    avg = _block_diag(jnp.full((8, W // 8, W // 8), 8.0 / W, F32)).astype(BF16)
    wl8 = _pad_rows(wl, HALO)
    ws8 = _pad_rows(ws, HALO)
    mixer_small = (wl8, lru_conv_b, bda, bdx, gate_a_b, gate_x_b, a_param, ws8, lru_out_g, conv_out_g, avg)
    proj, hl, mixed, kept, wmlp = _mixer_fwd(x2d, modraw, adab, norm1_g, *mixer_small, wmix, mlp_block)
    wmlp = _sibling_forward(wmlp)
    h2t, f, dx2, dz, vec2, loss8 = _mlp_fwd(x2d, mixed, tgt, modraw, adab, norm2_g, gf, wmlp)
    pos = jnp.stack([ci, 2 * xi + yi]).astype(jnp.int32)
    by_dest = lambda g: g.reshape((4, 2, -1) + g.shape[-1:])
    dh2_first, *for_sibling = _mlp_bwd_half(pos, h2t, f, dz, wmlp)
    sib_send, sib_recv, sib_thru, sib_land, token = _chips_start("mlp", for_sibling, stage="sibling")
    dh2, dw1, dw2 = _mlp_bwd_half(pos, h2t, f, dz, wmlp, prior=(dh2_first, token))
    done = dh2[0:HALO, 0:128] + dw1[0, 0:HALO, 0:128] + dw2[0, 0:HALO, 0:128]
    _, mlp_sib = _chips_wait("mlp", sib_send, sib_recv, sib_thru, sib_land, done, stage="sibling")
    mlp_parts = [dw1[:, None], dw2[:, None]]
    mlp_sums = [_pair_sum(pos, p, r) for p, r in zip(mlp_parts, mlp_sib)]
    mlp_send, mlp_recv, mlp_thru, mlp_land, token = _chips_start("mlp", mlp_sums)
    modraw_after = modraw + jnp.tile(token, (1, D // token.shape[1]))
    gx, vec, hb, dproj_t, dmixed, ycat_t, xl_t, dgate = _mixer_bwd(
        x2d, mixed, dh2, dx2, proj, hl, kept, modraw_after, adab, norm1_g, norm2_g, *mixer_small, wmix, [], None)
    dwint = _matmul("wgrad_in", dproj_t, hb)
    dwout = _matmul("wgrad_out", ycat_t, dmixed)
    gate_blocks = _gate_wgrad(xl_t, dgate, avg)
    msg_gate = gate_blocks.reshape(W, 128)
    done = dwint[0:HALO, 0:128] + dwout[0:HALO, 0:128] + gate_blocks[0, 0:HALO, :].sum() + gx[0:HALO, 0:128]
    _, mlp_chips = _chips_wait("mlp", mlp_send, mlp_recv, mlp_thru, mlp_land, done)
    mix_parts = [by_dest(dwout), by_dest(dwint)]
    gmod8 = (jnp.pad(vec[0:5], ((0, 1), (0, 0))) + jnp.pad(vec2[0:1], ((5, 0), (0, 0)))).reshape(N_DEV, 6 * D // N_DEV)
    sc_t = jnp.pad(sc_all.T, ((0, 0), (0, 128 - N_DEV))).astype(BF16)
    loss_rows = jnp.pad(loss8[0:1], ((0, HALO - 1), (0, D - loss8.shape[1])))
    msg_vec = jnp.concatenate([vec, vec2, loss_rows], axis=0)
    sib_send, sib_recv, sib_thru, sib_land, token = _chips_start("mixer", mix_parts, stage="sibling")
    g_adaw, sum_vec, sum_gate = _small_grad_exchange(gmod8, sc_t, msg_vec, msg_gate, token)
    mix_parts, mix_sib = _chips_wait("mixer", sib_send, sib_recv, sib_thru, sib_land, sum_vec[0:HALO, 0:128],
                                     stage="sibling")
    mix_sums = [_pair_sum(pos, p, r) for p, r in zip(mix_parts, mix_sib)]
    state = lambda n: (weights[n][0], given["m_" + n][0], given["v_" + n][0])
    mlp_jobs = [(p, r, q, *state(n)) for p, r, q, n in zip(mlp_parts, mlp_sib, mlp_chips, ("w_mlp1", "w_mlp2"))]
    mix_send, mix_recv, mix_thru, mix_land, token = _chips_start("mixer", mix_sums)
    mlp_done, (adaw_done,) = _update(pos, mlp_jobs, [(g_adaw, *state("ada_w"))], token)
    loss = sum_vec[LOSS_ROW, 0]
    sum_gate = sum_gate.reshape(2, W, W // 8)
    lo, hi = slice(0, W), slice(W, 2 * W)
    wl_full = sum_vec[V_WL01:V_WL23 + 1].reshape(CONV_L, W)
    ws_full = sum_vec[V_WS01:V_WS2 + 1].reshape(CONV_S + 1, W)[:CONV_S]
    row = lambda r, cols: sum_vec[r:r + 1, cols]
    small_grads = {
        "ada_b": sum_vec[GB_BASE:GB_BASE + N_DEV, :6 * D // N_DEV].reshape(1, 6 * D),
        "norm1_g": row(V_G1, slice(0, D)),
        "lru_conv_w": lax.dynamic_slice(wl_full, (0, me_lin * hd), (CONV_L, hd)),
        "lru_conv_b": row(V_BL_BA, lo),
        "gate_a_w": sum_gate[0],
        "gate_a_b": row(V_BL_BA, hi),
        "gate_x_w": sum_gate[1],
        "gate_x_b": row(V_BX_SP, lo),
        "a_param": row(V_BX_SP, hi),
        "short_conv_w": lax.dynamic_slice(ws_full, (0, me_lin * hd), (CONV_S, hd)),
        "lru_out_g": row(V_GL_GC, lo),
        "conv_out_g": row(V_GL_GC, hi),
        "norm2_g": row(V_G2, slice(0, D)),
        "final_g": sum_vec[V_ROWS + 1:V_ROWS + 2, :],
    }
    names = list(small_grads)
    as2d = lambda a, n: a.reshape(small_grads[n].shape)
    small = _adamw_small([as2d(weights[n], n) for n in names], [small_grads[n] for n in names],
                         [as2d(given["m_" + n], n) for n in names], [as2d(given["v_" + n], n) for n in names],
                         [n == "a_param" for n in names])
    result = {n: tuple(o.reshape(weights[n].shape) for o in outs) for n, outs in zip(names, small)}

    done = (small[0][1][:, 0:128] + mlp_done[0][2][0:HALO, 0:128] + mlp_done[1][2][0:HALO, 0:128]
            + adaw_done[1][0:HALO, 0:128])
    _, mix_chips = _chips_wait("mixer", mix_send, mix_recv, mix_thru, mix_land, done)
    g_wout, g_wint = (_final_sum(pos, p, r, q) for p, r, q in zip(mix_parts, mix_sib, mix_chips))
    for n, g in (("w_in", g_wint.T), ("w_out", g_wout)):
        w, m, v = state(n)
        result[n] = (g[None],) + tuple(o[None] for o in _adamw(n, w, g, m, v))
    result["w_mlp1"], result["w_mlp2"] = (tuple(o[None] for o in done) for done in mlp_done)
    result["ada_w"] = (g_adaw[None],) + tuple(o[None] for o in adaw_done)

    return (loss, gx[None], *[result[n][0] for n in _WEIGHT_NAMES], *[result[n][1] for n in _WEIGHT_NAMES],
            *[result[n][2] for n in _WEIGHT_NAMES], *[result[n][3] for n in _WEIGHT_NAMES])
```

```python
import functools

import jax
import jax.numpy as jnp
from jax import lax
from jax.experimental import pallas as pl
from jax.experimental.pallas import tpu as pltpu

F32 = jnp.float32
BF16 = jnp.bfloat16
MESH = pl.DeviceIdType.MESH

N_DEV = 8
D = 1024
W = 512
D_IN = 5 * W
D_FF = 4096
FF_BLK = D_FF // N_DEV
EPS = 1e-6
C_GATE = 8.0
CONV_L = 4
CONV_S = 3
HALO = 8

ROWS_W1T, ROWS_W2, ROWS_WOUT, ROWS_WIN = FF_BLK, FF_BLK, D // N_DEV, D_IN // N_DEV
OFF_WOUT = 0
OFF_WIN = OFF_WOUT + ROWS_WOUT
MIX_ROWS = OFF_WIN + ROWS_WIN
OFF_W1T = 0
OFF_W2 = OFF_W1T + ROWS_W1T
MLP_ROWS = OFF_W2 + ROWS_W2
CHIP_FLIPS = (4, 2, 6)
N_KEPT = 6

ADAM_LR = 0.001
ADAM_B1 = 0.9
ADAM_B2 = 0.999
ADAM_EPS = 1e-08
ADAM_WD = 0.01
ADAM_STEP = 10

VMEM_LIMIT = 56 * 1024 * 1024

TB_MIX = 256
TB_MIXB = 256
TB_MLP = 256
TB_MLPB = 512

ANY = pl.BlockSpec(memory_space=pl.ANY)
WHOLE = pl.BlockSpec(memory_space=pltpu.VMEM)


def _dot(a, b):
    return jnp.dot(a, b, preferred_element_type=F32)


def _dot_nt(a, b):
    return lax.dot_general(a, b, (((1,), (1,)), ((), ())), preferred_element_type=F32)


def _dot_tn(a, b):
    return lax.dot_general(a, b, (((0,), (0,)), ((), ())), preferred_element_type=F32)


def _sigmoid(v):
    return 1.0 / (1.0 + jnp.exp(-v))


def _softplus(v):
    t = jnp.exp(-jnp.abs(v))
    small = t * (1.0 - t * (0.5 - t * (1.0 / 3.0)))
    return jnp.maximum(v, 0.0) + jnp.where(t < 1e-2, small, jnp.log(1.0 + t))


def _one_minus_sq(a, log_a):
    return -jnp.tanh(log_a) * (a * a + 1.0)


_GELU_K = 0.7978845608028654
_GELU_C = 0.044715


def _gelu(u):
    th = jnp.tanh(_GELU_K * (u + _GELU_C * u * u * u))
    return 0.5 * u * (1.0 + th), th


def _gelu_grad(u, th):
    return 0.5 * (1.0 + th) + 0.5 * u * (1.0 - th * th) * _GELU_K * (1.0 + 3.0 * _GELU_C * u * u)


def _group_mean(v, avg):
    hi = v.astype(BF16)
    lo = (v - hi.astype(F32)).astype(BF16)
    return _dot(hi, avg) + _dot(lo, avg)


def _colsum(v):
    return jnp.sum(v, axis=0, keepdims=True)


def _rowmean(v):
    return jnp.mean(v, axis=-1, keepdims=True)


def _load_packed(wpack_hbm, off, rows, dst, sem):
    copies = [
        pltpu.make_async_copy(wpack_hbm.at[d, pl.ds(off, rows), :], dst.at[pl.ds(d * rows, rows), :], sem)
        for d in range(N_DEV)
    ]
    for cp in copies:
        cp.start()
    return copies


def _scan_groups(n_groups, a_ref, b_ref, out_ref, carry_ref, reverse):
    row = lax.broadcasted_iota(jnp.int32, (HALO, W), 0)

    def step(k, carry):
        g = (n_groups - 1 - k) if reverse else k
        rows = pl.ds(pl.multiple_of(g * HALO, HALO), HALO)
        a = a_ref[rows, :]
        b = b_ref[rows, :]
        for s in (1, 2, 4):
            if reverse:
                keep = row < HALO - s
                sh = HALO - s
            else:
                keep = row >= s
                sh = s
            a_sh = pltpu.roll(a, sh, axis=0)
            b_sh = pltpu.roll(b, sh, axis=0)
            b = jnp.where(keep, a * b_sh + b, b)
            a = jnp.where(keep, a * a_sh, a)
        h = b + a * carry
        out_ref[rows, :] = h
        edge = h[0:1, :] if reverse else h[HALO - 1:HALO, :]
        return jnp.broadcast_to(edge, (HALO, W))

    carry_ref[...] = lax.fori_loop(0, n_groups, step, carry_ref[...])


def _route_peers(me):
    x, y, c = me
    first = ((x + 1 - c) % 2, (y + c) % 2, c)
    second = ((x + c) % 2, (y + 1 - c) % 2, c)
    return first, second, (1 - x, 1 - y, c)


def _chip_gather_copies(block_hbm, out_hbm, send_sems, recv_sems):
    me = _position()
    first, second, diag = _route_peers(me)

    def copy(j, src, slot_of, to):
        return pltpu.make_async_remote_copy(
            src_ref=src, dst_ref=out_hbm.at[_linear(slot_of)], send_sem=send_sems.at[j], recv_sem=recv_sems.at[j],
            device_id=to, device_id_type=MESH)

    own_sends = [copy(0, block_hbm, me, first), copy(1, block_hbm, me, second)]
    forward = copy(2, out_hbm.at[_linear(first)], first, second)
    arrivals = [copy(0, block_hbm, first, first), copy(1, block_hbm, second, second), copy(2, block_hbm, diag, second)]
    return own_sends, forward, arrivals


def _mixer_fwd(x, modraw, adab, g1, wl, bl, bda, bdx, ba, bxb, ap, ws, gl, gc, avg, wpack, mlp_block):
    t_len = x.shape[0]
    tb = TB_MIX
    nb = t_len // tb

    def body(x_ref, modraw_ref, adab_ref, g1_ref, wl_ref, bl_ref, bda_ref, bdx_ref, ba_ref, bxb_ref, ap_ref,
             ws_ref, gl_ref, gc_ref, avg_ref, wpack_hbm, block_hbm, proj_ref, hl_ref, mixed_ref, kept_ref, wmlp_hbm,
             win_v, wout_v, sem, ulx_ext, cv_ext, hcar, a_s, b_s, send_sems, recv_sems, sib_send_sems, sib_recv_sems,
             local_sem):
        i = pl.program_id(0)
        x_pos, y_pos, c_pos = me = _position()
        sibling = (x_pos, y_pos, 1 - c_pos)
        own = pltpu.make_async_copy(block_hbm, wmlp_hbm.at[_linear(me)], local_sem)
        sends, forward, arrivals = _chip_gather_copies(block_hbm, wmlp_hbm, send_sems, recv_sems)

        def to_sibling(j, block_of, src=None):
            dst = wmlp_hbm.at[_linear(block_of)]
            return pltpu.make_async_remote_copy(
                src_ref=dst if src is None else src, dst_ref=dst, send_sem=sib_send_sems.at[j],
                recv_sem=sib_recv_sems.at[j], device_id=sibling, device_id_type=MESH)

        passes = [to_sibling(0, me, src=block_hbm)] + [to_sibling(1 + j, p) for j, p in enumerate(_route_peers(me))]

        @pl.when(i == 0)
        def _():
            own.start()
            for cp in sends:
                cp.start()
            passes[0].start()

        @pl.when(i == nb - 1)
        def _():
            arrivals[0].wait_recv()
            forward.start()
            passes[1].start()

        @pl.when(i == 0)
        def _():
            cps = _load_packed(wpack_hbm, OFF_WIN, ROWS_WIN, win_v, sem.at[0])
            cps += _load_packed(wpack_hbm, OFF_WOUT, ROWS_WOUT, wout_v, sem.at[1])
            ulx_ext[0:HALO, :] = jnp.zeros((HALO, W), F32)
            cv_ext[0:HALO, :] = jnp.zeros((HALO, W), F32)
            hcar[...] = jnp.zeros((HALO, W), F32)
            for cp in cps:
                cp.wait()

        mod = modraw_ref[...] + adab_ref[...]
        shift1, scale1, gate1 = mod[0:1], mod[1:2], mod[2:3]
        x = x_ref[...]
        r1 = lax.rsqrt(_rowmean(x * x) + EPS)
        h = (x * r1 * g1_ref[...]) * (1.0 + scale1) + shift1
        proj = _dot_nt(h.astype(BF16), win_v[...])
        proj_ref[...] = proj
        u_lx, u_ly, u_b, u_c, u_v = (proj[:, k * W:(k + 1) * W] for k in range(5))

        ulx_ext[HALO:HALO + tb, :] = u_lx
        xl = bl_ref[...] + wl_ref[CONV_L - 1:CONV_L, :] * u_lx
        for k in range(CONV_L - 1):
            xl = xl + wl_ref[k:k + 1, :] * ulx_ext[pl.ds(HALO - (CONV_L - 1) + k, tb), :]
        ulx_ext[0:HALO, :] = ulx_ext[tb:tb + HALO, :]
        xlb = xl.astype(BF16)
        r = _sigmoid(_dot(xlb, bda_ref[...]) + ba_ref[...])
        ig = _sigmoid(_dot(xlb, bdx_ref[...]) + bxb_ref[...])
        log_a = (-C_GATE) * r * _softplus(ap_ref[...])
        a = jnp.exp(log_a)
        mult = jnp.sqrt(_one_minus_sq(a, log_a))
        grow = i * tb + lax.broadcasted_iota(jnp.int32, (tb, W), 0)
        mult = jnp.where(grow == 0, 1.0, mult)
        a_s[...] = a
        b_s[...] = mult * (ig * xl)
        _scan_groups(tb // HALO, a_s, b_s, hl_ref, hcar, reverse=False)
        hl = hl_ref[...]
        ge, _ = _gelu(u_ly)
        p = ge * hl
        rp = lax.rsqrt(_group_mean(p * p, avg_ref[...]) + EPS)
        y_lru = p * rp * gl_ref[...]

        cv = u_c * u_v
        cv_ext[HALO:HALO + tb, :] = cv
        cc = ws_ref[CONV_S - 1:CONV_S, :] * cv
        for k in range(CONV_S - 1):
            cc = cc + ws_ref[k:k + 1, :] * cv_ext[pl.ds(HALO - (CONV_S - 1) + k, tb), :]
        cv_ext[0:HALO, :] = cv_ext[tb:tb + HALO, :]
        q = u_b * cc
        rq = lax.rsqrt(_group_mean(q * q, avg_ref[...]) + EPS)
        y_conv = q * rq * gc_ref[...]
        for k, kept in enumerate((xl, r, ig, rp, rq, cc)):
            kept_ref[:, k * W:(k + 1) * W] = kept

        mixed_ref[...] = (_dot(y_lru.astype(BF16), wout_v[0:W, :]) + _dot(y_conv.astype(BF16), wout_v[W:2 * W, :]))

        @pl.when(i == nb - 1)
        def _():
            arrivals[1].wait_recv()
            passes[2].start()
            arrivals[2].wait_recv()
            passes[3].start()
            for j, p in enumerate((sibling,) + _route_peers(sibling)):
                to_sibling(j, p).wait_recv()
            for cp in sends + [forward] + passes:
                cp.wait_send()
            own.wait()

    tok = lambda cols: pl.BlockSpec((tb, cols), lambda i: (i, 0))
    full = lambda a: pl.BlockSpec(a.shape, lambda i: (0,) * a.ndim)
    small = (modraw, adab, g1, wl, bl, bda, bdx, ba, bxb, ap, ws, gl, gc, avg)
    n_chips = len(CHIP_FLIPS)
    return pl.pallas_call(
        body,
        name="mixer_fwd",
        grid=(nb,),
        in_specs=[tok(D)] + [full(a) for a in small] + [ANY, ANY],
        out_specs=[tok(D_IN), tok(W), tok(D), tok(N_KEPT * W), ANY],
        out_shape=[jax.ShapeDtypeStruct((t_len, D_IN), F32), jax.ShapeDtypeStruct((t_len, W), F32),
                   jax.ShapeDtypeStruct((t_len, D), F32), jax.ShapeDtypeStruct((t_len, N_KEPT * W), F32),
                   jax.ShapeDtypeStruct((N_DEV,) + mlp_block.shape, BF16)],
        scratch_shapes=[pltpu.VMEM((D_IN, D), BF16), pltpu.VMEM((D, D), BF16), pltpu.SemaphoreType.DMA((2,)),
                        pltpu.VMEM((tb + HALO, W), F32), pltpu.VMEM((tb + HALO, W), F32), pltpu.VMEM((HALO, W), F32),
                        pltpu.VMEM((tb, W), F32), pltpu.VMEM((tb, W), F32),
                        pltpu.SemaphoreType.DMA((n_chips,)), pltpu.SemaphoreType.DMA((n_chips,)),
                        pltpu.SemaphoreType.DMA((4,)), pltpu.SemaphoreType.DMA((4,)), pltpu.SemaphoreType.DMA],
        compiler_params=pltpu.CompilerParams(dimension_semantics=("arbitrary",), vmem_limit_bytes=VMEM_LIMIT),
    )(x, *small, wpack, mlp_block)


def _mlp_fwd(x, mixed, tgt, modraw, adab, g2, gf, wpack):
    t_len = x.shape[0]
    tb = TB_MLP
    nb = t_len // tb

    def body(x_ref, mixed_ref, tgt_ref, modraw_ref, adab_ref, g2_ref, gf_ref, wpack_hbm,
             h2t_ref, f_ref, dx2_ref, dz_ref, vec_ref, loss_ref, w1t_v, w2_v, sem):
        i = pl.program_id(0)

        @pl.when(i == 0)
        def _():
            cps = _load_packed(wpack_hbm, OFF_W1T, ROWS_W1T, w1t_v, sem.at[0])
            cps += _load_packed(wpack_hbm, OFF_W2, ROWS_W2, w2_v, sem.at[1])
            vec_ref[...] = jnp.zeros(vec_ref.shape, F32)
            loss_ref[...] = jnp.zeros(loss_ref.shape, F32)
            for cp in cps:
                cp.wait()

        mod = modraw_ref[...] + adab_ref[...]
        gate1, shift2, scale2, gate2 = mod[2:3], mod[3:4], mod[4:5], mod[5:6]
        x1 = x_ref[...] + gate1 * mixed_ref[...]
        r2 = lax.rsqrt(_rowmean(x1 * x1) + EPS)
        h2 = (x1 * r2 * g2_ref[...]) * (1.0 + scale2) + shift2
        h2b = h2.astype(BF16)
        h2t_ref[...] = h2.T.astype(BF16)
        z = jnp.zeros((tb, D), F32)
        for j in range(N_DEV):
            cols = slice(j * FF_BLK, (j + 1) * FF_BLK)
            fj = _dot_nt(h2b, w1t_v[cols, :])
            f_ref[:, cols] = fj
            rf = jnp.maximum(fj, 0.0)
            z = z + _dot((rf * rf).astype(BF16), w2_v[cols, :])
        x2 = x1 + gate2 * z
        r3 = lax.rsqrt(_rowmean(x2 * x2) + EPS)
        xn3 = x2 * r3
        diff = xn3 * gf_ref[...] - tgt_ref[...]
        sq = _colsum(diff * diff)
        loss_ref[...] += jnp.broadcast_to(jnp.sum(sq, axis=1, keepdims=True) * (0.5 / D), loss_ref.shape)
        dy = diff * (1.0 / D)
        dyn = dy * gf_ref[...]
        dx2 = r3 * (dyn - xn3 * _rowmean(dyn * xn3))
        dx2_ref[...] = dx2
        dz_ref[...] = (gate2 * dx2).astype(BF16)
        vec_ref[0:1, :] += _colsum(dx2 * z)
        vec_ref[1:2, :] += _colsum(dy * xn3)

    tok = lambda cols: pl.BlockSpec((tb, cols), lambda i: (i, 0))
    tok_t = pl.BlockSpec((D, tb), lambda i: (0, i))
    full = lambda a: pl.BlockSpec(a.shape, lambda i: (0,) * a.ndim)
    small = (modraw, adab, g2, gf)
    return pl.pallas_call(
        body,
        name="mlp_fwd",
        grid=(nb,),
        in_specs=[tok(D), tok(D), tok(D)] + [full(a) for a in small] + [ANY],
        out_specs=[tok_t, tok(D_FF), tok(D), tok(D), pl.BlockSpec((8, D), lambda i: (0, 0)),
                   pl.BlockSpec((8, 128), lambda i: (0, 0))],
        out_shape=[jax.ShapeDtypeStruct((D, t_len), BF16), jax.ShapeDtypeStruct((t_len, D_FF), F32),
                   jax.ShapeDtypeStruct((t_len, D), F32), jax.ShapeDtypeStruct((t_len, D), BF16),
                   jax.ShapeDtypeStruct((8, D), F32), jax.ShapeDtypeStruct((8, 128), F32)],
        scratch_shapes=[pltpu.VMEM((D_FF, D), BF16), pltpu.VMEM((D_FF, D), BF16), pltpu.SemaphoreType.DMA((2,))],
        compiler_params=pltpu.CompilerParams(dimension_semantics=("arbitrary",), vmem_limit_bytes=VMEM_LIMIT),
    )(x, mixed, tgt, *small, wpack)


def _mlp_bwd_half(pos, h2t, f, dz, wpack, prior=None):
    t_len = dz.shape[0]
    tb = TB_MLPB
    nb = t_len // tb
    first = prior is None
    flip = 1 if first else 0

    def body(pos_ref, h2t_ref, f_ref, dz_ref, w1t_ref, w2_ref, *rest):
        if first:
            dh2_ref, dw1_ref, dw2_ref = rest
        else:
            dh2in_ref, _, dh2_ref, dw1_ref, dw2_ref = rest
        k = pl.program_id(0)
        t = pl.program_id(1)
        rows = pl.ds(pl.multiple_of(t * tb, tb), tb)
        w1t = w1t_ref[0]
        w2 = w2_ref[0]
        dz = dz_ref[...]
        rf = jnp.maximum(f_ref[...], 0.0)
        df = (_dot_nt(dz, w2) * (2.0 * rf)).astype(BF16)
        dh = _dot(df, w1t)
        g1 = _dot(h2t_ref[...], df)
        g2 = _dot_tn((rf * rf).astype(BF16), dz)

        @pl.when(t == 0)
        def _():
            dw2_ref[0] = g2
            dw1_ref[0] = g1

        @pl.when(t != 0)
        def _():
            dw2_ref[0] += g2
            dw1_ref[0] += g1

        @pl.when(k == 0)
        def _():
            dh2_ref[rows, :] = dh if first else dh2in_ref[...] + dh

        @pl.when(k != 0)
        def _():
            dh2_ref[rows, :] += dh

    blk = lambda k, pos: 2 * k + jnp.bitwise_xor(pos[0], flip)
    in_specs = [pl.BlockSpec((D, tb), lambda k, t, pos: (0, t)),
                pl.BlockSpec((tb, FF_BLK), lambda k, t, pos: (t, blk(k, pos))),
                pl.BlockSpec((tb, D), lambda k, t, pos: (t, 0)),
                pl.BlockSpec((1, ROWS_W1T, D), lambda k, t, pos: (blk(k, pos), OFF_W1T // ROWS_W1T, 0)),
                pl.BlockSpec((1, ROWS_W2, D), lambda k, t, pos: (blk(k, pos), OFF_W2 // ROWS_W2, 0))]
    grad_specs = [pl.BlockSpec((1, D, FF_BLK), lambda k, t, pos: (k, 0, 0)),
                  pl.BlockSpec((1, FF_BLK, D), lambda k, t, pos: (k, 0, 0))]
    out_specs = [pl.BlockSpec((t_len, D), lambda k, t, pos: (0, 0))] + grad_specs
    grad_shapes = [jax.ShapeDtypeStruct((4, D, FF_BLK), F32), jax.ShapeDtypeStruct((4, FF_BLK, D), F32)]
    out_shape = [jax.ShapeDtypeStruct((t_len, D), F32)] + grad_shapes
    args = [pos, h2t, f, dz, wpack, wpack]
    if not first:
        in_specs += [pl.BlockSpec((tb, D), lambda k, t, pos: (jnp.where(k == 0, t, nb - 1), 0)),
                     pl.BlockSpec(prior[1].shape, lambda k, t, pos: (0,) * prior[1].ndim)]
        args += list(prior)
    return pl.pallas_call(
        body,
        name="mlp_bwd_first" if first else "mlp_bwd_second",
        grid_spec=pltpu.PrefetchScalarGridSpec(num_scalar_prefetch=1, grid=(4, nb), in_specs=in_specs,
                                               out_specs=out_specs),
        out_shape=out_shape,
        compiler_params=pltpu.CompilerParams(dimension_semantics=("arbitrary", "arbitrary"),
                                             vmem_limit_bytes=VMEM_LIMIT),
    )(*args)


V_SHIFT1, V_SCALE1, V_GATE1, V_SHIFT2, V_SCALE2, V_G1, V_G2 = 0, 1, 2, 3, 4, 6, 7
V_BL_BA, V_BX_SP, V_GL_GC, V_WL01, V_WL23, V_WS01, V_WS2 = 8, 9, 10, 11, 12, 13, 14
V_ROWS = 16


def _chip_scatter_copies(srcs, dsts, send_sems, recv_sems, row_ranges=None):
    me = _position()
    copies = []
    for a, (src, dst) in enumerate(zip(srcs, dsts)):
        band = pl.ds(*row_ranges[a]) if row_ranges else slice(None)
        for j, k in enumerate(CHIP_FLIPS):
            peer = _flip(me, k)
            copies.append(pltpu.make_async_remote_copy(
                src_ref=src.at[2 * peer[0] + peer[1], band], dst_ref=dst.at[j, band],
                send_sem=send_sems.at[len(CHIP_FLIPS) * a + j], recv_sem=recv_sems.at[len(CHIP_FLIPS) * a + j],
                device_id=peer, device_id_type=MESH))
    return copies


def _mixer_bwd(x, mixed, dh2, dx2, proj, hl, kept, modraw, adab, g1, g2, wl, bl, bda, bdx, ba, bxb, ap, ws, gl, gc, avg, wpack,
               chip_sums, chip_rows):
    t_len = x.shape[0]
    tb = TB_MIXB
    nb = t_len // tb
    hb = tb // HALO
    n_sums = len(chip_sums)

    def body(x_ref, mixed_ref, dh2_ref, dx2_ref, proj_ref, projh_ref, hl_ref, hlh_ref, kept_ref,
             modraw_ref, adab_ref, g1_ref, g2_ref, wl_ref, bl_ref, bda_ref, bdx_ref, ba_ref, bxb_ref, ap_ref,
             ws_ref, gl_ref, gc_ref, avg_ref, wpack_hbm, *rest):
        sums_hbm, rest = rest[:n_sums], rest[n_sums:]
        gx_ref, vec_ref, hb_ref, dprojt_ref, dmixed_ref, ycatt_ref, xlt_ref, dgate_ref = rest[:8]
        landed_hbm, rest = rest[8:8 + n_sums], rest[8 + n_sums:]
        (win_v, wout_v, sem, ulx_ext, cv_ext, hl_ext, a_ext, dxl_ext, dcc_ext, dcar, an_s, g_s, dh_s,
         send_sems, recv_sems) = rest
        i = pl.program_id(0)
        blk = nb - 1 - i
        chip_copies = _chip_scatter_copies(sums_hbm, landed_hbm, send_sems, recv_sems, chip_rows)

        @pl.when(i == 0)
        def _():
            for cp in chip_copies:
                cp.start()
            cps = _load_packed(wpack_hbm, OFF_WIN, ROWS_WIN, win_v, sem.at[0])
            cps += _load_packed(wpack_hbm, OFF_WOUT, ROWS_WOUT, wout_v, sem.at[1])
            vec_ref[...] = jnp.zeros(vec_ref.shape, F32)
            zero = jnp.zeros((HALO, W), F32)
            a_ext[tb:tb + HALO, :] = zero
            dxl_ext[tb:tb + HALO, :] = zero
            dcc_ext[tb:tb + HALO, :] = zero
            dcar[...] = zero
            for cp in cps:
                cp.wait()

        mod = modraw_ref[...] + adab_ref[...]
        shift1, scale1, gate1, scale2 = mod[0:1], mod[1:2], mod[2:3], mod[4:5]
        x = x_ref[...]
        mixed = mixed_ref[...]

        x1 = x + gate1 * mixed
        r2 = lax.rsqrt(_rowmean(x1 * x1) + EPS)
        xn2 = x1 * r2
        dh2 = dh2_ref[...]
        vec_ref[V_SHIFT2:V_SHIFT2 + 1, :] += _colsum(dh2)
        vec_ref[V_SCALE2:V_SCALE2 + 1, :] += _colsum(dh2 * xn2 * g2_ref[...])
        vec_ref[V_G2:V_G2 + 1, :] += _colsum(dh2 * (1.0 + scale2) * xn2)
        dxn2 = dh2 * g2_ref[...] * (1.0 + scale2)
        dx1 = dx2_ref[...] + r2 * (dxn2 - xn2 * _rowmean(dxn2 * xn2))
        vec_ref[V_GATE1:V_GATE1 + 1, :] += _colsum(dx1 * mixed)
        dmixed = (gate1 * dx1).astype(BF16)

        proj = proj_ref[...]
        u_lx, u_ly, u_b, u_c, u_v = (proj[:, k * W:(k + 1) * W] for k in range(5))
        has_prev = (blk > 0).astype(F32)
        projh = projh_ref[...]
        ulx_ext[0:HALO, :] = projh[:, 0:W] * has_prev
        ulx_ext[HALO:HALO + tb, :] = u_lx
        xl, r, ig, rp, rq, cc = (kept_ref[:, k * W:(k + 1) * W] for k in range(N_KEPT))
        sp = _softplus(ap_ref[...])
        log_a = (-C_GATE) * r * sp
        a = jnp.exp(log_a)
        mult_raw = jnp.sqrt(_one_minus_sq(a, log_a))
        first = (blk * tb + lax.broadcasted_iota(jnp.int32, (tb, W), 0)) == 0
        mult = jnp.where(first, 1.0, mult_raw)
        hl = hl_ref[...]
        ge, th = _gelu(u_ly)
        pn = ge * hl * rp
        cv = u_c * u_v
        cv_ext[0:HALO, :] = projh[:, 3 * W:4 * W] * projh[:, 4 * W:5 * W] * has_prev
        cv_ext[HALO:HALO + tb, :] = cv
        qn = u_b * cc * rq

        dmixed_ref[...] = dmixed
        ycatt_ref[0:W, :] = (pn * gl_ref[...]).T.astype(BF16)
        ycatt_ref[W:2 * W, :] = (qn * gc_ref[...]).T.astype(BF16)
        dyl = _dot_nt(dmixed, wout_v[0:W, :])
        dyc = _dot_nt(dmixed, wout_v[W:2 * W, :])

        dqn = dyc * gc_ref[...]
        dq = rq * (dqn - qn * _group_mean(dqn * qn, avg_ref[...]))
        du_b = dq * cc
        dcc = dq * u_b
        dcc_ext[0:tb, :] = dcc
        dcv = ws_ref[CONV_S - 1:CONV_S, :] * dcc
        for k in range(CONV_S - 1):
            dcv = dcv + ws_ref[k:k + 1, :] * dcc_ext[pl.ds(CONV_S - 1 - k, tb), :]
        dcc_ext[tb:tb + HALO, :] = dcc_ext[0:HALO, :]
        du_c = dcv * u_v
        du_v = dcv * u_c
        dws = [_colsum(dcc * cv_ext[pl.ds(HALO - (CONV_S - 1) + k, tb), :]) for k in range(CONV_S)]

        dpn = dyl * gl_ref[...]
        dp = rp * (dpn - pn * _group_mean(dpn * pn, avg_ref[...]))
        du_ly = dp * hl * _gelu_grad(u_ly, th)
        g_s[...] = dp * ge
        a_ext[0:tb, :] = a
        an_s[...] = a_ext[pl.ds(1, tb), :]
        _scan_groups(hb, an_s, g_s, dh_s, dcar, reverse=True)
        a_ext[tb:tb + HALO, :] = a_ext[0:HALO, :]
        dh = dh_s[...]
        hl_ext[0:HALO, :] = hlh_ref[...] * has_prev
        hl_ext[HALO:HALO + tb, :] = hl
        da = dh * hl_ext[pl.ds(HALO - 1, tb), :]
        dmult = dh * (ig * xl)
        dig = dh * (mult * xl)
        dxl = dh * (mult * ig)
        dlog = da * a - jnp.where(first, 0.0, dmult * (a * a) / mult_raw)
        dr = dlog * ((-C_GATE) * sp)
        dsp = _colsum(dlog * ((-C_GATE) * r))
        dga = dr * r * (1.0 - r)
        dgx = dig * ig * (1.0 - ig)
        dgab = dga.astype(BF16)
        dgxb = dgx.astype(BF16)
        xlt_ref[...] = xl.T.astype(BF16)
        dgate_ref[:, 0:W] = dgab
        dgate_ref[:, W:2 * W] = dgxb
        dxl = dxl + _dot_nt(dgab, bda_ref[...]) + _dot_nt(dgxb, bdx_ref[...])
        dxl_ext[0:tb, :] = dxl
        du_lx = wl_ref[CONV_L - 1:CONV_L, :] * dxl
        for k in range(CONV_L - 1):
            du_lx = du_lx + wl_ref[k:k + 1, :] * dxl_ext[pl.ds(CONV_L - 1 - k, tb), :]
        dxl_ext[tb:tb + HALO, :] = dxl_ext[0:HALO, :]
        dwl = [_colsum(dxl * ulx_ext[pl.ds(HALO - (CONV_L - 1) + k, tb), :]) for k in range(CONV_L)]

        cat = lambda u, v: jnp.concatenate([u, v], axis=1)
        vec_ref[V_BL_BA:V_BL_BA + 1, :] += cat(_colsum(dxl), _colsum(dga))
        vec_ref[V_BX_SP:V_BX_SP + 1, :] += cat(_colsum(dgx), dsp)
        vec_ref[V_GL_GC:V_GL_GC + 1, :] += cat(_colsum(dyl * pn), _colsum(dyc * qn))
        vec_ref[V_WL01:V_WL01 + 1, :] += cat(dwl[0], dwl[1])
        vec_ref[V_WL23:V_WL23 + 1, :] += cat(dwl[2], dwl[3])
        vec_ref[V_WS01:V_WS01 + 1, :] += cat(dws[0], dws[1])
        vec_ref[V_WS2:V_WS2 + 1, 0:W] += dws[2]

        r1 = lax.rsqrt(_rowmean(x * x) + EPS)
        xn1 = x * r1
        hb_ref[...] = ((xn1 * g1_ref[...]) * (1.0 + scale1) + shift1).astype(BF16)
        dh_in = jnp.zeros((tb, D), F32)
        for k, du in enumerate((du_lx, du_ly, du_b, du_c, du_v)):
            dprojt_ref[k * W:(k + 1) * W, :] = du.T.astype(BF16)
            dh_in = dh_in + _dot(du.astype(BF16), win_v[k * W:(k + 1) * W, :])
        vec_ref[V_SHIFT1:V_SHIFT1 + 1, :] += _colsum(dh_in)
        vec_ref[V_SCALE1:V_SCALE1 + 1, :] += _colsum(dh_in * xn1 * g1_ref[...])
        vec_ref[V_G1:V_G1 + 1, :] += _colsum(dh_in * (1.0 + scale1) * xn1)
        dxn1 = dh_in * g1_ref[...] * (1.0 + scale1)
        gx_ref[...] = dx1 + r1 * (dxn1 - xn1 * _rowmean(dxn1 * xn1))

        @pl.when(i == nb - 1)
        def _():
            for cp in chip_copies:
                cp.wait_recv()
            for cp in chip_copies:
                cp.wait_send()

    rev = lambda cols: pl.BlockSpec((tb, cols), lambda i: (nb - 1 - i, 0))
    rev_t = lambda rows: pl.BlockSpec((rows, tb), lambda i: (0, nb - 1 - i))
    halo = lambda cols: pl.BlockSpec((HALO, cols), lambda i: (jnp.maximum((nb - 1 - i) * hb - 1, 0), 0))
    full = lambda a: pl.BlockSpec(a.shape, lambda i: (0,) * a.ndim)
    small = (modraw, adab, g1, g2, wl, bl, bda, bdx, ba, bxb, ap, ws, gl, gc, avg)
    ext = pltpu.VMEM((tb + HALO, W), F32)
    n_sems = max(len(CHIP_FLIPS) * n_sums, 1)
    return pl.pallas_call(
        body,
        name="mixer_bwd",
        grid=(nb,),
        in_specs=[rev(D), rev(D), rev(D), rev(D), rev(D_IN), halo(D_IN), rev(W), halo(W), rev(N_KEPT * W)]
        + [full(a) for a in small] + [ANY] * (1 + n_sums),
        out_specs=[rev(D), pl.BlockSpec((V_ROWS, D), lambda i: (0, 0)), rev(D), rev_t(D_IN), rev(D), rev_t(D),
                   rev_t(W), rev(2 * W)] + [ANY] * n_sums,
        out_shape=[jax.ShapeDtypeStruct((t_len, D), F32), jax.ShapeDtypeStruct((V_ROWS, D), F32),
                   jax.ShapeDtypeStruct((t_len, D), BF16), jax.ShapeDtypeStruct((D_IN, t_len), BF16),
                   jax.ShapeDtypeStruct((t_len, D), BF16), jax.ShapeDtypeStruct((D, t_len), BF16),
                   jax.ShapeDtypeStruct((W, t_len), BF16), jax.ShapeDtypeStruct((t_len, 2 * W), BF16)]
        + [jax.ShapeDtypeStruct((len(CHIP_FLIPS),) + s.shape[1:], s.dtype) for s in chip_sums],
        scratch_shapes=[pltpu.VMEM((D_IN, D), BF16), pltpu.VMEM((D, D), BF16), pltpu.SemaphoreType.DMA((2,)),
                        ext, ext, ext, ext, ext, ext, pltpu.VMEM((HALO, W), F32),
                        pltpu.VMEM((tb, W), F32), pltpu.VMEM((tb, W), F32), pltpu.VMEM((tb, W), F32),
                        pltpu.SemaphoreType.DMA((n_sems,)), pltpu.SemaphoreType.DMA((n_sems,))],
        compiler_params=pltpu.CompilerParams(dimension_semantics=("arbitrary",), vmem_limit_bytes=VMEM_LIMIT),
    )(x, mixed, dh2, dx2, proj, proj, hl, hl, kept, *small, wpack, *chip_sums)


def _matmul(name, a, b, tm=512):
    m, k = a.shape
    n = b.shape[1]

    def body(a_ref, b_ref, o_ref):
        o_ref[...] = _dot(a_ref[...], b_ref[...])

    return pl.pallas_call(
        body,
        name=name,
        grid=(m // tm,),
        in_specs=[pl.BlockSpec((tm, k), lambda i: (i, 0)), pl.BlockSpec((k, n), lambda i: (0, 0))],
        out_specs=pl.BlockSpec((tm, n), lambda i: (i, 0)),
        out_shape=jax.ShapeDtypeStruct((m, n), F32),
        compiler_params=pltpu.CompilerParams(dimension_semantics=("arbitrary",), vmem_limit_bytes=VMEM_LIMIT),
    )(a, b)


def _gate_wgrad(xl_t, dgate, avg):
    hd = W // 8

    def body(a_ref, b_ref, avg_ref, o_ref):
        full = _dot(a_ref[...], b_ref[...])
        row = lax.broadcasted_iota(jnp.int32, (W, hd), 0)
        col = lax.broadcasted_iota(jnp.int32, (W, hd), 1)
        fold = ((row & (hd - 1)) == col).astype(BF16)
        keep = avg_ref[...] != 0
        for g in range(2):
            m = jnp.where(keep, full[:, g * W:(g + 1) * W], 0.0)
            hi = m.astype(BF16)
            rest = m - hi.astype(F32)
            mid = rest.astype(BF16)
            lo = (rest - mid.astype(F32)).astype(BF16)
            o_ref[g] = _dot(hi, fold) + _dot(mid, fold) + _dot(lo, fold)

    return pl.pallas_call(
        body,
        name="wgrad_gate",
        in_specs=[WHOLE] * 3,
        out_specs=WHOLE,
        out_shape=jax.ShapeDtypeStruct((2, W, hd), F32),
        compiler_params=pltpu.CompilerParams(vmem_limit_bytes=VMEM_LIMIT),
    )(xl_t, dgate, avg)


def _block_diag(w):
    n, m, _ = w.shape
    eye = jnp.eye(n, dtype=w.dtype)
    return (w[:, :, None, :] * eye[:, None, :, None]).reshape(n * m, n * m)


def _pad_rows(a, rows):
    return jnp.pad(a, ((0, rows - a.shape[0]),) + ((0, 0),) * (a.ndim - 1))


def _position():
    return lax.axis_index("x"), lax.axis_index("y"), lax.axis_index("c")


def _linear(pos):
    return 4 * pos[0] + 2 * pos[1] + pos[2]


def _flip(pos, k):
    return tuple(1 - p if k & bit else p for p, bit in zip(pos, (4, 2, 1)))


def _exchange_all(make_copy, make_arrival):
    copies = [make_copy(k) for k in range(1, N_DEV)]
    for cp in copies:
        cp.start()
    for k in range(1, N_DEV):
        make_arrival(k).wait_recv()
    for cp in copies:
        cp.wait_send()


def _mod_exchange_steps(cols):
    def steps(msg_ref, adaw_ref, gath_ref, mod_ref, sendbuf, send_a, recv_a, send_b, recv_b):
        me = _position()
        me_lin = _linear(me)
        m = msg_ref[...]
        row = lax.broadcasted_iota(jnp.int32, m.shape, 0)
        gath_ref[me_lin] = jnp.where(row == 0, m * _sigmoid(m), m)

        def gather_copy(k, src_lin):
            return pltpu.make_async_remote_copy(
                src_ref=gath_ref.at[src_lin], dst_ref=gath_ref.at[src_lin], send_sem=send_a.at[k - 1],
                recv_sem=recv_a.at[k - 1], device_id=_flip(me, k), device_id_type=MESH)

        _exchange_all(lambda k: gather_copy(k, me_lin), lambda k: gather_copy(k, _linear(_flip(me, k))))

        sc_all = gath_ref[:, 0, :]
        scb = jnp.concatenate([sc_all, jnp.zeros_like(sc_all)], axis=0).astype(BF16)
        prod = _dot(scb, adaw_ref[...].astype(BF16))
        for b in range(N_DEV):
            sendbuf[b] = jnp.broadcast_to(prod[b:b + 1, :], (HALO, cols))
        mod_ref[me_lin] = sendbuf[me_lin]

        def row_copy(k, dst_lin):
            peer = _flip(me, k)
            return pltpu.make_async_remote_copy(
                src_ref=sendbuf.at[_linear(peer)], dst_ref=mod_ref.at[dst_lin], send_sem=send_b.at[k - 1],
                recv_sem=recv_b.at[k - 1], device_id=peer, device_id_type=MESH)

        _exchange_all(lambda k: row_copy(k, me_lin), lambda k: row_copy(k, _linear(_flip(me, k))))

    return steps


def _gather_and_mod(msg, ada_w, block):
    rows, cols = block.shape
    mod_cols = ada_w.shape[1]
    mod_steps = _mod_exchange_steps(mod_cols)

    def body(msg_ref, adaw_ref, x_ref, gath_ref, mod_ref, out_ref, sendbuf, send_a, recv_a, send_b, recv_b,
             send_sems, recv_sems, sib_send_sems, sib_recv_sems, local_sem):
        x, y, c = _position()
        me, sibling = (x, y, c), (x, y, 1 - c)
        sends, forward, arrivals = _chip_gather_copies(x_ref, out_ref, send_sems, recv_sems)

        def to_sibling(j, block_of, src=None):
            dst = out_ref.at[_linear(block_of)]
            return pltpu.make_async_remote_copy(
                src_ref=dst if src is None else src, dst_ref=dst, send_sem=sib_send_sems.at[j],
                recv_sem=sib_recv_sems.at[j], device_id=sibling, device_id_type=MESH)

        mine = pltpu.make_async_copy(x_ref, out_ref.at[_linear(me)], local_sem)
        mine.start()
        passes = [to_sibling(0, me, src=x_ref)] + [to_sibling(1 + j, p) for j, p in enumerate(_route_peers(me))]
        passes[0].start()
        for cp in sends:
            cp.start()
        mod_steps(msg_ref, adaw_ref, gath_ref, mod_ref, sendbuf, send_a, recv_a, send_b, recv_b)
        arrivals[0].wait_recv()
        forward.start()
        passes[1].start()
        arrivals[1].wait_recv()
        passes[2].start()
        arrivals[2].wait_recv()
        passes[3].start()
        for j, p in enumerate((sibling,) + _route_peers(sibling)):
            to_sibling(j, p).wait_recv()
        for cp in sends + [forward] + passes:
            cp.wait_send()
        mine.wait()

    return pl.pallas_call(
        body,
        name="gather_and_mod",
        in_specs=[WHOLE, WHOLE, ANY],
        out_specs=[WHOLE, WHOLE, ANY],
        out_shape=[jax.ShapeDtypeStruct((N_DEV, HALO, D), F32), jax.ShapeDtypeStruct((N_DEV, HALO, mod_cols), F32),
                   jax.ShapeDtypeStruct((N_DEV, rows, cols), block.dtype)],
        scratch_shapes=[pltpu.VMEM((N_DEV, HALO, mod_cols), F32)] + [pltpu.SemaphoreType.DMA((N_DEV - 1,))] * 4
        + [pltpu.SemaphoreType.DMA((3,)), pltpu.SemaphoreType.DMA((3,)), pltpu.SemaphoreType.DMA((4,)),
           pltpu.SemaphoreType.DMA((4,)), pltpu.SemaphoreType.DMA],
        compiler_params=pltpu.CompilerParams(vmem_limit_bytes=VMEM_LIMIT),
    )(msg, ada_w, block)


HBM = pl.BlockSpec(memory_space=pltpu.HBM)
SEM = pl.BlockSpec(memory_space=pltpu.SEMAPHORE)
EFFECT = pltpu.SideEffectType.DATAFLOW_SIDE_EFFECTING


def _stage_copies(stage):
    return {"chips": (_chip_scatter_copies, len(CHIP_FLIPS), len(CHIP_FLIPS)), "sibling": (_sibling_copies, 4, 4)}[stage]


def _chips_start(which, chip_sums, stage="chips"):
    n = len(chip_sums)
    make_copies, per_array, slots = _stage_copies(stage)
    n_sems = per_array * n

    def body(*refs):
        srcs, dsts = refs[:n], refs[n:2 * n]
        send_sems, recv_sems = refs[2 * n:2 * n + 2]
        token = refs[-1]
        for cp in make_copies(srcs, dsts, send_sems, recv_sems):
            cp.start()
        token[...] = jnp.zeros(token.shape, token.dtype)

    landing = [jax.ShapeDtypeStruct((slots,) + s.shape[-2:], s.dtype) for s in chip_sums]
    outs = pl.pallas_call(
        body,
        name=which + "_" + stage + "_start",
        in_specs=[HBM] * (2 * n),
        out_specs=[SEM, SEM] + [HBM] * (2 * n) + [WHOLE],
        out_shape=[pltpu.SemaphoreType.DMA((n_sems,)), pltpu.SemaphoreType.DMA((n_sems,))]
        + [pltpu.HBM(s.shape, s.dtype) for s in chip_sums] + [pltpu.HBM(s.shape, s.dtype) for s in landing]
        + [jax.ShapeDtypeStruct((HALO, 128), F32)],
        input_output_aliases={i: 2 + i for i in range(2 * n)},
        compiler_params=pltpu.CompilerParams(has_side_effects=EFFECT),
    )(*[pltpu.with_memory_space_constraint(s, pltpu.HBM) for s in chip_sums],
      *[pltpu.with_memory_space_constraint(lax.empty(s.shape, s.dtype), pltpu.HBM) for s in landing])
    return outs[0], outs[1], outs[2:2 + n], outs[2 + n:2 + 2 * n], outs[-1]


def _chips_wait(which, send_sems, recv_sems, srcs, landed, after, stage="chips"):
    n = len(srcs)
    make_copies = _stage_copies(stage)[0]

    def body(*refs):
        src_refs, dst_refs = refs[:n], refs[n:2 * n]
        sends, recvs = refs[2 * n:2 * n + 2]
        copies = make_copies(src_refs, dst_refs, sends, recvs)
        for cp in copies:
            cp.wait_send()
        for cp in copies:
            cp.wait_recv()

    outs = pl.pallas_call(
        body,
        name=which + "_" + stage + "_wait",
        in_specs=[HBM] * (2 * n) + [SEM, SEM, ANY],
        out_specs=[HBM] * (2 * n),
        out_shape=[pltpu.HBM(s.shape, s.dtype) for s in list(srcs) + list(landed)],
        input_output_aliases={i: i for i in range(2 * n)},
        compiler_params=pltpu.CompilerParams(has_side_effects=EFFECT),
    )(*srcs, *landed, send_sems, recv_sems, after)
    return list(outs[:n]), list(outs[n:])


def _sibling_copies(srcs, dsts, send_sems, recv_sems):
    x, y, c = _position()
    copies = []
    for a, (src, dst) in enumerate(zip(srcs, dsts)):
        for k in range(4):
            copies.append(pltpu.make_async_remote_copy(
                src_ref=src.at[k, 1 - c] if len(src.shape) == 4 else src.at[k], dst_ref=dst.at[k],
                send_sem=send_sems.at[4 * a + k],
                recv_sem=recv_sems.at[4 * a + k], device_id=(x, y, 1 - c), device_id_type=MESH))
    return copies


def _row_block(rows):
    return min(rows, 512)


def _pair_sum(pos, mine, recv):
    _, cores, rows, cols = mine.shape
    rb = _row_block(rows)

    def body(pos_ref, mine_ref, recv_ref, out_ref):
        out_ref[0] = (mine_ref[0, 0] + recv_ref[0]).astype(BF16)

    other = lambda k, pos: jnp.bitwise_xor(pos[1], k + 1)
    core = lambda pos: pos[0] * (cores - 1)
    return pl.pallas_call(
        body,
        name="grad_pair_sum",
        grid_spec=pltpu.PrefetchScalarGridSpec(
            num_scalar_prefetch=1, grid=(3, rows // rb),
            in_specs=[pl.BlockSpec((1, 1, rb, cols), lambda k, r, pos: (other(k, pos), core(pos), r, 0)),
                      pl.BlockSpec((1, rb, cols), lambda k, r, pos: (other(k, pos), r, 0))],
            out_specs=pl.BlockSpec((1, rb, cols), lambda k, r, pos: (other(k, pos), r, 0))),
        out_shape=jax.ShapeDtypeStruct((4, rows, cols), BF16),
        compiler_params=pltpu.CompilerParams(dimension_semantics=("arbitrary", "arbitrary")),
    )(pos, mine, recv)


def _final_sum(pos, mine, recv, chips):
    _, cores, rows, cols = mine.shape
    rb = _row_block(rows)

    def body(pos_ref, mine_ref, recv_ref, chips_ref, out_ref):
        g = mine_ref[0, 0] + recv_ref[0]
        for j in range(3):
            g = g + chips_ref[j].astype(F32)
        out_ref[...] = g

    return pl.pallas_call(
        body,
        name="grad_final_sum",
        grid_spec=pltpu.PrefetchScalarGridSpec(
            num_scalar_prefetch=1, grid=(rows // rb,),
            in_specs=[pl.BlockSpec((1, 1, rb, cols), lambda r, pos: (pos[1], pos[0] * (cores - 1), r, 0)),
                      pl.BlockSpec((1, rb, cols), lambda r, pos: (pos[1], r, 0)),
                      pl.BlockSpec((3, rb, cols), lambda r, pos: (0, r, 0))],
            out_specs=pl.BlockSpec((rb, cols), lambda r, pos: (r, 0))),
        out_shape=jax.ShapeDtypeStruct((rows, cols), F32),
        compiler_params=pltpu.CompilerParams(dimension_semantics=("arbitrary",)),
    )(pos, mine, recv, chips)


LOSS_ROW = V_ROWS + 8
GB_BASE = LOSS_ROW + 8


def _route_mod_grad_steps(cols):
    def steps(gmod_ref, sct_ref, gadaw_ref, sendbuf, grecv, send_a, recv_a):
        me = _position()
        me_lin = _linear(me)
        gm = gmod_ref[...]
        for b in range(N_DEV):
            sendbuf[b] = jnp.broadcast_to(gm[b:b + 1, :], (HALO, cols))
        grecv[me_lin] = sendbuf[me_lin]

        def row_copy(k, dst_lin):
            peer = _flip(me, k)
            return pltpu.make_async_remote_copy(
                src_ref=sendbuf.at[_linear(peer)], dst_ref=grecv.at[dst_lin], send_sem=send_a.at[k - 1],
                recv_sem=recv_a.at[k - 1], device_id=peer, device_id_type=MESH)

        _exchange_all(lambda k: row_copy(k, me_lin), lambda k: row_copy(k, _linear(_flip(me, k))))
        g_all = grecv[:, 0, :]
        g_pad = jnp.concatenate([g_all, jnp.zeros((sct_ref.shape[1] - N_DEV, cols), F32)], axis=0).astype(BF16)
        gadaw_ref[...] = _dot(sct_ref[...], g_pad)
        return _colsum(g_all)

    return steps


def _small_grad_exchange(gmod8, sc_t, msg_vec, msg_gate, after):
    cols = gmod8.shape[1]
    vec_rows = GB_BASE + N_DEV
    route_steps = _route_mod_grad_steps(cols)

    def body(gmod_ref, sct_ref, vec_ref, gate_ref, after_ref, gadaw_ref, sumv_ref, sumg_ref,
             sendbuf, grecv, send_a, recv_a, myv, myg, sibv, sibg, chipv, chipg, sib_send, sib_recv, peer_send, peer_recv):
        gb = route_steps(gmod_ref, sct_ref, gadaw_ref, sendbuf, grecv, send_a, recv_a)
        x, y, c = me = _position()
        my_chip = 2 * x + y
        myv[0:GB_BASE, :] = vec_ref[...]
        slot = lax.broadcasted_iota(jnp.int32, (N_DEV, D), 0) == _linear(me)
        gb_wide = jnp.concatenate([jnp.broadcast_to(gb, (N_DEV, cols)), jnp.zeros((N_DEV, D - cols), F32)], axis=1)
        myv[GB_BASE:vec_rows, :] = jnp.where(slot, gb_wide, 0.0)
        myg[...] = gate_ref[...]

        swaps = [pltpu.make_async_remote_copy(
            src_ref=src, dst_ref=dst, send_sem=sib_send.at[a], recv_sem=sib_recv.at[a], device_id=(x, y, 1 - c),
            device_id_type=MESH) for a, (src, dst) in enumerate(((myv, sibv), (myg, sibg)))]
        for cp in swaps:
            cp.start()
        for cp in swaps:
            cp.wait_recv()
        chipv[my_chip] = myv[...] + sibv[...]
        chipg[my_chip] = myg[...] + sibg[...]

        def chip_copy(a, buf, j, k, slot_chip):
            peer = _flip(me, k)
            return pltpu.make_async_remote_copy(
                src_ref=buf.at[slot_chip], dst_ref=buf.at[slot_chip], send_sem=peer_send.at[3 * a + j],
                recv_sem=peer_recv.at[3 * a + j], device_id=peer, device_id_type=MESH)

        sends = [chip_copy(a, buf, j, k, my_chip) for a, buf in enumerate((chipv, chipg)) for j, k in enumerate(CHIP_FLIPS)]
        for cp in sends:
            cp.start()
        for a, buf in enumerate((chipv, chipg)):
            for j, k in enumerate(CHIP_FLIPS):
                peer = _flip(me, k)
                chip_copy(a, buf, j, k, 2 * peer[0] + peer[1]).wait_recv()
        sumv_ref[...] = ((chipv[0] + chipv[1]) + chipv[2]) + chipv[3]
        sumg_ref[...] = ((chipg[0] + chipg[1]) + chipg[2]) + chipg[3]
        for cp in swaps + sends:
            cp.wait_send()

    vshape, gshape = (vec_rows, D), msg_gate.shape
    return pl.pallas_call(
        body,
        name="small_grad_exchange",
        in_specs=[WHOLE] * 5,
        out_specs=[WHOLE] * 3,
        out_shape=[jax.ShapeDtypeStruct((D, cols), F32), jax.ShapeDtypeStruct(vshape, F32),
                   jax.ShapeDtypeStruct(gshape, F32)],
        scratch_shapes=[pltpu.VMEM((N_DEV, HALO, cols), F32), pltpu.VMEM((N_DEV, HALO, cols), F32),
                        pltpu.SemaphoreType.DMA((N_DEV - 1,)), pltpu.SemaphoreType.DMA((N_DEV - 1,)),
                        pltpu.VMEM(vshape, F32), pltpu.VMEM(gshape, F32), pltpu.VMEM(vshape, F32),
                        pltpu.VMEM(gshape, F32), pltpu.VMEM((4,) + vshape, F32), pltpu.VMEM((4,) + gshape, F32),
                        pltpu.SemaphoreType.DMA((2,)), pltpu.SemaphoreType.DMA((2,)),
                        pltpu.SemaphoreType.DMA((2 * len(CHIP_FLIPS),)), pltpu.SemaphoreType.DMA((2 * len(CHIP_FLIPS),))],
        compiler_params=pltpu.CompilerParams(vmem_limit_bytes=VMEM_LIMIT),
    )(gmod8, sc_t, msg_vec, msg_gate, after)


def _adamw_math(w, g, m, v):
    m = ADAM_B1 * m + (1.0 - ADAM_B1) * g
    v = ADAM_B2 * v + (1.0 - ADAM_B2) * (g * g)
    m_hat = m / (1.0 - ADAM_B1 ** ADAM_STEP)
    v_hat = v / (1.0 - ADAM_B2 ** ADAM_STEP)
    delta = -ADAM_LR * (m_hat / (jnp.sqrt(v_hat) + ADAM_EPS) + ADAM_WD * w)
    return delta, m, v


def _adamw(name, w, g, m, v):
    rows, cols = w.shape
    rb = 256 if rows % 256 == 0 else rows

    def body(w_ref, g_ref, m_ref, v_ref, d_ref, mo_ref, vo_ref):
        d_ref[...], mo_ref[...], vo_ref[...] = _adamw_math(w_ref[...], g_ref[...], m_ref[...], v_ref[...])

    spec = pl.BlockSpec((rb, cols), lambda r: (r, 0))
    return pl.pallas_call(
        body,
        name="adamw_" + name,
        grid=(rows // rb,),
        in_specs=[spec] * 4,
        out_specs=[spec] * 3,
        out_shape=[jax.ShapeDtypeStruct((rows, cols), F32)] * 3,
        compiler_params=pltpu.CompilerParams(dimension_semantics=("arbitrary",)),
    )(w, g, m, v)


def _update(pos, sum_jobs, plain_jobs, after):
    rb = 256
    jobs = [("sum", j) for j in sum_jobs] + [("plain", j) for j in plain_jobs]
    offs, total = [], 0
    for _, j in jobs:
        offs.append(total)
        total += j[-1].shape[0] // rb
    n_in = sum(len(j) for _, j in jobs)

    def body(pos_ref, *refs):
        ins, outs = refs[:n_in], refs[n_in + 1:]
        s = pl.program_id(0)
        i_in = i_out = 0
        for (kind, j), off in zip(jobs, offs):
            steps = j[-1].shape[0] // rb
            j_in = ins[i_in:i_in + len(j)]
            i_in += len(j)
            j_out = outs[i_out:i_out + (4 if kind == "sum" else 3)]
            i_out += len(j_out)

            @pl.when((s >= off) & (s < off + steps))
            def _(kind=kind, j_in=j_in, j_out=j_out):
                if kind == "sum":
                    mine_ref, recv_ref, chips_ref, w_ref, m_ref, v_ref = j_in
                    g = mine_ref[0, 0] + recv_ref[0]
                    for q in range(len(CHIP_FLIPS)):
                        g = g + chips_ref[q].astype(F32)
                    j_out[0][...] = g
                    rest = j_out[1:]
                else:
                    g_ref, w_ref, m_ref, v_ref = j_in
                    g = g_ref[...]
                    rest = j_out
                rest[0][...], rest[1][...], rest[2][...] = _adamw_math(w_ref[...], g, m_ref[...], v_ref[...])

    in_specs, out_specs, out_shape, args = [], [], [], []
    for (kind, j), off in zip(jobs, offs):
        rows, cols = j[-1].shape
        steps = rows // rb
        blk = lambda s, off=off, steps=steps: jnp.clip(s - off, 0, steps - 1)
        flat = pl.BlockSpec((rb, cols), lambda s, pos, blk=blk: (blk(s), 0))
        if kind == "sum":
            in_specs += [pl.BlockSpec((1, 1, rb, cols), lambda s, pos, blk=blk: (pos[1], 0, blk(s), 0)),
                         pl.BlockSpec((1, rb, cols), lambda s, pos, blk=blk: (pos[1], blk(s), 0)),
                         pl.BlockSpec((len(CHIP_FLIPS), rb, cols), lambda s, pos, blk=blk: (0, blk(s), 0))]
            in_specs += [flat] * 3
        else:
            in_specs += [flat] * 4
        n_res = 4 if kind == "sum" else 3
        out_specs += [flat] * n_res
        out_shape += [jax.ShapeDtypeStruct((rows, cols), F32)] * n_res
        args += list(j)
    in_specs += [pl.BlockSpec(after.shape, lambda s, pos: (0,) * after.ndim)]
    outs = pl.pallas_call(
        body,
        name="update",
        grid_spec=pltpu.PrefetchScalarGridSpec(
            num_scalar_prefetch=1, grid=(total,), in_specs=in_specs, out_specs=out_specs),
        out_shape=out_shape,
        compiler_params=pltpu.CompilerParams(dimension_semantics=("arbitrary",), vmem_limit_bytes=VMEM_LIMIT),
    )(pos, *args, after)
    sums = [tuple(outs[4 * i:4 * i + 4]) for i in range(len(sum_jobs))]
    base = 4 * len(sum_jobs)
    plains = [tuple(outs[base + 3 * i:base + 3 * i + 3]) for i in range(len(plain_jobs))]
    return sums, plains


def _adamw_small(ws, gs, ms, vs, sigmoid_scaled):
    n = len(ws)

    def body(*refs):
        w_refs, g_refs, m_refs, v_refs = (refs[i * n:(i + 1) * n] for i in range(4))
        outs = refs[4 * n:]
        for i in range(n):
            w = w_refs[i][...]
            g = g_refs[i][...]
            if sigmoid_scaled[i]:
                g = g * _sigmoid(w)
            delta, m, v = _adamw_math(w, g, m_refs[i][...], v_refs[i][...])
            outs[4 * i][...] = g
            outs[4 * i + 1][...] = delta
            outs[4 * i + 2][...] = m
            outs[4 * i + 3][...] = v

    shapes = [jax.ShapeDtypeStruct(w.shape, F32) for w in ws for _ in range(4)]
    outs = pl.pallas_call(
        body,
        name="adamw_small",
        in_specs=[WHOLE] * (4 * n),
        out_specs=[WHOLE] * (4 * n),
        out_shape=shapes,
    )(*ws, *gs, *ms, *vs)
    return [outs[4 * i:4 * i + 4] for i in range(n)]


_WEIGHT_NAMES = ("ada_w", "ada_b", "norm1_g", "w_in", "lru_conv_w", "lru_conv_b", "gate_a_w", "gate_a_b", "gate_x_w",
                 "gate_x_b", "a_param", "short_conv_w", "lru_out_g", "conv_out_g", "w_out", "norm2_g", "w_mlp1",
                 "w_mlp2", "final_g")


def kernel(x, c, ada_w, ada_b, norm1_g, w_in, lru_conv_w, lru_conv_b, gate_a_w, gate_a_b, gate_x_w, gate_x_b, a_param, short_conv_w, lru_out_g, conv_out_g, w_out, norm2_g, w_mlp1, w_mlp2, final_g, loss_target, m_ada_w, m_ada_b, m_norm1_g, m_w_in, m_lru_conv_w, m_lru_conv_b, m_gate_a_w, m_gate_a_b, m_gate_x_w, m_gate_x_b, m_a_param, m_short_conv_w, m_lru_out_g, m_conv_out_g, m_w_out, m_norm2_g, m_w_mlp1, m_w_mlp2, m_final_g, v_ada_w, v_ada_b, v_norm1_g, v_w_in, v_lru_conv_w, v_lru_conv_b, v_gate_a_w, v_gate_a_b, v_gate_x_w, v_gate_x_b, v_a_param, v_short_conv_w, v_lru_out_g, v_conv_out_g, v_w_out, v_norm2_g, v_w_mlp1, v_w_mlp2, v_final_g):
    given = dict(locals())
    weights = {n: given[n] for n in _WEIGHT_NAMES}
    xi, yi, ci = _position()
    me_lin = _linear((xi, yi, ci))
    hd = W // N_DEV

    mixer_block = jnp.concatenate([w_out[0], w_in[0].T], axis=0).astype(BF16)
    mlp_block = jnp.concatenate([w_mlp1[0].T, w_mlp2[0]], axis=0).astype(BF16)

    msg = (jnp.pad(c, ((0, HALO - 1), (0, 0)))
           + jnp.pad(lru_conv_w[0], ((1, HALO - 1 - CONV_L), (0, D - hd)))
           + jnp.pad(short_conv_w[0], ((1 + CONV_L, 0), (0, D - hd))))
    gath, mod_all, wmix = _gather_and_mod(msg, ada_w[0], mixer_block)
    sc_all = gath[:, 0, :]
    wl = jnp.transpose(gath[:, 1:1 + CONV_L, :hd], (1, 0, 2)).reshape(CONV_L, W)
    ws = jnp.transpose(gath[:, 1 + CONV_L:HALO, :hd], (1, 0, 2)).reshape(CONV_S, W)
    modraw = _pad_rows(mod_all[:, 0, :].reshape(6, D), HALO)
    adab = _pad_rows(ada_b.reshape(6, D), HALO)

    x2d, tgt = x[0], loss_target[0]
    gf = final_g.reshape(1, D)
    bda = _block_diag(gate_a_w[0]).astype(BF16)
    bdx = _block_diag(gate_x_w[0]).astype(BF16)
    avg = _block_diag(jnp.full((8, W // 8, W // 8), 8.0 / W, F32)).astype(BF16)
    wl8 = _pad_rows(wl, HALO)
    ws8 = _pad_rows(ws, HALO)
    mixer_small = (wl8, lru_conv_b, bda, bdx, gate_a_b, gate_x_b, a_param, ws8, lru_out_g, conv_out_g, avg)
    proj, hl, mixed, kept, wmlp = _mixer_fwd(x2d, modraw, adab, norm1_g, *mixer_small, wmix, mlp_block)
    h2t, f, dx2, dz, vec2, loss8 = _mlp_fwd(x2d, mixed, tgt, modraw, adab, norm2_g, gf, wmlp)
    pos = jnp.stack([ci, 2 * xi + yi]).astype(jnp.int32)
    by_dest = lambda g: g.reshape((4, 2, -1) + g.shape[-1:])
    dh2_first, *for_sibling = _mlp_bwd_half(pos, h2t, f, dz, wmlp)
    sib_send, sib_recv, sib_thru, sib_land, token = _chips_start("mlp", for_sibling, stage="sibling")
    dh2, dw1, dw2 = _mlp_bwd_half(pos, h2t, f, dz, wmlp, prior=(dh2_first, token))
    done = dh2[0:HALO, 0:128] + dw1[0, 0:HALO, 0:128] + dw2[0, 0:HALO, 0:128]
    _, mlp_sib = _chips_wait("mlp", sib_send, sib_recv, sib_thru, sib_land, done, stage="sibling")
    mlp_parts = [dw1[:, None], dw2[:, None]]
    mlp_sums = [_pair_sum(pos, p, r) for p, r in zip(mlp_parts, mlp_sib)]
    mlp_send, mlp_recv, mlp_thru, mlp_land, token = _chips_start("mlp", mlp_sums)
    modraw_after = modraw + jnp.tile(token, (1, D // token.shape[1]))
    gx, vec, hb, dproj_t, dmixed, ycat_t, xl_t, dgate = _mixer_bwd(
        x2d, mixed, dh2, dx2, proj, hl, kept, modraw_after, adab, norm1_g, norm2_g, *mixer_small, wmix, [], None)
    dwint = _matmul("wgrad_in", dproj_t, hb)
    dwout = _matmul("wgrad_out", ycat_t, dmixed)
    gate_blocks = _gate_wgrad(xl_t, dgate, avg)
    msg_gate = gate_blocks.reshape(W, 128)
    done = dwint[0:HALO, 0:128] + dwout[0:HALO, 0:128] + gate_blocks[0, 0:HALO, :].sum() + gx[0:HALO, 0:128]
    _, mlp_chips = _chips_wait("mlp", mlp_send, mlp_recv, mlp_thru, mlp_land, done)
    mix_parts = [by_dest(dwout), by_dest(dwint)]
    gmod8 = (jnp.pad(vec[0:5], ((0, 1), (0, 0))) + jnp.pad(vec2[0:1], ((5, 0), (0, 0)))).reshape(N_DEV, 6 * D // N_DEV)
    sc_t = jnp.pad(sc_all.T, ((0, 0), (0, 128 - N_DEV))).astype(BF16)
    loss_rows = jnp.pad(loss8[0:1], ((0, HALO - 1), (0, D - loss8.shape[1])))
    msg_vec = jnp.concatenate([vec, vec2, loss_rows], axis=0)
    sib_send, sib_recv, sib_thru, sib_land, token = _chips_start("mixer", mix_parts, stage="sibling")
    g_adaw, sum_vec, sum_gate = _small_grad_exchange(gmod8, sc_t, msg_vec, msg_gate, token)
    mix_parts, mix_sib = _chips_wait("mixer", sib_send, sib_recv, sib_thru, sib_land, sum_vec[0:HALO, 0:128],
                                     stage="sibling")
    mix_sums = [_pair_sum(pos, p, r) for p, r in zip(mix_parts, mix_sib)]
    state = lambda n: (weights[n][0], given["m_" + n][0], given["v_" + n][0])
    mlp_jobs = [(p, r, q, *state(n)) for p, r, q, n in zip(mlp_parts, mlp_sib, mlp_chips, ("w_mlp1", "w_mlp2"))]
    mix_send, mix_recv, mix_thru, mix_land, token = _chips_start("mixer", mix_sums)
    mlp_done, (adaw_done,) = _update(pos, mlp_jobs, [(g_adaw, *state("ada_w"))], token)
    loss = sum_vec[LOSS_ROW, 0]
    sum_gate = sum_gate.reshape(2, W, W // 8)
    lo, hi = slice(0, W), slice(W, 2 * W)
    wl_full = sum_vec[V_WL01:V_WL23 + 1].reshape(CONV_L, W)
    ws_full = sum_vec[V_WS01:V_WS2 + 1].reshape(CONV_S + 1, W)[:CONV_S]
    row = lambda r, cols: sum_vec[r:r + 1, cols]
    small_grads = {
        "ada_b": sum_vec[GB_BASE:GB_BASE + N_DEV, :6 * D // N_DEV].reshape(1, 6 * D),
        "norm1_g": row(V_G1, slice(0, D)),
        "lru_conv_w": lax.dynamic_slice(wl_full, (0, me_lin * hd), (CONV_L, hd)),
        "lru_conv_b": row(V_BL_BA, lo),
        "gate_a_w": sum_gate[0],
        "gate_a_b": row(V_BL_BA, hi),
        "gate_x_w": sum_gate[1],
        "gate_x_b": row(V_BX_SP, lo),
        "a_param": row(V_BX_SP, hi),
        "short_conv_w": lax.dynamic_slice(ws_full, (0, me_lin * hd), (CONV_S, hd)),
        "lru_out_g": row(V_GL_GC, lo),
        "conv_out_g": row(V_GL_GC, hi),
        "norm2_g": row(V_G2, slice(0, D)),
        "final_g": sum_vec[V_ROWS + 1:V_ROWS + 2, :],
    }
    names = list(small_grads)
    as2d = lambda a, n: a.reshape(small_grads[n].shape)
    small = _adamw_small([as2d(weights[n], n) for n in names], [small_grads[n] for n in names],
                         [as2d(given["m_" + n], n) for n in names], [as2d(given["v_" + n], n) for n in names],
                         [n == "a_param" for n in names])
    result = {n: tuple(o.reshape(weights[n].shape) for o in outs) for n, outs in zip(names, small)}

    done = (small[0][1][:, 0:128] + mlp_done[0][2][0:HALO, 0:128] + mlp_done[1][2][0:HALO, 0:128]
            + adaw_done[1][0:HALO, 0:128])
    _, mix_chips = _chips_wait("mixer", mix_send, mix_recv, mix_thru, mix_land, done)
    g_wout, g_wint = (_final_sum(pos, p, r, q) for p, r, q in zip(mix_parts, mix_sib, mix_chips))
    for n, g in (("w_in", g_wint.T), ("w_out", g_wout)):
        w, m, v = state(n)
        result[n] = (g[None],) + tuple(o[None] for o in _adamw(n, w, g, m, v))
    result["w_mlp1"], result["w_mlp2"] = (tuple(o[None] for o in done) for done in mlp_done)
    result["ada_w"] = (g_adaw[None],) + tuple(o[None] for o in adaw_done)

    return (loss, gx[None], *[result[n][0] for n in _WEIGHT_NAMES], *[result[n][1] for n in _WEIGHT_NAMES],
            *[result[n][2] for n in _WEIGHT_NAMES], *[result[n][3] for n in _WEIGHT_NAMES])
```

```python
import functools

import jax
import jax.numpy as jnp
from jax import lax
from jax.experimental import pallas as pl
from jax.experimental.pallas import tpu as pltpu

F32 = jnp.float32
BF16 = jnp.bfloat16
MESH = pl.DeviceIdType.MESH

N_DEV = 8
D = 1024
W = 512
D_IN = 5 * W
D_FF = 4096
FF_BLK = D_FF // N_DEV
EPS = 1e-6
C_GATE = 8.0
CONV_L = 4
CONV_S = 3
HALO = 8

ROWS_W1T, ROWS_W2, ROWS_WOUT, ROWS_WIN = FF_BLK, FF_BLK, D // N_DEV, D_IN // N_DEV
OFF_WOUT = 0
OFF_WIN = OFF_WOUT + ROWS_WOUT
MIX_ROWS = OFF_WIN + ROWS_WIN
OFF_W1T = 0
OFF_W2 = OFF_W1T + ROWS_W1T
MLP_ROWS = OFF_W2 + ROWS_W2
CHIP_FLIPS = (4, 2, 6)
N_KEPT = 6

ADAM_LR = 0.001
ADAM_B1 = 0.9
ADAM_B2 = 0.999
ADAM_EPS = 1e-08
ADAM_WD = 0.01
ADAM_STEP = 10

VMEM_LIMIT = 56 * 1024 * 1024

TB_MIX = 256
TB_MIXB = 256
TB_MLP = 256
TB_MLPB = 512

ANY = pl.BlockSpec(memory_space=pl.ANY)
WHOLE = pl.BlockSpec(memory_space=pltpu.VMEM)


def _dot(a, b):
    return jnp.dot(a, b, preferred_element_type=F32)


def _dot_nt(a, b):
    return lax.dot_general(a, b, (((1,), (1,)), ((), ())), preferred_element_type=F32)


def _dot_tn(a, b):
    return lax.dot_general(a, b, (((0,), (0,)), ((), ())), preferred_element_type=F32)


def _sigmoid(v):
    return 1.0 / (1.0 + jnp.exp(-v))


def _softplus(v):
    t = jnp.exp(-jnp.abs(v))
    small = t * (1.0 - t * (0.5 - t * (1.0 / 3.0)))
    return jnp.maximum(v, 0.0) + jnp.where(t < 1e-2, small, jnp.log(1.0 + t))


def _one_minus_sq(a, log_a):
    return -jnp.tanh(log_a) * (a * a + 1.0)


_GELU_K = 0.7978845608028654
_GELU_C = 0.044715


def _gelu(u):
    th = jnp.tanh(_GELU_K * (u + _GELU_C * u * u * u))
    return 0.5 * u * (1.0 + th), th


def _gelu_grad(u, th):
    return 0.5 * (1.0 + th) + 0.5 * u * (1.0 - th * th) * _GELU_K * (1.0 + 3.0 * _GELU_C * u * u)


def _group_mean(v, avg):
    hi = v.astype(BF16)
    lo = (v - hi.astype(F32)).astype(BF16)
    return _dot(hi, avg) + _dot(lo, avg)


def _colsum(v):
    return jnp.sum(v, axis=0, keepdims=True)


def _rowmean(v):
    return jnp.mean(v, axis=-1, keepdims=True)


def _load_packed(wpack_hbm, off, rows, dst, sem):
    copies = [
        pltpu.make_async_copy(wpack_hbm.at[d, pl.ds(off, rows), :], dst.at[pl.ds(d * rows, rows), :], sem)
        for d in range(N_DEV)
    ]
    for cp in copies:
        cp.start()
    return copies


def _scan_groups(n_groups, a_ref, b_ref, out_ref, carry_ref, reverse):
    row = lax.broadcasted_iota(jnp.int32, (HALO, W), 0)

    def step(k, carry):
        g = (n_groups - 1 - k) if reverse else k
        rows = pl.ds(pl.multiple_of(g * HALO, HALO), HALO)
        a = a_ref[rows, :]
        b = b_ref[rows, :]
        for s in (1, 2, 4):
            if reverse:
                keep = row < HALO - s
                sh = HALO - s
            else:
                keep = row >= s
                sh = s
            a_sh = pltpu.roll(a, sh, axis=0)
            b_sh = pltpu.roll(b, sh, axis=0)
            b = jnp.where(keep, a * b_sh + b, b)
            a = jnp.where(keep, a * a_sh, a)
        h = b + a * carry
        out_ref[rows, :] = h
        edge = h[0:1, :] if reverse else h[HALO - 1:HALO, :]
        return jnp.broadcast_to(edge, (HALO, W))

    carry_ref[...] = lax.fori_loop(0, n_groups, step, carry_ref[...])


def _route_peers(me):
    x, y, c = me
    first = ((x + 1 - c) % 2, (y + c) % 2, c)
    second = ((x + c) % 2, (y + 1 - c) % 2, c)
    return first, second, (1 - x, 1 - y, c)


def _chip_gather_copies(block_hbm, out_hbm, send_sems, recv_sems):
    me = _position()
    first, second, diag = _route_peers(me)

    def copy(j, src, slot_of, to):
        return pltpu.make_async_remote_copy(
            src_ref=src, dst_ref=out_hbm.at[_linear(slot_of)], send_sem=send_sems.at[j], recv_sem=recv_sems.at[j],
            device_id=to, device_id_type=MESH)

    own_sends = [copy(0, block_hbm, me, first), copy(1, block_hbm, me, second)]
    forward = copy(2, out_hbm.at[_linear(first)], first, second)
    arrivals = [copy(0, block_hbm, first, first), copy(1, block_hbm, second, second), copy(2, block_hbm, diag, second)]
    return own_sends, forward, arrivals


def _mixer_fwd(x, modraw, adab, g1, wl, bl, bda, bdx, ba, bxb, ap, ws, gl, gc, avg, wpack, mlp_block, wmlp):
    t_len = x.shape[0]
    tb = TB_MIX
    nb = t_len // tb

    def body(x_ref, modraw_ref, adab_ref, g1_ref, wl_ref, bl_ref, bda_ref, bdx_ref, ba_ref, bxb_ref, ap_ref,
             ws_ref, gl_ref, gc_ref, avg_ref, wpack_hbm, block_hbm, wmlp_in, proj_ref, hl_ref, mixed_ref, kept_ref,
             wmlp_hbm, win_v, wout_v, sem, ulx_ext, cv_ext, hcar, a_s, b_s, send_sems, recv_sems, sib_send_sems,
             sib_recv_sems, local_sem):
        i = pl.program_id(0)
        x_pos, y_pos, c_pos = me = _position()
        sibling = (x_pos, y_pos, 1 - c_pos)
        first, second, diag = _route_peers(me)
        own = pltpu.make_async_copy(block_hbm, wmlp_hbm.at[_linear(me)], local_sem)

        def onward(slot_of):
            slot = wmlp_hbm.at[_linear(slot_of)]
            return pltpu.make_async_remote_copy(src_ref=slot, dst_ref=slot, send_sem=send_sems.at[0],
                                                recv_sem=recv_sems.at[0], device_id=second, device_id_type=MESH)

        def to_sibling(j, block_of):
            slot = wmlp_hbm.at[_linear(block_of)]
            return pltpu.make_async_remote_copy(src_ref=slot, dst_ref=slot, send_sem=sib_send_sems.at[j],
                                                recv_sem=sib_recv_sems.at[j], device_id=sibling, device_id_type=MESH)

        forward = onward(first)
        passes = [to_sibling(j, p) for j, p in enumerate((first, second, diag))]

        @pl.when(i == 0)
        def _():
            own.start()
            forward.start()
            passes[0].start()
            passes[1].start()

        @pl.when(i == 0)
        def _():
            cps = _load_packed(wpack_hbm, OFF_WIN, ROWS_WIN, win_v, sem.at[0])
            cps += _load_packed(wpack_hbm, OFF_WOUT, ROWS_WOUT, wout_v, sem.at[1])
            ulx_ext[0:HALO, :] = jnp.zeros((HALO, W), F32)
            cv_ext[0:HALO, :] = jnp.zeros((HALO, W), F32)
            hcar[...] = jnp.zeros((HALO, W), F32)
            for cp in cps:
                cp.wait()

        mod = modraw_ref[...] + adab_ref[...]
        shift1, scale1, gate1 = mod[0:1], mod[1:2], mod[2:3]
        x = x_ref[...]
        r1 = lax.rsqrt(_rowmean(x * x) + EPS)
        h = (x * r1 * g1_ref[...]) * (1.0 + scale1) + shift1
        proj = _dot_nt(h.astype(BF16), win_v[...])
        proj_ref[...] = proj
        u_lx, u_ly, u_b, u_c, u_v = (proj[:, k * W:(k + 1) * W] for k in range(5))

        ulx_ext[HALO:HALO + tb, :] = u_lx
        xl = bl_ref[...] + wl_ref[CONV_L - 1:CONV_L, :] * u_lx
        for k in range(CONV_L - 1):
            xl = xl + wl_ref[k:k + 1, :] * ulx_ext[pl.ds(HALO - (CONV_L - 1) + k, tb), :]
        ulx_ext[0:HALO, :] = ulx_ext[tb:tb + HALO, :]
        xlb = xl.astype(BF16)
        r = _sigmoid(_dot(xlb, bda_ref[...]) + ba_ref[...])
        ig = _sigmoid(_dot(xlb, bdx_ref[...]) + bxb_ref[...])
        log_a = (-C_GATE) * r * _softplus(ap_ref[...])
        a = jnp.exp(log_a)
        mult = jnp.sqrt(_one_minus_sq(a, log_a))
        grow = i * tb + lax.broadcasted_iota(jnp.int32, (tb, W), 0)
        mult = jnp.where(grow == 0, 1.0, mult)
        a_s[...] = a
        b_s[...] = mult * (ig * xl)
        _scan_groups(tb // HALO, a_s, b_s, hl_ref, hcar, reverse=False)
        hl = hl_ref[...]
        ge, _ = _gelu(u_ly)
        p = ge * hl
        rp = lax.rsqrt(_group_mean(p * p, avg_ref[...]) + EPS)
        y_lru = p * rp * gl_ref[...]

        cv = u_c * u_v
        cv_ext[HALO:HALO + tb, :] = cv
        cc = ws_ref[CONV_S - 1:CONV_S, :] * cv
        for k in range(CONV_S - 1):
            cc = cc + ws_ref[k:k + 1, :] * cv_ext[pl.ds(HALO - (CONV_S - 1) + k, tb), :]
        cv_ext[0:HALO, :] = cv_ext[tb:tb + HALO, :]
        q = u_b * cc
        rq = lax.rsqrt(_group_mean(q * q, avg_ref[...]) + EPS)
        y_conv = q * rq * gc_ref[...]
        for k, kept in enumerate((xl, r, ig, rp, rq, cc)):
            kept_ref[:, k * W:(k + 1) * W] = kept

        mixed_ref[...] = (_dot(y_lru.astype(BF16), wout_v[0:W, :]) + _dot(y_conv.astype(BF16), wout_v[W:2 * W, :]))

        @pl.when(i == nb - 1)
        def _():
            onward(diag).wait_recv()
            passes[2].start()
            for j, p in enumerate(_route_peers(sibling)):
                to_sibling(j, p).wait_recv()
            for cp in [forward] + passes:
                cp.wait_send()
            own.wait()

    tok = lambda cols: pl.BlockSpec((tb, cols), lambda i: (i, 0))
    full = lambda a: pl.BlockSpec(a.shape, lambda i: (0,) * a.ndim)
    small = (modraw, adab, g1, wl, bl, bda, bdx, ba, bxb, ap, ws, gl, gc, avg)
    n_chips = len(CHIP_FLIPS)
    return pl.pallas_call(
        body,
        name="mixer_fwd",
        grid=(nb,),
        in_specs=[tok(D)] + [full(a) for a in small] + [ANY, ANY, ANY],
        out_specs=[tok(D_IN), tok(W), tok(D), tok(N_KEPT * W), ANY],
        out_shape=[jax.ShapeDtypeStruct((t_len, D_IN), F32), jax.ShapeDtypeStruct((t_len, W), F32),
                   jax.ShapeDtypeStruct((t_len, D), F32), jax.ShapeDtypeStruct((t_len, N_KEPT * W), F32),
                   jax.ShapeDtypeStruct(wmlp.shape, wmlp.dtype)],
        input_output_aliases={len(small) + 3: 4},
        scratch_shapes=[pltpu.VMEM((D_IN, D), BF16), pltpu.VMEM((D, D), BF16), pltpu.SemaphoreType.DMA((2,)),
                        pltpu.VMEM((tb + HALO, W), F32), pltpu.VMEM((tb + HALO, W), F32), pltpu.VMEM((HALO, W), F32),
                        pltpu.VMEM((tb, W), F32), pltpu.VMEM((tb, W), F32),
                        pltpu.SemaphoreType.DMA((1,)), pltpu.SemaphoreType.DMA((1,)),
                        pltpu.SemaphoreType.DMA((3,)), pltpu.SemaphoreType.DMA((3,)), pltpu.SemaphoreType.DMA],
        compiler_params=pltpu.CompilerParams(dimension_semantics=("arbitrary",), vmem_limit_bytes=VMEM_LIMIT),
    )(x, *small, wpack, mlp_block, wmlp)


def _mlp_fwd(x, mixed, tgt, modraw, adab, g2, gf, wpack):
    t_len = x.shape[0]
    tb = TB_MLP
    nb = t_len // tb

    def body(x_ref, mixed_ref, tgt_ref, modraw_ref, adab_ref, g2_ref, gf_ref, wpack_hbm,
             h2t_ref, f_ref, dx2_ref, dz_ref, vec_ref, loss_ref, w1t_v, w2_v, sem):
        i = pl.program_id(0)

        @pl.when(i == 0)
        def _():
            cps = _load_packed(wpack_hbm, OFF_W1T, ROWS_W1T, w1t_v, sem.at[0])
            cps += _load_packed(wpack_hbm, OFF_W2, ROWS_W2, w2_v, sem.at[1])
            vec_ref[...] = jnp.zeros(vec_ref.shape, F32)
            loss_ref[...] = jnp.zeros(loss_ref.shape, F32)
            for cp in cps:
                cp.wait()

        mod = modraw_ref[...] + adab_ref[...]
        gate1, shift2, scale2, gate2 = mod[2:3], mod[3:4], mod[4:5], mod[5:6]
        x1 = x_ref[...] + gate1 * mixed_ref[...]
        r2 = lax.rsqrt(_rowmean(x1 * x1) + EPS)
        h2 = (x1 * r2 * g2_ref[...]) * (1.0 + scale2) + shift2
        h2b = h2.astype(BF16)
        h2t_ref[...] = h2.T.astype(BF16)
        z = jnp.zeros((tb, D), F32)
        for j in range(N_DEV):
            cols = slice(j * FF_BLK, (j + 1) * FF_BLK)
            fj = _dot_nt(h2b, w1t_v[cols, :])
            f_ref[:, cols] = fj
            rf = jnp.maximum(fj, 0.0)
            z = z + _dot((rf * rf).astype(BF16), w2_v[cols, :])
        x2 = x1 + gate2 * z
        r3 = lax.rsqrt(_rowmean(x2 * x2) + EPS)
        xn3 = x2 * r3
        diff = xn3 * gf_ref[...] - tgt_ref[...]
        sq = _colsum(diff * diff)
        loss_ref[...] += jnp.broadcast_to(jnp.sum(sq, axis=1, keepdims=True) * (0.5 / D), loss_ref.shape)
        dy = diff * (1.0 / D)
        dyn = dy * gf_ref[...]
        dx2 = r3 * (dyn - xn3 * _rowmean(dyn * xn3))
        dx2_ref[...] = dx2
        dz_ref[...] = (gate2 * dx2).astype(BF16)
        vec_ref[0:1, :] += _colsum(dx2 * z)
        vec_ref[1:2, :] += _colsum(dy * xn3)

    tok = lambda cols: pl.BlockSpec((tb, cols), lambda i: (i, 0))
    tok_t = pl.BlockSpec((D, tb), lambda i: (0, i))
    full = lambda a: pl.BlockSpec(a.shape, lambda i: (0,) * a.ndim)
    small = (modraw, adab, g2, gf)
    return pl.pallas_call(
        body,
        name="mlp_fwd",
        grid=(nb,),
        in_specs=[tok(D), tok(D), tok(D)] + [full(a) for a in small] + [ANY],
        out_specs=[tok_t, tok(D_FF), tok(D), tok(D), pl.BlockSpec((8, D), lambda i: (0, 0)),
                   pl.BlockSpec((8, 128), lambda i: (0, 0))],
        out_shape=[jax.ShapeDtypeStruct((D, t_len), BF16), jax.ShapeDtypeStruct((t_len, D_FF), F32),
                   jax.ShapeDtypeStruct((t_len, D), F32), jax.ShapeDtypeStruct((t_len, D), BF16),
                   jax.ShapeDtypeStruct((8, D), F32), jax.ShapeDtypeStruct((8, 128), F32)],
        scratch_shapes=[pltpu.VMEM((D_FF, D), BF16), pltpu.VMEM((D_FF, D), BF16), pltpu.SemaphoreType.DMA((2,))],
        compiler_params=pltpu.CompilerParams(dimension_semantics=("arbitrary",), vmem_limit_bytes=VMEM_LIMIT),
    )(x, mixed, tgt, *small, wpack)


def _mlp_bwd_half(pos, h2t, f, dz, wpack, prior=None):
    t_len = dz.shape[0]
    tb = TB_MLPB
    nb = t_len // tb
    first = prior is None
    flip = 1 if first else 0

    def body(pos_ref, h2t_ref, f_ref, dz_ref, w1t_ref, w2_ref, *rest):
        if first:
            dh2_ref, dw1_ref, dw2_ref = rest
        else:
            dh2in_ref, _, dh2_ref, dw1_ref, dw2_ref = rest
        k = pl.program_id(0)
        t = pl.program_id(1)
        rows = pl.ds(pl.multiple_of(t * tb, tb), tb)
        w1t = w1t_ref[0]
        w2 = w2_ref[0]
        dz = dz_ref[...]
        rf = jnp.maximum(f_ref[...], 0.0)
        df = (_dot_nt(dz, w2) * (2.0 * rf)).astype(BF16)
        dh = _dot(df, w1t)
        g1 = _dot(h2t_ref[...], df)
        g2 = _dot_tn((rf * rf).astype(BF16), dz)

        @pl.when(t == 0)
        def _():
            dw2_ref[0] = g2
            dw1_ref[0] = g1

        @pl.when(t != 0)
        def _():
            dw2_ref[0] += g2
            dw1_ref[0] += g1

        @pl.when(k == 0)
        def _():
            dh2_ref[rows, :] = dh if first else dh2in_ref[...] + dh

        @pl.when(k != 0)
        def _():
            dh2_ref[rows, :] += dh

    blk = lambda k, pos: 2 * k + jnp.bitwise_xor(pos[0], flip)
    in_specs = [pl.BlockSpec((D, tb), lambda k, t, pos: (0, t)),
                pl.BlockSpec((tb, FF_BLK), lambda k, t, pos: (t, blk(k, pos))),
                pl.BlockSpec((tb, D), lambda k, t, pos: (t, 0)),
                pl.BlockSpec((1, ROWS_W1T, D), lambda k, t, pos: (blk(k, pos), OFF_W1T // ROWS_W1T, 0)),
                pl.BlockSpec((1, ROWS_W2, D), lambda k, t, pos: (blk(k, pos), OFF_W2 // ROWS_W2, 0))]
    grad_specs = [pl.BlockSpec((1, D, FF_BLK), lambda k, t, pos: (k, 0, 0)),
                  pl.BlockSpec((1, FF_BLK, D), lambda k, t, pos: (k, 0, 0))]
    out_specs = [pl.BlockSpec((t_len, D), lambda k, t, pos: (0, 0))] + grad_specs
    grad_shapes = [jax.ShapeDtypeStruct((4, D, FF_BLK), F32), jax.ShapeDtypeStruct((4, FF_BLK, D), F32)]
    out_shape = [jax.ShapeDtypeStruct((t_len, D), F32)] + grad_shapes
    args = [pos, h2t, f, dz, wpack, wpack]
    if not first:
        in_specs += [pl.BlockSpec((tb, D), lambda k, t, pos: (jnp.where(k == 0, t, nb - 1), 0)),
                     pl.BlockSpec(prior[1].shape, lambda k, t, pos: (0,) * prior[1].ndim)]
        args += list(prior)
    return pl.pallas_call(
        body,
        name="mlp_bwd_first" if first else "mlp_bwd_second",
        grid_spec=pltpu.PrefetchScalarGridSpec(num_scalar_prefetch=1, grid=(4, nb), in_specs=in_specs,
                                               out_specs=out_specs),
        out_shape=out_shape,
        compiler_params=pltpu.CompilerParams(dimension_semantics=("arbitrary", "arbitrary"),
                                             vmem_limit_bytes=VMEM_LIMIT),
    )(*args)


V_SHIFT1, V_SCALE1, V_GATE1, V_SHIFT2, V_SCALE2, V_G1, V_G2 = 0, 1, 2, 3, 4, 6, 7
V_BL_BA, V_BX_SP, V_GL_GC, V_WL01, V_WL23, V_WS01, V_WS2 = 8, 9, 10, 11, 12, 13, 14
V_ROWS = 16


def _chip_scatter_copies(srcs, dsts, send_sems, recv_sems, row_ranges=None):
    me = _position()
    copies = []
    for a, (src, dst) in enumerate(zip(srcs, dsts)):
        band = pl.ds(*row_ranges[a]) if row_ranges else slice(None)
        for j, k in enumerate(CHIP_FLIPS):
            peer = _flip(me, k)
            copies.append(pltpu.make_async_remote_copy(
                src_ref=src.at[2 * peer[0] + peer[1], band], dst_ref=dst.at[j, band],
                send_sem=send_sems.at[len(CHIP_FLIPS) * a + j], recv_sem=recv_sems.at[len(CHIP_FLIPS) * a + j],
                device_id=peer, device_id_type=MESH))
    return copies


def _mixer_bwd(x, mixed, dh2, dx2, proj, hl, kept, modraw, adab, g1, g2, wl, bl, bda, bdx, ba, bxb, ap, ws, gl, gc, avg, wpack,
               chip_sums, chip_rows):
    t_len = x.shape[0]
    tb = TB_MIXB
    nb = t_len // tb
    hb = tb // HALO
    n_sums = len(chip_sums)

    def body(x_ref, mixed_ref, dh2_ref, dx2_ref, proj_ref, projh_ref, hl_ref, hlh_ref, kept_ref,
             modraw_ref, adab_ref, g1_ref, g2_ref, wl_ref, bl_ref, bda_ref, bdx_ref, ba_ref, bxb_ref, ap_ref,
             ws_ref, gl_ref, gc_ref, avg_ref, wpack_hbm, *rest):
        sums_hbm, rest = rest[:n_sums], rest[n_sums:]
        gx_ref, vec_ref, hb_ref, dprojt_ref, dmixed_ref, ycatt_ref, xlt_ref, dgate_ref = rest[:8]
        landed_hbm, rest = rest[8:8 + n_sums], rest[8 + n_sums:]
        (win_v, wout_v, sem, ulx_ext, cv_ext, hl_ext, a_ext, dxl_ext, dcc_ext, dcar, an_s, g_s, dh_s,
         send_sems, recv_sems) = rest
        i = pl.program_id(0)
        blk = nb - 1 - i
        chip_copies = _chip_scatter_copies(sums_hbm, landed_hbm, send_sems, recv_sems, chip_rows)

        @pl.when(i == 0)
        def _():
            for cp in chip_copies:
                cp.start()
            cps = _load_packed(wpack_hbm, OFF_WIN, ROWS_WIN, win_v, sem.at[0])
            cps += _load_packed(wpack_hbm, OFF_WOUT, ROWS_WOUT, wout_v, sem.at[1])
            vec_ref[...] = jnp.zeros(vec_ref.shape, F32)
            zero = jnp.zeros((HALO, W), F32)
            a_ext[tb:tb + HALO, :] = zero
            dxl_ext[tb:tb + HALO, :] = zero
            dcc_ext[tb:tb + HALO, :] = zero
            dcar[...] = zero
            for cp in cps:
                cp.wait()

        mod = modraw_ref[...] + adab_ref[...]
        shift1, scale1, gate1, scale2 = mod[0:1], mod[1:2], mod[2:3], mod[4:5]
        x = x_ref[...]
        mixed = mixed_ref[...]

        x1 = x + gate1 * mixed
        r2 = lax.rsqrt(_rowmean(x1 * x1) + EPS)
        xn2 = x1 * r2
        dh2 = dh2_ref[...]
        vec_ref[V_SHIFT2:V_SHIFT2 + 1, :] += _colsum(dh2)
        vec_ref[V_SCALE2:V_SCALE2 + 1, :] += _colsum(dh2 * xn2 * g2_ref[...])
        vec_ref[V_G2:V_G2 + 1, :] += _colsum(dh2 * (1.0 + scale2) * xn2)
        dxn2 = dh2 * g2_ref[...] * (1.0 + scale2)
        dx1 = dx2_ref[...] + r2 * (dxn2 - xn2 * _rowmean(dxn2 * xn2))
        vec_ref[V_GATE1:V_GATE1 + 1, :] += _colsum(dx1 * mixed)
        dmixed = (gate1 * dx1).astype(BF16)

        proj = proj_ref[...]
        u_lx, u_ly, u_b, u_c, u_v = (proj[:, k * W:(k + 1) * W] for k in range(5))
        has_prev = (blk > 0).astype(F32)
        projh = projh_ref[...]
        ulx_ext[0:HALO, :] = projh[:, 0:W] * has_prev
        ulx_ext[HALO:HALO + tb, :] = u_lx
        xl, r, ig, rp, rq, cc = (kept_ref[:, k * W:(k + 1) * W] for k in range(N_KEPT))
        sp = _softplus(ap_ref[...])
        log_a = (-C_GATE) * r * sp
        a = jnp.exp(log_a)
        mult_raw = jnp.sqrt(_one_minus_sq(a, log_a))
        first = (blk * tb + lax.broadcasted_iota(jnp.int32, (tb, W), 0)) == 0
        mult = jnp.where(first, 1.0, mult_raw)
        hl = hl_ref[...]
        ge, th = _gelu(u_ly)
        pn = ge * hl * rp
        cv = u_c * u_v
        cv_ext[0:HALO, :] = projh[:, 3 * W:4 * W] * projh[:, 4 * W:5 * W] * has_prev
        cv_ext[HALO:HALO + tb, :] = cv
        qn = u_b * cc * rq

        dmixed_ref[...] = dmixed
        ycatt_ref[0:W, :] = (pn * gl_ref[...]).T.astype(BF16)
        ycatt_ref[W:2 * W, :] = (qn * gc_ref[...]).T.astype(BF16)
        dyl = _dot_nt(dmixed, wout_v[0:W, :])
        dyc = _dot_nt(dmixed, wout_v[W:2 * W, :])

        dqn = dyc * gc_ref[...]
        dq = rq * (dqn - qn * _group_mean(dqn * qn, avg_ref[...]))
        du_b = dq * cc
        dcc = dq * u_b
        dcc_ext[0:tb, :] = dcc
        dcv = ws_ref[CONV_S - 1:CONV_S, :] * dcc
        for k in range(CONV_S - 1):
            dcv = dcv + ws_ref[k:k + 1, :] * dcc_ext[pl.ds(CONV_S - 1 - k, tb), :]
        dcc_ext[tb:tb + HALO, :] = dcc_ext[0:HALO, :]
        du_c = dcv * u_v
        du_v = dcv * u_c
        dws = [_colsum(dcc * cv_ext[pl.ds(HALO - (CONV_S - 1) + k, tb), :]) for k in range(CONV_S)]

        dpn = dyl * gl_ref[...]
        dp = rp * (dpn - pn * _group_mean(dpn * pn, avg_ref[...]))
        du_ly = dp * hl * _gelu_grad(u_ly, th)
        g_s[...] = dp * ge
        a_ext[0:tb, :] = a
        an_s[...] = a_ext[pl.ds(1, tb), :]
        _scan_groups(hb, an_s, g_s, dh_s, dcar, reverse=True)
        a_ext[tb:tb + HALO, :] = a_ext[0:HALO, :]
        dh = dh_s[...]
        hl_ext[0:HALO, :] = hlh_ref[...] * has_prev
        hl_ext[HALO:HALO + tb, :] = hl
        da = dh * hl_ext[pl.ds(HALO - 1, tb), :]
        dmult = dh * (ig * xl)
        dig = dh * (mult * xl)
        dxl = dh * (mult * ig)
        dlog = da * a - jnp.where(first, 0.0, dmult * (a * a) / mult_raw)
        dr = dlog * ((-C_GATE) * sp)
        dsp = _colsum(dlog * ((-C_GATE) * r))
        dga = dr * r * (1.0 - r)
        dgx = dig * ig * (1.0 - ig)
        dgab = dga.astype(BF16)
        dgxb = dgx.astype(BF16)
        xlt_ref[...] = xl.T.astype(BF16)
        dgate_ref[:, 0:W] = dgab
        dgate_ref[:, W:2 * W] = dgxb
        dxl = dxl + _dot_nt(dgab, bda_ref[...]) + _dot_nt(dgxb, bdx_ref[...])
        dxl_ext[0:tb, :] = dxl
        du_lx = wl_ref[CONV_L - 1:CONV_L, :] * dxl
        for k in range(CONV_L - 1):
            du_lx = du_lx + wl_ref[k:k + 1, :] * dxl_ext[pl.ds(CONV_L - 1 - k, tb), :]
        dxl_ext[tb:tb + HALO, :] = dxl_ext[0:HALO, :]
        dwl = [_colsum(dxl * ulx_ext[pl.ds(HALO - (CONV_L - 1) + k, tb), :]) for k in range(CONV_L)]

        cat = lambda u, v: jnp.concatenate([u, v], axis=1)
        vec_ref[V_BL_BA:V_BL_BA + 1, :] += cat(_colsum(dxl), _colsum(dga))
        vec_ref[V_BX_SP:V_BX_SP + 1, :] += cat(_colsum(dgx), dsp)
        vec_ref[V_GL_GC:V_GL_GC + 1, :] += cat(_colsum(dyl * pn), _colsum(dyc * qn))
        vec_ref[V_WL01:V_WL01 + 1, :] += cat(dwl[0], dwl[1])
        vec_ref[V_WL23:V_WL23 + 1, :] += cat(dwl[2], dwl[3])
        vec_ref[V_WS01:V_WS01 + 1, :] += cat(dws[0], dws[1])
        vec_ref[V_WS2:V_WS2 + 1, 0:W] += dws[2]

        r1 = lax.rsqrt(_rowmean(x * x) + EPS)
        xn1 = x * r1
        hb_ref[...] = ((xn1 * g1_ref[...]) * (1.0 + scale1) + shift1).astype(BF16)
        dh_in = jnp.zeros((tb, D), F32)
        for k, du in enumerate((du_lx, du_ly, du_b, du_c, du_v)):
            dprojt_ref[k * W:(k + 1) * W, :] = du.T.astype(BF16)
            dh_in = dh_in + _dot(du.astype(BF16), win_v[k * W:(k + 1) * W, :])
        vec_ref[V_SHIFT1:V_SHIFT1 + 1, :] += _colsum(dh_in)
        vec_ref[V_SCALE1:V_SCALE1 + 1, :] += _colsum(dh_in * xn1 * g1_ref[...])
        vec_ref[V_G1:V_G1 + 1, :] += _colsum(dh_in * (1.0 + scale1) * xn1)
        dxn1 = dh_in * g1_ref[...] * (1.0 + scale1)
        gx_ref[...] = dx1 + r1 * (dxn1 - xn1 * _rowmean(dxn1 * xn1))

        @pl.when(i == nb - 1)
        def _():
            for cp in chip_copies:
                cp.wait_recv()
            for cp in chip_copies:
                cp.wait_send()

    rev = lambda cols: pl.BlockSpec((tb, cols), lambda i: (nb - 1 - i, 0))
    rev_t = lambda rows: pl.BlockSpec((rows, tb), lambda i: (0, nb - 1 - i))
    halo = lambda cols: pl.BlockSpec((HALO, cols), lambda i: (jnp.maximum((nb - 1 - i) * hb - 1, 0), 0))
    full = lambda a: pl.BlockSpec(a.shape, lambda i: (0,) * a.ndim)
    small = (modraw, adab, g1, g2, wl, bl, bda, bdx, ba, bxb, ap, ws, gl, gc, avg)
    ext = pltpu.VMEM((tb + HALO, W), F32)
    n_sems = max(len(CHIP_FLIPS) * n_sums, 1)
    return pl.pallas_call(
        body,
        name="mixer_bwd",
        grid=(nb,),
        in_specs=[rev(D), rev(D), rev(D), rev(D), rev(D_IN), halo(D_IN), rev(W), halo(W), rev(N_KEPT * W)]
        + [full(a) for a in small] + [ANY] * (1 + n_sums),
        out_specs=[rev(D), pl.BlockSpec((V_ROWS, D), lambda i: (0, 0)), rev(D), rev_t(D_IN), rev(D), rev_t(D),
                   rev_t(W), rev(2 * W)] + [ANY] * n_sums,
        out_shape=[jax.ShapeDtypeStruct((t_len, D), F32), jax.ShapeDtypeStruct((V_ROWS, D), F32),
                   jax.ShapeDtypeStruct((t_len, D), BF16), jax.ShapeDtypeStruct((D_IN, t_len), BF16),
                   jax.ShapeDtypeStruct((t_len, D), BF16), jax.ShapeDtypeStruct((D, t_len), BF16),
                   jax.ShapeDtypeStruct((W, t_len), BF16), jax.ShapeDtypeStruct((t_len, 2 * W), BF16)]
        + [jax.ShapeDtypeStruct((len(CHIP_FLIPS),) + s.shape[1:], s.dtype) for s in chip_sums],
        scratch_shapes=[pltpu.VMEM((D_IN, D), BF16), pltpu.VMEM((D, D), BF16), pltpu.SemaphoreType.DMA((2,)),
                        ext, ext, ext, ext, ext, ext, pltpu.VMEM((HALO, W), F32),
                        pltpu.VMEM((tb, W), F32), pltpu.VMEM((tb, W), F32), pltpu.VMEM((tb, W), F32),
                        pltpu.SemaphoreType.DMA((n_sems,)), pltpu.SemaphoreType.DMA((n_sems,))],
        compiler_params=pltpu.CompilerParams(dimension_semantics=("arbitrary",), vmem_limit_bytes=VMEM_LIMIT),
    )(x, mixed, dh2, dx2, proj, proj, hl, hl, kept, *small, wpack, *chip_sums)


def _matmul(name, a, b, tm=512):
    m, k = a.shape
    n = b.shape[1]

    def body(a_ref, b_ref, o_ref):
        o_ref[...] = _dot(a_ref[...], b_ref[...])

    return pl.pallas_call(
        body,
        name=name,
        grid=(m // tm,),
        in_specs=[pl.BlockSpec((tm, k), lambda i: (i, 0)), pl.BlockSpec((k, n), lambda i: (0, 0))],
        out_specs=pl.BlockSpec((tm, n), lambda i: (i, 0)),
        out_shape=jax.ShapeDtypeStruct((m, n), F32),
        compiler_params=pltpu.CompilerParams(dimension_semantics=("arbitrary",), vmem_limit_bytes=VMEM_LIMIT),
    )(a, b)


def _gate_wgrad(xl_t, dgate, avg):
    hd = W // 8

    def body(a_ref, b_ref, avg_ref, o_ref):
        full = _dot(a_ref[...], b_ref[...])
        row = lax.broadcasted_iota(jnp.int32, (W, hd), 0)
        col = lax.broadcasted_iota(jnp.int32, (W, hd), 1)
        fold = ((row & (hd - 1)) == col).astype(BF16)
        keep = avg_ref[...] != 0
        for g in range(2):
            m = jnp.where(keep, full[:, g * W:(g + 1) * W], 0.0)
            hi = m.astype(BF16)
            rest = m - hi.astype(F32)
            mid = rest.astype(BF16)
            lo = (rest - mid.astype(F32)).astype(BF16)
            o_ref[g] = _dot(hi, fold) + _dot(mid, fold) + _dot(lo, fold)

    return pl.pallas_call(
        body,
        name="wgrad_gate",
        in_specs=[WHOLE] * 3,
        out_specs=WHOLE,
        out_shape=jax.ShapeDtypeStruct((2, W, hd), F32),
        compiler_params=pltpu.CompilerParams(vmem_limit_bytes=VMEM_LIMIT),
    )(xl_t, dgate, avg)


def _block_diag(w):
    n, m, _ = w.shape
    eye = jnp.eye(n, dtype=w.dtype)
    return (w[:, :, None, :] * eye[:, None, :, None]).reshape(n * m, n * m)


def _pad_rows(a, rows):
    return jnp.pad(a, ((0, rows - a.shape[0]),) + ((0, 0),) * (a.ndim - 1))


def _position():
    return lax.axis_index("x"), lax.axis_index("y"), lax.axis_index("c")


def _linear(pos):
    return 4 * pos[0] + 2 * pos[1] + pos[2]


def _flip(pos, k):
    return tuple(1 - p if k & bit else p for p, bit in zip(pos, (4, 2, 1)))


def _exchange_all(make_copy, make_arrival):
    copies = [make_copy(k) for k in range(1, N_DEV)]
    for cp in copies:
        cp.start()
    for k in range(1, N_DEV):
        make_arrival(k).wait_recv()
    for cp in copies:
        cp.wait_send()


def _mod_exchange_steps(cols):
    def steps(msg_ref, adaw_ref, gath_ref, mod_ref, sendbuf, send_a, recv_a, send_b, recv_b):
        me = _position()
        me_lin = _linear(me)
        m = msg_ref[...]
        row = lax.broadcasted_iota(jnp.int32, m.shape, 0)
        gath_ref[me_lin] = jnp.where(row == 0, m * _sigmoid(m), m)

        def gather_copy(k, src_lin):
            return pltpu.make_async_remote_copy(
                src_ref=gath_ref.at[src_lin], dst_ref=gath_ref.at[src_lin], send_sem=send_a.at[k - 1],
                recv_sem=recv_a.at[k - 1], device_id=_flip(me, k), device_id_type=MESH)

        _exchange_all(lambda k: gather_copy(k, me_lin), lambda k: gather_copy(k, _linear(_flip(me, k))))

        sc_all = gath_ref[:, 0, :]
        scb = jnp.concatenate([sc_all, jnp.zeros_like(sc_all)], axis=0).astype(BF16)
        prod = _dot(scb, adaw_ref[...].astype(BF16))
        for b in range(N_DEV):
            sendbuf[b] = jnp.broadcast_to(prod[b:b + 1, :], (HALO, cols))
        mod_ref[me_lin] = sendbuf[me_lin]

        def row_copy(k, dst_lin):
            peer = _flip(me, k)
            return pltpu.make_async_remote_copy(
                src_ref=sendbuf.at[_linear(peer)], dst_ref=mod_ref.at[dst_lin], send_sem=send_b.at[k - 1],
                recv_sem=recv_b.at[k - 1], device_id=peer, device_id_type=MESH)

        _exchange_all(lambda k: row_copy(k, me_lin), lambda k: row_copy(k, _linear(_flip(me, k))))

    return steps


def _gather_and_mod(msg, ada_w, block):
    rows, cols = block.shape
    mod_cols = ada_w.shape[1]
    mod_steps = _mod_exchange_steps(mod_cols)

    def body(msg_ref, adaw_ref, x_ref, gath_ref, mod_ref, out_ref, sendbuf, send_a, recv_a, send_b, recv_b,
             send_sems, recv_sems, sib_send_sems, sib_recv_sems, local_sem):
        x, y, c = _position()
        me, sibling = (x, y, c), (x, y, 1 - c)
        sends, forward, arrivals = _chip_gather_copies(x_ref, out_ref, send_sems, recv_sems)

        def to_sibling(j, block_of, src=None):
            dst = out_ref.at[_linear(block_of)]
            return pltpu.make_async_remote_copy(
                src_ref=dst if src is None else src, dst_ref=dst, send_sem=sib_send_sems.at[j],
                recv_sem=sib_recv_sems.at[j], device_id=sibling, device_id_type=MESH)

        mine = pltpu.make_async_copy(x_ref, out_ref.at[_linear(me)], local_sem)
        mine.start()
        passes = [to_sibling(0, me, src=x_ref)] + [to_sibling(1 + j, p) for j, p in enumerate(_route_peers(me))]
        passes[0].start()
        for cp in sends:
            cp.start()
        mod_steps(msg_ref, adaw_ref, gath_ref, mod_ref, sendbuf, send_a, recv_a, send_b, recv_b)
        arrivals[0].wait_recv()
        forward.start()
        passes[1].start()
        arrivals[1].wait_recv()
        passes[2].start()
        arrivals[2].wait_recv()
        passes[3].start()
        for j, p in enumerate((sibling,) + _route_peers(sibling)):
            to_sibling(j, p).wait_recv()
        for cp in sends + [forward] + passes:
            cp.wait_send()
        mine.wait()

    return pl.pallas_call(
        body,
        name="gather_and_mod",
        in_specs=[WHOLE, WHOLE, ANY],
        out_specs=[WHOLE, WHOLE, ANY],
        out_shape=[jax.ShapeDtypeStruct((N_DEV, HALO, D), F32), jax.ShapeDtypeStruct((N_DEV, HALO, mod_cols), F32),
                   jax.ShapeDtypeStruct((N_DEV, rows, cols), block.dtype)],
        scratch_shapes=[pltpu.VMEM((N_DEV, HALO, mod_cols), F32)] + [pltpu.SemaphoreType.DMA((N_DEV - 1,))] * 4
        + [pltpu.SemaphoreType.DMA((3,)), pltpu.SemaphoreType.DMA((3,)), pltpu.SemaphoreType.DMA((4,)),
           pltpu.SemaphoreType.DMA((4,)), pltpu.SemaphoreType.DMA],
        compiler_params=pltpu.CompilerParams(vmem_limit_bytes=VMEM_LIMIT),
    )(msg, ada_w, block)


HBM = pl.BlockSpec(memory_space=pltpu.HBM)
SEM = pl.BlockSpec(memory_space=pltpu.SEMAPHORE)
EFFECT = pltpu.SideEffectType.DATAFLOW_SIDE_EFFECTING


def _gather_first_copies(srcs, dsts, send_sems, recv_sems, arrivals=False):
    x, y, c = me = _position()
    first, second, _ = _route_peers(me)
    return [pltpu.make_async_remote_copy(
        src_ref=srcs[0], dst_ref=dsts[0].at[_linear(peer if arrivals else me)], send_sem=send_sems.at[j],
        recv_sem=recv_sems.at[j], device_id=peer, device_id_type=MESH)
        for j, peer in enumerate((first, second, (x, y, 1 - c)))]


def _stage_copies(stage):
    return {"chips": (_chip_scatter_copies, len(CHIP_FLIPS), len(CHIP_FLIPS)), "sibling": (_sibling_copies, 4, 4),
            "gather": (_gather_first_copies, 3, N_DEV)}[stage]


def _chips_start(which, chip_sums, stage="chips"):
    n = len(chip_sums)
    make_copies, per_array, slots = _stage_copies(stage)
    n_sems = per_array * n

    def body(*refs):
        srcs, dsts = refs[:n], refs[n:2 * n]
        send_sems, recv_sems = refs[2 * n:2 * n + 2]
        token = refs[-1]
        for cp in make_copies(srcs, dsts, send_sems, recv_sems):
            cp.start()
        token[...] = jnp.zeros(token.shape, token.dtype)

    landing = [jax.ShapeDtypeStruct((slots,) + s.shape[-2:], s.dtype) for s in chip_sums]
    outs = pl.pallas_call(
        body,
        name=which + "_" + stage + "_start",
        in_specs=[HBM] * (2 * n),
        out_specs=[SEM, SEM] + [HBM] * (2 * n) + [WHOLE],
        out_shape=[pltpu.SemaphoreType.DMA((n_sems,)), pltpu.SemaphoreType.DMA((n_sems,))]
        + [pltpu.HBM(s.shape, s.dtype) for s in chip_sums] + [pltpu.HBM(s.shape, s.dtype) for s in landing]
        + [jax.ShapeDtypeStruct((HALO, 128), F32)],
        input_output_aliases={i: 2 + i for i in range(2 * n)},
        compiler_params=pltpu.CompilerParams(has_side_effects=EFFECT),
    )(*[pltpu.with_memory_space_constraint(s, pltpu.HBM) for s in chip_sums],
      *[pltpu.with_memory_space_constraint(lax.empty(s.shape, s.dtype), pltpu.HBM) for s in landing])
    return outs[0], outs[1], outs[2:2 + n], outs[2 + n:2 + 2 * n], outs[-1]


def _chips_wait(which, send_sems, recv_sems, srcs, landed, after, stage="chips"):
    n = len(srcs)
    make_copies = _stage_copies(stage)[0]

    def body(*refs):
        src_refs, dst_refs = refs[:n], refs[n:2 * n]
        sends, recvs = refs[2 * n:2 * n + 2]
        copies = make_copies(src_refs, dst_refs, sends, recvs)
        landing = make_copies(src_refs, dst_refs, sends, recvs, arrivals=True) if stage == "gather" else copies
        for cp in copies:
            cp.wait_send()
        for cp in landing:
            cp.wait_recv()

    outs = pl.pallas_call(
        body,
        name=which + "_" + stage + "_wait",
        in_specs=[HBM] * (2 * n) + [SEM, SEM, ANY],
        out_specs=[HBM] * (2 * n),
        out_shape=[pltpu.HBM(s.shape, s.dtype) for s in list(srcs) + list(landed)],
        input_output_aliases={i: i for i in range(2 * n)},
        compiler_params=pltpu.CompilerParams(has_side_effects=EFFECT),
    )(*srcs, *landed, send_sems, recv_sems, after)
    return list(outs[:n]), list(outs[n:])


def _sibling_copies(srcs, dsts, send_sems, recv_sems):
    x, y, c = _position()
    copies = []
    for a, (src, dst) in enumerate(zip(srcs, dsts)):
        for k in range(4):
            copies.append(pltpu.make_async_remote_copy(
                src_ref=src.at[k, 1 - c] if len(src.shape) == 4 else src.at[k], dst_ref=dst.at[k],
                send_sem=send_sems.at[4 * a + k],
                recv_sem=recv_sems.at[4 * a + k], device_id=(x, y, 1 - c), device_id_type=MESH))
    return copies


def _row_block(rows):
    return min(rows, 512)


def _pair_sum(pos, mine, recv):
    _, cores, rows, cols = mine.shape
    rb = _row_block(rows)

    def body(pos_ref, mine_ref, recv_ref, out_ref):
        out_ref[0] = (mine_ref[0, 0] + recv_ref[0]).astype(BF16)

    other = lambda k, pos: jnp.bitwise_xor(pos[1], k + 1)
    core = lambda pos: pos[0] * (cores - 1)
    return pl.pallas_call(
        body,
        name="grad_pair_sum",
        grid_spec=pltpu.PrefetchScalarGridSpec(
            num_scalar_prefetch=1, grid=(3, rows // rb),
            in_specs=[pl.BlockSpec((1, 1, rb, cols), lambda k, r, pos: (other(k, pos), core(pos), r, 0)),
                      pl.BlockSpec((1, rb, cols), lambda k, r, pos: (other(k, pos), r, 0))],
            out_specs=pl.BlockSpec((1, rb, cols), lambda k, r, pos: (other(k, pos), r, 0))),
        out_shape=jax.ShapeDtypeStruct((4, rows, cols), BF16),
        compiler_params=pltpu.CompilerParams(dimension_semantics=("arbitrary", "arbitrary")),
    )(pos, mine, recv)


def _final_sum(pos, mine, recv, chips):
    _, cores, rows, cols = mine.shape
    rb = _row_block(rows)

    def body(pos_ref, mine_ref, recv_ref, chips_ref, out_ref):
        g = mine_ref[0, 0] + recv_ref[0]
        for j in range(3):
            g = g + chips_ref[j].astype(F32)
        out_ref[...] = g

    return pl.pallas_call(
        body,
        name="grad_final_sum",
        grid_spec=pltpu.PrefetchScalarGridSpec(
            num_scalar_prefetch=1, grid=(rows // rb,),
            in_specs=[pl.BlockSpec((1, 1, rb, cols), lambda r, pos: (pos[1], pos[0] * (cores - 1), r, 0)),
                      pl.BlockSpec((1, rb, cols), lambda r, pos: (pos[1], r, 0)),
                      pl.BlockSpec((3, rb, cols), lambda r, pos: (0, r, 0))],
            out_specs=pl.BlockSpec((rb, cols), lambda r, pos: (r, 0))),
        out_shape=jax.ShapeDtypeStruct((rows, cols), F32),
        compiler_params=pltpu.CompilerParams(dimension_semantics=("arbitrary",)),
    )(pos, mine, recv, chips)


LOSS_ROW = V_ROWS + 8
GB_BASE = LOSS_ROW + 8


def _route_mod_grad_steps(cols):
    def steps(gmod_ref, sct_ref, gadaw_ref, sendbuf, grecv, send_a, recv_a):
        me = _position()
        me_lin = _linear(me)
        gm = gmod_ref[...]
        for b in range(N_DEV):
            sendbuf[b] = jnp.broadcast_to(gm[b:b + 1, :], (HALO, cols))
        grecv[me_lin] = sendbuf[me_lin]

        def row_copy(k, dst_lin):
            peer = _flip(me, k)
            return pltpu.make_async_remote_copy(
                src_ref=sendbuf.at[_linear(peer)], dst_ref=grecv.at[dst_lin], send_sem=send_a.at[k - 1],
                recv_sem=recv_a.at[k - 1], device_id=peer, device_id_type=MESH)

        _exchange_all(lambda k: row_copy(k, me_lin), lambda k: row_copy(k, _linear(_flip(me, k))))
        g_all = grecv[:, 0, :]
        g_pad = jnp.concatenate([g_all, jnp.zeros((sct_ref.shape[1] - N_DEV, cols), F32)], axis=0).astype(BF16)
        gadaw_ref[...] = _dot(sct_ref[...], g_pad)
        return _colsum(g_all)

    return steps


def _small_grad_exchange(gmod8, sc_t, msg_vec, msg_gate, after):
    cols = gmod8.shape[1]
    vec_rows = GB_BASE + N_DEV
    route_steps = _route_mod_grad_steps(cols)

    def body(gmod_ref, sct_ref, vec_ref, gate_ref, after_ref, gadaw_ref, sumv_ref, sumg_ref,
             sendbuf, grecv, send_a, recv_a, myv, myg, sibv, sibg, chipv, chipg, sib_send, sib_recv, peer_send, peer_recv):
        gb = route_steps(gmod_ref, sct_ref, gadaw_ref, sendbuf, grecv, send_a, recv_a)
        x, y, c = me = _position()
        my_chip = 2 * x + y
        myv[0:GB_BASE, :] = vec_ref[...]
        slot = lax.broadcasted_iota(jnp.int32, (N_DEV, D), 0) == _linear(me)
        gb_wide = jnp.concatenate([jnp.broadcast_to(gb, (N_DEV, cols)), jnp.zeros((N_DEV, D - cols), F32)], axis=1)
        myv[GB_BASE:vec_rows, :] = jnp.where(slot, gb_wide, 0.0)
        myg[...] = gate_ref[...]

        swaps = [pltpu.make_async_remote_copy(
            src_ref=src, dst_ref=dst, send_sem=sib_send.at[a], recv_sem=sib_recv.at[a], device_id=(x, y, 1 - c),
            device_id_type=MESH) for a, (src, dst) in enumerate(((myv, sibv), (myg, sibg)))]
        for cp in swaps:
            cp.start()
        for cp in swaps:
            cp.wait_recv()
        chipv[my_chip] = myv[...] + sibv[...]
        chipg[my_chip] = myg[...] + sibg[...]

        def chip_copy(a, buf, j, k, slot_chip):
            peer = _flip(me, k)
            return pltpu.make_async_remote_copy(
                src_ref=buf.at[slot_chip], dst_ref=buf.at[slot_chip], send_sem=peer_send.at[3 * a + j],
                recv_sem=peer_recv.at[3 * a + j], device_id=peer, device_id_type=MESH)

        sends = [chip_copy(a, buf, j, k, my_chip) for a, buf in enumerate((chipv, chipg)) for j, k in enumerate(CHIP_FLIPS)]
        for cp in sends:
            cp.start()
        for a, buf in enumerate((chipv, chipg)):
            for j, k in enumerate(CHIP_FLIPS):
                peer = _flip(me, k)
                chip_copy(a, buf, j, k, 2 * peer[0] + peer[1]).wait_recv()
        sumv_ref[...] = ((chipv[0] + chipv[1]) + chipv[2]) + chipv[3]
        sumg_ref[...] = ((chipg[0] + chipg[1]) + chipg[2]) + chipg[3]
        for cp in swaps + sends:
            cp.wait_send()

    vshape, gshape = (vec_rows, D), msg_gate.shape
    return pl.pallas_call(
        body,
        name="small_grad_exchange",
        in_specs=[WHOLE] * 5,
        out_specs=[WHOLE] * 3,
        out_shape=[jax.ShapeDtypeStruct((D, cols), F32), jax.ShapeDtypeStruct(vshape, F32),
                   jax.ShapeDtypeStruct(gshape, F32)],
        scratch_shapes=[pltpu.VMEM((N_DEV, HALO, cols), F32), pltpu.VMEM((N_DEV, HALO, cols), F32),
                        pltpu.SemaphoreType.DMA((N_DEV - 1,)), pltpu.SemaphoreType.DMA((N_DEV - 1,)),
                        pltpu.VMEM(vshape, F32), pltpu.VMEM(gshape, F32), pltpu.VMEM(vshape, F32),
                        pltpu.VMEM(gshape, F32), pltpu.VMEM((4,) + vshape, F32), pltpu.VMEM((4,) + gshape, F32),
                        pltpu.SemaphoreType.DMA((2,)), pltpu.SemaphoreType.DMA((2,)),
                        pltpu.SemaphoreType.DMA((2 * len(CHIP_FLIPS),)), pltpu.SemaphoreType.DMA((2 * len(CHIP_FLIPS),))],
        compiler_params=pltpu.CompilerParams(vmem_limit_bytes=VMEM_LIMIT),
    )(gmod8, sc_t, msg_vec, msg_gate, after)


def _adamw_math(w, g, m, v):
    m = ADAM_B1 * m + (1.0 - ADAM_B1) * g
    v = ADAM_B2 * v + (1.0 - ADAM_B2) * (g * g)
    m_hat = m / (1.0 - ADAM_B1 ** ADAM_STEP)
    v_hat = v / (1.0 - ADAM_B2 ** ADAM_STEP)
    delta = -ADAM_LR * (m_hat / (jnp.sqrt(v_hat) + ADAM_EPS) + ADAM_WD * w)
    return delta, m, v


def _adamw(name, w, g, m, v):
    rows, cols = w.shape
    rb = 256 if rows % 256 == 0 else rows

    def body(w_ref, g_ref, m_ref, v_ref, d_ref, mo_ref, vo_ref):
        d_ref[...], mo_ref[...], vo_ref[...] = _adamw_math(w_ref[...], g_ref[...], m_ref[...], v_ref[...])

    spec = pl.BlockSpec((rb, cols), lambda r: (r, 0))
    return pl.pallas_call(
        body,
        name="adamw_" + name,
        grid=(rows // rb,),
        in_specs=[spec] * 4,
        out_specs=[spec] * 3,
        out_shape=[jax.ShapeDtypeStruct((rows, cols), F32)] * 3,
        compiler_params=pltpu.CompilerParams(dimension_semantics=("arbitrary",)),
    )(w, g, m, v)


def _update(pos, sum_jobs, plain_jobs, after):
    rb = 256
    jobs = [("sum", j) for j in sum_jobs] + [("plain", j) for j in plain_jobs]
    offs, total = [], 0
    for _, j in jobs:
        offs.append(total)
        total += j[-1].shape[0] // rb
    n_in = sum(len(j) for _, j in jobs)

    def body(pos_ref, *refs):
        ins, outs = refs[:n_in], refs[n_in + 1:]
        s = pl.program_id(0)
        i_in = i_out = 0
        for (kind, j), off in zip(jobs, offs):
            steps = j[-1].shape[0] // rb
            j_in = ins[i_in:i_in + len(j)]
            i_in += len(j)
            j_out = outs[i_out:i_out + (4 if kind == "sum" else 3)]
            i_out += len(j_out)

            @pl.when((s >= off) & (s < off + steps))
            def _(kind=kind, j_in=j_in, j_out=j_out):
                if kind == "sum":
                    mine_ref, recv_ref, chips_ref, w_ref, m_ref, v_ref = j_in
                    g = mine_ref[0, 0] + recv_ref[0]
                    for q in range(len(CHIP_FLIPS)):
                        g = g + chips_ref[q].astype(F32)
                    j_out[0][...] = g
                    rest = j_out[1:]
                else:
                    g_ref, w_ref, m_ref, v_ref = j_in
                    g = g_ref[...]
                    rest = j_out
                rest[0][...], rest[1][...], rest[2][...] = _adamw_math(w_ref[...], g, m_ref[...], v_ref[...])

    in_specs, out_specs, out_shape, args = [], [], [], []
    for (kind, j), off in zip(jobs, offs):
        rows, cols = j[-1].shape
        steps = rows // rb
        blk = lambda s, off=off, steps=steps: jnp.clip(s - off, 0, steps - 1)
        flat = pl.BlockSpec((rb, cols), lambda s, pos, blk=blk: (blk(s), 0))
        if kind == "sum":
            in_specs += [pl.BlockSpec((1, 1, rb, cols), lambda s, pos, blk=blk: (pos[1], 0, blk(s), 0)),
                         pl.BlockSpec((1, rb, cols), lambda s, pos, blk=blk: (pos[1], blk(s), 0)),
                         pl.BlockSpec((len(CHIP_FLIPS), rb, cols), lambda s, pos, blk=blk: (0, blk(s), 0))]
            in_specs += [flat] * 3
        else:
            in_specs += [flat] * 4
        n_res = 4 if kind == "sum" else 3
        out_specs += [flat] * n_res
        out_shape += [jax.ShapeDtypeStruct((rows, cols), F32)] * n_res
        args += list(j)
    in_specs += [pl.BlockSpec(after.shape, lambda s, pos: (0,) * after.ndim)]
    outs = pl.pallas_call(
        body,
        name="update",
        grid_spec=pltpu.PrefetchScalarGridSpec(
            num_scalar_prefetch=1, grid=(total,), in_specs=in_specs, out_specs=out_specs),
        out_shape=out_shape,
        compiler_params=pltpu.CompilerParams(dimension_semantics=("arbitrary",), vmem_limit_bytes=VMEM_LIMIT),
    )(pos, *args, after)
    sums = [tuple(outs[4 * i:4 * i + 4]) for i in range(len(sum_jobs))]
    base = 4 * len(sum_jobs)
    plains = [tuple(outs[base + 3 * i:base + 3 * i + 3]) for i in range(len(plain_jobs))]
    return sums, plains


def _adamw_small(ws, gs, ms, vs, sigmoid_scaled):
    n = len(ws)

    def body(*refs):
        w_refs, g_refs, m_refs, v_refs = (refs[i * n:(i + 1) * n] for i in range(4))
        outs = refs[4 * n:]
        for i in range(n):
            w = w_refs[i][...]
            g = g_refs[i][...]
            if sigmoid_scaled[i]:
                g = g * _sigmoid(w)
            delta, m, v = _adamw_math(w, g, m_refs[i][...], v_refs[i][...])
            outs[4 * i][...] = g
            outs[4 * i + 1][...] = delta
            outs[4 * i + 2][...] = m
            outs[4 * i + 3][...] = v

    shapes = [jax.ShapeDtypeStruct(w.shape, F32) for w in ws for _ in range(4)]
    outs = pl.pallas_call(
        body,
        name="adamw_small",
        in_specs=[WHOLE] * (4 * n),
        out_specs=[WHOLE] * (4 * n),
        out_shape=shapes,
    )(*ws, *gs, *ms, *vs)
    return [outs[4 * i:4 * i + 4] for i in range(n)]


_WEIGHT_NAMES = ("ada_w", "ada_b", "norm1_g", "w_in", "lru_conv_w", "lru_conv_b", "gate_a_w", "gate_a_b", "gate_x_w",
                 "gate_x_b", "a_param", "short_conv_w", "lru_out_g", "conv_out_g", "w_out", "norm2_g", "w_mlp1",
                 "w_mlp2", "final_g")


def kernel(x, c, ada_w, ada_b, norm1_g, w_in, lru_conv_w, lru_conv_b, gate_a_w, gate_a_b, gate_x_w, gate_x_b, a_param, short_conv_w, lru_out_g, conv_out_g, w_out, norm2_g, w_mlp1, w_mlp2, final_g, loss_target, m_ada_w, m_ada_b, m_norm1_g, m_w_in, m_lru_conv_w, m_lru_conv_b, m_gate_a_w, m_gate_a_b, m_gate_x_w, m_gate_x_b, m_a_param, m_short_conv_w, m_lru_out_g, m_conv_out_g, m_w_out, m_norm2_g, m_w_mlp1, m_w_mlp2, m_final_g, v_ada_w, v_ada_b, v_norm1_g, v_w_in, v_lru_conv_w, v_lru_conv_b, v_gate_a_w, v_gate_a_b, v_gate_x_w, v_gate_x_b, v_a_param, v_short_conv_w, v_lru_out_g, v_conv_out_g, v_w_out, v_norm2_g, v_w_mlp1, v_w_mlp2, v_final_g):
    given = dict(locals())
    weights = {n: given[n] for n in _WEIGHT_NAMES}
    xi, yi, ci = _position()
    me_lin = _linear((xi, yi, ci))
    hd = W // N_DEV

    mixer_block = jnp.concatenate([w_out[0], w_in[0].T], axis=0).astype(BF16)
    mlp_block = jnp.concatenate([w_mlp1[0].T, w_mlp2[0]], axis=0).astype(BF16)
    w_send, w_recv, w_thru, w_land, token = _chips_start("weights", [mlp_block], stage="gather")

    msg = (jnp.pad(c, ((0, HALO - 1), (0, 0)))
           + jnp.pad(lru_conv_w[0], ((1, HALO - 1 - CONV_L), (0, D - hd)))
           + jnp.pad(short_conv_w[0], ((1 + CONV_L, 0), (0, D - hd)))
           + jnp.tile(token, (1, D // token.shape[1])))
    gath, mod_all, wmix = _gather_and_mod(msg, ada_w[0], mixer_block)
    (mlp_block,), (wmlp,) = _chips_wait("weights", w_send, w_recv, w_thru, w_land, gath[0, :, 0:128], stage="gather")
    sc_all = gath[:, 0, :]
    wl = jnp.transpose(gath[:, 1:1 + CONV_L, :hd], (1, 0, 2)).reshape(CONV_L, W)
    ws = jnp.transpose(gath[:, 1 + CONV_L:HALO, :hd], (1, 0, 2)).reshape(CONV_S, W)
    modraw = _pad_rows(mod_all[:, 0, :].reshape(6, D), HALO)
    adab = _pad_rows(ada_b.reshape(6, D), HALO)

    x2d, tgt = x[0], loss_target[0]
    gf = final_g.reshape(1, D)
    bda = _block_diag(gate_a_w[0]).astype(BF16)
    bdx = _block_diag(gate_x_w[0]).astype(BF16)
    avg = _block_diag(jnp.full((8, W // 8, W // 8), 8.0 / W, F32)).astype(BF16)
    wl8 = _pad_rows(wl, HALO)
    ws8 = _pad_rows(ws, HALO)
    mixer_small = (wl8, lru_conv_b, bda, bdx, gate_a_b, gate_x_b, a_param, ws8, lru_out_g, conv_out_g, avg)
    proj, hl, mixed, kept, wmlp = _mixer_fwd(x2d, modraw, adab, norm1_g, *mixer_small, wmix, mlp_block, wmlp)
    h2t, f, dx2, dz, vec2, loss8 = _mlp_fwd(x2d, mixed, tgt, modraw, adab, norm2_g, gf, wmlp)
    pos = jnp.stack([ci, 2 * xi + yi]).astype(jnp.int32)
    by_dest = lambda g: g.reshape((4, 2, -1) + g.shape[-1:])
    dh2_first, *for_sibling = _mlp_bwd_half(pos, h2t, f, dz, wmlp)
    sib_send, sib_recv, sib_thru, sib_land, token = _chips_start("mlp", for_sibling, stage="sibling")
    dh2, dw1, dw2 = _mlp_bwd_half(pos, h2t, f, dz, wmlp, prior=(dh2_first, token))
    done = dh2[0:HALO, 0:128] + dw1[0, 0:HALO, 0:128] + dw2[0, 0:HALO, 0:128]
    _, mlp_sib = _chips_wait("mlp", sib_send, sib_recv, sib_thru, sib_land, done, stage="sibling")
    mlp_parts = [dw1[:, None], dw2[:, None]]
    mlp_sums = [_pair_sum(pos, p, r) for p, r in zip(mlp_parts, mlp_sib)]
    mlp_send, mlp_recv, mlp_thru, mlp_land, token = _chips_start("mlp", mlp_sums)
    modraw_after = modraw + jnp.tile(token, (1, D // token.shape[1]))
    gx, vec, hb, dproj_t, dmixed, ycat_t, xl_t, dgate = _mixer_bwd(
        x2d, mixed, dh2, dx2, proj, hl, kept, modraw_after, adab, norm1_g, norm2_g, *mixer_small, wmix, [], None)
    dwint = _matmul("wgrad_in", dproj_t, hb)
    dwout = _matmul("wgrad_out", ycat_t, dmixed)
    gate_blocks = _gate_wgrad(xl_t, dgate, avg)
    msg_gate = gate_blocks.reshape(W, 128)
    done = dwint[0:HALO, 0:128] + dwout[0:HALO, 0:128] + gate_blocks[0, 0:HALO, :].sum() + gx[0:HALO, 0:128]
    _, mlp_chips = _chips_wait("mlp", mlp_send, mlp_recv, mlp_thru, mlp_land, done)
    mix_parts = [by_dest(dwout), by_dest(dwint)]
    gmod8 = (jnp.pad(vec[0:5], ((0, 1), (0, 0))) + jnp.pad(vec2[0:1], ((5, 0), (0, 0)))).reshape(N_DEV, 6 * D // N_DEV)
    sc_t = jnp.pad(sc_all.T, ((0, 0), (0, 128 - N_DEV))).astype(BF16)
    loss_rows = jnp.pad(loss8[0:1], ((0, HALO - 1), (0, D - loss8.shape[1])))
    msg_vec = jnp.concatenate([vec, vec2, loss_rows], axis=0)
    sib_send, sib_recv, sib_thru, sib_land, token = _chips_start("mixer", mix_parts, stage="sibling")
    g_adaw, sum_vec, sum_gate = _small_grad_exchange(gmod8, sc_t, msg_vec, msg_gate, token)
    mix_parts, mix_sib = _chips_wait("mixer", sib_send, sib_recv, sib_thru, sib_land, sum_vec[0:HALO, 0:128],
                                     stage="sibling")
    mix_sums = [_pair_sum(pos, p, r) for p, r in zip(mix_parts, mix_sib)]
    state = lambda n: (weights[n][0], given["m_" + n][0], given["v_" + n][0])
    mlp_jobs = [(p, r, q, *state(n)) for p, r, q, n in zip(mlp_parts, mlp_sib, mlp_chips, ("w_mlp1", "w_mlp2"))]
    mix_send, mix_recv, mix_thru, mix_land, token = _chips_start("mixer", mix_sums)
    mlp_done, (adaw_done,) = _update(pos, mlp_jobs, [(g_adaw, *state("ada_w"))], token)
    loss = sum_vec[LOSS_ROW, 0]
    sum_gate = sum_gate.reshape(2, W, W // 8)
    lo, hi = slice(0, W), slice(W, 2 * W)
    wl_full = sum_vec[V_WL01:V_WL23 + 1].reshape(CONV_L, W)
    ws_full = sum_vec[V_WS01:V_WS2 + 1].reshape(CONV_S + 1, W)[:CONV_S]
    row = lambda r, cols: sum_vec[r:r + 1, cols]
    small_grads = {
        "ada_b": sum_vec[GB_BASE:GB_BASE + N_DEV, :6 * D // N_DEV].reshape(1, 6 * D),
        "norm1_g": row(V_G1, slice(0, D)),
        "lru_conv_w": lax.dynamic_slice(wl_full, (0, me_lin * hd), (CONV_L, hd)),
        "lru_conv_b": row(V_BL_BA, lo),
        "gate_a_w": sum_gate[0],
        "gate_a_b": row(V_BL_BA, hi),
        "gate_x_w": sum_gate[1],
        "gate_x_b": row(V_BX_SP, lo),
        "a_param": row(V_BX_SP, hi),
        "short_conv_w": lax.dynamic_slice(ws_full, (0, me_lin * hd), (CONV_S, hd)),
        "lru_out_g": row(V_GL_GC, lo),
        "conv_out_g": row(V_GL_GC, hi),
        "norm2_g": row(V_G2, slice(0, D)),
        "final_g": sum_vec[V_ROWS + 1:V_ROWS + 2, :],
    }
    names = list(small_grads)
    as2d = lambda a, n: a.reshape(small_grads[n].shape)
    small = _adamw_small([as2d(weights[n], n) for n in names], [small_grads[n] for n in names],
                         [as2d(given["m_" + n], n) for n in names], [as2d(given["v_" + n], n) for n in names],
                         [n == "a_param" for n in names])
    result = {n: tuple(o.reshape(weights[n].shape) for o in outs) for n, outs in zip(names, small)}

    done = (small[0][1][:, 0:128] + mlp_done[0][2][0:HALO, 0:128] + mlp_done[1][2][0:HALO, 0:128]
            + adaw_done[1][0:HALO, 0:128])
    _, mix_chips = _chips_wait("mixer", mix_send, mix_recv, mix_thru, mix_land, done)
    g_wout, g_wint = (_final_sum(pos, p, r, q) for p, r, q in zip(mix_parts, mix_sib, mix_chips))
    for n, g in (("w_in", g_wint.T), ("w_out", g_wout)):
        w, m, v = state(n)
        result[n] = (g[None],) + tuple(o[None] for o in _adamw(n, w, g, m, v))
    result["w_mlp1"], result["w_mlp2"] = (tuple(o[None] for o in done) for done in mlp_done)
    result["ada_w"] = (g_adaw[None],) + tuple(o[None] for o in adaw_done)

    return (loss, gx[None], *[result[n][0] for n in _WEIGHT_NAMES], *[result[n][1] for n in _WEIGHT_NAMES],
            *[result[n][2] for n in _WEIGHT_NAMES], *[result[n][3] for n in _WEIGHT_NAMES])
```

```python
import jax
import jax.numpy as jnp
from jax import lax
from jax.experimental import pallas as pl
from jax.experimental.pallas import tpu as pltpu

F32 = jnp.float32
BF16 = jnp.bfloat16
MESH = pl.DeviceIdType.MESH

N_DEV = 8
D = 1024
W = 512
D_IN = 5 * W
D_FF = 4096
FF_BLK = D_FF // N_DEV
EPS = 1e-6
C_GATE = 8.0
CONV_L = 4
CONV_S = 3
HALO = 8

ROWS_W1T, ROWS_W2, ROWS_WOUT, ROWS_WIN = FF_BLK, FF_BLK, D // N_DEV, D_IN // N_DEV
OFF_WOUT = 0
OFF_WIN = OFF_WOUT + ROWS_WOUT
MIX_ROWS = OFF_WIN + ROWS_WIN
OFF_W1T = 0
OFF_W2 = OFF_W1T + ROWS_W1T
MLP_ROWS = OFF_W2 + ROWS_W2
CHIP_FLIPS = (4, 2, 6)
N_KEPT = 6

ADAM_LR = 0.001
ADAM_B1 = 0.9
ADAM_B2 = 0.999
ADAM_EPS = 1e-08
ADAM_WD = 0.01
ADAM_STEP = 10

VMEM_LIMIT = 56 * 1024 * 1024

TB_MIX = 256
TB_MIXB = 256
TB_MLP = 256
TB_MLPB = 512

ANY = pl.BlockSpec(memory_space=pl.ANY)
WHOLE = pl.BlockSpec(memory_space=pltpu.VMEM)


def _dot(a, b):
    return jnp.dot(a, b, preferred_element_type=F32)


def _dot_nt(a, b):
    return lax.dot_general(a, b, (((1,), (1,)), ((), ())), preferred_element_type=F32)


def _dot_tn(a, b):
    return lax.dot_general(a, b, (((0,), (0,)), ((), ())), preferred_element_type=F32)


def _sigmoid(v):
    return 1.0 / (1.0 + jnp.exp(-v))


def _softplus(v):
    t = jnp.exp(-jnp.abs(v))
    small = t * (1.0 - t * (0.5 - t * (1.0 / 3.0)))
    return jnp.maximum(v, 0.0) + jnp.where(t < 1e-2, small, jnp.log(1.0 + t))


def _one_minus_sq(a, log_a):
    return -jnp.tanh(log_a) * (a * a + 1.0)


_GELU_K = 0.7978845608028654
_GELU_C = 0.044715


def _gelu(u):
    th = jnp.tanh(_GELU_K * (u + _GELU_C * u * u * u))
    return 0.5 * u * (1.0 + th), th


def _gelu_grad(u, th):
    return 0.5 * (1.0 + th) + 0.5 * u * (1.0 - th * th) * _GELU_K * (1.0 + 3.0 * _GELU_C * u * u)


def _group_mean(v, avg):
    hi = v.astype(BF16)
    lo = (v - hi.astype(F32)).astype(BF16)
    return _dot(hi, avg) + _dot(lo, avg)


def _colsum(v):
    return jnp.sum(v, axis=0, keepdims=True)


def _rowmean(v):
    return jnp.mean(v, axis=-1, keepdims=True)


def _load_packed(wpack_hbm, off, rows, dst, sem):
    copies = [
        pltpu.make_async_copy(wpack_hbm.at[d, pl.ds(off, rows), :], dst.at[pl.ds(d * rows, rows), :], sem)
        for d in range(N_DEV)
    ]
    for cp in copies:
        cp.start()
    return copies


def _scan_groups(n_groups, a_ref, b_ref, out_ref, carry_ref, reverse):
    row = lax.broadcasted_iota(jnp.int32, (HALO, W), 0)

    def step(k, carry):
        g = (n_groups - 1 - k) if reverse else k
        rows = pl.ds(pl.multiple_of(g * HALO, HALO), HALO)
        a = a_ref[rows, :]
        b = b_ref[rows, :]
        for s in (1, 2, 4):
            if reverse:
                keep = row < HALO - s
                sh = HALO - s
            else:
                keep = row >= s
                sh = s
            a_sh = pltpu.roll(a, sh, axis=0)
            b_sh = pltpu.roll(b, sh, axis=0)
            b = jnp.where(keep, a * b_sh + b, b)
            a = jnp.where(keep, a * a_sh, a)
        h = b + a * carry
        out_ref[rows, :] = h
        edge = h[0:1, :] if reverse else h[HALO - 1:HALO, :]
        return jnp.broadcast_to(edge, (HALO, W))

    carry_ref[...] = lax.fori_loop(0, n_groups, step, carry_ref[...])


def _route_peers(me):
    x, y, c = me
    first = ((x + 1 - c) % 2, (y + c) % 2, c)
    second = ((x + c) % 2, (y + 1 - c) % 2, c)
    return first, second, (1 - x, 1 - y, c)


def _chip_gather_copies(block_hbm, out_hbm, send_sems, recv_sems):
    me = _position()
    first, second, diag = _route_peers(me)

    def copy(j, src, slot_of, to):
        return pltpu.make_async_remote_copy(
            src_ref=src, dst_ref=out_hbm.at[_linear(slot_of)], send_sem=send_sems.at[j], recv_sem=recv_sems.at[j],
            device_id=to, device_id_type=MESH)

    own_sends = [copy(0, block_hbm, me, first), copy(1, block_hbm, me, second)]
    forward = copy(2, out_hbm.at[_linear(first)], first, second)
    arrivals = [copy(0, block_hbm, first, first), copy(1, block_hbm, second, second), copy(2, block_hbm, diag, second)]
    return own_sends, forward, arrivals


def _mixer_fwd(x, modraw, adab, g1, wl, bl, bda, bdx, ba, bxb, ap, ws, gl, gc, avg, wpack, mlp_block):
    t_len = x.shape[0]
    tb = TB_MIX
    nb = t_len // tb

    def body(x_ref, modraw_ref, adab_ref, g1_ref, wl_ref, bl_ref, bda_ref, bdx_ref, ba_ref, bxb_ref, ap_ref,
             ws_ref, gl_ref, gc_ref, avg_ref, wpack_hbm, block_hbm, proj_ref, hl_ref, mixed_ref, kept_ref, wmlp_hbm,
             win_v, wout_v, sem, ulx_ext, cv_ext, hcar, a_s, b_s, send_sems, recv_sems, sib_send_sems, sib_recv_sems,
             local_sem):
        i = pl.program_id(0)
        x_pos, y_pos, c_pos = me = _position()
        sibling = (x_pos, y_pos, 1 - c_pos)
        own = pltpu.make_async_copy(block_hbm, wmlp_hbm.at[_linear(me)], local_sem)
        sends, forward, arrivals = _chip_gather_copies(block_hbm, wmlp_hbm, send_sems, recv_sems)

        def to_sibling(j, block_of, src=None):
            dst = wmlp_hbm.at[_linear(block_of)]
            return pltpu.make_async_remote_copy(
                src_ref=dst if src is None else src, dst_ref=dst, send_sem=sib_send_sems.at[j],
                recv_sem=sib_recv_sems.at[j], device_id=sibling, device_id_type=MESH)

        passes = [to_sibling(0, me, src=block_hbm)] + [to_sibling(1 + j, p) for j, p in enumerate(_route_peers(me))]

        @pl.when(i == 0)
        def _():
            own.start()
            for cp in sends:
                cp.start()
            passes[0].start()

        @pl.when(i == nb - 1)
        def _():
            arrivals[0].wait_recv()
            forward.start()
            passes[1].start()

        @pl.when(i == 0)
        def _():
            cps = _load_packed(wpack_hbm, OFF_WIN, ROWS_WIN, win_v, sem.at[0])
            cps += _load_packed(wpack_hbm, OFF_WOUT, ROWS_WOUT, wout_v, sem.at[1])
            ulx_ext[0:HALO, :] = jnp.zeros((HALO, W), F32)
            cv_ext[0:HALO, :] = jnp.zeros((HALO, W), F32)
            hcar[...] = jnp.zeros((HALO, W), F32)
            for cp in cps:
                cp.wait()

        mod = modraw_ref[...] + adab_ref[...]
        shift1, scale1, gate1 = mod[0:1], mod[1:2], mod[2:3]
        x = x_ref[...]
        r1 = lax.rsqrt(_rowmean(x * x) + EPS)
        h = (x * r1 * g1_ref[...]) * (1.0 + scale1) + shift1
        proj = _dot_nt(h.astype(BF16), win_v[...])
        proj_ref[...] = proj
        u_lx, u_ly, u_b, u_c, u_v = (proj[:, k * W:(k + 1) * W] for k in range(5))

        ulx_ext[HALO:HALO + tb, :] = u_lx
        xl = bl_ref[...] + wl_ref[CONV_L - 1:CONV_L, :] * u_lx
        for k in range(CONV_L - 1):
            xl = xl + wl_ref[k:k + 1, :] * ulx_ext[pl.ds(HALO - (CONV_L - 1) + k, tb), :]
        ulx_ext[0:HALO, :] = ulx_ext[tb:tb + HALO, :]
        xlb = xl.astype(BF16)
        r = _sigmoid(_dot(xlb, bda_ref[...]) + ba_ref[...])
        ig = _sigmoid(_dot(xlb, bdx_ref[...]) + bxb_ref[...])
        log_a = (-C_GATE) * r * _softplus(ap_ref[...])
        a = jnp.exp(log_a)
        mult = jnp.sqrt(_one_minus_sq(a, log_a))
        grow = i * tb + lax.broadcasted_iota(jnp.int32, (tb, W), 0)
        mult = jnp.where(grow == 0, 1.0, mult)
        a_s[...] = a
        b_s[...] = mult * (ig * xl)
        _scan_groups(tb // HALO, a_s, b_s, hl_ref, hcar, reverse=False)
        hl = hl_ref[...]
        ge, _ = _gelu(u_ly)
        p = ge * hl
        rp = lax.rsqrt(_group_mean(p * p, avg_ref[...]) + EPS)
        y_lru = p * rp * gl_ref[...]

        cv = u_c * u_v
        cv_ext[HALO:HALO + tb, :] = cv
        cc = ws_ref[CONV_S - 1:CONV_S, :] * cv
        for k in range(CONV_S - 1):
            cc = cc + ws_ref[k:k + 1, :] * cv_ext[pl.ds(HALO - (CONV_S - 1) + k, tb), :]
        cv_ext[0:HALO, :] = cv_ext[tb:tb + HALO, :]
        q = u_b * cc
        rq = lax.rsqrt(_group_mean(q * q, avg_ref[...]) + EPS)
        y_conv = q * rq * gc_ref[...]
        for k, kept in enumerate((xl, r, ig, rp, rq, cc)):
            kept_ref[:, k * W:(k + 1) * W] = kept

        mixed_ref[...] = (_dot(y_lru.astype(BF16), wout_v[0:W, :]) + _dot(y_conv.astype(BF16), wout_v[W:2 * W, :]))

        @pl.when(i == nb - 1)
        def _():
            arrivals[1].wait_recv()
            passes[2].start()
            arrivals[2].wait_recv()
            passes[3].start()
            for j, p in enumerate((sibling,) + _route_peers(sibling)):
                to_sibling(j, p).wait_recv()
            for cp in sends + [forward] + passes:
                cp.wait_send()
            own.wait()

    tok = lambda cols: pl.BlockSpec((tb, cols), lambda i: (i, 0))
    full = lambda a: pl.BlockSpec(a.shape, lambda i: (0,) * a.ndim)
    small = (modraw, adab, g1, wl, bl, bda, bdx, ba, bxb, ap, ws, gl, gc, avg)
    n_chips = len(CHIP_FLIPS)
    return pl.pallas_call(
        body,
        name="mixer_fwd",
        grid=(nb,),
        in_specs=[tok(D)] + [full(a) for a in small] + [ANY, ANY],
        out_specs=[tok(D_IN), tok(W), tok(D), tok(N_KEPT * W), ANY],
        out_shape=[jax.ShapeDtypeStruct((t_len, D_IN), F32), jax.ShapeDtypeStruct((t_len, W), F32),
                   jax.ShapeDtypeStruct((t_len, D), F32), jax.ShapeDtypeStruct((t_len, N_KEPT * W), F32),
                   jax.ShapeDtypeStruct((N_DEV,) + mlp_block.shape, BF16)],
        scratch_shapes=[pltpu.VMEM((D_IN, D), BF16), pltpu.VMEM((D, D), BF16), pltpu.SemaphoreType.DMA((2,)),
                        pltpu.VMEM((tb + HALO, W), F32), pltpu.VMEM((tb + HALO, W), F32), pltpu.VMEM((HALO, W), F32),
                        pltpu.VMEM((tb, W), F32), pltpu.VMEM((tb, W), F32),
                        pltpu.SemaphoreType.DMA((n_chips,)), pltpu.SemaphoreType.DMA((n_chips,)),
                        pltpu.SemaphoreType.DMA((4,)), pltpu.SemaphoreType.DMA((4,)), pltpu.SemaphoreType.DMA],
        compiler_params=pltpu.CompilerParams(dimension_semantics=("arbitrary",), vmem_limit_bytes=VMEM_LIMIT),
    )(x, *small, wpack, mlp_block)


def _mlp_fwd(x, mixed, tgt, modraw, adab, g2, gf, wpack):
    t_len = x.shape[0]
    tb = TB_MLP
    nb = t_len // tb

    def body(x_ref, mixed_ref, tgt_ref, modraw_ref, adab_ref, g2_ref, gf_ref, wpack_hbm,
             h2t_ref, f_ref, dx2_ref, dz_ref, vec_ref, loss_ref, w1t_v, w2_v, sem):
        i = pl.program_id(0)

        @pl.when(i == 0)
        def _():
            cps = _load_packed(wpack_hbm, OFF_W1T, ROWS_W1T, w1t_v, sem.at[0])
            cps += _load_packed(wpack_hbm, OFF_W2, ROWS_W2, w2_v, sem.at[1])
            vec_ref[...] = jnp.zeros(vec_ref.shape, F32)
            loss_ref[...] = jnp.zeros(loss_ref.shape, F32)
            for cp in cps:
                cp.wait()

        mod = modraw_ref[...] + adab_ref[...]
        gate1, shift2, scale2, gate2 = mod[2:3], mod[3:4], mod[4:5], mod[5:6]
        x1 = x_ref[...] + gate1 * mixed_ref[...]
        r2 = lax.rsqrt(_rowmean(x1 * x1) + EPS)
        h2 = (x1 * r2 * g2_ref[...]) * (1.0 + scale2) + shift2
        h2b = h2.astype(BF16)
        h2t_ref[...] = h2.T.astype(BF16)
        z = jnp.zeros((tb, D), F32)
        for j in range(N_DEV):
            cols = slice(j * FF_BLK, (j + 1) * FF_BLK)
            fj = _dot_nt(h2b, w1t_v[cols, :])
            f_ref[:, cols] = fj
            rf = jnp.maximum(fj, 0.0)
            z = z + _dot((rf * rf).astype(BF16), w2_v[cols, :])
        x2 = x1 + gate2 * z
        r3 = lax.rsqrt(_rowmean(x2 * x2) + EPS)
        xn3 = x2 * r3
        diff = xn3 * gf_ref[...] - tgt_ref[...]
        sq = _colsum(diff * diff)
        loss_ref[...] += jnp.broadcast_to(jnp.sum(sq, axis=1, keepdims=True) * (0.5 / D), loss_ref.shape)
        dy = diff * (1.0 / D)
        dyn = dy * gf_ref[...]
        dx2 = r3 * (dyn - xn3 * _rowmean(dyn * xn3))
        dx2_ref[...] = dx2
        dz_ref[...] = (gate2 * dx2).astype(BF16)
        vec_ref[0:1, :] += _colsum(dx2 * z)
        vec_ref[1:2, :] += _colsum(dy * xn3)

    tok = lambda cols: pl.BlockSpec((tb, cols), lambda i: (i, 0))
    tok_t = pl.BlockSpec((D, tb), lambda i: (0, i))
    full = lambda a: pl.BlockSpec(a.shape, lambda i: (0,) * a.ndim)
    small = (modraw, adab, g2, gf)
    return pl.pallas_call(
        body,
        name="mlp_fwd",
        grid=(nb,),
        in_specs=[tok(D), tok(D), tok(D)] + [full(a) for a in small] + [ANY],
        out_specs=[tok_t, tok(D_FF), tok(D), tok(D), pl.BlockSpec((8, D), lambda i: (0, 0)),
                   pl.BlockSpec((8, 128), lambda i: (0, 0))],
        out_shape=[jax.ShapeDtypeStruct((D, t_len), BF16), jax.ShapeDtypeStruct((t_len, D_FF), F32),
                   jax.ShapeDtypeStruct((t_len, D), F32), jax.ShapeDtypeStruct((t_len, D), BF16),
                   jax.ShapeDtypeStruct((8, D), F32), jax.ShapeDtypeStruct((8, 128), F32)],
        scratch_shapes=[pltpu.VMEM((D_FF, D), BF16), pltpu.VMEM((D_FF, D), BF16), pltpu.SemaphoreType.DMA((2,))],
        compiler_params=pltpu.CompilerParams(dimension_semantics=("arbitrary",), vmem_limit_bytes=VMEM_LIMIT),
    )(x, mixed, tgt, *small, wpack)


def _mlp_bwd_half(pos, h2t, f, dz, wpack, prior=None):
    t_len = dz.shape[0]
    tb = TB_MLPB
    nb = t_len // tb
    first = prior is None
    flip = 1 if first else 0

    def body(pos_ref, h2t_ref, f_ref, dz_ref, w1t_ref, w2_ref, *rest):
        if first:
            dh2_ref, dw1_ref, dw2_ref = rest
        else:
            dh2in_ref, _, dh2_ref, dw1_ref, dw2_ref = rest
        k = pl.program_id(0)
        t = pl.program_id(1)
        rows = pl.ds(pl.multiple_of(t * tb, tb), tb)
        w1t = w1t_ref[0]
        w2 = w2_ref[0]
        dz = dz_ref[...]
        rf = jnp.maximum(f_ref[...], 0.0)
        df = (_dot_nt(dz, w2) * (2.0 * rf)).astype(BF16)
        dh = _dot(df, w1t)
        g1 = _dot(h2t_ref[...], df)
        g2 = _dot_tn((rf * rf).astype(BF16), dz)

        @pl.when(t == 0)
        def _():
            dw2_ref[0] = g2
            dw1_ref[0] = g1

        @pl.when(t != 0)
        def _():
            dw2_ref[0] += g2
            dw1_ref[0] += g1

        @pl.when(k == 0)
        def _():
            dh2_ref[rows, :] = dh if first else dh2in_ref[...] + dh

        @pl.when(k != 0)
        def _():
            dh2_ref[rows, :] += dh

    blk = lambda k, pos: 2 * k + jnp.bitwise_xor(pos[0], flip)
    in_specs = [pl.BlockSpec((D, tb), lambda k, t, pos: (0, t)),
                pl.BlockSpec((tb, FF_BLK), lambda k, t, pos: (t, blk(k, pos))),
                pl.BlockSpec((tb, D), lambda k, t, pos: (t, 0)),
                pl.BlockSpec((1, ROWS_W1T, D), lambda k, t, pos: (blk(k, pos), OFF_W1T // ROWS_W1T, 0)),
                pl.BlockSpec((1, ROWS_W2, D), lambda k, t, pos: (blk(k, pos), OFF_W2 // ROWS_W2, 0))]
    grad_specs = [pl.BlockSpec((1, D, FF_BLK), lambda k, t, pos: (k, 0, 0)),
                  pl.BlockSpec((1, FF_BLK, D), lambda k, t, pos: (k, 0, 0))]
    out_specs = [pl.BlockSpec((t_len, D), lambda k, t, pos: (0, 0))] + grad_specs
    grad_shapes = [jax.ShapeDtypeStruct((4, D, FF_BLK), F32), jax.ShapeDtypeStruct((4, FF_BLK, D), F32)]
    out_shape = [jax.ShapeDtypeStruct((t_len, D), F32)] + grad_shapes
    args = [pos, h2t, f, dz, wpack, wpack]
    if not first:
        in_specs += [pl.BlockSpec((tb, D), lambda k, t, pos: (jnp.where(k == 0, t, nb - 1), 0)),
                     pl.BlockSpec(prior[1].shape, lambda k, t, pos: (0,) * prior[1].ndim)]
        args += list(prior)
    return pl.pallas_call(
        body,
        name="mlp_bwd_first" if first else "mlp_bwd_second",
        grid_spec=pltpu.PrefetchScalarGridSpec(num_scalar_prefetch=1, grid=(4, nb), in_specs=in_specs,
                                               out_specs=out_specs),
        out_shape=out_shape,
        compiler_params=pltpu.CompilerParams(dimension_semantics=("arbitrary", "arbitrary"),
                                             vmem_limit_bytes=VMEM_LIMIT),
    )(*args)


V_SHIFT1, V_SCALE1, V_GATE1, V_SHIFT2, V_SCALE2, V_G1, V_G2 = 0, 1, 2, 3, 4, 6, 7
V_BL_BA, V_BX_SP, V_GL_GC, V_WL01, V_WL23, V_WS01, V_WS2 = 8, 9, 10, 11, 12, 13, 14
V_ROWS = 16


def _chip_scatter_copies(srcs, dsts, send_sems, recv_sems):
    me = _position()
    copies = []
    for a, (src, dst) in enumerate(zip(srcs, dsts)):
        for j, k in enumerate(CHIP_FLIPS):
            peer = _flip(me, k)
            copies.append(pltpu.make_async_remote_copy(
                src_ref=src.at[2 * peer[0] + peer[1]], dst_ref=dst.at[j], send_sem=send_sems.at[len(CHIP_FLIPS) * a + j],
                recv_sem=recv_sems.at[len(CHIP_FLIPS) * a + j], device_id=peer, device_id_type=MESH))
    return copies


def _mixer_bwd(x, mixed, dh2, dx2, proj, hl, kept, modraw, adab, g1, g2, wl, bl, bda, bdx, ba, bxb, ap, ws, gl, gc, avg, wpack):
    t_len = x.shape[0]
    tb = TB_MIXB
    nb = t_len // tb
    hb = tb // HALO

    def body(x_ref, mixed_ref, dh2_ref, dx2_ref, proj_ref, projh_ref, hl_ref, hlh_ref, kept_ref,
             modraw_ref, adab_ref, g1_ref, g2_ref, wl_ref, bl_ref, bda_ref, bdx_ref, ba_ref, bxb_ref, ap_ref,
             ws_ref, gl_ref, gc_ref, avg_ref, wpack_hbm,
             gx_ref, vec_ref, hb_ref, dprojt_ref, dmixed_ref, ycatt_ref, xlt_ref, dgate_ref,
             win_v, wout_v, sem, ulx_ext, cv_ext, hl_ext, a_ext, dxl_ext, dcc_ext, dcar, an_s, g_s, dh_s):
        i = pl.program_id(0)
        blk = nb - 1 - i

        @pl.when(i == 0)
        def _():
            cps = _load_packed(wpack_hbm, OFF_WIN, ROWS_WIN, win_v, sem.at[0])
            cps += _load_packed(wpack_hbm, OFF_WOUT, ROWS_WOUT, wout_v, sem.at[1])
            vec_ref[...] = jnp.zeros(vec_ref.shape, F32)
            zero = jnp.zeros((HALO, W), F32)
            a_ext[tb:tb + HALO, :] = zero
            dxl_ext[tb:tb + HALO, :] = zero
            dcc_ext[tb:tb + HALO, :] = zero
            dcar[...] = zero
            for cp in cps:
                cp.wait()

        mod = modraw_ref[...] + adab_ref[...]
        shift1, scale1, gate1, scale2 = mod[0:1], mod[1:2], mod[2:3], mod[4:5]
        x = x_ref[...]
        mixed = mixed_ref[...]

        x1 = x + gate1 * mixed
        r2 = lax.rsqrt(_rowmean(x1 * x1) + EPS)
        xn2 = x1 * r2
        dh2 = dh2_ref[...]
        vec_ref[V_SHIFT2:V_SHIFT2 + 1, :] += _colsum(dh2)
        vec_ref[V_SCALE2:V_SCALE2 + 1, :] += _colsum(dh2 * xn2 * g2_ref[...])
        vec_ref[V_G2:V_G2 + 1, :] += _colsum(dh2 * (1.0 + scale2) * xn2)
        dxn2 = dh2 * g2_ref[...] * (1.0 + scale2)
        dx1 = dx2_ref[...] + r2 * (dxn2 - xn2 * _rowmean(dxn2 * xn2))
        vec_ref[V_GATE1:V_GATE1 + 1, :] += _colsum(dx1 * mixed)
        dmixed = (gate1 * dx1).astype(BF16)

        proj = proj_ref[...]
        u_lx, u_ly, u_b, u_c, u_v = (proj[:, k * W:(k + 1) * W] for k in range(5))
        has_prev = (blk > 0).astype(F32)
        projh = projh_ref[...]
        ulx_ext[0:HALO, :] = projh[:, 0:W] * has_prev
        ulx_ext[HALO:HALO + tb, :] = u_lx
        xl, r, ig, rp, rq, cc = (kept_ref[:, k * W:(k + 1) * W] for k in range(N_KEPT))
        sp = _softplus(ap_ref[...])
        log_a = (-C_GATE) * r * sp
        a = jnp.exp(log_a)
        mult_raw = jnp.sqrt(_one_minus_sq(a, log_a))
        first = (blk * tb + lax.broadcasted_iota(jnp.int32, (tb, W), 0)) == 0
        mult = jnp.where(first, 1.0, mult_raw)
        hl = hl_ref[...]
        ge, th = _gelu(u_ly)
        pn = ge * hl * rp
        cv = u_c * u_v
        cv_ext[0:HALO, :] = projh[:, 3 * W:4 * W] * projh[:, 4 * W:5 * W] * has_prev
        cv_ext[HALO:HALO + tb, :] = cv
        qn = u_b * cc * rq

        dmixed_ref[...] = dmixed
        ycatt_ref[0:W, :] = (pn * gl_ref[...]).T.astype(BF16)
        ycatt_ref[W:2 * W, :] = (qn * gc_ref[...]).T.astype(BF16)
        dyl = _dot_nt(dmixed, wout_v[0:W, :])
        dyc = _dot_nt(dmixed, wout_v[W:2 * W, :])

        dqn = dyc * gc_ref[...]
        dq = rq * (dqn - qn * _group_mean(dqn * qn, avg_ref[...]))
        du_b = dq * cc
        dcc = dq * u_b
        dcc_ext[0:tb, :] = dcc
        dcv = ws_ref[CONV_S - 1:CONV_S, :] * dcc
        for k in range(CONV_S - 1):
            dcv = dcv + ws_ref[k:k + 1, :] * dcc_ext[pl.ds(CONV_S - 1 - k, tb), :]
        dcc_ext[tb:tb + HALO, :] = dcc_ext[0:HALO, :]
        du_c = dcv * u_v
        du_v = dcv * u_c
        dws = [_colsum(dcc * cv_ext[pl.ds(HALO - (CONV_S - 1) + k, tb), :]) for k in range(CONV_S)]

        dpn = dyl * gl_ref[...]
        dp = rp * (dpn - pn * _group_mean(dpn * pn, avg_ref[...]))
        du_ly = dp * hl * _gelu_grad(u_ly, th)
        g_s[...] = dp * ge
        a_ext[0:tb, :] = a
        an_s[...] = a_ext[pl.ds(1, tb), :]
        _scan_groups(hb, an_s, g_s, dh_s, dcar, reverse=True)
        a_ext[tb:tb + HALO, :] = a_ext[0:HALO, :]
        dh = dh_s[...]
        hl_ext[0:HALO, :] = hlh_ref[...] * has_prev
        hl_ext[HALO:HALO + tb, :] = hl
        da = dh * hl_ext[pl.ds(HALO - 1, tb), :]
        dmult = dh * (ig * xl)
        dig = dh * (mult * xl)
        dxl = dh * (mult * ig)
        dlog = da * a - jnp.where(first, 0.0, dmult * (a * a) / mult_raw)
        dr = dlog * ((-C_GATE) * sp)
        dsp = _colsum(dlog * ((-C_GATE) * r))
        dga = dr * r * (1.0 - r)
        dgx = dig * ig * (1.0 - ig)
        dgab = dga.astype(BF16)
        dgxb = dgx.astype(BF16)
        xlt_ref[...] = xl.T.astype(BF16)
        dgate_ref[:, 0:W] = dgab
        dgate_ref[:, W:2 * W] = dgxb
        dxl = dxl + _dot_nt(dgab, bda_ref[...]) + _dot_nt(dgxb, bdx_ref[...])
        dxl_ext[0:tb, :] = dxl
        du_lx = wl_ref[CONV_L - 1:CONV_L, :] * dxl
        for k in range(CONV_L - 1):
            du_lx = du_lx + wl_ref[k:k + 1, :] * dxl_ext[pl.ds(CONV_L - 1 - k, tb), :]
        dxl_ext[tb:tb + HALO, :] = dxl_ext[0:HALO, :]
        dwl = [_colsum(dxl * ulx_ext[pl.ds(HALO - (CONV_L - 1) + k, tb), :]) for k in range(CONV_L)]

        cat = lambda u, v: jnp.concatenate([u, v], axis=1)
        vec_ref[V_BL_BA:V_BL_BA + 1, :] += cat(_colsum(dxl), _colsum(dga))
        vec_ref[V_BX_SP:V_BX_SP + 1, :] += cat(_colsum(dgx), dsp)
        vec_ref[V_GL_GC:V_GL_GC + 1, :] += cat(_colsum(dyl * pn), _colsum(dyc * qn))
        vec_ref[V_WL01:V_WL01 + 1, :] += cat(dwl[0], dwl[1])
        vec_ref[V_WL23:V_WL23 + 1, :] += cat(dwl[2], dwl[3])
        vec_ref[V_WS01:V_WS01 + 1, :] += cat(dws[0], dws[1])
        vec_ref[V_WS2:V_WS2 + 1, 0:W] += dws[2]

        r1 = lax.rsqrt(_rowmean(x * x) + EPS)
        xn1 = x * r1
        hb_ref[...] = ((xn1 * g1_ref[...]) * (1.0 + scale1) + shift1).astype(BF16)
        dh_in = jnp.zeros((tb, D), F32)
        for k, du in enumerate((du_lx, du_ly, du_b, du_c, du_v)):
            dprojt_ref[k * W:(k + 1) * W, :] = du.T.astype(BF16)
            dh_in = dh_in + _dot(du.astype(BF16), win_v[k * W:(k + 1) * W, :])
        vec_ref[V_SHIFT1:V_SHIFT1 + 1, :] += _colsum(dh_in)
        vec_ref[V_SCALE1:V_SCALE1 + 1, :] += _colsum(dh_in * xn1 * g1_ref[...])
        vec_ref[V_G1:V_G1 + 1, :] += _colsum(dh_in * (1.0 + scale1) * xn1)
        dxn1 = dh_in * g1_ref[...] * (1.0 + scale1)
        gx_ref[...] = dx1 + r1 * (dxn1 - xn1 * _rowmean(dxn1 * xn1))

    rev = lambda cols: pl.BlockSpec((tb, cols), lambda i: (nb - 1 - i, 0))
    rev_t = lambda rows: pl.BlockSpec((rows, tb), lambda i: (0, nb - 1 - i))
    halo = lambda cols: pl.BlockSpec((HALO, cols), lambda i: (jnp.maximum((nb - 1 - i) * hb - 1, 0), 0))
    full = lambda a: pl.BlockSpec(a.shape, lambda i: (0,) * a.ndim)
    small = (modraw, adab, g1, g2, wl, bl, bda, bdx, ba, bxb, ap, ws, gl, gc, avg)
    ext = pltpu.VMEM((tb + HALO, W), F32)
    return pl.pallas_call(
        body,
        name="mixer_bwd",
        grid=(nb,),
        in_specs=[rev(D), rev(D), rev(D), rev(D), rev(D_IN), halo(D_IN), rev(W), halo(W), rev(N_KEPT * W)]
        + [full(a) for a in small] + [ANY],
        out_specs=[rev(D), pl.BlockSpec((V_ROWS, D), lambda i: (0, 0)), rev(D), rev_t(D_IN), rev(D), rev_t(D),
                   rev_t(W), rev(2 * W)],
        out_shape=[jax.ShapeDtypeStruct((t_len, D), F32), jax.ShapeDtypeStruct((V_ROWS, D), F32),
                   jax.ShapeDtypeStruct((t_len, D), BF16), jax.ShapeDtypeStruct((D_IN, t_len), BF16),
                   jax.ShapeDtypeStruct((t_len, D), BF16), jax.ShapeDtypeStruct((D, t_len), BF16),
                   jax.ShapeDtypeStruct((W, t_len), BF16), jax.ShapeDtypeStruct((t_len, 2 * W), BF16)],
        scratch_shapes=[pltpu.VMEM((D_IN, D), BF16), pltpu.VMEM((D, D), BF16), pltpu.SemaphoreType.DMA((2,)),
                        ext, ext, ext, ext, ext, ext, pltpu.VMEM((HALO, W), F32),
                        pltpu.VMEM((tb, W), F32), pltpu.VMEM((tb, W), F32), pltpu.VMEM((tb, W), F32)],
        compiler_params=pltpu.CompilerParams(dimension_semantics=("arbitrary",), vmem_limit_bytes=VMEM_LIMIT),
    )(x, mixed, dh2, dx2, proj, proj, hl, hl, kept, *small, wpack)


def _matmul(name, a, b, tm=512):
    m, k = a.shape
    n = b.shape[1]

    def body(a_ref, b_ref, o_ref):
        o_ref[...] = _dot(a_ref[...], b_ref[...])

    return pl.pallas_call(
        body,
        name=name,
        grid=(m // tm,),
        in_specs=[pl.BlockSpec((tm, k), lambda i: (i, 0)), pl.BlockSpec((k, n), lambda i: (0, 0))],
        out_specs=pl.BlockSpec((tm, n), lambda i: (i, 0)),
        out_shape=jax.ShapeDtypeStruct((m, n), F32),
        compiler_params=pltpu.CompilerParams(dimension_semantics=("arbitrary",), vmem_limit_bytes=VMEM_LIMIT),
    )(a, b)


def _gate_wgrad(xl_t, dgate, avg):
    hd = W // 8

    def body(a_ref, b_ref, avg_ref, o_ref):
        full = _dot(a_ref[...], b_ref[...])
        row = lax.broadcasted_iota(jnp.int32, (W, hd), 0)
        col = lax.broadcasted_iota(jnp.int32, (W, hd), 1)
        fold = ((row & (hd - 1)) == col).astype(BF16)
        keep = avg_ref[...] != 0
        for g in range(2):
            m = jnp.where(keep, full[:, g * W:(g + 1) * W], 0.0)
            hi = m.astype(BF16)
            rest = m - hi.astype(F32)
            mid = rest.astype(BF16)
            lo = (rest - mid.astype(F32)).astype(BF16)
            o_ref[g] = _dot(hi, fold) + _dot(mid, fold) + _dot(lo, fold)

    return pl.pallas_call(
        body,
        name="wgrad_gate",
        in_specs=[WHOLE] * 3,
        out_specs=WHOLE,
        out_shape=jax.ShapeDtypeStruct((2, W, hd), F32),
        compiler_params=pltpu.CompilerParams(vmem_limit_bytes=VMEM_LIMIT),
    )(xl_t, dgate, avg)


def _block_diag(w):
    n, m, _ = w.shape
    eye = jnp.eye(n, dtype=w.dtype)
    return (w[:, :, None, :] * eye[:, None, :, None]).reshape(n * m, n * m)


def _pad_rows(a, rows):
    return jnp.pad(a, ((0, rows - a.shape[0]),) + ((0, 0),) * (a.ndim - 1))


def _position():
    return lax.axis_index("x"), lax.axis_index("y"), lax.axis_index("c")


def _linear(pos):
    return 4 * pos[0] + 2 * pos[1] + pos[2]


def _flip(pos, k):
    return tuple(1 - p if k & bit else p for p, bit in zip(pos, (4, 2, 1)))


def _exchange_all(make_copy, make_arrival):
    copies = [make_copy(k) for k in range(1, N_DEV)]
    for cp in copies:
        cp.start()
    for k in range(1, N_DEV):
        make_arrival(k).wait_recv()
    for cp in copies:
        cp.wait_send()


def _mod_exchange_steps(cols):
    def steps(msg_ref, adaw_ref, gath_ref, mod_ref, sendbuf, send_a, recv_a, send_b, recv_b):
        me = _position()
        me_lin = _linear(me)
        m = msg_ref[...]
        row = lax.broadcasted_iota(jnp.int32, m.shape, 0)
        gath_ref[me_lin] = jnp.where(row == 0, m * _sigmoid(m), m)

        def gather_copy(k, src_lin):
            return pltpu.make_async_remote_copy(
                src_ref=gath_ref.at[src_lin], dst_ref=gath_ref.at[src_lin], send_sem=send_a.at[k - 1],
                recv_sem=recv_a.at[k - 1], device_id=_flip(me, k), device_id_type=MESH)

        _exchange_all(lambda k: gather_copy(k, me_lin), lambda k: gather_copy(k, _linear(_flip(me, k))))

        sc_all = gath_ref[:, 0, :]
        scb = jnp.concatenate([sc_all, jnp.zeros_like(sc_all)], axis=0).astype(BF16)
        prod = _dot(scb, adaw_ref[...].astype(BF16))
        for b in range(N_DEV):
            sendbuf[b] = jnp.broadcast_to(prod[b:b + 1, :], (HALO, cols))
        mod_ref[me_lin] = sendbuf[me_lin]

        def row_copy(k, dst_lin):
            peer = _flip(me, k)
            return pltpu.make_async_remote_copy(
                src_ref=sendbuf.at[_linear(peer)], dst_ref=mod_ref.at[dst_lin], send_sem=send_b.at[k - 1],
                recv_sem=recv_b.at[k - 1], device_id=peer, device_id_type=MESH)

        _exchange_all(lambda k: row_copy(k, me_lin), lambda k: row_copy(k, _linear(_flip(me, k))))

    return steps


def _gather_and_mod(msg, ada_w, block):
    rows, cols = block.shape
    mod_cols = ada_w.shape[1]
    mod_steps = _mod_exchange_steps(mod_cols)

    def body(msg_ref, adaw_ref, x_ref, gath_ref, mod_ref, out_ref, sendbuf, send_a, recv_a, send_b, recv_b,
             send_sems, recv_sems, sib_send_sems, sib_recv_sems, local_sem):
        x, y, c = _position()
        me, sibling = (x, y, c), (x, y, 1 - c)
        sends, forward, arrivals = _chip_gather_copies(x_ref, out_ref, send_sems, recv_sems)

        def to_sibling(j, block_of, src=None):
            dst = out_ref.at[_linear(block_of)]
            return pltpu.make_async_remote_copy(
                src_ref=dst if src is None else src, dst_ref=dst, send_sem=sib_send_sems.at[j],
                recv_sem=sib_recv_sems.at[j], device_id=sibling, device_id_type=MESH)

        mine = pltpu.make_async_copy(x_ref, out_ref.at[_linear(me)], local_sem)
        mine.start()
        passes = [to_sibling(0, me, src=x_ref)] + [to_sibling(1 + j, p) for j, p in enumerate(_route_peers(me))]
        passes[0].start()
        for cp in sends:
            cp.start()
        mod_steps(msg_ref, adaw_ref, gath_ref, mod_ref, sendbuf, send_a, recv_a, send_b, recv_b)
        arrivals[0].wait_recv()
        forward.start()
        passes[1].start()
        arrivals[1].wait_recv()
        passes[2].start()
        arrivals[2].wait_recv()
        passes[3].start()
        for j, p in enumerate((sibling,) + _route_peers(sibling)):
            to_sibling(j, p).wait_recv()
        for cp in sends + [forward] + passes:
            cp.wait_send()
        mine.wait()

    return pl.pallas_call(
        body,
        name="gather_and_mod",
        in_specs=[WHOLE, WHOLE, ANY],
        out_specs=[WHOLE, WHOLE, ANY],
        out_shape=[jax.ShapeDtypeStruct((N_DEV, HALO, D), F32), jax.ShapeDtypeStruct((N_DEV, HALO, mod_cols), F32),
                   jax.ShapeDtypeStruct((N_DEV, rows, cols), block.dtype)],
        scratch_shapes=[pltpu.VMEM((N_DEV, HALO, mod_cols), F32)] + [pltpu.SemaphoreType.DMA((N_DEV - 1,))] * 4
        + [pltpu.SemaphoreType.DMA((3,)), pltpu.SemaphoreType.DMA((3,)), pltpu.SemaphoreType.DMA((4,)),
           pltpu.SemaphoreType.DMA((4,)), pltpu.SemaphoreType.DMA],
        compiler_params=pltpu.CompilerParams(vmem_limit_bytes=VMEM_LIMIT),
    )(msg, ada_w, block)


HBM = pl.BlockSpec(memory_space=pltpu.HBM)
SEM = pl.BlockSpec(memory_space=pltpu.SEMAPHORE)
EFFECT = pltpu.SideEffectType.DATAFLOW_SIDE_EFFECTING


def _stage_copies(stage):
    return {"chips": (_chip_scatter_copies, len(CHIP_FLIPS), len(CHIP_FLIPS)), "sibling": (_sibling_copies, 4, 4)}[stage]


def _chips_start(which, chip_sums, stage="chips"):
    n = len(chip_sums)
    make_copies, per_array, slots = _stage_copies(stage)
    n_sems = per_array * n

    def body(*refs):
        srcs, dsts = refs[:n], refs[n:2 * n]
        send_sems, recv_sems = refs[2 * n:2 * n + 2]
        token = refs[-1]
        for cp in make_copies(srcs, dsts, send_sems, recv_sems):
            cp.start()
        token[...] = jnp.zeros(token.shape, token.dtype)

    landing = [jax.ShapeDtypeStruct((slots,) + s.shape[-2:], s.dtype) for s in chip_sums]
    outs = pl.pallas_call(
        body,
        name=which + "_" + stage + "_start",
        in_specs=[HBM] * (2 * n),
        out_specs=[SEM, SEM] + [HBM] * (2 * n) + [WHOLE],
        out_shape=[pltpu.SemaphoreType.DMA((n_sems,)), pltpu.SemaphoreType.DMA((n_sems,))]
        + [pltpu.HBM(s.shape, s.dtype) for s in chip_sums] + [pltpu.HBM(s.shape, s.dtype) for s in landing]
        + [jax.ShapeDtypeStruct((HALO, 128), F32)],
        input_output_aliases={i: 2 + i for i in range(2 * n)},
        compiler_params=pltpu.CompilerParams(has_side_effects=EFFECT),
    )(*[pltpu.with_memory_space_constraint(s, pltpu.HBM) for s in chip_sums],
      *[pltpu.with_memory_space_constraint(lax.empty(s.shape, s.dtype), pltpu.HBM) for s in landing])
    return outs[0], outs[1], outs[2:2 + n], outs[2 + n:2 + 2 * n], outs[-1]


def _chips_wait(which, send_sems, recv_sems, srcs, landed, after, stage="chips"):
    n = len(srcs)
    make_copies = _stage_copies(stage)[0]

    def body(*refs):
        src_refs, dst_refs = refs[:n], refs[n:2 * n]
        sends, recvs = refs[2 * n:2 * n + 2]
        copies = make_copies(src_refs, dst_refs, sends, recvs)
        for cp in copies:
            cp.wait_send()
        for cp in copies:
            cp.wait_recv()

    outs = pl.pallas_call(
        body,
        name=which + "_" + stage + "_wait",
        in_specs=[HBM] * (2 * n) + [SEM, SEM, ANY],
        out_specs=[HBM] * (2 * n),
        out_shape=[pltpu.HBM(s.shape, s.dtype) for s in list(srcs) + list(landed)],
        input_output_aliases={i: i for i in range(2 * n)},
        compiler_params=pltpu.CompilerParams(has_side_effects=EFFECT),
    )(*srcs, *landed, send_sems, recv_sems, after)
    return list(outs[:n]), list(outs[n:])


def _sibling_copies(srcs, dsts, send_sems, recv_sems):
    x, y, c = _position()
    copies = []
    for a, (src, dst) in enumerate(zip(srcs, dsts)):
        for k in range(4):
            copies.append(pltpu.make_async_remote_copy(
                src_ref=src.at[k, 1 - c] if len(src.shape) == 4 else src.at[k], dst_ref=dst.at[k],
                send_sem=send_sems.at[4 * a + k],
                recv_sem=recv_sems.at[4 * a + k], device_id=(x, y, 1 - c), device_id_type=MESH))
    return copies


def _row_block(rows):
    return min(rows, 512)


def _pair_sum(pos, mine, recv):
    _, cores, rows, cols = mine.shape
    rb = _row_block(rows)

    def body(pos_ref, mine_ref, recv_ref, out_ref):
        out_ref[0] = (mine_ref[0, 0] + recv_ref[0]).astype(BF16)

    other = lambda k, pos: jnp.bitwise_xor(pos[1], k + 1)
    core = lambda pos: pos[0] * (cores - 1)
    return pl.pallas_call(
        body,
        name="grad_pair_sum",
        grid_spec=pltpu.PrefetchScalarGridSpec(
            num_scalar_prefetch=1, grid=(3, rows // rb),
            in_specs=[pl.BlockSpec((1, 1, rb, cols), lambda k, r, pos: (other(k, pos), core(pos), r, 0)),
                      pl.BlockSpec((1, rb, cols), lambda k, r, pos: (other(k, pos), r, 0))],
            out_specs=pl.BlockSpec((1, rb, cols), lambda k, r, pos: (other(k, pos), r, 0))),
        out_shape=jax.ShapeDtypeStruct((4, rows, cols), BF16),
        compiler_params=pltpu.CompilerParams(dimension_semantics=("arbitrary", "arbitrary")),
    )(pos, mine, recv)


def _final_sum(pos, mine, recv, chips):
    _, cores, rows, cols = mine.shape
    rb = _row_block(rows)

    def body(pos_ref, mine_ref, recv_ref, chips_ref, out_ref):
        g = mine_ref[0, 0] + recv_ref[0]
        for j in range(3):
            g = g + chips_ref[j].astype(F32)
        out_ref[...] = g

    return pl.pallas_call(
        body,
        name="grad_final_sum",
        grid_spec=pltpu.PrefetchScalarGridSpec(
            num_scalar_prefetch=1, grid=(rows // rb,),
            in_specs=[pl.BlockSpec((1, 1, rb, cols), lambda r, pos: (pos[1], pos[0] * (cores - 1), r, 0)),
                      pl.BlockSpec((1, rb, cols), lambda r, pos: (pos[1], r, 0)),
                      pl.BlockSpec((3, rb, cols), lambda r, pos: (0, r, 0))],
            out_specs=pl.BlockSpec((rb, cols), lambda r, pos: (r, 0))),
        out_shape=jax.ShapeDtypeStruct((rows, cols), F32),
        compiler_params=pltpu.CompilerParams(dimension_semantics=("arbitrary",)),
    )(pos, mine, recv, chips)


LOSS_ROW = V_ROWS + 8
GB_BASE = LOSS_ROW + 8


def _route_mod_grad_steps(cols):
    def steps(gmod_ref, sct_ref, gadaw_ref, sendbuf, grecv, send_a, recv_a):
        me = _position()
        me_lin = _linear(me)
        gm = gmod_ref[...]
        for b in range(N_DEV):
            sendbuf[b] = jnp.broadcast_to(gm[b:b + 1, :], (HALO, cols))
        grecv[me_lin] = sendbuf[me_lin]

        def row_copy(k, dst_lin):
            peer = _flip(me, k)
            return pltpu.make_async_remote_copy(
                src_ref=sendbuf.at[_linear(peer)], dst_ref=grecv.at[dst_lin], send_sem=send_a.at[k - 1],
                recv_sem=recv_a.at[k - 1], device_id=peer, device_id_type=MESH)

        _exchange_all(lambda k: row_copy(k, me_lin), lambda k: row_copy(k, _linear(_flip(me, k))))
        g_all = grecv[:, 0, :]
        g_pad = jnp.concatenate([g_all, jnp.zeros((sct_ref.shape[1] - N_DEV, cols), F32)], axis=0).astype(BF16)
        gadaw_ref[...] = _dot(sct_ref[...], g_pad)
        return _colsum(g_all)

    return steps


def _small_grad_exchange(gmod8, sc_t, msg_vec, msg_gate, after):
    cols = gmod8.shape[1]
    vec_rows = GB_BASE + N_DEV
    route_steps = _route_mod_grad_steps(cols)

    def body(gmod_ref, sct_ref, vec_ref, gate_ref, after_ref, gadaw_ref, sumv_ref, sumg_ref,
             sendbuf, grecv, send_a, recv_a, myv, myg, sibv, sibg, chipv, chipg, sib_send, sib_recv, peer_send, peer_recv):
        gb = route_steps(gmod_ref, sct_ref, gadaw_ref, sendbuf, grecv, send_a, recv_a)
        x, y, c = me = _position()
        my_chip = 2 * x + y
        myv[0:GB_BASE, :] = vec_ref[...]
        slot = lax.broadcasted_iota(jnp.int32, (N_DEV, D), 0) == _linear(me)
        gb_wide = jnp.concatenate([jnp.broadcast_to(gb, (N_DEV, cols)), jnp.zeros((N_DEV, D - cols), F32)], axis=1)
        myv[GB_BASE:vec_rows, :] = jnp.where(slot, gb_wide, 0.0)
        myg[...] = gate_ref[...]

        swaps = [pltpu.make_async_remote_copy(
            src_ref=src, dst_ref=dst, send_sem=sib_send.at[a], recv_sem=sib_recv.at[a], device_id=(x, y, 1 - c),
            device_id_type=MESH) for a, (src, dst) in enumerate(((myv, sibv), (myg, sibg)))]
        for cp in swaps:
            cp.start()
        for cp in swaps:
            cp.wait_recv()
        chipv[my_chip] = myv[...] + sibv[...]
        chipg[my_chip] = myg[...] + sibg[...]

        def chip_copy(a, buf, j, k, slot_chip):
            peer = _flip(me, k)
            return pltpu.make_async_remote_copy(
                src_ref=buf.at[slot_chip], dst_ref=buf.at[slot_chip], send_sem=peer_send.at[3 * a + j],
                recv_sem=peer_recv.at[3 * a + j], device_id=peer, device_id_type=MESH)

        sends = [chip_copy(a, buf, j, k, my_chip) for a, buf in enumerate((chipv, chipg)) for j, k in enumerate(CHIP_FLIPS)]
        for cp in sends:
            cp.start()
        for a, buf in enumerate((chipv, chipg)):
            for j, k in enumerate(CHIP_FLIPS):
                peer = _flip(me, k)
                chip_copy(a, buf, j, k, 2 * peer[0] + peer[1]).wait_recv()
        sumv_ref[...] = ((chipv[0] + chipv[1]) + chipv[2]) + chipv[3]
        sumg_ref[...] = ((chipg[0] + chipg[1]) + chipg[2]) + chipg[3]
        for cp in swaps + sends:
            cp.wait_send()

    vshape, gshape = (vec_rows, D), msg_gate.shape
    return pl.pallas_call(
        body,
        name="small_grad_exchange",
        in_specs=[WHOLE] * 5,
        out_specs=[WHOLE] * 3,
        out_shape=[jax.ShapeDtypeStruct((D, cols), F32), jax.ShapeDtypeStruct(vshape, F32),
                   jax.ShapeDtypeStruct(gshape, F32)],
        scratch_shapes=[pltpu.VMEM((N_DEV, HALO, cols), F32), pltpu.VMEM((N_DEV, HALO, cols), F32),
                        pltpu.SemaphoreType.DMA((N_DEV - 1,)), pltpu.SemaphoreType.DMA((N_DEV - 1,)),
                        pltpu.VMEM(vshape, F32), pltpu.VMEM(gshape, F32), pltpu.VMEM(vshape, F32),
                        pltpu.VMEM(gshape, F32), pltpu.VMEM((4,) + vshape, F32), pltpu.VMEM((4,) + gshape, F32),
                        pltpu.SemaphoreType.DMA((2,)), pltpu.SemaphoreType.DMA((2,)),
                        pltpu.SemaphoreType.DMA((2 * len(CHIP_FLIPS),)), pltpu.SemaphoreType.DMA((2 * len(CHIP_FLIPS),))],
        compiler_params=pltpu.CompilerParams(vmem_limit_bytes=VMEM_LIMIT),
    )(gmod8, sc_t, msg_vec, msg_gate, after)


def _adamw_math(w, g, m, v):
    m = ADAM_B1 * m + (1.0 - ADAM_B1) * g
    v = ADAM_B2 * v + (1.0 - ADAM_B2) * (g * g)
    m_hat = m / (1.0 - ADAM_B1 ** ADAM_STEP)
    v_hat = v / (1.0 - ADAM_B2 ** ADAM_STEP)
    delta = -ADAM_LR * (m_hat / (jnp.sqrt(v_hat) + ADAM_EPS) + ADAM_WD * w)
    return delta, m, v


def _adamw(name, w, g, m, v):
    rows, cols = w.shape
    rb = 256 if rows % 256 == 0 else rows

    def body(w_ref, g_ref, m_ref, v_ref, d_ref, mo_ref, vo_ref):
        d_ref[...], mo_ref[...], vo_ref[...] = _adamw_math(w_ref[...], g_ref[...], m_ref[...], v_ref[...])

    spec = pl.BlockSpec((rb, cols), lambda r: (r, 0))
    return pl.pallas_call(
        body,
        name="adamw_" + name,
        grid=(rows // rb,),
        in_specs=[spec] * 4,
        out_specs=[spec] * 3,
        out_shape=[jax.ShapeDtypeStruct((rows, cols), F32)] * 3,
        compiler_params=pltpu.CompilerParams(dimension_semantics=("arbitrary",)),
    )(w, g, m, v)


def _update(pos, sum_jobs, plain_jobs, after):
    rb = 256
    jobs = [("sum", j) for j in sum_jobs] + [("plain", j) for j in plain_jobs]
    offs, total = [], 0
    for _, j in jobs:
        offs.append(total)
        total += j[-1].shape[0] // rb
    n_in = sum(len(j) for _, j in jobs)

    def body(pos_ref, *refs):
        ins, outs = refs[:n_in], refs[n_in + 1:]
        s = pl.program_id(0)
        i_in = i_out = 0
        for (kind, j), off in zip(jobs, offs):
            steps = j[-1].shape[0] // rb
            j_in = ins[i_in:i_in + len(j)]
            i_in += len(j)
            j_out = outs[i_out:i_out + (4 if kind == "sum" else 3)]
            i_out += len(j_out)

            @pl.when((s >= off) & (s < off + steps))
            def _(kind=kind, j_in=j_in, j_out=j_out):
                if kind == "sum":
                    mine_ref, recv_ref, chips_ref, w_ref, m_ref, v_ref = j_in
                    g = mine_ref[0, 0] + recv_ref[0]
                    for q in range(len(CHIP_FLIPS)):
                        g = g + chips_ref[q].astype(F32)
                    j_out[0][...] = g
                    rest = j_out[1:]
                else:
                    g_ref, w_ref, m_ref, v_ref = j_in
                    g = g_ref[...]
                    rest = j_out
                rest[0][...], rest[1][...], rest[2][...] = _adamw_math(w_ref[...], g, m_ref[...], v_ref[...])

    in_specs, out_specs, out_shape, args = [], [], [], []
    for (kind, j), off in zip(jobs, offs):
        rows, cols = j[-1].shape
        steps = rows // rb
        blk = lambda s, off=off, steps=steps: jnp.clip(s - off, 0, steps - 1)
        flat = pl.BlockSpec((rb, cols), lambda s, pos, blk=blk: (blk(s), 0))
        if kind == "sum":
            in_specs += [pl.BlockSpec((1, 1, rb, cols), lambda s, pos, blk=blk: (pos[1], 0, blk(s), 0)),
                         pl.BlockSpec((1, rb, cols), lambda s, pos, blk=blk: (pos[1], blk(s), 0)),
                         pl.BlockSpec((len(CHIP_FLIPS), rb, cols), lambda s, pos, blk=blk: (0, blk(s), 0))]
            in_specs += [flat] * 3
        else:
            in_specs += [flat] * 4
        n_res = 4 if kind == "sum" else 3
        out_specs += [flat] * n_res
        out_shape += [jax.ShapeDtypeStruct((rows, cols), F32)] * n_res
        args += list(j)
    in_specs += [pl.BlockSpec(after.shape, lambda s, pos: (0,) * after.ndim)]
    outs = pl.pallas_call(
        body,
        name="update",
        grid_spec=pltpu.PrefetchScalarGridSpec(
            num_scalar_prefetch=1, grid=(total,), in_specs=in_specs, out_specs=out_specs),
        out_shape=out_shape,
        compiler_params=pltpu.CompilerParams(dimension_semantics=("arbitrary",), vmem_limit_bytes=VMEM_LIMIT),
    )(pos, *args, after)
    sums = [tuple(outs[4 * i:4 * i + 4]) for i in range(len(sum_jobs))]
    base = 4 * len(sum_jobs)
    plains = [tuple(outs[base + 3 * i:base + 3 * i + 3]) for i in range(len(plain_jobs))]
    return sums, plains


def _adamw_small(ws, gs, ms, vs, sigmoid_scaled):
    n = len(ws)

    def body(*refs):
        w_refs, g_refs, m_refs, v_refs = (refs[i * n:(i + 1) * n] for i in range(4))
        outs = refs[4 * n:]
        for i in range(n):
            w = w_refs[i][...]
            g = g_refs[i][...]
            if sigmoid_scaled[i]:
                g = g * _sigmoid(w)
            delta, m, v = _adamw_math(w, g, m_refs[i][...], v_refs[i][...])
            outs[4 * i][...] = g
            outs[4 * i + 1][...] = delta
            outs[4 * i + 2][...] = m
            outs[4 * i + 3][...] = v

    shapes = [jax.ShapeDtypeStruct(w.shape, F32) for w in ws for _ in range(4)]
    outs = pl.pallas_call(
        body,
        name="adamw_small",
        in_specs=[WHOLE] * (4 * n),
        out_specs=[WHOLE] * (4 * n),
        out_shape=shapes,
    )(*ws, *gs, *ms, *vs)
    return [outs[4 * i:4 * i + 4] for i in range(n)]


_WEIGHT_NAMES = ("ada_w", "ada_b", "norm1_g", "w_in", "lru_conv_w", "lru_conv_b", "gate_a_w", "gate_a_b", "gate_x_w",
                 "gate_x_b", "a_param", "short_conv_w", "lru_out_g", "conv_out_g", "w_out", "norm2_g", "w_mlp1",
                 "w_mlp2", "final_g")


def kernel(x, c, ada_w, ada_b, norm1_g, w_in, lru_conv_w, lru_conv_b, gate_a_w, gate_a_b, gate_x_w, gate_x_b, a_param, short_conv_w, lru_out_g, conv_out_g, w_out, norm2_g, w_mlp1, w_mlp2, final_g, loss_target, m_ada_w, m_ada_b, m_norm1_g, m_w_in, m_lru_conv_w, m_lru_conv_b, m_gate_a_w, m_gate_a_b, m_gate_x_w, m_gate_x_b, m_a_param, m_short_conv_w, m_lru_out_g, m_conv_out_g, m_w_out, m_norm2_g, m_w_mlp1, m_w_mlp2, m_final_g, v_ada_w, v_ada_b, v_norm1_g, v_w_in, v_lru_conv_w, v_lru_conv_b, v_gate_a_w, v_gate_a_b, v_gate_x_w, v_gate_x_b, v_a_param, v_short_conv_w, v_lru_out_g, v_conv_out_g, v_w_out, v_norm2_g, v_w_mlp1, v_w_mlp2, v_final_g):
    given = dict(locals())
    weights = {n: given[n] for n in _WEIGHT_NAMES}
    xi, yi, ci = _position()
    me_lin = _linear((xi, yi, ci))
    hd = W // N_DEV

    mixer_block = jnp.concatenate([w_out[0], w_in[0].T], axis=0).astype(BF16)
    mlp_block = jnp.concatenate([w_mlp1[0].T, w_mlp2[0]], axis=0).astype(BF16)

    msg = (jnp.pad(c, ((0, HALO - 1), (0, 0)))
           + jnp.pad(lru_conv_w[0], ((1, HALO - 1 - CONV_L), (0, D - hd)))
           + jnp.pad(short_conv_w[0], ((1 + CONV_L, 0), (0, D - hd))))
    gath, mod_all, wmix = _gather_and_mod(msg, ada_w[0], mixer_block)
    sc_all = gath[:, 0, :]
    wl = jnp.transpose(gath[:, 1:1 + CONV_L, :hd], (1, 0, 2)).reshape(CONV_L, W)
    ws = jnp.transpose(gath[:, 1 + CONV_L:HALO, :hd], (1, 0, 2)).reshape(CONV_S, W)
    modraw = _pad_rows(mod_all[:, 0, :].reshape(6, D), HALO)
    adab = _pad_rows(ada_b.reshape(6, D), HALO)

    x2d, tgt = x[0], loss_target[0]
    gf = final_g.reshape(1, D)
    bda = _block_diag(gate_a_w[0]).astype(BF16)
    bdx = _block_diag(gate_x_w[0]).astype(BF16)
    avg = _block_diag(jnp.full((8, W // 8, W // 8), 8.0 / W, F32)).astype(BF16)
    wl8 = _pad_rows(wl, HALO)
    ws8 = _pad_rows(ws, HALO)
    mixer_small = (wl8, lru_conv_b, bda, bdx, gate_a_b, gate_x_b, a_param, ws8, lru_out_g, conv_out_g, avg)
    proj, hl, mixed, kept, wmlp = _mixer_fwd(x2d, modraw, adab, norm1_g, *mixer_small, wmix, mlp_block)
    h2t, f, dx2, dz, vec2, loss8 = _mlp_fwd(x2d, mixed, tgt, modraw, adab, norm2_g, gf, wmlp)
    pos = jnp.stack([ci, 2 * xi + yi]).astype(jnp.int32)
    by_dest = lambda g: g.reshape((4, 2, -1) + g.shape[-1:])
    dh2_first, *for_sibling = _mlp_bwd_half(pos, h2t, f, dz, wmlp)
    sib_send, sib_recv, sib_thru, sib_land, token = _chips_start("mlp", for_sibling, stage="sibling")
    dh2, dw1, dw2 = _mlp_bwd_half(pos, h2t, f, dz, wmlp, prior=(dh2_first, token))
    done = dh2[0:HALO, 0:128] + dw1[0, 0:HALO, 0:128] + dw2[0, 0:HALO, 0:128]
    _, mlp_sib = _chips_wait("mlp", sib_send, sib_recv, sib_thru, sib_land, done, stage="sibling")
    mlp_parts = [dw1[:, None], dw2[:, None]]
    mlp_sums = [_pair_sum(pos, p, r) for p, r in zip(mlp_parts, mlp_sib)]
    mlp_send, mlp_recv, mlp_thru, mlp_land, token = _chips_start("mlp", mlp_sums)
    modraw_after = modraw + jnp.tile(token, (1, D // token.shape[1]))
    gx, vec, hb, dproj_t, dmixed, ycat_t, xl_t, dgate = _mixer_bwd(
        x2d, mixed, dh2, dx2, proj, hl, kept, modraw_after, adab, norm1_g, norm2_g, *mixer_small, wmix)
    dwint = _matmul("wgrad_in", dproj_t, hb)
    dwout = _matmul("wgrad_out", ycat_t, dmixed)
    gate_blocks = _gate_wgrad(xl_t, dgate, avg)
    msg_gate = gate_blocks.reshape(W, 128)
    done = dwint[0:HALO, 0:128] + dwout[0:HALO, 0:128] + gate_blocks[0, 0:HALO, :].sum() + gx[0:HALO, 0:128]
    _, mlp_chips = _chips_wait("mlp", mlp_send, mlp_recv, mlp_thru, mlp_land, done)
    mix_parts = [by_dest(dwout), by_dest(dwint)]
    gmod8 = (jnp.pad(vec[0:5], ((0, 1), (0, 0))) + jnp.pad(vec2[0:1], ((5, 0), (0, 0)))).reshape(N_DEV, 6 * D // N_DEV)
    sc_t = jnp.pad(sc_all.T, ((0, 0), (0, 128 - N_DEV))).astype(BF16)
    loss_rows = jnp.pad(loss8[0:1], ((0, HALO - 1), (0, D - loss8.shape[1])))
    msg_vec = jnp.concatenate([vec, vec2, loss_rows], axis=0)
    sib_send, sib_recv, sib_thru, sib_land, token = _chips_start("mixer", mix_parts, stage="sibling")
    g_adaw, sum_vec, sum_gate = _small_grad_exchange(gmod8, sc_t, msg_vec, msg_gate, token)
    mix_parts, mix_sib = _chips_wait("mixer", sib_send, sib_recv, sib_thru, sib_land, sum_vec[0:HALO, 0:128],
                                     stage="sibling")
    mix_sums = [_pair_sum(pos, p, r) for p, r in zip(mix_parts, mix_sib)]
    state = lambda n: (weights[n][0], given["m_" + n][0], given["v_" + n][0])
    mlp_jobs = [(p, r, q, *state(n)) for p, r, q, n in zip(mlp_parts, mlp_sib, mlp_chips, ("w_mlp1", "w_mlp2"))]
    mix_send, mix_recv, mix_thru, mix_land, token = _chips_start("mixer", mix_sums)
    mlp_done, (adaw_done,) = _update(pos, mlp_jobs, [(g_adaw, *state("ada_w"))], token)
    loss = sum_vec[LOSS_ROW, 0]
    sum_gate = sum_gate.reshape(2, W, W // 8)
    lo, hi = slice(0, W), slice(W, 2 * W)
    wl_full = sum_vec[V_WL01:V_WL23 + 1].reshape(CONV_L, W)
    ws_full = sum_vec[V_WS01:V_WS2 + 1].reshape(CONV_S + 1, W)[:CONV_S]
    row = lambda r, cols: sum_vec[r:r + 1, cols]
    small_grads = {
        "ada_b": sum_vec[GB_BASE:GB_BASE + N_DEV, :6 * D // N_DEV].reshape(1, 6 * D),
        "norm1_g": row(V_G1, slice(0, D)),
        "lru_conv_w": lax.dynamic_slice(wl_full, (0, me_lin * hd), (CONV_L, hd)),
        "lru_conv_b": row(V_BL_BA, lo),
        "gate_a_w": sum_gate[0],
        "gate_a_b": row(V_BL_BA, hi),
        "gate_x_w": sum_gate[1],
        "gate_x_b": row(V_BX_SP, lo),
        "a_param": row(V_BX_SP, hi),
        "short_conv_w": lax.dynamic_slice(ws_full, (0, me_lin * hd), (CONV_S, hd)),
        "lru_out_g": row(V_GL_GC, lo),
        "conv_out_g": row(V_GL_GC, hi),
        "norm2_g": row(V_G2, slice(0, D)),
        "final_g": sum_vec[V_ROWS + 1:V_ROWS + 2, :],
    }
    names = list(small_grads)
    as2d = lambda a, n: a.reshape(small_grads[n].shape)
    small = _adamw_small([as2d(weights[n], n) for n in names], [small_grads[n] for n in names],
                         [as2d(given["m_" + n], n) for n in names], [as2d(given["v_" + n], n) for n in names],
                         [n == "a_param" for n in names])
    result = {n: tuple(o.reshape(weights[n].shape) for o in outs) for n, outs in zip(names, small)}

    done = (small[0][1][:, 0:128] + mlp_done[0][2][0:HALO, 0:128] + mlp_done[1][2][0:HALO, 0:128]
            + adaw_done[1][0:HALO, 0:128])
    _, mix_chips = _chips_wait("mixer", mix_send, mix_recv, mix_thru, mix_land, done)
    g_wout, g_wint = (_final_sum(pos, p, r, q) for p, r, q in zip(mix_parts, mix_sib, mix_chips))
    for n, g in (("w_in", g_wint.T), ("w_out", g_wout)):
        w, m, v = state(n)
        result[n] = (g[None],) + tuple(o[None] for o in _adamw(n, w, g, m, v))
    result["w_mlp1"], result["w_mlp2"] = (tuple(o[None] for o in done) for done in mlp_done)
    result["ada_w"] = (g_adaw[None],) + tuple(o[None] for o in adaw_done)

    return (loss, gx[None], *[result[n][0] for n in _WEIGHT_NAMES], *[result[n][1] for n in _WEIGHT_NAMES],
            *[result[n][2] for n in _WEIGHT_NAMES], *[result[n][3] for n in _WEIGHT_NAMES])
```

```python
import jax
import jax.numpy as jnp
from jax import lax
from jax.experimental import pallas as pl
from jax.experimental.pallas import tpu as pltpu

F32 = jnp.float32
BF16 = jnp.bfloat16
MESH = pl.DeviceIdType.MESH

N_DEV = 8
D = 1024
W = 512
D_IN = 5 * W
D_FF = 4096
FF_BLK = D_FF // N_DEV
EPS = 1e-6
C_GATE = 8.0
CONV_L = 4
CONV_S = 3
HALO = 8

ROWS_W1T, ROWS_W2, ROWS_WOUT, ROWS_WIN = FF_BLK, FF_BLK, D // N_DEV, D_IN // N_DEV
OFF_WOUT = 0
OFF_WIN = OFF_WOUT + ROWS_WOUT
MIX_ROWS = OFF_WIN + ROWS_WIN
OFF_W1T = 0
OFF_W2 = OFF_W1T + ROWS_W1T
MLP_ROWS = OFF_W2 + ROWS_W2
CHIP_FLIPS = (4, 2, 6)
N_KEPT = 6

ADAM_LR = 0.001
ADAM_B1 = 0.9
ADAM_B2 = 0.999
ADAM_EPS = 1e-08
ADAM_WD = 0.01
ADAM_STEP = 10

VMEM_LIMIT = 56 * 1024 * 1024

TB_MIX = 256
TB_MIXB = 256
TB_MLP = 256
TB_MLPB = 512

ANY = pl.BlockSpec(memory_space=pl.ANY)
WHOLE = pl.BlockSpec(memory_space=pltpu.VMEM)


def _dot(a, b):
    return jnp.dot(a, b, preferred_element_type=F32)


def _dot_nt(a, b):
    return lax.dot_general(a, b, (((1,), (1,)), ((), ())), preferred_element_type=F32)


def _dot_tn(a, b):
    return lax.dot_general(a, b, (((0,), (0,)), ((), ())), preferred_element_type=F32)


def _sigmoid(v):
    return 1.0 / (1.0 + jnp.exp(-v))


def _softplus(v):
    t = jnp.exp(-jnp.abs(v))
    small = t * (1.0 - t * (0.5 - t * (1.0 / 3.0)))
    return jnp.maximum(v, 0.0) + jnp.where(t < 1e-2, small, jnp.log(1.0 + t))


def _one_minus_sq(a, log_a):
    return -jnp.tanh(log_a) * (a * a + 1.0)


_GELU_K = 0.7978845608028654
_GELU_C = 0.044715


def _gelu(u):
    th = jnp.tanh(_GELU_K * (u + _GELU_C * u * u * u))
    return 0.5 * u * (1.0 + th), th


def _gelu_grad(u, th):
    return 0.5 * (1.0 + th) + 0.5 * u * (1.0 - th * th) * _GELU_K * (1.0 + 3.0 * _GELU_C * u * u)


def _group_mean(v, avg):
    hi = v.astype(BF16)
    lo = (v - hi.astype(F32)).astype(BF16)
    return _dot(hi, avg) + _dot(lo, avg)


def _colsum(v):
    return jnp.sum(v, axis=0, keepdims=True)


def _rowmean(v):
    return jnp.mean(v, axis=-1, keepdims=True)


def _load_packed(wpack_hbm, off, rows, dst, sem):
    copies = [
        pltpu.make_async_copy(wpack_hbm.at[d, pl.ds(off, rows), :], dst.at[pl.ds(d * rows, rows), :], sem)
        for d in range(N_DEV)
    ]
    for cp in copies:
        cp.start()
    return copies


def _scan_groups(n_groups, a_ref, b_ref, out_ref, carry_ref, reverse):
    row = lax.broadcasted_iota(jnp.int32, (HALO, W), 0)

    def step(k, carry):
        g = (n_groups - 1 - k) if reverse else k
        rows = pl.ds(pl.multiple_of(g * HALO, HALO), HALO)
        a = a_ref[rows, :]
        b = b_ref[rows, :]
        for s in (1, 2, 4):
            if reverse:
                keep = row < HALO - s
                sh = HALO - s
            else:
                keep = row >= s
                sh = s
            a_sh = pltpu.roll(a, sh, axis=0)
            b_sh = pltpu.roll(b, sh, axis=0)
            b = jnp.where(keep, a * b_sh + b, b)
            a = jnp.where(keep, a * a_sh, a)
        h = b + a * carry
        out_ref[rows, :] = h
        edge = h[0:1, :] if reverse else h[HALO - 1:HALO, :]
        return jnp.broadcast_to(edge, (HALO, W))

    carry_ref[...] = lax.fori_loop(0, n_groups, step, carry_ref[...])


def _route_peers(me):
    x, y, c = me
    first = ((x + 1 - c) % 2, (y + c) % 2, c)
    second = ((x + c) % 2, (y + 1 - c) % 2, c)
    return first, second, (1 - x, 1 - y, c)


def _chip_gather_copies(block_hbm, out_hbm, send_sems, recv_sems):
    me = _position()
    first, second, diag = _route_peers(me)

    def copy(j, src, slot_of, to):
        return pltpu.make_async_remote_copy(
            src_ref=src, dst_ref=out_hbm.at[_linear(slot_of)], send_sem=send_sems.at[j], recv_sem=recv_sems.at[j],
            device_id=to, device_id_type=MESH)

    own_sends = [copy(0, block_hbm, me, first), copy(1, block_hbm, me, second)]
    forward = copy(2, out_hbm.at[_linear(first)], first, second)
    arrivals = [copy(0, block_hbm, first, first), copy(1, block_hbm, second, second), copy(2, block_hbm, diag, second)]
    return own_sends, forward, arrivals


def _mixer_fwd(x, modraw, adab, g1, wl, bl, bda, bdx, ba, bxb, ap, ws, gl, gc, avg, wpack, mlp_block):
    t_len = x.shape[0]
    tb = TB_MIX
    nb = t_len // tb

    def body(x_ref, modraw_ref, adab_ref, g1_ref, wl_ref, bl_ref, bda_ref, bdx_ref, ba_ref, bxb_ref, ap_ref,
             ws_ref, gl_ref, gc_ref, avg_ref, wpack_hbm, block_hbm, proj_ref, hl_ref, mixed_ref, kept_ref, wmlp_hbm,
             win_v, wout_v, sem, ulx_ext, cv_ext, hcar, a_s, b_s, send_sems, recv_sems, sib_send_sems, sib_recv_sems,
             local_sem):
        i = pl.program_id(0)
        x_pos, y_pos, c_pos = me = _position()
        sibling = (x_pos, y_pos, 1 - c_pos)
        own = pltpu.make_async_copy(block_hbm, wmlp_hbm.at[_linear(me)], local_sem)
        sends, forward, arrivals = _chip_gather_copies(block_hbm, wmlp_hbm, send_sems, recv_sems)

        def to_sibling(j, block_of, src=None):
            dst = wmlp_hbm.at[_linear(block_of)]
            return pltpu.make_async_remote_copy(
                src_ref=dst if src is None else src, dst_ref=dst, send_sem=sib_send_sems.at[j],
                recv_sem=sib_recv_sems.at[j], device_id=sibling, device_id_type=MESH)

        passes = [to_sibling(0, me, src=block_hbm)] + [to_sibling(1 + j, p) for j, p in enumerate(_route_peers(me))]

        @pl.when(i == 0)
        def _():
            own.start()
            for cp in sends:
                cp.start()
            passes[0].start()

        @pl.when(i == nb - 1)
        def _():
            arrivals[0].wait_recv()
            forward.start()
            passes[1].start()

        @pl.when(i == 0)
        def _():
            cps = _load_packed(wpack_hbm, OFF_WIN, ROWS_WIN, win_v, sem.at[0])
            cps += _load_packed(wpack_hbm, OFF_WOUT, ROWS_WOUT, wout_v, sem.at[1])
            ulx_ext[0:HALO, :] = jnp.zeros((HALO, W), F32)
            cv_ext[0:HALO, :] = jnp.zeros((HALO, W), F32)
            hcar[...] = jnp.zeros((HALO, W), F32)
            for cp in cps:
                cp.wait()

        mod = modraw_ref[...] + adab_ref[...]
        shift1, scale1, gate1 = mod[0:1], mod[1:2], mod[2:3]
        x = x_ref[...]
        r1 = lax.rsqrt(_rowmean(x * x) + EPS)
        h = (x * r1 * g1_ref[...]) * (1.0 + scale1) + shift1
        proj = _dot_nt(h.astype(BF16), win_v[...])
        proj_ref[...] = proj
        u_lx, u_ly, u_b, u_c, u_v = (proj[:, k * W:(k + 1) * W] for k in range(5))

        ulx_ext[HALO:HALO + tb, :] = u_lx
        xl = bl_ref[...] + wl_ref[CONV_L - 1:CONV_L, :] * u_lx
        for k in range(CONV_L - 1):
            xl = xl + wl_ref[k:k + 1, :] * ulx_ext[pl.ds(HALO - (CONV_L - 1) + k, tb), :]
        ulx_ext[0:HALO, :] = ulx_ext[tb:tb + HALO, :]
        xlb = xl.astype(BF16)
        r = _sigmoid(_dot(xlb, bda_ref[...]) + ba_ref[...])
        ig = _sigmoid(_dot(xlb, bdx_ref[...]) + bxb_ref[...])
        log_a = (-C_GATE) * r * _softplus(ap_ref[...])
        a = jnp.exp(log_a)
        mult = jnp.sqrt(_one_minus_sq(a, log_a))
        grow = i * tb + lax.broadcasted_iota(jnp.int32, (tb, W), 0)
        mult = jnp.where(grow == 0, 1.0, mult)
        a_s[...] = a
        b_s[...] = mult * (ig * xl)
        _scan_groups(tb // HALO, a_s, b_s, hl_ref, hcar, reverse=False)
        hl = hl_ref[...]
        ge, _ = _gelu(u_ly)
        p = ge * hl
        rp = lax.rsqrt(_group_mean(p * p, avg_ref[...]) + EPS)
        y_lru = p * rp * gl_ref[...]

        cv = u_c * u_v
        cv_ext[HALO:HALO + tb, :] = cv
        cc = ws_ref[CONV_S - 1:CONV_S, :] * cv
        for k in range(CONV_S - 1):
            cc = cc + ws_ref[k:k + 1, :] * cv_ext[pl.ds(HALO - (CONV_S - 1) + k, tb), :]
        cv_ext[0:HALO, :] = cv_ext[tb:tb + HALO, :]
        q = u_b * cc
        rq = lax.rsqrt(_group_mean(q * q, avg_ref[...]) + EPS)
        y_conv = q * rq * gc_ref[...]
        for k, kept in enumerate((xl, r, ig, rp, rq, cc)):
            kept_ref[:, k * W:(k + 1) * W] = kept

        mixed_ref[...] = (_dot(y_lru.astype(BF16), wout_v[0:W, :]) + _dot(y_conv.astype(BF16), wout_v[W:2 * W, :]))

        @pl.when(i == nb - 1)
        def _():
            arrivals[1].wait_recv()
            passes[2].start()
            arrivals[2].wait_recv()
            passes[3].start()
            for j, p in enumerate((sibling,) + _route_peers(sibling)):
                to_sibling(j, p).wait_recv()
            for cp in sends + [forward] + passes:
                cp.wait_send()
            own.wait()

    tok = lambda cols: pl.BlockSpec((tb, cols), lambda i: (i, 0))
    full = lambda a: pl.BlockSpec(a.shape, lambda i: (0,) * a.ndim)
    small = (modraw, adab, g1, wl, bl, bda, bdx, ba, bxb, ap, ws, gl, gc, avg)
    n_chips = len(CHIP_FLIPS)
    return pl.pallas_call(
        body,
        name="mixer_fwd",
        grid=(nb,),
        in_specs=[tok(D)] + [full(a) for a in small] + [ANY, ANY],
        out_specs=[tok(D_IN), tok(W), tok(D), tok(N_KEPT * W), ANY],
        out_shape=[jax.ShapeDtypeStruct((t_len, D_IN), F32), jax.ShapeDtypeStruct((t_len, W), F32),
                   jax.ShapeDtypeStruct((t_len, D), F32), jax.ShapeDtypeStruct((t_len, N_KEPT * W), F32),
                   jax.ShapeDtypeStruct((N_DEV,) + mlp_block.shape, BF16)],
        scratch_shapes=[pltpu.VMEM((D_IN, D), BF16), pltpu.VMEM((D, D), BF16), pltpu.SemaphoreType.DMA((2,)),
                        pltpu.VMEM((tb + HALO, W), F32), pltpu.VMEM((tb + HALO, W), F32), pltpu.VMEM((HALO, W), F32),
                        pltpu.VMEM((tb, W), F32), pltpu.VMEM((tb, W), F32),
                        pltpu.SemaphoreType.DMA((n_chips,)), pltpu.SemaphoreType.DMA((n_chips,)),
                        pltpu.SemaphoreType.DMA((4,)), pltpu.SemaphoreType.DMA((4,)), pltpu.SemaphoreType.DMA],
        compiler_params=pltpu.CompilerParams(dimension_semantics=("arbitrary",), vmem_limit_bytes=VMEM_LIMIT),
    )(x, *small, wpack, mlp_block)


def _mlp_fwd(x, mixed, tgt, modraw, adab, g2, gf, wpack):
    t_len = x.shape[0]
    tb = TB_MLP
    nb = t_len // tb

    def body(x_ref, mixed_ref, tgt_ref, modraw_ref, adab_ref, g2_ref, gf_ref, wpack_hbm,
             h2t_ref, f_ref, dx2_ref, dz_ref, vec_ref, loss_ref, w1t_v, w2_v, sem):
        i = pl.program_id(0)

        @pl.when(i == 0)
        def _():
            cps = _load_packed(wpack_hbm, OFF_W1T, ROWS_W1T, w1t_v, sem.at[0])
            cps += _load_packed(wpack_hbm, OFF_W2, ROWS_W2, w2_v, sem.at[1])
            vec_ref[...] = jnp.zeros(vec_ref.shape, F32)
            loss_ref[...] = jnp.zeros(loss_ref.shape, F32)
            for cp in cps:
                cp.wait()

        mod = modraw_ref[...] + adab_ref[...]
        gate1, shift2, scale2, gate2 = mod[2:3], mod[3:4], mod[4:5], mod[5:6]
        x1 = x_ref[...] + gate1 * mixed_ref[...]
        r2 = lax.rsqrt(_rowmean(x1 * x1) + EPS)
        h2 = (x1 * r2 * g2_ref[...]) * (1.0 + scale2) + shift2
        h2b = h2.astype(BF16)
        h2t_ref[...] = h2.T.astype(BF16)
        z = jnp.zeros((tb, D), F32)
        for j in range(N_DEV):
            cols = slice(j * FF_BLK, (j + 1) * FF_BLK)
            fj = _dot_nt(h2b, w1t_v[cols, :])
            f_ref[:, cols] = fj
            rf = jnp.maximum(fj, 0.0)
            z = z + _dot((rf * rf).astype(BF16), w2_v[cols, :])
        x2 = x1 + gate2 * z
        r3 = lax.rsqrt(_rowmean(x2 * x2) + EPS)
        xn3 = x2 * r3
        diff = xn3 * gf_ref[...] - tgt_ref[...]
        sq = _colsum(diff * diff)
        loss_ref[...] += jnp.broadcast_to(jnp.sum(sq, axis=1, keepdims=True) * (0.5 / D), loss_ref.shape)
        dy = diff * (1.0 / D)
        dyn = dy * gf_ref[...]
        dx2 = r3 * (dyn - xn3 * _rowmean(dyn * xn3))
        dx2_ref[...] = dx2
        dz_ref[...] = (gate2 * dx2).astype(BF16)
        vec_ref[0:1, :] += _colsum(dx2 * z)
        vec_ref[1:2, :] += _colsum(dy * xn3)

    tok = lambda cols: pl.BlockSpec((tb, cols), lambda i: (i, 0))
    tok_t = pl.BlockSpec((D, tb), lambda i: (0, i))
    full = lambda a: pl.BlockSpec(a.shape, lambda i: (0,) * a.ndim)
    small = (modraw, adab, g2, gf)
    return pl.pallas_call(
        body,
        name="mlp_fwd",
        grid=(nb,),
        in_specs=[tok(D), tok(D), tok(D)] + [full(a) for a in small] + [ANY],
        out_specs=[tok_t, tok(D_FF), tok(D), tok(D), pl.BlockSpec((8, D), lambda i: (0, 0)),
                   pl.BlockSpec((8, 128), lambda i: (0, 0))],
        out_shape=[jax.ShapeDtypeStruct((D, t_len), BF16), jax.ShapeDtypeStruct((t_len, D_FF), F32),
                   jax.ShapeDtypeStruct((t_len, D), F32), jax.ShapeDtypeStruct((t_len, D), BF16),
                   jax.ShapeDtypeStruct((8, D), F32), jax.ShapeDtypeStruct((8, 128), F32)],
        scratch_shapes=[pltpu.VMEM((D_FF, D), BF16), pltpu.VMEM((D_FF, D), BF16), pltpu.SemaphoreType.DMA((2,))],
        compiler_params=pltpu.CompilerParams(dimension_semantics=("arbitrary",), vmem_limit_bytes=VMEM_LIMIT),
    )(x, mixed, tgt, *small, wpack)


def _mlp_bwd_half(pos, h2t, f, dz, wpack, prior=None):
    t_len = dz.shape[0]
    tb = TB_MLPB
    nb = t_len // tb
    first = prior is None
    flip = 1 if first else 0

    def body(pos_ref, h2t_ref, f_ref, dz_ref, w1t_ref, w2_ref, *rest):
        if first:
            dh2_ref, dw1_ref, dw2_ref = rest
        else:
            dh2in_ref, _, dh2_ref, dw1_ref, dw2_ref = rest
        k = pl.program_id(0)
        t = pl.program_id(1)
        rows = pl.ds(pl.multiple_of(t * tb, tb), tb)
        w1t = w1t_ref[0]
        w2 = w2_ref[0]
        dz = dz_ref[...]
        rf = jnp.maximum(f_ref[...], 0.0)
        df = (_dot_nt(dz, w2) * (2.0 * rf)).astype(BF16)
        dh = _dot(df, w1t)
        g1 = _dot(h2t_ref[...], df)
        g2 = _dot_tn((rf * rf).astype(BF16), dz)

        @pl.when(t == 0)
        def _():
            dw2_ref[0] = g2
            dw1_ref[0] = g1

        @pl.when(t != 0)
        def _():
            dw2_ref[0] += g2
            dw1_ref[0] += g1

        @pl.when(k == 0)
        def _():
            dh2_ref[rows, :] = dh if first else dh2in_ref[...] + dh

        @pl.when(k != 0)
        def _():
            dh2_ref[rows, :] += dh

    blk = lambda k, pos: 2 * k + jnp.bitwise_xor(pos[0], flip)
    in_specs = [pl.BlockSpec((D, tb), lambda k, t, pos: (0, t)),
                pl.BlockSpec((tb, FF_BLK), lambda k, t, pos: (t, blk(k, pos))),
                pl.BlockSpec((tb, D), lambda k, t, pos: (t, 0)),
                pl.BlockSpec((1, ROWS_W1T, D), lambda k, t, pos: (blk(k, pos), OFF_W1T // ROWS_W1T, 0)),
                pl.BlockSpec((1, ROWS_W2, D), lambda k, t, pos: (blk(k, pos), OFF_W2 // ROWS_W2, 0))]
    grad_specs = [pl.BlockSpec((1, D, FF_BLK), lambda k, t, pos: (k, 0, 0)),
                  pl.BlockSpec((1, FF_BLK, D), lambda k, t, pos: (k, 0, 0))]
    out_specs = [pl.BlockSpec((t_len, D), lambda k, t, pos: (0, 0))] + grad_specs
    grad_shapes = [jax.ShapeDtypeStruct((4, D, FF_BLK), F32), jax.ShapeDtypeStruct((4, FF_BLK, D), F32)]
    out_shape = [jax.ShapeDtypeStruct((t_len, D), F32)] + grad_shapes
    args = [pos, h2t, f, dz, wpack, wpack]
    if not first:
        in_specs += [pl.BlockSpec((tb, D), lambda k, t, pos: (jnp.where(k == 0, t, nb - 1), 0)),
                     pl.BlockSpec(prior[1].shape, lambda k, t, pos: (0,) * prior[1].ndim)]
        args += list(prior)
    return pl.pallas_call(
        body,
        name="mlp_bwd_first" if first else "mlp_bwd_second",
        grid_spec=pltpu.PrefetchScalarGridSpec(num_scalar_prefetch=1, grid=(4, nb), in_specs=in_specs,
                                               out_specs=out_specs),
        out_shape=out_shape,
        compiler_params=pltpu.CompilerParams(dimension_semantics=("arbitrary", "arbitrary"),
                                             vmem_limit_bytes=VMEM_LIMIT),
    )(*args)


V_SHIFT1, V_SCALE1, V_GATE1, V_SHIFT2, V_SCALE2, V_G1, V_G2 = 0, 1, 2, 3, 4, 6, 7
V_BL_BA, V_BX_SP, V_GL_GC, V_WL01, V_WL23, V_WS01, V_WS2 = 8, 9, 10, 11, 12, 13, 14
V_ROWS = 16


def _chip_scatter_copies(srcs, dsts, send_sems, recv_sems):
    me = _position()
    copies = []
    for a, (src, dst) in enumerate(zip(srcs, dsts)):
        for j, k in enumerate(CHIP_FLIPS):
            peer = _flip(me, k)
            copies.append(pltpu.make_async_remote_copy(
                src_ref=src.at[2 * peer[0] + peer[1]], dst_ref=dst.at[j], send_sem=send_sems.at[len(CHIP_FLIPS) * a + j],
                recv_sem=recv_sems.at[len(CHIP_FLIPS) * a + j], device_id=peer, device_id_type=MESH))
    return copies


def _mixer_bwd(x, mixed, dh2, dx2, proj, hl, kept, modraw, adab, g1, g2, wl, bl, bda, bdx, ba, bxb, ap, ws, gl, gc, avg, wpack):
    t_len = x.shape[0]
    tb = TB_MIXB
    nb = t_len // tb
    hb = tb // HALO

    def body(x_ref, mixed_ref, dh2_ref, dx2_ref, proj_ref, projh_ref, hl_ref, hlh_ref, kept_ref,
             modraw_ref, adab_ref, g1_ref, g2_ref, wl_ref, bl_ref, bda_ref, bdx_ref, ba_ref, bxb_ref, ap_ref,
             ws_ref, gl_ref, gc_ref, avg_ref, wpack_hbm,
             gx_ref, vec_ref, hb_ref, dprojt_ref, dmixed_ref, ycatt_ref, xlt_ref, dgate_ref,
             win_v, wout_v, sem, ulx_ext, cv_ext, hl_ext, a_ext, dxl_ext, dcc_ext, dcar, an_s, g_s, dh_s):
        i = pl.program_id(0)
        blk = nb - 1 - i

        @pl.when(i == 0)
        def _():
            cps = _load_packed(wpack_hbm, OFF_WIN, ROWS_WIN, win_v, sem.at[0])
            cps += _load_packed(wpack_hbm, OFF_WOUT, ROWS_WOUT, wout_v, sem.at[1])
            vec_ref[...] = jnp.zeros(vec_ref.shape, F32)
            zero = jnp.zeros((HALO, W), F32)
            a_ext[tb:tb + HALO, :] = zero
            dxl_ext[tb:tb + HALO, :] = zero
            dcc_ext[tb:tb + HALO, :] = zero
            dcar[...] = zero
            for cp in cps:
                cp.wait()

        mod = modraw_ref[...] + adab_ref[...]
        shift1, scale1, gate1, scale2 = mod[0:1], mod[1:2], mod[2:3], mod[4:5]
        x = x_ref[...]
        mixed = mixed_ref[...]

        x1 = x + gate1 * mixed
        r2 = lax.rsqrt(_rowmean(x1 * x1) + EPS)
        xn2 = x1 * r2
        dh2 = dh2_ref[...]
        vec_ref[V_SHIFT2:V_SHIFT2 + 1, :] += _colsum(dh2)
        vec_ref[V_SCALE2:V_SCALE2 + 1, :] += _colsum(dh2 * xn2 * g2_ref[...])
        vec_ref[V_G2:V_G2 + 1, :] += _colsum(dh2 * (1.0 + scale2) * xn2)
        dxn2 = dh2 * g2_ref[...] * (1.0 + scale2)
        dx1 = dx2_ref[...] + r2 * (dxn2 - xn2 * _rowmean(dxn2 * xn2))
        vec_ref[V_GATE1:V_GATE1 + 1, :] += _colsum(dx1 * mixed)
        dmixed = (gate1 * dx1).astype(BF16)

        proj = proj_ref[...]
        u_lx, u_ly, u_b, u_c, u_v = (proj[:, k * W:(k + 1) * W] for k in range(5))
        has_prev = (blk > 0).astype(F32)
        projh = projh_ref[...]
        ulx_ext[0:HALO, :] = projh[:, 0:W] * has_prev
        ulx_ext[HALO:HALO + tb, :] = u_lx
        xl, r, ig, rp, rq, cc = (kept_ref[:, k * W:(k + 1) * W] for k in range(N_KEPT))
        sp = _softplus(ap_ref[...])
        log_a = (-C_GATE) * r * sp
        a = jnp.exp(log_a)
        mult_raw = jnp.sqrt(_one_minus_sq(a, log_a))
        first = (blk * tb + lax.broadcasted_iota(jnp.int32, (tb, W), 0)) == 0
        mult = jnp.where(first, 1.0, mult_raw)
        hl = hl_ref[...]
        ge, th = _gelu(u_ly)
        pn = ge * hl * rp
        cv = u_c * u_v
        cv_ext[0:HALO, :] = projh[:, 3 * W:4 * W] * projh[:, 4 * W:5 * W] * has_prev
        cv_ext[HALO:HALO + tb, :] = cv
        qn = u_b * cc * rq

        dmixed_ref[...] = dmixed
        ycatt_ref[0:W, :] = (pn * gl_ref[...]).T.astype(BF16)
        ycatt_ref[W:2 * W, :] = (qn * gc_ref[...]).T.astype(BF16)
        dyl = _dot_nt(dmixed, wout_v[0:W, :])
        dyc = _dot_nt(dmixed, wout_v[W:2 * W, :])

        dqn = dyc * gc_ref[...]
        dq = rq * (dqn - qn * _group_mean(dqn * qn, avg_ref[...]))
        du_b = dq * cc
        dcc = dq * u_b
        dcc_ext[0:tb, :] = dcc
        dcv = ws_ref[CONV_S - 1:CONV_S, :] * dcc
        for k in range(CONV_S - 1):
            dcv = dcv + ws_ref[k:k + 1, :] * dcc_ext[pl.ds(CONV_S - 1 - k, tb), :]
        dcc_ext[tb:tb + HALO, :] = dcc_ext[0:HALO, :]
        du_c = dcv * u_v
        du_v = dcv * u_c
        dws = [_colsum(dcc * cv_ext[pl.ds(HALO - (CONV_S - 1) + k, tb), :]) for k in range(CONV_S)]

        dpn = dyl * gl_ref[...]
        dp = rp * (dpn - pn * _group_mean(dpn * pn, avg_ref[...]))
        du_ly = dp * hl * _gelu_grad(u_ly, th)
        g_s[...] = dp * ge
        a_ext[0:tb, :] = a
        an_s[...] = a_ext[pl.ds(1, tb), :]
        _scan_groups(hb, an_s, g_s, dh_s, dcar, reverse=True)
        a_ext[tb:tb + HALO, :] = a_ext[0:HALO, :]
        dh = dh_s[...]
        hl_ext[0:HALO, :] = hlh_ref[...] * has_prev
        hl_ext[HALO:HALO + tb, :] = hl
        da = dh * hl_ext[pl.ds(HALO - 1, tb), :]
        dmult = dh * (ig * xl)
        dig = dh * (mult * xl)
        dxl = dh * (mult * ig)
        dlog = da * a - jnp.where(first, 0.0, dmult * (a * a) / mult_raw)
        dr = dlog * ((-C_GATE) * sp)
        dsp = _colsum(dlog * ((-C_GATE) * r))
        dga = dr * r * (1.0 - r)
        dgx = dig * ig * (1.0 - ig)
        dgab = dga.astype(BF16)
        dgxb = dgx.astype(BF16)
        xlt_ref[...] = xl.T.astype(BF16)
        dgate_ref[:, 0:W] = dgab
        dgate_ref[:, W:2 * W] = dgxb
        dxl = dxl + _dot_nt(dgab, bda_ref[...]) + _dot_nt(dgxb, bdx_ref[...])
        dxl_ext[0:tb, :] = dxl
        du_lx = wl_ref[CONV_L - 1:CONV_L, :] * dxl
        for k in range(CONV_L - 1):
            du_lx = du_lx + wl_ref[k:k + 1, :] * dxl_ext[pl.ds(CONV_L - 1 - k, tb), :]
        dxl_ext[tb:tb + HALO, :] = dxl_ext[0:HALO, :]
        dwl = [_colsum(dxl * ulx_ext[pl.ds(HALO - (CONV_L - 1) + k, tb), :]) for k in range(CONV_L)]

        cat = lambda u, v: jnp.concatenate([u, v], axis=1)
        vec_ref[V_BL_BA:V_BL_BA + 1, :] += cat(_colsum(dxl), _colsum(dga))
        vec_ref[V_BX_SP:V_BX_SP + 1, :] += cat(_colsum(dgx), dsp)
        vec_ref[V_GL_GC:V_GL_GC + 1, :] += cat(_colsum(dyl * pn), _colsum(dyc * qn))
        vec_ref[V_WL01:V_WL01 + 1, :] += cat(dwl[0], dwl[1])
        vec_ref[V_WL23:V_WL23 + 1, :] += cat(dwl[2], dwl[3])
        vec_ref[V_WS01:V_WS01 + 1, :] += cat(dws[0], dws[1])
        vec_ref[V_WS2:V_WS2 + 1, 0:W] += dws[2]

        r1 = lax.rsqrt(_rowmean(x * x) + EPS)
        xn1 = x * r1
        hb_ref[...] = ((xn1 * g1_ref[...]) * (1.0 + scale1) + shift1).astype(BF16)
        dh_in = jnp.zeros((tb, D), F32)
        for k, du in enumerate((du_lx, du_ly, du_b, du_c, du_v)):
            dprojt_ref[k * W:(k + 1) * W, :] = du.T.astype(BF16)
            dh_in = dh_in + _dot(du.astype(BF16), win_v[k * W:(k + 1) * W, :])
        vec_ref[V_SHIFT1:V_SHIFT1 + 1, :] += _colsum(dh_in)
        vec_ref[V_SCALE1:V_SCALE1 + 1, :] += _colsum(dh_in * xn1 * g1_ref[...])
        vec_ref[V_G1:V_G1 + 1, :] += _colsum(dh_in * (1.0 + scale1) * xn1)
        dxn1 = dh_in * g1_ref[...] * (1.0 + scale1)
        gx_ref[...] = dx1 + r1 * (dxn1 - xn1 * _rowmean(dxn1 * xn1))

    rev = lambda cols: pl.BlockSpec((tb, cols), lambda i: (nb - 1 - i, 0))
    rev_t = lambda rows: pl.BlockSpec((rows, tb), lambda i: (0, nb - 1 - i))
    halo = lambda cols: pl.BlockSpec((HALO, cols), lambda i: (jnp.maximum((nb - 1 - i) * hb - 1, 0), 0))
    full = lambda a: pl.BlockSpec(a.shape, lambda i: (0,) * a.ndim)
    small = (modraw, adab, g1, g2, wl, bl, bda, bdx, ba, bxb, ap, ws, gl, gc, avg)
    ext = pltpu.VMEM((tb + HALO, W), F32)
    return pl.pallas_call(
        body,
        name="mixer_bwd",
        grid=(nb,),
        in_specs=[rev(D), rev(D), rev(D), rev(D), rev(D_IN), halo(D_IN), rev(W), halo(W), rev(N_KEPT * W)]
        + [full(a) for a in small] + [ANY],
        out_specs=[rev(D), pl.BlockSpec((V_ROWS, D), lambda i: (0, 0)), rev(D), rev_t(D_IN), rev(D), rev_t(D),
                   rev_t(W), rev(2 * W)],
        out_shape=[jax.ShapeDtypeStruct((t_len, D), F32), jax.ShapeDtypeStruct((V_ROWS, D), F32),
                   jax.ShapeDtypeStruct((t_len, D), BF16), jax.ShapeDtypeStruct((D_IN, t_len), BF16),
                   jax.ShapeDtypeStruct((t_len, D), BF16), jax.ShapeDtypeStruct((D, t_len), BF16),
                   jax.ShapeDtypeStruct((W, t_len), BF16), jax.ShapeDtypeStruct((t_len, 2 * W), BF16)],
        scratch_shapes=[pltpu.VMEM((D_IN, D), BF16), pltpu.VMEM((D, D), BF16), pltpu.SemaphoreType.DMA((2,)),
                        ext, ext, ext, ext, ext, ext, pltpu.VMEM((HALO, W), F32),
                        pltpu.VMEM((tb, W), F32), pltpu.VMEM((tb, W), F32), pltpu.VMEM((tb, W), F32)],
        compiler_params=pltpu.CompilerParams(dimension_semantics=("arbitrary",), vmem_limit_bytes=VMEM_LIMIT),
    )(x, mixed, dh2, dx2, proj, proj, hl, hl, kept, *small, wpack)


def _matmul(name, a, b, tm=512):
    m, k = a.shape
    n = b.shape[1]

    def body(a_ref, b_ref, o_ref):
        o_ref[...] = _dot(a_ref[...], b_ref[...])

    return pl.pallas_call(
        body,
        name=name,
        grid=(m // tm,),
        in_specs=[pl.BlockSpec((tm, k), lambda i: (i, 0)), pl.BlockSpec((k, n), lambda i: (0, 0))],
        out_specs=pl.BlockSpec((tm, n), lambda i: (i, 0)),
        out_shape=jax.ShapeDtypeStruct((m, n), F32),
        compiler_params=pltpu.CompilerParams(dimension_semantics=("arbitrary",), vmem_limit_bytes=VMEM_LIMIT),
    )(a, b)


def _gate_wgrad(xl_t, dgate, avg):
    hd = W // 8

    def body(a_ref, b_ref, avg_ref, o_ref):
        full = _dot(a_ref[...], b_ref[...])
        row = lax.broadcasted_iota(jnp.int32, (W, hd), 0)
        col = lax.broadcasted_iota(jnp.int32, (W, hd), 1)
        fold = ((row & (hd - 1)) == col).astype(BF16)
        keep = avg_ref[...] != 0
        for g in range(2):
            m = jnp.where(keep, full[:, g * W:(g + 1) * W], 0.0)
            hi = m.astype(BF16)
            rest = m - hi.astype(F32)
            mid = rest.astype(BF16)
            lo = (rest - mid.astype(F32)).astype(BF16)
            o_ref[g] = _dot(hi, fold) + _dot(mid, fold) + _dot(lo, fold)

    return pl.pallas_call(
        body,
        name="wgrad_gate",
        in_specs=[WHOLE] * 3,
        out_specs=WHOLE,
        out_shape=jax.ShapeDtypeStruct((2, W, hd), F32),
        compiler_params=pltpu.CompilerParams(vmem_limit_bytes=VMEM_LIMIT),
    )(xl_t, dgate, avg)


def _block_diag(w):
    n, m, _ = w.shape
    eye = jnp.eye(n, dtype=w.dtype)
    return (w[:, :, None, :] * eye[:, None, :, None]).reshape(n * m, n * m)


def _pad_rows(a, rows):
    return jnp.pad(a, ((0, rows - a.shape[0]),) + ((0, 0),) * (a.ndim - 1))


def _position():
    return lax.axis_index("x"), lax.axis_index("y"), lax.axis_index("c")


def _linear(pos):
    return 4 * pos[0] + 2 * pos[1] + pos[2]


def _flip(pos, k):
    return tuple(1 - p if k & bit else p for p, bit in zip(pos, (4, 2, 1)))


def _exchange_all(make_copy, make_arrival):
    copies = [make_copy(k) for k in range(1, N_DEV)]
    for cp in copies:
        cp.start()
    for k in range(1, N_DEV):
        make_arrival(k).wait_recv()
    for cp in copies:
        cp.wait_send()


def _mod_exchange_steps(cols):
    def steps(msg_ref, adaw_ref, gath_ref, mod_ref, sendbuf, send_a, recv_a, send_b, recv_b):
        me = _position()
        me_lin = _linear(me)
        m = msg_ref[...]
        row = lax.broadcasted_iota(jnp.int32, m.shape, 0)
        gath_ref[me_lin] = jnp.where(row == 0, m * _sigmoid(m), m)

        def gather_copy(k, src_lin):
            return pltpu.make_async_remote_copy(
                src_ref=gath_ref.at[src_lin], dst_ref=gath_ref.at[src_lin], send_sem=send_a.at[k - 1],
                recv_sem=recv_a.at[k - 1], device_id=_flip(me, k), device_id_type=MESH)

        _exchange_all(lambda k: gather_copy(k, me_lin), lambda k: gather_copy(k, _linear(_flip(me, k))))

        sc_all = gath_ref[:, 0, :]
        scb = jnp.concatenate([sc_all, jnp.zeros_like(sc_all)], axis=0).astype(BF16)
        prod = _dot(scb, adaw_ref[...].astype(BF16))
        for b in range(N_DEV):
            sendbuf[b] = jnp.broadcast_to(prod[b:b + 1, :], (HALO, cols))
        mod_ref[me_lin] = sendbuf[me_lin]

        def row_copy(k, dst_lin):
            peer = _flip(me, k)
            return pltpu.make_async_remote_copy(
                src_ref=sendbuf.at[_linear(peer)], dst_ref=mod_ref.at[dst_lin], send_sem=send_b.at[k - 1],
                recv_sem=recv_b.at[k - 1], device_id=peer, device_id_type=MESH)

        _exchange_all(lambda k: row_copy(k, me_lin), lambda k: row_copy(k, _linear(_flip(me, k))))

    return steps


def _gather_and_mod(msg, ada_w, block):
    rows, cols = block.shape
    mod_cols = ada_w.shape[1]
    mod_steps = _mod_exchange_steps(mod_cols)

    def body(msg_ref, adaw_ref, x_ref, gath_ref, mod_ref, out_ref, sendbuf, send_a, recv_a, send_b, recv_b,
             send_sems, recv_sems, sib_send_sems, sib_recv_sems, local_sem):
        x, y, c = _position()
        me, sibling = (x, y, c), (x, y, 1 - c)
        sends, forward, arrivals = _chip_gather_copies(x_ref, out_ref, send_sems, recv_sems)

        def to_sibling(j, block_of, src=None):
            dst = out_ref.at[_linear(block_of)]
            return pltpu.make_async_remote_copy(
                src_ref=dst if src is None else src, dst_ref=dst, send_sem=sib_send_sems.at[j],
                recv_sem=sib_recv_sems.at[j], device_id=sibling, device_id_type=MESH)

        mine = pltpu.make_async_copy(x_ref, out_ref.at[_linear(me)], local_sem)
        mine.start()
        passes = [to_sibling(0, me, src=x_ref)] + [to_sibling(1 + j, p) for j, p in enumerate(_route_peers(me))]
        passes[0].start()
        for cp in sends:
            cp.start()
        mod_steps(msg_ref, adaw_ref, gath_ref, mod_ref, sendbuf, send_a, recv_a, send_b, recv_b)
        arrivals[0].wait_recv()
        forward.start()
        passes[1].start()
        arrivals[1].wait_recv()
        passes[2].start()
        arrivals[2].wait_recv()
        passes[3].start()
        for j, p in enumerate((sibling,) + _route_peers(sibling)):
            to_sibling(j, p).wait_recv()
        for cp in sends + [forward] + passes:
            cp.wait_send()
        mine.wait()

    return pl.pallas_call(
        body,
        name="gather_and_mod",
        in_specs=[WHOLE, WHOLE, ANY],
        out_specs=[WHOLE, WHOLE, ANY],
        out_shape=[jax.ShapeDtypeStruct((N_DEV, HALO, D), F32), jax.ShapeDtypeStruct((N_DEV, HALO, mod_cols), F32),
                   jax.ShapeDtypeStruct((N_DEV, rows, cols), block.dtype)],
        scratch_shapes=[pltpu.VMEM((N_DEV, HALO, mod_cols), F32)] + [pltpu.SemaphoreType.DMA((N_DEV - 1,))] * 4
        + [pltpu.SemaphoreType.DMA((3,)), pltpu.SemaphoreType.DMA((3,)), pltpu.SemaphoreType.DMA((4,)),
           pltpu.SemaphoreType.DMA((4,)), pltpu.SemaphoreType.DMA],
        compiler_params=pltpu.CompilerParams(vmem_limit_bytes=VMEM_LIMIT),
    )(msg, ada_w, block)


HBM = pl.BlockSpec(memory_space=pltpu.HBM)
SEM = pl.BlockSpec(memory_space=pltpu.SEMAPHORE)
EFFECT = pltpu.SideEffectType.DATAFLOW_SIDE_EFFECTING


def _stage_copies(stage):
    return {"chips": (_chip_scatter_copies, len(CHIP_FLIPS), len(CHIP_FLIPS)), "sibling": (_sibling_copies, 4, 4)}[stage]


def _chips_start(which, chip_sums, stage="chips"):
    n = len(chip_sums)
    make_copies, per_array, slots = _stage_copies(stage)
    n_sems = per_array * n

    def body(*refs):
        srcs, dsts = refs[:n], refs[n:2 * n]
        send_sems, recv_sems = refs[2 * n:2 * n + 2]
        token = refs[-1]
        for cp in make_copies(srcs, dsts, send_sems, recv_sems):
            cp.start()
        token[...] = jnp.zeros(token.shape, token.dtype)

    landing = [jax.ShapeDtypeStruct((slots,) + s.shape[-2:], s.dtype) for s in chip_sums]
    outs = pl.pallas_call(
        body,
        name=which + "_" + stage + "_start",
        in_specs=[HBM] * (2 * n),
        out_specs=[SEM, SEM] + [HBM] * (2 * n) + [WHOLE],
        out_shape=[pltpu.SemaphoreType.DMA((n_sems,)), pltpu.SemaphoreType.DMA((n_sems,))]
        + [pltpu.HBM(s.shape, s.dtype) for s in chip_sums] + [pltpu.HBM(s.shape, s.dtype) for s in landing]
        + [jax.ShapeDtypeStruct((HALO, 128), F32)],
        input_output_aliases={i: 2 + i for i in range(2 * n)},
        compiler_params=pltpu.CompilerParams(has_side_effects=EFFECT),
    )(*[pltpu.with_memory_space_constraint(s, pltpu.HBM) for s in chip_sums],
      *[pltpu.with_memory_space_constraint(lax.empty(s.shape, s.dtype), pltpu.HBM) for s in landing])
    return outs[0], outs[1], outs[2:2 + n], outs[2 + n:2 + 2 * n], outs[-1]


def _chips_wait(which, send_sems, recv_sems, srcs, landed, after, stage="chips"):
    n = len(srcs)
    make_copies = _stage_copies(stage)[0]

    def body(*refs):
        src_refs, dst_refs = refs[:n], refs[n:2 * n]
        sends, recvs = refs[2 * n:2 * n + 2]
        copies = make_copies(src_refs, dst_refs, sends, recvs)
        for cp in copies:
            cp.wait_send()
        for cp in copies:
            cp.wait_recv()

    outs = pl.pallas_call(
        body,
        name=which + "_" + stage + "_wait",
        in_specs=[HBM] * (2 * n) + [SEM, SEM, ANY],
        out_specs=[HBM] * (2 * n),
        out_shape=[pltpu.HBM(s.shape, s.dtype) for s in list(srcs) + list(landed)],
        input_output_aliases={i: i for i in range(2 * n)},
        compiler_params=pltpu.CompilerParams(has_side_effects=EFFECT),
    )(*srcs, *landed, send_sems, recv_sems, after)
    return list(outs[:n]), list(outs[n:])


def _sibling_copies(srcs, dsts, send_sems, recv_sems):
    x, y, c = _position()
    copies = []
    for a, (src, dst) in enumerate(zip(srcs, dsts)):
        for k in range(4):
            copies.append(pltpu.make_async_remote_copy(
                src_ref=src.at[k, 1 - c] if len(src.shape) == 4 else src.at[k], dst_ref=dst.at[k],
                send_sem=send_sems.at[4 * a + k],
                recv_sem=recv_sems.at[4 * a + k], device_id=(x, y, 1 - c), device_id_type=MESH))
    return copies


def _row_block(rows):
    return min(rows, 512)


def _pair_sum(pos, mine, recv):
    _, cores, rows, cols = mine.shape
    rb = _row_block(rows)

    def body(pos_ref, mine_ref, recv_ref, out_ref):
        out_ref[0] = (mine_ref[0, 0] + recv_ref[0]).astype(BF16)

    other = lambda k, pos: jnp.bitwise_xor(pos[1], k + 1)
    core = lambda pos: pos[0] * (cores - 1)
    return pl.pallas_call(
        body,
        name="grad_pair_sum",
        grid_spec=pltpu.PrefetchScalarGridSpec(
            num_scalar_prefetch=1, grid=(3, rows // rb),
            in_specs=[pl.BlockSpec((1, 1, rb, cols), lambda k, r, pos: (other(k, pos), core(pos), r, 0)),
                      pl.BlockSpec((1, rb, cols), lambda k, r, pos: (other(k, pos), r, 0))],
            out_specs=pl.BlockSpec((1, rb, cols), lambda k, r, pos: (other(k, pos), r, 0))),
        out_shape=jax.ShapeDtypeStruct((4, rows, cols), BF16),
        compiler_params=pltpu.CompilerParams(dimension_semantics=("arbitrary", "arbitrary")),
    )(pos, mine, recv)


def _final_sum(pos, mine, recv, chips):
    _, cores, rows, cols = mine.shape
    rb = _row_block(rows)

    def body(pos_ref, mine_ref, recv_ref, chips_ref, out_ref):
        g = mine_ref[0, 0] + recv_ref[0]
        for j in range(3):
            g = g + chips_ref[j].astype(F32)
        out_ref[...] = g

    return pl.pallas_call(
        body,
        name="grad_final_sum",
        grid_spec=pltpu.PrefetchScalarGridSpec(
            num_scalar_prefetch=1, grid=(rows // rb,),
            in_specs=[pl.BlockSpec((1, 1, rb, cols), lambda r, pos: (pos[1], pos[0] * (cores - 1), r, 0)),
                      pl.BlockSpec((1, rb, cols), lambda r, pos: (pos[1], r, 0)),
                      pl.BlockSpec((3, rb, cols), lambda r, pos: (0, r, 0))],
            out_specs=pl.BlockSpec((rb, cols), lambda r, pos: (r, 0))),
        out_shape=jax.ShapeDtypeStruct((rows, cols), F32),
        compiler_params=pltpu.CompilerParams(dimension_semantics=("arbitrary",)),
    )(pos, mine, recv, chips)


LOSS_ROW = V_ROWS + 8
GB_BASE = LOSS_ROW + 8


def _route_mod_grad_steps(cols):
    def steps(gmod_ref, sct_ref, gadaw_ref, sendbuf, grecv, send_a, recv_a):
        me = _position()
        me_lin = _linear(me)
        gm = gmod_ref[...]
        for b in range(N_DEV):
            sendbuf[b] = jnp.broadcast_to(gm[b:b + 1, :], (HALO, cols))
        grecv[me_lin] = sendbuf[me_lin]

        def row_copy(k, dst_lin):
            peer = _flip(me, k)
            return pltpu.make_async_remote_copy(
                src_ref=sendbuf.at[_linear(peer)], dst_ref=grecv.at[dst_lin], send_sem=send_a.at[k - 1],
                recv_sem=recv_a.at[k - 1], device_id=peer, device_id_type=MESH)

        _exchange_all(lambda k: row_copy(k, me_lin), lambda k: row_copy(k, _linear(_flip(me, k))))
        g_all = grecv[:, 0, :]
        g_pad = jnp.concatenate([g_all, jnp.zeros((sct_ref.shape[1] - N_DEV, cols), F32)], axis=0).astype(BF16)
        gadaw_ref[...] = _dot(sct_ref[...], g_pad)
        return _colsum(g_all)

    return steps


def _small_grad_exchange(gmod8, sc_t, msg_vec, msg_gate, after):
    cols = gmod8.shape[1]
    vec_rows = GB_BASE + N_DEV
    route_steps = _route_mod_grad_steps(cols)

    def body(gmod_ref, sct_ref, vec_ref, gate_ref, after_ref, gadaw_ref, sumv_ref, sumg_ref,
             sendbuf, grecv, send_a, recv_a, myv, myg, sibv, sibg, chipv, chipg, sib_send, sib_recv, peer_send, peer_recv):
        x, y, c = me = _position()
        my_chip = 2 * x + y

        def swap(a, src, dst):
            return pltpu.make_async_remote_copy(
                src_ref=src, dst_ref=dst, send_sem=sib_send.at[a], recv_sem=sib_recv.at[a], device_id=(x, y, 1 - c),
                device_id_type=MESH)

        def chip_copy(a, buf, j, k, slot_chip):
            peer = _flip(me, k)
            return pltpu.make_async_remote_copy(
                src_ref=buf.at[slot_chip], dst_ref=buf.at[slot_chip], send_sem=peer_send.at[3 * a + j],
                recv_sem=peer_recv.at[3 * a + j], device_id=peer, device_id_type=MESH)

        def chip_stage(a, mine, theirs, buf):
            buf[my_chip] = mine[...] + theirs[...]
            sends = [chip_copy(a, buf, j, k, my_chip) for j, k in enumerate(CHIP_FLIPS)]
            for cp in sends:
                cp.start()
            return sends

        myg[...] = gate_ref[...]
        swap_g = swap(1, myg, sibg)
        swap_g.start()
        gb = route_steps(gmod_ref, sct_ref, gadaw_ref, sendbuf, grecv, send_a, recv_a)
        swap_g.wait_recv()
        sends = chip_stage(1, myg, sibg, chipg)

        myv[0:GB_BASE, :] = vec_ref[...]
        slot = lax.broadcasted_iota(jnp.int32, (N_DEV, D), 0) == _linear(me)
        gb_wide = jnp.concatenate([jnp.broadcast_to(gb, (N_DEV, cols)), jnp.zeros((N_DEV, D - cols), F32)], axis=1)
        myv[GB_BASE:vec_rows, :] = jnp.where(slot, gb_wide, 0.0)
        swap_v = swap(0, myv, sibv)
        swap_v.start()
        swap_v.wait_recv()
        sends += chip_stage(0, myv, sibv, chipv)

        for a, buf in enumerate((chipv, chipg)):
            for j, k in enumerate(CHIP_FLIPS):
                peer = _flip(me, k)
                chip_copy(a, buf, j, k, 2 * peer[0] + peer[1]).wait_recv()
        sumv_ref[...] = ((chipv[0] + chipv[1]) + chipv[2]) + chipv[3]
        sumg_ref[...] = ((chipg[0] + chipg[1]) + chipg[2]) + chipg[3]
        for cp in [swap_g, swap_v] + sends:
            cp.wait_send()

    vshape, gshape = (vec_rows, D), msg_gate.shape
    return pl.pallas_call(
        body,
        name="small_grad_exchange",
        in_specs=[WHOLE] * 5,
        out_specs=[WHOLE] * 3,
        out_shape=[jax.ShapeDtypeStruct((D, cols), F32), jax.ShapeDtypeStruct(vshape, F32),
                   jax.ShapeDtypeStruct(gshape, F32)],
        scratch_shapes=[pltpu.VMEM((N_DEV, HALO, cols), F32), pltpu.VMEM((N_DEV, HALO, cols), F32),
                        pltpu.SemaphoreType.DMA((N_DEV - 1,)), pltpu.SemaphoreType.DMA((N_DEV - 1,)),
                        pltpu.VMEM(vshape, F32), pltpu.VMEM(gshape, F32), pltpu.VMEM(vshape, F32),
                        pltpu.VMEM(gshape, F32), pltpu.VMEM((4,) + vshape, F32), pltpu.VMEM((4,) + gshape, F32),
                        pltpu.SemaphoreType.DMA((2,)), pltpu.SemaphoreType.DMA((2,)),
                        pltpu.SemaphoreType.DMA((2 * len(CHIP_FLIPS),)), pltpu.SemaphoreType.DMA((2 * len(CHIP_FLIPS),))],
        compiler_params=pltpu.CompilerParams(vmem_limit_bytes=VMEM_LIMIT),
    )(gmod8, sc_t, msg_vec, msg_gate, after)


def _adamw_math(w, g, m, v):
    m = ADAM_B1 * m + (1.0 - ADAM_B1) * g
    v = ADAM_B2 * v + (1.0 - ADAM_B2) * (g * g)
    m_hat = m / (1.0 - ADAM_B1 ** ADAM_STEP)
    v_hat = v / (1.0 - ADAM_B2 ** ADAM_STEP)
    delta = -ADAM_LR * (m_hat / (jnp.sqrt(v_hat) + ADAM_EPS) + ADAM_WD * w)
    return delta, m, v


def _adamw(name, w, g, m, v):
    rows, cols = w.shape
    rb = 256 if rows % 256 == 0 else rows

    def body(w_ref, g_ref, m_ref, v_ref, d_ref, mo_ref, vo_ref):
        d_ref[...], mo_ref[...], vo_ref[...] = _adamw_math(w_ref[...], g_ref[...], m_ref[...], v_ref[...])

    spec = pl.BlockSpec((rb, cols), lambda r: (r, 0))
    return pl.pallas_call(
        body,
        name="adamw_" + name,
        grid=(rows // rb,),
        in_specs=[spec] * 4,
        out_specs=[spec] * 3,
        out_shape=[jax.ShapeDtypeStruct((rows, cols), F32)] * 3,
        compiler_params=pltpu.CompilerParams(dimension_semantics=("arbitrary",)),
    )(w, g, m, v)


def _update(pos, sum_jobs, plain_jobs, after):
    rb = 256
    jobs = [("sum", j) for j in sum_jobs] + [("plain", j) for j in plain_jobs]
    offs, total = [], 0
    for _, j in jobs:
        offs.append(total)
        total += j[-1].shape[0] // rb
    n_in = sum(len(j) for _, j in jobs)

    def body(pos_ref, *refs):
        ins, outs = refs[:n_in], refs[n_in + 1:]
        s = pl.program_id(0)
        i_in = i_out = 0
        for (kind, j), off in zip(jobs, offs):
            steps = j[-1].shape[0] // rb
            j_in = ins[i_in:i_in + len(j)]
            i_in += len(j)
            j_out = outs[i_out:i_out + (4 if kind == "sum" else 3)]
            i_out += len(j_out)

            @pl.when((s >= off) & (s < off + steps))
            def _(kind=kind, j_in=j_in, j_out=j_out):
                if kind == "sum":
                    mine_ref, recv_ref, chips_ref, w_ref, m_ref, v_ref = j_in
                    g = mine_ref[0, 0] + recv_ref[0]
                    for q in range(len(CHIP_FLIPS)):
                        g = g + chips_ref[q].astype(F32)
                    j_out[0][...] = g
                    rest = j_out[1:]
                else:
                    g_ref, w_ref, m_ref, v_ref = j_in
                    g = g_ref[...]
                    rest = j_out
                rest[0][...], rest[1][...], rest[2][...] = _adamw_math(w_ref[...], g, m_ref[...], v_ref[...])

    in_specs, out_specs, out_shape, args = [], [], [], []
    for (kind, j), off in zip(jobs, offs):
        rows, cols = j[-1].shape
        steps = rows // rb
        blk = lambda s, off=off, steps=steps: jnp.clip(s - off, 0, steps - 1)
        flat = pl.BlockSpec((rb, cols), lambda s, pos, blk=blk: (blk(s), 0))
        if kind == "sum":
            in_specs += [pl.BlockSpec((1, 1, rb, cols), lambda s, pos, blk=blk: (pos[1], 0, blk(s), 0)),
                         pl.BlockSpec((1, rb, cols), lambda s, pos, blk=blk: (pos[1], blk(s), 0)),
                         pl.BlockSpec((len(CHIP_FLIPS), rb, cols), lambda s, pos, blk=blk: (0, blk(s), 0))]
            in_specs += [flat] * 3
        else:
            in_specs += [flat] * 4
        n_res = 4 if kind == "sum" else 3
        out_specs += [flat] * n_res
        out_shape += [jax.ShapeDtypeStruct((rows, cols), F32)] * n_res
        args += list(j)
    in_specs += [pl.BlockSpec(after.shape, lambda s, pos: (0,) * after.ndim)]
    outs = pl.pallas_call(
        body,
        name="update",
        grid_spec=pltpu.PrefetchScalarGridSpec(
            num_scalar_prefetch=1, grid=(total,), in_specs=in_specs, out_specs=out_specs),
        out_shape=out_shape,
        compiler_params=pltpu.CompilerParams(dimension_semantics=("arbitrary",), vmem_limit_bytes=VMEM_LIMIT),
    )(pos, *args, after)
    sums = [tuple(outs[4 * i:4 * i + 4]) for i in range(len(sum_jobs))]
    base = 4 * len(sum_jobs)
    plains = [tuple(outs[base + 3 * i:base + 3 * i + 3]) for i in range(len(plain_jobs))]
    return sums, plains


def _adamw_small(ws, gs, ms, vs, sigmoid_scaled):
    n = len(ws)

    def body(*refs):
        w_refs, g_refs, m_refs, v_refs = (refs[i * n:(i + 1) * n] for i in range(4))
        outs = refs[4 * n:]
        for i in range(n):
            w = w_refs[i][...]
            g = g_refs[i][...]
            if sigmoid_scaled[i]:
                g = g * _sigmoid(w)
            delta, m, v = _adamw_math(w, g, m_refs[i][...], v_refs[i][...])
            outs[4 * i][...] = g
            outs[4 * i + 1][...] = delta
            outs[4 * i + 2][...] = m
            outs[4 * i + 3][...] = v

    shapes = [jax.ShapeDtypeStruct(w.shape, F32) for w in ws for _ in range(4)]
    outs = pl.pallas_call(
        body,
        name="adamw_small",
        in_specs=[WHOLE] * (4 * n),
        out_specs=[WHOLE] * (4 * n),
        out_shape=shapes,
    )(*ws, *gs, *ms, *vs)
    return [outs[4 * i:4 * i + 4] for i in range(n)]


_WEIGHT_NAMES = ("ada_w", "ada_b", "norm1_g", "w_in", "lru_conv_w", "lru_conv_b", "gate_a_w", "gate_a_b", "gate_x_w",
                 "gate_x_b", "a_param", "short_conv_w", "lru_out_g", "conv_out_g", "w_out", "norm2_g", "w_mlp1",
                 "w_mlp2", "final_g")


def kernel(x, c, ada_w, ada_b, norm1_g, w_in, lru_conv_w, lru_conv_b, gate_a_w, gate_a_b, gate_x_w, gate_x_b, a_param, short_conv_w, lru_out_g, conv_out_g, w_out, norm2_g, w_mlp1, w_mlp2, final_g, loss_target, m_ada_w, m_ada_b, m_norm1_g, m_w_in, m_lru_conv_w, m_lru_conv_b, m_gate_a_w, m_gate_a_b, m_gate_x_w, m_gate_x_b, m_a_param, m_short_conv_w, m_lru_out_g, m_conv_out_g, m_w_out, m_norm2_g, m_w_mlp1, m_w_mlp2, m_final_g, v_ada_w, v_ada_b, v_norm1_g, v_w_in, v_lru_conv_w, v_lru_conv_b, v_gate_a_w, v_gate_a_b, v_gate_x_w, v_gate_x_b, v_a_param, v_short_conv_w, v_lru_out_g, v_conv_out_g, v_w_out, v_norm2_g, v_w_mlp1, v_w_mlp2, v_final_g):
    given = dict(locals())
    weights = {n: given[n] for n in _WEIGHT_NAMES}
    xi, yi, ci = _position()
    me_lin = _linear((xi, yi, ci))
    hd = W // N_DEV

    mixer_block = jnp.concatenate([w_out[0], w_in[0].T], axis=0).astype(BF16)
    mlp_block = jnp.concatenate([w_mlp1[0].T, w_mlp2[0]], axis=0).astype(BF16)

    msg = (jnp.pad(c, ((0, HALO - 1), (0, 0)))
           + jnp.pad(lru_conv_w[0], ((1, HALO - 1 - CONV_L), (0, D - hd)))
           + jnp.pad(short_conv_w[0], ((1 + CONV_L, 0), (0, D - hd))))
    gath, mod_all, wmix = _gather_and_mod(msg, ada_w[0], mixer_block)
    sc_all = gath[:, 0, :]
    wl = jnp.transpose(gath[:, 1:1 + CONV_L, :hd], (1, 0, 2)).reshape(CONV_L, W)
    ws = jnp.transpose(gath[:, 1 + CONV_L:HALO, :hd], (1, 0, 2)).reshape(CONV_S, W)
    modraw = _pad_rows(mod_all[:, 0, :].reshape(6, D), HALO)
    adab = _pad_rows(ada_b.reshape(6, D), HALO)

    x2d, tgt = x[0], loss_target[0]
    gf = final_g.reshape(1, D)
    bda = _block_diag(gate_a_w[0]).astype(BF16)
    bdx = _block_diag(gate_x_w[0]).astype(BF16)
    avg = _block_diag(jnp.full((8, W // 8, W // 8), 8.0 / W, F32)).astype(BF16)
    wl8 = _pad_rows(wl, HALO)
    ws8 = _pad_rows(ws, HALO)
    mixer_small = (wl8, lru_conv_b, bda, bdx, gate_a_b, gate_x_b, a_param, ws8, lru_out_g, conv_out_g, avg)
    proj, hl, mixed, kept, wmlp = _mixer_fwd(x2d, modraw, adab, norm1_g, *mixer_small, wmix, mlp_block)
    h2t, f, dx2, dz, vec2, loss8 = _mlp_fwd(x2d, mixed, tgt, modraw, adab, norm2_g, gf, wmlp)
    pos = jnp.stack([ci, 2 * xi + yi]).astype(jnp.int32)
    by_dest = lambda g: g.reshape((4, 2, -1) + g.shape[-1:])
    dh2_first, *for_sibling = _mlp_bwd_half(pos, h2t, f, dz, wmlp)
    sib_send, sib_recv, sib_thru, sib_land, token = _chips_start("mlp", for_sibling, stage="sibling")
    dh2, dw1, dw2 = _mlp_bwd_half(pos, h2t, f, dz, wmlp, prior=(dh2_first, token))
    done = dh2[0:HALO, 0:128] + dw1[0, 0:HALO, 0:128] + dw2[0, 0:HALO, 0:128]
    _, mlp_sib = _chips_wait("mlp", sib_send, sib_recv, sib_thru, sib_land, done, stage="sibling")
    mlp_parts = [dw1[:, None], dw2[:, None]]
    mlp_sums = [_pair_sum(pos, p, r) for p, r in zip(mlp_parts, mlp_sib)]
    mlp_send, mlp_recv, mlp_thru, mlp_land, token = _chips_start("mlp", mlp_sums)
    modraw_after = modraw + jnp.tile(token, (1, D // token.shape[1]))
    gx, vec, hb, dproj_t, dmixed, ycat_t, xl_t, dgate = _mixer_bwd(
        x2d, mixed, dh2, dx2, proj, hl, kept, modraw_after, adab, norm1_g, norm2_g, *mixer_small, wmix)
    dwint = _matmul("wgrad_in", dproj_t, hb)
    dwout = _matmul("wgrad_out", ycat_t, dmixed)
    gate_blocks = _gate_wgrad(xl_t, dgate, avg)
    msg_gate = gate_blocks.reshape(W, 128)
    done = dwint[0:HALO, 0:128] + dwout[0:HALO, 0:128] + gate_blocks[0, 0:HALO, :].sum() + gx[0:HALO, 0:128]
    _, mlp_chips = _chips_wait("mlp", mlp_send, mlp_recv, mlp_thru, mlp_land, done)
    mix_parts = [by_dest(dwout), by_dest(dwint)]
    gmod8 = (jnp.pad(vec[0:5], ((0, 1), (0, 0))) + jnp.pad(vec2[0:1], ((5, 0), (0, 0)))).reshape(N_DEV, 6 * D // N_DEV)
    sc_t = jnp.pad(sc_all.T, ((0, 0), (0, 128 - N_DEV))).astype(BF16)
    loss_rows = jnp.pad(loss8[0:1], ((0, HALO - 1), (0, D - loss8.shape[1])))
    msg_vec = jnp.concatenate([vec, vec2, loss_rows], axis=0)
    sib_send, sib_recv, sib_thru, sib_land, token = _chips_start("mixer", mix_parts, stage="sibling")
    g_adaw, sum_vec, sum_gate = _small_grad_exchange(gmod8, sc_t, msg_vec, msg_gate, token)
    mix_parts, mix_sib = _chips_wait("mixer", sib_send, sib_recv, sib_thru, sib_land, sum_vec[0:HALO, 0:128],
                                     stage="sibling")
    mix_sums = [_pair_sum(pos, p, r) for p, r in zip(mix_parts, mix_sib)]
    state = lambda n: (weights[n][0], given["m_" + n][0], given["v_" + n][0])
    mlp_jobs = [(p, r, q, *state(n)) for p, r, q, n in zip(mlp_parts, mlp_sib, mlp_chips, ("w_mlp1", "w_mlp2"))]
    mix_send, mix_recv, mix_thru, mix_land, token = _chips_start("mixer", mix_sums)
    mlp_done, (adaw_done,) = _update(pos, mlp_jobs, [(g_adaw, *state("ada_w"))], token)
    loss = sum_vec[LOSS_ROW, 0]
    sum_gate = sum_gate.reshape(2, W, W // 8)
    lo, hi = slice(0, W), slice(W, 2 * W)
    wl_full = sum_vec[V_WL01:V_WL23 + 1].reshape(CONV_L, W)
    ws_full = sum_vec[V_WS01:V_WS2 + 1].reshape(CONV_S + 1, W)[:CONV_S]
    row = lambda r, cols: sum_vec[r:r + 1, cols]
    small_grads = {
        "ada_b": sum_vec[GB_BASE:GB_BASE + N_DEV, :6 * D // N_DEV].reshape(1, 6 * D),
        "norm1_g": row(V_G1, slice(0, D)),
        "lru_conv_w": lax.dynamic_slice(wl_full, (0, me_lin * hd), (CONV_L, hd)),
        "lru_conv_b": row(V_BL_BA, lo),
        "gate_a_w": sum_gate[0],
        "gate_a_b": row(V_BL_BA, hi),
        "gate_x_w": sum_gate[1],
        "gate_x_b": row(V_BX_SP, lo),
        "a_param": row(V_BX_SP, hi),
        "short_conv_w": lax.dynamic_slice(ws_full, (0, me_lin * hd), (CONV_S, hd)),
        "lru_out_g": row(V_GL_GC, lo),
        "conv_out_g": row(V_GL_GC, hi),
        "norm2_g": row(V_G2, slice(0, D)),
        "final_g": sum_vec[V_ROWS + 1:V_ROWS + 2, :],
    }
    names = list(small_grads)
    as2d = lambda a, n: a.reshape(small_grads[n].shape)
    small = _adamw_small([as2d(weights[n], n) for n in names], [small_grads[n] for n in names],
                         [as2d(given["m_" + n], n) for n in names], [as2d(given["v_" + n], n) for n in names],
                         [n == "a_param" for n in names])
    result = {n: tuple(o.reshape(weights[n].shape) for o in outs) for n, outs in zip(names, small)}

    done = (small[0][1][:, 0:128] + mlp_done[0][2][0:HALO, 0:128] + mlp_done[1][2][0:HALO, 0:128]
            + adaw_done[1][0:HALO, 0:128])
    _, mix_chips = _chips_wait("mixer", mix_send, mix_recv, mix_thru, mix_land, done)
    g_wout, g_wint = (_final_sum(pos, p, r, q) for p, r, q in zip(mix_parts, mix_sib, mix_chips))
    for n, g in (("w_in", g_wint.T), ("w_out", g_wout)):
        w, m, v = state(n)
        result[n] = (g[None],) + tuple(o[None] for o in _adamw(n, w, g, m, v))
    result["w_mlp1"], result["w_mlp2"] = (tuple(o[None] for o in done) for done in mlp_done)
    result["ada_w"] = (g_adaw[None],) + tuple(o[None] for o in adaw_done)

    return (loss, gx[None], *[result[n][0] for n in _WEIGHT_NAMES], *[result[n][1] for n in _WEIGHT_NAMES],
            *[result[n][2] for n in _WEIGHT_NAMES], *[result[n][3] for n in _WEIGHT_NAMES])
```

```python
import jax
import jax.numpy as jnp
from jax import lax
from jax.experimental import pallas as pl
from jax.experimental.pallas import tpu as pltpu

F32 = jnp.float32
BF16 = jnp.bfloat16
MESH = pl.DeviceIdType.MESH

N_DEV = 8
D = 1024
W = 512
D_IN = 5 * W
D_FF = 4096
FF_BLK = D_FF // N_DEV
EPS = 1e-6
C_GATE = 8.0
CONV_L = 4
CONV_S = 3
HALO = 8

ROWS_W1T, ROWS_W2, ROWS_WOUT, ROWS_WIN = FF_BLK, FF_BLK, D // N_DEV, D_IN // N_DEV
OFF_WOUT = 0
OFF_WIN = OFF_WOUT + ROWS_WOUT
MIX_ROWS = OFF_WIN + ROWS_WIN
OFF_W1T = 0
OFF_W2 = OFF_W1T + ROWS_W1T
MLP_ROWS = OFF_W2 + ROWS_W2
CHIP_FLIPS = (4, 2, 6)
N_KEPT = 6

ADAM_LR = 0.001
ADAM_B1 = 0.9
ADAM_B2 = 0.999
ADAM_EPS = 1e-08
ADAM_WD = 0.01
ADAM_STEP = 10

VMEM_LIMIT = 56 * 1024 * 1024

TB_MIX = 256
TB_MIXB = 256
TB_MLP = 256
TB_MLPB = 512

ANY = pl.BlockSpec(memory_space=pl.ANY)
WHOLE = pl.BlockSpec(memory_space=pltpu.VMEM)


def _dot(a, b):
    return jnp.dot(a, b, preferred_element_type=F32)


def _dot_nt(a, b):
    return lax.dot_general(a, b, (((1,), (1,)), ((), ())), preferred_element_type=F32)


def _dot_tn(a, b):
    return lax.dot_general(a, b, (((0,), (0,)), ((), ())), preferred_element_type=F32)


def _sigmoid(v):
    return 1.0 / (1.0 + jnp.exp(-v))


def _softplus(v):
    t = jnp.exp(-jnp.abs(v))
    small = t * (1.0 - t * (0.5 - t * (1.0 / 3.0)))
    return jnp.maximum(v, 0.0) + jnp.where(t < 1e-2, small, jnp.log(1.0 + t))


def _one_minus_sq(a, log_a):
    return -jnp.tanh(log_a) * (a * a + 1.0)


_GELU_K = 0.7978845608028654
_GELU_C = 0.044715


def _gelu(u):
    th = jnp.tanh(_GELU_K * (u + _GELU_C * u * u * u))
    return 0.5 * u * (1.0 + th), th


def _gelu_grad(u, th):
    return 0.5 * (1.0 + th) + 0.5 * u * (1.0 - th * th) * _GELU_K * (1.0 + 3.0 * _GELU_C * u * u)


def _group_mean(v, avg):
    hi = v.astype(BF16)
    lo = (v - hi.astype(F32)).astype(BF16)
    return _dot(hi, avg) + _dot(lo, avg)


def _colsum(v):
    return jnp.sum(v, axis=0, keepdims=True)


def _rowmean(v):
    return jnp.mean(v, axis=-1, keepdims=True)


def _load_packed(wpack_hbm, off, rows, dst, sem):
    copies = [
        pltpu.make_async_copy(wpack_hbm.at[d, pl.ds(off, rows), :], dst.at[pl.ds(d * rows, rows), :], sem)
        for d in range(N_DEV)
    ]
    for cp in copies:
        cp.start()
    return copies


def _scan_groups(n_groups, a_ref, b_ref, out_ref, carry_ref, reverse):
    row = lax.broadcasted_iota(jnp.int32, (HALO, W), 0)

    def step(k, carry):
        g = (n_groups - 1 - k) if reverse else k
        rows = pl.ds(pl.multiple_of(g * HALO, HALO), HALO)
        a = a_ref[rows, :]
        b = b_ref[rows, :]
        for s in (1, 2, 4):
            if reverse:
                keep = row < HALO - s
                sh = HALO - s
            else:
                keep = row >= s
                sh = s
            a_sh = pltpu.roll(a, sh, axis=0)
            b_sh = pltpu.roll(b, sh, axis=0)
            b = jnp.where(keep, a * b_sh + b, b)
            a = jnp.where(keep, a * a_sh, a)
        h = b + a * carry
        out_ref[rows, :] = h
        edge = h[0:1, :] if reverse else h[HALO - 1:HALO, :]
        return jnp.broadcast_to(edge, (HALO, W))

    carry_ref[...] = lax.fori_loop(0, n_groups, step, carry_ref[...])


def _route_peers(me):
    x, y, c = me
    first = ((x + 1 - c) % 2, (y + c) % 2, c)
    second = ((x + c) % 2, (y + 1 - c) % 2, c)
    return first, second, (1 - x, 1 - y, c)


def _chip_gather_copies(block_hbm, out_hbm, send_sems, recv_sems):
    me = _position()
    first, second, diag = _route_peers(me)

    def copy(j, src, slot_of, to):
        return pltpu.make_async_remote_copy(
            src_ref=src, dst_ref=out_hbm.at[_linear(slot_of)], send_sem=send_sems.at[j], recv_sem=recv_sems.at[j],
            device_id=to, device_id_type=MESH)

    own_sends = [copy(0, block_hbm, me, first), copy(1, block_hbm, me, second)]
    forward = copy(2, out_hbm.at[_linear(first)], first, second)
    arrivals = [copy(0, block_hbm, first, first), copy(1, block_hbm, second, second), copy(2, block_hbm, diag, second)]
    return own_sends, forward, arrivals


def _mixer_fwd(x, modraw, adab, g1, wl, bl, bda, bdx, ba, bxb, ap, ws, gl, gc, avg, wpack, mlp_block):
    t_len = x.shape[0]
    tb = TB_MIX
    nb = t_len // tb

    def body(x_ref, modraw_ref, adab_ref, g1_ref, wl_ref, bl_ref, bda_ref, bdx_ref, ba_ref, bxb_ref, ap_ref,
             ws_ref, gl_ref, gc_ref, avg_ref, wpack_hbm, block_hbm, proj_ref, hl_ref, mixed_ref, kept_ref, wmlp_hbm,
             win_v, wout_v, sem, ulx_ext, cv_ext, hcar, a_s, b_s, send_sems, recv_sems, sib_send_sems, sib_recv_sems,
             local_sem):
        i = pl.program_id(0)
        x_pos, y_pos, c_pos = me = _position()
        sibling = (x_pos, y_pos, 1 - c_pos)
        own = pltpu.make_async_copy(block_hbm, wmlp_hbm.at[_linear(me)], local_sem)
        sends, forward, arrivals = _chip_gather_copies(block_hbm, wmlp_hbm, send_sems, recv_sems)

        def to_sibling(j, block_of, src=None):
            dst = wmlp_hbm.at[_linear(block_of)]
            return pltpu.make_async_remote_copy(
                src_ref=dst if src is None else src, dst_ref=dst, send_sem=sib_send_sems.at[j],
                recv_sem=sib_recv_sems.at[j], device_id=sibling, device_id_type=MESH)

        passes = [to_sibling(0, me, src=block_hbm)] + [to_sibling(1 + j, p) for j, p in enumerate(_route_peers(me))]

        @pl.when(i == 0)
        def _():
            own.start()
            for cp in sends:
                cp.start()
            passes[0].start()

        @pl.when(i == nb - 1)
        def _():
            arrivals[0].wait_recv()
            forward.start()
            passes[1].start()

        @pl.when(i == 0)
        def _():
            cps = _load_packed(wpack_hbm, OFF_WIN, ROWS_WIN, win_v, sem.at[0])
            cps += _load_packed(wpack_hbm, OFF_WOUT, ROWS_WOUT, wout_v, sem.at[1])
            ulx_ext[0:HALO, :] = jnp.zeros((HALO, W), F32)
            cv_ext[0:HALO, :] = jnp.zeros((HALO, W), F32)
            hcar[...] = jnp.zeros((HALO, W), F32)
            for cp in cps:
                cp.wait()

        mod = modraw_ref[...] + adab_ref[...]
        shift1, scale1, gate1 = mod[0:1], mod[1:2], mod[2:3]
        x = x_ref[...]
        r1 = lax.rsqrt(_rowmean(x * x) + EPS)
        h = (x * r1 * g1_ref[...]) * (1.0 + scale1) + shift1
        proj = _dot_nt(h.astype(BF16), win_v[...])
        proj_ref[...] = proj
        u_lx, u_ly, u_b, u_c, u_v = (proj[:, k * W:(k + 1) * W] for k in range(5))

        ulx_ext[HALO:HALO + tb, :] = u_lx
        xl = bl_ref[...] + wl_ref[CONV_L - 1:CONV_L, :] * u_lx
        for k in range(CONV_L - 1):
            xl = xl + wl_ref[k:k + 1, :] * ulx_ext[pl.ds(HALO - (CONV_L - 1) + k, tb), :]
        ulx_ext[0:HALO, :] = ulx_ext[tb:tb + HALO, :]
        xlb = xl.astype(BF16)
        r = _sigmoid(_dot(xlb, bda_ref[...]) + ba_ref[...])
        ig = _sigmoid(_dot(xlb, bdx_ref[...]) + bxb_ref[...])
        log_a = (-C_GATE) * r * _softplus(ap_ref[...])
        a = jnp.exp(log_a)
        mult = jnp.sqrt(_one_minus_sq(a, log_a))
        grow = i * tb + lax.broadcasted_iota(jnp.int32, (tb, W), 0)
        mult = jnp.where(grow == 0, 1.0, mult)
        a_s[...] = a
        b_s[...] = mult * (ig * xl)
        _scan_groups(tb // HALO, a_s, b_s, hl_ref, hcar, reverse=False)
        hl = hl_ref[...]
        ge, _ = _gelu(u_ly)
        p = ge * hl
        rp = lax.rsqrt(_group_mean(p * p, avg_ref[...]) + EPS)
        y_lru = p * rp * gl_ref[...]

        cv = u_c * u_v
        cv_ext[HALO:HALO + tb, :] = cv
        cc = ws_ref[CONV_S - 1:CONV_S, :] * cv
        for k in range(CONV_S - 1):
            cc = cc + ws_ref[k:k + 1, :] * cv_ext[pl.ds(HALO - (CONV_S - 1) + k, tb), :]
        cv_ext[0:HALO, :] = cv_ext[tb:tb + HALO, :]
        q = u_b * cc
        rq = lax.rsqrt(_group_mean(q * q, avg_ref[...]) + EPS)
        y_conv = q * rq * gc_ref[...]
        for k, kept in enumerate((xl, r, ig, rp, rq, cc)):
            kept_ref[:, k * W:(k + 1) * W] = kept

        mixed_ref[...] = (_dot(y_lru.astype(BF16), wout_v[0:W, :]) + _dot(y_conv.astype(BF16), wout_v[W:2 * W, :]))

        @pl.when(i == nb - 1)
        def _():
            arrivals[1].wait_recv()
            passes[2].start()
            arrivals[2].wait_recv()
            passes[3].start()
            for j, p in enumerate((sibling,) + _route_peers(sibling)):
                to_sibling(j, p).wait_recv()
            for cp in sends + [forward] + passes:
                cp.wait_send()
            own.wait()

    tok = lambda cols: pl.BlockSpec((tb, cols), lambda i: (i, 0))
    full = lambda a: pl.BlockSpec(a.shape, lambda i: (0,) * a.ndim)
    small = (modraw, adab, g1, wl, bl, bda, bdx, ba, bxb, ap, ws, gl, gc, avg)
    n_chips = len(CHIP_FLIPS)
    return pl.pallas_call(
        body,
        name="mixer_fwd",
        grid=(nb,),
        in_specs=[tok(D)] + [full(a) for a in small] + [ANY, ANY],
        out_specs=[tok(D_IN), tok(W), tok(D), tok(N_KEPT * W), ANY],
        out_shape=[jax.ShapeDtypeStruct((t_len, D_IN), F32), jax.ShapeDtypeStruct((t_len, W), F32),
                   jax.ShapeDtypeStruct((t_len, D), F32), jax.ShapeDtypeStruct((t_len, N_KEPT * W), F32),
                   jax.ShapeDtypeStruct((N_DEV,) + mlp_block.shape, BF16)],
        scratch_shapes=[pltpu.VMEM((D_IN, D), BF16), pltpu.VMEM((D, D), BF16), pltpu.SemaphoreType.DMA((2,)),
                        pltpu.VMEM((tb + HALO, W), F32), pltpu.VMEM((tb + HALO, W), F32), pltpu.VMEM((HALO, W), F32),
                        pltpu.VMEM((tb, W), F32), pltpu.VMEM((tb, W), F32),
                        pltpu.SemaphoreType.DMA((n_chips,)), pltpu.SemaphoreType.DMA((n_chips,)),
                        pltpu.SemaphoreType.DMA((4,)), pltpu.SemaphoreType.DMA((4,)), pltpu.SemaphoreType.DMA],
        compiler_params=pltpu.CompilerParams(dimension_semantics=("arbitrary",), vmem_limit_bytes=VMEM_LIMIT),
    )(x, *small, wpack, mlp_block)


def _mlp_fwd(x, mixed, tgt, modraw, adab, g2, gf, wpack):
    t_len = x.shape[0]
    tb = TB_MLP
    nb = t_len // tb

    def body(x_ref, mixed_ref, tgt_ref, modraw_ref, adab_ref, g2_ref, gf_ref, wpack_hbm,
             h2t_ref, f_ref, dx2_ref, dz_ref, vec_ref, loss_ref, w1t_v, w2_v, sem):
        i = pl.program_id(0)

        @pl.when(i == 0)
        def _():
            cps = _load_packed(wpack_hbm, OFF_W1T, ROWS_W1T, w1t_v, sem.at[0])
            cps += _load_packed(wpack_hbm, OFF_W2, ROWS_W2, w2_v, sem.at[1])
            vec_ref[...] = jnp.zeros(vec_ref.shape, F32)
            loss_ref[...] = jnp.zeros(loss_ref.shape, F32)
            for cp in cps:
                cp.wait()

        mod = modraw_ref[...] + adab_ref[...]
        gate1, shift2, scale2, gate2 = mod[2:3], mod[3:4], mod[4:5], mod[5:6]
        x1 = x_ref[...] + gate1 * mixed_ref[...]
        r2 = lax.rsqrt(_rowmean(x1 * x1) + EPS)
        h2 = (x1 * r2 * g2_ref[...]) * (1.0 + scale2) + shift2
        h2b = h2.astype(BF16)
        h2t_ref[...] = h2.T.astype(BF16)
        z = jnp.zeros((tb, D), F32)
        for j in range(N_DEV):
            cols = slice(j * FF_BLK, (j + 1) * FF_BLK)
            fj = _dot_nt(h2b, w1t_v[cols, :])
            f_ref[:, cols] = fj
            rf = jnp.maximum(fj, 0.0)
            z = z + _dot((rf * rf).astype(BF16), w2_v[cols, :])
        x2 = x1 + gate2 * z
        r3 = lax.rsqrt(_rowmean(x2 * x2) + EPS)
        xn3 = x2 * r3
        diff = xn3 * gf_ref[...] - tgt_ref[...]
        sq = _colsum(diff * diff)
        loss_ref[...] += jnp.broadcast_to(jnp.sum(sq, axis=1, keepdims=True) * (0.5 / D), loss_ref.shape)
        dy = diff * (1.0 / D)
        dyn = dy * gf_ref[...]
        dx2 = r3 * (dyn - xn3 * _rowmean(dyn * xn3))
        dx2_ref[...] = dx2
        dz_ref[...] = (gate2 * dx2).astype(BF16)
        vec_ref[0:1, :] += _colsum(dx2 * z)
        vec_ref[1:2, :] += _colsum(dy * xn3)

    tok = lambda cols: pl.BlockSpec((tb, cols), lambda i: (i, 0))
    tok_t = pl.BlockSpec((D, tb), lambda i: (0, i))
    full = lambda a: pl.BlockSpec(a.shape, lambda i: (0,) * a.ndim)
    small = (modraw, adab, g2, gf)
    return pl.pallas_call(
        body,
        name="mlp_fwd",
        grid=(nb,),
        in_specs=[tok(D), tok(D), tok(D)] + [full(a) for a in small] + [ANY],
        out_specs=[tok_t, tok(D_FF), tok(D), tok(D), pl.BlockSpec((8, D), lambda i: (0, 0)),
                   pl.BlockSpec((8, 128), lambda i: (0, 0))],
        out_shape=[jax.ShapeDtypeStruct((D, t_len), BF16), jax.ShapeDtypeStruct((t_len, D_FF), F32),
                   jax.ShapeDtypeStruct((t_len, D), F32), jax.ShapeDtypeStruct((t_len, D), BF16),
                   jax.ShapeDtypeStruct((8, D), F32), jax.ShapeDtypeStruct((8, 128), F32)],
        scratch_shapes=[pltpu.VMEM((D_FF, D), BF16), pltpu.VMEM((D_FF, D), BF16), pltpu.SemaphoreType.DMA((2,))],
        compiler_params=pltpu.CompilerParams(dimension_semantics=("arbitrary",), vmem_limit_bytes=VMEM_LIMIT),
    )(x, mixed, tgt, *small, wpack)


def _mlp_bwd_half(pos, h2t, f, dz, wpack, prior=None):
    t_len = dz.shape[0]
    tb = TB_MLPB
    nb = t_len // tb
    first = prior is None
    flip = 1 if first else 0

    def body(pos_ref, h2t_ref, f_ref, dz_ref, w1t_ref, w2_ref, *rest):
        if first:
            dh2_ref, dw1_ref, dw2_ref = rest
        else:
            dh2in_ref, _, dh2_ref, dw1_ref, dw2_ref = rest
        k = pl.program_id(0)
        t = pl.program_id(1)
        rows = pl.ds(pl.multiple_of(t * tb, tb), tb)
        w1t = w1t_ref[0]
        w2 = w2_ref[0]
        dz = dz_ref[...]
        rf = jnp.maximum(f_ref[...], 0.0)
        df = (_dot_nt(dz, w2) * (2.0 * rf)).astype(BF16)
        dh = _dot(df, w1t)
        g1 = _dot(h2t_ref[...], df)
        g2 = _dot_tn((rf * rf).astype(BF16), dz)

        @pl.when(t == 0)
        def _():
            dw2_ref[0] = g2
            dw1_ref[0] = g1

        @pl.when(t != 0)
        def _():
            dw2_ref[0] += g2
            dw1_ref[0] += g1

        @pl.when(k == 0)
        def _():
            dh2_ref[rows, :] = dh if first else dh2in_ref[...] + dh

        @pl.when(k != 0)
        def _():
            dh2_ref[rows, :] += dh

    blk = lambda k, pos: 2 * k + jnp.bitwise_xor(pos[0], flip)
    in_specs = [pl.BlockSpec((D, tb), lambda k, t, pos: (0, t)),
                pl.BlockSpec((tb, FF_BLK), lambda k, t, pos: (t, blk(k, pos))),
                pl.BlockSpec((tb, D), lambda k, t, pos: (t, 0)),
                pl.BlockSpec((1, ROWS_W1T, D), lambda k, t, pos: (blk(k, pos), OFF_W1T // ROWS_W1T, 0)),
                pl.BlockSpec((1, ROWS_W2, D), lambda k, t, pos: (blk(k, pos), OFF_W2 // ROWS_W2, 0))]
    grad_specs = [pl.BlockSpec((1, D, FF_BLK), lambda k, t, pos: (k, 0, 0)),
                  pl.BlockSpec((1, FF_BLK, D), lambda k, t, pos: (k, 0, 0))]
    out_specs = [pl.BlockSpec((t_len, D), lambda k, t, pos: (0, 0))] + grad_specs
    grad_shapes = [jax.ShapeDtypeStruct((4, D, FF_BLK), F32), jax.ShapeDtypeStruct((4, FF_BLK, D), F32)]
    out_shape = [jax.ShapeDtypeStruct((t_len, D), F32)] + grad_shapes
    args = [pos, h2t, f, dz, wpack, wpack]
    if not first:
        in_specs += [pl.BlockSpec((tb, D), lambda k, t, pos: (jnp.where(k == 0, t, nb - 1), 0)),
                     pl.BlockSpec(prior[1].shape, lambda k, t, pos: (0,) * prior[1].ndim)]
        args += list(prior)
    return pl.pallas_call(
        body,
        name="mlp_bwd_first" if first else "mlp_bwd_second",
        grid_spec=pltpu.PrefetchScalarGridSpec(num_scalar_prefetch=1, grid=(4, nb), in_specs=in_specs,
                                               out_specs=out_specs),
        out_shape=out_shape,
        compiler_params=pltpu.CompilerParams(dimension_semantics=("arbitrary", "arbitrary"),
                                             vmem_limit_bytes=VMEM_LIMIT),
    )(*args)


V_SHIFT1, V_SCALE1, V_GATE1, V_SHIFT2, V_SCALE2, V_G1, V_G2 = 0, 1, 2, 3, 4, 6, 7
V_BL_BA, V_BX_SP, V_GL_GC, V_WL01, V_WL23, V_WS01, V_WS2 = 8, 9, 10, 11, 12, 13, 14
V_ROWS = 16


def _chip_scatter_copies(srcs, dsts, send_sems, recv_sems):
    me = _position()
    copies = []
    for a, (src, dst) in enumerate(zip(srcs, dsts)):
        for j, k in enumerate(CHIP_FLIPS):
            peer = _flip(me, k)
            copies.append(pltpu.make_async_remote_copy(
                src_ref=src.at[2 * peer[0] + peer[1]], dst_ref=dst.at[j], send_sem=send_sems.at[len(CHIP_FLIPS) * a + j],
                recv_sem=recv_sems.at[len(CHIP_FLIPS) * a + j], device_id=peer, device_id_type=MESH))
    return copies


def _mixer_bwd(x, mixed, dh2, dx2, proj, hl, kept, modraw, adab, g1, g2, wl, bl, bda, bdx, ba, bxb, ap, ws, gl, gc, avg, wpack):
    t_len = x.shape[0]
    tb = TB_MIXB
    nb = t_len // tb
    hb = tb // HALO

    def body(x_ref, mixed_ref, dh2_ref, dx2_ref, proj_ref, projh_ref, hl_ref, hlh_ref, kept_ref,
             modraw_ref, adab_ref, g1_ref, g2_ref, wl_ref, bl_ref, bda_ref, bdx_ref, ba_ref, bxb_ref, ap_ref,
             ws_ref, gl_ref, gc_ref, avg_ref, wpack_hbm,
             gx_ref, vec_ref, hb_ref, dprojt_ref, dmixed_ref, ycatt_ref, xlt_ref, dgate_ref,
             win_v, wout_v, sem, ulx_ext, cv_ext, hl_ext, a_ext, dxl_ext, dcc_ext, dcar, an_s, g_s, dh_s):
        i = pl.program_id(0)
        blk = nb - 1 - i

        @pl.when(i == 0)
        def _():
            cps = _load_packed(wpack_hbm, OFF_WIN, ROWS_WIN, win_v, sem.at[0])
            cps += _load_packed(wpack_hbm, OFF_WOUT, ROWS_WOUT, wout_v, sem.at[1])
            vec_ref[...] = jnp.zeros(vec_ref.shape, F32)
            zero = jnp.zeros((HALO, W), F32)
            a_ext[tb:tb + HALO, :] = zero
            dxl_ext[tb:tb + HALO, :] = zero
            dcc_ext[tb:tb + HALO, :] = zero
            dcar[...] = zero
            for cp in cps:
                cp.wait()

        mod = modraw_ref[...] + adab_ref[...]
        shift1, scale1, gate1, scale2 = mod[0:1], mod[1:2], mod[2:3], mod[4:5]
        x = x_ref[...]
        mixed = mixed_ref[...]

        x1 = x + gate1 * mixed
        r2 = lax.rsqrt(_rowmean(x1 * x1) + EPS)
        xn2 = x1 * r2
        dh2 = dh2_ref[...]
        vec_ref[V_SHIFT2:V_SHIFT2 + 1, :] += _colsum(dh2)
        vec_ref[V_SCALE2:V_SCALE2 + 1, :] += _colsum(dh2 * xn2 * g2_ref[...])
        vec_ref[V_G2:V_G2 + 1, :] += _colsum(dh2 * (1.0 + scale2) * xn2)
        dxn2 = dh2 * g2_ref[...] * (1.0 + scale2)
        dx1 = dx2_ref[...] + r2 * (dxn2 - xn2 * _rowmean(dxn2 * xn2))
        vec_ref[V_GATE1:V_GATE1 + 1, :] += _colsum(dx1 * mixed)
        dmixed = (gate1 * dx1).astype(BF16)

        proj = proj_ref[...]
        u_lx, u_ly, u_b, u_c, u_v = (proj[:, k * W:(k + 1) * W] for k in range(5))
        has_prev = (blk > 0).astype(F32)
        projh = projh_ref[...]
        ulx_ext[0:HALO, :] = projh[:, 0:W] * has_prev
        ulx_ext[HALO:HALO + tb, :] = u_lx
        xl, r, ig, rp, rq, cc = (kept_ref[:, k * W:(k + 1) * W] for k in range(N_KEPT))
        sp = _softplus(ap_ref[...])
        log_a = (-C_GATE) * r * sp
        a = jnp.exp(log_a)
        mult_raw = jnp.sqrt(_one_minus_sq(a, log_a))
        first = (blk * tb + lax.broadcasted_iota(jnp.int32, (tb, W), 0)) == 0
        mult = jnp.where(first, 1.0, mult_raw)
        hl = hl_ref[...]
        ge, th = _gelu(u_ly)
        pn = ge * hl * rp
        cv = u_c * u_v
        cv_ext[0:HALO, :] = projh[:, 3 * W:4 * W] * projh[:, 4 * W:5 * W] * has_prev
        cv_ext[HALO:HALO + tb, :] = cv
        qn = u_b * cc * rq

        dmixed_ref[...] = dmixed
        ycatt_ref[0:W, :] = (pn * gl_ref[...]).T.astype(BF16)
        ycatt_ref[W:2 * W, :] = (qn * gc_ref[...]).T.astype(BF16)
        dyl = _dot_nt(dmixed, wout_v[0:W, :])
        dyc = _dot_nt(dmixed, wout_v[W:2 * W, :])

        dqn = dyc * gc_ref[...]
        dq = rq * (dqn - qn * _group_mean(dqn * qn, avg_ref[...]))
        du_b = dq * cc
        dcc = dq * u_b
        dcc_ext[0:tb, :] = dcc
        dcv = ws_ref[CONV_S - 1:CONV_S, :] * dcc
        for k in range(CONV_S - 1):
            dcv = dcv + ws_ref[k:k + 1, :] * dcc_ext[pl.ds(CONV_S - 1 - k, tb), :]
        dcc_ext[tb:tb + HALO, :] = dcc_ext[0:HALO, :]
        du_c = dcv * u_v
        du_v = dcv * u_c
        dws = [_colsum(dcc * cv_ext[pl.ds(HALO - (CONV_S - 1) + k, tb), :]) for k in range(CONV_S)]

        dpn = dyl * gl_ref[...]
        dp = rp * (dpn - pn * _group_mean(dpn * pn, avg_ref[...]))
        du_ly = dp * hl * _gelu_grad(u_ly, th)
        g_s[...] = dp * ge
        a_ext[0:tb, :] = a
        an_s[...] = a_ext[pl.ds(1, tb), :]
        _scan_groups(hb, an_s, g_s, dh_s, dcar, reverse=True)
        a_ext[tb:tb + HALO, :] = a_ext[0:HALO, :]
        dh = dh_s[...]
        hl_ext[0:HALO, :] = hlh_ref[...] * has_prev
        hl_ext[HALO:HALO + tb, :] = hl
        da = dh * hl_ext[pl.ds(HALO - 1, tb), :]
        dmult = dh * (ig * xl)
        dig = dh * (mult * xl)
        dxl = dh * (mult * ig)
        dlog = da * a - jnp.where(first, 0.0, dmult * (a * a) / mult_raw)
        dr = dlog * ((-C_GATE) * sp)
        dsp = _colsum(dlog * ((-C_GATE) * r))
        dga = dr * r * (1.0 - r)
        dgx = dig * ig * (1.0 - ig)
        dgab = dga.astype(BF16)
        dgxb = dgx.astype(BF16)
        xlt_ref[...] = xl.T.astype(BF16)
        dgate_ref[:, 0:W] = dgab
        dgate_ref[:, W:2 * W] = dgxb
        dxl = dxl + _dot_nt(dgab, bda_ref[...]) + _dot_nt(dgxb, bdx_ref[...])
        dxl_ext[0:tb, :] = dxl
        du_lx = wl_ref[CONV_L - 1:CONV_L, :] * dxl
        for k in range(CONV_L - 1):
            du_lx = du_lx + wl_ref[k:k + 1, :] * dxl_ext[pl.ds(CONV_L - 1 - k, tb), :]
        dxl_ext[tb:tb + HALO, :] = dxl_ext[0:HALO, :]
        dwl = [_colsum(dxl * ulx_ext[pl.ds(HALO - (CONV_L - 1) + k, tb), :]) for k in range(CONV_L)]

        cat = lambda u, v: jnp.concatenate([u, v], axis=1)
        vec_ref[V_BL_BA:V_BL_BA + 1, :] += cat(_colsum(dxl), _colsum(dga))
        vec_ref[V_BX_SP:V_BX_SP + 1, :] += cat(_colsum(dgx), dsp)
        vec_ref[V_GL_GC:V_GL_GC + 1, :] += cat(_colsum(dyl * pn), _colsum(dyc * qn))
        vec_ref[V_WL01:V_WL01 + 1, :] += cat(dwl[0], dwl[1])
        vec_ref[V_WL23:V_WL23 + 1, :] += cat(dwl[2], dwl[3])
        vec_ref[V_WS01:V_WS01 + 1, :] += cat(dws[0], dws[1])
        vec_ref[V_WS2:V_WS2 + 1, 0:W] += dws[2]

        r1 = lax.rsqrt(_rowmean(x * x) + EPS)
        xn1 = x * r1
        hb_ref[...] = ((xn1 * g1_ref[...]) * (1.0 + scale1) + shift1).astype(BF16)
        dh_in = jnp.zeros((tb, D), F32)
        for k, du in enumerate((du_lx, du_ly, du_b, du_c, du_v)):
            dprojt_ref[k * W:(k + 1) * W, :] = du.T.astype(BF16)
            dh_in = dh_in + _dot(du.astype(BF16), win_v[k * W:(k + 1) * W, :])
        vec_ref[V_SHIFT1:V_SHIFT1 + 1, :] += _colsum(dh_in)
        vec_ref[V_SCALE1:V_SCALE1 + 1, :] += _colsum(dh_in * xn1 * g1_ref[...])
        vec_ref[V_G1:V_G1 + 1, :] += _colsum(dh_in * (1.0 + scale1) * xn1)
        dxn1 = dh_in * g1_ref[...] * (1.0 + scale1)
        gx_ref[...] = dx1 + r1 * (dxn1 - xn1 * _rowmean(dxn1 * xn1))

    rev = lambda cols: pl.BlockSpec((tb, cols), lambda i: (nb - 1 - i, 0))
    rev_t = lambda rows: pl.BlockSpec((rows, tb), lambda i: (0, nb - 1 - i))
    halo = lambda cols: pl.BlockSpec((HALO, cols), lambda i: (jnp.maximum((nb - 1 - i) * hb - 1, 0), 0))
    full = lambda a: pl.BlockSpec(a.shape, lambda i: (0,) * a.ndim)
    small = (modraw, adab, g1, g2, wl, bl, bda, bdx, ba, bxb, ap, ws, gl, gc, avg)
    ext = pltpu.VMEM((tb + HALO, W), F32)
    return pl.pallas_call(
        body,
        name="mixer_bwd",
        grid=(nb,),
        in_specs=[rev(D), rev(D), rev(D), rev(D), rev(D_IN), halo(D_IN), rev(W), halo(W), rev(N_KEPT * W)]
        + [full(a) for a in small] + [ANY],
        out_specs=[rev(D), pl.BlockSpec((V_ROWS, D), lambda i: (0, 0)), rev(D), rev_t(D_IN), rev(D), rev_t(D),
                   rev_t(W), rev(2 * W)],
        out_shape=[jax.ShapeDtypeStruct((t_len, D), F32), jax.ShapeDtypeStruct((V_ROWS, D), F32),
                   jax.ShapeDtypeStruct((t_len, D), BF16), jax.ShapeDtypeStruct((D_IN, t_len), BF16),
                   jax.ShapeDtypeStruct((t_len, D), BF16), jax.ShapeDtypeStruct((D, t_len), BF16),
                   jax.ShapeDtypeStruct((W, t_len), BF16), jax.ShapeDtypeStruct((t_len, 2 * W), BF16)],
        scratch_shapes=[pltpu.VMEM((D_IN, D), BF16), pltpu.VMEM((D, D), BF16), pltpu.SemaphoreType.DMA((2,)),
                        ext, ext, ext, ext, ext, ext, pltpu.VMEM((HALO, W), F32),
                        pltpu.VMEM((tb, W), F32), pltpu.VMEM((tb, W), F32), pltpu.VMEM((tb, W), F32)],
        compiler_params=pltpu.CompilerParams(dimension_semantics=("arbitrary",), vmem_limit_bytes=VMEM_LIMIT),
    )(x, mixed, dh2, dx2, proj, proj, hl, hl, kept, *small, wpack)


def _matmul(name, a, b, tm=512):
    m, k = a.shape
    n = b.shape[1]

    def body(a_ref, b_ref, o_ref):
        o_ref[...] = _dot(a_ref[...], b_ref[...])

    return pl.pallas_call(
        body,
        name=name,
        grid=(m // tm,),
        in_specs=[pl.BlockSpec((tm, k), lambda i: (i, 0)), pl.BlockSpec((k, n), lambda i: (0, 0))],
        out_specs=pl.BlockSpec((tm, n), lambda i: (i, 0)),
        out_shape=jax.ShapeDtypeStruct((m, n), F32),
        compiler_params=pltpu.CompilerParams(dimension_semantics=("arbitrary",), vmem_limit_bytes=VMEM_LIMIT),
    )(a, b)


def _gate_wgrad(xl_t, dgate, avg):
    hd = W // 8

    def body(a_ref, b_ref, avg_ref, o_ref):
        full = _dot(a_ref[...], b_ref[...])
        row = lax.broadcasted_iota(jnp.int32, (W, hd), 0)
        col = lax.broadcasted_iota(jnp.int32, (W, hd), 1)
        fold = ((row & (hd - 1)) == col).astype(BF16)
        keep = avg_ref[...] != 0
        for g in range(2):
            m = jnp.where(keep, full[:, g * W:(g + 1) * W], 0.0)
            hi = m.astype(BF16)
            rest = m - hi.astype(F32)
            mid = rest.astype(BF16)
            lo = (rest - mid.astype(F32)).astype(BF16)
            o_ref[g] = _dot(hi, fold) + _dot(mid, fold) + _dot(lo, fold)

    return pl.pallas_call(
        body,
        name="wgrad_gate",
        in_specs=[WHOLE] * 3,
        out_specs=WHOLE,
        out_shape=jax.ShapeDtypeStruct((2, W, hd), F32),
        compiler_params=pltpu.CompilerParams(vmem_limit_bytes=VMEM_LIMIT),
    )(xl_t, dgate, avg)


def _block_diag(w):
    n, m, _ = w.shape
    eye = jnp.eye(n, dtype=w.dtype)
    return (w[:, :, None, :] * eye[:, None, :, None]).reshape(n * m, n * m)


def _pad_rows(a, rows):
    return jnp.pad(a, ((0, rows - a.shape[0]),) + ((0, 0),) * (a.ndim - 1))


def _position():
    return lax.axis_index("x"), lax.axis_index("y"), lax.axis_index("c")


def _linear(pos):
    return 4 * pos[0] + 2 * pos[1] + pos[2]


def _flip(pos, k):
    return tuple(1 - p if k & bit else p for p, bit in zip(pos, (4, 2, 1)))


def _exchange_all(make_copy, make_arrival):
    copies = [make_copy(k) for k in range(1, N_DEV)]
    for cp in copies:
        cp.start()
    for k in range(1, N_DEV):
        make_arrival(k).wait_recv()
    for cp in copies:
        cp.wait_send()


def _mod_exchange_steps(cols):
    def steps(msg_ref, adaw_ref, gath_ref, mod_ref, sendbuf, send_a, recv_a, send_b, recv_b):
        me = _position()
        me_lin = _linear(me)
        m = msg_ref[...]
        row = lax.broadcasted_iota(jnp.int32, m.shape, 0)
        gath_ref[me_lin] = jnp.where(row == 0, m * _sigmoid(m), m)

        def gather_copy(k, src_lin):
            return pltpu.make_async_remote_copy(
                src_ref=gath_ref.at[src_lin], dst_ref=gath_ref.at[src_lin], send_sem=send_a.at[k - 1],
                recv_sem=recv_a.at[k - 1], device_id=_flip(me, k), device_id_type=MESH)

        _exchange_all(lambda k: gather_copy(k, me_lin), lambda k: gather_copy(k, _linear(_flip(me, k))))

        sc_all = gath_ref[:, 0, :]
        scb = jnp.concatenate([sc_all, jnp.zeros_like(sc_all)], axis=0).astype(BF16)
        prod = _dot(scb, adaw_ref[...].astype(BF16))
        for b in range(N_DEV):
            sendbuf[b] = jnp.broadcast_to(prod[b:b + 1, :], (HALO, cols))
        mod_ref[me_lin] = sendbuf[me_lin]

        def row_copy(k, dst_lin):
            peer = _flip(me, k)
            return pltpu.make_async_remote_copy(
                src_ref=sendbuf.at[_linear(peer)], dst_ref=mod_ref.at[dst_lin], send_sem=send_b.at[k - 1],
                recv_sem=recv_b.at[k - 1], device_id=peer, device_id_type=MESH)

        _exchange_all(lambda k: row_copy(k, me_lin), lambda k: row_copy(k, _linear(_flip(me, k))))

    return steps


def _gather_and_mod(msg, ada_w, block):
    rows, cols = block.shape
    mod_cols = ada_w.shape[1]
    mod_steps = _mod_exchange_steps(mod_cols)

    def body(msg_ref, adaw_ref, x_ref, gath_ref, mod_ref, out_ref, sendbuf, send_a, recv_a, send_b, recv_b,
             send_sems, recv_sems, sib_send_sems, sib_recv_sems, local_sem):
        x, y, c = _position()
        me, sibling = (x, y, c), (x, y, 1 - c)
        sends, forward, arrivals = _chip_gather_copies(x_ref, out_ref, send_sems, recv_sems)

        def to_sibling(j, block_of, src=None):
            dst = out_ref.at[_linear(block_of)]
            return pltpu.make_async_remote_copy(
                src_ref=dst if src is None else src, dst_ref=dst, send_sem=sib_send_sems.at[j],
                recv_sem=sib_recv_sems.at[j], device_id=sibling, device_id_type=MESH)

        mine = pltpu.make_async_copy(x_ref, out_ref.at[_linear(me)], local_sem)
        mine.start()
        passes = [to_sibling(0, me, src=x_ref)] + [to_sibling(1 + j, p) for j, p in enumerate(_route_peers(me))]
        passes[0].start()
        for cp in sends:
            cp.start()
        mod_steps(msg_ref, adaw_ref, gath_ref, mod_ref, sendbuf, send_a, recv_a, send_b, recv_b)
        arrivals[0].wait_recv()
        forward.start()
        passes[1].start()
        arrivals[1].wait_recv()
        passes[2].start()
        arrivals[2].wait_recv()
        passes[3].start()
        for j, p in enumerate((sibling,) + _route_peers(sibling)):
            to_sibling(j, p).wait_recv()
        for cp in sends + [forward] + passes:
            cp.wait_send()
        mine.wait()

    return pl.pallas_call(
        body,
        name="gather_and_mod",
        in_specs=[WHOLE, WHOLE, ANY],
        out_specs=[WHOLE, WHOLE, ANY],
        out_shape=[jax.ShapeDtypeStruct((N_DEV, HALO, D), F32), jax.ShapeDtypeStruct((N_DEV, HALO, mod_cols), F32),
                   jax.ShapeDtypeStruct((N_DEV, rows, cols), block.dtype)],
        scratch_shapes=[pltpu.VMEM((N_DEV, HALO, mod_cols), F32)] + [pltpu.SemaphoreType.DMA((N_DEV - 1,))] * 4
        + [pltpu.SemaphoreType.DMA((3,)), pltpu.SemaphoreType.DMA((3,)), pltpu.SemaphoreType.DMA((4,)),
           pltpu.SemaphoreType.DMA((4,)), pltpu.SemaphoreType.DMA],
        compiler_params=pltpu.CompilerParams(vmem_limit_bytes=VMEM_LIMIT),
    )(msg, ada_w, block)


HBM = pl.BlockSpec(memory_space=pltpu.HBM)
SEM = pl.BlockSpec(memory_space=pltpu.SEMAPHORE)
EFFECT = pltpu.SideEffectType.DATAFLOW_SIDE_EFFECTING


def _stage_copies(stage):
    return {"chips": (_chip_scatter_copies, len(CHIP_FLIPS), len(CHIP_FLIPS)), "sibling": (_sibling_copies, 4, 4)}[stage]


def _chips_start(which, chip_sums, stage="chips"):
    n = len(chip_sums)
    make_copies, per_array, slots = _stage_copies(stage)
    n_sems = per_array * n

    def body(*refs):
        srcs, dsts = refs[:n], refs[n:2 * n]
        send_sems, recv_sems = refs[2 * n:2 * n + 2]
        token = refs[-1]
        for cp in make_copies(srcs, dsts, send_sems, recv_sems):
            cp.start()
        token[...] = jnp.zeros(token.shape, token.dtype)

    landing = [jax.ShapeDtypeStruct((slots,) + s.shape[-2:], s.dtype) for s in chip_sums]
    outs = pl.pallas_call(
        body,
        name=which + "_" + stage + "_start",
        in_specs=[HBM] * (2 * n),
        out_specs=[SEM, SEM] + [HBM] * (2 * n) + [WHOLE],
        out_shape=[pltpu.SemaphoreType.DMA((n_sems,)), pltpu.SemaphoreType.DMA((n_sems,))]
        + [pltpu.HBM(s.shape, s.dtype) for s in chip_sums] + [pltpu.HBM(s.shape, s.dtype) for s in landing]
        + [jax.ShapeDtypeStruct((HALO, 128), F32)],
        input_output_aliases={i: 2 + i for i in range(2 * n)},
        compiler_params=pltpu.CompilerParams(has_side_effects=EFFECT),
    )(*[pltpu.with_memory_space_constraint(s, pltpu.HBM) for s in chip_sums],
      *[pltpu.with_memory_space_constraint(lax.empty(s.shape, s.dtype), pltpu.HBM) for s in landing])
    return outs[0], outs[1], outs[2:2 + n], outs[2 + n:2 + 2 * n], outs[-1]


def _chips_wait(which, send_sems, recv_sems, srcs, landed, after, stage="chips"):
    n = len(srcs)
    make_copies = _stage_copies(stage)[0]

    def body(*refs):
        src_refs, dst_refs = refs[:n], refs[n:2 * n]
        sends, recvs = refs[2 * n:2 * n + 2]
        copies = make_copies(src_refs, dst_refs, sends, recvs)
        for cp in copies:
            cp.wait_send()
        for cp in copies:
            cp.wait_recv()

    outs = pl.pallas_call(
        body,
        name=which + "_" + stage + "_wait",
        in_specs=[HBM] * (2 * n) + [SEM, SEM, ANY],
        out_specs=[HBM] * (2 * n),
        out_shape=[pltpu.HBM(s.shape, s.dtype) for s in list(srcs) + list(landed)],
        input_output_aliases={i: i for i in range(2 * n)},
        compiler_params=pltpu.CompilerParams(has_side_effects=EFFECT),
    )(*srcs, *landed, send_sems, recv_sems, after)
    return list(outs[:n]), list(outs[n:])


def _sibling_copies(srcs, dsts, send_sems, recv_sems):
    x, y, c = _position()
    copies = []
    for a, (src, dst) in enumerate(zip(srcs, dsts)):
        for k in range(4):
            copies.append(pltpu.make_async_remote_copy(
                src_ref=src.at[k, 1 - c] if len(src.shape) == 4 else src.at[k], dst_ref=dst.at[k],
                send_sem=send_sems.at[4 * a + k],
                recv_sem=recv_sems.at[4 * a + k], device_id=(x, y, 1 - c), device_id_type=MESH))
    return copies


def _row_block(rows):
    return min(rows, 512)


def _pair_sum(pos, mine, recv):
    _, cores, rows, cols = mine.shape
    rb = _row_block(rows)

    def body(pos_ref, mine_ref, recv_ref, out_ref):
        out_ref[0] = (mine_ref[0, 0] + recv_ref[0]).astype(BF16)

    other = lambda k, pos: jnp.bitwise_xor(pos[1], k + 1)
    core = lambda pos: pos[0] * (cores - 1)
    return pl.pallas_call(
        body,
        name="grad_pair_sum",
        grid_spec=pltpu.PrefetchScalarGridSpec(
            num_scalar_prefetch=1, grid=(3, rows // rb),
            in_specs=[pl.BlockSpec((1, 1, rb, cols), lambda k, r, pos: (other(k, pos), core(pos), r, 0)),
                      pl.BlockSpec((1, rb, cols), lambda k, r, pos: (other(k, pos), r, 0))],
            out_specs=pl.BlockSpec((1, rb, cols), lambda k, r, pos: (other(k, pos), r, 0))),
        out_shape=jax.ShapeDtypeStruct((4, rows, cols), BF16),
        compiler_params=pltpu.CompilerParams(dimension_semantics=("arbitrary", "arbitrary")),
    )(pos, mine, recv)


def _final_sum(pos, mine, recv, chips):
    _, cores, rows, cols = mine.shape
    rb = _row_block(rows)

    def body(pos_ref, mine_ref, recv_ref, chips_ref, out_ref):
        g = mine_ref[0, 0] + recv_ref[0]
        for j in range(3):
            g = g + chips_ref[j].astype(F32)
        out_ref[...] = g

    return pl.pallas_call(
        body,
        name="grad_final_sum",
        grid_spec=pltpu.PrefetchScalarGridSpec(
            num_scalar_prefetch=1, grid=(rows // rb,),
            in_specs=[pl.BlockSpec((1, 1, rb, cols), lambda r, pos: (pos[1], pos[0] * (cores - 1), r, 0)),
                      pl.BlockSpec((1, rb, cols), lambda r, pos: (pos[1], r, 0)),
                      pl.BlockSpec((3, rb, cols), lambda r, pos: (0, r, 0))],
            out_specs=pl.BlockSpec((rb, cols), lambda r, pos: (r, 0))),
        out_shape=jax.ShapeDtypeStruct((rows, cols), F32),
        compiler_params=pltpu.CompilerParams(dimension_semantics=("arbitrary",)),
    )(pos, mine, recv, chips)


LOSS_ROW = V_ROWS + 8
GB_BASE = LOSS_ROW + 8


def _route_mod_grad_steps(cols):
    def steps(gmod_ref, sct_ref, gadaw_ref, sendbuf, grecv, send_a, recv_a):
        me = _position()
        me_lin = _linear(me)
        gm = gmod_ref[...]
        for b in range(N_DEV):
            sendbuf[b] = jnp.broadcast_to(gm[b:b + 1, :], (HALO, cols))
        grecv[me_lin] = sendbuf[me_lin]

        def row_copy(k, dst_lin):
            peer = _flip(me, k)
            return pltpu.make_async_remote_copy(
                src_ref=sendbuf.at[_linear(peer)], dst_ref=grecv.at[dst_lin], send_sem=send_a.at[k - 1],
                recv_sem=recv_a.at[k - 1], device_id=peer, device_id_type=MESH)

        _exchange_all(lambda k: row_copy(k, me_lin), lambda k: row_copy(k, _linear(_flip(me, k))))
        g_all = grecv[:, 0, :]
        g_pad = jnp.concatenate([g_all, jnp.zeros((sct_ref.shape[1] - N_DEV, cols), F32)], axis=0).astype(BF16)
        gadaw_ref[...] = _dot(sct_ref[...], g_pad)
        return _colsum(g_all)

    return steps


def _small_grad_exchange(gmod8, sc_t, msg_vec, msg_gate, after):
    cols = gmod8.shape[1]
    vec_rows = GB_BASE + N_DEV
    route_steps = _route_mod_grad_steps(cols)

    def body(gmod_ref, sct_ref, vec_ref, gate_ref, after_ref, gadaw_ref, sumv_ref, sumg_ref,
             sendbuf, grecv, send_a, recv_a, myv, myg, sibv, sibg, chipv, chipg, sib_send, sib_recv, peer_send, peer_recv):
        x, y, c = me = _position()
        my_chip = 2 * x + y

        def swap(a, src, dst):
            return pltpu.make_async_remote_copy(
                src_ref=src, dst_ref=dst, send_sem=sib_send.at[a], recv_sem=sib_recv.at[a], device_id=(x, y, 1 - c),
                device_id_type=MESH)

        def chip_copy(a, buf, j, k, slot_chip):
            peer = _flip(me, k)
            return pltpu.make_async_remote_copy(
                src_ref=buf.at[slot_chip], dst_ref=buf.at[slot_chip], send_sem=peer_send.at[3 * a + j],
                recv_sem=peer_recv.at[3 * a + j], device_id=peer, device_id_type=MESH)

        def chip_stage(a, mine, theirs, buf):
            buf[my_chip] = (mine[...] + theirs[...]).astype(buf.dtype)
            sends = [chip_copy(a, buf, j, k, my_chip) for j, k in enumerate(CHIP_FLIPS)]
            for cp in sends:
                cp.start()
            return sends

        myg[...] = gate_ref[...]
        swap_g = swap(1, myg, sibg)
        swap_g.start()
        gb = route_steps(gmod_ref, sct_ref, gadaw_ref, sendbuf, grecv, send_a, recv_a)
        swap_g.wait_recv()
        sends = chip_stage(1, myg, sibg, chipg)

        myv[0:GB_BASE, :] = vec_ref[...]
        slot = lax.broadcasted_iota(jnp.int32, (N_DEV, D), 0) == _linear(me)
        gb_wide = jnp.concatenate([jnp.broadcast_to(gb, (N_DEV, cols)), jnp.zeros((N_DEV, D - cols), F32)], axis=1)
        myv[GB_BASE:vec_rows, :] = jnp.where(slot, gb_wide, 0.0)
        swap_v = swap(0, myv, sibv)
        swap_v.start()
        swap_v.wait_recv()
        sends += chip_stage(0, myv, sibv, chipv)

        for a, buf in enumerate((chipv, chipg)):
            for j, k in enumerate(CHIP_FLIPS):
                peer = _flip(me, k)
                chip_copy(a, buf, j, k, 2 * peer[0] + peer[1]).wait_recv()
        sumv_ref[...] = ((chipv[0] + chipv[1]) + chipv[2]) + chipv[3]
        gate_sum = lambda k: chipg[k].astype(F32)
        sumg_ref[...] = ((gate_sum(0) + gate_sum(1)) + gate_sum(2)) + gate_sum(3)
        for cp in [swap_g, swap_v] + sends:
            cp.wait_send()

    vshape, gshape = (vec_rows, D), msg_gate.shape
    return pl.pallas_call(
        body,
        name="small_grad_exchange",
        in_specs=[WHOLE] * 5,
        out_specs=[WHOLE] * 3,
        out_shape=[jax.ShapeDtypeStruct((D, cols), F32), jax.ShapeDtypeStruct(vshape, F32),
                   jax.ShapeDtypeStruct(gshape, F32)],
        scratch_shapes=[pltpu.VMEM((N_DEV, HALO, cols), F32), pltpu.VMEM((N_DEV, HALO, cols), F32),
                        pltpu.SemaphoreType.DMA((N_DEV - 1,)), pltpu.SemaphoreType.DMA((N_DEV - 1,)),
                        pltpu.VMEM(vshape, F32), pltpu.VMEM(gshape, F32), pltpu.VMEM(vshape, F32),
                        pltpu.VMEM(gshape, F32), pltpu.VMEM((4,) + vshape, F32), pltpu.VMEM((4,) + gshape, BF16),
                        pltpu.SemaphoreType.DMA((2,)), pltpu.SemaphoreType.DMA((2,)),
                        pltpu.SemaphoreType.DMA((2 * len(CHIP_FLIPS),)), pltpu.SemaphoreType.DMA((2 * len(CHIP_FLIPS),))],
        compiler_params=pltpu.CompilerParams(vmem_limit_bytes=VMEM_LIMIT),
    )(gmod8, sc_t, msg_vec, msg_gate, after)


def _adamw_math(w, g, m, v):
    m = ADAM_B1 * m + (1.0 - ADAM_B1) * g
    v = ADAM_B2 * v + (1.0 - ADAM_B2) * (g * g)
    m_hat = m / (1.0 - ADAM_B1 ** ADAM_STEP)
    v_hat = v / (1.0 - ADAM_B2 ** ADAM_STEP)
    delta = -ADAM_LR * (m_hat / (jnp.sqrt(v_hat) + ADAM_EPS) + ADAM_WD * w)
    return delta, m, v


def _adamw(name, w, g, m, v):
    rows, cols = w.shape
    rb = 256 if rows % 256 == 0 else rows

    def body(w_ref, g_ref, m_ref, v_ref, d_ref, mo_ref, vo_ref):
        d_ref[...], mo_ref[...], vo_ref[...] = _adamw_math(w_ref[...], g_ref[...], m_ref[...], v_ref[...])

    spec = pl.BlockSpec((rb, cols), lambda r: (r, 0))
    return pl.pallas_call(
        body,
        name="adamw_" + name,
        grid=(rows // rb,),
        in_specs=[spec] * 4,
        out_specs=[spec] * 3,
        out_shape=[jax.ShapeDtypeStruct((rows, cols), F32)] * 3,
        compiler_params=pltpu.CompilerParams(dimension_semantics=("arbitrary",)),
    )(w, g, m, v)


def _update(pos, sum_jobs, plain_jobs, after):
    rb = 256
    jobs = [("sum", j) for j in sum_jobs] + [("plain", j) for j in plain_jobs]
    offs, total = [], 0
    for _, j in jobs:
        offs.append(total)
        total += j[-1].shape[0] // rb
    n_in = sum(len(j) for _, j in jobs)

    def body(pos_ref, *refs):
        ins, outs = refs[:n_in], refs[n_in + 1:]
        s = pl.program_id(0)
        i_in = i_out = 0
        for (kind, j), off in zip(jobs, offs):
            steps = j[-1].shape[0] // rb
            j_in = ins[i_in:i_in + len(j)]
            i_in += len(j)
            j_out = outs[i_out:i_out + (4 if kind == "sum" else 3)]
            i_out += len(j_out)

            @pl.when((s >= off) & (s < off + steps))
            def _(kind=kind, j_in=j_in, j_out=j_out):
                if kind == "sum":
                    mine_ref, recv_ref, chips_ref, w_ref, m_ref, v_ref = j_in
                    g = mine_ref[0, 0] + recv_ref[0]
                    for q in range(len(CHIP_FLIPS)):
                        g = g + chips_ref[q].astype(F32)
                    j_out[0][...] = g
                    rest = j_out[1:]
                else:
                    g_ref, w_ref, m_ref, v_ref = j_in
                    g = g_ref[...]
                    rest = j_out
                rest[0][...], rest[1][...], rest[2][...] = _adamw_math(w_ref[...], g, m_ref[...], v_ref[...])

    in_specs, out_specs, out_shape, args = [], [], [], []
    for (kind, j), off in zip(jobs, offs):
        rows, cols = j[-1].shape
        steps = rows // rb
        blk = lambda s, off=off, steps=steps: jnp.clip(s - off, 0, steps - 1)
        flat = pl.BlockSpec((rb, cols), lambda s, pos, blk=blk: (blk(s), 0))
        if kind == "sum":
            in_specs += [pl.BlockSpec((1, 1, rb, cols), lambda s, pos, blk=blk: (pos[1], 0, blk(s), 0)),
                         pl.BlockSpec((1, rb, cols), lambda s, pos, blk=blk: (pos[1], blk(s), 0)),
                         pl.BlockSpec((len(CHIP_FLIPS), rb, cols), lambda s, pos, blk=blk: (0, blk(s), 0))]
            in_specs += [flat] * 3
        else:
            in_specs += [flat] * 4
        n_res = 4 if kind == "sum" else 3
        out_specs += [flat] * n_res
        out_shape += [jax.ShapeDtypeStruct((rows, cols), F32)] * n_res
        args += list(j)
    in_specs += [pl.BlockSpec(after.shape, lambda s, pos: (0,) * after.ndim)]
    outs = pl.pallas_call(
        body,
        name="update",
        grid_spec=pltpu.PrefetchScalarGridSpec(
            num_scalar_prefetch=1, grid=(total,), in_specs=in_specs, out_specs=out_specs),
        out_shape=out_shape,
        compiler_params=pltpu.CompilerParams(dimension_semantics=("arbitrary",), vmem_limit_bytes=VMEM_LIMIT),
    )(pos, *args, after)
    sums = [tuple(outs[4 * i:4 * i + 4]) for i in range(len(sum_jobs))]
    base = 4 * len(sum_jobs)
    plains = [tuple(outs[base + 3 * i:base + 3 * i + 3]) for i in range(len(plain_jobs))]
    return sums, plains


def _adamw_small(ws, gs, ms, vs, sigmoid_scaled):
    n = len(ws)

    def body(*refs):
        w_refs, g_refs, m_refs, v_refs = (refs[i * n:(i + 1) * n] for i in range(4))
        outs = refs[4 * n:]
        for i in range(n):
            w = w_refs[i][...]
            g = g_refs[i][...]
            if sigmoid_scaled[i]:
                g = g * _sigmoid(w)
            delta, m, v = _adamw_math(w, g, m_refs[i][...], v_refs[i][...])
            outs[4 * i][...] = g
            outs[4 * i + 1][...] = delta
            outs[4 * i + 2][...] = m
            outs[4 * i + 3][...] = v

    shapes = [jax.ShapeDtypeStruct(w.shape, F32) for w in ws for _ in range(4)]
    outs = pl.pallas_call(
        body,
        name="adamw_small",
        in_specs=[WHOLE] * (4 * n),
        out_specs=[WHOLE] * (4 * n),
        out_shape=shapes,
    )(*ws, *gs, *ms, *vs)
    return [outs[4 * i:4 * i + 4] for i in range(n)]


_WEIGHT_NAMES = ("ada_w", "ada_b", "norm1_g", "w_in", "lru_conv_w", "lru_conv_b", "gate_a_w", "gate_a_b", "gate_x_w",
                 "gate_x_b", "a_param", "short_conv_w", "lru_out_g", "conv_out_g", "w_out", "norm2_g", "w_mlp1",
                 "w_mlp2", "final_g")


def kernel(x, c, ada_w, ada_b, norm1_g, w_in, lru_conv_w, lru_conv_b, gate_a_w, gate_a_b, gate_x_w, gate_x_b, a_param, short_conv_w, lru_out_g, conv_out_g, w_out, norm2_g, w_mlp1, w_mlp2, final_g, loss_target, m_ada_w, m_ada_b, m_norm1_g, m_w_in, m_lru_conv_w, m_lru_conv_b, m_gate_a_w, m_gate_a_b, m_gate_x_w, m_gate_x_b, m_a_param, m_short_conv_w, m_lru_out_g, m_conv_out_g, m_w_out, m_norm2_g, m_w_mlp1, m_w_mlp2, m_final_g, v_ada_w, v_ada_b, v_norm1_g, v_w_in, v_lru_conv_w, v_lru_conv_b, v_gate_a_w, v_gate_a_b, v_gate_x_w, v_gate_x_b, v_a_param, v_short_conv_w, v_lru_out_g, v_conv_out_g, v_w_out, v_norm2_g, v_w_mlp1, v_w_mlp2, v_final_g):
    given = dict(locals())
    weights = {n: given[n] for n in _WEIGHT_NAMES}
    xi, yi, ci = _position()
    me_lin = _linear((xi, yi, ci))
    hd = W // N_DEV

    mixer_block = jnp.concatenate([w_out[0], w_in[0].T], axis=0).astype(BF16)
    mlp_block = jnp.concatenate([w_mlp1[0].T, w_mlp2[0]], axis=0).astype(BF16)

    msg = (jnp.pad(c, ((0, HALO - 1), (0, 0)))
           + jnp.pad(lru_conv_w[0], ((1, HALO - 1 - CONV_L), (0, D - hd)))
           + jnp.pad(short_conv_w[0], ((1 + CONV_L, 0), (0, D - hd))))
    gath, mod_all, wmix = _gather_and_mod(msg, ada_w[0], mixer_block)
    sc_all = gath[:, 0, :]
    wl = jnp.transpose(gath[:, 1:1 + CONV_L, :hd], (1, 0, 2)).reshape(CONV_L, W)
    ws = jnp.transpose(gath[:, 1 + CONV_L:HALO, :hd], (1, 0, 2)).reshape(CONV_S, W)
    modraw = _pad_rows(mod_all[:, 0, :].reshape(6, D), HALO)
    adab = _pad_rows(ada_b.reshape(6, D), HALO)

    x2d, tgt = x[0], loss_target[0]
    gf = final_g.reshape(1, D)
    bda = _block_diag(gate_a_w[0]).astype(BF16)
    bdx = _block_diag(gate_x_w[0]).astype(BF16)
    avg = _block_diag(jnp.full((8, W // 8, W // 8), 8.0 / W, F32)).astype(BF16)
    wl8 = _pad_rows(wl, HALO)
    ws8 = _pad_rows(ws, HALO)
    mixer_small = (wl8, lru_conv_b, bda, bdx, gate_a_b, gate_x_b, a_param, ws8, lru_out_g, conv_out_g, avg)
    proj, hl, mixed, kept, wmlp = _mixer_fwd(x2d, modraw, adab, norm1_g, *mixer_small, wmix, mlp_block)
    h2t, f, dx2, dz, vec2, loss8 = _mlp_fwd(x2d, mixed, tgt, modraw, adab, norm2_g, gf, wmlp)
    pos = jnp.stack([ci, 2 * xi + yi]).astype(jnp.int32)
    by_dest = lambda g: g.reshape((4, 2, -1) + g.shape[-1:])
    dh2_first, *for_sibling = _mlp_bwd_half(pos, h2t, f, dz, wmlp)
    sib_send, sib_recv, sib_thru, sib_land, token = _chips_start("mlp", for_sibling, stage="sibling")
    dh2, dw1, dw2 = _mlp_bwd_half(pos, h2t, f, dz, wmlp, prior=(dh2_first, token))
    done = dh2[0:HALO, 0:128] + dw1[0, 0:HALO, 0:128] + dw2[0, 0:HALO, 0:128]
    _, mlp_sib = _chips_wait("mlp", sib_send, sib_recv, sib_thru, sib_land, done, stage="sibling")
    mlp_parts = [dw1[:, None], dw2[:, None]]
    mlp_sums = [_pair_sum(pos, p, r) for p, r in zip(mlp_parts, mlp_sib)]
    mlp_send, mlp_recv, mlp_thru, mlp_land, token = _chips_start("mlp", mlp_sums)
    modraw_after = modraw + jnp.tile(token, (1, D // token.shape[1]))
    gx, vec, hb, dproj_t, dmixed, ycat_t, xl_t, dgate = _mixer_bwd(
        x2d, mixed, dh2, dx2, proj, hl, kept, modraw_after, adab, norm1_g, norm2_g, *mixer_small, wmix)
    dwint = _matmul("wgrad_in", dproj_t, hb)
    dwout = _matmul("wgrad_out", ycat_t, dmixed)
    gate_blocks = _gate_wgrad(xl_t, dgate, avg)
    msg_gate = gate_blocks.reshape(W, 128)
    done = dwint[0:HALO, 0:128] + dwout[0:HALO, 0:128] + gate_blocks[0, 0:HALO, :].sum() + gx[0:HALO, 0:128]
    _, mlp_chips = _chips_wait("mlp", mlp_send, mlp_recv, mlp_thru, mlp_land, done)
    mix_parts = [by_dest(dwout), by_dest(dwint)]
    gmod8 = (jnp.pad(vec[0:5], ((0, 1), (0, 0))) + jnp.pad(vec2[0:1], ((5, 0), (0, 0)))).reshape(N_DEV, 6 * D // N_DEV)
    sc_t = jnp.pad(sc_all.T, ((0, 0), (0, 128 - N_DEV))).astype(BF16)
    loss_rows = jnp.pad(loss8[0:1], ((0, HALO - 1), (0, D - loss8.shape[1])))
    msg_vec = jnp.concatenate([vec, vec2, loss_rows], axis=0)
    sib_send, sib_recv, sib_thru, sib_land, token = _chips_start("mixer", mix_parts, stage="sibling")
    g_adaw, sum_vec, sum_gate = _small_grad_exchange(gmod8, sc_t, msg_vec, msg_gate, token)
    mix_parts, mix_sib = _chips_wait("mixer", sib_send, sib_recv, sib_thru, sib_land, sum_vec[0:HALO, 0:128],
                                     stage="sibling")
    mix_sums = [_pair_sum(pos, p, r) for p, r in zip(mix_parts, mix_sib)]
    state = lambda n: (weights[n][0], given["m_" + n][0], given["v_" + n][0])
    mlp_jobs = [(p, r, q, *state(n)) for p, r, q, n in zip(mlp_parts, mlp_sib, mlp_chips, ("w_mlp1", "w_mlp2"))]
    mix_send, mix_recv, mix_thru, mix_land, token = _chips_start("mixer", mix_sums)
    mlp_done, (adaw_done,) = _update(pos, mlp_jobs, [(g_adaw, *state("ada_w"))], token)
    loss = sum_vec[LOSS_ROW, 0]
    sum_gate = sum_gate.reshape(2, W, W // 8)
    lo, hi = slice(0, W), slice(W, 2 * W)
    wl_full = sum_vec[V_WL01:V_WL23 + 1].reshape(CONV_L, W)
    ws_full = sum_vec[V_WS01:V_WS2 + 1].reshape(CONV_S + 1, W)[:CONV_S]
    row = lambda r, cols: sum_vec[r:r + 1, cols]
    small_grads = {
        "ada_b": sum_vec[GB_BASE:GB_BASE + N_DEV, :6 * D // N_DEV].reshape(1, 6 * D),
        "norm1_g": row(V_G1, slice(0, D)),
        "lru_conv_w": lax.dynamic_slice(wl_full, (0, me_lin * hd), (CONV_L, hd)),
        "lru_conv_b": row(V_BL_BA, lo),
        "gate_a_w": sum_gate[0],
        "gate_a_b": row(V_BL_BA, hi),
        "gate_x_w": sum_gate[1],
        "gate_x_b": row(V_BX_SP, lo),
        "a_param": row(V_BX_SP, hi),
        "short_conv_w": lax.dynamic_slice(ws_full, (0, me_lin * hd), (CONV_S, hd)),
        "lru_out_g": row(V_GL_GC, lo),
        "conv_out_g": row(V_GL_GC, hi),
        "norm2_g": row(V_G2, slice(0, D)),
        "final_g": sum_vec[V_ROWS + 1:V_ROWS + 2, :],
    }
    names = list(small_grads)
    as2d = lambda a, n: a.reshape(small_grads[n].shape)
    small = _adamw_small([as2d(weights[n], n) for n in names], [small_grads[n] for n in names],
                         [as2d(given["m_" + n], n) for n in names], [as2d(given["v_" + n], n) for n in names],
                         [n == "a_param" for n in names])
    result = {n: tuple(o.reshape(weights[n].shape) for o in outs) for n, outs in zip(names, small)}

    done = (small[0][1][:, 0:128] + mlp_done[0][2][0:HALO, 0:128] + mlp_done[1][2][0:HALO, 0:128]
            + adaw_done[1][0:HALO, 0:128])
    _, mix_chips = _chips_wait("mixer", mix_send, mix_recv, mix_thru, mix_land, done)
    g_wout, g_wint = (_final_sum(pos, p, r, q) for p, r, q in zip(mix_parts, mix_sib, mix_chips))
    for n, g in (("w_in", g_wint.T), ("w_out", g_wout)):
        w, m, v = state(n)
        result[n] = (g[None],) + tuple(o[None] for o in _adamw(n, w, g, m, v))
    result["w_mlp1"], result["w_mlp2"] = (tuple(o[None] for o in done) for done in mlp_done)
    result["ada_w"] = (g_adaw[None],) + tuple(o[None] for o in adaw_done)

    return (loss, gx[None], *[result[n][0] for n in _WEIGHT_NAMES], *[result[n][1] for n in _WEIGHT_NAMES],
            *[result[n][2] for n in _WEIGHT_NAMES], *[result[n][3] for n in _WEIGHT_NAMES])
```

```python
import jax
import jax.numpy as jnp
from jax import lax
from jax.experimental import pallas as pl
from jax.experimental.pallas import tpu as pltpu

F32 = jnp.float32
BF16 = jnp.bfloat16
MESH = pl.DeviceIdType.MESH

N_DEV = 8
D = 1024
W = 512
D_IN = 5 * W
D_FF = 4096
FF_BLK = D_FF // N_DEV
EPS = 1e-6
C_GATE = 8.0
CONV_L = 4
CONV_S = 3
HALO = 8

ROWS_W1T, ROWS_W2, ROWS_WOUT, ROWS_WIN = FF_BLK, FF_BLK, D // N_DEV, D_IN // N_DEV
OFF_WOUT = 0
OFF_WIN = OFF_WOUT + ROWS_WOUT
MIX_ROWS = OFF_WIN + ROWS_WIN
OFF_W1T = 0
OFF_W2 = OFF_W1T + ROWS_W1T
MLP_ROWS = OFF_W2 + ROWS_W2
CHIP_FLIPS = (4, 2, 6)
N_KEPT = 6

ADAM_LR = 0.001
ADAM_B1 = 0.9
ADAM_B2 = 0.999
ADAM_EPS = 1e-08
ADAM_WD = 0.01
ADAM_STEP = 10

VMEM_LIMIT = 56 * 1024 * 1024

TB_MIX = 256
TB_MIXB = 256
TB_MLP = 256
TB_MLPB = 512

ANY = pl.BlockSpec(memory_space=pl.ANY)
WHOLE = pl.BlockSpec(memory_space=pltpu.VMEM)


def _dot(a, b):
    return jnp.dot(a, b, preferred_element_type=F32)


def _dot_nt(a, b):
    return lax.dot_general(a, b, (((1,), (1,)), ((), ())), preferred_element_type=F32)


def _dot_tn(a, b):
    return lax.dot_general(a, b, (((0,), (0,)), ((), ())), preferred_element_type=F32)


def _sigmoid(v):
    return 1.0 / (1.0 + jnp.exp(-v))


def _softplus(v):
    t = jnp.exp(-jnp.abs(v))
    small = t * (1.0 - t * (0.5 - t * (1.0 / 3.0)))
    return jnp.maximum(v, 0.0) + jnp.where(t < 1e-2, small, jnp.log(1.0 + t))


def _one_minus_sq(a, log_a):
    return -jnp.tanh(log_a) * (a * a + 1.0)


_GELU_K = 0.7978845608028654
_GELU_C = 0.044715


def _gelu(u):
    th = jnp.tanh(_GELU_K * (u + _GELU_C * u * u * u))
    return 0.5 * u * (1.0 + th), th


def _gelu_grad(u, th):
    return 0.5 * (1.0 + th) + 0.5 * u * (1.0 - th * th) * _GELU_K * (1.0 + 3.0 * _GELU_C * u * u)


def _group_mean(v, avg):
    hi = v.astype(BF16)
    lo = (v - hi.astype(F32)).astype(BF16)
    return _dot(hi, avg) + _dot(lo, avg)


def _colsum(v):
    return jnp.sum(v, axis=0, keepdims=True)


def _rowmean(v):
    return jnp.mean(v, axis=-1, keepdims=True)


def _load_packed(wpack_hbm, off, rows, dst, sem):
    copies = [
        pltpu.make_async_copy(wpack_hbm.at[d, pl.ds(off, rows), :], dst.at[pl.ds(d * rows, rows), :], sem)
        for d in range(N_DEV)
    ]
    for cp in copies:
        cp.start()
    return copies


def _scan_groups(n_groups, a_ref, b_ref, out_ref, carry_ref, reverse):
    row = lax.broadcasted_iota(jnp.int32, (HALO, W), 0)

    def step(k, carry):
        g = (n_groups - 1 - k) if reverse else k
        rows = pl.ds(pl.multiple_of(g * HALO, HALO), HALO)
        a = a_ref[rows, :]
        b = b_ref[rows, :]
        for s in (1, 2, 4):
            if reverse:
                keep = row < HALO - s
                sh = HALO - s
            else:
                keep = row >= s
                sh = s
            a_sh = pltpu.roll(a, sh, axis=0)
            b_sh = pltpu.roll(b, sh, axis=0)
            b = jnp.where(keep, a * b_sh + b, b)
            a = jnp.where(keep, a * a_sh, a)
        h = b + a * carry
        out_ref[rows, :] = h
        edge = h[0:1, :] if reverse else h[HALO - 1:HALO, :]
        return jnp.broadcast_to(edge, (HALO, W))

    carry_ref[...] = lax.fori_loop(0, n_groups, step, carry_ref[...])


def _route_peers(me):
    x, y, c = me
    first = ((x + 1 - c) % 2, (y + c) % 2, c)
    second = ((x + c) % 2, (y + 1 - c) % 2, c)
    return first, second, (1 - x, 1 - y, c)


def _chip_gather_copies(block_hbm, out_hbm, send_sems, recv_sems):
    me = _position()
    first, second, diag = _route_peers(me)

    def copy(j, src, slot_of, to):
        return pltpu.make_async_remote_copy(
            src_ref=src, dst_ref=out_hbm.at[_linear(slot_of)], send_sem=send_sems.at[j], recv_sem=recv_sems.at[j],
            device_id=to, device_id_type=MESH)

    own_sends = [copy(0, block_hbm, me, first), copy(1, block_hbm, me, second)]
    forward = copy(2, out_hbm.at[_linear(first)], first, second)
    arrivals = [copy(0, block_hbm, first, first), copy(1, block_hbm, second, second), copy(2, block_hbm, diag, second)]
    return own_sends, forward, arrivals


def _mixer_fwd(x, modraw, adab, g1, wl, bl, bda, bdx, ba, bxb, ap, ws, gl, gc, avg, wpack, mlp_block):
    t_len = x.shape[0]
    tb = TB_MIX
    nb = t_len // tb

    def body(x_ref, modraw_ref, adab_ref, g1_ref, wl_ref, bl_ref, bda_ref, bdx_ref, ba_ref, bxb_ref, ap_ref,
             ws_ref, gl_ref, gc_ref, avg_ref, wpack_hbm, block_hbm, proj_ref, hl_ref, mixed_ref, kept_ref, wmlp_hbm,
             win_v, wout_v, sem, ulx_ext, cv_ext, hcar, a_s, b_s, send_sems, recv_sems, sib_send_sems, sib_recv_sems,
             local_sem):
        i = pl.program_id(0)
        x_pos, y_pos, c_pos = me = _position()
        sibling = (x_pos, y_pos, 1 - c_pos)
        own = pltpu.make_async_copy(block_hbm, wmlp_hbm.at[_linear(me)], local_sem)
        sends, forward, arrivals = _chip_gather_copies(block_hbm, wmlp_hbm, send_sems, recv_sems)

        def to_sibling(j, block_of, src=None):
            dst = wmlp_hbm.at[_linear(block_of)]
            return pltpu.make_async_remote_copy(
                src_ref=dst if src is None else src, dst_ref=dst, send_sem=sib_send_sems.at[j],
                recv_sem=sib_recv_sems.at[j], device_id=sibling, device_id_type=MESH)

        passes = [to_sibling(0, me, src=block_hbm)] + [to_sibling(1 + j, p) for j, p in enumerate(_route_peers(me))]

        @pl.when(i == 0)
        def _():
            own.start()
            for cp in sends:
                cp.start()
            passes[0].start()

        @pl.when(i == nb - 1)
        def _():
            arrivals[0].wait_recv()
            forward.start()
            passes[1].start()

        @pl.when(i == 0)
        def _():
            cps = _load_packed(wpack_hbm, OFF_WIN, ROWS_WIN, win_v, sem.at[0])
            cps += _load_packed(wpack_hbm, OFF_WOUT, ROWS_WOUT, wout_v, sem.at[1])
            ulx_ext[0:HALO, :] = jnp.zeros((HALO, W), F32)
            cv_ext[0:HALO, :] = jnp.zeros((HALO, W), F32)
            hcar[...] = jnp.zeros((HALO, W), F32)
            for cp in cps:
                cp.wait()

        mod = modraw_ref[...] + adab_ref[...]
        shift1, scale1, gate1 = mod[0:1], mod[1:2], mod[2:3]
        x = x_ref[...]
        r1 = lax.rsqrt(_rowmean(x * x) + EPS)
        h = (x * r1 * g1_ref[...]) * (1.0 + scale1) + shift1
        proj = _dot_nt(h.astype(BF16), win_v[...])
        proj_ref[...] = proj
        u_lx, u_ly, u_b, u_c, u_v = (proj[:, k * W:(k + 1) * W] for k in range(5))

        ulx_ext[HALO:HALO + tb, :] = u_lx
        xl = bl_ref[...] + wl_ref[CONV_L - 1:CONV_L, :] * u_lx
        for k in range(CONV_L - 1):
            xl = xl + wl_ref[k:k + 1, :] * ulx_ext[pl.ds(HALO - (CONV_L - 1) + k, tb), :]
        ulx_ext[0:HALO, :] = ulx_ext[tb:tb + HALO, :]
        xlb = xl.astype(BF16)
        r = _sigmoid(_dot(xlb, bda_ref[...]) + ba_ref[...])
        ig = _sigmoid(_dot(xlb, bdx_ref[...]) + bxb_ref[...])
        log_a = (-C_GATE) * r * _softplus(ap_ref[...])
        a = jnp.exp(log_a)
        mult = jnp.sqrt(_one_minus_sq(a, log_a))
        grow = i * tb + lax.broadcasted_iota(jnp.int32, (tb, W), 0)
        mult = jnp.where(grow == 0, 1.0, mult)
        a_s[...] = a
        b_s[...] = mult * (ig * xl)
        _scan_groups(tb // HALO, a_s, b_s, hl_ref, hcar, reverse=False)
        hl = hl_ref[...]
        ge, _ = _gelu(u_ly)
        p = ge * hl
        rp = lax.rsqrt(_group_mean(p * p, avg_ref[...]) + EPS)
        y_lru = p * rp * gl_ref[...]

        cv = u_c * u_v
        cv_ext[HALO:HALO + tb, :] = cv
        cc = ws_ref[CONV_S - 1:CONV_S, :] * cv
        for k in range(CONV_S - 1):
            cc = cc + ws_ref[k:k + 1, :] * cv_ext[pl.ds(HALO - (CONV_S - 1) + k, tb), :]
        cv_ext[0:HALO, :] = cv_ext[tb:tb + HALO, :]
        q = u_b * cc
        rq = lax.rsqrt(_group_mean(q * q, avg_ref[...]) + EPS)
        y_conv = q * rq * gc_ref[...]
        for k, kept in enumerate((xl, r, ig, rp, rq, cc)):
            kept_ref[:, k * W:(k + 1) * W] = kept

        mixed_ref[...] = (_dot(y_lru.astype(BF16), wout_v[0:W, :]) + _dot(y_conv.astype(BF16), wout_v[W:2 * W, :]))

        @pl.when(i == nb - 1)
        def _():
            arrivals[1].wait_recv()
            passes[2].start()
            arrivals[2].wait_recv()
            passes[3].start()
            for j, p in enumerate((sibling,) + _route_peers(sibling)):
                to_sibling(j, p).wait_recv()
            for cp in sends + [forward] + passes:
                cp.wait_send()
            own.wait()

    tok = lambda cols: pl.BlockSpec((tb, cols), lambda i: (i, 0))
    full = lambda a: pl.BlockSpec(a.shape, lambda i: (0,) * a.ndim)
    small = (modraw, adab, g1, wl, bl, bda, bdx, ba, bxb, ap, ws, gl, gc, avg)
    n_chips = len(CHIP_FLIPS)
    return pl.pallas_call(
        body,
        name="mixer_fwd",
        grid=(nb,),
        in_specs=[tok(D)] + [full(a) for a in small] + [ANY, ANY],
        out_specs=[tok(D_IN), tok(W), tok(D), tok(N_KEPT * W), ANY],
        out_shape=[jax.ShapeDtypeStruct((t_len, D_IN), F32), jax.ShapeDtypeStruct((t_len, W), F32),
                   jax.ShapeDtypeStruct((t_len, D), F32), jax.ShapeDtypeStruct((t_len, N_KEPT * W), F32),
                   jax.ShapeDtypeStruct((N_DEV,) + mlp_block.shape, BF16)],
        scratch_shapes=[pltpu.VMEM((D_IN, D), BF16), pltpu.VMEM((D, D), BF16), pltpu.SemaphoreType.DMA((2,)),
                        pltpu.VMEM((tb + HALO, W), F32), pltpu.VMEM((tb + HALO, W), F32), pltpu.VMEM((HALO, W), F32),
                        pltpu.VMEM((tb, W), F32), pltpu.VMEM((tb, W), F32),
                        pltpu.SemaphoreType.DMA((n_chips,)), pltpu.SemaphoreType.DMA((n_chips,)),
                        pltpu.SemaphoreType.DMA((4,)), pltpu.SemaphoreType.DMA((4,)), pltpu.SemaphoreType.DMA],
        compiler_params=pltpu.CompilerParams(dimension_semantics=("arbitrary",), vmem_limit_bytes=VMEM_LIMIT),
    )(x, *small, wpack, mlp_block)


def _mlp_fwd(x, mixed, tgt, modraw, adab, g2, gf, wpack):
    t_len = x.shape[0]
    tb = TB_MLP
    nb = t_len // tb

    def body(x_ref, mixed_ref, tgt_ref, modraw_ref, adab_ref, g2_ref, gf_ref, wpack_hbm,
             h2t_ref, f_ref, dx2_ref, dz_ref, vec_ref, loss_ref, w1t_v, w2_v, sem):
        i = pl.program_id(0)

        @pl.when(i == 0)
        def _():
            cps = _load_packed(wpack_hbm, OFF_W1T, ROWS_W1T, w1t_v, sem.at[0])
            cps += _load_packed(wpack_hbm, OFF_W2, ROWS_W2, w2_v, sem.at[1])
            vec_ref[...] = jnp.zeros(vec_ref.shape, F32)
            loss_ref[...] = jnp.zeros(loss_ref.shape, F32)
            for cp in cps:
                cp.wait()

        mod = modraw_ref[...] + adab_ref[...]
        gate1, shift2, scale2, gate2 = mod[2:3], mod[3:4], mod[4:5], mod[5:6]
        x1 = x_ref[...] + gate1 * mixed_ref[...]
        r2 = lax.rsqrt(_rowmean(x1 * x1) + EPS)
        h2 = (x1 * r2 * g2_ref[...]) * (1.0 + scale2) + shift2
        h2b = h2.astype(BF16)
        h2t_ref[...] = h2.T.astype(BF16)
        z = jnp.zeros((tb, D), F32)
        for j in range(N_DEV):
            cols = slice(j * FF_BLK, (j + 1) * FF_BLK)
            fj = _dot_nt(h2b, w1t_v[cols, :])
            f_ref[:, cols] = fj
            rf = jnp.maximum(fj, 0.0)
            z = z + _dot((rf * rf).astype(BF16), w2_v[cols, :])
        x2 = x1 + gate2 * z
        r3 = lax.rsqrt(_rowmean(x2 * x2) + EPS)
        xn3 = x2 * r3
        diff = xn3 * gf_ref[...] - tgt_ref[...]
        sq = _colsum(diff * diff)
        loss_ref[...] += jnp.broadcast_to(jnp.sum(sq, axis=1, keepdims=True) * (0.5 / D), loss_ref.shape)
        dy = diff * (1.0 / D)
        dyn = dy * gf_ref[...]
        dx2 = r3 * (dyn - xn3 * _rowmean(dyn * xn3))
        dx2_ref[...] = dx2
        dz_ref[...] = (gate2 * dx2).astype(BF16)
        vec_ref[0:1, :] += _colsum(dx2 * z)
        vec_ref[1:2, :] += _colsum(dy * xn3)

    tok = lambda cols: pl.BlockSpec((tb, cols), lambda i: (i, 0))
    tok_t = pl.BlockSpec((D, tb), lambda i: (0, i))
    full = lambda a: pl.BlockSpec(a.shape, lambda i: (0,) * a.ndim)
    small = (modraw, adab, g2, gf)
    return pl.pallas_call(
        body,
        name="mlp_fwd",
        grid=(nb,),
        in_specs=[tok(D), tok(D), tok(D)] + [full(a) for a in small] + [ANY],
        out_specs=[tok_t, tok(D_FF), tok(D), tok(D), pl.BlockSpec((8, D), lambda i: (0, 0)),
                   pl.BlockSpec((8, 128), lambda i: (0, 0))],
        out_shape=[jax.ShapeDtypeStruct((D, t_len), BF16), jax.ShapeDtypeStruct((t_len, D_FF), F32),
                   jax.ShapeDtypeStruct((t_len, D), F32), jax.ShapeDtypeStruct((t_len, D), BF16),
                   jax.ShapeDtypeStruct((8, D), F32), jax.ShapeDtypeStruct((8, 128), F32)],
        scratch_shapes=[pltpu.VMEM((D_FF, D), BF16), pltpu.VMEM((D_FF, D), BF16), pltpu.SemaphoreType.DMA((2,))],
        compiler_params=pltpu.CompilerParams(dimension_semantics=("arbitrary",), vmem_limit_bytes=VMEM_LIMIT),
    )(x, mixed, tgt, *small, wpack)


def _mlp_bwd_half(pos, h2t, f, dz, wpack, prior=None):
    t_len = dz.shape[0]
    tb = TB_MLPB
    nb = t_len // tb
    first = prior is None
    flip = 1 if first else 0

    def body(pos_ref, h2t_ref, f_ref, dz_ref, w1t_ref, w2_ref, *rest):
        if first:
            dh2_ref, dw1_ref, dw2_ref = rest
        else:
            dh2in_ref, _, dh2_ref, dw1_ref, dw2_ref = rest
        k = pl.program_id(0)
        t = pl.program_id(1)
        rows = pl.ds(pl.multiple_of(t * tb, tb), tb)
        w1t = w1t_ref[0]
        w2 = w2_ref[0]
        dz = dz_ref[...]
        rf = jnp.maximum(f_ref[...], 0.0)
        df = (_dot_nt(dz, w2) * (2.0 * rf)).astype(BF16)
        dh = _dot(df, w1t)
        g1 = _dot(h2t_ref[...], df)
        g2 = _dot_tn((rf * rf).astype(BF16), dz)

        @pl.when(t == 0)
        def _():
            dw2_ref[0] = g2
            dw1_ref[0] = g1

        @pl.when(t != 0)
        def _():
            dw2_ref[0] += g2
            dw1_ref[0] += g1

        @pl.when(k == 0)
        def _():
            dh2_ref[rows, :] = dh if first else dh2in_ref[...] + dh

        @pl.when(k != 0)
        def _():
            dh2_ref[rows, :] += dh

    blk = lambda k, pos: 2 * k + jnp.bitwise_xor(pos[0], flip)
    in_specs = [pl.BlockSpec((D, tb), lambda k, t, pos: (0, t)),
                pl.BlockSpec((tb, FF_BLK), lambda k, t, pos: (t, blk(k, pos))),
                pl.BlockSpec((tb, D), lambda k, t, pos: (t, 0)),
                pl.BlockSpec((1, ROWS_W1T, D), lambda k, t, pos: (blk(k, pos), OFF_W1T // ROWS_W1T, 0)),
                pl.BlockSpec((1, ROWS_W2, D), lambda k, t, pos: (blk(k, pos), OFF_W2 // ROWS_W2, 0))]
    grad_specs = [pl.BlockSpec((1, D, FF_BLK), lambda k, t, pos: (k, 0, 0)),
                  pl.BlockSpec((1, FF_BLK, D), lambda k, t, pos: (k, 0, 0))]
    out_specs = [pl.BlockSpec((t_len, D), lambda k, t, pos: (0, 0))] + grad_specs
    grad_shapes = [jax.ShapeDtypeStruct((4, D, FF_BLK), F32), jax.ShapeDtypeStruct((4, FF_BLK, D), F32)]
    out_shape = [jax.ShapeDtypeStruct((t_len, D), F32)] + grad_shapes
    args = [pos, h2t, f, dz, wpack, wpack]
    if not first:
        in_specs += [pl.BlockSpec((tb, D), lambda k, t, pos: (jnp.where(k == 0, t, nb - 1), 0)),
                     pl.BlockSpec(prior[1].shape, lambda k, t, pos: (0,) * prior[1].ndim)]
        args += list(prior)
    return pl.pallas_call(
        body,
        name="mlp_bwd_first" if first else "mlp_bwd_second",
        grid_spec=pltpu.PrefetchScalarGridSpec(num_scalar_prefetch=1, grid=(4, nb), in_specs=in_specs,
                                               out_specs=out_specs),
        out_shape=out_shape,
        compiler_params=pltpu.CompilerParams(dimension_semantics=("arbitrary", "arbitrary"),
                                             vmem_limit_bytes=VMEM_LIMIT),
    )(*args)


V_SHIFT1, V_SCALE1, V_GATE1, V_SHIFT2, V_SCALE2, V_G1, V_G2 = 0, 1, 2, 3, 4, 6, 7
V_BL_BA, V_BX_SP, V_GL_GC, V_WL01, V_WL23, V_WS01, V_WS2 = 8, 9, 10, 11, 12, 13, 14
V_ROWS = 16


def _chip_scatter_copies(srcs, dsts, send_sems, recv_sems):
    me = _position()
    copies = []
    for a, (src, dst) in enumerate(zip(srcs, dsts)):
        for j, k in enumerate(CHIP_FLIPS):
            peer = _flip(me, k)
            copies.append(pltpu.make_async_remote_copy(
                src_ref=src.at[2 * peer[0] + peer[1]], dst_ref=dst.at[j], send_sem=send_sems.at[len(CHIP_FLIPS) * a + j],
                recv_sem=recv_sems.at[len(CHIP_FLIPS) * a + j], device_id=peer, device_id_type=MESH))
    return copies


def _mixer_bwd(x, mixed, dh2, dx2, proj, hl, kept, modraw, adab, g1, g2, wl, bl, bda, bdx, ba, bxb, ap, ws, gl, gc, avg, wpack):
    t_len = x.shape[0]
    tb = TB_MIXB
    nb = t_len // tb
    hb = tb // HALO

    def body(x_ref, mixed_ref, dh2_ref, dx2_ref, proj_ref, projh_ref, hl_ref, hlh_ref, kept_ref,
             modraw_ref, adab_ref, g1_ref, g2_ref, wl_ref, bl_ref, bda_ref, bdx_ref, ba_ref, bxb_ref, ap_ref,
             ws_ref, gl_ref, gc_ref, avg_ref, wpack_hbm,
             gx_ref, vec_ref, hb_ref, dprojt_ref, dmixed_ref, ycatt_ref, xlt_ref, dgate_ref,
             win_v, wout_v, sem, ulx_ext, cv_ext, hl_ext, a_ext, dxl_ext, dcc_ext, dcar, an_s, g_s, dh_s):
        i = pl.program_id(0)
        blk = nb - 1 - i

        @pl.when(i == 0)
        def _():
            cps = _load_packed(wpack_hbm, OFF_WIN, ROWS_WIN, win_v, sem.at[0])
            cps += _load_packed(wpack_hbm, OFF_WOUT, ROWS_WOUT, wout_v, sem.at[1])
            vec_ref[...] = jnp.zeros(vec_ref.shape, F32)
            zero = jnp.zeros((HALO, W), F32)
            a_ext[tb:tb + HALO, :] = zero
            dxl_ext[tb:tb + HALO, :] = zero
            dcc_ext[tb:tb + HALO, :] = zero
            dcar[...] = zero
            for cp in cps:
                cp.wait()

        mod = modraw_ref[...] + adab_ref[...]
        shift1, scale1, gate1, scale2 = mod[0:1], mod[1:2], mod[2:3], mod[4:5]
        x = x_ref[...]
        mixed = mixed_ref[...]

        x1 = x + gate1 * mixed
        r2 = lax.rsqrt(_rowmean(x1 * x1) + EPS)
        xn2 = x1 * r2
        dh2 = dh2_ref[...]
        vec_ref[V_SHIFT2:V_SHIFT2 + 1, :] += _colsum(dh2)
        vec_ref[V_SCALE2:V_SCALE2 + 1, :] += _colsum(dh2 * xn2 * g2_ref[...])
        vec_ref[V_G2:V_G2 + 1, :] += _colsum(dh2 * (1.0 + scale2) * xn2)
        dxn2 = dh2 * g2_ref[...] * (1.0 + scale2)
        dx1 = dx2_ref[...] + r2 * (dxn2 - xn2 * _rowmean(dxn2 * xn2))
        vec_ref[V_GATE1:V_GATE1 + 1, :] += _colsum(dx1 * mixed)
        dmixed = (gate1 * dx1).astype(BF16)

        proj = proj_ref[...]
        u_lx, u_ly, u_b, u_c, u_v = (proj[:, k * W:(k + 1) * W] for k in range(5))
        has_prev = (blk > 0).astype(F32)
        projh = projh_ref[...]
        ulx_ext[0:HALO, :] = projh[:, 0:W] * has_prev
        ulx_ext[HALO:HALO + tb, :] = u_lx
        xl, r, ig, rp, rq, cc = (kept_ref[:, k * W:(k + 1) * W] for k in range(N_KEPT))
        sp = _softplus(ap_ref[...])
        log_a = (-C_GATE) * r * sp
        a = jnp.exp(log_a)
        mult_raw = jnp.sqrt(_one_minus_sq(a, log_a))
        first = (blk * tb + lax.broadcasted_iota(jnp.int32, (tb, W), 0)) == 0
        mult = jnp.where(first, 1.0, mult_raw)
        hl = hl_ref[...]
        ge, th = _gelu(u_ly)
        pn = ge * hl * rp
        cv = u_c * u_v
        cv_ext[0:HALO, :] = projh[:, 3 * W:4 * W] * projh[:, 4 * W:5 * W] * has_prev
        cv_ext[HALO:HALO + tb, :] = cv
        qn = u_b * cc * rq

        dmixed_ref[...] = dmixed
        ycatt_ref[0:W, :] = (pn * gl_ref[...]).T.astype(BF16)
        ycatt_ref[W:2 * W, :] = (qn * gc_ref[...]).T.astype(BF16)
        dyl = _dot_nt(dmixed, wout_v[0:W, :])
        dyc = _dot_nt(dmixed, wout_v[W:2 * W, :])

        dqn = dyc * gc_ref[...]
        dq = rq * (dqn - qn * _group_mean(dqn * qn, avg_ref[...]))
        du_b = dq * cc
        dcc = dq * u_b
        dcc_ext[0:tb, :] = dcc
        dcv = ws_ref[CONV_S - 1:CONV_S, :] * dcc
        for k in range(CONV_S - 1):
            dcv = dcv + ws_ref[k:k + 1, :] * dcc_ext[pl.ds(CONV_S - 1 - k, tb), :]
        dcc_ext[tb:tb + HALO, :] = dcc_ext[0:HALO, :]
        du_c = dcv * u_v
        du_v = dcv * u_c
        dws = [_colsum(dcc * cv_ext[pl.ds(HALO - (CONV_S - 1) + k, tb), :]) for k in range(CONV_S)]

        dpn = dyl * gl_ref[...]
        dp = rp * (dpn - pn * _group_mean(dpn * pn, avg_ref[...]))
        du_ly = dp * hl * _gelu_grad(u_ly, th)
        g_s[...] = dp * ge
        a_ext[0:tb, :] = a
        an_s[...] = a_ext[pl.ds(1, tb), :]
        _scan_groups(hb, an_s, g_s, dh_s, dcar, reverse=True)
        a_ext[tb:tb + HALO, :] = a_ext[0:HALO, :]
        dh = dh_s[...]
        hl_ext[0:HALO, :] = hlh_ref[...] * has_prev
        hl_ext[HALO:HALO + tb, :] = hl
        da = dh * hl_ext[pl.ds(HALO - 1, tb), :]
        dmult = dh * (ig * xl)
        dig = dh * (mult * xl)
        dxl = dh * (mult * ig)
        dlog = da * a - jnp.where(first, 0.0, dmult * (a * a) / mult_raw)
        dr = dlog * ((-C_GATE) * sp)
        dsp = _colsum(dlog * ((-C_GATE) * r))
        dga = dr * r * (1.0 - r)
        dgx = dig * ig * (1.0 - ig)
        dgab = dga.astype(BF16)
        dgxb = dgx.astype(BF16)
        xlt_ref[...] = xl.T.astype(BF16)
        dgate_ref[:, 0:W] = dgab
        dgate_ref[:, W:2 * W] = dgxb
        dxl = dxl + _dot_nt(dgab, bda_ref[...]) + _dot_nt(dgxb, bdx_ref[...])
        dxl_ext[0:tb, :] = dxl
        du_lx = wl_ref[CONV_L - 1:CONV_L, :] * dxl
        for k in range(CONV_L - 1):
            du_lx = du_lx + wl_ref[k:k + 1, :] * dxl_ext[pl.ds(CONV_L - 1 - k, tb), :]
        dxl_ext[tb:tb + HALO, :] = dxl_ext[0:HALO, :]
        dwl = [_colsum(dxl * ulx_ext[pl.ds(HALO - (CONV_L - 1) + k, tb), :]) for k in range(CONV_L)]

        cat = lambda u, v: jnp.concatenate([u, v], axis=1)
        vec_ref[V_BL_BA:V_BL_BA + 1, :] += cat(_colsum(dxl), _colsum(dga))
        vec_ref[V_BX_SP:V_BX_SP + 1, :] += cat(_colsum(dgx), dsp)
        vec_ref[V_GL_GC:V_GL_GC + 1, :] += cat(_colsum(dyl * pn), _colsum(dyc * qn))
        vec_ref[V_WL01:V_WL01 + 1, :] += cat(dwl[0], dwl[1])
        vec_ref[V_WL23:V_WL23 + 1, :] += cat(dwl[2], dwl[3])
        vec_ref[V_WS01:V_WS01 + 1, :] += cat(dws[0], dws[1])
        vec_ref[V_WS2:V_WS2 + 1, 0:W] += dws[2]

        r1 = lax.rsqrt(_rowmean(x * x) + EPS)
        xn1 = x * r1
        hb_ref[...] = ((xn1 * g1_ref[...]) * (1.0 + scale1) + shift1).astype(BF16)
        dh_in = jnp.zeros((tb, D), F32)
        for k, du in enumerate((du_lx, du_ly, du_b, du_c, du_v)):
            dprojt_ref[k * W:(k + 1) * W, :] = du.T.astype(BF16)
            dh_in = dh_in + _dot(du.astype(BF16), win_v[k * W:(k + 1) * W, :])
        vec_ref[V_SHIFT1:V_SHIFT1 + 1, :] += _colsum(dh_in)
        vec_ref[V_SCALE1:V_SCALE1 + 1, :] += _colsum(dh_in * xn1 * g1_ref[...])
        vec_ref[V_G1:V_G1 + 1, :] += _colsum(dh_in * (1.0 + scale1) * xn1)
        dxn1 = dh_in * g1_ref[...] * (1.0 + scale1)
        gx_ref[...] = dx1 + r1 * (dxn1 - xn1 * _rowmean(dxn1 * xn1))

    rev = lambda cols: pl.BlockSpec((tb, cols), lambda i: (nb - 1 - i, 0))
    rev_t = lambda rows: pl.BlockSpec((rows, tb), lambda i: (0, nb - 1 - i))
    halo = lambda cols: pl.BlockSpec((HALO, cols), lambda i: (jnp.maximum((nb - 1 - i) * hb - 1, 0), 0))
    full = lambda a: pl.BlockSpec(a.shape, lambda i: (0,) * a.ndim)
    small = (modraw, adab, g1, g2, wl, bl, bda, bdx, ba, bxb, ap, ws, gl, gc, avg)
    ext = pltpu.VMEM((tb + HALO, W), F32)
    return pl.pallas_call(
        body,
        name="mixer_bwd",
        grid=(nb,),
        in_specs=[rev(D), rev(D), rev(D), rev(D), rev(D_IN), halo(D_IN), rev(W), halo(W), rev(N_KEPT * W)]
        + [full(a) for a in small] + [ANY],
        out_specs=[rev(D), pl.BlockSpec((V_ROWS, D), lambda i: (0, 0)), rev(D), rev_t(D_IN), rev(D), rev_t(D),
                   rev_t(W), rev(2 * W)],
        out_shape=[jax.ShapeDtypeStruct((t_len, D), F32), jax.ShapeDtypeStruct((V_ROWS, D), F32),
                   jax.ShapeDtypeStruct((t_len, D), BF16), jax.ShapeDtypeStruct((D_IN, t_len), BF16),
                   jax.ShapeDtypeStruct((t_len, D), BF16), jax.ShapeDtypeStruct((D, t_len), BF16),
                   jax.ShapeDtypeStruct((W, t_len), BF16), jax.ShapeDtypeStruct((t_len, 2 * W), BF16)],
        scratch_shapes=[pltpu.VMEM((D_IN, D), BF16), pltpu.VMEM((D, D), BF16), pltpu.SemaphoreType.DMA((2,)),
                        ext, ext, ext, ext, ext, ext, pltpu.VMEM((HALO, W), F32),
                        pltpu.VMEM((tb, W), F32), pltpu.VMEM((tb, W), F32), pltpu.VMEM((tb, W), F32)],
        compiler_params=pltpu.CompilerParams(dimension_semantics=("arbitrary",), vmem_limit_bytes=VMEM_LIMIT),
    )(x, mixed, dh2, dx2, proj, proj, hl, hl, kept, *small, wpack)


def _matmul(name, a, b, tm=512):
    m, k = a.shape
    n = b.shape[1]

    def body(a_ref, b_ref, o_ref):
        o_ref[...] = _dot(a_ref[...], b_ref[...])

    return pl.pallas_call(
        body,
        name=name,
        grid=(m // tm,),
        in_specs=[pl.BlockSpec((tm, k), lambda i: (i, 0)), pl.BlockSpec((k, n), lambda i: (0, 0))],
        out_specs=pl.BlockSpec((tm, n), lambda i: (i, 0)),
        out_shape=jax.ShapeDtypeStruct((m, n), F32),
        compiler_params=pltpu.CompilerParams(dimension_semantics=("arbitrary",), vmem_limit_bytes=VMEM_LIMIT),
    )(a, b)


def _gate_wgrad(xl_t, dgate, avg):
    hd = W // 8

    def body(a_ref, b_ref, avg_ref, o_ref):
        full = _dot(a_ref[...], b_ref[...])
        row = lax.broadcasted_iota(jnp.int32, (W, hd), 0)
        col = lax.broadcasted_iota(jnp.int32, (W, hd), 1)
        fold = ((row & (hd - 1)) == col).astype(BF16)
        keep = avg_ref[...] != 0
        for g in range(2):
            m = jnp.where(keep, full[:, g * W:(g + 1) * W], 0.0)
            hi = m.astype(BF16)
            rest = m - hi.astype(F32)
            mid = rest.astype(BF16)
            lo = (rest - mid.astype(F32)).astype(BF16)
            o_ref[g] = _dot(hi, fold) + _dot(mid, fold) + _dot(lo, fold)

    return pl.pallas_call(
        body,
        name="wgrad_gate",
        in_specs=[WHOLE] * 3,
        out_specs=WHOLE,
        out_shape=jax.ShapeDtypeStruct((2, W, hd), F32),
        compiler_params=pltpu.CompilerParams(vmem_limit_bytes=VMEM_LIMIT),
    )(xl_t, dgate, avg)


def _block_diag(w):
    n, m, _ = w.shape
    eye = jnp.eye(n, dtype=w.dtype)
    return (w[:, :, None, :] * eye[:, None, :, None]).reshape(n * m, n * m)


def _pad_rows(a, rows):
    return jnp.pad(a, ((0, rows - a.shape[0]),) + ((0, 0),) * (a.ndim - 1))


def _position():
    return lax.axis_index("x"), lax.axis_index("y"), lax.axis_index("c")


def _linear(pos):
    return 4 * pos[0] + 2 * pos[1] + pos[2]


def _flip(pos, k):
    return tuple(1 - p if k & bit else p for p, bit in zip(pos, (4, 2, 1)))


def _exchange_all(make_copy, make_arrival):
    copies = [make_copy(k) for k in range(1, N_DEV)]
    for cp in copies:
        cp.start()
    for k in range(1, N_DEV):
        make_arrival(k).wait_recv()
    for cp in copies:
        cp.wait_send()


def _mod_exchange_steps(cols):
    def steps(msg_ref, adaw_ref, gath_ref, mod_ref, sendbuf, send_a, recv_a, send_b, recv_b):
        me = _position()
        me_lin = _linear(me)
        peers = range(1, N_DEV)
        m = msg_ref[...]
        row = lax.broadcasted_iota(jnp.int32, m.shape, 0)
        gath_ref[me_lin] = jnp.where(row == 0, m * _sigmoid(m), m)

        def gather_copy(k, src_lin):
            return pltpu.make_async_remote_copy(
                src_ref=gath_ref.at[src_lin], dst_ref=gath_ref.at[src_lin], send_sem=send_a.at[k - 1],
                recv_sem=recv_a.at[k - 1], device_id=_flip(me, k), device_id_type=MESH)

        first_round = [gather_copy(k, me_lin) for k in peers]
        for cp in first_round:
            cp.start()
        yield
        for k in peers:
            gather_copy(k, _linear(_flip(me, k))).wait_recv()

        sc_all = gath_ref[:, 0, :]
        scb = jnp.concatenate([sc_all, jnp.zeros_like(sc_all)], axis=0).astype(BF16)
        prod = _dot(scb, adaw_ref[...].astype(BF16))
        for b in range(N_DEV):
            sendbuf[b] = jnp.broadcast_to(prod[b:b + 1, :], (HALO, cols))
        mod_ref[me_lin] = sendbuf[me_lin]

        def row_copy(k, dst_lin):
            peer = _flip(me, k)
            return pltpu.make_async_remote_copy(
                src_ref=sendbuf.at[_linear(peer)], dst_ref=mod_ref.at[dst_lin], send_sem=send_b.at[k - 1],
                recv_sem=recv_b.at[k - 1], device_id=peer, device_id_type=MESH)

        second_round = [row_copy(k, me_lin) for k in peers]
        for cp in second_round:
            cp.start()
        yield
        for k in peers:
            row_copy(k, _linear(_flip(me, k))).wait_recv()
        for cp in first_round + second_round:
            cp.wait_send()

    return steps


def _gather_and_mod(msg, ada_w, block):
    rows, cols = block.shape
    mod_cols = ada_w.shape[1]
    mod_steps = _mod_exchange_steps(mod_cols)

    def body(msg_ref, adaw_ref, x_ref, gath_ref, mod_ref, out_ref, sendbuf, send_a, recv_a, send_b, recv_b,
             send_sems, recv_sems, sib_send_sems, sib_recv_sems, local_sem):
        x, y, c = _position()
        me, sibling = (x, y, c), (x, y, 1 - c)
        sends, forward, arrivals = _chip_gather_copies(x_ref, out_ref, send_sems, recv_sems)

        def to_sibling(j, block_of, src=None):
            dst = out_ref.at[_linear(block_of)]
            return pltpu.make_async_remote_copy(
                src_ref=dst if src is None else src, dst_ref=dst, send_sem=sib_send_sems.at[j],
                recv_sem=sib_recv_sems.at[j], device_id=sibling, device_id_type=MESH)

        mine = pltpu.make_async_copy(x_ref, out_ref.at[_linear(me)], local_sem)
        mine.start()
        passes = [to_sibling(0, me, src=x_ref)] + [to_sibling(1 + j, p) for j, p in enumerate(_route_peers(me))]
        small = mod_steps(msg_ref, adaw_ref, gath_ref, mod_ref, sendbuf, send_a, recv_a, send_b, recv_b)
        next(small)
        passes[0].start()
        for cp in sends:
            cp.start()
        next(small)
        arrivals[0].wait_recv()
        forward.start()
        passes[1].start()
        arrivals[1].wait_recv()
        passes[2].start()
        arrivals[2].wait_recv()
        passes[3].start()
        for _ in small:
            pass
        for j, p in enumerate((sibling,) + _route_peers(sibling)):
            to_sibling(j, p).wait_recv()
        for cp in sends + [forward] + passes:
            cp.wait_send()
        mine.wait()

    return pl.pallas_call(
        body,
        name="gather_and_mod",
        in_specs=[WHOLE, WHOLE, ANY],
        out_specs=[WHOLE, WHOLE, ANY],
        out_shape=[jax.ShapeDtypeStruct((N_DEV, HALO, D), F32), jax.ShapeDtypeStruct((N_DEV, HALO, mod_cols), F32),
                   jax.ShapeDtypeStruct((N_DEV, rows, cols), block.dtype)],
        scratch_shapes=[pltpu.VMEM((N_DEV, HALO, mod_cols), F32)] + [pltpu.SemaphoreType.DMA((N_DEV - 1,))] * 4
        + [pltpu.SemaphoreType.DMA((3,)), pltpu.SemaphoreType.DMA((3,)), pltpu.SemaphoreType.DMA((4,)),
           pltpu.SemaphoreType.DMA((4,)), pltpu.SemaphoreType.DMA],
        compiler_params=pltpu.CompilerParams(vmem_limit_bytes=VMEM_LIMIT),
    )(msg, ada_w, block)


HBM = pl.BlockSpec(memory_space=pltpu.HBM)
SEM = pl.BlockSpec(memory_space=pltpu.SEMAPHORE)
EFFECT = pltpu.SideEffectType.DATAFLOW_SIDE_EFFECTING


def _stage_copies(stage):
    return {"chips": (_chip_scatter_copies, len(CHIP_FLIPS), len(CHIP_FLIPS)), "sibling": (_sibling_copies, 4, 4)}[stage]


def _chips_start(which, chip_sums, stage="chips"):
    n = len(chip_sums)
    make_copies, per_array, slots = _stage_copies(stage)
    n_sems = per_array * n

    def body(*refs):
        srcs, dsts = refs[:n], refs[n:2 * n]
        send_sems, recv_sems = refs[2 * n:2 * n + 2]
        token = refs[-1]
        for cp in make_copies(srcs, dsts, send_sems, recv_sems):
            cp.start()
        token[...] = jnp.zeros(token.shape, token.dtype)

    landing = [jax.ShapeDtypeStruct((slots,) + s.shape[-2:], s.dtype) for s in chip_sums]
    outs = pl.pallas_call(
        body,
        name=which + "_" + stage + "_start",
        in_specs=[HBM] * (2 * n),
        out_specs=[SEM, SEM] + [HBM] * (2 * n) + [WHOLE],
        out_shape=[pltpu.SemaphoreType.DMA((n_sems,)), pltpu.SemaphoreType.DMA((n_sems,))]
        + [pltpu.HBM(s.shape, s.dtype) for s in chip_sums] + [pltpu.HBM(s.shape, s.dtype) for s in landing]
        + [jax.ShapeDtypeStruct((HALO, 128), F32)],
        input_output_aliases={i: 2 + i for i in range(2 * n)},
        compiler_params=pltpu.CompilerParams(has_side_effects=EFFECT),
    )(*[pltpu.with_memory_space_constraint(s, pltpu.HBM) for s in chip_sums],
      *[pltpu.with_memory_space_constraint(lax.empty(s.shape, s.dtype), pltpu.HBM) for s in landing])
    return outs[0], outs[1], outs[2:2 + n], outs[2 + n:2 + 2 * n], outs[-1]


def _chips_wait(which, send_sems, recv_sems, srcs, landed, after, stage="chips"):
    n = len(srcs)
    make_copies = _stage_copies(stage)[0]

    def body(*refs):
        src_refs, dst_refs = refs[:n], refs[n:2 * n]
        sends, recvs = refs[2 * n:2 * n + 2]
        copies = make_copies(src_refs, dst_refs, sends, recvs)
        for cp in copies:
            cp.wait_send()
        for cp in copies:
            cp.wait_recv()

    outs = pl.pallas_call(
        body,
        name=which + "_" + stage + "_wait",
        in_specs=[HBM] * (2 * n) + [SEM, SEM, ANY],
        out_specs=[HBM] * (2 * n),
        out_shape=[pltpu.HBM(s.shape, s.dtype) for s in list(srcs) + list(landed)],
        input_output_aliases={i: i for i in range(2 * n)},
        compiler_params=pltpu.CompilerParams(has_side_effects=EFFECT),
    )(*srcs, *landed, send_sems, recv_sems, after)
    return list(outs[:n]), list(outs[n:])


def _sibling_copies(srcs, dsts, send_sems, recv_sems):
    x, y, c = _position()
    copies = []
    for a, (src, dst) in enumerate(zip(srcs, dsts)):
        for k in range(4):
            copies.append(pltpu.make_async_remote_copy(
                src_ref=src.at[k, 1 - c] if len(src.shape) == 4 else src.at[k], dst_ref=dst.at[k],
                send_sem=send_sems.at[4 * a + k],
                recv_sem=recv_sems.at[4 * a + k], device_id=(x, y, 1 - c), device_id_type=MESH))
    return copies


def _row_block(rows):
    return min(rows, 512)


def _pair_sum(pos, mine, recv):
    _, cores, rows, cols = mine.shape
    rb = _row_block(rows)

    def body(pos_ref, mine_ref, recv_ref, out_ref):
        out_ref[0] = (mine_ref[0, 0] + recv_ref[0]).astype(BF16)

    other = lambda k, pos: jnp.bitwise_xor(pos[1], k + 1)
    core = lambda pos: pos[0] * (cores - 1)
    return pl.pallas_call(
        body,
        name="grad_pair_sum",
        grid_spec=pltpu.PrefetchScalarGridSpec(
            num_scalar_prefetch=1, grid=(3, rows // rb),
            in_specs=[pl.BlockSpec((1, 1, rb, cols), lambda k, r, pos: (other(k, pos), core(pos), r, 0)),
                      pl.BlockSpec((1, rb, cols), lambda k, r, pos: (other(k, pos), r, 0))],
            out_specs=pl.BlockSpec((1, rb, cols), lambda k, r, pos: (other(k, pos), r, 0))),
        out_shape=jax.ShapeDtypeStruct((4, rows, cols), BF16),
        compiler_params=pltpu.CompilerParams(dimension_semantics=("arbitrary", "arbitrary")),
    )(pos, mine, recv)


def _final_sum(pos, mine, recv, chips):
    _, cores, rows, cols = mine.shape
    rb = _row_block(rows)

    def body(pos_ref, mine_ref, recv_ref, chips_ref, out_ref):
        g = mine_ref[0, 0] + recv_ref[0]
        for j in range(3):
            g = g + chips_ref[j].astype(F32)
        out_ref[...] = g

    return pl.pallas_call(
        body,
        name="grad_final_sum",
        grid_spec=pltpu.PrefetchScalarGridSpec(
            num_scalar_prefetch=1, grid=(rows // rb,),
            in_specs=[pl.BlockSpec((1, 1, rb, cols), lambda r, pos: (pos[1], pos[0] * (cores - 1), r, 0)),
                      pl.BlockSpec((1, rb, cols), lambda r, pos: (pos[1], r, 0)),
                      pl.BlockSpec((3, rb, cols), lambda r, pos: (0, r, 0))],
            out_specs=pl.BlockSpec((rb, cols), lambda r, pos: (r, 0))),
        out_shape=jax.ShapeDtypeStruct((rows, cols), F32),
        compiler_params=pltpu.CompilerParams(dimension_semantics=("arbitrary",)),
    )(pos, mine, recv, chips)


LOSS_ROW = V_ROWS + 8
GB_BASE = LOSS_ROW + 8


def _route_mod_grad_steps(cols):
    def steps(gmod_ref, sct_ref, gadaw_ref, sendbuf, grecv, send_a, recv_a):
        me = _position()
        me_lin = _linear(me)
        gm = gmod_ref[...]
        for b in range(N_DEV):
            sendbuf[b] = jnp.broadcast_to(gm[b:b + 1, :], (HALO, cols))
        grecv[me_lin] = sendbuf[me_lin]

        def row_copy(k, dst_lin):
            peer = _flip(me, k)
            return pltpu.make_async_remote_copy(
                src_ref=sendbuf.at[_linear(peer)], dst_ref=grecv.at[dst_lin], send_sem=send_a.at[k - 1],
                recv_sem=recv_a.at[k - 1], device_id=peer, device_id_type=MESH)

        _exchange_all(lambda k: row_copy(k, me_lin), lambda k: row_copy(k, _linear(_flip(me, k))))
        g_all = grecv[:, 0, :]
        g_pad = jnp.concatenate([g_all, jnp.zeros((sct_ref.shape[1] - N_DEV, cols), F32)], axis=0).astype(BF16)
        gadaw_ref[...] = _dot(sct_ref[...], g_pad)
        return _colsum(g_all)

    return steps


def _small_grad_exchange(gmod8, sc_t, msg_vec, msg_gate, after):
    cols = gmod8.shape[1]
    vec_rows = GB_BASE + N_DEV
    route_steps = _route_mod_grad_steps(cols)

    def body(gmod_ref, sct_ref, vec_ref, gate_ref, after_ref, gadaw_ref, sumv_ref, sumg_ref,
             sendbuf, grecv, send_a, recv_a, myv, myg, sibv, sibg, chipv, chipg, sib_send, sib_recv, peer_send, peer_recv):
        x, y, c = me = _position()
        my_chip = 2 * x + y

        def swap(a, src, dst):
            return pltpu.make_async_remote_copy(
                src_ref=src, dst_ref=dst, send_sem=sib_send.at[a], recv_sem=sib_recv.at[a], device_id=(x, y, 1 - c),
                device_id_type=MESH)

        def chip_copy(a, buf, j, k, slot_chip):
            peer = _flip(me, k)
            return pltpu.make_async_remote_copy(
                src_ref=buf.at[slot_chip], dst_ref=buf.at[slot_chip], send_sem=peer_send.at[3 * a + j],
                recv_sem=peer_recv.at[3 * a + j], device_id=peer, device_id_type=MESH)

        def chip_stage(a, mine, theirs, buf):
            buf[my_chip] = (mine[...] + theirs[...]).astype(buf.dtype)
            sends = [chip_copy(a, buf, j, k, my_chip) for j, k in enumerate(CHIP_FLIPS)]
            for cp in sends:
                cp.start()
            return sends

        myg[...] = gate_ref[...]
        swap_g = swap(1, myg, sibg)
        swap_g.start()
        gb = route_steps(gmod_ref, sct_ref, gadaw_ref, sendbuf, grecv, send_a, recv_a)
        swap_g.wait_recv()
        sends = chip_stage(1, myg, sibg, chipg)

        myv[0:GB_BASE, :] = vec_ref[...]
        slot = lax.broadcasted_iota(jnp.int32, (N_DEV, D), 0) == _linear(me)
        gb_wide = jnp.concatenate([jnp.broadcast_to(gb, (N_DEV, cols)), jnp.zeros((N_DEV, D - cols), F32)], axis=1)
        myv[GB_BASE:vec_rows, :] = jnp.where(slot, gb_wide, 0.0)
        swap_v = swap(0, myv, sibv)
        swap_v.start()
        swap_v.wait_recv()
        sends += chip_stage(0, myv, sibv, chipv)

        for a, buf in enumerate((chipv, chipg)):
            for j, k in enumerate(CHIP_FLIPS):
                peer = _flip(me, k)
                chip_copy(a, buf, j, k, 2 * peer[0] + peer[1]).wait_recv()
        sumv_ref[...] = ((chipv[0] + chipv[1]) + chipv[2]) + chipv[3]
        gate_sum = lambda k: chipg[k].astype(F32)
        sumg_ref[...] = ((gate_sum(0) + gate_sum(1)) + gate_sum(2)) + gate_sum(3)
        for cp in [swap_g, swap_v] + sends:
            cp.wait_send()

    vshape, gshape = (vec_rows, D), msg_gate.shape
    return pl.pallas_call(
        body,
        name="small_grad_exchange",
        in_specs=[WHOLE] * 5,
        out_specs=[WHOLE] * 3,
        out_shape=[jax.ShapeDtypeStruct((D, cols), F32), jax.ShapeDtypeStruct(vshape, F32),
                   jax.ShapeDtypeStruct(gshape, F32)],
        scratch_shapes=[pltpu.VMEM((N_DEV, HALO, cols), F32), pltpu.VMEM((N_DEV, HALO, cols), F32),
                        pltpu.SemaphoreType.DMA((N_DEV - 1,)), pltpu.SemaphoreType.DMA((N_DEV - 1,)),
                        pltpu.VMEM(vshape, F32), pltpu.VMEM(gshape, F32), pltpu.VMEM(vshape, F32),
                        pltpu.VMEM(gshape, F32), pltpu.VMEM((4,) + vshape, F32), pltpu.VMEM((4,) + gshape, BF16),
                        pltpu.SemaphoreType.DMA((2,)), pltpu.SemaphoreType.DMA((2,)),
                        pltpu.SemaphoreType.DMA((2 * len(CHIP_FLIPS),)), pltpu.SemaphoreType.DMA((2 * len(CHIP_FLIPS),))],
        compiler_params=pltpu.CompilerParams(vmem_limit_bytes=VMEM_LIMIT),
    )(gmod8, sc_t, msg_vec, msg_gate, after)


def _adamw_math(w, g, m, v):
    m = ADAM_B1 * m + (1.0 - ADAM_B1) * g
    v = ADAM_B2 * v + (1.0 - ADAM_B2) * (g * g)
    m_hat = m / (1.0 - ADAM_B1 ** ADAM_STEP)
    v_hat = v / (1.0 - ADAM_B2 ** ADAM_STEP)
    delta = -ADAM_LR * (m_hat / (jnp.sqrt(v_hat) + ADAM_EPS) + ADAM_WD * w)
    return delta, m, v


def _adamw(name, w, g, m, v):
    rows, cols = w.shape
    rb = 256 if rows % 256 == 0 else rows

    def body(w_ref, g_ref, m_ref, v_ref, d_ref, mo_ref, vo_ref):
        d_ref[...], mo_ref[...], vo_ref[...] = _adamw_math(w_ref[...], g_ref[...], m_ref[...], v_ref[...])

    spec = pl.BlockSpec((rb, cols), lambda r: (r, 0))
    return pl.pallas_call(
        body,
        name="adamw_" + name,
        grid=(rows // rb,),
        in_specs=[spec] * 4,
        out_specs=[spec] * 3,
        out_shape=[jax.ShapeDtypeStruct((rows, cols), F32)] * 3,
        compiler_params=pltpu.CompilerParams(dimension_semantics=("arbitrary",)),
    )(w, g, m, v)


def _update(pos, sum_jobs, plain_jobs, after):
    rb = 256
    jobs = [("sum", j) for j in sum_jobs] + [("plain", j) for j in plain_jobs]
    offs, total = [], 0
    for _, j in jobs:
        offs.append(total)
        total += j[-1].shape[0] // rb
    n_in = sum(len(j) for _, j in jobs)

    def body(pos_ref, *refs):
        ins, outs = refs[:n_in], refs[n_in + 1:]
        s = pl.program_id(0)
        i_in = i_out = 0
        for (kind, j), off in zip(jobs, offs):
            steps = j[-1].shape[0] // rb
            j_in = ins[i_in:i_in + len(j)]
            i_in += len(j)
            j_out = outs[i_out:i_out + (4 if kind == "sum" else 3)]
            i_out += len(j_out)

            @pl.when((s >= off) & (s < off + steps))
            def _(kind=kind, j_in=j_in, j_out=j_out):
                if kind == "sum":
                    mine_ref, recv_ref, chips_ref, w_ref, m_ref, v_ref = j_in
                    g = mine_ref[0, 0] + recv_ref[0]
                    for q in range(len(CHIP_FLIPS)):
                        g = g + chips_ref[q].astype(F32)
                    j_out[0][...] = g
                    rest = j_out[1:]
                else:
                    g_ref, w_ref, m_ref, v_ref = j_in
                    g = g_ref[...]
                    rest = j_out
                rest[0][...], rest[1][...], rest[2][...] = _adamw_math(w_ref[...], g, m_ref[...], v_ref[...])

    in_specs, out_specs, out_shape, args = [], [], [], []
    for (kind, j), off in zip(jobs, offs):
        rows, cols = j[-1].shape
        steps = rows // rb
        blk = lambda s, off=off, steps=steps: jnp.clip(s - off, 0, steps - 1)
        flat = pl.BlockSpec((rb, cols), lambda s, pos, blk=blk: (blk(s), 0))
        if kind == "sum":
            in_specs += [pl.BlockSpec((1, 1, rb, cols), lambda s, pos, blk=blk: (pos[1], 0, blk(s), 0)),
                         pl.BlockSpec((1, rb, cols), lambda s, pos, blk=blk: (pos[1], blk(s), 0)),
                         pl.BlockSpec((len(CHIP_FLIPS), rb, cols), lambda s, pos, blk=blk: (0, blk(s), 0))]
            in_specs += [flat] * 3
        else:
            in_specs += [flat] * 4
        n_res = 4 if kind == "sum" else 3
        out_specs += [flat] * n_res
        out_shape += [jax.ShapeDtypeStruct((rows, cols), F32)] * n_res
        args += list(j)
    in_specs += [pl.BlockSpec(after.shape, lambda s, pos: (0,) * after.ndim)]
    outs = pl.pallas_call(
        body,
        name="update",
        grid_spec=pltpu.PrefetchScalarGridSpec(
            num_scalar_prefetch=1, grid=(total,), in_specs=in_specs, out_specs=out_specs),
        out_shape=out_shape,
        compiler_params=pltpu.CompilerParams(dimension_semantics=("arbitrary",), vmem_limit_bytes=VMEM_LIMIT),
    )(pos, *args, after)
    sums = [tuple(outs[4 * i:4 * i + 4]) for i in range(len(sum_jobs))]
    base = 4 * len(sum_jobs)
    plains = [tuple(outs[base + 3 * i:base + 3 * i + 3]) for i in range(len(plain_jobs))]
    return sums, plains


def _adamw_small(ws, gs, ms, vs, sigmoid_scaled):
    n = len(ws)

    def body(*refs):
        w_refs, g_refs, m_refs, v_refs = (refs[i * n:(i + 1) * n] for i in range(4))
        outs = refs[4 * n:]
        for i in range(n):
            w = w_refs[i][...]
            g = g_refs[i][...]
            if sigmoid_scaled[i]:
                g = g * _sigmoid(w)
            delta, m, v = _adamw_math(w, g, m_refs[i][...], v_refs[i][...])
            outs[4 * i][...] = g
            outs[4 * i + 1][...] = delta
            outs[4 * i + 2][...] = m
            outs[4 * i + 3][...] = v

    shapes = [jax.ShapeDtypeStruct(w.shape, F32) for w in ws for _ in range(4)]
    outs = pl.pallas_call(
        body,
        name="adamw_small",
        in_specs=[WHOLE] * (4 * n),
        out_specs=[WHOLE] * (4 * n),
        out_shape=shapes,
    )(*ws, *gs, *ms, *vs)
    return [outs[4 * i:4 * i + 4] for i in range(n)]


_WEIGHT_NAMES = ("ada_w", "ada_b", "norm1_g", "w_in", "lru_conv_w", "lru_conv_b", "gate_a_w", "gate_a_b", "gate_x_w",
                 "gate_x_b", "a_param", "short_conv_w", "lru_out_g", "conv_out_g", "w_out", "norm2_g", "w_mlp1",
                 "w_mlp2", "final_g")


def kernel(x, c, ada_w, ada_b, norm1_g, w_in, lru_conv_w, lru_conv_b, gate_a_w, gate_a_b, gate_x_w, gate_x_b, a_param, short_conv_w, lru_out_g, conv_out_g, w_out, norm2_g, w_mlp1, w_mlp2, final_g, loss_target, m_ada_w, m_ada_b, m_norm1_g, m_w_in, m_lru_conv_w, m_lru_conv_b, m_gate_a_w, m_gate_a_b, m_gate_x_w, m_gate_x_b, m_a_param, m_short_conv_w, m_lru_out_g, m_conv_out_g, m_w_out, m_norm2_g, m_w_mlp1, m_w_mlp2, m_final_g, v_ada_w, v_ada_b, v_norm1_g, v_w_in, v_lru_conv_w, v_lru_conv_b, v_gate_a_w, v_gate_a_b, v_gate_x_w, v_gate_x_b, v_a_param, v_short_conv_w, v_lru_out_g, v_conv_out_g, v_w_out, v_norm2_g, v_w_mlp1, v_w_mlp2, v_final_g):
    given = dict(locals())
    weights = {n: given[n] for n in _WEIGHT_NAMES}
    xi, yi, ci = _position()
    me_lin = _linear((xi, yi, ci))
    hd = W // N_DEV

    mixer_block = jnp.concatenate([w_out[0], w_in[0].T], axis=0).astype(BF16)
    mlp_block = jnp.concatenate([w_mlp1[0].T, w_mlp2[0]], axis=0).astype(BF16)

    msg = (jnp.pad(c, ((0, HALO - 1), (0, 0)))
           + jnp.pad(lru_conv_w[0], ((1, HALO - 1 - CONV_L), (0, D - hd)))
           + jnp.pad(short_conv_w[0], ((1 + CONV_L, 0), (0, D - hd))))
    gath, mod_all, wmix = _gather_and_mod(msg, ada_w[0], mixer_block)
    sc_all = gath[:, 0, :]
    wl = jnp.transpose(gath[:, 1:1 + CONV_L, :hd], (1, 0, 2)).reshape(CONV_L, W)
    ws = jnp.transpose(gath[:, 1 + CONV_L:HALO, :hd], (1, 0, 2)).reshape(CONV_S, W)
    modraw = _pad_rows(mod_all[:, 0, :].reshape(6, D), HALO)
    adab = _pad_rows(ada_b.reshape(6, D), HALO)

    x2d, tgt = x[0], loss_target[0]
    gf = final_g.reshape(1, D)
    bda = _block_diag(gate_a_w[0]).astype(BF16)
    bdx = _block_diag(gate_x_w[0]).astype(BF16)
    avg = _block_diag(jnp.full((8, W // 8, W // 8), 8.0 / W, F32)).astype(BF16)
    wl8 = _pad_rows(wl, HALO)
    ws8 = _pad_rows(ws, HALO)
    mixer_small = (wl8, lru_conv_b, bda, bdx, gate_a_b, gate_x_b, a_param, ws8, lru_out_g, conv_out_g, avg)
    proj, hl, mixed, kept, wmlp = _mixer_fwd(x2d, modraw, adab, norm1_g, *mixer_small, wmix, mlp_block)
    h2t, f, dx2, dz, vec2, loss8 = _mlp_fwd(x2d, mixed, tgt, modraw, adab, norm2_g, gf, wmlp)
    pos = jnp.stack([ci, 2 * xi + yi]).astype(jnp.int32)
    by_dest = lambda g: g.reshape((4, 2, -1) + g.shape[-1:])
    dh2_first, *for_sibling = _mlp_bwd_half(pos, h2t, f, dz, wmlp)
    sib_send, sib_recv, sib_thru, sib_land, token = _chips_start("mlp", for_sibling, stage="sibling")
    dh2, dw1, dw2 = _mlp_bwd_half(pos, h2t, f, dz, wmlp, prior=(dh2_first, token))
    done = dh2[0:HALO, 0:128] + dw1[0, 0:HALO, 0:128] + dw2[0, 0:HALO, 0:128]
    _, mlp_sib = _chips_wait("mlp", sib_send, sib_recv, sib_thru, sib_land, done, stage="sibling")
    mlp_parts = [dw1[:, None], dw2[:, None]]
    mlp_sums = [_pair_sum(pos, p, r) for p, r in zip(mlp_parts, mlp_sib)]
    mlp_send, mlp_recv, mlp_thru, mlp_land, token = _chips_start("mlp", mlp_sums)
    modraw_after = modraw + jnp.tile(token, (1, D // token.shape[1]))
    gx, vec, hb, dproj_t, dmixed, ycat_t, xl_t, dgate = _mixer_bwd(
        x2d, mixed, dh2, dx2, proj, hl, kept, modraw_after, adab, norm1_g, norm2_g, *mixer_small, wmix)
    dwint = _matmul("wgrad_in", dproj_t, hb)
    dwout = _matmul("wgrad_out", ycat_t, dmixed)
    gate_blocks = _gate_wgrad(xl_t, dgate, avg)
    msg_gate = gate_blocks.reshape(W, 128)
    done = dwint[0:HALO, 0:128] + dwout[0:HALO, 0:128] + gate_blocks[0, 0:HALO, :].sum() + gx[0:HALO, 0:128]
    _, mlp_chips = _chips_wait("mlp", mlp_send, mlp_recv, mlp_thru, mlp_land, done)
    mix_parts = [by_dest(dwout), by_dest(dwint)]
    gmod8 = (jnp.pad(vec[0:5], ((0, 1), (0, 0))) + jnp.pad(vec2[0:1], ((5, 0), (0, 0)))).reshape(N_DEV, 6 * D // N_DEV)
    sc_t = jnp.pad(sc_all.T, ((0, 0), (0, 128 - N_DEV))).astype(BF16)
    loss_rows = jnp.pad(loss8[0:1], ((0, HALO - 1), (0, D - loss8.shape[1])))
    msg_vec = jnp.concatenate([vec, vec2, loss_rows], axis=0)
    sib_send, sib_recv, sib_thru, sib_land, token = _chips_start("mixer", mix_parts, stage="sibling")
    g_adaw, sum_vec, sum_gate = _small_grad_exchange(gmod8, sc_t, msg_vec, msg_gate, token)
    mix_parts, mix_sib = _chips_wait("mixer", sib_send, sib_recv, sib_thru, sib_land, sum_vec[0:HALO, 0:128],
                                     stage="sibling")
    mix_sums = [_pair_sum(pos, p, r) for p, r in zip(mix_parts, mix_sib)]
    state = lambda n: (weights[n][0], given["m_" + n][0], given["v_" + n][0])
    mlp_jobs = [(p, r, q, *state(n)) for p, r, q, n in zip(mlp_parts, mlp_sib, mlp_chips, ("w_mlp1", "w_mlp2"))]
    mix_send, mix_recv, mix_thru, mix_land, token = _chips_start("mixer", mix_sums)
    mlp_done, (adaw_done,) = _update(pos, mlp_jobs, [(g_adaw, *state("ada_w"))], token)
    loss = sum_vec[LOSS_ROW, 0]
    sum_gate = sum_gate.reshape(2, W, W // 8)
    lo, hi = slice(0, W), slice(W, 2 * W)
    wl_full = sum_vec[V_WL01:V_WL23 + 1].reshape(CONV_L, W)
    ws_full = sum_vec[V_WS01:V_WS2 + 1].reshape(CONV_S + 1, W)[:CONV_S]
    row = lambda r, cols: sum_vec[r:r + 1, cols]
    small_grads = {
        "ada_b": sum_vec[GB_BASE:GB_BASE + N_DEV, :6 * D // N_DEV].reshape(1, 6 * D),
        "norm1_g": row(V_G1, slice(0, D)),
        "lru_conv_w": lax.dynamic_slice(wl_full, (0, me_lin * hd), (CONV_L, hd)),
        "lru_conv_b": row(V_BL_BA, lo),
        "gate_a_w": sum_gate[0],
        "gate_a_b": row(V_BL_BA, hi),
        "gate_x_w": sum_gate[1],
        "gate_x_b": row(V_BX_SP, lo),
        "a_param": row(V_BX_SP, hi),
        "short_conv_w": lax.dynamic_slice(ws_full, (0, me_lin * hd), (CONV_S, hd)),
        "lru_out_g": row(V_GL_GC, lo),
        "conv_out_g": row(V_GL_GC, hi),
        "norm2_g": row(V_G2, slice(0, D)),
        "final_g": sum_vec[V_ROWS + 1:V_ROWS + 2, :],
    }
    names = list(small_grads)
    as2d = lambda a, n: a.reshape(small_grads[n].shape)
    small = _adamw_small([as2d(weights[n], n) for n in names], [small_grads[n] for n in names],
                         [as2d(given["m_" + n], n) for n in names], [as2d(given["v_" + n], n) for n in names],
                         [n == "a_param" for n in names])
    result = {n: tuple(o.reshape(weights[n].shape) for o in outs) for n, outs in zip(names, small)}

    done = (small[0][1][:, 0:128] + mlp_done[0][2][0:HALO, 0:128] + mlp_done[1][2][0:HALO, 0:128]
            + adaw_done[1][0:HALO, 0:128])
    _, mix_chips = _chips_wait("mixer", mix_send, mix_recv, mix_thru, mix_land, done)
    g_wout, g_wint = (_final_sum(pos, p, r, q) for p, r, q in zip(mix_parts, mix_sib, mix_chips))
    for n, g in (("w_in", g_wint.T), ("w_out", g_wout)):
        w, m, v = state(n)
        result[n] = (g[None],) + tuple(o[None] for o in _adamw(n, w, g, m, v))
    result["w_mlp1"], result["w_mlp2"] = (tuple(o[None] for o in done) for done in mlp_done)
    result["ada_w"] = (g_adaw[None],) + tuple(o[None] for o in adaw_done)

    return (loss, gx[None], *[result[n][0] for n in _WEIGHT_NAMES], *[result[n][1] for n in _WEIGHT_NAMES],
            *[result[n][2] for n in _WEIGHT_NAMES], *[result[n][3] for n in _WEIGHT_NAMES])
```

```python
import jax
import jax.numpy as jnp
import numpy as np
from jax import lax
from jax.experimental import pallas as pl
from jax.experimental.pallas import tpu as pltpu

F32 = jnp.float32
BF16 = jnp.bfloat16
MESH = pl.DeviceIdType.MESH

N_DEV = 8
D = 1024
W = 512
D_IN = 5 * W
D_FF = 4096
FF_BLK = D_FF // N_DEV
EPS = 1e-6
C_GATE = 8.0
CONV_L = 4
CONV_S = 3
HALO = 8

ROWS_W1T, ROWS_W2, ROWS_WOUT, ROWS_WIN = FF_BLK, FF_BLK, D // N_DEV, D_IN // N_DEV
OFF_WOUT = 0
OFF_WIN = OFF_WOUT + ROWS_WOUT
MIX_ROWS = OFF_WIN + ROWS_WIN
OFF_W1T = 0
OFF_W2 = OFF_W1T + ROWS_W1T
MLP_ROWS = OFF_W2 + ROWS_W2
CHIP_FLIPS = (4, 2, 6)
N_KEPT = 6

ADAM_LR = 0.001
ADAM_B1 = 0.9
ADAM_B2 = 0.999
ADAM_EPS = 1e-08
ADAM_WD = 0.01
ADAM_STEP = 10

VMEM_LIMIT = 56 * 1024 * 1024

TB_MIX = 256
TB_MIXB = 256
TB_MLP = 256
TB_MLPB = 512

ANY = pl.BlockSpec(memory_space=pl.ANY)
WHOLE = pl.BlockSpec(memory_space=pltpu.VMEM)


def _dot(a, b):
    return jnp.dot(a, b, preferred_element_type=F32)


def _dot_nt(a, b):
    return lax.dot_general(a, b, (((1,), (1,)), ((), ())), preferred_element_type=F32)


def _dot_tn(a, b):
    return lax.dot_general(a, b, (((0,), (0,)), ((), ())), preferred_element_type=F32)


def _sigmoid(v):
    return 1.0 / (1.0 + jnp.exp(-v))


def _softplus(v):
    t = jnp.exp(-jnp.abs(v))
    small = t * (1.0 - t * (0.5 - t * (1.0 / 3.0)))
    return jnp.maximum(v, 0.0) + jnp.where(t < 1e-2, small, jnp.log(1.0 + t))


def _one_minus_sq(a, log_a):
    return -jnp.tanh(log_a) * (a * a + 1.0)


_GELU_K = 0.7978845608028654
_GELU_C = 0.044715


def _gelu(u):
    th = jnp.tanh(_GELU_K * (u + _GELU_C * u * u * u))
    return 0.5 * u * (1.0 + th), th


def _gelu_grad(u, th):
    return 0.5 * (1.0 + th) + 0.5 * u * (1.0 - th * th) * _GELU_K * (1.0 + 3.0 * _GELU_C * u * u)


def _group_mean(v, avg):
    hi = v.astype(BF16)
    lo = (v - hi.astype(F32)).astype(BF16)
    return _dot(hi, avg) + _dot(lo, avg)


def _colsum(v):
    return jnp.sum(v, axis=0, keepdims=True)


def _rowmean(v):
    return jnp.mean(v, axis=-1, keepdims=True)


def _load_packed(wpack_hbm, off, rows, dst, sem):
    copies = [
        pltpu.make_async_copy(wpack_hbm.at[d, pl.ds(off, rows), :], dst.at[pl.ds(d * rows, rows), :], sem)
        for d in range(N_DEV)
    ]
    for cp in copies:
        cp.start()
    return copies


def _scan_groups(n_groups, a_ref, b_ref, out_ref, carry_ref, reverse):
    row = lax.broadcasted_iota(jnp.int32, (HALO, W), 0)

    def step(k, carry):
        g = (n_groups - 1 - k) if reverse else k
        rows = pl.ds(pl.multiple_of(g * HALO, HALO), HALO)
        a = a_ref[rows, :]
        b = b_ref[rows, :]
        for s in (1, 2, 4):
            if reverse:
                keep = row < HALO - s
                sh = HALO - s
            else:
                keep = row >= s
                sh = s
            a_sh = pltpu.roll(a, sh, axis=0)
            b_sh = pltpu.roll(b, sh, axis=0)
            b = jnp.where(keep, a * b_sh + b, b)
            a = jnp.where(keep, a * a_sh, a)
        h = b + a * carry
        out_ref[rows, :] = h
        edge = h[0:1, :] if reverse else h[HALO - 1:HALO, :]
        return jnp.broadcast_to(edge, (HALO, W))

    carry_ref[...] = lax.fori_loop(0, n_groups, step, carry_ref[...])


def _route_peers(me):
    x, y, c = me
    first = ((x + 1 - c) % 2, (y + c) % 2, c)
    second = ((x + c) % 2, (y + 1 - c) % 2, c)
    return first, second, (1 - x, 1 - y, c)


def _chip_gather_copies(block_hbm, out_hbm, send_sems, recv_sems):
    me = _position()
    first, second, diag = _route_peers(me)

    def copy(j, src, slot_of, to):
        return pltpu.make_async_remote_copy(
            src_ref=src, dst_ref=out_hbm.at[_linear(slot_of)], send_sem=send_sems.at[j], recv_sem=recv_sems.at[j],
            device_id=to, device_id_type=MESH)

    own_sends = [copy(0, block_hbm, me, first), copy(1, block_hbm, me, second)]
    forward = copy(2, out_hbm.at[_linear(first)], first, second)
    arrivals = [copy(0, block_hbm, first, first), copy(1, block_hbm, second, second), copy(2, block_hbm, diag, second)]
    return own_sends, forward, arrivals


def _mixer_fwd(x, modraw, adab, g1, wl, bl, bda, bdx, ba, bxb, ap, ws, gl, gc, avg, wpack, mlp_block):
    t_len = x.shape[0]
    tb = TB_MIX
    nb = t_len // tb

    def body(x_ref, modraw_ref, adab_ref, g1_ref, wl_ref, bl_ref, bda_ref, bdx_ref, ba_ref, bxb_ref, ap_ref,
             ws_ref, gl_ref, gc_ref, avg_ref, wpack_hbm, block_hbm, proj_ref, hl_ref, mixed_ref, kept_ref, wmlp_hbm,
             win_v, wout_v, sem, ulx_ext, cv_ext, hcar, a_s, b_s, send_sems, recv_sems, sib_send_sems, sib_recv_sems,
             local_sem):
        i = pl.program_id(0)
        x_pos, y_pos, c_pos = me = _position()
        sibling = (x_pos, y_pos, 1 - c_pos)
        own = pltpu.make_async_copy(block_hbm, wmlp_hbm.at[_linear(me)], local_sem)
        sends, forward, arrivals = _chip_gather_copies(block_hbm, wmlp_hbm, send_sems, recv_sems)

        def to_sibling(j, block_of, src=None):
            dst = wmlp_hbm.at[_linear(block_of)]
            return pltpu.make_async_remote_copy(
                src_ref=dst if src is None else src, dst_ref=dst, send_sem=sib_send_sems.at[j],
                recv_sem=sib_recv_sems.at[j], device_id=sibling, device_id_type=MESH)

        passes = [to_sibling(0, me, src=block_hbm)] + [to_sibling(1 + j, p) for j, p in enumerate(_route_peers(me))]

        @pl.when(i == 0)
        def _():
            own.start()
            for cp in sends:
                cp.start()
            passes[0].start()

        @pl.when(i == nb - 1)
        def _():
            arrivals[0].wait_recv()
            forward.start()
            passes[1].start()

        @pl.when(i == 0)
        def _():
            cps = _load_packed(wpack_hbm, OFF_WIN, ROWS_WIN, win_v, sem.at[0])
            cps += _load_packed(wpack_hbm, OFF_WOUT, ROWS_WOUT, wout_v, sem.at[1])
            ulx_ext[0:HALO, :] = jnp.zeros((HALO, W), F32)
            cv_ext[0:HALO, :] = jnp.zeros((HALO, W), F32)
            hcar[...] = jnp.zeros((HALO, W), F32)
            for cp in cps:
                cp.wait()

        mod = modraw_ref[...] + adab_ref[...]
        shift1, scale1, gate1 = mod[0:1], mod[1:2], mod[2:3]
        x = x_ref[...]
        r1 = lax.rsqrt(_rowmean(x * x) + EPS)
        h = (x * r1 * g1_ref[...]) * (1.0 + scale1) + shift1
        proj = _dot_nt(h.astype(BF16), win_v[...])
        proj_ref[...] = proj
        u_lx, u_ly, u_b, u_c, u_v = (proj[:, k * W:(k + 1) * W] for k in range(5))

        ulx_ext[HALO:HALO + tb, :] = u_lx
        xl = bl_ref[...] + wl_ref[CONV_L - 1:CONV_L, :] * u_lx
        for k in range(CONV_L - 1):
            xl = xl + wl_ref[k:k + 1, :] * ulx_ext[pl.ds(HALO - (CONV_L - 1) + k, tb), :]
        ulx_ext[0:HALO, :] = ulx_ext[tb:tb + HALO, :]
        xlb = xl.astype(BF16)
        r = _sigmoid(_dot(xlb, bda_ref[...]) + ba_ref[...])
        ig = _sigmoid(_dot(xlb, bdx_ref[...]) + bxb_ref[...])
        log_a = (-C_GATE) * r * _softplus(ap_ref[...])
        a = jnp.exp(log_a)
        mult = jnp.sqrt(_one_minus_sq(a, log_a))
        grow = i * tb + lax.broadcasted_iota(jnp.int32, (tb, W), 0)
        mult = jnp.where(grow == 0, 1.0, mult)
        a_s[...] = a
        b_s[...] = mult * (ig * xl)
        _scan_groups(tb // HALO, a_s, b_s, hl_ref, hcar, reverse=False)
        hl = hl_ref[...]
        ge, _ = _gelu(u_ly)
        p = ge * hl
        rp = lax.rsqrt(_group_mean(p * p, avg_ref[...]) + EPS)
        y_lru = p * rp * gl_ref[...]

        cv = u_c * u_v
        cv_ext[HALO:HALO + tb, :] = cv
        cc = ws_ref[CONV_S - 1:CONV_S, :] * cv
        for k in range(CONV_S - 1):
            cc = cc + ws_ref[k:k + 1, :] * cv_ext[pl.ds(HALO - (CONV_S - 1) + k, tb), :]
        cv_ext[0:HALO, :] = cv_ext[tb:tb + HALO, :]
        q = u_b * cc
        rq = lax.rsqrt(_group_mean(q * q, avg_ref[...]) + EPS)
        y_conv = q * rq * gc_ref[...]
        for k, kept in enumerate((xl, r, ig, rp, rq, cc)):
            kept_ref[:, k * W:(k + 1) * W] = kept

        mixed_ref[...] = (_dot(y_lru.astype(BF16), wout_v[0:W, :]) + _dot(y_conv.astype(BF16), wout_v[W:2 * W, :]))

        @pl.when(i == nb - 1)
        def _():
            arrivals[1].wait_recv()
            passes[2].start()
            arrivals[2].wait_recv()
            passes[3].start()
            for j, p in enumerate((sibling,) + _route_peers(sibling)):
                to_sibling(j, p).wait_recv()
            for cp in sends + [forward] + passes:
                cp.wait_send()
            own.wait()

    tok = lambda cols: pl.BlockSpec((tb, cols), lambda i: (i, 0))
    full = lambda a: pl.BlockSpec(a.shape, lambda i: (0,) * a.ndim)
    small = (modraw, adab, g1, wl, bl, bda, bdx, ba, bxb, ap, ws, gl, gc, avg)
    n_chips = len(CHIP_FLIPS)
    return pl.pallas_call(
        body,
        name="mixer_fwd",
        grid=(nb,),
        in_specs=[tok(D)] + [full(a) for a in small] + [ANY, ANY],
        out_specs=[tok(D_IN), tok(W), tok(D), tok(N_KEPT * W), ANY],
        out_shape=[jax.ShapeDtypeStruct((t_len, D_IN), F32), jax.ShapeDtypeStruct((t_len, W), F32),
                   jax.ShapeDtypeStruct((t_len, D), F32), jax.ShapeDtypeStruct((t_len, N_KEPT * W), F32),
                   jax.ShapeDtypeStruct((N_DEV,) + mlp_block.shape, BF16)],
        scratch_shapes=[pltpu.VMEM((D_IN, D), BF16), pltpu.VMEM((D, D), BF16), pltpu.SemaphoreType.DMA((2,)),
                        pltpu.VMEM((tb + HALO, W), F32), pltpu.VMEM((tb + HALO, W), F32), pltpu.VMEM((HALO, W), F32),
                        pltpu.VMEM((tb, W), F32), pltpu.VMEM((tb, W), F32),
                        pltpu.SemaphoreType.DMA((n_chips,)), pltpu.SemaphoreType.DMA((n_chips,)),
                        pltpu.SemaphoreType.DMA((4,)), pltpu.SemaphoreType.DMA((4,)), pltpu.SemaphoreType.DMA],
        compiler_params=pltpu.CompilerParams(dimension_semantics=("arbitrary",), vmem_limit_bytes=VMEM_LIMIT),
    )(x, *small, wpack, mlp_block)


def _mlp_fwd(x, mixed, tgt, modraw, adab, g2, gf, wpack):
    t_len = x.shape[0]
    tb = TB_MLP
    nb = t_len // tb

    def body(x_ref, mixed_ref, tgt_ref, modraw_ref, adab_ref, g2_ref, gf_ref, wpack_hbm,
             h2t_ref, f_ref, dx2_ref, dz_ref, vec_ref, loss_ref, w1t_v, w2_v, sem):
        i = pl.program_id(0)

        @pl.when(i == 0)
        def _():
            cps = _load_packed(wpack_hbm, OFF_W1T, ROWS_W1T, w1t_v, sem.at[0])
            cps += _load_packed(wpack_hbm, OFF_W2, ROWS_W2, w2_v, sem.at[1])
            vec_ref[...] = jnp.zeros(vec_ref.shape, F32)
            loss_ref[...] = jnp.zeros(loss_ref.shape, F32)
            for cp in cps:
                cp.wait()

        mod = modraw_ref[...] + adab_ref[...]
        gate1, shift2, scale2, gate2 = mod[2:3], mod[3:4], mod[4:5], mod[5:6]
        x1 = x_ref[...] + gate1 * mixed_ref[...]
        r2 = lax.rsqrt(_rowmean(x1 * x1) + EPS)
        h2 = (x1 * r2 * g2_ref[...]) * (1.0 + scale2) + shift2
        h2b = h2.astype(BF16)
        h2t_ref[...] = h2.T.astype(BF16)
        z = jnp.zeros((tb, D), F32)
        for j in range(N_DEV):
            cols = slice(j * FF_BLK, (j + 1) * FF_BLK)
            fj = _dot_nt(h2b, w1t_v[cols, :])
            f_ref[:, cols] = fj
            rf = jnp.maximum(fj, 0.0)
            z = z + _dot((rf * rf).astype(BF16), w2_v[cols, :])
        x2 = x1 + gate2 * z
        r3 = lax.rsqrt(_rowmean(x2 * x2) + EPS)
        xn3 = x2 * r3
        diff = xn3 * gf_ref[...] - tgt_ref[...]
        sq = _colsum(diff * diff)
        loss_ref[...] += jnp.broadcast_to(jnp.sum(sq, axis=1, keepdims=True) * (0.5 / D), loss_ref.shape)
        dy = diff * (1.0 / D)
        dyn = dy * gf_ref[...]
        dx2 = r3 * (dyn - xn3 * _rowmean(dyn * xn3))
        dx2_ref[...] = dx2
        dz_ref[...] = (gate2 * dx2).astype(BF16)
        vec_ref[0:1, :] += _colsum(dx2 * z)
        vec_ref[1:2, :] += _colsum(dy * xn3)

    tok = lambda cols: pl.BlockSpec((tb, cols), lambda i: (i, 0))
    tok_t = pl.BlockSpec((D, tb), lambda i: (0, i))
    full = lambda a: pl.BlockSpec(a.shape, lambda i: (0,) * a.ndim)
    small = (modraw, adab, g2, gf)
    return pl.pallas_call(
        body,
        name="mlp_fwd",
        grid=(nb,),
        in_specs=[tok(D), tok(D), tok(D)] + [full(a) for a in small] + [ANY],
        out_specs=[tok_t, tok(D_FF), tok(D), tok(D), pl.BlockSpec((8, D), lambda i: (0, 0)),
                   pl.BlockSpec((8, 128), lambda i: (0, 0))],
        out_shape=[jax.ShapeDtypeStruct((D, t_len), BF16), jax.ShapeDtypeStruct((t_len, D_FF), F32),
                   jax.ShapeDtypeStruct((t_len, D), F32), jax.ShapeDtypeStruct((t_len, D), BF16),
                   jax.ShapeDtypeStruct((8, D), F32), jax.ShapeDtypeStruct((8, 128), F32)],
        scratch_shapes=[pltpu.VMEM((D_FF, D), BF16), pltpu.VMEM((D_FF, D), BF16), pltpu.SemaphoreType.DMA((2,))],
        compiler_params=pltpu.CompilerParams(dimension_semantics=("arbitrary",), vmem_limit_bytes=VMEM_LIMIT),
    )(x, mixed, tgt, *small, wpack)


def _mlp_bwd_half(pos, h2t, f, dz, wpack, prior=None):
    t_len = dz.shape[0]
    tb = TB_MLPB
    nb = t_len // tb
    first = prior is None
    flip = 1 if first else 0

    def body(pos_ref, h2t_ref, f_ref, dz_ref, w1t_ref, w2_ref, *rest):
        if first:
            dh2_ref, dw1_ref, dw2_ref = rest
        else:
            dh2in_ref, _, dh2_ref, dw1_ref, dw2_ref = rest
        k = pl.program_id(0)
        t = pl.program_id(1)
        rows = pl.ds(pl.multiple_of(t * tb, tb), tb)
        w1t = w1t_ref[0]
        w2 = w2_ref[0]
        dz = dz_ref[...]
        rf = jnp.maximum(f_ref[...], 0.0)
        df = (_dot_nt(dz, w2) * (2.0 * rf)).astype(BF16)
        dh = _dot(df, w1t)
        g1 = _dot(h2t_ref[...], df)
        g2 = _dot_tn((rf * rf).astype(BF16), dz)

        @pl.when(t == 0)
        def _():
            dw2_ref[0] = g2
            dw1_ref[0] = g1

        @pl.when(t != 0)
        def _():
            dw2_ref[0] += g2
            dw1_ref[0] += g1

        @pl.when(k == 0)
        def _():
            dh2_ref[rows, :] = dh if first else dh2in_ref[...] + dh

        @pl.when(k != 0)
        def _():
            dh2_ref[rows, :] += dh

    blk = lambda k, pos: 2 * k + jnp.bitwise_xor(pos[0], flip)
    in_specs = [pl.BlockSpec((D, tb), lambda k, t, pos: (0, t)),
                pl.BlockSpec((tb, FF_BLK), lambda k, t, pos: (t, blk(k, pos))),
                pl.BlockSpec((tb, D), lambda k, t, pos: (t, 0)),
                pl.BlockSpec((1, ROWS_W1T, D), lambda k, t, pos: (blk(k, pos), OFF_W1T // ROWS_W1T, 0)),
                pl.BlockSpec((1, ROWS_W2, D), lambda k, t, pos: (blk(k, pos), OFF_W2 // ROWS_W2, 0))]
    grad_specs = [pl.BlockSpec((1, D, FF_BLK), lambda k, t, pos: (k, 0, 0)),
                  pl.BlockSpec((1, FF_BLK, D), lambda k, t, pos: (k, 0, 0))]
    out_specs = [pl.BlockSpec((t_len, D), lambda k, t, pos: (0, 0))] + grad_specs
    grad_shapes = [jax.ShapeDtypeStruct((4, D, FF_BLK), F32), jax.ShapeDtypeStruct((4, FF_BLK, D), F32)]
    out_shape = [jax.ShapeDtypeStruct((t_len, D), F32)] + grad_shapes
    args = [pos, h2t, f, dz, wpack, wpack]
    if not first:
        in_specs += [pl.BlockSpec((tb, D), lambda k, t, pos: (jnp.where(k == 0, t, nb - 1), 0)),
                     pl.BlockSpec(prior[1].shape, lambda k, t, pos: (0,) * prior[1].ndim)]
        args += list(prior)
    return pl.pallas_call(
        body,
        name="mlp_bwd_first" if first else "mlp_bwd_second",
        grid_spec=pltpu.PrefetchScalarGridSpec(num_scalar_prefetch=1, grid=(4, nb), in_specs=in_specs,
                                               out_specs=out_specs),
        out_shape=out_shape,
        compiler_params=pltpu.CompilerParams(dimension_semantics=("arbitrary", "arbitrary"),
                                             vmem_limit_bytes=VMEM_LIMIT),
    )(*args)


V_SHIFT1, V_SCALE1, V_GATE1, V_SHIFT2, V_SCALE2, V_G1, V_G2 = 0, 1, 2, 3, 4, 6, 7
V_BL_BA, V_BX_SP, V_GL_GC, V_WL01, V_WL23, V_WS01, V_WS2 = 8, 9, 10, 11, 12, 13, 14
V_ROWS = 16


def _chip_scatter_copies(srcs, dsts, send_sems, recv_sems):
    me = _position()
    copies = []
    for a, (src, dst) in enumerate(zip(srcs, dsts)):
        for j, k in enumerate(CHIP_FLIPS):
            peer = _flip(me, k)
            copies.append(pltpu.make_async_remote_copy(
                src_ref=src.at[2 * peer[0] + peer[1]], dst_ref=dst.at[j], send_sem=send_sems.at[len(CHIP_FLIPS) * a + j],
                recv_sem=recv_sems.at[len(CHIP_FLIPS) * a + j], device_id=peer, device_id_type=MESH))
    return copies


def _mixer_bwd(x, mixed, dh2, dx2, proj, hl, kept, modraw, adab, g1, g2, wl, bl, bda, bdx, ba, bxb, ap, ws, gl, gc, avg, wpack):
    t_len = x.shape[0]
    tb = TB_MIXB
    nb = t_len // tb
    hb = tb // HALO

    def body(x_ref, mixed_ref, dh2_ref, dx2_ref, proj_ref, projh_ref, hl_ref, hlh_ref, kept_ref,
             modraw_ref, adab_ref, g1_ref, g2_ref, wl_ref, bl_ref, bda_ref, bdx_ref, ba_ref, bxb_ref, ap_ref,
             ws_ref, gl_ref, gc_ref, avg_ref, wpack_hbm,
             gx_ref, vec_ref, hb_ref, dprojt_ref, dmixed_ref, ycatt_ref, xlt_ref, dgate_ref,
             win_v, wout_v, sem, ulx_ext, cv_ext, hl_ext, a_ext, dxl_ext, dcc_ext, dcar, an_s, g_s, dh_s):
        i = pl.program_id(0)
        blk = nb - 1 - i

        @pl.when(i == 0)
        def _():
            cps = _load_packed(wpack_hbm, OFF_WIN, ROWS_WIN, win_v, sem.at[0])
            cps += _load_packed(wpack_hbm, OFF_WOUT, ROWS_WOUT, wout_v, sem.at[1])
            vec_ref[...] = jnp.zeros(vec_ref.shape, F32)
            zero = jnp.zeros((HALO, W), F32)
            a_ext[tb:tb + HALO, :] = zero
            dxl_ext[tb:tb + HALO, :] = zero
            dcc_ext[tb:tb + HALO, :] = zero
            dcar[...] = zero
            for cp in cps:
                cp.wait()

        mod = modraw_ref[...] + adab_ref[...]
        shift1, scale1, gate1, scale2 = mod[0:1], mod[1:2], mod[2:3], mod[4:5]
        x = x_ref[...]
        mixed = mixed_ref[...]

        x1 = x + gate1 * mixed
        r2 = lax.rsqrt(_rowmean(x1 * x1) + EPS)
        xn2 = x1 * r2
        dh2 = dh2_ref[...]
        vec_ref[V_SHIFT2:V_SHIFT2 + 1, :] += _colsum(dh2)
        vec_ref[V_SCALE2:V_SCALE2 + 1, :] += _colsum(dh2 * xn2 * g2_ref[...])
        vec_ref[V_G2:V_G2 + 1, :] += _colsum(dh2 * (1.0 + scale2) * xn2)
        dxn2 = dh2 * g2_ref[...] * (1.0 + scale2)
        dx1 = dx2_ref[...] + r2 * (dxn2 - xn2 * _rowmean(dxn2 * xn2))
        vec_ref[V_GATE1:V_GATE1 + 1, :] += _colsum(dx1 * mixed)
        dmixed = (gate1 * dx1).astype(BF16)

        proj = proj_ref[...]
        u_lx, u_ly, u_b, u_c, u_v = (proj[:, k * W:(k + 1) * W] for k in range(5))
        has_prev = (blk > 0).astype(F32)
        projh = projh_ref[...]
        ulx_ext[0:HALO, :] = projh[:, 0:W] * has_prev
        ulx_ext[HALO:HALO + tb, :] = u_lx
        xl, r, ig, rp, rq, cc = (kept_ref[:, k * W:(k + 1) * W] for k in range(N_KEPT))
        sp = _softplus(ap_ref[...])
        log_a = (-C_GATE) * r * sp
        a = jnp.exp(log_a)
        mult_raw = jnp.sqrt(_one_minus_sq(a, log_a))
        first = (blk * tb + lax.broadcasted_iota(jnp.int32, (tb, W), 0)) == 0
        mult = jnp.where(first, 1.0, mult_raw)
        hl = hl_ref[...]
        ge, th = _gelu(u_ly)
        pn = ge * hl * rp
        cv = u_c * u_v
        cv_ext[0:HALO, :] = projh[:, 3 * W:4 * W] * projh[:, 4 * W:5 * W] * has_prev
        cv_ext[HALO:HALO + tb, :] = cv
        qn = u_b * cc * rq

        dmixed_ref[...] = dmixed
        ycatt_ref[0:W, :] = (pn * gl_ref[...]).T.astype(BF16)
        ycatt_ref[W:2 * W, :] = (qn * gc_ref[...]).T.astype(BF16)
        dyl = _dot_nt(dmixed, wout_v[0:W, :])
        dyc = _dot_nt(dmixed, wout_v[W:2 * W, :])

        dqn = dyc * gc_ref[...]
        dq = rq * (dqn - qn * _group_mean(dqn * qn, avg_ref[...]))
        du_b = dq * cc
        dcc = dq * u_b
        dcc_ext[0:tb, :] = dcc
        dcv = ws_ref[CONV_S - 1:CONV_S, :] * dcc
        for k in range(CONV_S - 1):
            dcv = dcv + ws_ref[k:k + 1, :] * dcc_ext[pl.ds(CONV_S - 1 - k, tb), :]
        dcc_ext[tb:tb + HALO, :] = dcc_ext[0:HALO, :]
        du_c = dcv * u_v
        du_v = dcv * u_c
        dws = [_colsum(dcc * cv_ext[pl.ds(HALO - (CONV_S - 1) + k, tb), :]) for k in range(CONV_S)]

        dpn = dyl * gl_ref[...]
        dp = rp * (dpn - pn * _group_mean(dpn * pn, avg_ref[...]))
        du_ly = dp * hl * _gelu_grad(u_ly, th)
        g_s[...] = dp * ge
        a_ext[0:tb, :] = a
        an_s[...] = a_ext[pl.ds(1, tb), :]
        _scan_groups(hb, an_s, g_s, dh_s, dcar, reverse=True)
        a_ext[tb:tb + HALO, :] = a_ext[0:HALO, :]
        dh = dh_s[...]
        hl_ext[0:HALO, :] = hlh_ref[...] * has_prev
        hl_ext[HALO:HALO + tb, :] = hl
        da = dh * hl_ext[pl.ds(HALO - 1, tb), :]
        dmult = dh * (ig * xl)
        dig = dh * (mult * xl)
        dxl = dh * (mult * ig)
        dlog = da * a - jnp.where(first, 0.0, dmult * (a * a) / mult_raw)
        dr = dlog * ((-C_GATE) * sp)
        dsp = _colsum(dlog * ((-C_GATE) * r))
        dga = dr * r * (1.0 - r)
        dgx = dig * ig * (1.0 - ig)
        dgab = dga.astype(BF16)
        dgxb = dgx.astype(BF16)
        xlt_ref[...] = xl.T.astype(BF16)
        dgate_ref[:, 0:W] = dgab
        dgate_ref[:, W:2 * W] = dgxb
        dxl = dxl + _dot_nt(dgab, bda_ref[...]) + _dot_nt(dgxb, bdx_ref[...])
        dxl_ext[0:tb, :] = dxl
        du_lx = wl_ref[CONV_L - 1:CONV_L, :] * dxl
        for k in range(CONV_L - 1):
            du_lx = du_lx + wl_ref[k:k + 1, :] * dxl_ext[pl.ds(CONV_L - 1 - k, tb), :]
        dxl_ext[tb:tb + HALO, :] = dxl_ext[0:HALO, :]
        dwl = [_colsum(dxl * ulx_ext[pl.ds(HALO - (CONV_L - 1) + k, tb), :]) for k in range(CONV_L)]

        cat = lambda u, v: jnp.concatenate([u, v], axis=1)
        vec_ref[V_BL_BA:V_BL_BA + 1, :] += cat(_colsum(dxl), _colsum(dga))
        vec_ref[V_BX_SP:V_BX_SP + 1, :] += cat(_colsum(dgx), dsp)
        vec_ref[V_GL_GC:V_GL_GC + 1, :] += cat(_colsum(dyl * pn), _colsum(dyc * qn))
        vec_ref[V_WL01:V_WL01 + 1, :] += cat(dwl[0], dwl[1])
        vec_ref[V_WL23:V_WL23 + 1, :] += cat(dwl[2], dwl[3])
        vec_ref[V_WS01:V_WS01 + 1, :] += cat(dws[0], dws[1])
        vec_ref[V_WS2:V_WS2 + 1, 0:W] += dws[2]

        r1 = lax.rsqrt(_rowmean(x * x) + EPS)
        xn1 = x * r1
        hb_ref[...] = ((xn1 * g1_ref[...]) * (1.0 + scale1) + shift1).astype(BF16)
        dh_in = jnp.zeros((tb, D), F32)
        for k, du in enumerate((du_lx, du_ly, du_b, du_c, du_v)):
            dprojt_ref[k * W:(k + 1) * W, :] = du.T.astype(BF16)
            dh_in = dh_in + _dot(du.astype(BF16), win_v[k * W:(k + 1) * W, :])
        vec_ref[V_SHIFT1:V_SHIFT1 + 1, :] += _colsum(dh_in)
        vec_ref[V_SCALE1:V_SCALE1 + 1, :] += _colsum(dh_in * xn1 * g1_ref[...])
        vec_ref[V_G1:V_G1 + 1, :] += _colsum(dh_in * (1.0 + scale1) * xn1)
        dxn1 = dh_in * g1_ref[...] * (1.0 + scale1)
        gx_ref[...] = dx1 + r1 * (dxn1 - xn1 * _rowmean(dxn1 * xn1))

    rev = lambda cols: pl.BlockSpec((tb, cols), lambda i: (nb - 1 - i, 0))
    rev_t = lambda rows: pl.BlockSpec((rows, tb), lambda i: (0, nb - 1 - i))
    halo = lambda cols: pl.BlockSpec((HALO, cols), lambda i: (jnp.maximum((nb - 1 - i) * hb - 1, 0), 0))
    full = lambda a: pl.BlockSpec(a.shape, lambda i: (0,) * a.ndim)
    small = (modraw, adab, g1, g2, wl, bl, bda, bdx, ba, bxb, ap, ws, gl, gc, avg)
    ext = pltpu.VMEM((tb + HALO, W), F32)
    return pl.pallas_call(
        body,
        name="mixer_bwd",
        grid=(nb,),
        in_specs=[rev(D), rev(D), rev(D), rev(D), rev(D_IN), halo(D_IN), rev(W), halo(W), rev(N_KEPT * W)]
        + [full(a) for a in small] + [ANY],
        out_specs=[rev(D), pl.BlockSpec((V_ROWS, D), lambda i: (0, 0)), rev(D), rev_t(D_IN), rev(D), rev_t(D),
                   rev_t(W), rev(2 * W)],
        out_shape=[jax.ShapeDtypeStruct((t_len, D), F32), jax.ShapeDtypeStruct((V_ROWS, D), F32),
                   jax.ShapeDtypeStruct((t_len, D), BF16), jax.ShapeDtypeStruct((D_IN, t_len), BF16),
                   jax.ShapeDtypeStruct((t_len, D), BF16), jax.ShapeDtypeStruct((D, t_len), BF16),
                   jax.ShapeDtypeStruct((W, t_len), BF16), jax.ShapeDtypeStruct((t_len, 2 * W), BF16)],
        scratch_shapes=[pltpu.VMEM((D_IN, D), BF16), pltpu.VMEM((D, D), BF16), pltpu.SemaphoreType.DMA((2,)),
                        ext, ext, ext, ext, ext, ext, pltpu.VMEM((HALO, W), F32),
                        pltpu.VMEM((tb, W), F32), pltpu.VMEM((tb, W), F32), pltpu.VMEM((tb, W), F32)],
        compiler_params=pltpu.CompilerParams(dimension_semantics=("arbitrary",), vmem_limit_bytes=VMEM_LIMIT),
    )(x, mixed, dh2, dx2, proj, proj, hl, hl, kept, *small, wpack)


def _matmul(name, a, b, tm=512):
    m, k = a.shape
    n = b.shape[1]

    def body(a_ref, b_ref, o_ref):
        o_ref[...] = _dot(a_ref[...], b_ref[...])

    return pl.pallas_call(
        body,
        name=name,
        grid=(m // tm,),
        in_specs=[pl.BlockSpec((tm, k), lambda i: (i, 0)), pl.BlockSpec((k, n), lambda i: (0, 0))],
        out_specs=pl.BlockSpec((tm, n), lambda i: (i, 0)),
        out_shape=jax.ShapeDtypeStruct((m, n), F32),
        compiler_params=pltpu.CompilerParams(dimension_semantics=("arbitrary",), vmem_limit_bytes=VMEM_LIMIT),
    )(a, b)


def _gate_wgrad(xl_t, dgate, avg):
    hd = W // 8

    def body(a_ref, b_ref, avg_ref, o_ref):
        full = _dot(a_ref[...], b_ref[...])
        row = lax.broadcasted_iota(jnp.int32, (W, hd), 0)
        col = lax.broadcasted_iota(jnp.int32, (W, hd), 1)
        fold = ((row & (hd - 1)) == col).astype(BF16)
        keep = avg_ref[...] != 0
        for g in range(2):
            m = jnp.where(keep, full[:, g * W:(g + 1) * W], 0.0)
            hi = m.astype(BF16)
            rest = m - hi.astype(F32)
            mid = rest.astype(BF16)
            lo = (rest - mid.astype(F32)).astype(BF16)
            o_ref[g] = _dot(hi, fold) + _dot(mid, fold) + _dot(lo, fold)

    return pl.pallas_call(
        body,
        name="wgrad_gate",
        in_specs=[WHOLE] * 3,
        out_specs=WHOLE,
        out_shape=jax.ShapeDtypeStruct((2, W, hd), F32),
        compiler_params=pltpu.CompilerParams(vmem_limit_bytes=VMEM_LIMIT),
    )(xl_t, dgate, avg)


def _block_diag(w):
    n, m, _ = w.shape
    eye = jnp.eye(n, dtype=w.dtype)
    return (w[:, :, None, :] * eye[:, None, :, None]).reshape(n * m, n * m)


def _pad_rows(a, rows):
    return jnp.pad(a, ((0, rows - a.shape[0]),) + ((0, 0),) * (a.ndim - 1))


def _position():
    return lax.axis_index("x"), lax.axis_index("y"), lax.axis_index("c")


def _linear(pos):
    return 4 * pos[0] + 2 * pos[1] + pos[2]


def _flip(pos, k):
    return tuple(1 - p if k & bit else p for p, bit in zip(pos, (4, 2, 1)))


def _exchange_all(make_copy, make_arrival):
    copies = [make_copy(k) for k in range(1, N_DEV)]
    for cp in copies:
        cp.start()
    for k in range(1, N_DEV):
        make_arrival(k).wait_recv()
    for cp in copies:
        cp.wait_send()


def _mod_exchange_steps(cols):
    def steps(msg_ref, adaw_ref, gath_ref, mod_ref, sendbuf, send_a, recv_a, send_b, recv_b):
        me = _position()
        me_lin = _linear(me)
        peers = range(1, N_DEV)
        m = msg_ref[...]
        row = lax.broadcasted_iota(jnp.int32, m.shape, 0)
        gath_ref[me_lin] = jnp.where(row == 0, m * _sigmoid(m), m)

        def gather_copy(k, src_lin):
            return pltpu.make_async_remote_copy(
                src_ref=gath_ref.at[src_lin], dst_ref=gath_ref.at[src_lin], send_sem=send_a.at[k - 1],
                recv_sem=recv_a.at[k - 1], device_id=_flip(me, k), device_id_type=MESH)

        first_round = [gather_copy(k, me_lin) for k in peers]
        for cp in first_round:
            cp.start()
        yield
        for k in peers:
            gather_copy(k, _linear(_flip(me, k))).wait_recv()

        sc_all = gath_ref[:, 0, :]
        scb = jnp.concatenate([sc_all, jnp.zeros_like(sc_all)], axis=0).astype(BF16)
        prod = _dot(scb, adaw_ref[...].astype(BF16))
        for b in range(N_DEV):
            sendbuf[b] = jnp.broadcast_to(prod[b:b + 1, :], (HALO, cols))
        mod_ref[me_lin] = sendbuf[me_lin]

        def row_copy(k, dst_lin):
            peer = _flip(me, k)
            return pltpu.make_async_remote_copy(
                src_ref=sendbuf.at[_linear(peer)], dst_ref=mod_ref.at[dst_lin], send_sem=send_b.at[k - 1],
                recv_sem=recv_b.at[k - 1], device_id=peer, device_id_type=MESH)

        second_round = [row_copy(k, me_lin) for k in peers]
        for cp in second_round:
            cp.start()
        yield
        for k in peers:
            row_copy(k, _linear(_flip(me, k))).wait_recv()
        for cp in first_round + second_round:
            cp.wait_send()

    return steps


def _gather_and_mod(msg, ada_w, block):
    rows, cols = block.shape
    mod_cols = ada_w.shape[1]
    mod_steps = _mod_exchange_steps(mod_cols)

    def body(msg_ref, adaw_ref, x_ref, gath_ref, mod_ref, out_ref, sendbuf, send_a, recv_a, send_b, recv_b,
             send_sems, recv_sems, sib_send_sems, sib_recv_sems, local_sem):
        x, y, c = _position()
        me, sibling = (x, y, c), (x, y, 1 - c)
        sends, forward, arrivals = _chip_gather_copies(x_ref, out_ref, send_sems, recv_sems)

        def to_sibling(j, block_of, src=None):
            dst = out_ref.at[_linear(block_of)]
            return pltpu.make_async_remote_copy(
                src_ref=dst if src is None else src, dst_ref=dst, send_sem=sib_send_sems.at[j],
                recv_sem=sib_recv_sems.at[j], device_id=sibling, device_id_type=MESH)

        mine = pltpu.make_async_copy(x_ref, out_ref.at[_linear(me)], local_sem)
        mine.start()
        passes = [to_sibling(0, me, src=x_ref)] + [to_sibling(1 + j, p) for j, p in enumerate(_route_peers(me))]
        small = mod_steps(msg_ref, adaw_ref, gath_ref, mod_ref, sendbuf, send_a, recv_a, send_b, recv_b)
        next(small)
        passes[0].start()
        for cp in sends:
            cp.start()
        next(small)
        arrivals[0].wait_recv()
        forward.start()
        passes[1].start()
        arrivals[1].wait_recv()
        passes[2].start()
        arrivals[2].wait_recv()
        passes[3].start()
        for _ in small:
            pass
        for j, p in enumerate((sibling,) + _route_peers(sibling)):
            to_sibling(j, p).wait_recv()
        for cp in sends + [forward] + passes:
            cp.wait_send()
        mine.wait()

    return pl.pallas_call(
        body,
        name="gather_and_mod",
        in_specs=[WHOLE, WHOLE, ANY],
        out_specs=[WHOLE, WHOLE, ANY],
        out_shape=[jax.ShapeDtypeStruct((N_DEV, HALO, D), F32), jax.ShapeDtypeStruct((N_DEV, HALO, mod_cols), F32),
                   jax.ShapeDtypeStruct((N_DEV, rows, cols), block.dtype)],
        scratch_shapes=[pltpu.VMEM((N_DEV, HALO, mod_cols), F32)] + [pltpu.SemaphoreType.DMA((N_DEV - 1,))] * 4
        + [pltpu.SemaphoreType.DMA((3,)), pltpu.SemaphoreType.DMA((3,)), pltpu.SemaphoreType.DMA((4,)),
           pltpu.SemaphoreType.DMA((4,)), pltpu.SemaphoreType.DMA],
        compiler_params=pltpu.CompilerParams(vmem_limit_bytes=VMEM_LIMIT),
    )(msg, ada_w, block)


HBM = pl.BlockSpec(memory_space=pltpu.HBM)
SEM = pl.BlockSpec(memory_space=pltpu.SEMAPHORE)
EFFECT = pltpu.SideEffectType.DATAFLOW_SIDE_EFFECTING


def _stage_copies(stage):
    return {"chips": (_chip_scatter_copies, len(CHIP_FLIPS), len(CHIP_FLIPS)), "sibling": (_sibling_copies, 4, 4)}[stage]


def _chips_start(which, chip_sums, stage="chips"):
    n = len(chip_sums)
    make_copies, per_array, slots = _stage_copies(stage)
    n_sems = per_array * n

    def body(*refs):
        srcs, dsts = refs[:n], refs[n:2 * n]
        send_sems, recv_sems = refs[2 * n:2 * n + 2]
        token = refs[-1]
        for cp in make_copies(srcs, dsts, send_sems, recv_sems):
            cp.start()
        token[...] = jnp.zeros(token.shape, token.dtype)

    landing = [jax.ShapeDtypeStruct((slots,) + s.shape[-2:], s.dtype) for s in chip_sums]
    outs = pl.pallas_call(
        body,
        name=which + "_" + stage + "_start",
        in_specs=[HBM] * (2 * n),
        out_specs=[SEM, SEM] + [HBM] * (2 * n) + [WHOLE],
        out_shape=[pltpu.SemaphoreType.DMA((n_sems,)), pltpu.SemaphoreType.DMA((n_sems,))]
        + [pltpu.HBM(s.shape, s.dtype) for s in chip_sums] + [pltpu.HBM(s.shape, s.dtype) for s in landing]
        + [jax.ShapeDtypeStruct((HALO, 128), F32)],
        input_output_aliases={i: 2 + i for i in range(2 * n)},
        compiler_params=pltpu.CompilerParams(has_side_effects=EFFECT),
    )(*[pltpu.with_memory_space_constraint(s, pltpu.HBM) for s in chip_sums],
      *[pltpu.with_memory_space_constraint(lax.empty(s.shape, s.dtype), pltpu.HBM) for s in landing])
    return outs[0], outs[1], outs[2:2 + n], outs[2 + n:2 + 2 * n], outs[-1]


def _chips_wait(which, send_sems, recv_sems, srcs, landed, after, stage="chips"):
    n = len(srcs)
    make_copies = _stage_copies(stage)[0]

    def body(*refs):
        src_refs, dst_refs = refs[:n], refs[n:2 * n]
        sends, recvs = refs[2 * n:2 * n + 2]
        copies = make_copies(src_refs, dst_refs, sends, recvs)
        for cp in copies:
            cp.wait_send()
        for cp in copies:
            cp.wait_recv()

    outs = pl.pallas_call(
        body,
        name=which + "_" + stage + "_wait",
        in_specs=[HBM] * (2 * n) + [SEM, SEM, ANY],
        out_specs=[HBM] * (2 * n),
        out_shape=[pltpu.HBM(s.shape, s.dtype) for s in list(srcs) + list(landed)],
        input_output_aliases={i: i for i in range(2 * n)},
        compiler_params=pltpu.CompilerParams(has_side_effects=EFFECT),
    )(*srcs, *landed, send_sems, recv_sems, after)
    return list(outs[:n]), list(outs[n:])


def _sibling_copies(srcs, dsts, send_sems, recv_sems):
    x, y, c = _position()
    copies = []
    for a, (src, dst) in enumerate(zip(srcs, dsts)):
        for k in range(4):
            copies.append(pltpu.make_async_remote_copy(
                src_ref=src.at[k, 1 - c] if len(src.shape) == 4 else src.at[k], dst_ref=dst.at[k],
                send_sem=send_sems.at[4 * a + k],
                recv_sem=recv_sems.at[4 * a + k], device_id=(x, y, 1 - c), device_id_type=MESH))
    return copies


def _row_block(rows):
    return min(rows, 512)


def _pair_sum(pos, mine, recv):
    _, cores, rows, cols = mine.shape
    rb = _row_block(rows)

    def body(pos_ref, mine_ref, recv_ref, out_ref):
        out_ref[0] = (mine_ref[0, 0] + recv_ref[0]).astype(BF16)

    other = lambda k, pos: jnp.bitwise_xor(pos[1], k + 1)
    core = lambda pos: pos[0] * (cores - 1)
    return pl.pallas_call(
        body,
        name="grad_pair_sum",
        grid_spec=pltpu.PrefetchScalarGridSpec(
            num_scalar_prefetch=1, grid=(3, rows // rb),
            in_specs=[pl.BlockSpec((1, 1, rb, cols), lambda k, r, pos: (other(k, pos), core(pos), r, 0)),
                      pl.BlockSpec((1, rb, cols), lambda k, r, pos: (other(k, pos), r, 0))],
            out_specs=pl.BlockSpec((1, rb, cols), lambda k, r, pos: (other(k, pos), r, 0))),
        out_shape=jax.ShapeDtypeStruct((4, rows, cols), BF16),
        compiler_params=pltpu.CompilerParams(dimension_semantics=("arbitrary", "arbitrary")),
    )(pos, mine, recv)


def _final_sum(pos, mine, recv, chips):
    _, cores, rows, cols = mine.shape
    rb = _row_block(rows)

    def body(pos_ref, mine_ref, recv_ref, chips_ref, out_ref):
        g = mine_ref[0, 0] + recv_ref[0]
        for j in range(3):
            g = g + chips_ref[j].astype(F32)
        out_ref[...] = g

    return pl.pallas_call(
        body,
        name="grad_final_sum",
        grid_spec=pltpu.PrefetchScalarGridSpec(
            num_scalar_prefetch=1, grid=(rows // rb,),
            in_specs=[pl.BlockSpec((1, 1, rb, cols), lambda r, pos: (pos[1], pos[0] * (cores - 1), r, 0)),
                      pl.BlockSpec((1, rb, cols), lambda r, pos: (pos[1], r, 0)),
                      pl.BlockSpec((3, rb, cols), lambda r, pos: (0, r, 0))],
            out_specs=pl.BlockSpec((rb, cols), lambda r, pos: (r, 0))),
        out_shape=jax.ShapeDtypeStruct((rows, cols), F32),
        compiler_params=pltpu.CompilerParams(dimension_semantics=("arbitrary",)),
    )(pos, mine, recv, chips)


LOSS_ROW = V_ROWS + 8
GB_BASE = LOSS_ROW + 8


def _route_mod_grad_steps(cols):
    def steps(gmod_ref, sct_ref, gadaw_ref, sendbuf, grecv, send_a, recv_a):
        me = _position()
        me_lin = _linear(me)
        gm = gmod_ref[...]
        for b in range(N_DEV):
            sendbuf[b] = jnp.broadcast_to(gm[b:b + 1, :], (HALO, cols))
        grecv[me_lin] = sendbuf[me_lin]

        def row_copy(k, dst_lin):
            peer = _flip(me, k)
            return pltpu.make_async_remote_copy(
                src_ref=sendbuf.at[_linear(peer)], dst_ref=grecv.at[dst_lin], send_sem=send_a.at[k - 1],
                recv_sem=recv_a.at[k - 1], device_id=peer, device_id_type=MESH)

        _exchange_all(lambda k: row_copy(k, me_lin), lambda k: row_copy(k, _linear(_flip(me, k))))
        g_all = grecv[:, 0, :]
        g_pad = jnp.concatenate([g_all, jnp.zeros((sct_ref.shape[1] - N_DEV, cols), F32)], axis=0).astype(BF16)
        gadaw_ref[...] = _dot(sct_ref[...], g_pad)
        return _colsum(g_all)

    return steps


def _small_grad_exchange(gmod8, sc_t, msg_vec, msg_gate, after):
    cols = gmod8.shape[1]
    vec_rows = GB_BASE + N_DEV
    route_steps = _route_mod_grad_steps(cols)

    def body(gmod_ref, sct_ref, vec_ref, gate_ref, after_ref, gadaw_ref, sumv_ref, sumg_ref,
             sendbuf, grecv, send_a, recv_a, myv, myg, sibv, sibg, chipv, chipg, sib_send, sib_recv, peer_send, peer_recv):
        x, y, c = me = _position()
        my_chip = 2 * x + y

        def swap(a, src, dst):
            return pltpu.make_async_remote_copy(
                src_ref=src, dst_ref=dst, send_sem=sib_send.at[a], recv_sem=sib_recv.at[a], device_id=(x, y, 1 - c),
                device_id_type=MESH)

        def chip_copy(a, buf, j, k, slot_chip):
            peer = _flip(me, k)
            return pltpu.make_async_remote_copy(
                src_ref=buf.at[slot_chip], dst_ref=buf.at[slot_chip], send_sem=peer_send.at[3 * a + j],
                recv_sem=peer_recv.at[3 * a + j], device_id=peer, device_id_type=MESH)

        def chip_stage(a, mine, theirs, buf):
            buf[my_chip] = (mine[...] + theirs[...]).astype(buf.dtype)
            sends = [chip_copy(a, buf, j, k, my_chip) for j, k in enumerate(CHIP_FLIPS)]
            for cp in sends:
                cp.start()
            return sends

        myg[...] = gate_ref[...]
        swap_g = swap(1, myg, sibg)
        swap_g.start()
        gb = route_steps(gmod_ref, sct_ref, gadaw_ref, sendbuf, grecv, send_a, recv_a)
        swap_g.wait_recv()
        sends = chip_stage(1, myg, sibg, chipg)

        myv[0:GB_BASE, :] = vec_ref[...]
        slot = lax.broadcasted_iota(jnp.int32, (N_DEV, D), 0) == _linear(me)
        gb_wide = jnp.concatenate([jnp.broadcast_to(gb, (N_DEV, cols)), jnp.zeros((N_DEV, D - cols), F32)], axis=1)
        myv[GB_BASE:vec_rows, :] = jnp.where(slot, gb_wide, 0.0)
        swap_v = swap(0, myv, sibv)
        swap_v.start()
        swap_v.wait_recv()
        sends += chip_stage(0, myv, sibv, chipv)

        for a, buf in enumerate((chipv, chipg)):
            for j, k in enumerate(CHIP_FLIPS):
                peer = _flip(me, k)
                chip_copy(a, buf, j, k, 2 * peer[0] + peer[1]).wait_recv()
        sumv_ref[...] = ((chipv[0] + chipv[1]) + chipv[2]) + chipv[3]
        gate_sum = lambda k: chipg[k].astype(F32)
        sumg_ref[...] = ((gate_sum(0) + gate_sum(1)) + gate_sum(2)) + gate_sum(3)
        for cp in [swap_g, swap_v] + sends:
            cp.wait_send()

    vshape, gshape = (vec_rows, D), msg_gate.shape
    return pl.pallas_call(
        body,
        name="small_grad_exchange",
        in_specs=[WHOLE] * 5,
        out_specs=[WHOLE] * 3,
        out_shape=[jax.ShapeDtypeStruct((D, cols), F32), jax.ShapeDtypeStruct(vshape, F32),
                   jax.ShapeDtypeStruct(gshape, F32)],
        scratch_shapes=[pltpu.VMEM((N_DEV, HALO, cols), F32), pltpu.VMEM((N_DEV, HALO, cols), F32),
                        pltpu.SemaphoreType.DMA((N_DEV - 1,)), pltpu.SemaphoreType.DMA((N_DEV - 1,)),
                        pltpu.VMEM(vshape, F32), pltpu.VMEM(gshape, F32), pltpu.VMEM(vshape, F32),
                        pltpu.VMEM(gshape, F32), pltpu.VMEM((4,) + vshape, F32), pltpu.VMEM((4,) + gshape, BF16),
                        pltpu.SemaphoreType.DMA((2,)), pltpu.SemaphoreType.DMA((2,)),
                        pltpu.SemaphoreType.DMA((2 * len(CHIP_FLIPS),)), pltpu.SemaphoreType.DMA((2 * len(CHIP_FLIPS),))],
        compiler_params=pltpu.CompilerParams(vmem_limit_bytes=VMEM_LIMIT),
    )(gmod8, sc_t, msg_vec, msg_gate, after)


def _adamw_math(w, g, m, v):
    m = ADAM_B1 * m + (1.0 - ADAM_B1) * g
    v = ADAM_B2 * v + (1.0 - ADAM_B2) * (g * g)
    m_hat = m / (1.0 - ADAM_B1 ** ADAM_STEP)
    v_hat = v / (1.0 - ADAM_B2 ** ADAM_STEP)
    delta = -ADAM_LR * (m_hat / (jnp.sqrt(v_hat) + ADAM_EPS) + ADAM_WD * w)
    return delta, m, v


def _adamw(name, w, g, m, v):
    rows, cols = w.shape
    rb = 256 if rows % 256 == 0 else rows

    def body(w_ref, g_ref, m_ref, v_ref, d_ref, mo_ref, vo_ref):
        d_ref[...], mo_ref[...], vo_ref[...] = _adamw_math(w_ref[...], g_ref[...], m_ref[...], v_ref[...])

    spec = pl.BlockSpec((rb, cols), lambda r: (r, 0))
    return pl.pallas_call(
        body,
        name="adamw_" + name,
        grid=(rows // rb,),
        in_specs=[spec] * 4,
        out_specs=[spec] * 3,
        out_shape=[jax.ShapeDtypeStruct((rows, cols), F32)] * 3,
        compiler_params=pltpu.CompilerParams(dimension_semantics=("arbitrary",)),
    )(w, g, m, v)


def _update(pos, sum_jobs, plain_jobs, after):
    rb = 256
    jobs = [("sum", j) for j in sum_jobs] + [("plain", j) for j in plain_jobs]
    offs, total = [], 0
    for _, j in jobs:
        offs.append(total)
        total += j[-1].shape[0] // rb
    n_in = sum(len(j) for _, j in jobs)

    def body(pos_ref, *refs):
        ins, outs = refs[:n_in], refs[n_in + 1:]
        s = pl.program_id(0)
        i_in = i_out = 0
        for (kind, j), off in zip(jobs, offs):
            steps = j[-1].shape[0] // rb
            j_in = ins[i_in:i_in + len(j)]
            i_in += len(j)
            j_out = outs[i_out:i_out + (4 if kind == "sum" else 3)]
            i_out += len(j_out)

            @pl.when((s >= off) & (s < off + steps))
            def _(kind=kind, j_in=j_in, j_out=j_out):
                if kind == "sum":
                    mine_ref, recv_ref, chips_ref, w_ref, m_ref, v_ref = j_in
                    g = mine_ref[0, 0] + recv_ref[0]
                    for q in range(len(CHIP_FLIPS)):
                        g = g + chips_ref[q].astype(F32)
                    j_out[0][...] = g
                    rest = j_out[1:]
                else:
                    g_ref, w_ref, m_ref, v_ref = j_in
                    g = g_ref[...]
                    rest = j_out
                rest[0][...], rest[1][...], rest[2][...] = _adamw_math(w_ref[...], g, m_ref[...], v_ref[...])

    in_specs, out_specs, out_shape, args = [], [], [], []
    for (kind, j), off in zip(jobs, offs):
        rows, cols = j[-1].shape
        steps = rows // rb
        blk = lambda s, off=off, steps=steps: jnp.clip(s - off, 0, steps - 1)
        flat = pl.BlockSpec((rb, cols), lambda s, pos, blk=blk: (blk(s), 0))
        if kind == "sum":
            in_specs += [pl.BlockSpec((1, 1, rb, cols), lambda s, pos, blk=blk: (pos[1], 0, blk(s), 0)),
                         pl.BlockSpec((1, rb, cols), lambda s, pos, blk=blk: (pos[1], blk(s), 0)),
                         pl.BlockSpec((len(CHIP_FLIPS), rb, cols), lambda s, pos, blk=blk: (0, blk(s), 0))]
            in_specs += [flat] * 3
        else:
            in_specs += [flat] * 4
        n_res = 4 if kind == "sum" else 3
        out_specs += [flat] * n_res
        out_shape += [jax.ShapeDtypeStruct((rows, cols), F32)] * n_res
        args += list(j)
    in_specs += [pl.BlockSpec(after.shape, lambda s, pos: (0,) * after.ndim)]
    outs = pl.pallas_call(
        body,
        name="update",
        grid_spec=pltpu.PrefetchScalarGridSpec(
            num_scalar_prefetch=1, grid=(total,), in_specs=in_specs, out_specs=out_specs),
        out_shape=out_shape,
        compiler_params=pltpu.CompilerParams(dimension_semantics=("arbitrary",), vmem_limit_bytes=VMEM_LIMIT),
    )(pos, *args, after)
    sums = [tuple(outs[4 * i:4 * i + 4]) for i in range(len(sum_jobs))]
    base = 4 * len(sum_jobs)
    plains = [tuple(outs[base + 3 * i:base + 3 * i + 3]) for i in range(len(plain_jobs))]
    return sums, plains


def _adamw_small(ws, gs, ms, vs, sigmoid_scaled):
    n = len(ws)

    def body(*refs):
        w_refs, g_refs, m_refs, v_refs = (refs[i * n:(i + 1) * n] for i in range(4))
        outs = refs[4 * n:]
        for i in range(n):
            w = w_refs[i][...]
            g = g_refs[i][...]
            if sigmoid_scaled[i]:
                g = g * _sigmoid(w)
            delta, m, v = _adamw_math(w, g, m_refs[i][...], v_refs[i][...])
            outs[4 * i][...] = g
            outs[4 * i + 1][...] = delta
            outs[4 * i + 2][...] = m
            outs[4 * i + 3][...] = v

    shapes = [jax.ShapeDtypeStruct(w.shape, F32) for w in ws for _ in range(4)]
    outs = pl.pallas_call(
        body,
        name="adamw_small",
        in_specs=[WHOLE] * (4 * n),
        out_specs=[WHOLE] * (4 * n),
        out_shape=shapes,
    )(*ws, *gs, *ms, *vs)
    return [outs[4 * i:4 * i + 4] for i in range(n)]


_WEIGHT_NAMES = ("ada_w", "ada_b", "norm1_g", "w_in", "lru_conv_w", "lru_conv_b", "gate_a_w", "gate_a_b", "gate_x_w",
                 "gate_x_b", "a_param", "short_conv_w", "lru_out_g", "conv_out_g", "w_out", "norm2_g", "w_mlp1",
                 "w_mlp2", "final_g")


def kernel(x, c, ada_w, ada_b, norm1_g, w_in, lru_conv_w, lru_conv_b, gate_a_w, gate_a_b, gate_x_w, gate_x_b, a_param, short_conv_w, lru_out_g, conv_out_g, w_out, norm2_g, w_mlp1, w_mlp2, final_g, loss_target, m_ada_w, m_ada_b, m_norm1_g, m_w_in, m_lru_conv_w, m_lru_conv_b, m_gate_a_w, m_gate_a_b, m_gate_x_w, m_gate_x_b, m_a_param, m_short_conv_w, m_lru_out_g, m_conv_out_g, m_w_out, m_norm2_g, m_w_mlp1, m_w_mlp2, m_final_g, v_ada_w, v_ada_b, v_norm1_g, v_w_in, v_lru_conv_w, v_lru_conv_b, v_gate_a_w, v_gate_a_b, v_gate_x_w, v_gate_x_b, v_a_param, v_short_conv_w, v_lru_out_g, v_conv_out_g, v_w_out, v_norm2_g, v_w_mlp1, v_w_mlp2, v_final_g):
    given = dict(locals())
    weights = {n: given[n] for n in _WEIGHT_NAMES}
    xi, yi, ci = _position()
    me_lin = _linear((xi, yi, ci))
    hd = W // N_DEV

    mixer_block = jnp.concatenate([w_out[0], w_in[0].T], axis=0).astype(BF16)
    mlp_block = jnp.concatenate([w_mlp1[0].T, w_mlp2[0]], axis=0).astype(BF16)

    msg = (jnp.pad(c, ((0, HALO - 1), (0, 0)))
           + jnp.pad(lru_conv_w[0], ((1, HALO - 1 - CONV_L), (0, D - hd)))
           + jnp.pad(short_conv_w[0], ((1 + CONV_L, 0), (0, D - hd))))
    gath, mod_all, wmix = _gather_and_mod(msg, ada_w[0], mixer_block)
    sc_all = gath[:, 0, :]
    wl = jnp.transpose(gath[:, 1:1 + CONV_L, :hd], (1, 0, 2)).reshape(CONV_L, W)
    ws = jnp.transpose(gath[:, 1 + CONV_L:HALO, :hd], (1, 0, 2)).reshape(CONV_S, W)
    modraw = _pad_rows(mod_all[:, 0, :].reshape(6, D), HALO)
    adab = _pad_rows(ada_b.reshape(6, D), HALO)

    x2d, tgt = x[0], loss_target[0]
    gf = final_g.reshape(1, D)
    bda = _block_diag(gate_a_w[0]).astype(BF16)
    bdx = _block_diag(gate_x_w[0]).astype(BF16)
    avg = jnp.asarray(np.kron(np.eye(8, dtype=np.float32), np.full((W // 8, W // 8), 8.0 / W, np.float32)), BF16)
    wl8 = _pad_rows(wl, HALO)
    ws8 = _pad_rows(ws, HALO)
    mixer_small = (wl8, lru_conv_b, bda, bdx, gate_a_b, gate_x_b, a_param, ws8, lru_out_g, conv_out_g, avg)
    proj, hl, mixed, kept, wmlp = _mixer_fwd(x2d, modraw, adab, norm1_g, *mixer_small, wmix, mlp_block)
    h2t, f, dx2, dz, vec2, loss8 = _mlp_fwd(x2d, mixed, tgt, modraw, adab, norm2_g, gf, wmlp)
    pos = jnp.stack([ci, 2 * xi + yi]).astype(jnp.int32)
    by_dest = lambda g: g.reshape((4, 2, -1) + g.shape[-1:])
    dh2_first, *for_sibling = _mlp_bwd_half(pos, h2t, f, dz, wmlp)
    sib_send, sib_recv, sib_thru, sib_land, token = _chips_start("mlp", for_sibling, stage="sibling")
    dh2, dw1, dw2 = _mlp_bwd_half(pos, h2t, f, dz, wmlp, prior=(dh2_first, token))
    done = dh2[0:HALO, 0:128] + dw1[0, 0:HALO, 0:128] + dw2[0, 0:HALO, 0:128]
    _, mlp_sib = _chips_wait("mlp", sib_send, sib_recv, sib_thru, sib_land, done, stage="sibling")
    mlp_parts = [dw1[:, None], dw2[:, None]]
    mlp_sums = [_pair_sum(pos, p, r) for p, r in zip(mlp_parts, mlp_sib)]
    mlp_send, mlp_recv, mlp_thru, mlp_land, token = _chips_start("mlp", mlp_sums)
    modraw_after = modraw + jnp.tile(token, (1, D // token.shape[1]))
    gx, vec, hb, dproj_t, dmixed, ycat_t, xl_t, dgate = _mixer_bwd(
        x2d, mixed, dh2, dx2, proj, hl, kept, modraw_after, adab, norm1_g, norm2_g, *mixer_small, wmix)
    dwint = _matmul("wgrad_in", dproj_t, hb)
    dwout = _matmul("wgrad_out", ycat_t, dmixed)
    gate_blocks = _gate_wgrad(xl_t, dgate, avg)
    msg_gate = gate_blocks.reshape(W, 128)
    done = dwint[0:HALO, 0:128] + dwout[0:HALO, 0:128] + gate_blocks[0, 0:HALO, :].sum() + gx[0:HALO, 0:128]
    _, mlp_chips = _chips_wait("mlp", mlp_send, mlp_recv, mlp_thru, mlp_land, done)
    mix_parts = [by_dest(dwout), by_dest(dwint)]
    gmod8 = (jnp.pad(vec[0:5], ((0, 1), (0, 0))) + jnp.pad(vec2[0:1], ((5, 0), (0, 0)))).reshape(N_DEV, 6 * D // N_DEV)
    sc_t = jnp.pad(sc_all.T, ((0, 0), (0, 128 - N_DEV))).astype(BF16)
    loss_rows = jnp.pad(loss8[0:1], ((0, HALO - 1), (0, D - loss8.shape[1])))
    msg_vec = jnp.concatenate([vec, vec2, loss_rows], axis=0)
    sib_send, sib_recv, sib_thru, sib_land, token = _chips_start("mixer", mix_parts, stage="sibling")
    g_adaw, sum_vec, sum_gate = _small_grad_exchange(gmod8, sc_t, msg_vec, msg_gate, token)
    mix_parts, mix_sib = _chips_wait("mixer", sib_send, sib_recv, sib_thru, sib_land, sum_vec[0:HALO, 0:128],
                                     stage="sibling")
    mix_sums = [_pair_sum(pos, p, r) for p, r in zip(mix_parts, mix_sib)]
    state = lambda n: (weights[n][0], given["m_" + n][0], given["v_" + n][0])
    mlp_jobs = [(p, r, q, *state(n)) for p, r, q, n in zip(mlp_parts, mlp_sib, mlp_chips, ("w_mlp1", "w_mlp2"))]
    mix_send, mix_recv, mix_thru, mix_land, token = _chips_start("mixer", mix_sums)
    mlp_done, (adaw_done,) = _update(pos, mlp_jobs, [(g_adaw, *state("ada_w"))], token)
    loss = sum_vec[LOSS_ROW, 0]
    sum_gate = sum_gate.reshape(2, W, W // 8)
    lo, hi = slice(0, W), slice(W, 2 * W)
    wl_full = sum_vec[V_WL01:V_WL23 + 1].reshape(CONV_L, W)
    ws_full = sum_vec[V_WS01:V_WS2 + 1].reshape(CONV_S + 1, W)[:CONV_S]
    row = lambda r, cols: sum_vec[r:r + 1, cols]
    small_grads = {
        "ada_b": sum_vec[GB_BASE:GB_BASE + N_DEV, :6 * D // N_DEV].reshape(1, 6 * D),
        "norm1_g": row(V_G1, slice(0, D)),
        "lru_conv_w": lax.dynamic_slice(wl_full, (0, me_lin * hd), (CONV_L, hd)),
        "lru_conv_b": row(V_BL_BA, lo),
        "gate_a_w": sum_gate[0],
        "gate_a_b": row(V_BL_BA, hi),
        "gate_x_w": sum_gate[1],
        "gate_x_b": row(V_BX_SP, lo),
        "a_param": row(V_BX_SP, hi),
        "short_conv_w": lax.dynamic_slice(ws_full, (0, me_lin * hd), (CONV_S, hd)),
        "lru_out_g": row(V_GL_GC, lo),
        "conv_out_g": row(V_GL_GC, hi),
        "norm2_g": row(V_G2, slice(0, D)),
        "final_g": sum_vec[V_ROWS + 1:V_ROWS + 2, :],
    }
    names = list(small_grads)
    as2d = lambda a, n: a.reshape(small_grads[n].shape)
    small = _adamw_small([as2d(weights[n], n) for n in names], [small_grads[n] for n in names],
                         [as2d(given["m_" + n], n) for n in names], [as2d(given["v_" + n], n) for n in names],
                         [n == "a_param" for n in names])
    result = {n: tuple(o.reshape(weights[n].shape) for o in outs) for n, outs in zip(names, small)}

    done = (small[0][1][:, 0:128] + mlp_done[0][2][0:HALO, 0:128] + mlp_done[1][2][0:HALO, 0:128]
            + adaw_done[1][0:HALO, 0:128])
    _, mix_chips = _chips_wait("mixer", mix_send, mix_recv, mix_thru, mix_land, done)
    g_wout, g_wint = (_final_sum(pos, p, r, q) for p, r, q in zip(mix_parts, mix_sib, mix_chips))
    for n, g in (("w_in", g_wint.T), ("w_out", g_wout)):
        w, m, v = state(n)
        result[n] = (g[None],) + tuple(o[None] for o in _adamw(n, w, g, m, v))
    result["w_mlp1"], result["w_mlp2"] = (tuple(o[None] for o in done) for done in mlp_done)
    result["ada_w"] = (g_adaw[None],) + tuple(o[None] for o in adaw_done)

    return (loss, gx[None], *[result[n][0] for n in _WEIGHT_NAMES], *[result[n][1] for n in _WEIGHT_NAMES],
            *[result[n][2] for n in _WEIGHT_NAMES], *[result[n][3] for n in _WEIGHT_NAMES])
```

```python
import jax
import jax.numpy as jnp
import numpy as np
from jax import lax
from jax.experimental import pallas as pl
from jax.experimental.pallas import tpu as pltpu

F32 = jnp.float32
BF16 = jnp.bfloat16
MESH = pl.DeviceIdType.MESH

N_DEV = 8
D = 1024
W = 512
D_IN = 5 * W
D_FF = 4096
FF_BLK = D_FF // N_DEV
EPS = 1e-6
C_GATE = 8.0
CONV_L = 4
CONV_S = 3
HALO = 8

ROWS_W1T, ROWS_W2, ROWS_WOUT, ROWS_WIN = FF_BLK, FF_BLK, D // N_DEV, D_IN // N_DEV
OFF_WOUT = 0
OFF_WIN = OFF_WOUT + ROWS_WOUT
MIX_ROWS = OFF_WIN + ROWS_WIN
OFF_W1T = 0
OFF_W2 = OFF_W1T + ROWS_W1T
MLP_ROWS = OFF_W2 + ROWS_W2
CHIP_FLIPS = (4, 2, 6)
N_KEPT = 6

ADAM_LR = 0.001
ADAM_B1 = 0.9
ADAM_B2 = 0.999
ADAM_EPS = 1e-08
ADAM_WD = 0.01
ADAM_STEP = 10

VMEM_LIMIT = 56 * 1024 * 1024

TB_MIX = 256
TB_MIXB = 256
TB_MLP = 256
TB_MLPB = 512

ANY = pl.BlockSpec(memory_space=pl.ANY)
WHOLE = pl.BlockSpec(memory_space=pltpu.VMEM)


def _dot(a, b):
    return jnp.dot(a, b, preferred_element_type=F32)


def _dot_nt(a, b):
    return lax.dot_general(a, b, (((1,), (1,)), ((), ())), preferred_element_type=F32)


def _dot_tn(a, b):
    return lax.dot_general(a, b, (((0,), (0,)), ((), ())), preferred_element_type=F32)


def _sigmoid(v):
    return 1.0 / (1.0 + jnp.exp(-v))


def _softplus(v):
    t = jnp.exp(-jnp.abs(v))
    small = t * (1.0 - t * (0.5 - t * (1.0 / 3.0)))
    return jnp.maximum(v, 0.0) + jnp.where(t < 1e-2, small, jnp.log(1.0 + t))


def _one_minus_sq(a, log_a):
    return -jnp.tanh(log_a) * (a * a + 1.0)


_GELU_K = 0.7978845608028654
_GELU_C = 0.044715


def _gelu(u):
    th = jnp.tanh(_GELU_K * (u + _GELU_C * u * u * u))
    return 0.5 * u * (1.0 + th), th


def _gelu_grad(u, th):
    return 0.5 * (1.0 + th) + 0.5 * u * (1.0 - th * th) * _GELU_K * (1.0 + 3.0 * _GELU_C * u * u)


def _group_mean(v, avg):
    hi = v.astype(BF16)
    lo = (v - hi.astype(F32)).astype(BF16)
    return _dot(hi, avg) + _dot(lo, avg)


def _colsum(v):
    return jnp.sum(v, axis=0, keepdims=True)


def _rowmean(v):
    return jnp.mean(v, axis=-1, keepdims=True)


def _load_packed(wpack_hbm, off, rows, dst, sem):
    copies = [
        pltpu.make_async_copy(wpack_hbm.at[d, pl.ds(off, rows), :], dst.at[pl.ds(d * rows, rows), :], sem)
        for d in range(N_DEV)
    ]
    for cp in copies:
        cp.start()
    return copies


def _scan_groups(n_groups, a_ref, b_ref, out_ref, carry_ref, reverse):
    row = lax.broadcasted_iota(jnp.int32, (HALO, W), 0)

    def step(k, carry):
        g = (n_groups - 1 - k) if reverse else k
        rows = pl.ds(pl.multiple_of(g * HALO, HALO), HALO)
        a = a_ref[rows, :]
        b = b_ref[rows, :]
        for s in (1, 2, 4):
            if reverse:
                keep = row < HALO - s
                sh = HALO - s
            else:
                keep = row >= s
                sh = s
            a_sh = pltpu.roll(a, sh, axis=0)
            b_sh = pltpu.roll(b, sh, axis=0)
            b = jnp.where(keep, a * b_sh + b, b)
            a = jnp.where(keep, a * a_sh, a)
        h = b + a * carry
        out_ref[rows, :] = h
        edge = h[0:1, :] if reverse else h[HALO - 1:HALO, :]
        return jnp.broadcast_to(edge, (HALO, W))

    carry_ref[...] = lax.fori_loop(0, n_groups, step, carry_ref[...])


def _route_peers(me):
    x, y, c = me
    first = ((x + 1 - c) % 2, (y + c) % 2, c)
    second = ((x + c) % 2, (y + 1 - c) % 2, c)
    return first, second, (1 - x, 1 - y, c)


def _chip_gather_copies(block_hbm, out_hbm, send_sems, recv_sems):
    me = _position()
    first, second, diag = _route_peers(me)

    def copy(j, src, slot_of, to):
        return pltpu.make_async_remote_copy(
            src_ref=src, dst_ref=out_hbm.at[_linear(slot_of)], send_sem=send_sems.at[j], recv_sem=recv_sems.at[j],
            device_id=to, device_id_type=MESH)

    own_sends = [copy(0, block_hbm, me, first), copy(1, block_hbm, me, second)]
    forward = copy(2, out_hbm.at[_linear(first)], first, second)
    arrivals = [copy(0, block_hbm, first, first), copy(1, block_hbm, second, second), copy(2, block_hbm, diag, second)]
    return own_sends, forward, arrivals


def _mixer_fwd(x, modraw, adab, g1, wl, bl, bda, bdx, ba, bxb, ap, ws, gl, gc, avg, wpack, mlp_block):
    t_len = x.shape[0]
    tb = TB_MIX
    nb = t_len // tb

    def body(x_ref, modraw_ref, adab_ref, g1_ref, wl_ref, bl_ref, bda_ref, bdx_ref, ba_ref, bxb_ref, ap_ref,
             ws_ref, gl_ref, gc_ref, avg_ref, wpack_hbm, block_hbm, proj_ref, hl_ref, mixed_ref, kept_ref, wmlp_hbm,
             win_v, wout_v, sem, ulx_ext, cv_ext, hcar, a_s, b_s, send_sems, recv_sems, sib_send_sems, sib_recv_sems,
             local_sem):
        i = pl.program_id(0)
        x_pos, y_pos, c_pos = me = _position()
        sibling = (x_pos, y_pos, 1 - c_pos)
        own = pltpu.make_async_copy(block_hbm, wmlp_hbm.at[_linear(me)], local_sem)
        sends, forward, arrivals = _chip_gather_copies(block_hbm, wmlp_hbm, send_sems, recv_sems)

        def to_sibling(j, block_of, src=None):
            dst = wmlp_hbm.at[_linear(block_of)]
            return pltpu.make_async_remote_copy(
                src_ref=dst if src is None else src, dst_ref=dst, send_sem=sib_send_sems.at[j],
                recv_sem=sib_recv_sems.at[j], device_id=sibling, device_id_type=MESH)

        passes = [to_sibling(0, me, src=block_hbm)] + [to_sibling(1 + j, p) for j, p in enumerate(_route_peers(me))]

        @pl.when(i == 0)
        def _():
            own.start()
            for cp in sends:
                cp.start()
            passes[0].start()

        @pl.when(i == nb - 1)
        def _():
            arrivals[0].wait_recv()
            forward.start()
            passes[1].start()

        @pl.when(i == 0)
        def _():
            cps = _load_packed(wpack_hbm, OFF_WIN, ROWS_WIN, win_v, sem.at[0])
            cps += _load_packed(wpack_hbm, OFF_WOUT, ROWS_WOUT, wout_v, sem.at[1])
            ulx_ext[0:HALO, :] = jnp.zeros((HALO, W), F32)
            cv_ext[0:HALO, :] = jnp.zeros((HALO, W), F32)
            hcar[...] = jnp.zeros((HALO, W), F32)
            for cp in cps:
                cp.wait()

        mod = modraw_ref[...] + adab_ref[...]
        shift1, scale1, gate1 = mod[0:1], mod[1:2], mod[2:3]
        x = x_ref[...]
        r1 = lax.rsqrt(_rowmean(x * x) + EPS)
        h = (x * r1 * g1_ref[...]) * (1.0 + scale1) + shift1
        proj = _dot_nt(h.astype(BF16), win_v[...])
        proj_ref[...] = proj
        u_lx, u_ly, u_b, u_c, u_v = (proj[:, k * W:(k + 1) * W] for k in range(5))

        ulx_ext[HALO:HALO + tb, :] = u_lx
        xl = bl_ref[...] + wl_ref[CONV_L - 1:CONV_L, :] * u_lx
        for k in range(CONV_L - 1):
            xl = xl + wl_ref[k:k + 1, :] * ulx_ext[pl.ds(HALO - (CONV_L - 1) + k, tb), :]
        ulx_ext[0:HALO, :] = ulx_ext[tb:tb + HALO, :]
        xlb = xl.astype(BF16)
        r = _sigmoid(_dot(xlb, bda_ref[...]) + ba_ref[...])
        ig = _sigmoid(_dot(xlb, bdx_ref[...]) + bxb_ref[...])
        log_a = (-C_GATE) * r * _softplus(ap_ref[...])
        a = jnp.exp(log_a)
        mult = jnp.sqrt(_one_minus_sq(a, log_a))
        grow = i * tb + lax.broadcasted_iota(jnp.int32, (tb, W), 0)
        mult = jnp.where(grow == 0, 1.0, mult)
        a_s[...] = a
        b_s[...] = mult * (ig * xl)
        _scan_groups(tb // HALO, a_s, b_s, hl_ref, hcar, reverse=False)
        hl = hl_ref[...]
        ge, _ = _gelu(u_ly)
        p = ge * hl
        rp = lax.rsqrt(_group_mean(p * p, avg_ref[...]) + EPS)
        y_lru = p * rp * gl_ref[...]

        cv = u_c * u_v
        cv_ext[HALO:HALO + tb, :] = cv
        cc = ws_ref[CONV_S - 1:CONV_S, :] * cv
        for k in range(CONV_S - 1):
            cc = cc + ws_ref[k:k + 1, :] * cv_ext[pl.ds(HALO - (CONV_S - 1) + k, tb), :]
        cv_ext[0:HALO, :] = cv_ext[tb:tb + HALO, :]
        q = u_b * cc
        rq = lax.rsqrt(_group_mean(q * q, avg_ref[...]) + EPS)
        y_conv = q * rq * gc_ref[...]
        for k, kept in enumerate((xl, r, ig, rp, rq, cc)):
            kept_ref[:, k * W:(k + 1) * W] = kept

        mixed_ref[...] = (_dot(y_lru.astype(BF16), wout_v[0:W, :]) + _dot(y_conv.astype(BF16), wout_v[W:2 * W, :]))

        @pl.when(i == nb - 1)
        def _():
            arrivals[1].wait_recv()
            passes[2].start()
            arrivals[2].wait_recv()
            passes[3].start()
            for j, p in enumerate((sibling,) + _route_peers(sibling)):
                to_sibling(j, p).wait_recv()
            for cp in sends + [forward] + passes:
                cp.wait_send()
            own.wait()

    tok = lambda cols: pl.BlockSpec((tb, cols), lambda i: (i, 0))
    full = lambda a: pl.BlockSpec(a.shape, lambda i: (0,) * a.ndim)
    small = (modraw, adab, g1, wl, bl, bda, bdx, ba, bxb, ap, ws, gl, gc, avg)
    n_chips = len(CHIP_FLIPS)
    return pl.pallas_call(
        body,
        name="mixer_fwd",
        grid=(nb,),
        in_specs=[tok(D)] + [full(a) for a in small] + [ANY, ANY],
        out_specs=[tok(D_IN), tok(W), tok(D), tok(N_KEPT * W), ANY],
        out_shape=[jax.ShapeDtypeStruct((t_len, D_IN), F32), jax.ShapeDtypeStruct((t_len, W), F32),
                   jax.ShapeDtypeStruct((t_len, D), F32), jax.ShapeDtypeStruct((t_len, N_KEPT * W), F32),
                   jax.ShapeDtypeStruct((N_DEV,) + mlp_block.shape, BF16)],
        scratch_shapes=[pltpu.VMEM((D_IN, D), BF16), pltpu.VMEM((D, D), BF16), pltpu.SemaphoreType.DMA((2,)),
                        pltpu.VMEM((tb + HALO, W), F32), pltpu.VMEM((tb + HALO, W), F32), pltpu.VMEM((HALO, W), F32),
                        pltpu.VMEM((tb, W), F32), pltpu.VMEM((tb, W), F32),
                        pltpu.SemaphoreType.DMA((n_chips,)), pltpu.SemaphoreType.DMA((n_chips,)),
                        pltpu.SemaphoreType.DMA((4,)), pltpu.SemaphoreType.DMA((4,)), pltpu.SemaphoreType.DMA],
        compiler_params=pltpu.CompilerParams(dimension_semantics=("arbitrary",), vmem_limit_bytes=VMEM_LIMIT),
    )(x, *small, wpack, mlp_block)


def _mlp_fwd(x, mixed, tgt, modraw, adab, g2, gf, wpack):
    t_len = x.shape[0]
    tb = TB_MLP
    nb = t_len // tb

    def body(x_ref, mixed_ref, tgt_ref, modraw_ref, adab_ref, g2_ref, gf_ref, wpack_hbm,
             h2t_ref, f_ref, dx2_ref, dz_ref, vec_ref, loss_ref, w1t_v, w2_v, sem):
        i = pl.program_id(0)

        @pl.when(i == 0)
        def _():
            cps = _load_packed(wpack_hbm, OFF_W1T, ROWS_W1T, w1t_v, sem.at[0])
            cps += _load_packed(wpack_hbm, OFF_W2, ROWS_W2, w2_v, sem.at[1])
            vec_ref[...] = jnp.zeros(vec_ref.shape, F32)
            loss_ref[...] = jnp.zeros(loss_ref.shape, F32)
            for cp in cps:
                cp.wait()

        mod = modraw_ref[...] + adab_ref[...]
        gate1, shift2, scale2, gate2 = mod[2:3], mod[3:4], mod[4:5], mod[5:6]
        x1 = x_ref[...] + gate1 * mixed_ref[...]
        r2 = lax.rsqrt(_rowmean(x1 * x1) + EPS)
        h2 = (x1 * r2 * g2_ref[...]) * (1.0 + scale2) + shift2
        h2b = h2.astype(BF16)
        h2t_ref[...] = h2.T.astype(BF16)
        z = jnp.zeros((tb, D), F32)
        for j in range(N_DEV):
            cols = slice(j * FF_BLK, (j + 1) * FF_BLK)
            fj = _dot_nt(h2b, w1t_v[cols, :])
            f_ref[:, cols] = fj
            rf = jnp.maximum(fj, 0.0)
            z = z + _dot((rf * rf).astype(BF16), w2_v[cols, :])
        x2 = x1 + gate2 * z
        r3 = lax.rsqrt(_rowmean(x2 * x2) + EPS)
        xn3 = x2 * r3
        diff = xn3 * gf_ref[...] - tgt_ref[...]
        sq = _colsum(diff * diff)
        loss_ref[...] += jnp.broadcast_to(jnp.sum(sq, axis=1, keepdims=True) * (0.5 / D), loss_ref.shape)
        dy = diff * (1.0 / D)
        dyn = dy * gf_ref[...]
        dx2 = r3 * (dyn - xn3 * _rowmean(dyn * xn3))
        dx2_ref[...] = dx2
        dz_ref[...] = (gate2 * dx2).astype(BF16)
        vec_ref[0:1, :] += _colsum(dx2 * z)
        vec_ref[1:2, :] += _colsum(dy * xn3)

    tok = lambda cols: pl.BlockSpec((tb, cols), lambda i: (i, 0))
    tok_t = pl.BlockSpec((D, tb), lambda i: (0, i))
    full = lambda a: pl.BlockSpec(a.shape, lambda i: (0,) * a.ndim)
    small = (modraw, adab, g2, gf)
    return pl.pallas_call(
        body,
        name="mlp_fwd",
        grid=(nb,),
        in_specs=[tok(D), tok(D), tok(D)] + [full(a) for a in small] + [ANY],
        out_specs=[tok_t, tok(D_FF), tok(D), tok(D), pl.BlockSpec((8, D), lambda i: (0, 0)),
                   pl.BlockSpec((8, 128), lambda i: (0, 0))],
        out_shape=[jax.ShapeDtypeStruct((D, t_len), BF16), jax.ShapeDtypeStruct((t_len, D_FF), F32),
                   jax.ShapeDtypeStruct((t_len, D), F32), jax.ShapeDtypeStruct((t_len, D), BF16),
                   jax.ShapeDtypeStruct((8, D), F32), jax.ShapeDtypeStruct((8, 128), F32)],
        scratch_shapes=[pltpu.VMEM((D_FF, D), BF16), pltpu.VMEM((D_FF, D), BF16), pltpu.SemaphoreType.DMA((2,))],
        compiler_params=pltpu.CompilerParams(dimension_semantics=("arbitrary",), vmem_limit_bytes=VMEM_LIMIT),
    )(x, mixed, tgt, *small, wpack)


def _mlp_bwd_half(pos, h2t, f, dz, wpack, prior=None):
    t_len = dz.shape[0]
    tb = TB_MLPB
    nb = t_len // tb
    first = prior is None
    flip = 1 if first else 0

    def body(pos_ref, h2t_ref, f_ref, dz_ref, w1t_ref, w2_ref, *rest):
        if first:
            dh2_ref, dw1_ref, dw2_ref = rest
        else:
            dh2in_ref, _, dh2_ref, dw1_ref, dw2_ref = rest
        k = pl.program_id(0)
        t = pl.program_id(1)
        rows = pl.ds(pl.multiple_of(t * tb, tb), tb)
        w1t = w1t_ref[0]
        w2 = w2_ref[0]
        dz = dz_ref[...]
        rf = jnp.maximum(f_ref[...], 0.0)
        df = (_dot_nt(dz, w2) * (2.0 * rf)).astype(BF16)
        dh = _dot(df, w1t)
        g1 = _dot(h2t_ref[...], df)
        g2 = _dot_tn((rf * rf).astype(BF16), dz)

        @pl.when(t == 0)
        def _():
            dw2_ref[0] = g2
            dw1_ref[0] = g1

        @pl.when(t != 0)
        def _():
            dw2_ref[0] += g2
            dw1_ref[0] += g1

        @pl.when(k == 0)
        def _():
            dh2_ref[rows, :] = dh if first else dh2in_ref[...] + dh

        @pl.when(k != 0)
        def _():
            dh2_ref[rows, :] += dh

    blk = lambda k, pos: 2 * k + jnp.bitwise_xor(pos[0], flip)
    in_specs = [pl.BlockSpec((D, tb), lambda k, t, pos: (0, t)),
                pl.BlockSpec((tb, FF_BLK), lambda k, t, pos: (t, blk(k, pos))),
                pl.BlockSpec((tb, D), lambda k, t, pos: (t, 0)),
                pl.BlockSpec((1, ROWS_W1T, D), lambda k, t, pos: (blk(k, pos), OFF_W1T // ROWS_W1T, 0)),
                pl.BlockSpec((1, ROWS_W2, D), lambda k, t, pos: (blk(k, pos), OFF_W2 // ROWS_W2, 0))]
    grad_specs = [pl.BlockSpec((1, D, FF_BLK), lambda k, t, pos: (k, 0, 0)),
                  pl.BlockSpec((1, FF_BLK, D), lambda k, t, pos: (k, 0, 0))]
    out_specs = [pl.BlockSpec((t_len, D), lambda k, t, pos: (0, 0))] + grad_specs
    grad_shapes = [jax.ShapeDtypeStruct((4, D, FF_BLK), F32), jax.ShapeDtypeStruct((4, FF_BLK, D), F32)]
    out_shape = [jax.ShapeDtypeStruct((t_len, D), F32)] + grad_shapes
    args = [pos, h2t, f, dz, wpack, wpack]
    if not first:
        in_specs += [pl.BlockSpec((tb, D), lambda k, t, pos: (jnp.where(k == 0, t, nb - 1), 0)),
                     pl.BlockSpec(prior[1].shape, lambda k, t, pos: (0,) * prior[1].ndim)]
        args += list(prior)
    return pl.pallas_call(
        body,
        name="mlp_bwd_first" if first else "mlp_bwd_second",
        grid_spec=pltpu.PrefetchScalarGridSpec(num_scalar_prefetch=1, grid=(4, nb), in_specs=in_specs,
                                               out_specs=out_specs),
        out_shape=out_shape,
        compiler_params=pltpu.CompilerParams(dimension_semantics=("arbitrary", "arbitrary"),
                                             vmem_limit_bytes=VMEM_LIMIT),
    )(*args)


V_SHIFT1, V_SCALE1, V_GATE1, V_SHIFT2, V_SCALE2, V_G1, V_G2 = 0, 1, 2, 3, 4, 6, 7
V_BL_BA, V_BX_SP, V_GL_GC, V_WL01, V_WL23, V_WS01, V_WS2 = 8, 9, 10, 11, 12, 13, 14
V_ROWS = 16


def _chip_scatter_copies(srcs, dsts, send_sems, recv_sems):
    me = _position()
    copies = []
    for a, (src, dst) in enumerate(zip(srcs, dsts)):
        for j, k in enumerate(CHIP_FLIPS):
            peer = _flip(me, k)
            copies.append(pltpu.make_async_remote_copy(
                src_ref=src.at[2 * peer[0] + peer[1]], dst_ref=dst.at[j], send_sem=send_sems.at[len(CHIP_FLIPS) * a + j],
                recv_sem=recv_sems.at[len(CHIP_FLIPS) * a + j], device_id=peer, device_id_type=MESH))
    return copies


def _mixer_bwd(x, mixed, dh2, dx2, proj, hl, kept, modraw, adab, g1, g2, wl, bl, bda, bdx, ba, bxb, ap, ws, gl, gc, avg, wpack):
    t_len = x.shape[0]
    tb = TB_MIXB
    nb = t_len // tb
    hb = tb // HALO

    def body(x_ref, mixed_ref, dh2_ref, dx2_ref, proj_ref, projh_ref, hl_ref, hlh_ref, kept_ref,
             modraw_ref, adab_ref, g1_ref, g2_ref, wl_ref, bl_ref, bda_ref, bdx_ref, ba_ref, bxb_ref, ap_ref,
             ws_ref, gl_ref, gc_ref, avg_ref, wpack_hbm,
             gx_ref, vec_ref, hb_ref, dprojt_ref, dmixed_ref, ycatt_ref, xlt_ref, dgate_ref,
             win_v, wout_v, sem, ulx_ext, cv_ext, hl_ext, a_ext, dxl_ext, dcc_ext, dcar, an_s, g_s, dh_s):
        i = pl.program_id(0)
        blk = nb - 1 - i

        @pl.when(i == 0)
        def _():
            cps = _load_packed(wpack_hbm, OFF_WIN, ROWS_WIN, win_v, sem.at[0])
            cps += _load_packed(wpack_hbm, OFF_WOUT, ROWS_WOUT, wout_v, sem.at[1])
            vec_ref[...] = jnp.zeros(vec_ref.shape, F32)
            zero = jnp.zeros((HALO, W), F32)
            a_ext[tb:tb + HALO, :] = zero
            dxl_ext[tb:tb + HALO, :] = zero
            dcc_ext[tb:tb + HALO, :] = zero
            dcar[...] = zero
            for cp in cps:
                cp.wait()

        mod = modraw_ref[...] + adab_ref[...]
        shift1, scale1, gate1, scale2 = mod[0:1], mod[1:2], mod[2:3], mod[4:5]
        x = x_ref[...]
        mixed = mixed_ref[...]

        x1 = x + gate1 * mixed
        r2 = lax.rsqrt(_rowmean(x1 * x1) + EPS)
        xn2 = x1 * r2
        dh2 = dh2_ref[...]
        vec_ref[V_SHIFT2:V_SHIFT2 + 1, :] += _colsum(dh2)
        vec_ref[V_SCALE2:V_SCALE2 + 1, :] += _colsum(dh2 * xn2 * g2_ref[...])
        vec_ref[V_G2:V_G2 + 1, :] += _colsum(dh2 * (1.0 + scale2) * xn2)
        dxn2 = dh2 * g2_ref[...] * (1.0 + scale2)
        dx1 = dx2_ref[...] + r2 * (dxn2 - xn2 * _rowmean(dxn2 * xn2))
        vec_ref[V_GATE1:V_GATE1 + 1, :] += _colsum(dx1 * mixed)
        dmixed = (gate1 * dx1).astype(BF16)

        proj = proj_ref[...]
        u_lx, u_ly, u_b, u_c, u_v = (proj[:, k * W:(k + 1) * W] for k in range(5))
        has_prev = (blk > 0).astype(F32)
        projh = projh_ref[...]
        ulx_ext[0:HALO, :] = projh[:, 0:W] * has_prev
        ulx_ext[HALO:HALO + tb, :] = u_lx
        xl, r, ig, rp, rq, cc = (kept_ref[:, k * W:(k + 1) * W] for k in range(N_KEPT))
        sp = _softplus(ap_ref[...])
        log_a = (-C_GATE) * r * sp
        a = jnp.exp(log_a)
        mult_raw = jnp.sqrt(_one_minus_sq(a, log_a))
        first = (blk * tb + lax.broadcasted_iota(jnp.int32, (tb, W), 0)) == 0
        mult = jnp.where(first, 1.0, mult_raw)
        hl = hl_ref[...]
        ge, th = _gelu(u_ly)
        pn = ge * hl * rp
        cv = u_c * u_v
        cv_ext[0:HALO, :] = projh[:, 3 * W:4 * W] * projh[:, 4 * W:5 * W] * has_prev
        cv_ext[HALO:HALO + tb, :] = cv
        qn = u_b * cc * rq

        dmixed_ref[...] = dmixed
        ycatt_ref[0:W, :] = (pn * gl_ref[...]).T.astype(BF16)
        ycatt_ref[W:2 * W, :] = (qn * gc_ref[...]).T.astype(BF16)
        dyl = _dot_nt(dmixed, wout_v[0:W, :])
        dyc = _dot_nt(dmixed, wout_v[W:2 * W, :])

        dqn = dyc * gc_ref[...]
        dq = rq * (dqn - qn * _group_mean(dqn * qn, avg_ref[...]))
        du_b = dq * cc
        dcc = dq * u_b
        dcc_ext[0:tb, :] = dcc
        dcv = ws_ref[CONV_S - 1:CONV_S, :] * dcc
        for k in range(CONV_S - 1):
            dcv = dcv + ws_ref[k:k + 1, :] * dcc_ext[pl.ds(CONV_S - 1 - k, tb), :]
        dcc_ext[tb:tb + HALO, :] = dcc_ext[0:HALO, :]
        du_c = dcv * u_v
        du_v = dcv * u_c
        dws = [_colsum(dcc * cv_ext[pl.ds(HALO - (CONV_S - 1) + k, tb), :]) for k in range(CONV_S)]

        dpn = dyl * gl_ref[...]
        dp = rp * (dpn - pn * _group_mean(dpn * pn, avg_ref[...]))
        du_ly = dp * hl * _gelu_grad(u_ly, th)
        g_s[...] = dp * ge
        a_ext[0:tb, :] = a
        an_s[...] = a_ext[pl.ds(1, tb), :]
        _scan_groups(hb, an_s, g_s, dh_s, dcar, reverse=True)
        a_ext[tb:tb + HALO, :] = a_ext[0:HALO, :]
        dh = dh_s[...]
        hl_ext[0:HALO, :] = hlh_ref[...] * has_prev
        hl_ext[HALO:HALO + tb, :] = hl
        da = dh * hl_ext[pl.ds(HALO - 1, tb), :]
        dmult = dh * (ig * xl)
        dig = dh * (mult * xl)
        dxl = dh * (mult * ig)
        dlog = da * a - jnp.where(first, 0.0, dmult * (a * a) / mult_raw)
        dr = dlog * ((-C_GATE) * sp)
        dsp = _colsum(dlog * ((-C_GATE) * r))
        dga = dr * r * (1.0 - r)
        dgx = dig * ig * (1.0 - ig)
        dgab = dga.astype(BF16)
        dgxb = dgx.astype(BF16)
        xlt_ref[...] = xl.T.astype(BF16)
        dgate_ref[:, 0:W] = dgab
        dgate_ref[:, W:2 * W] = dgxb
        dxl = dxl + _dot_nt(dgab, bda_ref[...]) + _dot_nt(dgxb, bdx_ref[...])
        dxl_ext[0:tb, :] = dxl
        du_lx = wl_ref[CONV_L - 1:CONV_L, :] * dxl
        for k in range(CONV_L - 1):
            du_lx = du_lx + wl_ref[k:k + 1, :] * dxl_ext[pl.ds(CONV_L - 1 - k, tb), :]
        dxl_ext[tb:tb + HALO, :] = dxl_ext[0:HALO, :]
        dwl = [_colsum(dxl * ulx_ext[pl.ds(HALO - (CONV_L - 1) + k, tb), :]) for k in range(CONV_L)]

        cat = lambda u, v: jnp.concatenate([u, v], axis=1)
        vec_ref[V_BL_BA:V_BL_BA + 1, :] += cat(_colsum(dxl), _colsum(dga))
        vec_ref[V_BX_SP:V_BX_SP + 1, :] += cat(_colsum(dgx), dsp)
        vec_ref[V_GL_GC:V_GL_GC + 1, :] += cat(_colsum(dyl * pn), _colsum(dyc * qn))
        vec_ref[V_WL01:V_WL01 + 1, :] += cat(dwl[0], dwl[1])
        vec_ref[V_WL23:V_WL23 + 1, :] += cat(dwl[2], dwl[3])
        vec_ref[V_WS01:V_WS01 + 1, :] += cat(dws[0], dws[1])
        vec_ref[V_WS2:V_WS2 + 1, 0:W] += dws[2]

        r1 = lax.rsqrt(_rowmean(x * x) + EPS)
        xn1 = x * r1
        hb_ref[...] = ((xn1 * g1_ref[...]) * (1.0 + scale1) + shift1).astype(BF16)
        dh_in = jnp.zeros((tb, D), F32)
        for k, du in enumerate((du_lx, du_ly, du_b, du_c, du_v)):
            dprojt_ref[k * W:(k + 1) * W, :] = du.T.astype(BF16)
            dh_in = dh_in + _dot(du.astype(BF16), win_v[k * W:(k + 1) * W, :])
        vec_ref[V_SHIFT1:V_SHIFT1 + 1, :] += _colsum(dh_in)
        vec_ref[V_SCALE1:V_SCALE1 + 1, :] += _colsum(dh_in * xn1 * g1_ref[...])
        vec_ref[V_G1:V_G1 + 1, :] += _colsum(dh_in * (1.0 + scale1) * xn1)
        dxn1 = dh_in * g1_ref[...] * (1.0 + scale1)
        gx_ref[...] = dx1 + r1 * (dxn1 - xn1 * _rowmean(dxn1 * xn1))

    rev = lambda cols: pl.BlockSpec((tb, cols), lambda i: (nb - 1 - i, 0))
    rev_t = lambda rows: pl.BlockSpec((rows, tb), lambda i: (0, nb - 1 - i))
    halo = lambda cols: pl.BlockSpec((HALO, cols), lambda i: (jnp.maximum((nb - 1 - i) * hb - 1, 0), 0))
    full = lambda a: pl.BlockSpec(a.shape, lambda i: (0,) * a.ndim)
    small = (modraw, adab, g1, g2, wl, bl, bda, bdx, ba, bxb, ap, ws, gl, gc, avg)
    ext = pltpu.VMEM((tb + HALO, W), F32)
    return pl.pallas_call(
        body,
        name="mixer_bwd",
        grid=(nb,),
        in_specs=[rev(D), rev(D), rev(D), rev(D), rev(D_IN), halo(D_IN), rev(W), halo(W), rev(N_KEPT * W)]
        + [full(a) for a in small] + [ANY],
        out_specs=[rev(D), pl.BlockSpec((V_ROWS, D), lambda i: (0, 0)), rev(D), rev_t(D_IN), rev(D), rev_t(D),
                   rev_t(W), rev(2 * W)],
        out_shape=[jax.ShapeDtypeStruct((t_len, D), F32), jax.ShapeDtypeStruct((V_ROWS, D), F32),
                   jax.ShapeDtypeStruct((t_len, D), BF16), jax.ShapeDtypeStruct((D_IN, t_len), BF16),
                   jax.ShapeDtypeStruct((t_len, D), BF16), jax.ShapeDtypeStruct((D, t_len), BF16),
                   jax.ShapeDtypeStruct((W, t_len), BF16), jax.ShapeDtypeStruct((t_len, 2 * W), BF16)],
        scratch_shapes=[pltpu.VMEM((D_IN, D), BF16), pltpu.VMEM((D, D), BF16), pltpu.SemaphoreType.DMA((2,)),
                        ext, ext, ext, ext, ext, ext, pltpu.VMEM((HALO, W), F32),
                        pltpu.VMEM((tb, W), F32), pltpu.VMEM((tb, W), F32), pltpu.VMEM((tb, W), F32)],
        compiler_params=pltpu.CompilerParams(dimension_semantics=("arbitrary",), vmem_limit_bytes=VMEM_LIMIT),
    )(x, mixed, dh2, dx2, proj, proj, hl, hl, kept, *small, wpack)


def _matmul(name, a, b, tm=512):
    m, k = a.shape
    n = b.shape[1]

    def body(a_ref, b_ref, o_ref):
        o_ref[...] = _dot(a_ref[...], b_ref[...])

    return pl.pallas_call(
        body,
        name=name,
        grid=(m // tm,),
        in_specs=[pl.BlockSpec((tm, k), lambda i: (i, 0)), pl.BlockSpec((k, n), lambda i: (0, 0))],
        out_specs=pl.BlockSpec((tm, n), lambda i: (i, 0)),
        out_shape=jax.ShapeDtypeStruct((m, n), F32),
        compiler_params=pltpu.CompilerParams(dimension_semantics=("arbitrary",), vmem_limit_bytes=VMEM_LIMIT),
    )(a, b)


def _gate_wgrad(xl_t, dgate, avg):
    hd = W // 8

    def body(a_ref, b_ref, avg_ref, o_ref):
        full = _dot(a_ref[...], b_ref[...])
        row = lax.broadcasted_iota(jnp.int32, (W, hd), 0)
        col = lax.broadcasted_iota(jnp.int32, (W, hd), 1)
        fold = ((row & (hd - 1)) == col).astype(BF16)
        keep = avg_ref[...] != 0
        for g in range(2):
            m = jnp.where(keep, full[:, g * W:(g + 1) * W], 0.0)
            hi = m.astype(BF16)
            rest = m - hi.astype(F32)
            mid = rest.astype(BF16)
            lo = (rest - mid.astype(F32)).astype(BF16)
            o_ref[g] = _dot(hi, fold) + _dot(mid, fold) + _dot(lo, fold)

    return pl.pallas_call(
        body,
        name="wgrad_gate",
        in_specs=[WHOLE] * 3,
        out_specs=WHOLE,
        out_shape=jax.ShapeDtypeStruct((2, W, hd), F32),
        compiler_params=pltpu.CompilerParams(vmem_limit_bytes=VMEM_LIMIT),
    )(xl_t, dgate, avg)


def _block_diag(w):
    n, m, _ = w.shape
    eye = jnp.eye(n, dtype=w.dtype)
    return (w[:, :, None, :] * eye[:, None, :, None]).reshape(n * m, n * m)


def _pad_rows(a, rows):
    return jnp.pad(a, ((0, rows - a.shape[0]),) + ((0, 0),) * (a.ndim - 1))


def _position():
    return lax.axis_index("x"), lax.axis_index("y"), lax.axis_index("c")


def _linear(pos):
    return 4 * pos[0] + 2 * pos[1] + pos[2]


def _flip(pos, k):
    return tuple(1 - p if k & bit else p for p, bit in zip(pos, (4, 2, 1)))


def _exchange_all(make_copy, make_arrival):
    copies = [make_copy(k) for k in range(1, N_DEV)]
    for cp in copies:
        cp.start()
    for k in range(1, N_DEV):
        make_arrival(k).wait_recv()
    for cp in copies:
        cp.wait_send()


def _mod_exchange_steps(cols):
    def steps(msg_ref, adaw_ref, gath_ref, mod_ref, sendbuf, send_a, recv_a, send_b, recv_b):
        me = _position()
        me_lin = _linear(me)
        peers = range(1, N_DEV)
        m = msg_ref[...]
        row = lax.broadcasted_iota(jnp.int32, m.shape, 0)
        gath_ref[me_lin] = jnp.where(row == 0, m * _sigmoid(m), m)

        def gather_copy(k, src_lin):
            return pltpu.make_async_remote_copy(
                src_ref=gath_ref.at[src_lin], dst_ref=gath_ref.at[src_lin], send_sem=send_a.at[k - 1],
                recv_sem=recv_a.at[k - 1], device_id=_flip(me, k), device_id_type=MESH)

        first_round = [gather_copy(k, me_lin) for k in peers]
        for cp in first_round:
            cp.start()
        yield
        for k in peers:
            gather_copy(k, _linear(_flip(me, k))).wait_recv()

        sc_all = gath_ref[:, 0, :]
        scb = jnp.concatenate([sc_all, jnp.zeros_like(sc_all)], axis=0).astype(BF16)
        prod = _dot(scb, adaw_ref[...].astype(BF16))
        for b in range(N_DEV):
            sendbuf[b] = jnp.broadcast_to(prod[b:b + 1, :], (HALO, cols))
        mod_ref[me_lin] = sendbuf[me_lin]

        def row_copy(k, dst_lin):
            peer = _flip(me, k)
            return pltpu.make_async_remote_copy(
                src_ref=sendbuf.at[_linear(peer)], dst_ref=mod_ref.at[dst_lin], send_sem=send_b.at[k - 1],
                recv_sem=recv_b.at[k - 1], device_id=peer, device_id_type=MESH)

        second_round = [row_copy(k, me_lin) for k in peers]
        for cp in second_round:
            cp.start()
        yield
        for k in peers:
            row_copy(k, _linear(_flip(me, k))).wait_recv()
        for cp in first_round + second_round:
            cp.wait_send()

    return steps


def _gather_and_mod(msg, ada_w, block):
    rows, cols = block.shape
    mod_cols = ada_w.shape[1]
    mod_steps = _mod_exchange_steps(mod_cols)

    def body(msg_ref, adaw_ref, x_ref, gath_ref, mod_ref, out_ref, sendbuf, send_a, recv_a, send_b, recv_b,
             send_sems, recv_sems, sib_send_sems, sib_recv_sems, local_sem):
        x, y, c = _position()
        me, sibling = (x, y, c), (x, y, 1 - c)
        sends, forward, arrivals = _chip_gather_copies(x_ref, out_ref, send_sems, recv_sems)

        def to_sibling(j, block_of, src=None):
            dst = out_ref.at[_linear(block_of)]
            return pltpu.make_async_remote_copy(
                src_ref=dst if src is None else src, dst_ref=dst, send_sem=sib_send_sems.at[j],
                recv_sem=sib_recv_sems.at[j], device_id=sibling, device_id_type=MESH)

        mine = pltpu.make_async_copy(x_ref, out_ref.at[_linear(me)], local_sem)
        mine.start()
        passes = [to_sibling(0, me, src=x_ref)] + [to_sibling(1 + j, p) for j, p in enumerate(_route_peers(me))]
        small = mod_steps(msg_ref, adaw_ref, gath_ref, mod_ref, sendbuf, send_a, recv_a, send_b, recv_b)
        next(small)
        passes[0].start()
        for cp in sends:
            cp.start()
        next(small)
        arrivals[0].wait_recv()
        forward.start()
        passes[1].start()
        arrivals[1].wait_recv()
        passes[2].start()
        arrivals[2].wait_recv()
        passes[3].start()
        for _ in small:
            pass
        for j, p in enumerate((sibling,) + _route_peers(sibling)):
            to_sibling(j, p).wait_recv()
        for cp in sends + [forward] + passes:
            cp.wait_send()
        mine.wait()

    return pl.pallas_call(
        body,
        name="gather_and_mod",
        in_specs=[WHOLE, WHOLE, ANY],
        out_specs=[WHOLE, WHOLE, ANY],
        out_shape=[jax.ShapeDtypeStruct((N_DEV, HALO, D), F32), jax.ShapeDtypeStruct((N_DEV, HALO, mod_cols), F32),
                   jax.ShapeDtypeStruct((N_DEV, rows, cols), block.dtype)],
        scratch_shapes=[pltpu.VMEM((N_DEV, HALO, mod_cols), F32)] + [pltpu.SemaphoreType.DMA((N_DEV - 1,))] * 4
        + [pltpu.SemaphoreType.DMA((3,)), pltpu.SemaphoreType.DMA((3,)), pltpu.SemaphoreType.DMA((4,)),
           pltpu.SemaphoreType.DMA((4,)), pltpu.SemaphoreType.DMA],
        compiler_params=pltpu.CompilerParams(vmem_limit_bytes=VMEM_LIMIT),
    )(msg, ada_w, block)


HBM = pl.BlockSpec(memory_space=pltpu.HBM)
SEM = pl.BlockSpec(memory_space=pltpu.SEMAPHORE)
EFFECT = pltpu.SideEffectType.DATAFLOW_SIDE_EFFECTING


def _stage_copies(stage):
    return {"chips": (_chip_scatter_copies, len(CHIP_FLIPS), len(CHIP_FLIPS)), "sibling": (_sibling_copies, 4, 4)}[stage]


def _chips_start(which, chip_sums, stage="chips"):
    n = len(chip_sums)
    make_copies, per_array, slots = _stage_copies(stage)
    n_sems = per_array * n

    def body(*refs):
        srcs, dsts = refs[:n], refs[n:2 * n]
        send_sems, recv_sems = refs[2 * n:2 * n + 2]
        token = refs[-1]
        for cp in make_copies(srcs, dsts, send_sems, recv_sems):
            cp.start()
        token[...] = jnp.zeros(token.shape, token.dtype)

    landing = [jax.ShapeDtypeStruct((slots,) + s.shape[-2:], s.dtype) for s in chip_sums]
    outs = pl.pallas_call(
        body,
        name=which + "_" + stage + "_start",
        in_specs=[HBM] * (2 * n),
        out_specs=[SEM, SEM] + [HBM] * (2 * n) + [WHOLE],
        out_shape=[pltpu.SemaphoreType.DMA((n_sems,)), pltpu.SemaphoreType.DMA((n_sems,))]
        + [pltpu.HBM(s.shape, s.dtype) for s in chip_sums] + [pltpu.HBM(s.shape, s.dtype) for s in landing]
        + [jax.ShapeDtypeStruct((HALO, 128), F32)],
        input_output_aliases={i: 2 + i for i in range(2 * n)},
        compiler_params=pltpu.CompilerParams(has_side_effects=EFFECT),
    )(*[pltpu.with_memory_space_constraint(s, pltpu.HBM) for s in chip_sums],
      *[pltpu.with_memory_space_constraint(lax.empty(s.shape, s.dtype), pltpu.HBM) for s in landing])
    return outs[0], outs[1], outs[2:2 + n], outs[2 + n:2 + 2 * n], outs[-1]


def _chips_wait(which, send_sems, recv_sems, srcs, landed, after, stage="chips"):
    n = len(srcs)
    make_copies = _stage_copies(stage)[0]

    def body(*refs):
        src_refs, dst_refs = refs[:n], refs[n:2 * n]
        sends, recvs = refs[2 * n:2 * n + 2]
        copies = make_copies(src_refs, dst_refs, sends, recvs)
        for cp in copies:
            cp.wait_send()
        for cp in copies:
            cp.wait_recv()

    outs = pl.pallas_call(
        body,
        name=which + "_" + stage + "_wait",
        in_specs=[HBM] * (2 * n) + [SEM, SEM, ANY],
        out_specs=[HBM] * (2 * n),
        out_shape=[pltpu.HBM(s.shape, s.dtype) for s in list(srcs) + list(landed)],
        input_output_aliases={i: i for i in range(2 * n)},
        compiler_params=pltpu.CompilerParams(has_side_effects=EFFECT),
    )(*srcs, *landed, send_sems, recv_sems, after)
    return list(outs[:n]), list(outs[n:])


def _sibling_copies(srcs, dsts, send_sems, recv_sems):
    x, y, c = _position()
    copies = []
    for a, (src, dst) in enumerate(zip(srcs, dsts)):
        for k in range(4):
            copies.append(pltpu.make_async_remote_copy(
                src_ref=src.at[k, 1 - c] if len(src.shape) == 4 else src.at[k], dst_ref=dst.at[k],
                send_sem=send_sems.at[4 * a + k],
                recv_sem=recv_sems.at[4 * a + k], device_id=(x, y, 1 - c), device_id_type=MESH))
    return copies


def _row_block(rows):
    return min(rows, 512)


def _pair_sum(pos, mine, recv):
    _, cores, rows, cols = mine.shape
    rb = _row_block(rows)

    def body(pos_ref, mine_ref, recv_ref, out_ref):
        out_ref[0] = (mine_ref[0, 0] + recv_ref[0]).astype(BF16)

    other = lambda k, pos: jnp.bitwise_xor(pos[1], k + 1)
    core = lambda pos: pos[0] * (cores - 1)
    return pl.pallas_call(
        body,
        name="grad_pair_sum",
        grid_spec=pltpu.PrefetchScalarGridSpec(
            num_scalar_prefetch=1, grid=(3, rows // rb),
            in_specs=[pl.BlockSpec((1, 1, rb, cols), lambda k, r, pos: (other(k, pos), core(pos), r, 0)),
                      pl.BlockSpec((1, rb, cols), lambda k, r, pos: (other(k, pos), r, 0))],
            out_specs=pl.BlockSpec((1, rb, cols), lambda k, r, pos: (other(k, pos), r, 0))),
        out_shape=jax.ShapeDtypeStruct((4, rows, cols), BF16),
        compiler_params=pltpu.CompilerParams(dimension_semantics=("arbitrary", "arbitrary")),
    )(pos, mine, recv)


def _final_sum(pos, mine, recv, chips):
    _, cores, rows, cols = mine.shape
    rb = _row_block(rows)

    def body(pos_ref, mine_ref, recv_ref, chips_ref, out_ref):
        g = mine_ref[0, 0] + recv_ref[0]
        for j in range(3):
            g = g + chips_ref[j].astype(F32)
        out_ref[...] = g

    return pl.pallas_call(
        body,
        name="grad_final_sum",
        grid_spec=pltpu.PrefetchScalarGridSpec(
            num_scalar_prefetch=1, grid=(rows // rb,),
            in_specs=[pl.BlockSpec((1, 1, rb, cols), lambda r, pos: (pos[1], pos[0] * (cores - 1), r, 0)),
                      pl.BlockSpec((1, rb, cols), lambda r, pos: (pos[1], r, 0)),
                      pl.BlockSpec((3, rb, cols), lambda r, pos: (0, r, 0))],
            out_specs=pl.BlockSpec((rb, cols), lambda r, pos: (r, 0))),
        out_shape=jax.ShapeDtypeStruct((rows, cols), F32),
        compiler_params=pltpu.CompilerParams(dimension_semantics=("arbitrary",)),
    )(pos, mine, recv, chips)


LOSS_ROW = V_ROWS + 8
GB_BASE = LOSS_ROW + 8


def _route_mod_grad_steps(cols):
    def steps(gmod_ref, sct_ref, gadaw_ref, sendbuf, grecv, send_a, recv_a):
        me = _position()
        me_lin = _linear(me)
        gm = gmod_ref[...]
        for b in range(N_DEV):
            sendbuf[b] = jnp.broadcast_to(gm[b:b + 1, :], (HALO, cols))
        grecv[me_lin] = sendbuf[me_lin]

        def row_copy(k, dst_lin):
            peer = _flip(me, k)
            return pltpu.make_async_remote_copy(
                src_ref=sendbuf.at[_linear(peer)], dst_ref=grecv.at[dst_lin], send_sem=send_a.at[k - 1],
                recv_sem=recv_a.at[k - 1], device_id=peer, device_id_type=MESH)

        _exchange_all(lambda k: row_copy(k, me_lin), lambda k: row_copy(k, _linear(_flip(me, k))))
        g_all = grecv[:, 0, :]
        g_pad = jnp.concatenate([g_all, jnp.zeros((sct_ref.shape[1] - N_DEV, cols), F32)], axis=0).astype(BF16)
        gadaw_ref[...] = _dot(sct_ref[...], g_pad)
        return _colsum(g_all)

    return steps


def _small_grad_exchange(gmod8, sc_t, msg_vec, msg_gate, after):
    cols = gmod8.shape[1]
    vec_rows = GB_BASE + N_DEV
    route_steps = _route_mod_grad_steps(cols)

    def body(gmod_ref, sct_ref, vec_ref, gate_ref, after_ref, gadaw_ref, sumv_ref, sumg_ref,
             sendbuf, grecv, send_a, recv_a, myv, myg, sibv, sibg, chipv, chipg, sib_send, sib_recv, peer_send, peer_recv):
        x, y, c = me = _position()
        my_chip = 2 * x + y

        def swap(a, src, dst):
            return pltpu.make_async_remote_copy(
                src_ref=src, dst_ref=dst, send_sem=sib_send.at[a], recv_sem=sib_recv.at[a], device_id=(x, y, 1 - c),
                device_id_type=MESH)

        def chip_copy(a, buf, j, k, slot_chip):
            peer = _flip(me, k)
            return pltpu.make_async_remote_copy(
                src_ref=buf.at[slot_chip], dst_ref=buf.at[slot_chip], send_sem=peer_send.at[3 * a + j],
                recv_sem=peer_recv.at[3 * a + j], device_id=peer, device_id_type=MESH)

        def chip_stage(a, mine, theirs, buf):
            buf[my_chip] = (mine[...] + theirs[...]).astype(buf.dtype)
            sends = [chip_copy(a, buf, j, k, my_chip) for j, k in enumerate(CHIP_FLIPS)]
            for cp in sends:
                cp.start()
            return sends

        myg[...] = gate_ref[...]
        swap_g = swap(1, myg, sibg)
        swap_g.start()
        gb = route_steps(gmod_ref, sct_ref, gadaw_ref, sendbuf, grecv, send_a, recv_a)
        swap_g.wait_recv()
        sends = chip_stage(1, myg, sibg, chipg)

        myv[0:GB_BASE, :] = vec_ref[...]
        slot = lax.broadcasted_iota(jnp.int32, (N_DEV, D), 0) == _linear(me)
        gb_wide = jnp.concatenate([jnp.broadcast_to(gb, (N_DEV, cols)), jnp.zeros((N_DEV, D - cols), F32)], axis=1)
        myv[GB_BASE:vec_rows, :] = jnp.where(slot, gb_wide, 0.0)
        swap_v = swap(0, myv, sibv)
        swap_v.start()
        swap_v.wait_recv()
        sends += chip_stage(0, myv, sibv, chipv)

        for a, buf in enumerate((chipv, chipg)):
            for j, k in enumerate(CHIP_FLIPS):
                peer = _flip(me, k)
                chip_copy(a, buf, j, k, 2 * peer[0] + peer[1]).wait_recv()
        sumv_ref[...] = ((chipv[0] + chipv[1]) + chipv[2]) + chipv[3]
        gate_sum = lambda k: chipg[k].astype(F32)
        sumg_ref[...] = ((gate_sum(0) + gate_sum(1)) + gate_sum(2)) + gate_sum(3)
        for cp in [swap_g, swap_v] + sends:
            cp.wait_send()

    vshape, gshape = (vec_rows, D), msg_gate.shape
    return pl.pallas_call(
        body,
        name="small_grad_exchange",
        in_specs=[WHOLE] * 5,
        out_specs=[WHOLE] * 3,
        out_shape=[jax.ShapeDtypeStruct((D, cols), F32), jax.ShapeDtypeStruct(vshape, F32),
                   jax.ShapeDtypeStruct(gshape, F32)],
        scratch_shapes=[pltpu.VMEM((N_DEV, HALO, cols), F32), pltpu.VMEM((N_DEV, HALO, cols), F32),
                        pltpu.SemaphoreType.DMA((N_DEV - 1,)), pltpu.SemaphoreType.DMA((N_DEV - 1,)),
                        pltpu.VMEM(vshape, F32), pltpu.VMEM(gshape, F32), pltpu.VMEM(vshape, F32),
                        pltpu.VMEM(gshape, F32), pltpu.VMEM((4,) + vshape, F32), pltpu.VMEM((4,) + gshape, BF16),
                        pltpu.SemaphoreType.DMA((2,)), pltpu.SemaphoreType.DMA((2,)),
                        pltpu.SemaphoreType.DMA((2 * len(CHIP_FLIPS),)), pltpu.SemaphoreType.DMA((2 * len(CHIP_FLIPS),))],
        compiler_params=pltpu.CompilerParams(vmem_limit_bytes=VMEM_LIMIT),
    )(gmod8, sc_t, msg_vec, msg_gate, after)


def _adamw_math(w, g, m, v):
    m = ADAM_B1 * m + (1.0 - ADAM_B1) * g
    v = ADAM_B2 * v + (1.0 - ADAM_B2) * (g * g)
    m_hat = m / (1.0 - ADAM_B1 ** ADAM_STEP)
    v_hat = v / (1.0 - ADAM_B2 ** ADAM_STEP)
    delta = -ADAM_LR * (m_hat / (jnp.sqrt(v_hat) + ADAM_EPS) + ADAM_WD * w)
    return delta, m, v


def _adamw(name, w, g, m, v):
    rows, cols = w.shape
    rb = 256 if rows % 256 == 0 else rows

    def body(w_ref, g_ref, m_ref, v_ref, d_ref, mo_ref, vo_ref):
        d_ref[...], mo_ref[...], vo_ref[...] = _adamw_math(w_ref[...], g_ref[...], m_ref[...], v_ref[...])

    spec = pl.BlockSpec((rb, cols), lambda r: (r, 0))
    return pl.pallas_call(
        body,
        name="adamw_" + name,
        grid=(rows // rb,),
        in_specs=[spec] * 4,
        out_specs=[spec] * 3,
        out_shape=[jax.ShapeDtypeStruct((rows, cols), F32)] * 3,
        compiler_params=pltpu.CompilerParams(dimension_semantics=("arbitrary",)),
    )(w, g, m, v)


def _update(pos, sum_jobs, plain_jobs, after):
    rb = 256
    jobs = [("sum", j) for j in sum_jobs] + [("plain", j) for j in plain_jobs]
    offs, total = [], 0
    for _, j in jobs:
        offs.append(total)
        total += j[-1].shape[0] // rb
    n_in = sum(len(j) for _, j in jobs)

    def body(pos_ref, *refs):
        ins, outs = refs[:n_in], refs[n_in + 1:]
        s = pl.program_id(0)
        i_in = i_out = 0
        for (kind, j), off in zip(jobs, offs):
            steps = j[-1].shape[0] // rb
            j_in = ins[i_in:i_in + len(j)]
            i_in += len(j)
            j_out = outs[i_out:i_out + (4 if kind == "sum" else 3)]
            i_out += len(j_out)

            @pl.when((s >= off) & (s < off + steps))
            def _(kind=kind, j_in=j_in, j_out=j_out):
                if kind == "sum":
                    mine_ref, recv_ref, chips_ref, w_ref, m_ref, v_ref = j_in
                    g = mine_ref[0, 0] + recv_ref[0]
                    for q in range(len(CHIP_FLIPS)):
                        g = g + chips_ref[q].astype(F32)
                    j_out[0][...] = g
                    rest = j_out[1:]
                else:
                    g_ref, w_ref, m_ref, v_ref = j_in
                    g = g_ref[...]
                    rest = j_out
                rest[0][...], rest[1][...], rest[2][...] = _adamw_math(w_ref[...], g, m_ref[...], v_ref[...])

    in_specs, out_specs, out_shape, args = [], [], [], []
    for (kind, j), off in zip(jobs, offs):
        rows, cols = j[-1].shape
        steps = rows // rb
        blk = lambda s, off=off, steps=steps: jnp.clip(s - off, 0, steps - 1)
        flat = pl.BlockSpec((rb, cols), lambda s, pos, blk=blk: (blk(s), 0))
        if kind == "sum":
            in_specs += [pl.BlockSpec((1, 1, rb, cols), lambda s, pos, blk=blk: (pos[1], 0, blk(s), 0)),
                         pl.BlockSpec((1, rb, cols), lambda s, pos, blk=blk: (pos[1], blk(s), 0)),
                         pl.BlockSpec((len(CHIP_FLIPS), rb, cols), lambda s, pos, blk=blk: (0, blk(s), 0))]
            in_specs += [flat] * 3
        else:
            in_specs += [flat] * 4
        n_res = 4 if kind == "sum" else 3
        out_specs += [flat] * n_res
        out_shape += [jax.ShapeDtypeStruct((rows, cols), F32)] * n_res
        args += list(j)
    in_specs += [pl.BlockSpec(after.shape, lambda s, pos: (0,) * after.ndim)]
    outs = pl.pallas_call(
        body,
        name="update",
        grid_spec=pltpu.PrefetchScalarGridSpec(
            num_scalar_prefetch=1, grid=(total,), in_specs=in_specs, out_specs=out_specs),
        out_shape=out_shape,
        compiler_params=pltpu.CompilerParams(dimension_semantics=("arbitrary",), vmem_limit_bytes=VMEM_LIMIT),
    )(pos, *args, after)
    sums = [tuple(outs[4 * i:4 * i + 4]) for i in range(len(sum_jobs))]
    base = 4 * len(sum_jobs)
    plains = [tuple(outs[base + 3 * i:base + 3 * i + 3]) for i in range(len(plain_jobs))]
    return sums, plains


def _adamw_small(ws, gs, ms, vs, sigmoid_scaled):
    n = len(ws)

    def body(*refs):
        w_refs, g_refs, m_refs, v_refs = (refs[i * n:(i + 1) * n] for i in range(4))
        outs = refs[4 * n:]
        for i in range(n):
            w = w_refs[i][...]
            g = g_refs[i][...]
            if sigmoid_scaled[i]:
                g = g * _sigmoid(w)
            delta, m, v = _adamw_math(w, g, m_refs[i][...], v_refs[i][...])
            outs[4 * i][...] = g
            outs[4 * i + 1][...] = delta
            outs[4 * i + 2][...] = m
            outs[4 * i + 3][...] = v

    shapes = [jax.ShapeDtypeStruct(w.shape, F32) for w in ws for _ in range(4)]
    outs = pl.pallas_call(
        body,
        name="adamw_small",
        in_specs=[WHOLE] * (4 * n),
        out_specs=[WHOLE] * (4 * n),
        out_shape=shapes,
    )(*ws, *gs, *ms, *vs)
    return [outs[4 * i:4 * i + 4] for i in range(n)]


_WEIGHT_NAMES = ("ada_w", "ada_b", "norm1_g", "w_in", "lru_conv_w", "lru_conv_b", "gate_a_w", "gate_a_b", "gate_x_w",
                 "gate_x_b", "a_param", "short_conv_w", "lru_out_g", "conv_out_g", "w_out", "norm2_g", "w_mlp1",
                 "w_mlp2", "final_g")


def kernel(x, c, ada_w, ada_b, norm1_g, w_in, lru_conv_w, lru_conv_b, gate_a_w, gate_a_b, gate_x_w, gate_x_b, a_param, short_conv_w, lru_out_g, conv_out_g, w_out, norm2_g, w_mlp1, w_mlp2, final_g, loss_target, m_ada_w, m_ada_b, m_norm1_g, m_w_in, m_lru_conv_w, m_lru_conv_b, m_gate_a_w, m_gate_a_b, m_gate_x_w, m_gate_x_b, m_a_param, m_short_conv_w, m_lru_out_g, m_conv_out_g, m_w_out, m_norm2_g, m_w_mlp1, m_w_mlp2, m_final_g, v_ada_w, v_ada_b, v_norm1_g, v_w_in, v_lru_conv_w, v_lru_conv_b, v_gate_a_w, v_gate_a_b, v_gate_x_w, v_gate_x_b, v_a_param, v_short_conv_w, v_lru_out_g, v_conv_out_g, v_w_out, v_norm2_g, v_w_mlp1, v_w_mlp2, v_final_g):
    given = dict(locals())
    weights = {n: given[n] for n in _WEIGHT_NAMES}
    xi, yi, ci = _position()
    me_lin = _linear((xi, yi, ci))
    hd = W // N_DEV

    mixer_block = jnp.concatenate([w_out[0], w_in[0].T], axis=0).astype(BF16)
    mlp_block = jnp.concatenate([w_mlp1[0].T, w_mlp2[0]], axis=0).astype(BF16)

    msg = (jnp.pad(c, ((0, HALO - 1), (0, 0)))
           + jnp.pad(lru_conv_w[0], ((1, HALO - 1 - CONV_L), (0, D - hd)))
           + jnp.pad(short_conv_w[0], ((1 + CONV_L, 0), (0, D - hd))))
    gath, mod_all, wmix = _gather_and_mod(msg, ada_w[0], mixer_block)
    sc_all = gath[:, 0, :]
    wl = jnp.transpose(gath[:, 1:1 + CONV_L, :hd], (1, 0, 2)).reshape(CONV_L, W)
    ws = jnp.transpose(gath[:, 1 + CONV_L:HALO, :hd], (1, 0, 2)).reshape(CONV_S, W)
    modraw = _pad_rows(mod_all[:, 0, :].reshape(6, D), HALO)
    adab = _pad_rows(ada_b.reshape(6, D), HALO)

    x2d, tgt = x[0], loss_target[0]
    gf = final_g.reshape(1, D)
    bda = _block_diag(gate_a_w[0].astype(BF16))
    bdx = _block_diag(gate_x_w[0].astype(BF16))
    avg = jnp.asarray(np.kron(np.eye(8, dtype=np.float32), np.full((W // 8, W // 8), 8.0 / W, np.float32)), BF16)
    wl8 = _pad_rows(wl, HALO)
    ws8 = _pad_rows(ws, HALO)
    mixer_small = (wl8, lru_conv_b, bda, bdx, gate_a_b, gate_x_b, a_param, ws8, lru_out_g, conv_out_g, avg)
    proj, hl, mixed, kept, wmlp = _mixer_fwd(x2d, modraw, adab, norm1_g, *mixer_small, wmix, mlp_block)
    h2t, f, dx2, dz, vec2, loss8 = _mlp_fwd(x2d, mixed, tgt, modraw, adab, norm2_g, gf, wmlp)
    pos = jnp.stack([ci, 2 * xi + yi]).astype(jnp.int32)
    by_dest = lambda g: g.reshape((4, 2, -1) + g.shape[-1:])
    dh2_first, *for_sibling = _mlp_bwd_half(pos, h2t, f, dz, wmlp)
    sib_send, sib_recv, sib_thru, sib_land, token = _chips_start("mlp", for_sibling, stage="sibling")
    dh2, dw1, dw2 = _mlp_bwd_half(pos, h2t, f, dz, wmlp, prior=(dh2_first, token))
    done = dh2[0:HALO, 0:128] + dw1[0, 0:HALO, 0:128] + dw2[0, 0:HALO, 0:128]
    _, mlp_sib = _chips_wait("mlp", sib_send, sib_recv, sib_thru, sib_land, done, stage="sibling")
    mlp_parts = [dw1[:, None], dw2[:, None]]
    mlp_sums = [_pair_sum(pos, p, r) for p, r in zip(mlp_parts, mlp_sib)]
    mlp_send, mlp_recv, mlp_thru, mlp_land, token = _chips_start("mlp", mlp_sums)
    modraw_after = modraw + jnp.tile(token, (1, D // token.shape[1]))
    gx, vec, hb, dproj_t, dmixed, ycat_t, xl_t, dgate = _mixer_bwd(
        x2d, mixed, dh2, dx2, proj, hl, kept, modraw_after, adab, norm1_g, norm2_g, *mixer_small, wmix)
    dwint = _matmul("wgrad_in", dproj_t, hb)
    dwout = _matmul("wgrad_out", ycat_t, dmixed)
    gate_blocks = _gate_wgrad(xl_t, dgate, avg)
    msg_gate = gate_blocks.reshape(W, 128)
    done = dwint[0:HALO, 0:128] + dwout[0:HALO, 0:128] + gate_blocks[0, 0:HALO, :].sum() + gx[0:HALO, 0:128]
    _, mlp_chips = _chips_wait("mlp", mlp_send, mlp_recv, mlp_thru, mlp_land, done)
    mix_parts = [by_dest(dwout), by_dest(dwint)]
    gmod8 = (jnp.pad(vec[0:5], ((0, 1), (0, 0))) + jnp.pad(vec2[0:1], ((5, 0), (0, 0)))).reshape(N_DEV, 6 * D // N_DEV)
    sc_t = jnp.pad(sc_all.T, ((0, 0), (0, 128 - N_DEV))).astype(BF16)
    loss_rows = jnp.pad(loss8[0:1], ((0, HALO - 1), (0, D - loss8.shape[1])))
    msg_vec = jnp.concatenate([vec, vec2, loss_rows], axis=0)
    sib_send, sib_recv, sib_thru, sib_land, token = _chips_start("mixer", mix_parts, stage="sibling")
    g_adaw, sum_vec, sum_gate = _small_grad_exchange(gmod8, sc_t, msg_vec, msg_gate, token)
    mix_parts, mix_sib = _chips_wait("mixer", sib_send, sib_recv, sib_thru, sib_land, sum_vec[0:HALO, 0:128],
                                     stage="sibling")
    mix_sums = [_pair_sum(pos, p, r) for p, r in zip(mix_parts, mix_sib)]
    state = lambda n: (weights[n][0], given["m_" + n][0], given["v_" + n][0])
    mlp_jobs = [(p, r, q, *state(n)) for p, r, q, n in zip(mlp_parts, mlp_sib, mlp_chips, ("w_mlp1", "w_mlp2"))]
    mix_send, mix_recv, mix_thru, mix_land, token = _chips_start("mixer", mix_sums)
    mlp_done, (adaw_done,) = _update(pos, mlp_jobs, [(g_adaw, *state("ada_w"))], token)
    loss = sum_vec[LOSS_ROW, 0]
    sum_gate = sum_gate.reshape(2, W, W // 8)
    lo, hi = slice(0, W), slice(W, 2 * W)
    wl_full = sum_vec[V_WL01:V_WL23 + 1].reshape(CONV_L, W)
    ws_full = sum_vec[V_WS01:V_WS2 + 1].reshape(CONV_S + 1, W)[:CONV_S]
    row = lambda r, cols: sum_vec[r:r + 1, cols]
    small_grads = {
        "ada_b": sum_vec[GB_BASE:GB_BASE + N_DEV, :6 * D // N_DEV].reshape(1, 6 * D),
        "norm1_g": row(V_G1, slice(0, D)),
        "lru_conv_w": lax.dynamic_slice(wl_full, (0, me_lin * hd), (CONV_L, hd)),
        "lru_conv_b": row(V_BL_BA, lo),
        "gate_a_w": sum_gate[0],
        "gate_a_b": row(V_BL_BA, hi),
        "gate_x_w": sum_gate[1],
        "gate_x_b": row(V_BX_SP, lo),
        "a_param": row(V_BX_SP, hi),
        "short_conv_w": lax.dynamic_slice(ws_full, (0, me_lin * hd), (CONV_S, hd)),
        "lru_out_g": row(V_GL_GC, lo),
        "conv_out_g": row(V_GL_GC, hi),
        "norm2_g": row(V_G2, slice(0, D)),
        "final_g": sum_vec[V_ROWS + 1:V_ROWS + 2, :],
    }
    names = list(small_grads)
    as2d = lambda a, n: a.reshape(small_grads[n].shape)
    small = _adamw_small([as2d(weights[n], n) for n in names], [small_grads[n] for n in names],
                         [as2d(given["m_" + n], n) for n in names], [as2d(given["v_" + n], n) for n in names],
                         [n == "a_param" for n in names])
    result = {n: tuple(o.reshape(weights[n].shape) for o in outs) for n, outs in zip(names, small)}

    done = (small[0][1][:, 0:128] + mlp_done[0][2][0:HALO, 0:128] + mlp_done[1][2][0:HALO, 0:128]
            + adaw_done[1][0:HALO, 0:128])
    _, mix_chips = _chips_wait("mixer", mix_send, mix_recv, mix_thru, mix_land, done)
    g_wout, g_wint = (_final_sum(pos, p, r, q) for p, r, q in zip(mix_parts, mix_sib, mix_chips))
    for n, g in (("w_in", g_wint.T), ("w_out", g_wout)):
        w, m, v = state(n)
        result[n] = (g[None],) + tuple(o[None] for o in _adamw(n, w, g, m, v))
    result["w_mlp1"], result["w_mlp2"] = (tuple(o[None] for o in done) for done in mlp_done)
    result["ada_w"] = (g_adaw[None],) + tuple(o[None] for o in adaw_done)

    return (loss, gx[None], *[result[n][0] for n in _WEIGHT_NAMES], *[result[n][1] for n in _WEIGHT_NAMES],
            *[result[n][2] for n in _WEIGHT_NAMES], *[result[n][3] for n in _WEIGHT_NAMES])
```
